```python
import jax, jax.numpy as jnp
from jax import lax
import numpy as np

D_MODEL = 2048
BATCH = 8
SEQ = 4096
DEPTH = 1

HEAD_DIM = 128
N_HEADS = D_MODEL // HEAD_DIM
N_HEADS_B = N_HEADS // 4
N_HEADS_A = N_HEADS - N_HEADS_B
DILATION_PATTERNS = ((128, 1), (512, 4), (2048, 16))
HEADS_PER_GROUP = N_HEADS_A // len(DILATION_PATTERNS)
WIDTH_A = HEADS_PER_GROUP * HEAD_DIM
WIDTH_B = N_HEADS_B * HEAD_DIM
GRID_W = 64
NA_ROWS = 8
NA_COLS = 16
D_FF = 4 * D_MODEL
N_BRANCHES = 2
BAND_BLOCK = 64
EPS = 1e-6
NEG = -1e30

kernel_name = "hybrid_dilated_neighbourhood_gated_encoder"


def _rmsnorm(x, g):
    xf = x.astype(jnp.float32)
    y = xf * lax.rsqrt(jnp.mean(xf * xf, axis=-1, keepdims=True) + EPS)
    return (y * g.astype(jnp.float32)).astype(x.dtype)


def _alibi_slopes(n):
    return jnp.asarray(2.0 ** (-8.0 * np.arange(1, n + 1) / n), dtype=jnp.float32)


def _banded_attention(q, k, v, slopes, half_window, stride):
    B, H, N, L, hd = q.shape
    Qb = BAND_BLOCK
    W = half_window
    nb = -(-L // Qb)
    Lp = nb * Qb
    Kb = Qb + 2 * W
    qp = jnp.pad(q, ((0, 0), (0, 0), (0, 0), (0, Lp - L), (0, 0)))
    kp = jnp.pad(k, ((0, 0), (0, 0), (0, 0), (W, Lp - L + W), (0, 0)))
    vp = jnp.pad(v, ((0, 0), (0, 0), (0, 0), (W, Lp - L + W), (0, 0)))
    key_idx = np.arange(nb)[:, None] * Qb + np.arange(Kb)[None, :]
    kb = jnp.take(kp, key_idx, axis=3)
    vb = jnp.take(vp, key_idx, axis=3)
    qb = qp.reshape(B, H, N, nb, Qb, hd)
    s = jnp.einsum('bhniqd,bhnikd->bhniqk', qb, kb).astype(jnp.float32) * (HEAD_DIM ** -0.5)
    qpos = np.arange(Lp).reshape(nb, Qb)
    kpos = key_idx - W
    rel = np.abs(kpos[:, None, :] - qpos[:, :, None])
    valid = (rel <= W) & (kpos >= 0)[:, None, :] & (kpos < L)[:, None, :]
    bias = -(slopes * stride)[:, None, None, None, None] * rel.astype(np.float32)[None, None]
    s = jnp.where(valid, s + bias, NEG)
    m = jnp.max(s, axis=-1, keepdims=True)
    p = jnp.exp(s - m)
    den = jnp.sum(p, axis=-1, keepdims=True)
    o = jnp.einsum('bhniqk,bhnikd->bhniqd', (p / den).astype(v.dtype), vb)
    lse = (m + jnp.log(den))[..., 0]
    o = o.reshape(B, H, N, Lp, hd)[:, :, :, :L]
    lse = lse.reshape(B, H, N, Lp)[:, :, :, :L]
    return o, lse


def _dilated_mixer(q, k, v):
    B, S = q.shape[0], q.shape[1]
    slopes = _alibi_slopes(N_HEADS_A)
    outs, lses = [], []
    for g, (window, d) in enumerate(DILATION_PATTERNS):
        hs = slice(g * HEADS_PER_GROUP, (g + 1) * HEADS_PER_GROUP)

        def to_residue(t):
            return t[:, :, hs].reshape(B, S // d, d, HEADS_PER_GROUP, HEAD_DIM).transpose(0, 3, 2, 1, 4)

        o, lse = _banded_attention(to_residue(q), to_residue(k), to_residue(v),
                                   slopes[hs], window // (2 * d), d)
        outs.append(o.transpose(0, 3, 2, 1, 4).reshape(B, S, HEADS_PER_GROUP, HEAD_DIM))
        lses.append(lse.transpose(0, 3, 2, 1).reshape(B, S, HEADS_PER_GROUP))
    o_all = jnp.stack(outs, axis=0).astype(jnp.float32)
    alpha = jax.nn.softmax(jnp.stack(lses, axis=0), axis=0)
    y = jnp.sum(alpha[..., None] * o_all, axis=0)
    return y.reshape(B, S, WIDTH_A).astype(q.dtype)


def _neighbourhood_mixer(q, k, v, rpb):
    B, S, H, hd = q.shape
    rows = S // GRID_W
    kh = min(NA_ROWS, rows)
    kw = NA_COLS

    def grid(t):
        return t.reshape(B, rows, GRID_W, H, hd).transpose(0, 3, 1, 2, 4)

    row_start = np.clip(np.arange(rows) - kh // 2, 0, rows - kh)
    row_idx = row_start[:, None] + np.arange(kh)[None, :]
    kg = jnp.take(grid(k), row_idx, axis=2).reshape(B, H, rows, kh * GRID_W, hd)
    vg = jnp.take(grid(v), row_idx, axis=2).reshape(B, H, rows, kh * GRID_W, hd)
    col = np.arange(GRID_W)
    col_start = np.clip(col - kw // 2, 0, GRID_W - kw)
    col_ok = (col[None, :] >= col_start[:, None]) & (col[None, :] < col_start[:, None] + kw)
    mask = np.tile(col_ok, (1, kh))
    dr = row_idx - np.arange(rows)[:, None]
    dc = np.clip(col[None, :] - col[:, None], -(kw - 1), kw - 1)
    bias = rpb.astype(jnp.float32)[:, dr + NA_ROWS - 1][..., dc + NA_COLS - 1]
    bias = bias.transpose(0, 1, 3, 2, 4).reshape(H, rows, GRID_W, kh * GRID_W)
    s = jnp.einsum('bhrqd,bhrkd->bhrqk', grid(q), kg).astype(jnp.float32) * (HEAD_DIM ** -0.5) + bias
    s = jnp.where(mask, s, NEG)
    p = jax.nn.softmax(s, axis=-1)
    o = jnp.einsum('bhrqk,bhrkd->bhrqd', p.astype(v.dtype), vg)
    return o.transpose(0, 2, 3, 1, 4).reshape(B, S, WIDTH_B)


def _mixer_block(h, w_qkv, w_gate, b_gate, rpb, w_proj_a, w_proj_b, w_out):
    B, S, _ = h.shape
    qkv = (h @ w_qkv).reshape(B, S, 3, N_HEADS, HEAD_DIM)
    q, k, v = qkv[:, :, 0], qkv[:, :, 1], qkv[:, :, 2]
    y_a = _dilated_mixer(q[:, :, :N_HEADS_A], k[:, :, :N_HEADS_A], v[:, :, :N_HEADS_A])
    y_b = _neighbourhood_mixer(q[:, :, N_HEADS_A:], k[:, :, N_HEADS_A:], v[:, :, N_HEADS_A:], rpb)
    gates = jax.nn.sigmoid(h @ w_gate + b_gate).reshape(B, S, N_BRANCHES, D_MODEL)
    merged = gates[:, :, 0] * (y_a @ w_proj_a) + gates[:, :, 1] * (y_b @ w_proj_b)
    return merged @ w_out


def _sqrelu_mlp(h, w_up, w_down):
    return jnp.square(jax.nn.relu(h @ w_up)) @ w_down


def _fwd_setup_inputs(seed: int = 0) -> dict:
    key = jax.random.key(seed)
    ks = jax.random.split(key, 14)
    f32 = jnp.float32

    def nrm(k, shape, scale):
        return jax.random.normal(k, shape, f32) * scale

    return {
        "x": nrm(ks[0], (BATCH, SEQ, D_MODEL), 1.0),
        "norm_mix": 1.0 + nrm(ks[1], (DEPTH, D_MODEL), 0.02),
        "w_qkv": nrm(ks[2], (DEPTH, D_MODEL, 3 * N_HEADS * HEAD_DIM), D_MODEL ** -0.5),
        "w_gate": nrm(ks[3], (DEPTH, D_MODEL, N_BRANCHES * D_MODEL), D_MODEL ** -0.5),
        "b_gate": nrm(ks[4], (DEPTH, N_BRANCHES * D_MODEL), 0.02),
        "rpb": nrm(ks[5], (DEPTH, N_HEADS_B, 2 * NA_ROWS - 1, 2 * NA_COLS - 1), 0.1),
        "w_proj_a": nrm(ks[6], (DEPTH, WIDTH_A, D_MODEL), WIDTH_A ** -0.5),
        "w_proj_b": nrm(ks[7], (DEPTH, WIDTH_B, D_MODEL), WIDTH_B ** -0.5),
        "w_out": nrm(ks[8], (DEPTH, D_MODEL, D_MODEL), D_MODEL ** -0.5),
        "norm_mlp": 1.0 + nrm(ks[9], (DEPTH, D_MODEL), 0.02),
        "w_up": nrm(ks[10], (DEPTH, D_MODEL, D_FF), D_MODEL ** -0.5),
        "w_down": nrm(ks[11], (DEPTH, D_FF, D_MODEL), D_FF ** -0.5),
        "norm_final": 1.0 + nrm(ks[12], (D_MODEL,), 0.02),
    }


def _fwd_reference(x, norm_mix, w_qkv, w_gate, b_gate, rpb, w_proj_a, w_proj_b, w_out,
              norm_mlp, w_up, w_down, norm_final):
    for l in range(DEPTH):
        h = _rmsnorm(x, norm_mix[l])
        x = x + _mixer_block(h, w_qkv[l], w_gate[l], b_gate[l], rpb[l],
                             w_proj_a[l], w_proj_b[l], w_out[l])
        h = _rmsnorm(x, norm_mlp[l])
        x = x + _sqrelu_mlp(h, w_up[l], w_down[l])
    return _rmsnorm(x, norm_final)


import jax as _jax
import jax.numpy as _jnp

TWIN_FORMAT = 'train_step'
FWD_PARAMS = ['x', 'norm_mix', 'w_qkv', 'w_gate', 'b_gate', 'rpb', 'w_proj_a', 'w_proj_b', 'w_out', 'norm_mlp', 'w_up', 'w_down', 'norm_final']
TWIN_WEIGHTS = ['norm_mix', 'w_qkv', 'w_gate', 'b_gate', 'rpb', 'w_proj_a', 'w_proj_b', 'w_out', 'norm_mlp', 'w_up', 'w_down', 'norm_final']
TWIN_DIFF_INPUT = 'x'
TWIN_INPUTS = ['x', 'norm_mix', 'w_qkv', 'w_gate', 'b_gate', 'rpb', 'w_proj_a', 'w_proj_b', 'w_out', 'norm_mlp', 'w_up', 'w_down', 'norm_final', 'loss_target', 'm_norm_mix', 'm_w_qkv', 'm_w_gate', 'm_b_gate', 'm_rpb', 'm_w_proj_a', 'm_w_proj_b', 'm_w_out', 'm_norm_mlp', 'm_w_up', 'm_w_down', 'm_norm_final', 'v_norm_mix', 'v_w_qkv', 'v_w_gate', 'v_b_gate', 'v_rpb', 'v_w_proj_a', 'v_w_proj_b', 'v_w_out', 'v_norm_mlp', 'v_w_up', 'v_w_down', 'v_norm_final']
TWIN_OUTPUTS = ['loss', 'grad_x', 'grad_norm_mix', 'grad_w_qkv', 'grad_w_gate', 'grad_b_gate', 'grad_rpb', 'grad_w_proj_a', 'grad_w_proj_b', 'grad_w_out', 'grad_norm_mlp', 'grad_w_up', 'grad_w_down', 'grad_norm_final', 'delta_norm_mix', 'delta_w_qkv', 'delta_w_gate', 'delta_b_gate', 'delta_rpb', 'delta_w_proj_a', 'delta_w_proj_b', 'delta_w_out', 'delta_norm_mlp', 'delta_w_up', 'delta_w_down', 'delta_norm_final', 'new_m_norm_mix', 'new_m_w_qkv', 'new_m_w_gate', 'new_m_b_gate', 'new_m_rpb', 'new_m_w_proj_a', 'new_m_w_proj_b', 'new_m_w_out', 'new_m_norm_mlp', 'new_m_w_up', 'new_m_w_down', 'new_m_norm_final', 'new_v_norm_mix', 'new_v_w_qkv', 'new_v_w_gate', 'new_v_b_gate', 'new_v_rpb', 'new_v_w_proj_a', 'new_v_w_proj_b', 'new_v_w_out', 'new_v_norm_mlp', 'new_v_w_up', 'new_v_w_down', 'new_v_norm_final']
TWIN_LEAF_KINDS = {'loss': 'loss', 'grad_x': 'grad_x', 'grad_norm_mix': 'grad_w', 'grad_w_qkv': 'grad_w', 'grad_w_gate': 'grad_w', 'grad_b_gate': 'grad_w', 'grad_rpb': 'grad_w', 'grad_w_proj_a': 'grad_w', 'grad_w_proj_b': 'grad_w', 'grad_w_out': 'grad_w', 'grad_norm_mlp': 'grad_w', 'grad_w_up': 'grad_w', 'grad_w_down': 'grad_w', 'grad_norm_final': 'grad_w', 'delta_norm_mix': 'delta_w', 'delta_w_qkv': 'delta_w', 'delta_w_gate': 'delta_w', 'delta_b_gate': 'delta_w', 'delta_rpb': 'delta_w', 'delta_w_proj_a': 'delta_w', 'delta_w_proj_b': 'delta_w', 'delta_w_out': 'delta_w', 'delta_norm_mlp': 'delta_w', 'delta_w_up': 'delta_w', 'delta_w_down': 'delta_w', 'delta_norm_final': 'delta_w', 'new_m_norm_mix': 'new_m', 'new_m_w_qkv': 'new_m', 'new_m_w_gate': 'new_m', 'new_m_b_gate': 'new_m', 'new_m_rpb': 'new_m', 'new_m_w_proj_a': 'new_m', 'new_m_w_proj_b': 'new_m', 'new_m_w_out': 'new_m', 'new_m_norm_mlp': 'new_m', 'new_m_w_up': 'new_m', 'new_m_w_down': 'new_m', 'new_m_norm_final': 'new_m', 'new_v_norm_mix': 'new_v', 'new_v_w_qkv': 'new_v', 'new_v_w_gate': 'new_v', 'new_v_b_gate': 'new_v', 'new_v_rpb': 'new_v', 'new_v_w_proj_a': 'new_v', 'new_v_w_proj_b': 'new_v', 'new_v_w_out': 'new_v', 'new_v_norm_mlp': 'new_v', 'new_v_w_up': 'new_v', 'new_v_w_down': 'new_v', 'new_v_norm_final': 'new_v'}


def _forward(args):
    return _fwd_reference(*[args[k] for k in FWD_PARAMS])


def _output_shape():
    def fwd():
        inp = _fwd_setup_inputs(0)
        return _fwd_reference(*[inp[k] for k in FWD_PARAMS])
    out = _jax.eval_shape(fwd)
    return out.shape, out.dtype

N_MICROBATCH = 1
ADAM_LR = 0.001
ADAM_B1 = 0.9
ADAM_B2 = 0.999
ADAM_EPS = 1e-08
ADAM_WD = 0.01
ADAM_STEP = 10
PER_EXAMPLE_BATCH_AXIS = {'x': 0, 'loss_target': 0}
SHARED_INPUTS = []
_WEIGHT_DTYPES = {'norm_mix': _jnp.float32, 'w_qkv': _jnp.float32, 'w_gate': _jnp.float32, 'b_gate': _jnp.float32, 'rpb': _jnp.float32, 'w_proj_a': _jnp.float32, 'w_proj_b': _jnp.float32, 'w_out': _jnp.float32, 'norm_mlp': _jnp.float32, 'w_up': _jnp.float32, 'w_down': _jnp.float32, 'norm_final': _jnp.float32}
MOMENT_SCALE = {'norm_mix': 2.801479e-02, 'w_qkv': 1.529443e-02, 'w_gate': 4.481742e-03, 'b_gate': 4.526473e-03, 'rpb': 1.055970e-02, 'w_proj_a': 1.251792e-02, 'w_proj_b': 1.028709e-02, 'w_out': 1.621651e-02, 'norm_mlp': 8.138401e-02, 'w_up': 3.952357e-02, 'w_down': 8.139095e-02, 'norm_final': 1.615579e+01}


def _to_microbatches(a, axis):
    t = _jnp.moveaxis(a, axis, 0)
    t = t.reshape((N_MICROBATCH, t.shape[0] // N_MICROBATCH) + t.shape[1:])
    return _jnp.moveaxis(t, 1, axis + 1)


def setup_inputs(seed: int = 0) -> dict:
    inp = _fwd_setup_inputs(seed)
    key = _jax.random.fold_in(_jax.random.key(seed), 7919)
    shape, _ = _output_shape()
    out = dict(inp)
    out["loss_target"] = _jax.random.normal(_jax.random.fold_in(key, 0), shape, _jnp.float32)
    for i, name in enumerate(TWIN_WEIGHTS):
        w = inp[name].astype(_jnp.float32)
        if MOMENT_SCALE is None:
            s = _jnp.sqrt(_jnp.mean(_jnp.square(w)) + 1e-30)
        else:
            s = MOMENT_SCALE[name]
        km, kv = _jax.random.split(_jax.random.fold_in(key, i + 1))
        out[name] = w
        out["m_" + name] = s * _jax.random.normal(km, w.shape, _jnp.float32)
        out["v_" + name] = (s * s) * _jax.random.uniform(kv, w.shape, _jnp.float32, 0.5, 1.5)
    if N_MICROBATCH > 1:
        for name, axis in PER_EXAMPLE_BATCH_AXIS.items():
            out[name] = _to_microbatches(out[name], axis)
    return {'x': out['x'], 'norm_mix': out['norm_mix'], 'w_qkv': out['w_qkv'], 'w_gate': out['w_gate'], 'b_gate': out['b_gate'], 'rpb': out['rpb'], 'w_proj_a': out['w_proj_a'], 'w_proj_b': out['w_proj_b'], 'w_out': out['w_out'], 'norm_mlp': out['norm_mlp'], 'w_up': out['w_up'], 'w_down': out['w_down'], 'norm_final': out['norm_final'], 'loss_target': out['loss_target'], 'm_norm_mix': out['m_norm_mix'], 'm_w_qkv': out['m_w_qkv'], 'm_w_gate': out['m_w_gate'], 'm_b_gate': out['m_b_gate'], 'm_rpb': out['m_rpb'], 'm_w_proj_a': out['m_w_proj_a'], 'm_w_proj_b': out['m_w_proj_b'], 'm_w_out': out['m_w_out'], 'm_norm_mlp': out['m_norm_mlp'], 'm_w_up': out['m_w_up'], 'm_w_down': out['m_w_down'], 'm_norm_final': out['m_norm_final'], 'v_norm_mix': out['v_norm_mix'], 'v_w_qkv': out['v_w_qkv'], 'v_w_gate': out['v_w_gate'], 'v_b_gate': out['v_b_gate'], 'v_rpb': out['v_rpb'], 'v_w_proj_a': out['v_w_proj_a'], 'v_w_proj_b': out['v_w_proj_b'], 'v_w_out': out['v_w_out'], 'v_norm_mlp': out['v_norm_mlp'], 'v_w_up': out['v_w_up'], 'v_w_down': out['v_w_down'], 'v_norm_final': out['v_norm_final']}


def _loss(weights, diff, rest, loss_target):
    with _jax.named_scope("forward"):
        args = {**rest, TWIN_DIFF_INPUT: diff, **{k: w.astype(_WEIGHT_DTYPES[k]) for k, w in weights.items()}}
        y = _forward(args)
    with _jax.named_scope("loss_head"):
        err = _jnp.square(y.astype(_jnp.float32) - loss_target)
        return 0.5 * _jnp.sum(_jnp.mean(err, axis=-1)) if err.ndim else 0.5 * err


def _adamw(w, g, m, v):
    m = ADAM_B1 * m + (1.0 - ADAM_B1) * g
    v = ADAM_B2 * v + (1.0 - ADAM_B2) * _jnp.square(g)
    m_hat = m / (1.0 - ADAM_B1 ** ADAM_STEP)
    v_hat = v / (1.0 - ADAM_B2 ** ADAM_STEP)
    delta = -ADAM_LR * (m_hat / (_jnp.sqrt(v_hat) + ADAM_EPS) + ADAM_WD * w)
    return delta, m, v


def reference(x, norm_mix, w_qkv, w_gate, b_gate, rpb, w_proj_a, w_proj_b, w_out, norm_mlp, w_up, w_down, norm_final, loss_target, m_norm_mix, m_w_qkv, m_w_gate, m_b_gate, m_rpb, m_w_proj_a, m_w_proj_b, m_w_out, m_norm_mlp, m_w_up, m_w_down, m_norm_final, v_norm_mix, v_w_qkv, v_w_gate, v_b_gate, v_rpb, v_w_proj_a, v_w_proj_b, v_w_out, v_norm_mlp, v_w_up, v_w_down, v_norm_final):
    given = dict(x=x, norm_mix=norm_mix, w_qkv=w_qkv, w_gate=w_gate, b_gate=b_gate, rpb=rpb, w_proj_a=w_proj_a, w_proj_b=w_proj_b, w_out=w_out, norm_mlp=norm_mlp, w_up=w_up, w_down=w_down, norm_final=norm_final, loss_target=loss_target, m_norm_mix=m_norm_mix, m_w_qkv=m_w_qkv, m_w_gate=m_w_gate, m_b_gate=m_b_gate, m_rpb=m_rpb, m_w_proj_a=m_w_proj_a, m_w_proj_b=m_w_proj_b, m_w_out=m_w_out, m_norm_mlp=m_norm_mlp, m_w_up=m_w_up, m_w_down=m_w_down, m_norm_final=m_norm_final, v_norm_mix=v_norm_mix, v_w_qkv=v_w_qkv, v_w_gate=v_w_gate, v_b_gate=v_b_gate, v_rpb=v_rpb, v_w_proj_a=v_w_proj_a, v_w_proj_b=v_w_proj_b, v_w_out=v_w_out, v_norm_mlp=v_norm_mlp, v_w_up=v_w_up, v_w_down=v_w_down, v_norm_final=v_norm_final)
    weights = {n: given[n] for n in TWIN_WEIGHTS}
    shared = {n: given[n] for n in SHARED_INPUTS}
    per_example = {n: given[n] for n in ['x']}
    grad_fn = _jax.value_and_grad(_loss, argnums=(0, 1))

    def one_microbatch(ex, loss_target):
        ex = dict(ex)
        diff = ex.pop(TWIN_DIFF_INPUT)
        return grad_fn(weights, diff, {**shared, **ex}, loss_target)

    if N_MICROBATCH == 1:
        loss, (grad_w, grad_x) = one_microbatch(per_example, given["loss_target"])
    else:
        def body(carry, xs):
            loss_sum, grad_sum = carry
            l_k, (gw_k, gx_k) = one_microbatch(xs[0], xs[1])
            with _jax.named_scope("update"):
                return (loss_sum + l_k, _jax.tree.map(_jnp.add, grad_sum, gw_k)), gx_k

        init = (_jnp.zeros((), _jnp.float32), _jax.tree.map(_jnp.zeros_like, weights))
        (loss, grad_w), grad_x = _jax.lax.scan(body, init, (per_example, given["loss_target"]))
    with _jax.named_scope("update"):
        delta_w, new_m, new_v = {}, {}, {}
        for n in TWIN_WEIGHTS:
            delta_w[n], new_m[n], new_v[n] = _adamw(weights[n], grad_w[n], given["m_" + n], given["v_" + n])
    return (loss, grad_x, *[grad_w[n] for n in TWIN_WEIGHTS], *[delta_w[n] for n in TWIN_WEIGHTS],
            *[new_m[n] for n in TWIN_WEIGHTS], *[new_v[n] for n in TWIN_WEIGHTS])
```

```python
import numpy as np
import jax
import jax.numpy as jnp
from jax import lax
from jax.experimental import pallas as pl
from jax.experimental.pallas import tpu as pltpu

F32 = jnp.float32
BF16 = jnp.bfloat16
MESH = pl.DeviceIdType.MESH

D_MODEL = 2048
HEAD_DIM = 128
N_HEADS = 16
N_HEADS_A = 12
N_HEADS_B = 4
DILATIONS = (1, 4, 16)
HALF_WINDOW = 64
GRID_W = 64
NA_ROWS = 8
NA_COLS = 16
D_FF = 4 * D_MODEL
N_DEV = 8
EPS = 1e-6
NEG = -1e30
SCALE = HEAD_DIM ** -0.5
ADAM_LR, ADAM_B1, ADAM_B2, ADAM_EPS, ADAM_WD, ADAM_STEP = 0.001, 0.9, 0.999, 1e-08, 0.01, 10

VMEM_LIMIT = 56 * 1024 * 1024
TM = 512
BQ_A = 128


def _pc(body, *, name, grid, in_specs, out_specs, out_shape, scratch=(), sem=None):
    return pl.pallas_call(
        body, name=name, grid=grid, in_specs=in_specs, out_specs=out_specs, out_shape=out_shape,
        scratch_shapes=list(scratch),
        compiler_params=pltpu.CompilerParams(dimension_semantics=sem, vmem_limit_bytes=VMEM_LIMIT))


_DOT_DIMS = {"nn": (((1,), (0,)), ((), ())), "nt": (((1,), (1,)), ((), ())), "tn": (((0,), (0,)), ((), ()))}


def _dot(a, b, mode, precision=None):
    return lax.dot_general(a, b, _DOT_DIMS[mode], preferred_element_type=F32, precision=precision)


def _mm(name, grid, pairs, extras, outs, epilogue, acc_shape, sem):
    n_pairs, n_extra, n_out = len(pairs), len(extras), len(outs)
    k_axis = len(grid) - 1
    nk = grid[k_axis]

    def body(*refs):
        pids = [pl.program_id(ax) for ax in range(len(grid))]
        ab = refs[:2 * n_pairs]
        ex = refs[2 * n_pairs:2 * n_pairs + n_extra]
        o = refs[2 * n_pairs + n_extra:2 * n_pairs + n_extra + n_out]
        part = None
        for p, pair in enumerate(pairs):
            a = ab[2 * p][...]
            if pair[5] is not None:
                a = pair[5](a)
            d = _dot(a, ab[2 * p + 1][...], pair[4])
            part = d if part is None else part + d
        if nk == 1:
            epilogue(part, ex, o, pids)
        else:
            acc = refs[-1]
            k = pids[k_axis]

            @pl.when(k == 0)
            def _():
                acc[...] = part

            @pl.when(k > 0)
            def _():
                acc[...] += part

            @pl.when(k == nk - 1)
            def _():
                epilogue(acc[...], ex, o, pids)

    operands, in_specs = [], []
    for a, a_spec, b, b_spec, _, _ in pairs:
        operands += [a, b]
        in_specs += [a_spec, b_spec]
    for e, e_spec in extras:
        operands.append(e)
        in_specs.append(e_spec)
    res = _pc(body, name=name, grid=grid, in_specs=in_specs, out_specs=[s for _, s in outs],
              out_shape=[o for o, _ in outs], scratch=[pltpu.VMEM(acc_shape, F32)] if nk > 1 else [], sem=sem)(*operands)
    return res


def _relu2_bf16(u):
    r = jnp.maximum(u.astype(F32), 0.0)
    return (r * r).astype(BF16)


def _rmsnorm_fwd(x, g, name):
    T, D = x.shape

    def body(x_ref, g_ref, h_ref, r_ref):
        xv = x_ref[...]
        r = lax.rsqrt(jnp.mean(xv * xv, axis=-1, keepdims=True) + EPS)
        h_ref[...] = (xv * r * g_ref[...]).astype(BF16)
        r_ref[...] = r

    return _pc(body, name=name, grid=(T // TM,),
               in_specs=[pl.BlockSpec((TM, D), lambda i: (i, 0)), pl.BlockSpec((1, D), lambda i: (0, 0))],
               out_specs=[pl.BlockSpec((TM, D), lambda i: (i, 0)), pl.BlockSpec((TM, 1), lambda i: (i, 0))],
               out_shape=[jax.ShapeDtypeStruct((T, D), BF16), jax.ShapeDtypeStruct((T, 1), F32)],
               sem=("parallel",))(x, g)


def _rmsnorm_bwd_block(dh, x, r, g, dres):
    xh = x * r
    dxh = dh * g
    dx = dres + r * (dxh - xh * jnp.mean(dxh * xh, axis=-1, keepdims=True))
    return dx, jnp.sum(dh * xh, axis=0, keepdims=True)


def _final_norm_loss(x3, g, tgt):
    T, D = x3.shape

    def body(x_ref, g_ref, t_ref, loss_ref, dx_ref, dxb_ref, dg_ref):
        i = pl.program_id(0)
        xv = x_ref[...]
        gv = g_ref[...]
        r = lax.rsqrt(jnp.mean(xv * xv, axis=-1, keepdims=True) + EPS)
        xh = xv * r
        e = xh * gv - t_ref[...]
        part = 0.5 * jnp.sum(jnp.mean(e * e, axis=-1, keepdims=True))
        dy = e * (1.0 / D)
        dxh = dy * gv
        dx = r * (dxh - xh * jnp.mean(dxh * xh, axis=-1, keepdims=True))
        dx_ref[...] = dx
        dxb_ref[...] = dx.astype(BF16)
        dgp = jnp.sum(dy * xh, axis=0, keepdims=True)

        @pl.when(i == 0)
        def _():
            loss_ref[...] = jnp.zeros(loss_ref.shape, F32) + part
            dg_ref[...] = dgp

        @pl.when(i > 0)
        def _():
            loss_ref[...] += part
            dg_ref[...] += dgp

    row = pl.BlockSpec((TM, D), lambda i: (i, 0))
    vec = pl.BlockSpec((1, D), lambda i: (0, 0))
    return _pc(body, name="final_norm_loss", grid=(T // TM,), in_specs=[row, vec, row],
               out_specs=[pl.BlockSpec((8, 128), lambda i: (0, 0)), row, row, vec],
               out_shape=[jax.ShapeDtypeStruct((8, 128), F32), jax.ShapeDtypeStruct((T, D), F32),
                          jax.ShapeDtypeStruct((T, D), BF16), jax.ShapeDtypeStruct((1, D), F32)],
               sem=("arbitrary",))(x3, g, tgt)


def _alibi_slopes():
    return np.asarray(2.0 ** (-8.0 * np.arange(1, N_HEADS_A + 1) / N_HEADS_A), dtype=np.float32)


def _a_windows(T):
    half = [HALF_WINDOW * d for d in DILATIONS]
    return half, [min(BQ_A + 2 * w, T) for w in half]


def _a_scores(q, kw, slope, g, t0, ws, wk, half):
    d = DILATIONS[g]
    s = _dot(q, kw, "nt") * SCALE
    qi = t0 + lax.broadcasted_iota(jnp.int32, (BQ_A, wk), 0)
    ki = ws + lax.broadcasted_iota(jnp.int32, (BQ_A, wk), 1)
    dlt = ki - qi
    ad = jnp.maximum(dlt, -dlt)
    ok = (ad <= half) & ((ad & (d - 1)) == 0)
    return jnp.where(ok, s - slope * ad.astype(F32), NEG)


def _a_qkv_specs(T, inner):
    specs = []
    for g in range(3):
        specs.append(pl.BlockSpec((BQ_A, HEAD_DIM), lambda j, i, g=g: (i, 4 * g + j)))
        specs.append(pl.BlockSpec((T, HEAD_DIM), lambda j, i, g=g: (0, N_HEADS + 4 * g + j)))
        specs.append(pl.BlockSpec((T, HEAD_DIM), lambda j, i, g=g: (0, 2 * N_HEADS + 4 * g + j)))
    return specs


def _attn_a_fwd(qkv, slopes):
    T = qkv.shape[0]
    nq = T // BQ_A
    half, wks = _a_windows(T)

    def body(sl_ref, *refs):
        qkv_refs, (y_ref, lse_ref) = refs[:9], refs[9:]
        j = pl.program_id(0)
        t0 = pl.program_id(1) * BQ_A
        ss, vws = [], []
        for g in range(3):
            q_ref, k_ref, v_ref = qkv_refs[3 * g:3 * g + 3]
            ws = pl.multiple_of(jnp.clip(t0 - half[g], 0, T - wks[g]), 64)
            ss.append(_a_scores(q_ref[...], k_ref[pl.ds(ws, wks[g]), :], sl_ref[4 * g + j], g, t0, ws, wks[g], half[g]))
            vws.append(v_ref[pl.ds(ws, wks[g]), :])
        m = jnp.maximum(jnp.maximum(jnp.max(ss[0], axis=-1, keepdims=True), jnp.max(ss[1], axis=-1, keepdims=True)),
                        jnp.max(ss[2], axis=-1, keepdims=True))
        den = jnp.zeros((BQ_A, 1), F32)
        acc = jnp.zeros((BQ_A, HEAD_DIM), F32)
        for g in range(3):
            p = jnp.exp(ss[g] - m)
            den = den + jnp.sum(p, axis=-1, keepdims=True)
            acc = acc + _dot(p.astype(BF16), vws[g], "nn")
        y_ref[...] = (acc / den).astype(BF16)
        lse_ref[...] = m + jnp.log(den)

    return _pc(body, name="attn_a_fwd", grid=(4, nq),
               in_specs=[pl.BlockSpec(memory_space=pltpu.SMEM)] + _a_qkv_specs(T, nq),
               out_specs=[pl.BlockSpec((BQ_A, HEAD_DIM), lambda j, i: (i, j)),
                          pl.BlockSpec((None, BQ_A, 1), lambda j, i: (j, i, 0))],
               out_shape=[jax.ShapeDtypeStruct((T, 4 * HEAD_DIM), BF16), jax.ShapeDtypeStruct((4, T, 1), F32)],
               sem=("parallel", "parallel"))(slopes, *([qkv] * 9))


def _attn_a_bwd(qkv, slopes, dy, y, lse):
    T = qkv.shape[0]
    nq = T // BQ_A
    half, wks = _a_windows(T)

    def body(sl_ref, *refs):
        qkv_refs = refs[:9]
        dy_ref, y_ref, lse_ref = refs[9:12]
        dq_ref, dk_ref, dv_ref = refs[12:15]
        dk_acc, dv_acc = refs[15:]
        j = pl.program_id(0)
        i = pl.program_id(1)
        t0 = i * BQ_A

        @pl.when(i == 0)
        def _():
            dk_acc[...] = jnp.zeros_like(dk_acc)
            dv_acc[...] = jnp.zeros_like(dv_acc)

        dyv = dy_ref[...]
        delta = jnp.sum(dyv.astype(F32) * y_ref[...].astype(F32), axis=-1, keepdims=True)
        lse_v = lse_ref[...]
        for g in range(3):
            q_ref, k_ref, v_ref = qkv_refs[3 * g:3 * g + 3]
            ws = pl.multiple_of(jnp.clip(t0 - half[g], 0, T - wks[g]), 64)
            win = pl.ds(ws, wks[g])
            qv = q_ref[...]
            kw = k_ref[win, :]
            s = _a_scores(qv, kw, sl_ref[4 * g + j], g, t0, ws, wks[g], half[g])
            p = jnp.exp(s - lse_v)
            dp = _dot(dyv, v_ref[win, :], "nt")
            ds = (p * (dp - delta)).astype(BF16)
            dq_ref[g] = (_dot(ds, kw, "nn") * SCALE).astype(BF16)
            dk_acc[g, win, :] += _dot(ds, qv, "tn") * SCALE
            dv_acc[g, win, :] += _dot(p.astype(BF16), dyv, "tn")

        @pl.when(i == nq - 1)
        def _():
            dk_ref[...] = dk_acc[...].astype(BF16)
            dv_ref[...] = dv_acc[...].astype(BF16)

    blk = pl.BlockSpec((BQ_A, HEAD_DIM), lambda j, i: (i, j))
    whole = pl.BlockSpec((3, T, HEAD_DIM), lambda j, i: (0, 0, j))
    full = jax.ShapeDtypeStruct((3, T, 4 * HEAD_DIM), BF16)
    return _pc(body, name="attn_a_bwd", grid=(4, nq),
               in_specs=[pl.BlockSpec(memory_space=pltpu.SMEM)] + _a_qkv_specs(T, nq)
               + [blk, blk, pl.BlockSpec((None, BQ_A, 1), lambda j, i: (j, i, 0))],
               out_specs=[pl.BlockSpec((3, BQ_A, HEAD_DIM), lambda j, i: (0, i, j)), whole, whole],
               out_shape=[full, full, full],
               scratch=[pltpu.VMEM((3, T, HEAD_DIM), F32), pltpu.VMEM((3, T, HEAD_DIM), F32)],
               sem=("parallel", "arbitrary"))(slopes, *([qkv] * 9), dy, y, lse)


N_DR = 2 * NA_ROWS - 1
N_DC = 2 * NA_COLS - 1
WK_B = NA_ROWS * GRID_W


def _toeplitz_consts():
    col = np.arange(GRID_W)
    start = np.clip(col - NA_COLS // 2, 0, GRID_W - NA_COLS)
    ok = (col[None, :] >= start[:, None]) & (col[None, :] < start[:, None] + NA_COLS)
    dc = np.clip(col[None, :] - col[:, None], -(NA_COLS - 1), NA_COLS - 1) + NA_COLS - 1
    sel = np.zeros((128, GRID_W * GRID_W), np.float32)
    sel[dc.reshape(-1), np.arange(GRID_W * GRID_W)] = ok.reshape(-1).astype(np.float32)
    mask = np.where(ok, 0.0, NEG).astype(np.float32).reshape(1, -1)
    return sel, mask


def _rpb_expand(rpb_rows, sel, mask):
    def body(r_ref, s_ref, m_ref, o_ref):
        o_ref[...] = _dot(r_ref[...], s_ref[...], "nn", precision=lax.Precision.HIGHEST) + m_ref[...]

    n = rpb_rows.shape[0]
    return _pc(body, name="rpb_expand", grid=(1,),
               in_specs=[pl.BlockSpec((n, 128), lambda i: (0, 0)), pl.BlockSpec(sel.shape, lambda i: (0, 0)),
                         pl.BlockSpec(mask.shape, lambda i: (0, 0))],
               out_specs=pl.BlockSpec((n, sel.shape[1]), lambda i: (0, 0)),
               out_shape=jax.ShapeDtypeStruct((n, sel.shape[1]), F32), sem=("arbitrary",))(rpb_rows, sel, mask)


def _rpb_reduce(dtiles, sel):
    def body(t_ref, s_ref, o_ref):
        o_ref[...] = _dot(t_ref[...], s_ref[...], "nt", precision=lax.Precision.HIGHEST)

    n = dtiles.shape[0]
    return _pc(body, name="rpb_reduce", grid=(1,),
               in_specs=[pl.BlockSpec(dtiles.shape, lambda i: (0, 0)), pl.BlockSpec(sel.shape, lambda i: (0, 0))],
               out_specs=pl.BlockSpec((n, 128), lambda i: (0, 0)),
               out_shape=jax.ShapeDtypeStruct((n, 128), F32), sem=("arbitrary",))(dtiles, sel)


def _b_specs(T):
    return [pl.BlockSpec((GRID_W, HEAD_DIM), lambda h, r: (r, N_HEADS_A + h)),
            pl.BlockSpec((T, HEAD_DIM), lambda h, r: (0, N_HEADS + N_HEADS_A + h)),
            pl.BlockSpec((T, HEAD_DIM), lambda h, r: (0, 2 * N_HEADS + N_HEADS_A + h)),
            pl.BlockSpec((None, N_DR, GRID_W, GRID_W), lambda h, r: (h, 0, 0, 0))]


def _b_row_start(r, rows):
    return jnp.clip(r - NA_ROWS // 2, 0, rows - NA_ROWS)


def _attn_b_fwd(qkv, tiles):
    T = qkv.shape[0]
    rows = T // GRID_W

    def body(q_ref, k_ref, v_ref, t_ref, y_ref, lse_ref, bias_ref):
        r = pl.program_id(1)
        rs = _b_row_start(r, rows)
        win = pl.ds(pl.multiple_of(rs * GRID_W, GRID_W), WK_B)
        for a in range(NA_ROWS):
            bias_ref[:, a * GRID_W:(a + 1) * GRID_W] = t_ref[rs + a - r + NA_ROWS - 1]
        s = _dot(q_ref[...], k_ref[win, :], "nt") * SCALE + bias_ref[...]
        m = jnp.max(s, axis=-1, keepdims=True)
        p = jnp.exp(s - m)
        den = jnp.sum(p, axis=-1, keepdims=True)
        y_ref[...] = (_dot(p.astype(BF16), v_ref[win, :], "nn") / den).astype(BF16)
        lse_ref[...] = m + jnp.log(den)

    return _pc(body, name="attn_b_fwd", grid=(N_HEADS_B, rows), in_specs=_b_specs(T),
               out_specs=[pl.BlockSpec((GRID_W, HEAD_DIM), lambda h, r: (r, h)),
                          pl.BlockSpec((None, GRID_W, 1), lambda h, r: (h, r, 0))],
               out_shape=[jax.ShapeDtypeStruct((T, N_HEADS_B * HEAD_DIM), BF16),
                          jax.ShapeDtypeStruct((N_HEADS_B, T, 1), F32)],
               scratch=[pltpu.VMEM((GRID_W, WK_B), F32)], sem=("parallel", "parallel"))(qkv, qkv, qkv, tiles)


def _attn_b_bwd(qkv, tiles, dy, y, lse):
    T = qkv.shape[0]
    rows = T // GRID_W

    def body(q_ref, k_ref, v_ref, t_ref, dy_ref, y_ref, lse_ref, dq_ref, dk_ref, dv_ref, dt_ref,
             bias_ref, ds_ref, dk_acc, dv_acc):
        r = pl.program_id(1)
        rs = _b_row_start(r, rows)
        win = pl.ds(pl.multiple_of(rs * GRID_W, GRID_W), WK_B)

        @pl.when(r == 0)
        def _():
            dk_acc[...] = jnp.zeros_like(dk_acc)
            dv_acc[...] = jnp.zeros_like(dv_acc)
            dt_ref[...] = jnp.zeros_like(dt_ref)

        for a in range(NA_ROWS):
            bias_ref[:, a * GRID_W:(a + 1) * GRID_W] = t_ref[rs + a - r + NA_ROWS - 1]
        qv = q_ref[...]
        kw = k_ref[win, :]
        dyv = dy_ref[...]
        s = _dot(qv, kw, "nt") * SCALE + bias_ref[...]
        p = jnp.exp(s - lse_ref[...])
        delta = jnp.sum(dyv.astype(F32) * y_ref[...].astype(F32), axis=-1, keepdims=True)
        ds = p * (_dot(dyv, v_ref[win, :], "nt") - delta)
        ds_ref[...] = ds
        for a in range(NA_ROWS):
            dt_ref[rs + a - r + NA_ROWS - 1] += ds_ref[:, a * GRID_W:(a + 1) * GRID_W]
        dsb = ds.astype(BF16)
        dq_ref[...] = (_dot(dsb, kw, "nn") * SCALE).astype(BF16)
        dk_acc[win, :] += _dot(dsb, qv, "tn") * SCALE
        dv_acc[win, :] += _dot(p.astype(BF16), dyv, "tn")

        @pl.when(r == rows - 1)
        def _():
            dk_ref[...] = dk_acc[...].astype(BF16)
            dv_ref[...] = dv_acc[...].astype(BF16)

    blk = pl.BlockSpec((GRID_W, HEAD_DIM), lambda h, r: (r, h))
    whole = pl.BlockSpec((T, HEAD_DIM), lambda h, r: (0, h))
    full = jax.ShapeDtypeStruct((T, N_HEADS_B * HEAD_DIM), BF16)
    return _pc(body, name="attn_b_bwd", grid=(N_HEADS_B, rows),
               in_specs=_b_specs(T) + [blk, blk, pl.BlockSpec((None, GRID_W, 1), lambda h, r: (h, r, 0))],
               out_specs=[blk, whole, whole, pl.BlockSpec((None, N_DR, GRID_W, GRID_W), lambda h, r: (h, 0, 0, 0))],
               out_shape=[full, full, full, jax.ShapeDtypeStruct((N_HEADS_B, N_DR, GRID_W, GRID_W), F32)],
               scratch=[pltpu.VMEM((GRID_W, WK_B), F32), pltpu.VMEM((GRID_W, WK_B), F32),
                        pltpu.VMEM((T, HEAD_DIM), F32), pltpu.VMEM((T, HEAD_DIM), F32)],
               sem=("parallel", "arbitrary"))(qkv, qkv, qkv, tiles, dy, y, lse)


def _sds(shape, dtype):
    return jax.ShapeDtypeStruct(shape, dtype)


def _local_step(x, tgt, norm_mix, b_gate, rpb, norm_mlp, norm_final, wq, wg, wpa, wpb, wo, wu, wd):
    T, D = x.shape
    nt = T // TM
    PP, PA = ("parallel", "parallel", "arbitrary"), ("arbitrary", "arbitrary", "arbitrary")
    wo2 = wo.reshape(D, D)
    wd2 = wd.reshape(D_FF, D)
    slopes = jnp.asarray(_alibi_slopes())
    sel_np, mask_np = _toeplitz_consts()
    sel, mask = jnp.asarray(sel_np), jnp.asarray(mask_np)

    def rows(w):
        return pl.BlockSpec((TM, w), lambda i, j, k: (i, j))

    row_all = pl.BlockSpec((TM, D), lambda i, j, k: (i, 0))
    rvec = pl.BlockSpec((TM, 1), lambda i, j, k: (i, 0))
    gvec = pl.BlockSpec((1, D), lambda i, j, k: (0, 0))

    h1, r1 = _rmsnorm_fwd(x, norm_mix, "rmsnorm_mix")

    def store_bf16(acc, ex, o, pids):
        o[0][...] = acc.astype(BF16)

    (qkv,) = _mm("qkv_proj", (nt, N_DEV, 1),
                 [(h1, row_all, wq, pl.BlockSpec((None, D, 768), lambda i, j, k: (j, 0, 0)), "nn", None)],
                 [], [(_sds((T, 3 * D), BF16), rows(768))], store_bf16, None, PP)

    def gate_epi(acc, ex, o, pids):
        o[0][...] = jax.nn.sigmoid(acc + ex[0][...])

    (gates,) = _mm("gate_proj", (nt, N_DEV, 1),
                   [(h1, row_all, wg, pl.BlockSpec((None, D, 512), lambda i, j, k: (j, 0, 0)), "nn", None)],
                   [(b_gate, pl.BlockSpec((1, 512), lambda i, j, k: (0, j)))],
                   [(_sds((T, 2 * D), F32), rows(512))], gate_epi, None, PP)

    rpb_rows = jnp.pad(rpb.reshape(N_HEADS_B * N_DR, N_DC), ((0, 0), (0, 128 - N_DC)))
    tiles = _rpb_expand(rpb_rows, sel, mask).reshape(N_HEADS_B, N_DR, GRID_W, GRID_W)
    ya, lse_a = _attn_a_fwd(qkv, slopes)
    yb, lse_b = _attn_b_fwd(qkv, tiles)

    def merge_body(ya_ref, wa_ref, yb_ref, wb_ref, ga_ref, gb_ref, pa_ref, pb_ref, mg_ref):
        pa = _dot(ya_ref[...], wa_ref[...], "nn")
        pb = _dot(yb_ref[...], wb_ref[...], "nn")
        pa_ref[...] = pa
        pb_ref[...] = pb
        mg_ref[...] = (ga_ref[...] * pa + gb_ref[...] * pb).astype(BF16)

    y_spec = pl.BlockSpec((TM, 512), lambda i, j: (i, 0))
    wp_spec = pl.BlockSpec((None, 512, 256), lambda i, j: (j, 0, 0))
    c_spec = pl.BlockSpec((TM, 256), lambda i, j: (i, j))
    pa, pb, merged = _pc(merge_body, name="merge_proj", grid=(nt, N_DEV),
                         in_specs=[y_spec, wp_spec, y_spec, wp_spec, c_spec,
                                   pl.BlockSpec((TM, 256), lambda i, j: (i, N_DEV + j))],
                         out_specs=[c_spec, c_spec, c_spec],
                         out_shape=[_sds((T, D), F32), _sds((T, D), F32), _sds((T, D), BF16)],
                         sem=("parallel", "parallel"))(ya, wpa, yb, wpb, gates, gates)

    def add_res(acc, ex, o, pids):
        o[0][...] = ex[0][...] + acc

    (x2,) = _mm("out_proj", (nt, 4, 1),
                [(merged, row_all, wo2, pl.BlockSpec((D, 512), lambda i, j, k: (0, j)), "nn", None)],
                [(x, rows(512))], [(_sds((T, D), F32), rows(512))], add_res, None, PP)

    h2, r2 = _rmsnorm_fwd(x2, norm_mlp, "rmsnorm_mlp")
    (u,) = _mm("mlp_up", (nt, N_DEV, 1),
               [(h2, row_all, wu, pl.BlockSpec((None, D, 1024), lambda i, j, k: (j, 0, 0)), "nn", None)],
               [], [(_sds((T, D_FF), BF16), rows(1024))], store_bf16, None, PP)
    (x3,) = _mm("mlp_down", (nt, 1, N_DEV),
                [(u, pl.BlockSpec((TM, 1024), lambda i, j, k: (i, k)), wd2,
                  pl.BlockSpec((1024, D), lambda i, j, k: (k, 0)), "nn", _relu2_bf16)],
                [(x2, row_all)], [(_sds((T, D), F32), row_all)], add_res, (TM, D), PP)

    loss, dx3, dx3b, dg3 = _final_norm_loss(x3, norm_final.reshape(1, D), tgt)

    def du_epi(acc, ex, o, pids):
        o[0][...] = (acc * (2.0 * jnp.maximum(ex[0][...].astype(F32), 0.0))).astype(BF16)

    (du,) = _mm("mlp_down_dx", (nt, N_DEV, 1),
                [(dx3b, row_all, wd2, pl.BlockSpec((1024, D), lambda i, j, k: (j, 0)), "nt", None)],
                [(u, rows(1024))], [(_sds((T, D_FF), BF16), rows(1024))], du_epi, None, PP)
    (dwd,) = _mm("mlp_down_dw", (N_DEV, 1, nt),
                 [(u, pl.BlockSpec((TM, 1024), lambda i, j, k: (k, i)), dx3b,
                   pl.BlockSpec((TM, D), lambda i, j, k: (k, 0)), "tn", _relu2_bf16)],
                 [], [(_sds((N_DEV, 1024, D), BF16), pl.BlockSpec((None, 1024, D), lambda i, j, k: (i, 0, 0)))],
                 store_bf16, (1024, D), PP)
    (dwu,) = _mm("mlp_up_dw", (N_DEV, 1, nt),
                 [(h2, pl.BlockSpec((TM, D), lambda i, j, k: (k, 0)), du,
                   pl.BlockSpec((TM, 1024), lambda i, j, k: (k, i)), "tn", None)],
                 [], [(_sds((N_DEV, D, 1024), BF16), pl.BlockSpec((None, D, 1024), lambda i, j, k: (i, 0, 0)))],
                 store_bf16, (D, 1024), PP)

    def norm_bwd_epi(acc, ex, o, pids):
        x_ref, r_ref, g_ref, dres_ref = ex
        dx, dgp = _rmsnorm_bwd_block(acc, x_ref[...], r_ref[...], g_ref[...], dres_ref[...])
        o[0][...] = dx
        if len(o) == 3:
            o[1][...] = dx.astype(BF16)
        first = pids[0] == 0

        @pl.when(first)
        def _():
            o[-1][...] = dgp

        @pl.when(jnp.logical_not(first))
        def _():
            o[-1][...] += dgp

    dx2, dx2b, dg2 = _mm("mlp_up_dx", (nt, 1, N_DEV),
                         [(du, pl.BlockSpec((TM, 1024), lambda i, j, k: (i, k)), wu,
                           pl.BlockSpec((None, D, 1024), lambda i, j, k: (k, 0, 0)), "nt", None)],
                         [(x2, row_all), (r2, rvec), (norm_mlp, gvec), (dx3, row_all)],
                         [(_sds((T, D), F32), row_all), (_sds((T, D), BF16), row_all), (_sds((1, D), F32), gvec)],
                         norm_bwd_epi, (TM, D), PA)

    (dwo,) = _mm("out_proj_dw", (4, 1, nt),
                 [(merged, pl.BlockSpec((TM, 512), lambda i, j, k: (k, i)), dx2b,
                   pl.BlockSpec((TM, D), lambda i, j, k: (k, 0)), "tn", None)],
                 [], [(_sds((D, D), BF16), pl.BlockSpec((512, D), lambda i, j, k: (i, 0)))], store_bf16, (512, D), PP)

    def dmerge_epi(acc, ex, o, pids):
        ga, gb, pa_v, pb_v = (e[...] for e in ex)
        o[0][...] = (acc * ga).astype(BF16)
        o[1][...] = (acc * gb).astype(BF16)
        dga = acc * pa_v * ga * (1.0 - ga)
        dgb = acc * pb_v * gb * (1.0 - gb)
        o[2][...] = dga.astype(BF16)
        o[3][...] = dgb.astype(BF16)
        sa = jnp.sum(dga, axis=0, keepdims=True)
        sb = jnp.sum(dgb, axis=0, keepdims=True)
        first = pids[1] == 0

        @pl.when(first)
        def _():
            o[4][...] = sa
            o[5][...] = sb

        @pl.when(jnp.logical_not(first))
        def _():
            o[4][...] += sa
            o[5][...] += sb

    cj = pl.BlockSpec((TM, 512), lambda j, i, k: (i, j))
    bj = pl.BlockSpec((1, 512), lambda j, i, k: (0, j))
    dpa, dpb, dga, dgb, dba, dbb = _mm(
        "out_proj_dx", (4, nt, 1),
        [(dx2b, pl.BlockSpec((TM, D), lambda j, i, k: (i, 0)), wo2, pl.BlockSpec((512, D), lambda j, i, k: (j, 0)), "nt", None)],
        [(gates, cj), (gates, pl.BlockSpec((TM, 512), lambda j, i, k: (i, 4 + j))), (pa, cj), (pb, cj)],
        [(_sds((T, D), BF16), cj)] * 4 + [(_sds((1, D), F32), bj)] * 2, dmerge_epi, None,
        ("parallel", "arbitrary", "arbitrary"))

    def proj_dw(name, y, dp):
        (dw,) = _mm(name, (N_DEV, 1, nt),
                    [(y, pl.BlockSpec((TM, 512), lambda i, j, k: (k, 0)), dp,
                      pl.BlockSpec((TM, 256), lambda i, j, k: (k, i)), "tn", None)],
                    [], [(_sds((N_DEV, 512, 256), BF16), pl.BlockSpec((None, 512, 256), lambda i, j, k: (i, 0, 0)))],
                    store_bf16, (512, 256), PP)
        return dw

    def proj_dx(name, dp, w):
        (dy,) = _mm(name, (nt, 1, N_DEV),
                    [(dp, pl.BlockSpec((TM, 256), lambda i, j, k: (i, k)), w,
                      pl.BlockSpec((None, 512, 256), lambda i, j, k: (k, 0, 0)), "nt", None)],
                    [], [(_sds((T, 512), BF16), pl.BlockSpec((TM, 512), lambda i, j, k: (i, 0)))],
                    store_bf16, (TM, 512), PP)
        return dy

    dwpa = proj_dw("proj_a_dw", ya, dpa)
    dwpb = proj_dw("proj_b_dw", yb, dpb)
    dya = proj_dx("proj_a_dx", dpa, wpa)
    dyb = proj_dx("proj_b_dx", dpb, wpb)

    dq_a, dk_a, dv_a = _attn_a_bwd(qkv, slopes, dya, ya, lse_a)
    dq_b, dk_b, dv_b, dtiles = _attn_b_bwd(qkv, tiles, dyb, yb, lse_b)
    drpb = _rpb_reduce(dtiles.reshape(N_HEADS_B * N_DR, GRID_W * GRID_W), sel)[:, :N_DC]
    dqkv = jnp.concatenate([dq_a[0], dq_a[1], dq_a[2], dq_b, dk_a[0], dk_a[1], dk_a[2], dk_b,
                            dv_a[0], dv_a[1], dv_a[2], dv_b], axis=1)
    dgate = jnp.concatenate([dga, dgb], axis=1)

    def in_dw(name, dact, n):
        (dw,) = _mm(name, (N_DEV, 1, nt),
                    [(h1, pl.BlockSpec((TM, D), lambda i, j, k: (k, 0)), dact,
                      pl.BlockSpec((TM, n), lambda i, j, k: (k, i)), "tn", None)],
                    [], [(_sds((N_DEV, D, n), BF16), pl.BlockSpec((None, D, n), lambda i, j, k: (i, 0, 0)))],
                    store_bf16, (D, n), PP)
        return dw

    dwq = in_dw("qkv_proj_dw", dqkv, 768)
    dwg = in_dw("gate_proj_dw", dgate, 512)
    grad_x, dg1 = _mm("in_proj_dx", (nt, 1, N_DEV),
                      [(dqkv, pl.BlockSpec((TM, 768), lambda i, j, k: (i, k)), wq,
                        pl.BlockSpec((None, D, 768), lambda i, j, k: (k, 0, 0)), "nt", None),
                       (dgate, pl.BlockSpec((TM, 512), lambda i, j, k: (i, k)), wg,
                        pl.BlockSpec((None, D, 512), lambda i, j, k: (k, 0, 0)), "nt", None)],
                      [(x, row_all), (r1, rvec), (norm_mix, gvec), (dx2, row_all)],
                      [(_sds((T, D), F32), row_all), (_sds((1, D), F32), gvec)], norm_bwd_epi, (TM, D), PA)

    big = dict(w_qkv=dwq, w_gate=dwg, w_proj_a=dwpa, w_proj_b=dwpb, w_out=dwo.reshape(N_DEV, D // N_DEV, D),
               w_up=dwu, w_down=dwd)
    small = dict(norm_mix=dg1, b_gate=jnp.concatenate([dba, dbb], axis=1), rpb=drpb, norm_mlp=dg2, norm_final=dg3)
    return loss, grad_x, big, small


BIG = ("w_qkv", "w_gate", "w_proj_a", "w_proj_b", "w_out", "w_up", "w_down")
SMALL = ("norm_mix", "b_gate", "rpb", "norm_mlp", "norm_final")
SMALL_ROWS = 96


def _place():
    return lax.axis_index("x"), lax.axis_index("y"), lax.axis_index("c")


def _all_gather_weights(shards):
    n_arr = len(shards)

    def body(*refs):
        ins, outs = refs[:n_arr], refs[n_arr:2 * n_arr]
        send_sems, recv_sems, local_sems = refs[2 * n_arr:]
        x, y, c = _place()
        me, sibling = (x, y, c), (x, y, 1 - c)
        chips = [(1 - x, y), (x, 1 - y), (1 - x, 1 - y)]

        def copy(a, k, block, to, from_input=False):
            dst = outs[a].at[4 * block[0] + 2 * block[1] + block[2]]
            return pltpu.make_async_remote_copy(
                src_ref=ins[a] if from_input else dst, dst_ref=dst, send_sem=send_sems.at[7 * a + k],
                recv_sem=recv_sems.at[7 * a + k], device_id=to, device_id_type=MESH)

        mine = [pltpu.make_async_copy(ins[a], outs[a].at[4 * x + 2 * y + c], local_sems.at[a]) for a in range(n_arr)]
        for cp in mine:
            cp.start()
        first = []
        for a in range(n_arr):
            first.append(copy(a, 0, me, sibling, True))
            first += [copy(a, 1 + j, me, (*chip, c), True) for j, chip in enumerate(chips)]
        for cp in first:
            cp.start()
        passed = []
        for j, chip in enumerate(chips):
            for a in range(n_arr):
                copy(a, 1 + j, (*chip, c), me).wait_recv()
                fwd = copy(a, 4 + j, (*chip, c), sibling)
                fwd.start()
                passed.append(fwd)
        for a in range(n_arr):
            copy(a, 0, sibling, me).wait_recv()
            for j, chip in enumerate(chips):
                copy(a, 4 + j, (*chip, 1 - c), me).wait_recv()
        for cp in first + passed:
            cp.wait_send()
        for cp in mine:
            cp.wait()

    any_spec = pl.BlockSpec(memory_space=pl.ANY)
    return pl.pallas_call(
        body, name="all_gather_weights", in_specs=[any_spec] * n_arr, out_specs=[any_spec] * n_arr,
        out_shape=[jax.ShapeDtypeStruct((N_DEV,) + s.shape, s.dtype) for s in shards],
        scratch_shapes=[pltpu.SemaphoreType.DMA((7 * n_arr,)), pltpu.SemaphoreType.DMA((7 * n_arr,)),
                        pltpu.SemaphoreType.DMA((n_arr,))])(*shards)


def _exchange_grads(parts, small):
    n_big = len(parts)
    n_arr = n_big + 1

    def body(*refs):
        ins, outs = refs[:n_arr], refs[n_arr:2 * n_arr]
        send_sems, recv_sems, local_sems = refs[2 * n_arr:]
        x, y, c = _place()
        me = 4 * x + 2 * y + c
        copies = []
        for a in range(n_arr):
            src = ins[a].at[me] if a < n_big else ins[a]
            local = pltpu.make_async_copy(src, outs[a].at[me], local_sems.at[a])
            local.start()
            copies.append(local)
        for mask in range(1, N_DEV):
            px, py, pc = x ^ (mask >> 2), y ^ ((mask >> 1) & 1), c ^ (mask & 1)
            for a in range(n_arr):
                src = ins[a].at[4 * px + 2 * py + pc] if a < n_big else ins[a]
                cp = pltpu.make_async_remote_copy(
                    src_ref=src, dst_ref=outs[a].at[me], send_sem=send_sems.at[7 * a + mask - 1],
                    recv_sem=recv_sems.at[7 * a + mask - 1], device_id=(px, py, pc), device_id_type=MESH)
                cp.start()
                copies.append(cp)
        for cp in copies:
            cp.wait()

    any_spec = pl.BlockSpec(memory_space=pl.ANY)
    arrays = list(parts) + [small]
    return pl.pallas_call(
        body, name="exchange_grads", in_specs=[any_spec] * n_arr, out_specs=[any_spec] * n_arr,
        out_shape=[jax.ShapeDtypeStruct(p.shape, p.dtype) for p in parts]
        + [jax.ShapeDtypeStruct((N_DEV,) + small.shape, small.dtype)],
        scratch_shapes=[pltpu.SemaphoreType.DMA((7 * n_arr,)), pltpu.SemaphoreType.DMA((7 * n_arr,)),
                        pltpu.SemaphoreType.DMA((n_arr,))])(*arrays)


def _adamw(name, parts, w, m, v):
    K, n = w.shape
    tr = min(K, 128)

    def body(p_ref, w_ref, m_ref, v_ref, g_ref, d_ref, nm_ref, nv_ref):
        g = p_ref[0].astype(F32)
        for i in range(1, N_DEV):
            g = g + p_ref[i].astype(F32)
        nm = ADAM_B1 * m_ref[...] + (1.0 - ADAM_B1) * g
        nv = ADAM_B2 * v_ref[...] + (1.0 - ADAM_B2) * jnp.square(g)
        m_hat = nm / (1.0 - ADAM_B1 ** ADAM_STEP)
        v_hat = nv / (1.0 - ADAM_B2 ** ADAM_STEP)
        g_ref[...] = g
        d_ref[...] = -ADAM_LR * (m_hat / (jnp.sqrt(v_hat) + ADAM_EPS) + ADAM_WD * w_ref[...])
        nm_ref[...] = nm
        nv_ref[...] = nv

    blk = pl.BlockSpec((tr, n), lambda i: (i, 0))
    return _pc(body, name=name, grid=(K // tr,), in_specs=[pl.BlockSpec((N_DEV, tr, n), lambda i: (0, i, 0)), blk, blk, blk],
               out_specs=[blk] * 4, out_shape=[jax.ShapeDtypeStruct((K, n), F32)] * 4, sem=("parallel",))(parts, w, m, v)


def _pack_small(vals):
    flat = jnp.concatenate([v.reshape(-1) for v in vals])
    return jnp.pad(flat, (0, SMALL_ROWS * 128 - flat.shape[0])).reshape(SMALL_ROWS, 128)


def _unpack_small(packed, shapes):
    flat, out, off = packed.reshape(-1), [], 0
    for s in shapes:
        size = int(np.prod(s))
        out.append(flat[off:off + size].reshape(s))
        off += size
    return out


def kernel(x, norm_mix, w_qkv, w_gate, b_gate, rpb, w_proj_a, w_proj_b, w_out, norm_mlp, w_up, w_down, norm_final, loss_target, m_norm_mix, m_w_qkv, m_w_gate, m_b_gate, m_rpb, m_w_proj_a, m_w_proj_b, m_w_out, m_norm_mlp, m_w_up, m_w_down, m_norm_final, v_norm_mix, v_w_qkv, v_w_gate, v_b_gate, v_rpb, v_w_proj_a, v_w_proj_b, v_w_out, v_norm_mlp, v_w_up, v_w_down, v_norm_final):
    w = dict(norm_mix=norm_mix, w_qkv=w_qkv, w_gate=w_gate, b_gate=b_gate, rpb=rpb, w_proj_a=w_proj_a, w_proj_b=w_proj_b,
             w_out=w_out, norm_mlp=norm_mlp, w_up=w_up, w_down=w_down, norm_final=norm_final)
    m = dict(norm_mix=m_norm_mix, w_qkv=m_w_qkv, w_gate=m_w_gate, b_gate=m_b_gate, rpb=m_rpb, w_proj_a=m_w_proj_a,
             w_proj_b=m_w_proj_b, w_out=m_w_out, norm_mlp=m_norm_mlp, w_up=m_w_up, w_down=m_w_down, norm_final=m_norm_final)
    v = dict(norm_mix=v_norm_mix, w_qkv=v_w_qkv, w_gate=v_w_gate, b_gate=v_b_gate, rpb=v_rpb, w_proj_a=v_w_proj_a,
             w_proj_b=v_w_proj_b, w_out=v_w_out, norm_mlp=v_norm_mlp, w_up=v_w_up, w_down=v_w_down, norm_final=v_norm_final)
    order = ("norm_mix", "w_qkv", "w_gate", "b_gate", "rpb", "w_proj_a", "w_proj_b", "w_out", "norm_mlp", "w_up",
             "w_down", "norm_final")

    gathered = _all_gather_weights([w[n][0].astype(BF16) for n in BIG])
    loss, grad_x, big, small = _local_step(x[0], loss_target[0], norm_mix, b_gate, rpb[0], norm_mlp, norm_final, *gathered)
    exchanged = _exchange_grads([big[n] for n in BIG], _pack_small([small[n] for n in SMALL]))

    grad, delta, new_m, new_v = {}, {}, {}, {}
    for n, parts in zip(BIG, exchanged[:-1]):
        res = _adamw("adamw_" + n, parts, w[n][0], m[n][0], v[n][0])
        grad[n], delta[n], new_m[n], new_v[n] = (r[None] for r in res)
    res = _adamw("adamw_small", exchanged[-1], _pack_small([w[n] for n in SMALL]), _pack_small([m[n] for n in SMALL]),
                 _pack_small([v[n] for n in SMALL]))
    shapes = [w[n].shape for n in SMALL]
    for tree, packed in zip((grad, delta, new_m, new_v), res):
        for n, val in zip(SMALL, _unpack_small(packed, shapes)):
            tree[n] = val

    total = lax.psum(loss[0, 0], ("x", "y", "c"))
    return (total, grad_x[None], *[grad[n] for n in order], *[delta[n] for n in order],
            *[new_m[n] for n in order], *[new_v[n] for n in order])
```

```python
import numpy as np
import jax
import jax.numpy as jnp
from jax import lax
from jax.experimental import pallas as pl
from jax.experimental.pallas import tpu as pltpu

F32 = jnp.float32
BF16 = jnp.bfloat16
MESH = pl.DeviceIdType.MESH

D_MODEL = 2048
HEAD_DIM = 128
N_HEADS = 16
N_HEADS_A = 12
N_HEADS_B = 4
DILATIONS = (1, 4, 16)
HALF_WINDOW = 64
GRID_W = 64
NA_ROWS = 8
NA_COLS = 16
D_FF = 4 * D_MODEL
N_DEV = 8
EPS = 1e-6
NEG = -1e30
SCALE = HEAD_DIM ** -0.5
ADAM_LR, ADAM_B1, ADAM_B2, ADAM_EPS, ADAM_WD, ADAM_STEP = 0.001, 0.9, 0.999, 1e-08, 0.01, 10

VMEM_LIMIT = 56 * 1024 * 1024
TM = 512
BQ_A = 128


def _pc(body, *, name, grid, in_specs, out_specs, out_shape, scratch=(), sem=None):
    return pl.pallas_call(
        body, name=name, grid=grid, in_specs=in_specs, out_specs=out_specs, out_shape=out_shape,
        scratch_shapes=list(scratch),
        compiler_params=pltpu.CompilerParams(dimension_semantics=sem, vmem_limit_bytes=VMEM_LIMIT))


_DOT_DIMS = {"nn": (((1,), (0,)), ((), ())), "nt": (((1,), (1,)), ((), ())), "tn": (((0,), (0,)), ((), ()))}


def _dot(a, b, mode, precision=None):
    return lax.dot_general(a, b, _DOT_DIMS[mode], preferred_element_type=F32, precision=precision)


def _mm(name, grid, pairs, extras, outs, epilogue, acc_shape, sem):
    n_pairs, n_extra, n_out = len(pairs), len(extras), len(outs)
    k_axis = len(grid) - 1
    nk = grid[k_axis]

    def body(*refs):
        pids = [pl.program_id(ax) for ax in range(len(grid))]
        ab = refs[:2 * n_pairs]
        ex = refs[2 * n_pairs:2 * n_pairs + n_extra]
        o = refs[2 * n_pairs + n_extra:2 * n_pairs + n_extra + n_out]
        part = None
        for p, pair in enumerate(pairs):
            a = ab[2 * p][...]
            if pair[5] is not None:
                a = pair[5](a)
            d = _dot(a, ab[2 * p + 1][...], pair[4])
            part = d if part is None else part + d
        if nk == 1:
            epilogue(part, ex, o, pids)
        else:
            acc = refs[-1]
            k = pids[k_axis]

            @pl.when(k == 0)
            def _():
                acc[...] = part

            @pl.when(k > 0)
            def _():
                acc[...] += part

            @pl.when(k == nk - 1)
            def _():
                epilogue(acc[...], ex, o, pids)

    operands, in_specs = [], []
    for a, a_spec, b, b_spec, _, _ in pairs:
        operands += [a, b]
        in_specs += [a_spec, b_spec]
    for e, e_spec in extras:
        operands.append(e)
        in_specs.append(e_spec)
    res = _pc(body, name=name, grid=grid, in_specs=in_specs, out_specs=[s for _, s in outs],
              out_shape=[o for o, _ in outs], scratch=[pltpu.VMEM(acc_shape, F32)] if nk > 1 else [], sem=sem)(*operands)
    return res


def _relu2_bf16(u):
    r = jnp.maximum(u.astype(F32), 0.0)
    return (r * r).astype(BF16)


def _rmsnorm_fwd(x, g, name):
    T, D = x.shape

    def body(x_ref, g_ref, h_ref, r_ref):
        xv = x_ref[...]
        r = lax.rsqrt(jnp.mean(xv * xv, axis=-1, keepdims=True) + EPS)
        h_ref[...] = (xv * r * g_ref[...]).astype(BF16)
        r_ref[...] = r

    return _pc(body, name=name, grid=(T // TM,),
               in_specs=[pl.BlockSpec((TM, D), lambda i: (i, 0)), pl.BlockSpec((1, D), lambda i: (0, 0))],
               out_specs=[pl.BlockSpec((TM, D), lambda i: (i, 0)), pl.BlockSpec((TM, 1), lambda i: (i, 0))],
               out_shape=[jax.ShapeDtypeStruct((T, D), BF16), jax.ShapeDtypeStruct((T, 1), F32)],
               sem=("parallel",))(x, g)


def _rmsnorm_bwd_block(dh, x, r, g, dres):
    xh = x * r
    dxh = dh * g
    dx = dres + r * (dxh - xh * jnp.mean(dxh * xh, axis=-1, keepdims=True))
    return dx, jnp.sum(dh * xh, axis=0, keepdims=True)


def _final_norm_loss(x3, g, tgt):
    T, D = x3.shape

    def body(x_ref, g_ref, t_ref, loss_ref, dx_ref, dxb_ref, dg_ref):
        i = pl.program_id(0)
        xv = x_ref[...]
        gv = g_ref[...]
        r = lax.rsqrt(jnp.mean(xv * xv, axis=-1, keepdims=True) + EPS)
        xh = xv * r
        e = xh * gv - t_ref[...]
        part = 0.5 * jnp.sum(jnp.mean(e * e, axis=-1, keepdims=True))
        dy = e * (1.0 / D)
        dxh = dy * gv
        dx = r * (dxh - xh * jnp.mean(dxh * xh, axis=-1, keepdims=True))
        dx_ref[...] = dx
        dxb_ref[...] = dx.astype(BF16)
        dgp = jnp.sum(dy * xh, axis=0, keepdims=True)

        @pl.when(i == 0)
        def _():
            loss_ref[...] = jnp.zeros(loss_ref.shape, F32) + part
            dg_ref[...] = dgp

        @pl.when(i > 0)
        def _():
            loss_ref[...] += part
            dg_ref[...] += dgp

    row = pl.BlockSpec((TM, D), lambda i: (i, 0))
    vec = pl.BlockSpec((1, D), lambda i: (0, 0))
    return _pc(body, name="final_norm_loss", grid=(T // TM,), in_specs=[row, vec, row],
               out_specs=[pl.BlockSpec((8, 128), lambda i: (0, 0)), row, row, vec],
               out_shape=[jax.ShapeDtypeStruct((8, 128), F32), jax.ShapeDtypeStruct((T, D), F32),
                          jax.ShapeDtypeStruct((T, D), BF16), jax.ShapeDtypeStruct((1, D), F32)],
               sem=("arbitrary",))(x3, g, tgt)


def _alibi_slopes():
    return np.asarray(2.0 ** (-8.0 * np.arange(1, N_HEADS_A + 1) / N_HEADS_A), dtype=np.float32)


def _a_windows(T):
    half = [HALF_WINDOW * d for d in DILATIONS]
    return half, [min(BQ_A + 2 * w, T) for w in half]


def _a_scores(q, kw, slope, g, t0, ws, wk, half):
    d = DILATIONS[g]
    s = _dot(q, kw, "nt") * SCALE
    qi = t0 + lax.broadcasted_iota(jnp.int32, (BQ_A, wk), 0)
    ki = ws + lax.broadcasted_iota(jnp.int32, (BQ_A, wk), 1)
    dlt = ki - qi
    ad = jnp.maximum(dlt, -dlt)
    ok = (ad <= half) & ((ad & (d - 1)) == 0)
    return jnp.where(ok, s - slope * ad.astype(F32), NEG)


def _a_qkv_specs(T, inner):
    specs = []
    for g in range(3):
        specs.append(pl.BlockSpec((BQ_A, HEAD_DIM), lambda j, i, g=g: (i, 4 * g + j)))
        specs.append(pl.BlockSpec((T, HEAD_DIM), lambda j, i, g=g: (0, N_HEADS + 4 * g + j)))
        specs.append(pl.BlockSpec((T, HEAD_DIM), lambda j, i, g=g: (0, 2 * N_HEADS + 4 * g + j)))
    return specs


def _attn_a_fwd(qkv, slopes):
    T = qkv.shape[0]
    nq = T // BQ_A
    half, wks = _a_windows(T)

    def body(sl_ref, *refs):
        qkv_refs, (y_ref, lse_ref) = refs[:9], refs[9:]
        j = pl.program_id(0)
        t0 = pl.program_id(1) * BQ_A
        ss, vws = [], []
        for g in range(3):
            q_ref, k_ref, v_ref = qkv_refs[3 * g:3 * g + 3]
            ws = pl.multiple_of(jnp.clip(t0 - half[g], 0, T - wks[g]), 64)
            ss.append(_a_scores(q_ref[...], k_ref[pl.ds(ws, wks[g]), :], sl_ref[4 * g + j], g, t0, ws, wks[g], half[g]))
            vws.append(v_ref[pl.ds(ws, wks[g]), :])
        m = jnp.maximum(jnp.maximum(jnp.max(ss[0], axis=-1, keepdims=True), jnp.max(ss[1], axis=-1, keepdims=True)),
                        jnp.max(ss[2], axis=-1, keepdims=True))
        den = jnp.zeros((BQ_A, 1), F32)
        acc = jnp.zeros((BQ_A, HEAD_DIM), F32)
        for g in range(3):
            p = jnp.exp(ss[g] - m)
            den = den + jnp.sum(p, axis=-1, keepdims=True)
            acc = acc + _dot(p.astype(BF16), vws[g], "nn")
        y_ref[...] = (acc / den).astype(BF16)
        lse_ref[...] = m + jnp.log(den)

    return _pc(body, name="attn_a_fwd", grid=(4, nq),
               in_specs=[pl.BlockSpec(memory_space=pltpu.SMEM)] + _a_qkv_specs(T, nq),
               out_specs=[pl.BlockSpec((BQ_A, HEAD_DIM), lambda j, i: (i, j)),
                          pl.BlockSpec((None, BQ_A, 1), lambda j, i: (j, i, 0))],
               out_shape=[jax.ShapeDtypeStruct((T, 4 * HEAD_DIM), BF16), jax.ShapeDtypeStruct((4, T, 1), F32)],
               sem=("parallel", "parallel"))(slopes, *([qkv] * 9))


def _attn_a_bwd(qkv, slopes, dy, y, lse):
    T = qkv.shape[0]
    nq = T // BQ_A
    half, wks = _a_windows(T)

    def body(sl_ref, *refs):
        qkv_refs = refs[:9]
        dy_ref, y_ref, lse_ref = refs[9:12]
        dq_ref, dk_ref, dv_ref = refs[12:15]
        dk_acc, dv_acc = refs[15:]
        j = pl.program_id(0)
        i = pl.program_id(1)
        t0 = i * BQ_A

        @pl.when(i == 0)
        def _():
            dk_acc[...] = jnp.zeros_like(dk_acc)
            dv_acc[...] = jnp.zeros_like(dv_acc)

        dyv = dy_ref[...]
        delta = jnp.sum(dyv.astype(F32) * y_ref[...].astype(F32), axis=-1, keepdims=True)
        lse_v = lse_ref[...]
        for g in range(3):
            q_ref, k_ref, v_ref = qkv_refs[3 * g:3 * g + 3]
            ws = pl.multiple_of(jnp.clip(t0 - half[g], 0, T - wks[g]), 64)
            win = pl.ds(ws, wks[g])
            qv = q_ref[...]
            kw = k_ref[win, :]
            s = _a_scores(qv, kw, sl_ref[4 * g + j], g, t0, ws, wks[g], half[g])
            p = jnp.exp(s - lse_v)
            dp = _dot(dyv, v_ref[win, :], "nt")
            ds = (p * (dp - delta)).astype(BF16)
            dq_ref[g] = (_dot(ds, kw, "nn") * SCALE).astype(BF16)
            dk_acc[g, win, :] += _dot(ds, qv, "tn") * SCALE
            dv_acc[g, win, :] += _dot(p.astype(BF16), dyv, "tn")

        @pl.when(i == nq - 1)
        def _():
            dk_ref[...] = dk_acc[...].astype(BF16)
            dv_ref[...] = dv_acc[...].astype(BF16)

    blk = pl.BlockSpec((BQ_A, HEAD_DIM), lambda j, i: (i, j))
    whole = pl.BlockSpec((3, T, HEAD_DIM), lambda j, i: (0, 0, j))
    full = jax.ShapeDtypeStruct((3, T, 4 * HEAD_DIM), BF16)
    return _pc(body, name="attn_a_bwd", grid=(4, nq),
               in_specs=[pl.BlockSpec(memory_space=pltpu.SMEM)] + _a_qkv_specs(T, nq)
               + [blk, blk, pl.BlockSpec((None, BQ_A, 1), lambda j, i: (j, i, 0))],
               out_specs=[pl.BlockSpec((3, BQ_A, HEAD_DIM), lambda j, i: (0, i, j)), whole, whole],
               out_shape=[full, full, full],
               scratch=[pltpu.VMEM((3, T, HEAD_DIM), F32), pltpu.VMEM((3, T, HEAD_DIM), F32)],
               sem=("parallel", "arbitrary"))(slopes, *([qkv] * 9), dy, y, lse)


N_DR = 2 * NA_ROWS - 1
N_DC = 2 * NA_COLS - 1
WK_B = NA_ROWS * GRID_W


def _toeplitz_consts():
    col = np.arange(GRID_W)
    start = np.clip(col - NA_COLS // 2, 0, GRID_W - NA_COLS)
    ok = (col[None, :] >= start[:, None]) & (col[None, :] < start[:, None] + NA_COLS)
    dc = np.clip(col[None, :] - col[:, None], -(NA_COLS - 1), NA_COLS - 1) + NA_COLS - 1
    sel = np.zeros((128, GRID_W * GRID_W), np.float32)
    sel[dc.reshape(-1), np.arange(GRID_W * GRID_W)] = ok.reshape(-1).astype(np.float32)
    mask = np.where(ok, 0.0, NEG).astype(np.float32).reshape(1, -1)
    return sel, mask


def _rpb_expand(rpb_rows, sel, mask):
    def body(r_ref, s_ref, m_ref, o_ref):
        o_ref[...] = _dot(r_ref[...], s_ref[...], "nn", precision=lax.Precision.HIGHEST) + m_ref[...]

    n = rpb_rows.shape[0]
    return _pc(body, name="rpb_expand", grid=(1,),
               in_specs=[pl.BlockSpec((n, 128), lambda i: (0, 0)), pl.BlockSpec(sel.shape, lambda i: (0, 0)),
                         pl.BlockSpec(mask.shape, lambda i: (0, 0))],
               out_specs=pl.BlockSpec((n, sel.shape[1]), lambda i: (0, 0)),
               out_shape=jax.ShapeDtypeStruct((n, sel.shape[1]), F32), sem=("arbitrary",))(rpb_rows, sel, mask)


def _rpb_reduce(dtiles, sel):
    def body(t_ref, s_ref, o_ref):
        o_ref[...] = _dot(t_ref[...], s_ref[...], "nt", precision=lax.Precision.HIGHEST)

    n = dtiles.shape[0]
    return _pc(body, name="rpb_reduce", grid=(1,),
               in_specs=[pl.BlockSpec(dtiles.shape, lambda i: (0, 0)), pl.BlockSpec(sel.shape, lambda i: (0, 0))],
               out_specs=pl.BlockSpec((n, 128), lambda i: (0, 0)),
               out_shape=jax.ShapeDtypeStruct((n, 128), F32), sem=("arbitrary",))(dtiles, sel)


def _b_specs(T):
    return [pl.BlockSpec((GRID_W, HEAD_DIM), lambda h, r: (r, N_HEADS_A + h)),
            pl.BlockSpec((T, HEAD_DIM), lambda h, r: (0, N_HEADS + N_HEADS_A + h)),
            pl.BlockSpec((T, HEAD_DIM), lambda h, r: (0, 2 * N_HEADS + N_HEADS_A + h)),
            pl.BlockSpec((None, N_DR, GRID_W, GRID_W), lambda h, r: (h, 0, 0, 0))]


def _b_row_start(r, rows):
    return jnp.clip(r - NA_ROWS // 2, 0, rows - NA_ROWS)


def _attn_b_fwd(qkv, tiles):
    T = qkv.shape[0]
    rows = T // GRID_W

    def body(q_ref, k_ref, v_ref, t_ref, y_ref, lse_ref, bias_ref):
        r = pl.program_id(1)
        rs = _b_row_start(r, rows)
        win = pl.ds(pl.multiple_of(rs * GRID_W, GRID_W), WK_B)
        for a in range(NA_ROWS):
            bias_ref[:, a * GRID_W:(a + 1) * GRID_W] = t_ref[rs + a - r + NA_ROWS - 1]
        s = _dot(q_ref[...], k_ref[win, :], "nt") * SCALE + bias_ref[...]
        m = jnp.max(s, axis=-1, keepdims=True)
        p = jnp.exp(s - m)
        den = jnp.sum(p, axis=-1, keepdims=True)
        y_ref[...] = (_dot(p.astype(BF16), v_ref[win, :], "nn") / den).astype(BF16)
        lse_ref[...] = m + jnp.log(den)

    return _pc(body, name="attn_b_fwd", grid=(N_HEADS_B, rows), in_specs=_b_specs(T),
               out_specs=[pl.BlockSpec((GRID_W, HEAD_DIM), lambda h, r: (r, h)),
                          pl.BlockSpec((None, GRID_W, 1), lambda h, r: (h, r, 0))],
               out_shape=[jax.ShapeDtypeStruct((T, N_HEADS_B * HEAD_DIM), BF16),
                          jax.ShapeDtypeStruct((N_HEADS_B, T, 1), F32)],
               scratch=[pltpu.VMEM((GRID_W, WK_B), F32)], sem=("parallel", "parallel"))(qkv, qkv, qkv, tiles)


def _attn_b_bwd(qkv, tiles, dy, y, lse):
    T = qkv.shape[0]
    rows = T // GRID_W

    def body(q_ref, k_ref, v_ref, t_ref, dy_ref, y_ref, lse_ref, dq_ref, dk_ref, dv_ref, dt_ref,
             bias_ref, ds_ref, dk_acc, dv_acc):
        r = pl.program_id(1)
        rs = _b_row_start(r, rows)
        win = pl.ds(pl.multiple_of(rs * GRID_W, GRID_W), WK_B)

        @pl.when(r == 0)
        def _():
            dk_acc[...] = jnp.zeros_like(dk_acc)
            dv_acc[...] = jnp.zeros_like(dv_acc)
            dt_ref[...] = jnp.zeros_like(dt_ref)

        for a in range(NA_ROWS):
            bias_ref[:, a * GRID_W:(a + 1) * GRID_W] = t_ref[rs + a - r + NA_ROWS - 1]
        qv = q_ref[...]
        kw = k_ref[win, :]
        dyv = dy_ref[...]
        s = _dot(qv, kw, "nt") * SCALE + bias_ref[...]
        p = jnp.exp(s - lse_ref[...])
        delta = jnp.sum(dyv.astype(F32) * y_ref[...].astype(F32), axis=-1, keepdims=True)
        ds = p * (_dot(dyv, v_ref[win, :], "nt") - delta)
        ds_ref[...] = ds
        for a in range(NA_ROWS):
            dt_ref[rs + a - r + NA_ROWS - 1] += ds_ref[:, a * GRID_W:(a + 1) * GRID_W]
        dsb = ds.astype(BF16)
        dq_ref[...] = (_dot(dsb, kw, "nn") * SCALE).astype(BF16)
        dk_acc[win, :] += _dot(dsb, qv, "tn") * SCALE
        dv_acc[win, :] += _dot(p.astype(BF16), dyv, "tn")

        @pl.when(r == rows - 1)
        def _():
            dk_ref[...] = dk_acc[...].astype(BF16)
            dv_ref[...] = dv_acc[...].astype(BF16)

    blk = pl.BlockSpec((GRID_W, HEAD_DIM), lambda h, r: (r, h))
    whole = pl.BlockSpec((T, HEAD_DIM), lambda h, r: (0, h))
    full = jax.ShapeDtypeStruct((T, N_HEADS_B * HEAD_DIM), BF16)
    return _pc(body, name="attn_b_bwd", grid=(N_HEADS_B, rows),
               in_specs=_b_specs(T) + [blk, blk, pl.BlockSpec((None, GRID_W, 1), lambda h, r: (h, r, 0))],
               out_specs=[blk, whole, whole, pl.BlockSpec((None, N_DR, GRID_W, GRID_W), lambda h, r: (h, 0, 0, 0))],
               out_shape=[full, full, full, jax.ShapeDtypeStruct((N_HEADS_B, N_DR, GRID_W, GRID_W), F32)],
               scratch=[pltpu.VMEM((GRID_W, WK_B), F32), pltpu.VMEM((GRID_W, WK_B), F32),
                        pltpu.VMEM((T, HEAD_DIM), F32), pltpu.VMEM((T, HEAD_DIM), F32)],
               sem=("parallel", "arbitrary"))(qkv, qkv, qkv, tiles, dy, y, lse)


def _sds(shape, dtype):
    return jax.ShapeDtypeStruct(shape, dtype)


def _behind(value, token):
    return value if token is None else value + token[0, 0]


def _local_step(x, tgt, norm_mix, b_gate, rpb, norm_mlp, norm_final, wq, wg, late_weights, send):
    T, D = x.shape
    nt = T // TM
    PP, PA = ("parallel", "parallel", "arbitrary"), ("arbitrary", "arbitrary", "arbitrary")
    slopes = jnp.asarray(_alibi_slopes())
    sel_np, mask_np = _toeplitz_consts()
    sel, mask = jnp.asarray(sel_np), jnp.asarray(mask_np)

    def rows(w):
        return pl.BlockSpec((TM, w), lambda i, j, k: (i, j))

    row_all = pl.BlockSpec((TM, D), lambda i, j, k: (i, 0))
    rvec = pl.BlockSpec((TM, 1), lambda i, j, k: (i, 0))
    gvec = pl.BlockSpec((1, D), lambda i, j, k: (0, 0))

    h1, r1 = _rmsnorm_fwd(x, norm_mix, "rmsnorm_mix")

    def store_bf16(acc, ex, o, pids):
        o[0][...] = acc.astype(BF16)

    (qkv,) = _mm("qkv_proj", (nt, N_DEV, 1),
                 [(h1, row_all, wq, pl.BlockSpec((None, D, 768), lambda i, j, k: (j, 0, 0)), "nn", None)],
                 [], [(_sds((T, 3 * D), BF16), rows(768))], store_bf16, None, PP)

    def gate_epi(acc, ex, o, pids):
        o[0][...] = jax.nn.sigmoid(acc + ex[0][...])

    (gates,) = _mm("gate_proj", (nt, N_DEV, 1),
                   [(h1, row_all, wg, pl.BlockSpec((None, D, 512), lambda i, j, k: (j, 0, 0)), "nn", None)],
                   [(b_gate, pl.BlockSpec((1, 512), lambda i, j, k: (0, j)))],
                   [(_sds((T, 2 * D), F32), rows(512))], gate_epi, None, PP)

    rpb_rows = jnp.pad(rpb.reshape(N_HEADS_B * N_DR, N_DC), ((0, 0), (0, 128 - N_DC)))
    tiles = _rpb_expand(rpb_rows, sel, mask).reshape(N_HEADS_B, N_DR, GRID_W, GRID_W)
    ya, lse_a = _attn_a_fwd(qkv, slopes)
    yb, lse_b = _attn_b_fwd(qkv, tiles)

    def merge_body(ya_ref, wa_ref, yb_ref, wb_ref, ga_ref, gb_ref, pa_ref, pb_ref, mg_ref):
        pa = _dot(ya_ref[...], wa_ref[...], "nn")
        pb = _dot(yb_ref[...], wb_ref[...], "nn")
        pa_ref[...] = pa
        pb_ref[...] = pb
        mg_ref[...] = (ga_ref[...] * pa + gb_ref[...] * pb).astype(BF16)

    wpa, wpb, wo = late_weights("proj", yb)
    wo2 = wo.reshape(D, D)
    y_spec = pl.BlockSpec((TM, 512), lambda i, j: (i, 0))
    wp_spec = pl.BlockSpec((None, 512, 256), lambda i, j: (j, 0, 0))
    c_spec = pl.BlockSpec((TM, 256), lambda i, j: (i, j))
    pa, pb, merged = _pc(merge_body, name="merge_proj", grid=(nt, N_DEV),
                         in_specs=[y_spec, wp_spec, y_spec, wp_spec, c_spec,
                                   pl.BlockSpec((TM, 256), lambda i, j: (i, N_DEV + j))],
                         out_specs=[c_spec, c_spec, c_spec],
                         out_shape=[_sds((T, D), F32), _sds((T, D), F32), _sds((T, D), BF16)],
                         sem=("parallel", "parallel"))(ya, wpa, yb, wpb, gates, gates)

    def add_res(acc, ex, o, pids):
        o[0][...] = ex[0][...] + acc

    (x2,) = _mm("out_proj", (nt, 4, 1),
                [(merged, row_all, wo2, pl.BlockSpec((D, 512), lambda i, j, k: (0, j)), "nn", None)],
                [(x, rows(512))], [(_sds((T, D), F32), rows(512))], add_res, None, PP)

    h2, r2 = _rmsnorm_fwd(x2, norm_mlp, "rmsnorm_mlp")
    wu, wd = late_weights("mlp", h2)
    wd2 = wd.reshape(D_FF, D)
    (u,) = _mm("mlp_up", (nt, N_DEV, 1),
               [(h2, row_all, wu, pl.BlockSpec((None, D, 1024), lambda i, j, k: (j, 0, 0)), "nn", None)],
               [], [(_sds((T, D_FF), BF16), rows(1024))], store_bf16, None, PP)
    (x3,) = _mm("mlp_down", (nt, 1, N_DEV),
                [(u, pl.BlockSpec((TM, 1024), lambda i, j, k: (i, k)), wd2,
                  pl.BlockSpec((1024, D), lambda i, j, k: (k, 0)), "nn", _relu2_bf16)],
                [(x2, row_all)], [(_sds((T, D), F32), row_all)], add_res, (TM, D), PP)

    loss, dx3, dx3b, dg3 = _final_norm_loss(x3, norm_final.reshape(1, D), tgt)

    def du_epi(acc, ex, o, pids):
        o[0][...] = (acc * (2.0 * jnp.maximum(ex[0][...].astype(F32), 0.0))).astype(BF16)

    (du,) = _mm("mlp_down_dx", (nt, N_DEV, 1),
                [(dx3b, row_all, wd2, pl.BlockSpec((1024, D), lambda i, j, k: (j, 0)), "nt", None)],
                [(u, rows(1024))], [(_sds((T, D_FF), BF16), rows(1024))], du_epi, None, PP)
    (dwd,) = _mm("mlp_down_dw", (N_DEV, 1, nt),
                 [(u, pl.BlockSpec((TM, 1024), lambda i, j, k: (k, i)), dx3b,
                   pl.BlockSpec((TM, D), lambda i, j, k: (k, 0)), "tn", _relu2_bf16)],
                 [], [(_sds((N_DEV, 1024, D), BF16), pl.BlockSpec((None, 1024, D), lambda i, j, k: (i, 0, 0)))],
                 store_bf16, (1024, D), PP)
    (dwu,) = _mm("mlp_up_dw", (N_DEV, 1, nt),
                 [(h2, pl.BlockSpec((TM, D), lambda i, j, k: (k, 0)), du,
                   pl.BlockSpec((TM, 1024), lambda i, j, k: (k, i)), "tn", None)],
                 [], [(_sds((N_DEV, D, 1024), BF16), pl.BlockSpec((None, D, 1024), lambda i, j, k: (i, 0, 0)))],
                 store_bf16, (D, 1024), PP)

    def norm_bwd_epi(acc, ex, o, pids):
        x_ref, r_ref, g_ref, dres_ref = ex
        dx, dgp = _rmsnorm_bwd_block(acc, x_ref[...], r_ref[...], g_ref[...], dres_ref[...])
        o[0][...] = dx
        if len(o) == 3:
            o[1][...] = dx.astype(BF16)
        first = pids[0] == 0

        @pl.when(first)
        def _():
            o[-1][...] = dgp

        @pl.when(jnp.logical_not(first))
        def _():
            o[-1][...] += dgp

    dx2, dx2b, dg2 = _mm("mlp_up_dx", (nt, 1, N_DEV),
                         [(du, pl.BlockSpec((TM, 1024), lambda i, j, k: (i, k)), wu,
                           pl.BlockSpec((None, D, 1024), lambda i, j, k: (k, 0, 0)), "nt", None)],
                         [(x2, row_all), (r2, rvec), (_behind(norm_mlp, send("mlp", dict(w_down=dwd, w_up=dwu))), gvec),
                          (dx3, row_all)],
                         [(_sds((T, D), F32), row_all), (_sds((T, D), BF16), row_all), (_sds((1, D), F32), gvec)],
                         norm_bwd_epi, (TM, D), PA)

    (dwo,) = _mm("out_proj_dw", (4, 1, nt),
                 [(merged, pl.BlockSpec((TM, 512), lambda i, j, k: (k, i)), dx2b,
                   pl.BlockSpec((TM, D), lambda i, j, k: (k, 0)), "tn", None)],
                 [], [(_sds((D, D), BF16), pl.BlockSpec((512, D), lambda i, j, k: (i, 0)))], store_bf16, (512, D), PP)

    def dmerge_epi(acc, ex, o, pids):
        ga, gb, pa_v, pb_v = (e[...] for e in ex)
        o[0][...] = (acc * ga).astype(BF16)
        o[1][...] = (acc * gb).astype(BF16)
        dga = acc * pa_v * ga * (1.0 - ga)
        dgb = acc * pb_v * gb * (1.0 - gb)
        o[2][...] = dga.astype(BF16)
        o[3][...] = dgb.astype(BF16)
        sa = jnp.sum(dga, axis=0, keepdims=True)
        sb = jnp.sum(dgb, axis=0, keepdims=True)
        first = pids[1] == 0

        @pl.when(first)
        def _():
            o[4][...] = sa
            o[5][...] = sb

        @pl.when(jnp.logical_not(first))
        def _():
            o[4][...] += sa
            o[5][...] += sb

    cj = pl.BlockSpec((TM, 512), lambda j, i, k: (i, j))
    bj = pl.BlockSpec((1, 512), lambda j, i, k: (0, j))
    dpa, dpb, dga, dgb, dba, dbb = _mm(
        "out_proj_dx", (4, nt, 1),
        [(dx2b, pl.BlockSpec((TM, D), lambda j, i, k: (i, 0)), wo2, pl.BlockSpec((512, D), lambda j, i, k: (j, 0)), "nt", None)],
        [(gates, cj), (gates, pl.BlockSpec((TM, 512), lambda j, i, k: (i, 4 + j))), (pa, cj), (pb, cj)],
        [(_sds((T, D), BF16), cj)] * 4 + [(_sds((1, D), F32), bj)] * 2, dmerge_epi, None,
        ("parallel", "arbitrary", "arbitrary"))

    def proj_dw(name, y, dp):
        (dw,) = _mm(name, (N_DEV, 1, nt),
                    [(y, pl.BlockSpec((TM, 512), lambda i, j, k: (k, 0)), dp,
                      pl.BlockSpec((TM, 256), lambda i, j, k: (k, i)), "tn", None)],
                    [], [(_sds((N_DEV, 512, 256), BF16), pl.BlockSpec((None, 512, 256), lambda i, j, k: (i, 0, 0)))],
                    store_bf16, (512, 256), PP)
        return dw

    def proj_dx(name, dp, w):
        (dy,) = _mm(name, (nt, 1, N_DEV),
                    [(dp, pl.BlockSpec((TM, 256), lambda i, j, k: (i, k)), w,
                      pl.BlockSpec((None, 512, 256), lambda i, j, k: (k, 0, 0)), "nt", None)],
                    [], [(_sds((T, 512), BF16), pl.BlockSpec((TM, 512), lambda i, j, k: (i, 0)))],
                    store_bf16, (TM, 512), PP)
        return dy

    dwpa = proj_dw("proj_a_dw", ya, dpa)
    dwpb = proj_dw("proj_b_dw", yb, dpb)
    token = send("proj", dict(w_out=dwo.reshape(N_DEV, D // N_DEV, D), w_proj_a=dwpa, w_proj_b=dwpb))
    dya = proj_dx("proj_a_dx", dpa, wpa)
    dyb = proj_dx("proj_b_dx", dpb, wpb)

    dq_a, dk_a, dv_a = _attn_a_bwd(qkv, _behind(slopes, token), dya, ya, lse_a)
    dq_b, dk_b, dv_b, dtiles = _attn_b_bwd(qkv, tiles, dyb, yb, lse_b)
    drpb = _rpb_reduce(dtiles.reshape(N_HEADS_B * N_DR, GRID_W * GRID_W), sel)[:, :N_DC]
    dqkv = jnp.concatenate([dq_a[0], dq_a[1], dq_a[2], dq_b, dk_a[0], dk_a[1], dk_a[2], dk_b,
                            dv_a[0], dv_a[1], dv_a[2], dv_b], axis=1)
    dgate = jnp.concatenate([dga, dgb], axis=1)

    def in_dw(name, dact, n):
        (dw,) = _mm(name, (N_DEV, 1, nt),
                    [(h1, pl.BlockSpec((TM, D), lambda i, j, k: (k, 0)), dact,
                      pl.BlockSpec((TM, n), lambda i, j, k: (k, i)), "tn", None)],
                    [], [(_sds((N_DEV, D, n), BF16), pl.BlockSpec((None, D, n), lambda i, j, k: (i, 0, 0)))],
                    store_bf16, (D, n), PP)
        return dw

    dwq = in_dw("qkv_proj_dw", dqkv, 768)
    dwg = in_dw("gate_proj_dw", dgate, 512)
    grad_x, dg1 = _mm("in_proj_dx", (nt, 1, N_DEV),
                      [(dqkv, pl.BlockSpec((TM, 768), lambda i, j, k: (i, k)), wq,
                        pl.BlockSpec((None, D, 768), lambda i, j, k: (k, 0, 0)), "nt", None),
                       (dgate, pl.BlockSpec((TM, 512), lambda i, j, k: (i, k)), wg,
                        pl.BlockSpec((None, D, 512), lambda i, j, k: (k, 0, 0)), "nt", None)],
                      [(x, row_all), (r1, rvec), (_behind(norm_mix, send("in", dict(w_qkv=dwq, w_gate=dwg))), gvec),
                       (dx2, row_all)],
                      [(_sds((T, D), F32), row_all), (_sds((1, D), F32), gvec)], norm_bwd_epi, (TM, D), PA)

    small = dict(norm_mix=dg1, b_gate=jnp.concatenate([dba, dbb], axis=1), rpb=drpb, norm_mlp=dg2, norm_final=dg3)
    return loss, grad_x, small


BIG = ("w_qkv", "w_gate", "w_proj_a", "w_proj_b", "w_out", "w_up", "w_down")
SMALL = ("norm_mix", "b_gate", "rpb", "norm_mlp", "norm_final")
SMALL_ROWS = 96


def _place():
    return lax.axis_index("x"), lax.axis_index("y"), lax.axis_index("c")


def _all_gather_weights(shards):
    n_arr = len(shards)

    def body(*refs):
        ins, outs = refs[:n_arr], refs[n_arr:2 * n_arr]
        send_sems, recv_sems, local_sems = refs[2 * n_arr:]
        x, y, c = _place()
        me, sibling = (x, y, c), (x, y, 1 - c)
        chips = [(1 - x, y), (x, 1 - y), (1 - x, 1 - y)]

        def copy(a, k, block, to, from_input=False):
            dst = outs[a].at[4 * block[0] + 2 * block[1] + block[2]]
            return pltpu.make_async_remote_copy(
                src_ref=ins[a] if from_input else dst, dst_ref=dst, send_sem=send_sems.at[7 * a + k],
                recv_sem=recv_sems.at[7 * a + k], device_id=to, device_id_type=MESH)

        mine = [pltpu.make_async_copy(ins[a], outs[a].at[4 * x + 2 * y + c], local_sems.at[a]) for a in range(n_arr)]
        for cp in mine:
            cp.start()
        first = []
        for a in range(n_arr):
            first.append(copy(a, 0, me, sibling, True))
            first += [copy(a, 1 + j, me, (*chip, c), True) for j, chip in enumerate(chips)]
        for cp in first:
            cp.start()
        passed = []
        for j, chip in enumerate(chips):
            for a in range(n_arr):
                copy(a, 1 + j, (*chip, c), me).wait_recv()
                fwd = copy(a, 4 + j, (*chip, c), sibling)
                fwd.start()
                passed.append(fwd)
        for a in range(n_arr):
            copy(a, 0, sibling, me).wait_recv()
            for j, chip in enumerate(chips):
                copy(a, 4 + j, (*chip, 1 - c), me).wait_recv()
        for cp in first + passed:
            cp.wait_send()
        for cp in mine:
            cp.wait()

    any_spec = pl.BlockSpec(memory_space=pl.ANY)
    return pl.pallas_call(
        body, name="all_gather_weights", in_specs=[any_spec] * n_arr, out_specs=[any_spec] * n_arr,
        out_shape=[jax.ShapeDtypeStruct((N_DEV,) + s.shape, s.dtype) for s in shards],
        scratch_shapes=[pltpu.SemaphoreType.DMA((7 * n_arr,)), pltpu.SemaphoreType.DMA((7 * n_arr,)),
                        pltpu.SemaphoreType.DMA((n_arr,))])(*shards)


def _peer(x, y, c, mask):
    return x ^ (mask >> 2), y ^ ((mask >> 1) & 1), c ^ (mask & 1)


def _gather_small(small):
    def body(in_ref, out_ref, send_sems, recv_sems, local_sem):
        x, y, c = _place()
        me = 4 * x + 2 * y + c
        copies = [pltpu.make_async_copy(in_ref, out_ref.at[me], local_sem)]
        for mask in range(1, N_DEV):
            copies.append(pltpu.make_async_remote_copy(
                src_ref=in_ref, dst_ref=out_ref.at[me], send_sem=send_sems.at[mask - 1],
                recv_sem=recv_sems.at[mask - 1], device_id=_peer(x, y, c, mask), device_id_type=MESH))
        for cp in copies:
            cp.start()
        for cp in copies:
            cp.wait()

    any_spec = pl.BlockSpec(memory_space=pl.ANY)
    return pl.pallas_call(
        body, name="gather_small", in_specs=[any_spec], out_specs=any_spec,
        out_shape=jax.ShapeDtypeStruct((N_DEV,) + small.shape, small.dtype),
        scratch_shapes=[pltpu.SemaphoreType.DMA((7,)), pltpu.SemaphoreType.DMA((7,)), pltpu.SemaphoreType.DMA])(small)


_HBM = pl.BlockSpec(memory_space=pltpu.HBM)
_SEM = pl.BlockSpec(memory_space=pltpu.SEMAPHORE)
_EFFECT = pltpu.SideEffectType.DATAFLOW_SIDE_EFFECTING


def _send_copy(src_refs, land_refs, send_sems, recv_sems, a, mask, scatter, x, y, c):
    px, py, pc = _peer(x, y, c, mask)
    src = src_refs[a].at[4 * px + 2 * py + pc] if scatter else src_refs[a]
    return pltpu.make_async_remote_copy(
        src_ref=src, dst_ref=land_refs[a].at[4 * x + 2 * y + c], send_sem=send_sems[a].at[mask - 1],
        recv_sem=recv_sems[a].at[mask - 1], device_id=(px, py, pc), device_id_type=MESH)


def _send_start(name, srcs, scatter, after=None):
    n = len(srcs)
    n_in = 2 * n + (after is not None)
    shapes = [s.shape[1:] if scatter else s.shape for s in srcs]

    def body(*refs):
        src_refs, land_refs = refs[:n], refs[n:2 * n]
        outs = refs[n_in:]
        send_sems, recv_sems = outs[:n], outs[n:2 * n]
        token_ref, local_sems = outs[4 * n], outs[4 * n + 1]
        x, y, c = _place()
        me = 4 * x + 2 * y + c
        local = [pltpu.make_async_copy(src_refs[a].at[me] if scatter else src_refs[a], land_refs[a].at[me], local_sems.at[a])
                 for a in range(n)]
        for cp in local:
            cp.start()
        for a in range(n):
            for mask in range(1, N_DEV):
                _send_copy(src_refs, land_refs, send_sems, recv_sems, a, mask, scatter, x, y, c).start()
        for cp in local:
            cp.wait()
        token_ref[...] = jnp.zeros_like(token_ref)

    lands = [pltpu.with_memory_space_constraint(lax.empty((N_DEV,) + sh, s.dtype), pltpu.HBM) for sh, s in zip(shapes, srcs)]
    srcs = [pltpu.with_memory_space_constraint(s, pltpu.HBM) for s in srcs]
    res = pl.pallas_call(
        body, name=name,
        out_shape=[pltpu.SemaphoreType.DMA((7,))] * (2 * n) + [pltpu.HBM(s.shape, s.dtype) for s in srcs]
        + [pltpu.HBM(l.shape, l.dtype) for l in lands] + [jax.ShapeDtypeStruct((8, 128), F32)],
        in_specs=[_HBM] * (2 * n) + [pl.BlockSpec(memory_space=pl.ANY)] * (n_in - 2 * n),
        out_specs=[_SEM] * (2 * n) + [_HBM] * (2 * n) + [pl.BlockSpec(memory_space=pltpu.VMEM)],
        input_output_aliases={i: 2 * n + i for i in range(2 * n)},
        scratch_shapes=[pltpu.SemaphoreType.DMA((n,))],
        compiler_params=pltpu.CompilerParams(has_side_effects=_EFFECT))(*srcs, *lands, *([] if after is None else [after]))
    handles = [(res[a], res[n + a], res[2 * n + a], res[3 * n + a]) for a in range(n)]
    return handles, res[4 * n]


def _send_wait(name, handles, scatter, after):
    n = len(handles)

    def body(*refs):
        src_refs, land_refs = refs[:n], refs[n:2 * n]
        send_sems, recv_sems = refs[2 * n:3 * n], refs[3 * n:4 * n]
        x, y, c = _place()
        for a in range(n):
            for mask in range(1, N_DEV):
                cp = _send_copy(src_refs, land_refs, send_sems, recv_sems, a, mask, scatter, x, y, c)
                cp.wait_send()
                cp.wait_recv()

    srcs = [h[2] for h in handles]
    lands = [h[3] for h in handles]
    res = pl.pallas_call(
        body, name=name, out_shape=[pltpu.HBM(s.shape, s.dtype) for s in srcs] + [pltpu.HBM(l.shape, l.dtype) for l in lands],
        in_specs=[_HBM] * (2 * n) + [_SEM] * (2 * n) + [pl.BlockSpec(memory_space=pl.ANY)], out_specs=[_HBM] * (2 * n),
        input_output_aliases={i: i for i in range(2 * n)},
        compiler_params=pltpu.CompilerParams(has_side_effects=_EFFECT))(
            *srcs, *lands, *[h[0] for h in handles], *[h[1] for h in handles], after)
    return list(res[n:])


def _adamw(name, parts, w, m, v):
    K, n = w.shape
    tr = min(K, 128)

    def body(p_ref, w_ref, m_ref, v_ref, g_ref, d_ref, nm_ref, nv_ref):
        g = p_ref[0].astype(F32)
        for i in range(1, N_DEV):
            g = g + p_ref[i].astype(F32)
        nm = ADAM_B1 * m_ref[...] + (1.0 - ADAM_B1) * g
        nv = ADAM_B2 * v_ref[...] + (1.0 - ADAM_B2) * jnp.square(g)
        m_hat = nm / (1.0 - ADAM_B1 ** ADAM_STEP)
        v_hat = nv / (1.0 - ADAM_B2 ** ADAM_STEP)
        g_ref[...] = g
        d_ref[...] = -ADAM_LR * (m_hat / (jnp.sqrt(v_hat) + ADAM_EPS) + ADAM_WD * w_ref[...])
        nm_ref[...] = nm
        nv_ref[...] = nv

    blk = pl.BlockSpec((tr, n), lambda i: (i, 0))
    return _pc(body, name=name, grid=(K // tr,), in_specs=[pl.BlockSpec((N_DEV, tr, n), lambda i: (0, i, 0)), blk, blk, blk],
               out_specs=[blk] * 4, out_shape=[jax.ShapeDtypeStruct((K, n), F32)] * 4, sem=("parallel",))(parts, w, m, v)


def _pack_small(vals):
    flat = jnp.concatenate([v.reshape(-1) for v in vals])
    return jnp.pad(flat, (0, SMALL_ROWS * 128 - flat.shape[0])).reshape(SMALL_ROWS, 128)


def _unpack_small(packed, shapes):
    flat, out, off = packed.reshape(-1), [], 0
    for s in shapes:
        size = int(np.prod(s))
        out.append(flat[off:off + size].reshape(s))
        off += size
    return out


def kernel(x, norm_mix, w_qkv, w_gate, b_gate, rpb, w_proj_a, w_proj_b, w_out, norm_mlp, w_up, w_down, norm_final, loss_target, m_norm_mix, m_w_qkv, m_w_gate, m_b_gate, m_rpb, m_w_proj_a, m_w_proj_b, m_w_out, m_norm_mlp, m_w_up, m_w_down, m_norm_final, v_norm_mix, v_w_qkv, v_w_gate, v_b_gate, v_rpb, v_w_proj_a, v_w_proj_b, v_w_out, v_norm_mlp, v_w_up, v_w_down, v_norm_final):
    w = dict(norm_mix=norm_mix, w_qkv=w_qkv, w_gate=w_gate, b_gate=b_gate, rpb=rpb, w_proj_a=w_proj_a, w_proj_b=w_proj_b,
             w_out=w_out, norm_mlp=norm_mlp, w_up=w_up, w_down=w_down, norm_final=norm_final)
    m = dict(norm_mix=m_norm_mix, w_qkv=m_w_qkv, w_gate=m_w_gate, b_gate=m_b_gate, rpb=m_rpb, w_proj_a=m_w_proj_a,
             w_proj_b=m_w_proj_b, w_out=m_w_out, norm_mlp=m_norm_mlp, w_up=m_w_up, w_down=m_w_down, norm_final=m_norm_final)
    v = dict(norm_mix=v_norm_mix, w_qkv=v_w_qkv, w_gate=v_w_gate, b_gate=v_b_gate, rpb=v_rpb, w_proj_a=v_w_proj_a,
             w_proj_b=v_w_proj_b, w_out=v_w_out, norm_mlp=v_norm_mlp, w_up=v_w_up, w_down=v_w_down, norm_final=v_norm_final)
    order = ("norm_mix", "w_qkv", "w_gate", "b_gate", "rpb", "w_proj_a", "w_proj_b", "w_out", "norm_mlp", "w_up",
             "w_down", "norm_final")

    shard = {n: w[n][0].astype(BF16) for n in BIG}
    wq, wg = _all_gather_weights([shard["w_qkv"], shard["w_gate"]])
    late_names = ("w_proj_a", "w_proj_b", "w_out", "w_up", "w_down")
    late, token = _send_start("gather_late_start", [shard[n] for n in late_names], False, wq)
    late = dict(zip(late_names, late))

    def late_weights(stage, after):
        names = {"proj": ("w_proj_a", "w_proj_b", "w_out"), "mlp": ("w_up", "w_down")}[stage]
        return _send_wait("gather_late_wait_" + stage, [late[n] for n in names], False, after)

    sent = {}

    def send(stage, grads):
        handles, tok = _send_start("grads_start_" + stage, list(grads.values()), True)
        sent[stage] = (tuple(grads), handles)
        return tok

    loss, grad_x, small = _local_step(x[0], loss_target[0], _behind(norm_mix, token), b_gate, rpb[0], norm_mlp, norm_final,
                                      wq, wg, late_weights, send)

    grad, delta, new_m, new_v = {}, {}, {}, {}
    after = grad_x
    for stage in ("mlp", "proj", "in"):
        names, handles = sent[stage]
        for n, parts in zip(names, _send_wait("grads_wait_" + stage, handles, True, after)):
            res = _adamw("adamw_" + n, parts, w[n][0], m[n][0], v[n][0])
            grad[n], delta[n], new_m[n], new_v[n] = (r[None] for r in res)
            after = res[1]
    res = _adamw("adamw_small", _gather_small(_pack_small([small[n] for n in SMALL])),
                 _pack_small([w[n] for n in SMALL]), _pack_small([m[n] for n in SMALL]), _pack_small([v[n] for n in SMALL]))
    shapes = [w[n].shape for n in SMALL]
    for tree, packed in zip((grad, delta, new_m, new_v), res):
        for n, val in zip(SMALL, _unpack_small(packed, shapes)):
            tree[n] = val

    total = lax.psum(loss[0, 0], ("x", "y", "c"))
    return (total, grad_x[None], *[grad[n] for n in order], *[delta[n] for n in order],
            *[new_m[n] for n in order], *[new_v[n] for n in order])
```

```python
import numpy as np
import jax
import jax.numpy as jnp
from jax import lax
from jax.experimental import pallas as pl
from jax.experimental.pallas import tpu as pltpu

F32 = jnp.float32
BF16 = jnp.bfloat16
MESH = pl.DeviceIdType.MESH

D_MODEL = 2048
HEAD_DIM = 128
N_HEADS = 16
N_HEADS_A = 12
N_HEADS_B = 4
DILATIONS = (1, 4, 16)
HALF_WINDOW = 64
GRID_W = 64
NA_ROWS = 8
NA_COLS = 16
D_FF = 4 * D_MODEL
N_DEV = 8
EPS = 1e-6
NEG = -1e30
SCALE = HEAD_DIM ** -0.5
ADAM_LR, ADAM_B1, ADAM_B2, ADAM_EPS, ADAM_WD, ADAM_STEP = 0.001, 0.9, 0.999, 1e-08, 0.01, 10

VMEM_LIMIT = 56 * 1024 * 1024
TM = 512
BQ_A = 128


def _pc(body, *, name, grid, in_specs, out_specs, out_shape, scratch=(), sem=None):
    return pl.pallas_call(
        body, name=name, grid=grid, in_specs=in_specs, out_specs=out_specs, out_shape=out_shape,
        scratch_shapes=list(scratch),
        compiler_params=pltpu.CompilerParams(dimension_semantics=sem, vmem_limit_bytes=VMEM_LIMIT))


_DOT_DIMS = {"nn": (((1,), (0,)), ((), ())), "nt": (((1,), (1,)), ((), ())), "tn": (((0,), (0,)), ((), ()))}


def _dot(a, b, mode, precision=None):
    return lax.dot_general(a, b, _DOT_DIMS[mode], preferred_element_type=F32, precision=precision)


def _mm(name, grid, pairs, extras, outs, epilogue, acc_shape, sem):
    n_pairs, n_extra, n_out = len(pairs), len(extras), len(outs)
    k_axis = len(grid) - 1
    nk = grid[k_axis]

    def body(*refs):
        pids = [pl.program_id(ax) for ax in range(len(grid))]
        ab = refs[:2 * n_pairs]
        ex = refs[2 * n_pairs:2 * n_pairs + n_extra]
        o = refs[2 * n_pairs + n_extra:2 * n_pairs + n_extra + n_out]
        part = None
        for p, pair in enumerate(pairs):
            a = ab[2 * p][...]
            if pair[5] is not None:
                a = pair[5](a)
            d = _dot(a, ab[2 * p + 1][...], pair[4])
            part = d if part is None else part + d
        if nk == 1:
            epilogue(part, ex, o, pids)
        else:
            acc = refs[-1]
            k = pids[k_axis]

            @pl.when(k == 0)
            def _():
                acc[...] = part

            @pl.when(k > 0)
            def _():
                acc[...] += part

            @pl.when(k == nk - 1)
            def _():
                epilogue(acc[...], ex, o, pids)

    operands, in_specs = [], []
    for a, a_spec, b, b_spec, _, _ in pairs:
        operands += [a, b]
        in_specs += [a_spec, b_spec]
    for e, e_spec in extras:
        operands.append(e)
        in_specs.append(e_spec)
    res = _pc(body, name=name, grid=grid, in_specs=in_specs, out_specs=[s for _, s in outs],
              out_shape=[o for o, _ in outs], scratch=[pltpu.VMEM(acc_shape, F32)] if nk > 1 else [], sem=sem)(*operands)
    return res


def _relu2_bf16(u):
    r = jnp.maximum(u.astype(F32), 0.0)
    return (r * r).astype(BF16)


def _rmsnorm_fwd(x, g, name):
    T, D = x.shape

    def body(x_ref, g_ref, h_ref, r_ref):
        xv = x_ref[...]
        r = lax.rsqrt(jnp.mean(xv * xv, axis=-1, keepdims=True) + EPS)
        h_ref[...] = (xv * r * g_ref[...]).astype(BF16)
        r_ref[...] = r

    return _pc(body, name=name, grid=(T // TM,),
               in_specs=[pl.BlockSpec((TM, D), lambda i: (i, 0)), pl.BlockSpec((1, D), lambda i: (0, 0))],
               out_specs=[pl.BlockSpec((TM, D), lambda i: (i, 0)), pl.BlockSpec((TM, 1), lambda i: (i, 0))],
               out_shape=[jax.ShapeDtypeStruct((T, D), BF16), jax.ShapeDtypeStruct((T, 1), F32)],
               sem=("parallel",))(x, g)


def _rmsnorm_bwd_block(dh, x, r, g, dres):
    xh = x * r
    dxh = dh * g
    dx = dres + r * (dxh - xh * jnp.mean(dxh * xh, axis=-1, keepdims=True))
    return dx, jnp.sum(dh * xh, axis=0, keepdims=True)


def _final_norm_loss(x3, g, tgt):
    T, D = x3.shape

    def body(x_ref, g_ref, t_ref, loss_ref, dx_ref, dxb_ref, dg_ref):
        i = pl.program_id(0)
        xv = x_ref[...]
        gv = g_ref[...]
        r = lax.rsqrt(jnp.mean(xv * xv, axis=-1, keepdims=True) + EPS)
        xh = xv * r
        e = xh * gv - t_ref[...]
        part = 0.5 * jnp.sum(jnp.mean(e * e, axis=-1, keepdims=True))
        dy = e * (1.0 / D)
        dxh = dy * gv
        dx = r * (dxh - xh * jnp.mean(dxh * xh, axis=-1, keepdims=True))
        dx_ref[...] = dx
        dxb_ref[...] = dx.astype(BF16)
        dgp = jnp.sum(dy * xh, axis=0, keepdims=True)

        @pl.when(i == 0)
        def _():
            loss_ref[...] = jnp.zeros(loss_ref.shape, F32) + part
            dg_ref[...] = dgp

        @pl.when(i > 0)
        def _():
            loss_ref[...] += part
            dg_ref[...] += dgp

    row = pl.BlockSpec((TM, D), lambda i: (i, 0))
    vec = pl.BlockSpec((1, D), lambda i: (0, 0))
    return _pc(body, name="final_norm_loss", grid=(T // TM,), in_specs=[row, vec, row],
               out_specs=[pl.BlockSpec((8, 128), lambda i: (0, 0)), row, row, vec],
               out_shape=[jax.ShapeDtypeStruct((8, 128), F32), jax.ShapeDtypeStruct((T, D), F32),
                          jax.ShapeDtypeStruct((T, D), BF16), jax.ShapeDtypeStruct((1, D), F32)],
               sem=("arbitrary",))(x3, g, tgt)


def _alibi_slopes():
    return np.asarray(2.0 ** (-8.0 * np.arange(1, N_HEADS_A + 1) / N_HEADS_A), dtype=np.float32)


def _a_windows(T):
    half = [HALF_WINDOW * d for d in DILATIONS]
    return half, [min(BQ_A + 2 * w, T) for w in half]


def _a_scores(q, kw, slope, g, t0, ws, wk, half):
    d = DILATIONS[g]
    s = _dot(q, kw, "nt") * SCALE
    qi = t0 + lax.broadcasted_iota(jnp.int32, (BQ_A, wk), 0)
    ki = ws + lax.broadcasted_iota(jnp.int32, (BQ_A, wk), 1)
    dlt = ki - qi
    ad = jnp.maximum(dlt, -dlt)
    ok = (ad <= half) & ((ad & (d - 1)) == 0)
    return jnp.where(ok, s - slope * ad.astype(F32), NEG)


def _a_qkv_specs(T, inner):
    specs = []
    for g in range(3):
        specs.append(pl.BlockSpec((BQ_A, HEAD_DIM), lambda j, i, g=g: (i, 4 * g + j)))
        specs.append(pl.BlockSpec((T, HEAD_DIM), lambda j, i, g=g: (0, N_HEADS + 4 * g + j)))
        specs.append(pl.BlockSpec((T, HEAD_DIM), lambda j, i, g=g: (0, 2 * N_HEADS + 4 * g + j)))
    return specs


def _attn_a_fwd(qkv, slopes):
    T = qkv.shape[0]
    nq = T // BQ_A
    half, wks = _a_windows(T)

    def body(sl_ref, *refs):
        qkv_refs, (y_ref, lse_ref) = refs[:9], refs[9:]
        j = pl.program_id(0)
        t0 = pl.program_id(1) * BQ_A
        ss, vws = [], []
        for g in range(3):
            q_ref, k_ref, v_ref = qkv_refs[3 * g:3 * g + 3]
            ws = pl.multiple_of(jnp.clip(t0 - half[g], 0, T - wks[g]), 64)
            ss.append(_a_scores(q_ref[...], k_ref[pl.ds(ws, wks[g]), :], sl_ref[4 * g + j], g, t0, ws, wks[g], half[g]))
            vws.append(v_ref[pl.ds(ws, wks[g]), :])
        m = jnp.maximum(jnp.maximum(jnp.max(ss[0], axis=-1, keepdims=True), jnp.max(ss[1], axis=-1, keepdims=True)),
                        jnp.max(ss[2], axis=-1, keepdims=True))
        den = jnp.zeros((BQ_A, 1), F32)
        acc = jnp.zeros((BQ_A, HEAD_DIM), F32)
        for g in range(3):
            p = jnp.exp(ss[g] - m)
            den = den + jnp.sum(p, axis=-1, keepdims=True)
            acc = acc + _dot(p.astype(BF16), vws[g], "nn")
        y_ref[...] = (acc / den).astype(BF16)
        lse_ref[...] = m + jnp.log(den)

    return _pc(body, name="attn_a_fwd", grid=(4, nq),
               in_specs=[pl.BlockSpec(memory_space=pltpu.SMEM)] + _a_qkv_specs(T, nq),
               out_specs=[pl.BlockSpec((BQ_A, HEAD_DIM), lambda j, i: (i, j)),
                          pl.BlockSpec((None, BQ_A, 1), lambda j, i: (j, i, 0))],
               out_shape=[jax.ShapeDtypeStruct((T, 4 * HEAD_DIM), BF16), jax.ShapeDtypeStruct((4, T, 1), F32)],
               sem=("parallel", "parallel"))(slopes, *([qkv] * 9))


def _attn_a_bwd(qkv, slopes, dy, y, lse):
    T = qkv.shape[0]
    nq = T // BQ_A
    half, wks = _a_windows(T)

    def body(sl_ref, *refs):
        qkv_refs = refs[:9]
        dy_ref, y_ref, lse_ref = refs[9:12]
        dq_ref, dk_ref, dv_ref = refs[12:15]
        dk_acc, dv_acc = refs[15:]
        j = pl.program_id(0)
        i = pl.program_id(1)
        t0 = i * BQ_A

        @pl.when(i == 0)
        def _():
            dk_acc[...] = jnp.zeros_like(dk_acc)
            dv_acc[...] = jnp.zeros_like(dv_acc)

        dyv = dy_ref[...]
        delta = jnp.sum(dyv.astype(F32) * y_ref[...].astype(F32), axis=-1, keepdims=True)
        lse_v = lse_ref[...]
        for g in range(3):
            q_ref, k_ref, v_ref = qkv_refs[3 * g:3 * g + 3]
            ws = pl.multiple_of(jnp.clip(t0 - half[g], 0, T - wks[g]), 64)
            win = pl.ds(ws, wks[g])
            qv = q_ref[...]
            kw = k_ref[win, :]
            s = _a_scores(qv, kw, sl_ref[4 * g + j], g, t0, ws, wks[g], half[g])
            p = jnp.exp(s - lse_v)
            dp = _dot(dyv, v_ref[win, :], "nt")
            ds = (p * (dp - delta)).astype(BF16)
            dq_ref[g] = (_dot(ds, kw, "nn") * SCALE).astype(BF16)
            dk_acc[g, win, :] += _dot(ds, qv, "tn") * SCALE
            dv_acc[g, win, :] += _dot(p.astype(BF16), dyv, "tn")

        @pl.when(i == nq - 1)
        def _():
            dk_ref[...] = dk_acc[...].astype(BF16)
            dv_ref[...] = dv_acc[...].astype(BF16)

    blk = pl.BlockSpec((BQ_A, HEAD_DIM), lambda j, i: (i, j))
    whole = pl.BlockSpec((3, T, HEAD_DIM), lambda j, i: (0, 0, j))
    full = jax.ShapeDtypeStruct((3, T, 4 * HEAD_DIM), BF16)
    return _pc(body, name="attn_a_bwd", grid=(4, nq),
               in_specs=[pl.BlockSpec(memory_space=pltpu.SMEM)] + _a_qkv_specs(T, nq)
               + [blk, blk, pl.BlockSpec((None, BQ_A, 1), lambda j, i: (j, i, 0))],
               out_specs=[pl.BlockSpec((3, BQ_A, HEAD_DIM), lambda j, i: (0, i, j)), whole, whole],
               out_shape=[full, full, full],
               scratch=[pltpu.VMEM((3, T, HEAD_DIM), F32), pltpu.VMEM((3, T, HEAD_DIM), F32)],
               sem=("parallel", "arbitrary"))(slopes, *([qkv] * 9), dy, y, lse)


N_DR = 2 * NA_ROWS - 1
N_DC = 2 * NA_COLS - 1
WK_B = NA_ROWS * GRID_W


def _toeplitz_consts():
    col = np.arange(GRID_W)
    start = np.clip(col - NA_COLS // 2, 0, GRID_W - NA_COLS)
    ok = (col[None, :] >= start[:, None]) & (col[None, :] < start[:, None] + NA_COLS)
    dc = np.clip(col[None, :] - col[:, None], -(NA_COLS - 1), NA_COLS - 1) + NA_COLS - 1
    sel = np.zeros((128, GRID_W * GRID_W), np.float32)
    sel[dc.reshape(-1), np.arange(GRID_W * GRID_W)] = ok.reshape(-1).astype(np.float32)
    mask = np.where(ok, 0.0, NEG).astype(np.float32).reshape(1, -1)
    return sel, mask


def _rpb_expand(rpb_rows, sel, mask):
    def body(r_ref, s_ref, m_ref, o_ref):
        o_ref[...] = _dot(r_ref[...], s_ref[...], "nn", precision=lax.Precision.HIGHEST) + m_ref[...]

    n = rpb_rows.shape[0]
    return _pc(body, name="rpb_expand", grid=(1,),
               in_specs=[pl.BlockSpec((n, 128), lambda i: (0, 0)), pl.BlockSpec(sel.shape, lambda i: (0, 0)),
                         pl.BlockSpec(mask.shape, lambda i: (0, 0))],
               out_specs=pl.BlockSpec((n, sel.shape[1]), lambda i: (0, 0)),
               out_shape=jax.ShapeDtypeStruct((n, sel.shape[1]), F32), sem=("arbitrary",))(rpb_rows, sel, mask)


def _rpb_reduce(dtiles, sel):
    def body(t_ref, s_ref, o_ref):
        o_ref[...] = _dot(t_ref[...], s_ref[...], "nt", precision=lax.Precision.HIGHEST)

    n = dtiles.shape[0]
    return _pc(body, name="rpb_reduce", grid=(1,),
               in_specs=[pl.BlockSpec(dtiles.shape, lambda i: (0, 0)), pl.BlockSpec(sel.shape, lambda i: (0, 0))],
               out_specs=pl.BlockSpec((n, 128), lambda i: (0, 0)),
               out_shape=jax.ShapeDtypeStruct((n, 128), F32), sem=("arbitrary",))(dtiles, sel)


def _b_specs(T):
    return [pl.BlockSpec((GRID_W, HEAD_DIM), lambda h, r: (r, N_HEADS_A + h)),
            pl.BlockSpec((T, HEAD_DIM), lambda h, r: (0, N_HEADS + N_HEADS_A + h)),
            pl.BlockSpec((T, HEAD_DIM), lambda h, r: (0, 2 * N_HEADS + N_HEADS_A + h)),
            pl.BlockSpec((None, N_DR, GRID_W, GRID_W), lambda h, r: (h, 0, 0, 0))]


def _b_row_start(r, rows):
    return jnp.clip(r - NA_ROWS // 2, 0, rows - NA_ROWS)


def _attn_b_fwd(qkv, tiles):
    T = qkv.shape[0]
    rows = T // GRID_W

    def body(q_ref, k_ref, v_ref, t_ref, y_ref, lse_ref, bias_ref):
        r = pl.program_id(1)
        rs = _b_row_start(r, rows)
        win = pl.ds(pl.multiple_of(rs * GRID_W, GRID_W), WK_B)
        for a in range(NA_ROWS):
            bias_ref[:, a * GRID_W:(a + 1) * GRID_W] = t_ref[rs + a - r + NA_ROWS - 1]
        s = _dot(q_ref[...], k_ref[win, :], "nt") * SCALE + bias_ref[...]
        m = jnp.max(s, axis=-1, keepdims=True)
        p = jnp.exp(s - m)
        den = jnp.sum(p, axis=-1, keepdims=True)
        y_ref[...] = (_dot(p.astype(BF16), v_ref[win, :], "nn") / den).astype(BF16)
        lse_ref[...] = m + jnp.log(den)

    return _pc(body, name="attn_b_fwd", grid=(N_HEADS_B, rows), in_specs=_b_specs(T),
               out_specs=[pl.BlockSpec((GRID_W, HEAD_DIM), lambda h, r: (r, h)),
                          pl.BlockSpec((None, GRID_W, 1), lambda h, r: (h, r, 0))],
               out_shape=[jax.ShapeDtypeStruct((T, N_HEADS_B * HEAD_DIM), BF16),
                          jax.ShapeDtypeStruct((N_HEADS_B, T, 1), F32)],
               scratch=[pltpu.VMEM((GRID_W, WK_B), F32)], sem=("parallel", "parallel"))(qkv, qkv, qkv, tiles)


def _attn_b_bwd(qkv, tiles, dy, y, lse):
    T = qkv.shape[0]
    rows = T // GRID_W

    def body(q_ref, k_ref, v_ref, t_ref, dy_ref, y_ref, lse_ref, dq_ref, dk_ref, dv_ref, dt_ref,
             bias_ref, ds_ref, dk_acc, dv_acc):
        r = pl.program_id(1)
        rs = _b_row_start(r, rows)
        win = pl.ds(pl.multiple_of(rs * GRID_W, GRID_W), WK_B)

        @pl.when(r == 0)
        def _():
            dk_acc[...] = jnp.zeros_like(dk_acc)
            dv_acc[...] = jnp.zeros_like(dv_acc)
            dt_ref[...] = jnp.zeros_like(dt_ref)

        for a in range(NA_ROWS):
            bias_ref[:, a * GRID_W:(a + 1) * GRID_W] = t_ref[rs + a - r + NA_ROWS - 1]
        qv = q_ref[...]
        kw = k_ref[win, :]
        dyv = dy_ref[...]
        s = _dot(qv, kw, "nt") * SCALE + bias_ref[...]
        p = jnp.exp(s - lse_ref[...])
        delta = jnp.sum(dyv.astype(F32) * y_ref[...].astype(F32), axis=-1, keepdims=True)
        ds = p * (_dot(dyv, v_ref[win, :], "nt") - delta)
        ds_ref[...] = ds
        for a in range(NA_ROWS):
            dt_ref[rs + a - r + NA_ROWS - 1] += ds_ref[:, a * GRID_W:(a + 1) * GRID_W]
        dsb = ds.astype(BF16)
        dq_ref[...] = (_dot(dsb, kw, "nn") * SCALE).astype(BF16)
        dk_acc[win, :] += _dot(dsb, qv, "tn") * SCALE
        dv_acc[win, :] += _dot(p.astype(BF16), dyv, "tn")

        @pl.when(r == rows - 1)
        def _():
            dk_ref[...] = dk_acc[...].astype(BF16)
            dv_ref[...] = dv_acc[...].astype(BF16)

    blk = pl.BlockSpec((GRID_W, HEAD_DIM), lambda h, r: (r, h))
    whole = pl.BlockSpec((T, HEAD_DIM), lambda h, r: (0, h))
    full = jax.ShapeDtypeStruct((T, N_HEADS_B * HEAD_DIM), BF16)
    return _pc(body, name="attn_b_bwd", grid=(N_HEADS_B, rows),
               in_specs=_b_specs(T) + [blk, blk, pl.BlockSpec((None, GRID_W, 1), lambda h, r: (h, r, 0))],
               out_specs=[blk, whole, whole, pl.BlockSpec((None, N_DR, GRID_W, GRID_W), lambda h, r: (h, 0, 0, 0))],
               out_shape=[full, full, full, jax.ShapeDtypeStruct((N_HEADS_B, N_DR, GRID_W, GRID_W), F32)],
               scratch=[pltpu.VMEM((GRID_W, WK_B), F32), pltpu.VMEM((GRID_W, WK_B), F32),
                        pltpu.VMEM((T, HEAD_DIM), F32), pltpu.VMEM((T, HEAD_DIM), F32)],
               sem=("parallel", "arbitrary"))(qkv, qkv, qkv, tiles, dy, y, lse)


def _sds(shape, dtype):
    return jax.ShapeDtypeStruct(shape, dtype)


def _behind(value, token):
    return value if token is None else value + token[0, 0]


def _local_step(x, tgt, norm_mix, b_gate, rpb, norm_mlp, norm_final, wq, late_weights, send):
    T, D = x.shape
    nt = T // TM
    PP, PA = ("parallel", "parallel", "arbitrary"), ("arbitrary", "arbitrary", "arbitrary")
    slopes = jnp.asarray(_alibi_slopes())
    sel_np, mask_np = _toeplitz_consts()
    sel, mask = jnp.asarray(sel_np), jnp.asarray(mask_np)

    def rows(w):
        return pl.BlockSpec((TM, w), lambda i, j, k: (i, j))

    row_all = pl.BlockSpec((TM, D), lambda i, j, k: (i, 0))
    rvec = pl.BlockSpec((TM, 1), lambda i, j, k: (i, 0))
    gvec = pl.BlockSpec((1, D), lambda i, j, k: (0, 0))

    h1, r1 = _rmsnorm_fwd(x, norm_mix, "rmsnorm_mix")

    def store_bf16(acc, ex, o, pids):
        o[0][...] = acc.astype(BF16)

    (qkv,) = _mm("qkv_proj", (nt, N_DEV, 1),
                 [(h1, row_all, wq, pl.BlockSpec((None, D, 768), lambda i, j, k: (j, 0, 0)), "nn", None)],
                 [], [(_sds((T, 3 * D), BF16), rows(768))], store_bf16, None, PP)

    def gate_epi(acc, ex, o, pids):
        o[0][...] = jax.nn.sigmoid(acc + ex[0][...])

    (wg,) = late_weights("gate", qkv)

    (gates,) = _mm("gate_proj", (nt, N_DEV, 1),
                   [(h1, row_all, wg, pl.BlockSpec((None, D, 512), lambda i, j, k: (j, 0, 0)), "nn", None)],
                   [(b_gate, pl.BlockSpec((1, 512), lambda i, j, k: (0, j)))],
                   [(_sds((T, 2 * D), F32), rows(512))], gate_epi, None, PP)

    rpb_rows = jnp.pad(rpb.reshape(N_HEADS_B * N_DR, N_DC), ((0, 0), (0, 128 - N_DC)))
    tiles = _rpb_expand(rpb_rows, sel, mask).reshape(N_HEADS_B, N_DR, GRID_W, GRID_W)
    ya, lse_a = _attn_a_fwd(qkv, slopes)
    yb, lse_b = _attn_b_fwd(qkv, tiles)

    def merge_body(ya_ref, wa_ref, yb_ref, wb_ref, ga_ref, gb_ref, pa_ref, pb_ref, mg_ref):
        pa = _dot(ya_ref[...], wa_ref[...], "nn")
        pb = _dot(yb_ref[...], wb_ref[...], "nn")
        pa_ref[...] = pa
        pb_ref[...] = pb
        mg_ref[...] = (ga_ref[...] * pa + gb_ref[...] * pb).astype(BF16)

    wpa, wpb, wo = late_weights("proj", yb)
    wo2 = wo.reshape(D, D)
    y_spec = pl.BlockSpec((TM, 512), lambda i, j: (i, 0))
    wp_spec = pl.BlockSpec((None, 512, 256), lambda i, j: (j, 0, 0))
    c_spec = pl.BlockSpec((TM, 256), lambda i, j: (i, j))
    pa, pb, merged = _pc(merge_body, name="merge_proj", grid=(nt, N_DEV),
                         in_specs=[y_spec, wp_spec, y_spec, wp_spec, c_spec,
                                   pl.BlockSpec((TM, 256), lambda i, j: (i, N_DEV + j))],
                         out_specs=[c_spec, c_spec, c_spec],
                         out_shape=[_sds((T, D), F32), _sds((T, D), F32), _sds((T, D), BF16)],
                         sem=("parallel", "parallel"))(ya, wpa, yb, wpb, gates, gates)

    def add_res(acc, ex, o, pids):
        o[0][...] = ex[0][...] + acc

    (x2,) = _mm("out_proj", (nt, 4, 1),
                [(merged, row_all, wo2, pl.BlockSpec((D, 512), lambda i, j, k: (0, j)), "nn", None)],
                [(x, rows(512))], [(_sds((T, D), F32), rows(512))], add_res, None, PP)

    h2, r2 = _rmsnorm_fwd(x2, norm_mlp, "rmsnorm_mlp")
    wu, wd = late_weights("mlp", h2)
    wd2 = wd.reshape(D_FF, D)
    (u,) = _mm("mlp_up", (nt, N_DEV, 1),
               [(h2, row_all, wu, pl.BlockSpec((None, D, 1024), lambda i, j, k: (j, 0, 0)), "nn", None)],
               [], [(_sds((T, D_FF), BF16), rows(1024))], store_bf16, None, PP)
    (x3,) = _mm("mlp_down", (nt, 1, N_DEV),
                [(u, pl.BlockSpec((TM, 1024), lambda i, j, k: (i, k)), wd2,
                  pl.BlockSpec((1024, D), lambda i, j, k: (k, 0)), "nn", _relu2_bf16)],
                [(x2, row_all)], [(_sds((T, D), F32), row_all)], add_res, (TM, D), PP)

    loss, dx3, dx3b, dg3 = _final_norm_loss(x3, norm_final.reshape(1, D), tgt)

    def du_epi(acc, ex, o, pids):
        o[0][...] = (acc * (2.0 * jnp.maximum(ex[0][...].astype(F32), 0.0))).astype(BF16)

    (du,) = _mm("mlp_down_dx", (nt, N_DEV, 1),
                [(dx3b, row_all, wd2, pl.BlockSpec((1024, D), lambda i, j, k: (j, 0)), "nt", None)],
                [(u, rows(1024))], [(_sds((T, D_FF), BF16), rows(1024))], du_epi, None, PP)
    (dwd,) = _mm("mlp_down_dw", (N_DEV, 1, nt),
                 [(u, pl.BlockSpec((TM, 1024), lambda i, j, k: (k, i)), dx3b,
                   pl.BlockSpec((TM, D), lambda i, j, k: (k, 0)), "tn", _relu2_bf16)],
                 [], [(_sds((N_DEV, 1024, D), BF16), pl.BlockSpec((None, 1024, D), lambda i, j, k: (i, 0, 0)))],
                 store_bf16, (1024, D), PP)
    (dwu,) = _mm("mlp_up_dw", (N_DEV, 1, nt),
                 [(h2, pl.BlockSpec((TM, D), lambda i, j, k: (k, 0)), du,
                   pl.BlockSpec((TM, 1024), lambda i, j, k: (k, i)), "tn", None)],
                 [], [(_sds((N_DEV, D, 1024), BF16), pl.BlockSpec((None, D, 1024), lambda i, j, k: (i, 0, 0)))],
                 store_bf16, (D, 1024), PP)

    def norm_bwd_epi(acc, ex, o, pids):
        x_ref, r_ref, g_ref, dres_ref = ex
        dx, dgp = _rmsnorm_bwd_block(acc, x_ref[...], r_ref[...], g_ref[...], dres_ref[...])
        o[0][...] = dx
        if len(o) == 3:
            o[1][...] = dx.astype(BF16)
        first = pids[0] == 0

        @pl.when(first)
        def _():
            o[-1][...] = dgp

        @pl.when(jnp.logical_not(first))
        def _():
            o[-1][...] += dgp

    dx2, dx2b, dg2 = _mm("mlp_up_dx", (nt, 1, N_DEV),
                         [(du, pl.BlockSpec((TM, 1024), lambda i, j, k: (i, k)), wu,
                           pl.BlockSpec((None, D, 1024), lambda i, j, k: (k, 0, 0)), "nt", None)],
                         [(x2, row_all), (r2, rvec), (_behind(norm_mlp, send("mlp", dict(w_down=dwd, w_up=dwu))), gvec),
                          (dx3, row_all)],
                         [(_sds((T, D), F32), row_all), (_sds((T, D), BF16), row_all), (_sds((1, D), F32), gvec)],
                         norm_bwd_epi, (TM, D), PA)

    (dwo,) = _mm("out_proj_dw", (4, 1, nt),
                 [(merged, pl.BlockSpec((TM, 512), lambda i, j, k: (k, i)), dx2b,
                   pl.BlockSpec((TM, D), lambda i, j, k: (k, 0)), "tn", None)],
                 [], [(_sds((D, D), BF16), pl.BlockSpec((512, D), lambda i, j, k: (i, 0)))], store_bf16, (512, D), PP)

    def dmerge_epi(acc, ex, o, pids):
        ga, gb, pa_v, pb_v = (e[...] for e in ex)
        o[0][...] = (acc * ga).astype(BF16)
        o[1][...] = (acc * gb).astype(BF16)
        dga = acc * pa_v * ga * (1.0 - ga)
        dgb = acc * pb_v * gb * (1.0 - gb)
        o[2][...] = dga.astype(BF16)
        o[3][...] = dgb.astype(BF16)
        sa = jnp.sum(dga, axis=0, keepdims=True)
        sb = jnp.sum(dgb, axis=0, keepdims=True)
        first = pids[1] == 0

        @pl.when(first)
        def _():
            o[4][...] = sa
            o[5][...] = sb

        @pl.when(jnp.logical_not(first))
        def _():
            o[4][...] += sa
            o[5][...] += sb

    cj = pl.BlockSpec((TM, 512), lambda j, i, k: (i, j))
    bj = pl.BlockSpec((1, 512), lambda j, i, k: (0, j))
    dpa, dpb, dga, dgb, dba, dbb = _mm(
        "out_proj_dx", (4, nt, 1),
        [(dx2b, pl.BlockSpec((TM, D), lambda j, i, k: (i, 0)), wo2, pl.BlockSpec((512, D), lambda j, i, k: (j, 0)), "nt", None)],
        [(gates, cj), (gates, pl.BlockSpec((TM, 512), lambda j, i, k: (i, 4 + j))), (pa, cj), (pb, cj)],
        [(_sds((T, D), BF16), cj)] * 4 + [(_sds((1, D), F32), bj)] * 2, dmerge_epi, None,
        ("parallel", "arbitrary", "arbitrary"))

    def proj_dw(name, y, dp):
        (dw,) = _mm(name, (N_DEV, 1, nt),
                    [(y, pl.BlockSpec((TM, 512), lambda i, j, k: (k, 0)), dp,
                      pl.BlockSpec((TM, 256), lambda i, j, k: (k, i)), "tn", None)],
                    [], [(_sds((N_DEV, 512, 256), BF16), pl.BlockSpec((None, 512, 256), lambda i, j, k: (i, 0, 0)))],
                    store_bf16, (512, 256), PP)
        return dw

    def proj_dx(name, dp, w):
        (dy,) = _mm(name, (nt, 1, N_DEV),
                    [(dp, pl.BlockSpec((TM, 256), lambda i, j, k: (i, k)), w,
                      pl.BlockSpec((None, 512, 256), lambda i, j, k: (k, 0, 0)), "nt", None)],
                    [], [(_sds((T, 512), BF16), pl.BlockSpec((TM, 512), lambda i, j, k: (i, 0)))],
                    store_bf16, (TM, 512), PP)
        return dy

    def in_dw(name, dact, n):
        (dw,) = _mm(name, (N_DEV, 1, nt),
                    [(h1, pl.BlockSpec((TM, D), lambda i, j, k: (k, 0)), dact,
                      pl.BlockSpec((TM, n), lambda i, j, k: (k, i)), "tn", None)],
                    [], [(_sds((N_DEV, D, n), BF16), pl.BlockSpec((None, D, n), lambda i, j, k: (i, 0, 0)))],
                    store_bf16, (D, n), PP)
        return dw

    dgate = jnp.concatenate([dga, dgb], axis=1)
    dwg = in_dw("gate_proj_dw", dgate, 512)
    dwpa = proj_dw("proj_a_dw", ya, dpa)
    dwpb = proj_dw("proj_b_dw", yb, dpb)
    token = send("proj", dict(w_gate=dwg, w_out=dwo.reshape(N_DEV, D // N_DEV, D), w_proj_a=dwpa, w_proj_b=dwpb))
    dya = proj_dx("proj_a_dx", dpa, wpa)
    dyb = proj_dx("proj_b_dx", dpb, wpb)

    dq_a, dk_a, dv_a = _attn_a_bwd(qkv, _behind(slopes, token), dya, ya, lse_a)
    dq_b, dk_b, dv_b, dtiles = _attn_b_bwd(qkv, tiles, dyb, yb, lse_b)
    drpb = _rpb_reduce(dtiles.reshape(N_HEADS_B * N_DR, GRID_W * GRID_W), sel)[:, :N_DC]
    dqkv = jnp.concatenate([dq_a[0], dq_a[1], dq_a[2], dq_b, dk_a[0], dk_a[1], dk_a[2], dk_b,
                            dv_a[0], dv_a[1], dv_a[2], dv_b], axis=1)
    dwq = in_dw("qkv_proj_dw", dqkv, 768)
    grad_x, dg1 = _mm("in_proj_dx", (nt, 1, N_DEV),
                      [(dqkv, pl.BlockSpec((TM, 768), lambda i, j, k: (i, k)), wq,
                        pl.BlockSpec((None, D, 768), lambda i, j, k: (k, 0, 0)), "nt", None),
                       (dgate, pl.BlockSpec((TM, 512), lambda i, j, k: (i, k)), wg,
                        pl.BlockSpec((None, D, 512), lambda i, j, k: (k, 0, 0)), "nt", None)],
                      [(x, row_all), (r1, rvec), (_behind(norm_mix, send("in", dict(w_qkv=dwq))), gvec),
                       (dx2, row_all)],
                      [(_sds((T, D), F32), row_all), (_sds((1, D), F32), gvec)], norm_bwd_epi, (TM, D), PA)

    small = dict(norm_mix=dg1, b_gate=jnp.concatenate([dba, dbb], axis=1), rpb=drpb, norm_mlp=dg2, norm_final=dg3)
    return loss, grad_x, small


BIG = ("w_qkv", "w_gate", "w_proj_a", "w_proj_b", "w_out", "w_up", "w_down")
SMALL = ("norm_mix", "b_gate", "rpb", "norm_mlp", "norm_final")
SMALL_ROWS = 96


def _place():
    return lax.axis_index("x"), lax.axis_index("y"), lax.axis_index("c")


def _all_gather_weights(shards):
    n_arr = len(shards)

    def body(*refs):
        ins, outs = refs[:n_arr], refs[n_arr:2 * n_arr]
        send_sems, recv_sems, local_sems = refs[2 * n_arr:]
        x, y, c = _place()
        me, sibling = (x, y, c), (x, y, 1 - c)
        chips = [(1 - x, y), (x, 1 - y), (1 - x, 1 - y)]

        def copy(a, k, block, to, from_input=False):
            dst = outs[a].at[4 * block[0] + 2 * block[1] + block[2]]
            return pltpu.make_async_remote_copy(
                src_ref=ins[a] if from_input else dst, dst_ref=dst, send_sem=send_sems.at[7 * a + k],
                recv_sem=recv_sems.at[7 * a + k], device_id=to, device_id_type=MESH)

        mine = [pltpu.make_async_copy(ins[a], outs[a].at[4 * x + 2 * y + c], local_sems.at[a]) for a in range(n_arr)]
        for cp in mine:
            cp.start()
        first = []
        for a in range(n_arr):
            first.append(copy(a, 0, me, sibling, True))
            first += [copy(a, 1 + j, me, (*chip, c), True) for j, chip in enumerate(chips)]
        for cp in first:
            cp.start()
        passed = []
        for j, chip in enumerate(chips):
            for a in range(n_arr):
                copy(a, 1 + j, (*chip, c), me).wait_recv()
                fwd = copy(a, 4 + j, (*chip, c), sibling)
                fwd.start()
                passed.append(fwd)
        for a in range(n_arr):
            copy(a, 0, sibling, me).wait_recv()
            for j, chip in enumerate(chips):
                copy(a, 4 + j, (*chip, 1 - c), me).wait_recv()
        for cp in first + passed:
            cp.wait_send()
        for cp in mine:
            cp.wait()

    any_spec = pl.BlockSpec(memory_space=pl.ANY)
    return pl.pallas_call(
        body, name="all_gather_weights", in_specs=[any_spec] * n_arr, out_specs=[any_spec] * n_arr,
        out_shape=[jax.ShapeDtypeStruct((N_DEV,) + s.shape, s.dtype) for s in shards],
        scratch_shapes=[pltpu.SemaphoreType.DMA((7 * n_arr,)), pltpu.SemaphoreType.DMA((7 * n_arr,)),
                        pltpu.SemaphoreType.DMA((n_arr,))])(*shards)


def _peer(x, y, c, mask):
    return x ^ (mask >> 2), y ^ ((mask >> 1) & 1), c ^ (mask & 1)


def _gather_small(small):
    def body(in_ref, out_ref, send_sems, recv_sems, local_sem):
        x, y, c = _place()
        me = 4 * x + 2 * y + c
        copies = [pltpu.make_async_copy(in_ref, out_ref.at[me], local_sem)]
        for mask in range(1, N_DEV):
            copies.append(pltpu.make_async_remote_copy(
                src_ref=in_ref, dst_ref=out_ref.at[me], send_sem=send_sems.at[mask - 1],
                recv_sem=recv_sems.at[mask - 1], device_id=_peer(x, y, c, mask), device_id_type=MESH))
        for cp in copies:
            cp.start()
        for cp in copies:
            cp.wait()

    any_spec = pl.BlockSpec(memory_space=pl.ANY)
    return pl.pallas_call(
        body, name="gather_small", in_specs=[any_spec], out_specs=any_spec,
        out_shape=jax.ShapeDtypeStruct((N_DEV,) + small.shape, small.dtype),
        scratch_shapes=[pltpu.SemaphoreType.DMA((7,)), pltpu.SemaphoreType.DMA((7,)), pltpu.SemaphoreType.DMA])(small)


_HBM = pl.BlockSpec(memory_space=pltpu.HBM)
_SEM = pl.BlockSpec(memory_space=pltpu.SEMAPHORE)
_EFFECT = pltpu.SideEffectType.DATAFLOW_SIDE_EFFECTING


def _send_copy(src_refs, land_refs, send_sems, recv_sems, a, mask, scatter, x, y, c):
    px, py, pc = _peer(x, y, c, mask)
    src = src_refs[a].at[4 * px + 2 * py + pc] if scatter else src_refs[a]
    return pltpu.make_async_remote_copy(
        src_ref=src, dst_ref=land_refs[a].at[4 * x + 2 * y + c], send_sem=send_sems[a].at[mask - 1],
        recv_sem=recv_sems[a].at[mask - 1], device_id=(px, py, pc), device_id_type=MESH)


def _send_start(name, srcs, scatter, after=None):
    n = len(srcs)
    n_in = 2 * n + (after is not None)
    shapes = [s.shape[1:] if scatter else s.shape for s in srcs]

    def body(*refs):
        src_refs, land_refs = refs[:n], refs[n:2 * n]
        outs = refs[n_in:]
        send_sems, recv_sems = outs[:n], outs[n:2 * n]
        token_ref = outs[4 * n]
        x, y, c = _place()
        for a in range(n):
            for mask in range(1, N_DEV):
                _send_copy(src_refs, land_refs, send_sems, recv_sems, a, mask, scatter, x, y, c).start()
        token_ref[...] = jnp.zeros_like(token_ref)

    lands = [pltpu.with_memory_space_constraint(lax.empty((N_DEV,) + sh, s.dtype), pltpu.HBM) for sh, s in zip(shapes, srcs)]
    srcs = [pltpu.with_memory_space_constraint(s, pltpu.HBM) for s in srcs]
    res = pl.pallas_call(
        body, name=name,
        out_shape=[pltpu.SemaphoreType.DMA((7,))] * (2 * n) + [pltpu.HBM(s.shape, s.dtype) for s in srcs]
        + [pltpu.HBM(l.shape, l.dtype) for l in lands] + [jax.ShapeDtypeStruct((8, 128), F32)],
        in_specs=[_HBM] * (2 * n) + [pl.BlockSpec(memory_space=pl.ANY)] * (n_in - 2 * n),
        out_specs=[_SEM] * (2 * n) + [_HBM] * (2 * n) + [pl.BlockSpec(memory_space=pltpu.VMEM)],
        input_output_aliases={i: 2 * n + i for i in range(2 * n)},
        compiler_params=pltpu.CompilerParams(has_side_effects=_EFFECT))(*srcs, *lands, *([] if after is None else [after]))
    handles = [(res[a], res[n + a], res[2 * n + a], res[3 * n + a]) for a in range(n)]
    return handles, res[4 * n]


def _send_wait(name, handles, scatter, after):
    n = len(handles)

    def body(*refs):
        src_refs, land_refs = refs[:n], refs[n:2 * n]
        send_sems, recv_sems = refs[2 * n:3 * n], refs[3 * n:4 * n]
        local_sems = refs[-1]
        x, y, c = _place()
        me = 4 * x + 2 * y + c
        local = [pltpu.make_async_copy(src_refs[a].at[me] if scatter else src_refs[a], land_refs[a].at[me], local_sems.at[a])
                 for a in range(n)]
        for cp in local:
            cp.start()
        for a in range(n):
            for mask in range(1, N_DEV):
                cp = _send_copy(src_refs, land_refs, send_sems, recv_sems, a, mask, scatter, x, y, c)
                cp.wait_send()
                cp.wait_recv()
        for cp in local:
            cp.wait()

    srcs = [h[2] for h in handles]
    lands = [h[3] for h in handles]
    res = pl.pallas_call(
        body, name=name, out_shape=[pltpu.HBM(s.shape, s.dtype) for s in srcs] + [pltpu.HBM(l.shape, l.dtype) for l in lands],
        in_specs=[_HBM] * (2 * n) + [_SEM] * (2 * n) + [pl.BlockSpec(memory_space=pl.ANY)], out_specs=[_HBM] * (2 * n),
        input_output_aliases={i: i for i in range(2 * n)},
        scratch_shapes=[pltpu.SemaphoreType.DMA((n,))],
        compiler_params=pltpu.CompilerParams(has_side_effects=_EFFECT))(
            *srcs, *lands, *[h[0] for h in handles], *[h[1] for h in handles], after)
    return list(res[n:])


def _adamw(name, parts, w, m, v):
    K, n = w.shape
    tr = min(K, 128)

    def body(p_ref, w_ref, m_ref, v_ref, g_ref, d_ref, nm_ref, nv_ref):
        g = p_ref[0].astype(F32)
        for i in range(1, N_DEV):
            g = g + p_ref[i].astype(F32)
        nm = ADAM_B1 * m_ref[...] + (1.0 - ADAM_B1) * g
        nv = ADAM_B2 * v_ref[...] + (1.0 - ADAM_B2) * jnp.square(g)
        m_hat = nm / (1.0 - ADAM_B1 ** ADAM_STEP)
        v_hat = nv / (1.0 - ADAM_B2 ** ADAM_STEP)
        g_ref[...] = g
        d_ref[...] = -ADAM_LR * (m_hat / (jnp.sqrt(v_hat) + ADAM_EPS) + ADAM_WD * w_ref[...])
        nm_ref[...] = nm
        nv_ref[...] = nv

    blk = pl.BlockSpec((tr, n), lambda i: (i, 0))
    return _pc(body, name=name, grid=(K // tr,), in_specs=[pl.BlockSpec((N_DEV, tr, n), lambda i: (0, i, 0)), blk, blk, blk],
               out_specs=[blk] * 4, out_shape=[jax.ShapeDtypeStruct((K, n), F32)] * 4, sem=("parallel",))(parts, w, m, v)


def _pack_small(vals):
    flat = jnp.concatenate([v.reshape(-1) for v in vals])
    return jnp.pad(flat, (0, SMALL_ROWS * 128 - flat.shape[0])).reshape(SMALL_ROWS, 128)


def _unpack_small(packed, shapes):
    flat, out, off = packed.reshape(-1), [], 0
    for s in shapes:
        size = int(np.prod(s))
        out.append(flat[off:off + size].reshape(s))
        off += size
    return out


def kernel(x, norm_mix, w_qkv, w_gate, b_gate, rpb, w_proj_a, w_proj_b, w_out, norm_mlp, w_up, w_down, norm_final, loss_target, m_norm_mix, m_w_qkv, m_w_gate, m_b_gate, m_rpb, m_w_proj_a, m_w_proj_b, m_w_out, m_norm_mlp, m_w_up, m_w_down, m_norm_final, v_norm_mix, v_w_qkv, v_w_gate, v_b_gate, v_rpb, v_w_proj_a, v_w_proj_b, v_w_out, v_norm_mlp, v_w_up, v_w_down, v_norm_final):
    w = dict(norm_mix=norm_mix, w_qkv=w_qkv, w_gate=w_gate, b_gate=b_gate, rpb=rpb, w_proj_a=w_proj_a, w_proj_b=w_proj_b,
             w_out=w_out, norm_mlp=norm_mlp, w_up=w_up, w_down=w_down, norm_final=norm_final)
    m = dict(norm_mix=m_norm_mix, w_qkv=m_w_qkv, w_gate=m_w_gate, b_gate=m_b_gate, rpb=m_rpb, w_proj_a=m_w_proj_a,
             w_proj_b=m_w_proj_b, w_out=m_w_out, norm_mlp=m_norm_mlp, w_up=m_w_up, w_down=m_w_down, norm_final=m_norm_final)
    v = dict(norm_mix=v_norm_mix, w_qkv=v_w_qkv, w_gate=v_w_gate, b_gate=v_b_gate, rpb=v_rpb, w_proj_a=v_w_proj_a,
             w_proj_b=v_w_proj_b, w_out=v_w_out, norm_mlp=v_norm_mlp, w_up=v_w_up, w_down=v_w_down, norm_final=v_norm_final)
    order = ("norm_mix", "w_qkv", "w_gate", "b_gate", "rpb", "w_proj_a", "w_proj_b", "w_out", "norm_mlp", "w_up",
             "w_down", "norm_final")

    shard = {n: w[n][0].astype(BF16) for n in BIG}
    (wq,) = _all_gather_weights([shard["w_qkv"]])
    late_names = ("w_gate", "w_proj_a", "w_proj_b", "w_out", "w_up", "w_down")
    late, token = _send_start("gather_late_start", [shard[n] for n in late_names], False, wq)
    late = dict(zip(late_names, late))

    def late_weights(stage, after):
        names = {"gate": ("w_gate",), "proj": ("w_proj_a", "w_proj_b", "w_out"), "mlp": ("w_up", "w_down")}[stage]
        return _send_wait("gather_late_wait_" + stage, [late[n] for n in names], False, after)

    sent = {}

    def send(stage, grads):
        handles, tok = _send_start("grads_start_" + stage, list(grads.values()), True)
        sent[stage] = (tuple(grads), handles)
        return tok

    loss, grad_x, small = _local_step(x[0], loss_target[0], _behind(norm_mix, token), b_gate, rpb[0], norm_mlp, norm_final,
                                      wq, late_weights, send)

    grad, delta, new_m, new_v = {}, {}, {}, {}
    after = grad_x
    for stage in ("mlp", "proj", "in"):
        names, handles = sent[stage]
        for n, parts in zip(names, _send_wait("grads_wait_" + stage, handles, True, after)):
            res = _adamw("adamw_" + n, parts, w[n][0], m[n][0], v[n][0])
            grad[n], delta[n], new_m[n], new_v[n] = (r[None] for r in res)
            after = res[1]
    res = _adamw("adamw_small", _gather_small(_pack_small([small[n] for n in SMALL])),
                 _pack_small([w[n] for n in SMALL]), _pack_small([m[n] for n in SMALL]), _pack_small([v[n] for n in SMALL]))
    shapes = [w[n].shape for n in SMALL]
    for tree, packed in zip((grad, delta, new_m, new_v), res):
        for n, val in zip(SMALL, _unpack_small(packed, shapes)):
            tree[n] = val

    total = lax.psum(loss[0, 0], ("x", "y", "c"))
    return (total, grad_x[None], *[grad[n] for n in order], *[delta[n] for n in order],
            *[new_m[n] for n in order], *[new_v[n] for n in order])
```

```python
import numpy as np
import jax
import jax.numpy as jnp
from jax import lax
from jax.experimental import pallas as pl
from jax.experimental.pallas import tpu as pltpu

F32 = jnp.float32
BF16 = jnp.bfloat16
MESH = pl.DeviceIdType.MESH

D_MODEL = 2048
HEAD_DIM = 128
N_HEADS = 16
N_HEADS_A = 12
N_HEADS_B = 4
DILATIONS = (1, 4, 16)
HALF_WINDOW = 64
GRID_W = 64
NA_ROWS = 8
NA_COLS = 16
D_FF = 4 * D_MODEL
N_DEV = 8
EPS = 1e-6
NEG = -1e30
SCALE = HEAD_DIM ** -0.5
ADAM_LR, ADAM_B1, ADAM_B2, ADAM_EPS, ADAM_WD, ADAM_STEP = 0.001, 0.9, 0.999, 1e-08, 0.01, 10

VMEM_LIMIT = 56 * 1024 * 1024
TM = 512
BQ_A = 128
LOCAL_CHUNKS = 8


def _pc(body, *, name, grid, in_specs, out_specs, out_shape, scratch=(), sem=None):
    return pl.pallas_call(
        body, name=name, grid=grid, in_specs=in_specs, out_specs=out_specs, out_shape=out_shape,
        scratch_shapes=list(scratch),
        compiler_params=pltpu.CompilerParams(dimension_semantics=sem, vmem_limit_bytes=VMEM_LIMIT))


_DOT_DIMS = {"nn": (((1,), (0,)), ((), ())), "nt": (((1,), (1,)), ((), ())), "tn": (((0,), (0,)), ((), ()))}


def _dot(a, b, mode, precision=None):
    return lax.dot_general(a, b, _DOT_DIMS[mode], preferred_element_type=F32, precision=precision)


def _mm(name, grid, pairs, extras, outs, epilogue, acc_shape, sem):
    n_pairs, n_extra, n_out = len(pairs), len(extras), len(outs)
    k_axis = len(grid) - 1
    nk = grid[k_axis]

    def body(*refs):
        pids = [pl.program_id(ax) for ax in range(len(grid))]
        ab = refs[:2 * n_pairs]
        ex = refs[2 * n_pairs:2 * n_pairs + n_extra]
        o = refs[2 * n_pairs + n_extra:2 * n_pairs + n_extra + n_out]
        part = None
        for p, pair in enumerate(pairs):
            a = ab[2 * p][...]
            if pair[5] is not None:
                a = pair[5](a)
            d = _dot(a, ab[2 * p + 1][...], pair[4])
            part = d if part is None else part + d
        if nk == 1:
            epilogue(part, ex, o, pids)
        else:
            acc = refs[-1]
            k = pids[k_axis]

            @pl.when(k == 0)
            def _():
                acc[...] = part

            @pl.when(k > 0)
            def _():
                acc[...] += part

            @pl.when(k == nk - 1)
            def _():
                epilogue(acc[...], ex, o, pids)

    operands, in_specs = [], []
    for a, a_spec, b, b_spec, _, _ in pairs:
        operands += [a, b]
        in_specs += [a_spec, b_spec]
    for e, e_spec in extras:
        operands.append(e)
        in_specs.append(e_spec)
    res = _pc(body, name=name, grid=grid, in_specs=in_specs, out_specs=[s for _, s in outs],
              out_shape=[o for o, _ in outs], scratch=[pltpu.VMEM(acc_shape, F32)] if nk > 1 else [], sem=sem)(*operands)
    return res


def _relu2_bf16(u):
    r = jnp.maximum(u.astype(F32), 0.0)
    return (r * r).astype(BF16)


def _rmsnorm_fwd(x, g, name):
    T, D = x.shape

    def body(x_ref, g_ref, h_ref, r_ref):
        xv = x_ref[...]
        r = lax.rsqrt(jnp.mean(xv * xv, axis=-1, keepdims=True) + EPS)
        h_ref[...] = (xv * r * g_ref[...]).astype(BF16)
        r_ref[...] = r

    return _pc(body, name=name, grid=(T // TM,),
               in_specs=[pl.BlockSpec((TM, D), lambda i: (i, 0)), pl.BlockSpec((1, D), lambda i: (0, 0))],
               out_specs=[pl.BlockSpec((TM, D), lambda i: (i, 0)), pl.BlockSpec((TM, 1), lambda i: (i, 0))],
               out_shape=[jax.ShapeDtypeStruct((T, D), BF16), jax.ShapeDtypeStruct((T, 1), F32)],
               sem=("parallel",))(x, g)


def _rmsnorm_bwd_block(dh, x, r, g, dres):
    xh = x * r
    dxh = dh * g
    dx = dres + r * (dxh - xh * jnp.mean(dxh * xh, axis=-1, keepdims=True))
    return dx, jnp.sum(dh * xh, axis=0, keepdims=True)


def _final_norm_loss(x3, g, tgt):
    T, D = x3.shape

    def body(x_ref, g_ref, t_ref, loss_ref, dx_ref, dxb_ref, dg_ref):
        i = pl.program_id(0)
        xv = x_ref[...]
        gv = g_ref[...]
        r = lax.rsqrt(jnp.mean(xv * xv, axis=-1, keepdims=True) + EPS)
        xh = xv * r
        e = xh * gv - t_ref[...]
        part = 0.5 * jnp.sum(jnp.mean(e * e, axis=-1, keepdims=True))
        dy = e * (1.0 / D)
        dxh = dy * gv
        dx = r * (dxh - xh * jnp.mean(dxh * xh, axis=-1, keepdims=True))
        dx_ref[...] = dx
        dxb_ref[...] = dx.astype(BF16)
        dgp = jnp.sum(dy * xh, axis=0, keepdims=True)

        @pl.when(i == 0)
        def _():
            loss_ref[...] = jnp.zeros(loss_ref.shape, F32) + part
            dg_ref[...] = dgp

        @pl.when(i > 0)
        def _():
            loss_ref[...] += part
            dg_ref[...] += dgp

    row = pl.BlockSpec((TM, D), lambda i: (i, 0))
    vec = pl.BlockSpec((1, D), lambda i: (0, 0))
    return _pc(body, name="final_norm_loss", grid=(T // TM,), in_specs=[row, vec, row],
               out_specs=[pl.BlockSpec((8, 128), lambda i: (0, 0)), row, row, vec],
               out_shape=[jax.ShapeDtypeStruct((8, 128), F32), jax.ShapeDtypeStruct((T, D), F32),
                          jax.ShapeDtypeStruct((T, D), BF16), jax.ShapeDtypeStruct((1, D), F32)],
               sem=("arbitrary",))(x3, g, tgt)


def _alibi_slopes():
    return np.asarray(2.0 ** (-8.0 * np.arange(1, N_HEADS_A + 1) / N_HEADS_A), dtype=np.float32)


def _a_windows(T):
    half = [HALF_WINDOW * d for d in DILATIONS]
    return half, [min(BQ_A + 2 * w, T) for w in half]


def _a_scores(q, kw, slope, g, t0, ws, wk, half):
    d = DILATIONS[g]
    s = _dot(q, kw, "nt") * SCALE
    qi = t0 + lax.broadcasted_iota(jnp.int32, (BQ_A, wk), 0)
    ki = ws + lax.broadcasted_iota(jnp.int32, (BQ_A, wk), 1)
    dlt = ki - qi
    ad = jnp.maximum(dlt, -dlt)
    ok = (ad <= half) & ((ad & (d - 1)) == 0)
    return jnp.where(ok, s - slope * ad.astype(F32), NEG)


def _a_qkv_specs(T, inner):
    specs = []
    for g in range(3):
        specs.append(pl.BlockSpec((BQ_A, HEAD_DIM), lambda j, i, g=g: (i, 4 * g + j)))
        specs.append(pl.BlockSpec((T, HEAD_DIM), lambda j, i, g=g: (0, N_HEADS + 4 * g + j)))
        specs.append(pl.BlockSpec((T, HEAD_DIM), lambda j, i, g=g: (0, 2 * N_HEADS + 4 * g + j)))
    return specs


def _attn_a_fwd(qkv, slopes):
    T = qkv.shape[0]
    nq = T // BQ_A
    half, wks = _a_windows(T)

    def body(sl_ref, *refs):
        qkv_refs, (y_ref, lse_ref) = refs[:9], refs[9:]
        j = pl.program_id(0)
        t0 = pl.program_id(1) * BQ_A
        ss, vws = [], []
        for g in range(3):
            q_ref, k_ref, v_ref = qkv_refs[3 * g:3 * g + 3]
            ws = pl.multiple_of(jnp.clip(t0 - half[g], 0, T - wks[g]), 64)
            ss.append(_a_scores(q_ref[...], k_ref[pl.ds(ws, wks[g]), :], sl_ref[4 * g + j], g, t0, ws, wks[g], half[g]))
            vws.append(v_ref[pl.ds(ws, wks[g]), :])
        m = jnp.maximum(jnp.maximum(jnp.max(ss[0], axis=-1, keepdims=True), jnp.max(ss[1], axis=-1, keepdims=True)),
                        jnp.max(ss[2], axis=-1, keepdims=True))
        den = jnp.zeros((BQ_A, 1), F32)
        acc = jnp.zeros((BQ_A, HEAD_DIM), F32)
        for g in range(3):
            p = jnp.exp(ss[g] - m)
            den = den + jnp.sum(p, axis=-1, keepdims=True)
            acc = acc + _dot(p.astype(BF16), vws[g], "nn")
        y_ref[...] = (acc / den).astype(BF16)
        lse_ref[...] = m + jnp.log(den)

    return _pc(body, name="attn_a_fwd", grid=(4, nq),
               in_specs=[pl.BlockSpec(memory_space=pltpu.SMEM)] + _a_qkv_specs(T, nq),
               out_specs=[pl.BlockSpec((BQ_A, HEAD_DIM), lambda j, i: (i, j)),
                          pl.BlockSpec((None, BQ_A, 1), lambda j, i: (j, i, 0))],
               out_shape=[jax.ShapeDtypeStruct((T, 4 * HEAD_DIM), BF16), jax.ShapeDtypeStruct((4, T, 1), F32)],
               sem=("parallel", "parallel"))(slopes, *([qkv] * 9))


def _attn_a_bwd(qkv, slopes, dy, y, lse):
    T = qkv.shape[0]
    nq = T // BQ_A
    half, wks = _a_windows(T)

    def body(sl_ref, *refs):
        qkv_refs = refs[:9]
        dy_ref, y_ref, lse_ref = refs[9:12]
        dq_ref, dk_ref, dv_ref = refs[12:15]
        dk_acc, dv_acc = refs[15:]
        j = pl.program_id(0)
        i = pl.program_id(1)
        t0 = i * BQ_A

        @pl.when(i == 0)
        def _():
            dk_acc[...] = jnp.zeros_like(dk_acc)
            dv_acc[...] = jnp.zeros_like(dv_acc)

        dyv = dy_ref[...]
        delta = jnp.sum(dyv.astype(F32) * y_ref[...].astype(F32), axis=-1, keepdims=True)
        lse_v = lse_ref[...]
        for g in range(3):
            q_ref, k_ref, v_ref = qkv_refs[3 * g:3 * g + 3]
            ws = pl.multiple_of(jnp.clip(t0 - half[g], 0, T - wks[g]), 64)
            win = pl.ds(ws, wks[g])
            qv = q_ref[...]
            kw = k_ref[win, :]
            s = _a_scores(qv, kw, sl_ref[4 * g + j], g, t0, ws, wks[g], half[g])
            p = jnp.exp(s - lse_v)
            dp = _dot(dyv, v_ref[win, :], "nt")
            ds = (p * (dp - delta)).astype(BF16)
            dq_ref[g] = (_dot(ds, kw, "nn") * SCALE).astype(BF16)
            dk_acc[g, win, :] += _dot(ds, qv, "tn") * SCALE
            dv_acc[g, win, :] += _dot(p.astype(BF16), dyv, "tn")

        @pl.when(i == nq - 1)
        def _():
            dk_ref[...] = dk_acc[...].astype(BF16)
            dv_ref[...] = dv_acc[...].astype(BF16)

    blk = pl.BlockSpec((BQ_A, HEAD_DIM), lambda j, i: (i, j))
    whole = pl.BlockSpec((3, T, HEAD_DIM), lambda j, i: (0, 0, j))
    full = jax.ShapeDtypeStruct((3, T, 4 * HEAD_DIM), BF16)
    return _pc(body, name="attn_a_bwd", grid=(4, nq),
               in_specs=[pl.BlockSpec(memory_space=pltpu.SMEM)] + _a_qkv_specs(T, nq)
               + [blk, blk, pl.BlockSpec((None, BQ_A, 1), lambda j, i: (j, i, 0))],
               out_specs=[pl.BlockSpec((3, BQ_A, HEAD_DIM), lambda j, i: (0, i, j)), whole, whole],
               out_shape=[full, full, full],
               scratch=[pltpu.VMEM((3, T, HEAD_DIM), F32), pltpu.VMEM((3, T, HEAD_DIM), F32)],
               sem=("parallel", "arbitrary"))(slopes, *([qkv] * 9), dy, y, lse)


N_DR = 2 * NA_ROWS - 1
N_DC = 2 * NA_COLS - 1
WK_B = NA_ROWS * GRID_W


def _toeplitz_consts():
    col = np.arange(GRID_W)
    start = np.clip(col - NA_COLS // 2, 0, GRID_W - NA_COLS)
    ok = (col[None, :] >= start[:, None]) & (col[None, :] < start[:, None] + NA_COLS)
    dc = np.clip(col[None, :] - col[:, None], -(NA_COLS - 1), NA_COLS - 1) + NA_COLS - 1
    sel = np.zeros((128, GRID_W * GRID_W), np.float32)
    sel[dc.reshape(-1), np.arange(GRID_W * GRID_W)] = ok.reshape(-1).astype(np.float32)
    mask = np.where(ok, 0.0, NEG).astype(np.float32).reshape(1, -1)
    return sel, mask


def _rpb_expand(rpb_rows, sel, mask):
    def body(r_ref, s_ref, m_ref, o_ref):
        o_ref[...] = _dot(r_ref[...], s_ref[...], "nn", precision=lax.Precision.HIGHEST) + m_ref[...]

    n = rpb_rows.shape[0]
    return _pc(body, name="rpb_expand", grid=(1,),
               in_specs=[pl.BlockSpec((n, 128), lambda i: (0, 0)), pl.BlockSpec(sel.shape, lambda i: (0, 0)),
                         pl.BlockSpec(mask.shape, lambda i: (0, 0))],
               out_specs=pl.BlockSpec((n, sel.shape[1]), lambda i: (0, 0)),
               out_shape=jax.ShapeDtypeStruct((n, sel.shape[1]), F32), sem=("arbitrary",))(rpb_rows, sel, mask)


def _rpb_reduce(dtiles, sel):
    def body(t_ref, s_ref, o_ref):
        o_ref[...] = _dot(t_ref[...], s_ref[...], "nt", precision=lax.Precision.HIGHEST)

    n = dtiles.shape[0]
    return _pc(body, name="rpb_reduce", grid=(1,),
               in_specs=[pl.BlockSpec(dtiles.shape, lambda i: (0, 0)), pl.BlockSpec(sel.shape, lambda i: (0, 0))],
               out_specs=pl.BlockSpec((n, 128), lambda i: (0, 0)),
               out_shape=jax.ShapeDtypeStruct((n, 128), F32), sem=("arbitrary",))(dtiles, sel)


def _b_specs(T):
    return [pl.BlockSpec((GRID_W, HEAD_DIM), lambda h, r: (r, N_HEADS_A + h)),
            pl.BlockSpec((T, HEAD_DIM), lambda h, r: (0, N_HEADS + N_HEADS_A + h)),
            pl.BlockSpec((T, HEAD_DIM), lambda h, r: (0, 2 * N_HEADS + N_HEADS_A + h)),
            pl.BlockSpec((None, N_DR, GRID_W, GRID_W), lambda h, r: (h, 0, 0, 0))]


def _b_row_start(r, rows):
    return jnp.clip(r - NA_ROWS // 2, 0, rows - NA_ROWS)


def _attn_b_fwd(qkv, tiles):
    T = qkv.shape[0]
    rows = T // GRID_W

    def body(q_ref, k_ref, v_ref, t_ref, y_ref, lse_ref, bias_ref):
        r = pl.program_id(1)
        rs = _b_row_start(r, rows)
        win = pl.ds(pl.multiple_of(rs * GRID_W, GRID_W), WK_B)
        for a in range(NA_ROWS):
            bias_ref[:, a * GRID_W:(a + 1) * GRID_W] = t_ref[rs + a - r + NA_ROWS - 1]
        s = _dot(q_ref[...], k_ref[win, :], "nt") * SCALE + bias_ref[...]
        m = jnp.max(s, axis=-1, keepdims=True)
        p = jnp.exp(s - m)
        den = jnp.sum(p, axis=-1, keepdims=True)
        y_ref[...] = (_dot(p.astype(BF16), v_ref[win, :], "nn") / den).astype(BF16)
        lse_ref[...] = m + jnp.log(den)

    return _pc(body, name="attn_b_fwd", grid=(N_HEADS_B, rows), in_specs=_b_specs(T),
               out_specs=[pl.BlockSpec((GRID_W, HEAD_DIM), lambda h, r: (r, h)),
                          pl.BlockSpec((None, GRID_W, 1), lambda h, r: (h, r, 0))],
               out_shape=[jax.ShapeDtypeStruct((T, N_HEADS_B * HEAD_DIM), BF16),
                          jax.ShapeDtypeStruct((N_HEADS_B, T, 1), F32)],
               scratch=[pltpu.VMEM((GRID_W, WK_B), F32)], sem=("parallel", "parallel"))(qkv, qkv, qkv, tiles)


def _attn_b_bwd(qkv, tiles, dy, y, lse):
    T = qkv.shape[0]
    rows = T // GRID_W

    def body(q_ref, k_ref, v_ref, t_ref, dy_ref, y_ref, lse_ref, dq_ref, dk_ref, dv_ref, dt_ref,
             bias_ref, ds_ref, dk_acc, dv_acc):
        r = pl.program_id(1)
        rs = _b_row_start(r, rows)
        win = pl.ds(pl.multiple_of(rs * GRID_W, GRID_W), WK_B)

        @pl.when(r == 0)
        def _():
            dk_acc[...] = jnp.zeros_like(dk_acc)
            dv_acc[...] = jnp.zeros_like(dv_acc)
            dt_ref[...] = jnp.zeros_like(dt_ref)

        for a in range(NA_ROWS):
            bias_ref[:, a * GRID_W:(a + 1) * GRID_W] = t_ref[rs + a - r + NA_ROWS - 1]
        qv = q_ref[...]
        kw = k_ref[win, :]
        dyv = dy_ref[...]
        s = _dot(qv, kw, "nt") * SCALE + bias_ref[...]
        p = jnp.exp(s - lse_ref[...])
        delta = jnp.sum(dyv.astype(F32) * y_ref[...].astype(F32), axis=-1, keepdims=True)
        ds = p * (_dot(dyv, v_ref[win, :], "nt") - delta)
        ds_ref[...] = ds
        for a in range(NA_ROWS):
            dt_ref[rs + a - r + NA_ROWS - 1] += ds_ref[:, a * GRID_W:(a + 1) * GRID_W]
        dsb = ds.astype(BF16)
        dq_ref[...] = (_dot(dsb, kw, "nn") * SCALE).astype(BF16)
        dk_acc[win, :] += _dot(dsb, qv, "tn") * SCALE
        dv_acc[win, :] += _dot(p.astype(BF16), dyv, "tn")

        @pl.when(r == rows - 1)
        def _():
            dk_ref[...] = dk_acc[...].astype(BF16)
            dv_ref[...] = dv_acc[...].astype(BF16)

    blk = pl.BlockSpec((GRID_W, HEAD_DIM), lambda h, r: (r, h))
    whole = pl.BlockSpec((T, HEAD_DIM), lambda h, r: (0, h))
    full = jax.ShapeDtypeStruct((T, N_HEADS_B * HEAD_DIM), BF16)
    return _pc(body, name="attn_b_bwd", grid=(N_HEADS_B, rows),
               in_specs=_b_specs(T) + [blk, blk, pl.BlockSpec((None, GRID_W, 1), lambda h, r: (h, r, 0))],
               out_specs=[blk, whole, whole, pl.BlockSpec((None, N_DR, GRID_W, GRID_W), lambda h, r: (h, 0, 0, 0))],
               out_shape=[full, full, full, jax.ShapeDtypeStruct((N_HEADS_B, N_DR, GRID_W, GRID_W), F32)],
               scratch=[pltpu.VMEM((GRID_W, WK_B), F32), pltpu.VMEM((GRID_W, WK_B), F32),
                        pltpu.VMEM((T, HEAD_DIM), F32), pltpu.VMEM((T, HEAD_DIM), F32)],
               sem=("parallel", "arbitrary"))(qkv, qkv, qkv, tiles, dy, y, lse)


def _sds(shape, dtype):
    return jax.ShapeDtypeStruct(shape, dtype)


def _behind(value, token):
    return value if token is None else value + token[0, 0]


def _local_step(x, tgt, norm_mix, b_gate, rpb, norm_mlp, norm_final, wq, late_weights, send):
    T, D = x.shape
    nt = T // TM
    PP, PA = ("parallel", "parallel", "arbitrary"), ("arbitrary", "arbitrary", "arbitrary")
    slopes = jnp.asarray(_alibi_slopes())
    sel_np, mask_np = _toeplitz_consts()
    sel, mask = jnp.asarray(sel_np), jnp.asarray(mask_np)

    def rows(w):
        return pl.BlockSpec((TM, w), lambda i, j, k: (i, j))

    row_all = pl.BlockSpec((TM, D), lambda i, j, k: (i, 0))
    rvec = pl.BlockSpec((TM, 1), lambda i, j, k: (i, 0))
    gvec = pl.BlockSpec((1, D), lambda i, j, k: (0, 0))

    h1, r1 = _rmsnorm_fwd(x, norm_mix, "rmsnorm_mix")

    def store_bf16(acc, ex, o, pids):
        o[0][...] = acc.astype(BF16)

    (qkv,) = _mm("qkv_proj", (nt, N_DEV, 1),
                 [(h1, row_all, wq, pl.BlockSpec((None, D, 768), lambda i, j, k: (j, 0, 0)), "nn", None)],
                 [], [(_sds((T, 3 * D), BF16), rows(768))], store_bf16, None, PP)

    def gate_epi(acc, ex, o, pids):
        o[0][...] = jax.nn.sigmoid(acc + ex[0][...])

    (wg,) = late_weights("gate", qkv)

    (gates,) = _mm("gate_proj", (nt, N_DEV, 1),
                   [(h1, row_all, wg, pl.BlockSpec((None, D, 512), lambda i, j, k: (j, 0, 0)), "nn", None)],
                   [(b_gate, pl.BlockSpec((1, 512), lambda i, j, k: (0, j)))],
                   [(_sds((T, 2 * D), F32), rows(512))], gate_epi, None, PP)

    rpb_rows = jnp.pad(rpb.reshape(N_HEADS_B * N_DR, N_DC), ((0, 0), (0, 128 - N_DC)))
    tiles = _rpb_expand(rpb_rows, sel, mask).reshape(N_HEADS_B, N_DR, GRID_W, GRID_W)
    ya, lse_a = _attn_a_fwd(qkv, slopes)
    yb, lse_b = _attn_b_fwd(qkv, tiles)

    def merge_body(ya_ref, wa_ref, yb_ref, wb_ref, ga_ref, gb_ref, pa_ref, pb_ref, mg_ref):
        pa = _dot(ya_ref[...], wa_ref[...], "nn")
        pb = _dot(yb_ref[...], wb_ref[...], "nn")
        pa_ref[...] = pa
        pb_ref[...] = pb
        mg_ref[...] = (ga_ref[...] * pa + gb_ref[...] * pb).astype(BF16)

    wpa, wpb, wo = late_weights("proj", yb)
    wo2 = wo.reshape(D, D)
    y_spec = pl.BlockSpec((TM, 512), lambda i, j: (i, 0))
    wp_spec = pl.BlockSpec((None, 512, 256), lambda i, j: (j, 0, 0))
    c_spec = pl.BlockSpec((TM, 256), lambda i, j: (i, j))
    pa, pb, merged = _pc(merge_body, name="merge_proj", grid=(nt, N_DEV),
                         in_specs=[y_spec, wp_spec, y_spec, wp_spec, c_spec,
                                   pl.BlockSpec((TM, 256), lambda i, j: (i, N_DEV + j))],
                         out_specs=[c_spec, c_spec, c_spec],
                         out_shape=[_sds((T, D), F32), _sds((T, D), F32), _sds((T, D), BF16)],
                         sem=("parallel", "parallel"))(ya, wpa, yb, wpb, gates, gates)

    def add_res(acc, ex, o, pids):
        o[0][...] = ex[0][...] + acc

    (x2,) = _mm("out_proj", (nt, 4, 1),
                [(merged, row_all, wo2, pl.BlockSpec((D, 512), lambda i, j, k: (0, j)), "nn", None)],
                [(x, rows(512))], [(_sds((T, D), F32), rows(512))], add_res, None, PP)

    h2, r2 = _rmsnorm_fwd(x2, norm_mlp, "rmsnorm_mlp")
    wu, wd = late_weights("mlp", h2)
    wd2 = wd.reshape(D_FF, D)
    (u,) = _mm("mlp_up", (nt, N_DEV, 1),
               [(h2, row_all, wu, pl.BlockSpec((None, D, 1024), lambda i, j, k: (j, 0, 0)), "nn", None)],
               [], [(_sds((T, D_FF), BF16), rows(1024))], store_bf16, None, PP)
    (x3,) = _mm("mlp_down", (nt, 1, N_DEV),
                [(u, pl.BlockSpec((TM, 1024), lambda i, j, k: (i, k)), wd2,
                  pl.BlockSpec((1024, D), lambda i, j, k: (k, 0)), "nn", _relu2_bf16)],
                [(x2, row_all)], [(_sds((T, D), F32), row_all)], add_res, (TM, D), PP)

    loss, dx3, dx3b, dg3 = _final_norm_loss(x3, norm_final.reshape(1, D), tgt)

    def du_epi(acc, ex, o, pids):
        o[0][...] = (acc * (2.0 * jnp.maximum(ex[0][...].astype(F32), 0.0))).astype(BF16)

    (du,) = _mm("mlp_down_dx", (nt, N_DEV, 1),
                [(dx3b, row_all, wd2, pl.BlockSpec((1024, D), lambda i, j, k: (j, 0)), "nt", None)],
                [(u, rows(1024))], [(_sds((T, D_FF), BF16), rows(1024))], du_epi, None, PP)
    (dwd,) = _mm("mlp_down_dw", (N_DEV, 1, nt),
                 [(u, pl.BlockSpec((TM, 1024), lambda i, j, k: (k, i)), dx3b,
                   pl.BlockSpec((TM, D), lambda i, j, k: (k, 0)), "tn", _relu2_bf16)],
                 [], [(_sds((N_DEV, 1024, D), BF16), pl.BlockSpec((None, 1024, D), lambda i, j, k: (i, 0, 0)))],
                 store_bf16, (1024, D), PP)
    (dwu,) = _mm("mlp_up_dw", (N_DEV, 1, nt),
                 [(h2, pl.BlockSpec((TM, D), lambda i, j, k: (k, 0)), du,
                   pl.BlockSpec((TM, 1024), lambda i, j, k: (k, i)), "tn", None)],
                 [], [(_sds((N_DEV, D, 1024), BF16), pl.BlockSpec((None, D, 1024), lambda i, j, k: (i, 0, 0)))],
                 store_bf16, (D, 1024), PP)

    def norm_bwd_epi(acc, ex, o, pids):
        x_ref, r_ref, g_ref, dres_ref = ex
        dx, dgp = _rmsnorm_bwd_block(acc, x_ref[...], r_ref[...], g_ref[...], dres_ref[...])
        o[0][...] = dx
        if len(o) == 3:
            o[1][...] = dx.astype(BF16)
        first = pids[0] == 0

        @pl.when(first)
        def _():
            o[-1][...] = dgp

        @pl.when(jnp.logical_not(first))
        def _():
            o[-1][...] += dgp

    dx2, dx2b, dg2 = _mm("mlp_up_dx", (nt, 1, N_DEV),
                         [(du, pl.BlockSpec((TM, 1024), lambda i, j, k: (i, k)), wu,
                           pl.BlockSpec((None, D, 1024), lambda i, j, k: (k, 0, 0)), "nt", None)],
                         [(x2, row_all), (r2, rvec), (_behind(norm_mlp, send("mlp", dict(w_down=dwd, w_up=dwu))), gvec),
                          (dx3, row_all)],
                         [(_sds((T, D), F32), row_all), (_sds((T, D), BF16), row_all), (_sds((1, D), F32), gvec)],
                         norm_bwd_epi, (TM, D), PA)

    (dwo,) = _mm("out_proj_dw", (4, 1, nt),
                 [(merged, pl.BlockSpec((TM, 512), lambda i, j, k: (k, i)), dx2b,
                   pl.BlockSpec((TM, D), lambda i, j, k: (k, 0)), "tn", None)],
                 [], [(_sds((D, D), BF16), pl.BlockSpec((512, D), lambda i, j, k: (i, 0)))], store_bf16, (512, D), PP)

    def dmerge_epi(acc, ex, o, pids):
        ga, gb, pa_v, pb_v = (e[...] for e in ex)
        o[0][...] = (acc * ga).astype(BF16)
        o[1][...] = (acc * gb).astype(BF16)
        dga = acc * pa_v * ga * (1.0 - ga)
        dgb = acc * pb_v * gb * (1.0 - gb)
        o[2][...] = dga.astype(BF16)
        o[3][...] = dgb.astype(BF16)
        sa = jnp.sum(dga, axis=0, keepdims=True)
        sb = jnp.sum(dgb, axis=0, keepdims=True)
        first = pids[1] == 0

        @pl.when(first)
        def _():
            o[4][...] = sa
            o[5][...] = sb

        @pl.when(jnp.logical_not(first))
        def _():
            o[4][...] += sa
            o[5][...] += sb

    cj = pl.BlockSpec((TM, 512), lambda j, i, k: (i, j))
    bj = pl.BlockSpec((1, 512), lambda j, i, k: (0, j))
    dpa, dpb, dga, dgb, dba, dbb = _mm(
        "out_proj_dx", (4, nt, 1),
        [(dx2b, pl.BlockSpec((TM, D), lambda j, i, k: (i, 0)), wo2, pl.BlockSpec((512, D), lambda j, i, k: (j, 0)), "nt", None)],
        [(gates, cj), (gates, pl.BlockSpec((TM, 512), lambda j, i, k: (i, 4 + j))), (pa, cj), (pb, cj)],
        [(_sds((T, D), BF16), cj)] * 4 + [(_sds((1, D), F32), bj)] * 2, dmerge_epi, None,
        ("parallel", "arbitrary", "arbitrary"))

    def proj_dw(name, y, dp):
        (dw,) = _mm(name, (N_DEV, 1, nt),
                    [(y, pl.BlockSpec((TM, 512), lambda i, j, k: (k, 0)), dp,
                      pl.BlockSpec((TM, 256), lambda i, j, k: (k, i)), "tn", None)],
                    [], [(_sds((N_DEV, 512, 256), BF16), pl.BlockSpec((None, 512, 256), lambda i, j, k: (i, 0, 0)))],
                    store_bf16, (512, 256), PP)
        return dw

    def proj_dx(name, dp, w):
        (dy,) = _mm(name, (nt, 1, N_DEV),
                    [(dp, pl.BlockSpec((TM, 256), lambda i, j, k: (i, k)), w,
                      pl.BlockSpec((None, 512, 256), lambda i, j, k: (k, 0, 0)), "nt", None)],
                    [], [(_sds((T, 512), BF16), pl.BlockSpec((TM, 512), lambda i, j, k: (i, 0)))],
                    store_bf16, (TM, 512), PP)
        return dy

    def in_dw(name, dact, n):
        (dw,) = _mm(name, (N_DEV, 1, nt),
                    [(h1, pl.BlockSpec((TM, D), lambda i, j, k: (k, 0)), dact,
                      pl.BlockSpec((TM, n), lambda i, j, k: (k, i)), "tn", None)],
                    [], [(_sds((N_DEV, D, n), BF16), pl.BlockSpec((None, D, n), lambda i, j, k: (i, 0, 0)))],
                    store_bf16, (D, n), PP)
        return dw

    dgate = jnp.concatenate([dga, dgb], axis=1)
    dwg = in_dw("gate_proj_dw", dgate, 512)
    dwpa = proj_dw("proj_a_dw", ya, dpa)
    dwpb = proj_dw("proj_b_dw", yb, dpb)
    token = send("proj", dict(w_gate=dwg, w_out=dwo.reshape(N_DEV, D // N_DEV, D), w_proj_a=dwpa, w_proj_b=dwpb))
    dya = proj_dx("proj_a_dx", dpa, wpa)
    dyb = proj_dx("proj_b_dx", dpb, wpb)

    dq_a, dk_a, dv_a = _attn_a_bwd(qkv, _behind(slopes, token), dya, ya, lse_a)
    dq_b, dk_b, dv_b, dtiles = _attn_b_bwd(qkv, tiles, dyb, yb, lse_b)
    drpb = _rpb_reduce(dtiles.reshape(N_HEADS_B * N_DR, GRID_W * GRID_W), sel)[:, :N_DC]
    dqkv = jnp.concatenate([dq_a[0], dq_a[1], dq_a[2], dq_b, dk_a[0], dk_a[1], dk_a[2], dk_b,
                            dv_a[0], dv_a[1], dv_a[2], dv_b], axis=1)
    dwq = in_dw("qkv_proj_dw", dqkv, 768)
    grad_x, dg1 = _mm("in_proj_dx", (nt, 1, N_DEV),
                      [(dqkv, pl.BlockSpec((TM, 768), lambda i, j, k: (i, k)), wq,
                        pl.BlockSpec((None, D, 768), lambda i, j, k: (k, 0, 0)), "nt", None),
                       (dgate, pl.BlockSpec((TM, 512), lambda i, j, k: (i, k)), wg,
                        pl.BlockSpec((None, D, 512), lambda i, j, k: (k, 0, 0)), "nt", None)],
                      [(x, row_all), (r1, rvec), (_behind(norm_mix, send("in", dict(w_qkv=dwq))), gvec),
                       (dx2, row_all)],
                      [(_sds((T, D), F32), row_all), (_sds((1, D), F32), gvec)], norm_bwd_epi, (TM, D), PA)

    small = dict(norm_mix=dg1, b_gate=jnp.concatenate([dba, dbb], axis=1), rpb=drpb, norm_mlp=dg2, norm_final=dg3)
    return loss, grad_x, small


BIG = ("w_qkv", "w_gate", "w_proj_a", "w_proj_b", "w_out", "w_up", "w_down")
SMALL = ("norm_mix", "b_gate", "rpb", "norm_mlp", "norm_final")
SMALL_ROWS = 96


def _place():
    return lax.axis_index("x"), lax.axis_index("y"), lax.axis_index("c")


def _all_gather_weights(shards):
    n_arr = len(shards)

    def body(*refs):
        ins, outs = refs[:n_arr], refs[n_arr:2 * n_arr]
        send_sems, recv_sems, local_sems = refs[2 * n_arr:]
        x, y, c = _place()
        me, sibling = (x, y, c), (x, y, 1 - c)
        chips = [(1 - x, y), (x, 1 - y), (1 - x, 1 - y)]

        def copy(a, k, block, to, from_input=False):
            dst = outs[a].at[4 * block[0] + 2 * block[1] + block[2]]
            return pltpu.make_async_remote_copy(
                src_ref=ins[a] if from_input else dst, dst_ref=dst, send_sem=send_sems.at[7 * a + k],
                recv_sem=recv_sems.at[7 * a + k], device_id=to, device_id_type=MESH)

        mine = [pltpu.make_async_copy(ins[a], outs[a].at[4 * x + 2 * y + c], local_sems.at[a]) for a in range(n_arr)]
        for a in range(n_arr):
            rows = ins[a].shape[0] // LOCAL_CHUNKS
            for k in range(LOCAL_CHUNKS):
                pltpu.make_async_copy(ins[a].at[pl.ds(k * rows, rows)],
                                      outs[a].at[4 * x + 2 * y + c, pl.ds(k * rows, rows)], local_sems.at[a]).start()
        first = []
        for a in range(n_arr):
            first.append(copy(a, 0, me, sibling, True))
            first += [copy(a, 1 + j, me, (*chip, c), True) for j, chip in enumerate(chips)]
        for cp in first:
            cp.start()
        passed = []
        for j, chip in enumerate(chips):
            for a in range(n_arr):
                copy(a, 1 + j, (*chip, c), me).wait_recv()
                fwd = copy(a, 4 + j, (*chip, c), sibling)
                fwd.start()
                passed.append(fwd)
        for a in range(n_arr):
            copy(a, 0, sibling, me).wait_recv()
            for j, chip in enumerate(chips):
                copy(a, 4 + j, (*chip, 1 - c), me).wait_recv()
        for cp in first + passed:
            cp.wait_send()
        for cp in mine:
            cp.wait()

    any_spec = pl.BlockSpec(memory_space=pl.ANY)
    return pl.pallas_call(
        body, name="all_gather_weights", in_specs=[any_spec] * n_arr, out_specs=[any_spec] * n_arr,
        out_shape=[jax.ShapeDtypeStruct((N_DEV,) + s.shape, s.dtype) for s in shards],
        scratch_shapes=[pltpu.SemaphoreType.DMA((7 * n_arr,)), pltpu.SemaphoreType.DMA((7 * n_arr,)),
                        pltpu.SemaphoreType.DMA((n_arr,))])(*shards)


def _peer(x, y, c, mask):
    return x ^ (mask >> 2), y ^ ((mask >> 1) & 1), c ^ (mask & 1)


def _gather_small(small):
    def body(in_ref, out_ref, send_sems, recv_sems, local_sem):
        x, y, c = _place()
        me = 4 * x + 2 * y + c
        copies = [pltpu.make_async_copy(in_ref, out_ref.at[me], local_sem)]
        for mask in range(1, N_DEV):
            copies.append(pltpu.make_async_remote_copy(
                src_ref=in_ref, dst_ref=out_ref.at[me], send_sem=send_sems.at[mask - 1],
                recv_sem=recv_sems.at[mask - 1], device_id=_peer(x, y, c, mask), device_id_type=MESH))
        for cp in copies:
            cp.start()
        for cp in copies:
            cp.wait()

    any_spec = pl.BlockSpec(memory_space=pl.ANY)
    return pl.pallas_call(
        body, name="gather_small", in_specs=[any_spec], out_specs=any_spec,
        out_shape=jax.ShapeDtypeStruct((N_DEV,) + small.shape, small.dtype),
        scratch_shapes=[pltpu.SemaphoreType.DMA((7,)), pltpu.SemaphoreType.DMA((7,)), pltpu.SemaphoreType.DMA])(small)


_HBM = pl.BlockSpec(memory_space=pltpu.HBM)
_SEM = pl.BlockSpec(memory_space=pltpu.SEMAPHORE)
_EFFECT = pltpu.SideEffectType.DATAFLOW_SIDE_EFFECTING


def _send_copy(src_refs, land_refs, send_sems, recv_sems, a, mask, scatter, x, y, c):
    px, py, pc = _peer(x, y, c, mask)
    src = src_refs[a].at[4 * px + 2 * py + pc] if scatter else src_refs[a]
    return pltpu.make_async_remote_copy(
        src_ref=src, dst_ref=land_refs[a].at[4 * x + 2 * y + c], send_sem=send_sems[a].at[mask - 1],
        recv_sem=recv_sems[a].at[mask - 1], device_id=(px, py, pc), device_id_type=MESH)


def _local_copy(src_refs, land_refs, local_sems, a, scatter, x, y, c):
    me = 4 * x + 2 * y + c
    return pltpu.make_async_copy(src_refs[a].at[me] if scatter else src_refs[a], land_refs[a].at[me], local_sems[a].at[0])


def _send_start(name, srcs, scatter, after=None):
    n = len(srcs)
    n_in = 2 * n + (after is not None)
    shapes = [s.shape[1:] if scatter else s.shape for s in srcs]

    def body(*refs):
        src_refs, land_refs = refs[:n], refs[n:2 * n]
        outs = refs[n_in:]
        send_sems, recv_sems, local_sems = outs[:n], outs[n:2 * n], outs[2 * n:3 * n]
        token_ref = outs[5 * n]
        x, y, c = _place()
        for a in range(n):
            for mask in range(1, N_DEV):
                _send_copy(src_refs, land_refs, send_sems, recv_sems, a, mask, scatter, x, y, c).start()
        for a in range(n):
            _local_copy(src_refs, land_refs, local_sems, a, scatter, x, y, c).start()
        token_ref[...] = jnp.zeros_like(token_ref)

    lands = [pltpu.with_memory_space_constraint(lax.empty((N_DEV,) + sh, s.dtype), pltpu.HBM) for sh, s in zip(shapes, srcs)]
    srcs = [pltpu.with_memory_space_constraint(s, pltpu.HBM) for s in srcs]
    res = pl.pallas_call(
        body, name=name,
        out_shape=[pltpu.SemaphoreType.DMA((7,))] * (2 * n) + [pltpu.SemaphoreType.DMA((1,))] * n
        + [pltpu.HBM(s.shape, s.dtype) for s in srcs] + [pltpu.HBM(l.shape, l.dtype) for l in lands]
        + [jax.ShapeDtypeStruct((8, 128), F32)],
        in_specs=[_HBM] * (2 * n) + [pl.BlockSpec(memory_space=pl.ANY)] * (n_in - 2 * n),
        out_specs=[_SEM] * (3 * n) + [_HBM] * (2 * n) + [pl.BlockSpec(memory_space=pltpu.VMEM)],
        input_output_aliases={i: 3 * n + i for i in range(2 * n)},
        compiler_params=pltpu.CompilerParams(has_side_effects=_EFFECT))(*srcs, *lands, *([] if after is None else [after]))
    handles = [tuple(res[k * n + a] for k in range(5)) for a in range(n)]
    return handles, res[5 * n]


def _send_wait(name, handles, scatter, after):
    n = len(handles)

    def body(*refs):
        src_refs, land_refs = refs[:n], refs[n:2 * n]
        send_sems, recv_sems, local_sems = refs[2 * n:3 * n], refs[3 * n:4 * n], refs[4 * n:5 * n]
        x, y, c = _place()
        for a in range(n):
            for mask in range(1, N_DEV):
                cp = _send_copy(src_refs, land_refs, send_sems, recv_sems, a, mask, scatter, x, y, c)
                cp.wait_send()
                cp.wait_recv()
            _local_copy(src_refs, land_refs, local_sems, a, scatter, x, y, c).wait()

    srcs = [h[3] for h in handles]
    lands = [h[4] for h in handles]
    res = pl.pallas_call(
        body, name=name, out_shape=[pltpu.HBM(s.shape, s.dtype) for s in srcs] + [pltpu.HBM(l.shape, l.dtype) for l in lands],
        in_specs=[_HBM] * (2 * n) + [_SEM] * (3 * n) + [pl.BlockSpec(memory_space=pl.ANY)], out_specs=[_HBM] * (2 * n),
        input_output_aliases={i: i for i in range(2 * n)},
        compiler_params=pltpu.CompilerParams(has_side_effects=_EFFECT))(
            *srcs, *lands, *[h[0] for h in handles], *[h[1] for h in handles], *[h[2] for h in handles], after)
    return list(res[n:])


def _adamw(name, parts, w, m, v):
    K, n = w.shape
    tr = min(K, 128)

    def body(p_ref, w_ref, m_ref, v_ref, g_ref, d_ref, nm_ref, nv_ref):
        g = p_ref[0].astype(F32)
        for i in range(1, N_DEV):
            g = g + p_ref[i].astype(F32)
        nm = ADAM_B1 * m_ref[...] + (1.0 - ADAM_B1) * g
        nv = ADAM_B2 * v_ref[...] + (1.0 - ADAM_B2) * jnp.square(g)
        m_hat = nm / (1.0 - ADAM_B1 ** ADAM_STEP)
        v_hat = nv / (1.0 - ADAM_B2 ** ADAM_STEP)
        g_ref[...] = g
        d_ref[...] = -ADAM_LR * (m_hat / (jnp.sqrt(v_hat) + ADAM_EPS) + ADAM_WD * w_ref[...])
        nm_ref[...] = nm
        nv_ref[...] = nv

    blk = pl.BlockSpec((tr, n), lambda i: (i, 0))
    return _pc(body, name=name, grid=(K // tr,), in_specs=[pl.BlockSpec((N_DEV, tr, n), lambda i: (0, i, 0)), blk, blk, blk],
               out_specs=[blk] * 4, out_shape=[jax.ShapeDtypeStruct((K, n), F32)] * 4, sem=("parallel",))(parts, w, m, v)


def _pack_small(vals):
    flat = jnp.concatenate([v.reshape(-1) for v in vals])
    return jnp.pad(flat, (0, SMALL_ROWS * 128 - flat.shape[0])).reshape(SMALL_ROWS, 128)


def _unpack_small(packed, shapes):
    flat, out, off = packed.reshape(-1), [], 0
    for s in shapes:
        size = int(np.prod(s))
        out.append(flat[off:off + size].reshape(s))
        off += size
    return out


def kernel(x, norm_mix, w_qkv, w_gate, b_gate, rpb, w_proj_a, w_proj_b, w_out, norm_mlp, w_up, w_down, norm_final, loss_target, m_norm_mix, m_w_qkv, m_w_gate, m_b_gate, m_rpb, m_w_proj_a, m_w_proj_b, m_w_out, m_norm_mlp, m_w_up, m_w_down, m_norm_final, v_norm_mix, v_w_qkv, v_w_gate, v_b_gate, v_rpb, v_w_proj_a, v_w_proj_b, v_w_out, v_norm_mlp, v_w_up, v_w_down, v_norm_final):
    w = dict(norm_mix=norm_mix, w_qkv=w_qkv, w_gate=w_gate, b_gate=b_gate, rpb=rpb, w_proj_a=w_proj_a, w_proj_b=w_proj_b,
             w_out=w_out, norm_mlp=norm_mlp, w_up=w_up, w_down=w_down, norm_final=norm_final)
    m = dict(norm_mix=m_norm_mix, w_qkv=m_w_qkv, w_gate=m_w_gate, b_gate=m_b_gate, rpb=m_rpb, w_proj_a=m_w_proj_a,
             w_proj_b=m_w_proj_b, w_out=m_w_out, norm_mlp=m_norm_mlp, w_up=m_w_up, w_down=m_w_down, norm_final=m_norm_final)
    v = dict(norm_mix=v_norm_mix, w_qkv=v_w_qkv, w_gate=v_w_gate, b_gate=v_b_gate, rpb=v_rpb, w_proj_a=v_w_proj_a,
             w_proj_b=v_w_proj_b, w_out=v_w_out, norm_mlp=v_norm_mlp, w_up=v_w_up, w_down=v_w_down, norm_final=v_norm_final)
    order = ("norm_mix", "w_qkv", "w_gate", "b_gate", "rpb", "w_proj_a", "w_proj_b", "w_out", "norm_mlp", "w_up",
             "w_down", "norm_final")

    shard = {n: w[n][0].astype(BF16) for n in BIG}
    (wq,) = _all_gather_weights([shard["w_qkv"]])
    late_names = ("w_gate", "w_proj_a", "w_proj_b", "w_out", "w_up", "w_down")
    late, token = _send_start("gather_late_start", [shard[n] for n in late_names], False, wq)
    late = dict(zip(late_names, late))

    def late_weights(stage, after):
        names = {"gate": ("w_gate",), "proj": ("w_proj_a", "w_proj_b", "w_out"), "mlp": ("w_up", "w_down")}[stage]
        return _send_wait("gather_late_wait_" + stage, [late[n] for n in names], False, after)

    sent = {}

    def send(stage, grads):
        handles, tok = _send_start("grads_start_" + stage, list(grads.values()), True)
        sent[stage] = (tuple(grads), handles)
        return tok

    loss, grad_x, small = _local_step(x[0], loss_target[0], _behind(norm_mix, token), b_gate, rpb[0], norm_mlp, norm_final,
                                      wq, late_weights, send)

    grad, delta, new_m, new_v = {}, {}, {}, {}
    after = grad_x
    for stage in ("mlp", "proj", "in"):
        names, handles = sent[stage]
        for n, parts in zip(names, _send_wait("grads_wait_" + stage, handles, True, after)):
            res = _adamw("adamw_" + n, parts, w[n][0], m[n][0], v[n][0])
            grad[n], delta[n], new_m[n], new_v[n] = (r[None] for r in res)
            after = res[1]
    res = _adamw("adamw_small", _gather_small(_pack_small([small[n] for n in SMALL])),
                 _pack_small([w[n] for n in SMALL]), _pack_small([m[n] for n in SMALL]), _pack_small([v[n] for n in SMALL]))
    shapes = [w[n].shape for n in SMALL]
    for tree, packed in zip((grad, delta, new_m, new_v), res):
        for n, val in zip(SMALL, _unpack_small(packed, shapes)):
            tree[n] = val

    total = lax.psum(loss[0, 0], ("x", "y", "c"))
    return (total, grad_x[None], *[grad[n] for n in order], *[delta[n] for n in order],
            *[new_m[n] for n in order], *[new_v[n] for n in order])
```

```python
import numpy as np
import jax
import jax.numpy as jnp
from jax import lax
from jax.experimental import pallas as pl
from jax.experimental.pallas import tpu as pltpu

F32 = jnp.float32
BF16 = jnp.bfloat16
MESH = pl.DeviceIdType.MESH

D_MODEL = 2048
HEAD_DIM = 128
N_HEADS = 16
N_HEADS_A = 12
N_HEADS_B = 4
DILATIONS = (1, 4, 16)
HALF_WINDOW = 64
GRID_W = 64
NA_ROWS = 8
NA_COLS = 16
D_FF = 4 * D_MODEL
N_DEV = 8
EPS = 1e-6
NEG = -1e30
SCALE = HEAD_DIM ** -0.5
ADAM_LR, ADAM_B1, ADAM_B2, ADAM_EPS, ADAM_WD, ADAM_STEP = 0.001, 0.9, 0.999, 1e-08, 0.01, 10

VMEM_LIMIT = 56 * 1024 * 1024
TM = 512
TL = 1024
TS = 256
BQ_A = 128
LOCAL_CHUNKS = 8


def _pc(body, *, name, grid, in_specs, out_specs, out_shape, scratch=(), sem=None):
    return pl.pallas_call(
        body, name=name, grid=grid, in_specs=in_specs, out_specs=out_specs, out_shape=out_shape,
        scratch_shapes=list(scratch),
        compiler_params=pltpu.CompilerParams(dimension_semantics=sem, vmem_limit_bytes=VMEM_LIMIT))


_DOT_DIMS = {"nn": (((1,), (0,)), ((), ())), "nt": (((1,), (1,)), ((), ())), "tn": (((0,), (0,)), ((), ()))}


def _dot(a, b, mode, precision=None):
    if mode == "nt_shards":
        n = b.shape[2]
        out = _dot(a[:, :n], b[0], "nt")
        for s in range(1, b.shape[0]):
            out = out + _dot(a[:, s * n:(s + 1) * n], b[s], "nt")
        return out
    return lax.dot_general(a, b, _DOT_DIMS[mode], preferred_element_type=F32, precision=precision)


def _mm(name, grid, pairs, extras, outs, epilogue, acc_shape, sem):
    n_pairs, n_extra, n_out = len(pairs), len(extras), len(outs)
    k_axis = len(grid) - 1
    nk = grid[k_axis]

    def body(*refs):
        pids = [pl.program_id(ax) for ax in range(len(grid))]
        ab = refs[:2 * n_pairs]
        ex = refs[2 * n_pairs:2 * n_pairs + n_extra]
        o = refs[2 * n_pairs + n_extra:2 * n_pairs + n_extra + n_out]
        part = None
        for p, pair in enumerate(pairs):
            a = ab[2 * p][...]
            if pair[5] is not None:
                a = pair[5](a)
            d = _dot(a, ab[2 * p + 1][...], pair[4])
            part = d if part is None else part + d
        if nk == 1:
            epilogue(part, ex, o, pids)
        else:
            acc = refs[-1]
            k = pids[k_axis]

            @pl.when(k == 0)
            def _():
                acc[...] = part

            @pl.when(k > 0)
            def _():
                acc[...] += part

            @pl.when(k == nk - 1)
            def _():
                epilogue(acc[...], ex, o, pids)

    operands, in_specs = [], []
    for a, a_spec, b, b_spec, _, _ in pairs:
        operands += [a, b]
        in_specs += [a_spec, b_spec]
    for e, e_spec in extras:
        operands.append(e)
        in_specs.append(e_spec)
    res = _pc(body, name=name, grid=grid, in_specs=in_specs, out_specs=[s for _, s in outs],
              out_shape=[o for o, _ in outs], scratch=[pltpu.VMEM(acc_shape, F32)] if nk > 1 else [], sem=sem)(*operands)
    return res


def _rmsnorm_fwd(x, g, name):
    T, D = x.shape

    def body(x_ref, g_ref, h_ref, r_ref):
        xv = x_ref[...]
        r = lax.rsqrt(jnp.mean(xv * xv, axis=-1, keepdims=True) + EPS)
        h_ref[...] = (xv * r * g_ref[...]).astype(BF16)
        r_ref[...] = r

    return _pc(body, name=name, grid=(T // TM,),
               in_specs=[pl.BlockSpec((TM, D), lambda i: (i, 0)), pl.BlockSpec((1, D), lambda i: (0, 0))],
               out_specs=[pl.BlockSpec((TM, D), lambda i: (i, 0)), pl.BlockSpec((TM, 1), lambda i: (i, 0))],
               out_shape=[jax.ShapeDtypeStruct((T, D), BF16), jax.ShapeDtypeStruct((T, 1), F32)],
               sem=("parallel",))(x, g)


def _rmsnorm_bwd_block(dh, x, r, g, dres):
    xh = x * r
    dxh = dh * g
    dx = dres + r * (dxh - xh * jnp.mean(dxh * xh, axis=-1, keepdims=True))
    return dx, jnp.sum(dh * xh, axis=0, keepdims=True)


def _final_norm_loss(x3, g, tgt):
    T, D = x3.shape

    def body(x_ref, g_ref, t_ref, loss_ref, dx_ref, dxb_ref, dg_ref):
        i = pl.program_id(0)
        xv = x_ref[...]
        gv = g_ref[...]
        r = lax.rsqrt(jnp.mean(xv * xv, axis=-1, keepdims=True) + EPS)
        xh = xv * r
        e = xh * gv - t_ref[...]
        part = 0.5 * jnp.sum(jnp.mean(e * e, axis=-1, keepdims=True))
        dy = e * (1.0 / D)
        dxh = dy * gv
        dx = r * (dxh - xh * jnp.mean(dxh * xh, axis=-1, keepdims=True))
        dx_ref[...] = dx
        dxb_ref[...] = dx.astype(BF16)
        dgp = jnp.sum(dy * xh, axis=0, keepdims=True)

        @pl.when(i == 0)
        def _():
            loss_ref[...] = jnp.zeros(loss_ref.shape, F32) + part
            dg_ref[...] = dgp

        @pl.when(i > 0)
        def _():
            loss_ref[...] += part
            dg_ref[...] += dgp

    row = pl.BlockSpec((TM, D), lambda i: (i, 0))
    vec = pl.BlockSpec((1, D), lambda i: (0, 0))
    return _pc(body, name="final_norm_loss", grid=(T // TM,), in_specs=[row, vec, row],
               out_specs=[pl.BlockSpec((8, 128), lambda i: (0, 0)), row, row, vec],
               out_shape=[jax.ShapeDtypeStruct((8, 128), F32), jax.ShapeDtypeStruct((T, D), F32),
                          jax.ShapeDtypeStruct((T, D), BF16), jax.ShapeDtypeStruct((1, D), F32)],
               sem=("arbitrary",))(x3, g, tgt)


def _alibi_slopes():
    return np.asarray(2.0 ** (-8.0 * np.arange(1, N_HEADS_A + 1) / N_HEADS_A), dtype=np.float32)


def _a_windows(T):
    half = [HALF_WINDOW * d for d in DILATIONS]
    return half, [min(BQ_A + 2 * w, T) for w in half]


def _a_scores(q, kw, slope, g, t0, ws, wk, half):
    d = DILATIONS[g]
    s = _dot(q, kw, "nt") * SCALE
    qi = t0 + lax.broadcasted_iota(jnp.int32, (BQ_A, wk), 0)
    ki = ws + lax.broadcasted_iota(jnp.int32, (BQ_A, wk), 1)
    dlt = ki - qi
    ad = jnp.maximum(dlt, -dlt)
    ok = (ad <= half) & ((ad & (d - 1)) == 0)
    return jnp.where(ok, s - slope * ad.astype(F32), NEG)


def _a_qkv_specs(T, inner):
    specs = []
    for g in range(3):
        specs.append(pl.BlockSpec((BQ_A, HEAD_DIM), lambda j, i, g=g: (i, 4 * g + j)))
        specs.append(pl.BlockSpec((T, HEAD_DIM), lambda j, i, g=g: (0, N_HEADS + 4 * g + j)))
        specs.append(pl.BlockSpec((T, HEAD_DIM), lambda j, i, g=g: (0, 2 * N_HEADS + 4 * g + j)))
    return specs


def _attn_a_fwd(qkv, slopes):
    T = qkv.shape[0]
    nq = T // BQ_A
    half, wks = _a_windows(T)

    def body(sl_ref, *refs):
        qkv_refs, (y_ref, lse_ref) = refs[:9], refs[9:]
        j = pl.program_id(0)
        t0 = pl.program_id(1) * BQ_A
        ss, vws = [], []
        for g in range(3):
            q_ref, k_ref, v_ref = qkv_refs[3 * g:3 * g + 3]
            ws = pl.multiple_of(jnp.clip(t0 - half[g], 0, T - wks[g]), 64)
            ss.append(_a_scores(q_ref[...], k_ref[pl.ds(ws, wks[g]), :], sl_ref[4 * g + j], g, t0, ws, wks[g], half[g]))
            vws.append(v_ref[pl.ds(ws, wks[g]), :])
        m = jnp.maximum(jnp.maximum(jnp.max(ss[0], axis=-1, keepdims=True), jnp.max(ss[1], axis=-1, keepdims=True)),
                        jnp.max(ss[2], axis=-1, keepdims=True))
        den = jnp.zeros((BQ_A, 1), F32)
        acc = jnp.zeros((BQ_A, HEAD_DIM), F32)
        for g in range(3):
            p = jnp.exp(ss[g] - m)
            den = den + jnp.sum(p, axis=-1, keepdims=True)
            acc = acc + _dot(p.astype(BF16), vws[g], "nn")
        y_ref[...] = (acc / den).astype(BF16)
        lse_ref[...] = m + jnp.log(den)

    return _pc(body, name="attn_a_fwd", grid=(4, nq),
               in_specs=[pl.BlockSpec(memory_space=pltpu.SMEM)] + _a_qkv_specs(T, nq),
               out_specs=[pl.BlockSpec((BQ_A, HEAD_DIM), lambda j, i: (i, j)),
                          pl.BlockSpec((None, BQ_A, 1), lambda j, i: (j, i, 0))],
               out_shape=[jax.ShapeDtypeStruct((T, 4 * HEAD_DIM), BF16), jax.ShapeDtypeStruct((4, T, 1), F32)],
               sem=("parallel", "parallel"))(slopes, *([qkv] * 9))


def _attn_a_bwd(qkv, slopes, dy, y, lse):
    T = qkv.shape[0]
    nq = T // BQ_A
    half, wks = _a_windows(T)

    def body(sl_ref, *refs):
        qkv_refs = refs[:9]
        dy_ref, y_ref, lse_ref = refs[9:12]
        dq_ref, dk_ref, dv_ref = refs[12:15]
        dk_acc, dv_acc = refs[15:]
        j = pl.program_id(0)
        i = pl.program_id(1)
        t0 = i * BQ_A

        @pl.when(i == 0)
        def _():
            dk_acc[...] = jnp.zeros_like(dk_acc)
            dv_acc[...] = jnp.zeros_like(dv_acc)

        dyv = dy_ref[...]
        delta = jnp.sum(dyv.astype(F32) * y_ref[...].astype(F32), axis=-1, keepdims=True)
        lse_v = lse_ref[...]
        for g in range(3):
            q_ref, k_ref, v_ref = qkv_refs[3 * g:3 * g + 3]
            ws = pl.multiple_of(jnp.clip(t0 - half[g], 0, T - wks[g]), 64)
            win = pl.ds(ws, wks[g])
            qv = q_ref[...]
            kw = k_ref[win, :]
            s = _a_scores(qv, kw, sl_ref[4 * g + j], g, t0, ws, wks[g], half[g])
            p = jnp.exp(s - lse_v)
            dp = _dot(dyv, v_ref[win, :], "nt")
            ds = (p * (dp - delta)).astype(BF16)
            dq_ref[g] = (_dot(ds, kw, "nn") * SCALE).astype(BF16)
            dk_acc[g, win, :] += _dot(ds, qv, "tn") * SCALE
            dv_acc[g, win, :] += _dot(p.astype(BF16), dyv, "tn")

        @pl.when(i == nq - 1)
        def _():
            dk_ref[...] = dk_acc[...].astype(BF16)
            dv_ref[...] = dv_acc[...].astype(BF16)

    blk = pl.BlockSpec((BQ_A, HEAD_DIM), lambda j, i: (i, j))
    whole = pl.BlockSpec((3, T, HEAD_DIM), lambda j, i: (0, 0, j))
    full = jax.ShapeDtypeStruct((3, T, 4 * HEAD_DIM), BF16)
    return _pc(body, name="attn_a_bwd", grid=(4, nq),
               in_specs=[pl.BlockSpec(memory_space=pltpu.SMEM)] + _a_qkv_specs(T, nq)
               + [blk, blk, pl.BlockSpec((None, BQ_A, 1), lambda j, i: (j, i, 0))],
               out_specs=[pl.BlockSpec((3, BQ_A, HEAD_DIM), lambda j, i: (0, i, j)), whole, whole],
               out_shape=[full, full, full],
               scratch=[pltpu.VMEM((3, T, HEAD_DIM), F32), pltpu.VMEM((3, T, HEAD_DIM), F32)],
               sem=("parallel", "arbitrary"))(slopes, *([qkv] * 9), dy, y, lse)


N_DR = 2 * NA_ROWS - 1
N_DC = 2 * NA_COLS - 1
WK_B = NA_ROWS * GRID_W


def _toeplitz_consts():
    col = np.arange(GRID_W)
    start = np.clip(col - NA_COLS // 2, 0, GRID_W - NA_COLS)
    ok = (col[None, :] >= start[:, None]) & (col[None, :] < start[:, None] + NA_COLS)
    dc = np.clip(col[None, :] - col[:, None], -(NA_COLS - 1), NA_COLS - 1) + NA_COLS - 1
    sel = np.zeros((128, GRID_W * GRID_W), np.float32)
    sel[dc.reshape(-1), np.arange(GRID_W * GRID_W)] = ok.reshape(-1).astype(np.float32)
    mask = np.where(ok, 0.0, NEG).astype(np.float32).reshape(1, -1)
    return sel, mask


def _rpb_expand(rpb_rows, sel, mask):
    def body(r_ref, s_ref, m_ref, o_ref):
        o_ref[...] = _dot(r_ref[...], s_ref[...], "nn", precision=lax.Precision.HIGHEST) + m_ref[...]

    n = rpb_rows.shape[0]
    return _pc(body, name="rpb_expand", grid=(1,),
               in_specs=[pl.BlockSpec((n, 128), lambda i: (0, 0)), pl.BlockSpec(sel.shape, lambda i: (0, 0)),
                         pl.BlockSpec(mask.shape, lambda i: (0, 0))],
               out_specs=pl.BlockSpec((n, sel.shape[1]), lambda i: (0, 0)),
               out_shape=jax.ShapeDtypeStruct((n, sel.shape[1]), F32), sem=("arbitrary",))(rpb_rows, sel, mask)


def _rpb_reduce(dtiles, sel):
    def body(t_ref, s_ref, o_ref):
        o_ref[...] = _dot(t_ref[...], s_ref[...], "nt", precision=lax.Precision.HIGHEST)

    n = dtiles.shape[0]
    return _pc(body, name="rpb_reduce", grid=(1,),
               in_specs=[pl.BlockSpec(dtiles.shape, lambda i: (0, 0)), pl.BlockSpec(sel.shape, lambda i: (0, 0))],
               out_specs=pl.BlockSpec((n, 128), lambda i: (0, 0)),
               out_shape=jax.ShapeDtypeStruct((n, 128), F32), sem=("arbitrary",))(dtiles, sel)


def _b_specs(T):
    return [pl.BlockSpec((GRID_W, HEAD_DIM), lambda h, r: (r, N_HEADS_A + h)),
            pl.BlockSpec((T, HEAD_DIM), lambda h, r: (0, N_HEADS + N_HEADS_A + h)),
            pl.BlockSpec((T, HEAD_DIM), lambda h, r: (0, 2 * N_HEADS + N_HEADS_A + h)),
            pl.BlockSpec((None, N_DR, GRID_W, GRID_W), lambda h, r: (h, 0, 0, 0))]


def _b_row_start(r, rows):
    return jnp.clip(r - NA_ROWS // 2, 0, rows - NA_ROWS)


def _attn_b_fwd(qkv, tiles):
    T = qkv.shape[0]
    rows = T // GRID_W

    def body(q_ref, k_ref, v_ref, t_ref, y_ref, lse_ref, bias_ref):
        r = pl.program_id(1)
        rs = _b_row_start(r, rows)
        win = pl.ds(pl.multiple_of(rs * GRID_W, GRID_W), WK_B)
        for a in range(NA_ROWS):
            bias_ref[:, a * GRID_W:(a + 1) * GRID_W] = t_ref[rs + a - r + NA_ROWS - 1]
        s = _dot(q_ref[...], k_ref[win, :], "nt") * SCALE + bias_ref[...]
        m = jnp.max(s, axis=-1, keepdims=True)
        p = jnp.exp(s - m)
        den = jnp.sum(p, axis=-1, keepdims=True)
        y_ref[...] = (_dot(p.astype(BF16), v_ref[win, :], "nn") / den).astype(BF16)
        lse_ref[...] = m + jnp.log(den)

    return _pc(body, name="attn_b_fwd", grid=(N_HEADS_B, rows), in_specs=_b_specs(T),
               out_specs=[pl.BlockSpec((GRID_W, HEAD_DIM), lambda h, r: (r, h)),
                          pl.BlockSpec((None, GRID_W, 1), lambda h, r: (h, r, 0))],
               out_shape=[jax.ShapeDtypeStruct((T, N_HEADS_B * HEAD_DIM), BF16),
                          jax.ShapeDtypeStruct((N_HEADS_B, T, 1), F32)],
               scratch=[pltpu.VMEM((GRID_W, WK_B), F32)], sem=("parallel", "parallel"))(qkv, qkv, qkv, tiles)


def _attn_b_bwd(qkv, tiles, dy, y, lse):
    T = qkv.shape[0]
    rows = T // GRID_W

    def body(q_ref, k_ref, v_ref, t_ref, dy_ref, y_ref, lse_ref, dq_ref, dk_ref, dv_ref, dt_ref,
             bias_ref, ds_ref, dk_acc, dv_acc):
        r = pl.program_id(1)
        rs = _b_row_start(r, rows)
        win = pl.ds(pl.multiple_of(rs * GRID_W, GRID_W), WK_B)

        @pl.when(r == 0)
        def _():
            dk_acc[...] = jnp.zeros_like(dk_acc)
            dv_acc[...] = jnp.zeros_like(dv_acc)
            dt_ref[...] = jnp.zeros_like(dt_ref)

        for a in range(NA_ROWS):
            bias_ref[:, a * GRID_W:(a + 1) * GRID_W] = t_ref[rs + a - r + NA_ROWS - 1]
        qv = q_ref[...]
        kw = k_ref[win, :]
        dyv = dy_ref[...]
        s = _dot(qv, kw, "nt") * SCALE + bias_ref[...]
        p = jnp.exp(s - lse_ref[...])
        delta = jnp.sum(dyv.astype(F32) * y_ref[...].astype(F32), axis=-1, keepdims=True)
        ds = p * (_dot(dyv, v_ref[win, :], "nt") - delta)
        ds_ref[...] = ds
        for a in range(NA_ROWS):
            dt_ref[rs + a - r + NA_ROWS - 1] += ds_ref[:, a * GRID_W:(a + 1) * GRID_W]
        dsb = ds.astype(BF16)
        dq_ref[...] = (_dot(dsb, kw, "nn") * SCALE).astype(BF16)
        dk_acc[win, :] += _dot(dsb, qv, "tn") * SCALE
        dv_acc[win, :] += _dot(p.astype(BF16), dyv, "tn")

        @pl.when(r == rows - 1)
        def _():
            dk_ref[...] = dk_acc[...].astype(BF16)
            dv_ref[...] = dv_acc[...].astype(BF16)

    blk = pl.BlockSpec((GRID_W, HEAD_DIM), lambda h, r: (r, h))
    whole = pl.BlockSpec((T, HEAD_DIM), lambda h, r: (0, h))
    full = jax.ShapeDtypeStruct((T, N_HEADS_B * HEAD_DIM), BF16)
    return _pc(body, name="attn_b_bwd", grid=(N_HEADS_B, rows),
               in_specs=_b_specs(T) + [blk, blk, pl.BlockSpec((None, GRID_W, 1), lambda h, r: (h, r, 0))],
               out_specs=[blk, whole, whole, pl.BlockSpec((None, N_DR, GRID_W, GRID_W), lambda h, r: (h, 0, 0, 0))],
               out_shape=[full, full, full, jax.ShapeDtypeStruct((N_HEADS_B, N_DR, GRID_W, GRID_W), F32)],
               scratch=[pltpu.VMEM((GRID_W, WK_B), F32), pltpu.VMEM((GRID_W, WK_B), F32),
                        pltpu.VMEM((T, HEAD_DIM), F32), pltpu.VMEM((T, HEAD_DIM), F32)],
               sem=("parallel", "arbitrary"))(qkv, qkv, qkv, tiles, dy, y, lse)


def _sds(shape, dtype):
    return jax.ShapeDtypeStruct(shape, dtype)


def _behind(value, token):
    return value if token is None else value + token[0, 0]


def _local_step(x, tgt, norm_mix, b_gate, rpb, norm_mlp, norm_final, wq, late_weights, send):
    T, D = x.shape
    nt = T // TM
    PP, PA = ("parallel", "parallel", "arbitrary"), ("arbitrary", "arbitrary", "arbitrary")
    slopes = jnp.asarray(_alibi_slopes())
    sel_np, mask_np = _toeplitz_consts()
    sel, mask = jnp.asarray(sel_np), jnp.asarray(mask_np)

    def rows(w, tm=TM):
        return pl.BlockSpec((tm, w), lambda i, j, k: (i, j))

    def full_rows(tm, w=D):
        return pl.BlockSpec((tm, w), lambda i, j, k: (i, 0))

    def tall(w):
        return pl.BlockSpec((T, w), lambda j, i, k: (0, i))

    def tall_j(w):
        return pl.BlockSpec((T, w), lambda j, i, k: (0, j))

    rvec = pl.BlockSpec((TS, 1), lambda i, j, k: (i, 0))
    gvec = pl.BlockSpec((1, D), lambda i, j, k: (0, 0))

    h1, r1 = _rmsnorm_fwd(x, norm_mix, "rmsnorm_mix")

    def store_bf16(acc, ex, o, pids):
        o[0][...] = acc.astype(BF16)

    (qkv,) = _mm("qkv_proj", (T // TL, N_DEV, 1),
                 [(h1, full_rows(TL), wq, pl.BlockSpec((None, D, 768), lambda i, j, k: (j, 0, 0)), "nn", None)],
                 [], [(_sds((T, 3 * D), BF16), rows(768, TL))], store_bf16, None, PP)

    def gate_epi(acc, ex, o, pids):
        o[0][...] = jax.nn.sigmoid(acc + ex[0][...])

    (wg,) = late_weights("gate", qkv)

    (gates,) = _mm("gate_proj", (T // TL, N_DEV, 1),
                   [(h1, full_rows(TL), wg, pl.BlockSpec((None, D, 512), lambda i, j, k: (j, 0, 0)), "nn", None)],
                   [(b_gate, pl.BlockSpec((1, 512), lambda i, j, k: (0, j)))],
                   [(_sds((T, 2 * D), F32), rows(512, TL))], gate_epi, None, PP)

    rpb_rows = jnp.pad(rpb.reshape(N_HEADS_B * N_DR, N_DC), ((0, 0), (0, 128 - N_DC)))
    tiles = _rpb_expand(rpb_rows, sel, mask).reshape(N_HEADS_B, N_DR, GRID_W, GRID_W)
    ya, lse_a = _attn_a_fwd(qkv, slopes)
    yb, lse_b = _attn_b_fwd(qkv, tiles)

    def merge_body(ya_ref, wa_ref, yb_ref, wb_ref, ga_ref, gb_ref, pa_ref, pb_ref, mg_ref):
        ya_v, yb_v = ya_ref[...], yb_ref[...]
        for s in range(N_DEV):
            cols = slice(s * 256, (s + 1) * 256)
            pa = _dot(ya_v, wa_ref[s], "nn")
            pb = _dot(yb_v, wb_ref[s], "nn")
            pa_ref[:, cols] = pa
            pb_ref[:, cols] = pb
            mg_ref[:, cols] = (ga_ref[:, cols] * pa + gb_ref[:, cols] * pb).astype(BF16)

    wpa, wpb, wo = late_weights("proj", yb)
    wo2 = wo.reshape(D, D)
    y_spec = pl.BlockSpec((TS, 512), lambda i: (i, 0))
    wp_spec = pl.BlockSpec((N_DEV, 512, 256), lambda i: (0, 0, 0))
    c_spec = pl.BlockSpec((TS, D), lambda i: (i, 0))
    pa, pb, merged = _pc(merge_body, name="merge_proj", grid=(T // TS,),
                         in_specs=[y_spec, wp_spec, y_spec, wp_spec, c_spec, pl.BlockSpec((TS, D), lambda i: (i, 1))],
                         out_specs=[c_spec, c_spec, c_spec],
                         out_shape=[_sds((T, D), F32), _sds((T, D), F32), _sds((T, D), BF16)],
                         sem=("parallel",))(ya, wpa, yb, wpb, gates, gates)

    def add_res(acc, ex, o, pids):
        o[0][...] = ex[0][...] + acc

    (x2,) = _mm("out_proj", (T // TL, 4, 1),
                [(merged, full_rows(TL), wo2, pl.BlockSpec((D, 512), lambda i, j, k: (0, j)), "nn", None)],
                [(x, rows(512, TL))], [(_sds((T, D), F32), rows(512, TL))], add_res, None, PP)

    h2, r2 = _rmsnorm_fwd(x2, norm_mlp, "rmsnorm_mlp")
    wu, wd = late_weights("mlp", h2)
    wd2 = wd.reshape(D_FF, D)

    def up_epi(acc, ex, o, pids):
        o[0][...] = acc.astype(BF16)
        r = jnp.maximum(acc, 0.0)
        o[1][...] = (r * r).astype(BF16)

    u, act = _mm("mlp_up", (T // TL, N_DEV, 1),
                 [(h2, full_rows(TL), wu, pl.BlockSpec((None, D, 1024), lambda i, j, k: (j, 0, 0)), "nn", None)],
                 [], [(_sds((T, D_FF), BF16), rows(1024, TL))] * 2, up_epi, None, PP)
    (x3,) = _mm("mlp_down", (nt, D // 256, 1),
                [(act, full_rows(TM, D_FF), wd2, pl.BlockSpec((D_FF, 256), lambda i, j, k: (0, j)), "nn", None)],
                [(x2, rows(256))], [(_sds((T, D), F32), rows(256))], add_res, None, PP)

    loss, dx3, dx3b, dg3 = _final_norm_loss(x3, norm_final.reshape(1, D), tgt)

    def du_epi(acc, ex, o, pids):
        o[0][...] = (acc * (2.0 * jnp.maximum(ex[0][...].astype(F32), 0.0))).astype(BF16)

    (du,) = _mm("mlp_down_dx", (T // TL, N_DEV, 1),
                [(dx3b, full_rows(TL), wd2, pl.BlockSpec((1024, D), lambda i, j, k: (j, 0)), "nt", None)],
                [(u, rows(1024, TL))], [(_sds((T, D_FF), BF16), rows(1024, TL))], du_epi, None, PP)
    (dwd,) = _mm("mlp_down_dw", (D // 1024, D_FF // 512, 1), [(act, tall(512), dx3b, tall_j(1024), "tn", None)],
                 [], [(_sds((D_FF, D), BF16), pl.BlockSpec((512, 1024), lambda j, i, k: (i, j)))], store_bf16, None, PP)
    dwd = dwd.reshape(N_DEV, D_FF // N_DEV, D)
    (dwu,) = _mm("mlp_up_dw", (N_DEV, D // 512, 1), [(h2, tall(512), du, tall_j(1024), "tn", None)],
                 [], [(_sds((N_DEV, D, 1024), BF16), pl.BlockSpec((None, 512, 1024), lambda j, i, k: (j, i, 0)))],
                 store_bf16, None, PP)

    def norm_bwd_epi(acc, ex, o, pids):
        x_ref, r_ref, g_ref, dres_ref = ex
        dx, dgp = _rmsnorm_bwd_block(acc, x_ref[...], r_ref[...], g_ref[...], dres_ref[...])
        o[0][...] = dx
        if len(o) == 3:
            o[1][...] = dx.astype(BF16)
        first = pids[0] == 0

        @pl.when(first)
        def _():
            o[-1][...] = dgp

        @pl.when(jnp.logical_not(first))
        def _():
            o[-1][...] += dgp

    row_s = full_rows(TS)
    dx2, dx2b, dg2 = _mm("mlp_up_dx", (T // TS, 1, N_DEV // 2),
                         [(du, pl.BlockSpec((TS, 2048), lambda i, j, k: (i, k)), wu,
                           pl.BlockSpec((2, D, 1024), lambda i, j, k: (k, 0, 0)), "nt_shards", None)],
                         [(x2, row_s), (r2, rvec), (_behind(norm_mlp, send("mlp", dict(w_down=dwd, w_up=dwu))), gvec),
                          (dx3, row_s)],
                         [(_sds((T, D), F32), row_s), (_sds((T, D), BF16), row_s), (_sds((1, D), F32), gvec)],
                         norm_bwd_epi, (TS, D), PA)

    (dwo,) = _mm("out_proj_dw", (D // 1024, D // 512, 1), [(merged, tall(512), dx2b, tall_j(1024), "tn", None)],
                 [], [(_sds((D, D), BF16), pl.BlockSpec((512, 1024), lambda j, i, k: (i, j)))], store_bf16, None, PP)

    def dmerge_epi(acc, ex, o, pids):
        ga, gb, pa_v, pb_v = (e[...] for e in ex)
        o[0][...] = (acc * ga).astype(BF16)
        o[1][...] = (acc * gb).astype(BF16)
        dga = acc * pa_v * ga * (1.0 - ga)
        dgb = acc * pb_v * gb * (1.0 - gb)
        o[2][...] = dga.astype(BF16)
        o[3][...] = dgb.astype(BF16)
        sa = jnp.sum(dga, axis=0, keepdims=True)
        sb = jnp.sum(dgb, axis=0, keepdims=True)
        first = pids[1] == 0

        @pl.when(first)
        def _():
            o[4][...] = sa
            o[5][...] = sb

        @pl.when(jnp.logical_not(first))
        def _():
            o[4][...] += sa
            o[5][...] += sb

    cj = pl.BlockSpec((TM, 512), lambda j, i, k: (i, j))
    bj = pl.BlockSpec((1, 512), lambda j, i, k: (0, j))
    dpa, dpb, dga, dgb, dba, dbb = _mm(
        "out_proj_dx", (4, nt, 1),
        [(dx2b, pl.BlockSpec((TM, D), lambda j, i, k: (i, 0)), wo2, pl.BlockSpec((512, D), lambda j, i, k: (j, 0)), "nt", None)],
        [(gates, cj), (gates, pl.BlockSpec((TM, 512), lambda j, i, k: (i, 4 + j))), (pa, cj), (pb, cj)],
        [(_sds((T, D), BF16), cj)] * 4 + [(_sds((1, D), F32), bj)] * 2, dmerge_epi, None,
        ("parallel", "arbitrary", "arbitrary"))

    def store_shards(acc, ex, o, pids):
        for s in range(4):
            o[0][s] = acc[:, s * 256:(s + 1) * 256].astype(BF16)

    def proj_dw(name, y, dp):
        (dw,) = _mm(name, (2, 1, 1), [(y, tall(512), dp, tall_j(1024), "tn", None)],
                    [], [(_sds((N_DEV, 512, 256), BF16), pl.BlockSpec((4, 512, 256), lambda j, i, k: (j, 0, 0)))],
                    store_shards, None, PP)
        return dw

    def proj_dx(name, dp, w):
        (dy,) = _mm(name, (nt, 1, 1),
                    [(dp, full_rows(TM), w, pl.BlockSpec((N_DEV, 512, 256), lambda i, j, k: (0, 0, 0)), "nt_shards", None)],
                    [], [(_sds((T, 512), BF16), full_rows(TM, 512))], store_bf16, None, PP)
        return dy

    def in_dw(name, dact, n):
        (dw,) = _mm(name, (N_DEV, D // 512, 1), [(h1, tall(512), dact, tall_j(n), "tn", None)],
                    [], [(_sds((N_DEV, D, n), BF16), pl.BlockSpec((None, 512, n), lambda j, i, k: (j, i, 0)))],
                    store_bf16, None, PP)
        return dw

    dgate = jnp.concatenate([dga, dgb], axis=1)
    dwg = in_dw("gate_proj_dw", dgate, 512)
    dwpa = proj_dw("proj_a_dw", ya, dpa)
    dwpb = proj_dw("proj_b_dw", yb, dpb)
    token = send("proj", dict(w_gate=dwg, w_out=dwo.reshape(N_DEV, D // N_DEV, D), w_proj_a=dwpa, w_proj_b=dwpb))
    dya = proj_dx("proj_a_dx", dpa, wpa)
    dyb = proj_dx("proj_b_dx", dpb, wpb)

    dq_a, dk_a, dv_a = _attn_a_bwd(qkv, _behind(slopes, token), dya, ya, lse_a)
    dq_b, dk_b, dv_b, dtiles = _attn_b_bwd(qkv, tiles, dyb, yb, lse_b)
    drpb = _rpb_reduce(dtiles.reshape(N_HEADS_B * N_DR, GRID_W * GRID_W), sel)[:, :N_DC]
    dqkv = jnp.concatenate([dq_a[0], dq_a[1], dq_a[2], dq_b, dk_a[0], dk_a[1], dk_a[2], dk_b,
                            dv_a[0], dv_a[1], dv_a[2], dv_b], axis=1)
    dwq = in_dw("qkv_proj_dw", dqkv, 768)
    grad_x, dg1 = _mm("in_proj_dx", (T // TS, 1, N_DEV // 2),
                      [(dqkv, pl.BlockSpec((TS, 1536), lambda i, j, k: (i, k)), wq,
                        pl.BlockSpec((2, D, 768), lambda i, j, k: (k, 0, 0)), "nt_shards", None),
                       (dgate, pl.BlockSpec((TS, 1024), lambda i, j, k: (i, k)), wg,
                        pl.BlockSpec((2, D, 512), lambda i, j, k: (k, 0, 0)), "nt_shards", None)],
                      [(x, row_s), (r1, rvec), (_behind(norm_mix, send("in", dict(w_qkv=dwq))), gvec),
                       (dx2, row_s)],
                      [(_sds((T, D), F32), row_s), (_sds((1, D), F32), gvec)], norm_bwd_epi, (TS, D), PA)

    small = dict(norm_mix=dg1, b_gate=jnp.concatenate([dba, dbb], axis=1), rpb=drpb, norm_mlp=dg2, norm_final=dg3)
    return loss, grad_x, small


BIG = ("w_qkv", "w_gate", "w_proj_a", "w_proj_b", "w_out", "w_up", "w_down")
SMALL = ("norm_mix", "b_gate", "rpb", "norm_mlp", "norm_final")
SMALL_ROWS = 96


def _place():
    return lax.axis_index("x"), lax.axis_index("y"), lax.axis_index("c")


def _all_gather_weights(shards):
    n_arr = len(shards)

    def body(*refs):
        ins, outs = refs[:n_arr], refs[n_arr:2 * n_arr]
        send_sems, recv_sems, local_sems = refs[2 * n_arr:]
        x, y, c = _place()
        me, sibling = (x, y, c), (x, y, 1 - c)
        chips = [(1 - x, y), (x, 1 - y), (1 - x, 1 - y)]

        def copy(a, k, block, to, from_input=False):
            dst = outs[a].at[4 * block[0] + 2 * block[1] + block[2]]
            return pltpu.make_async_remote_copy(
                src_ref=ins[a] if from_input else dst, dst_ref=dst, send_sem=send_sems.at[7 * a + k],
                recv_sem=recv_sems.at[7 * a + k], device_id=to, device_id_type=MESH)

        mine = [pltpu.make_async_copy(ins[a], outs[a].at[4 * x + 2 * y + c], local_sems.at[a]) for a in range(n_arr)]
        for a in range(n_arr):
            rows = ins[a].shape[0] // LOCAL_CHUNKS
            for k in range(LOCAL_CHUNKS):
                pltpu.make_async_copy(ins[a].at[pl.ds(k * rows, rows)],
                                      outs[a].at[4 * x + 2 * y + c, pl.ds(k * rows, rows)], local_sems.at[a]).start()
        first = []
        for a in range(n_arr):
            first.append(copy(a, 0, me, sibling, True))
            first += [copy(a, 1 + j, me, (*chip, c), True) for j, chip in enumerate(chips)]
        for cp in first:
            cp.start()
        passed = []
        for j, chip in enumerate(chips):
            for a in range(n_arr):
                copy(a, 1 + j, (*chip, c), me).wait_recv()
                fwd = copy(a, 4 + j, (*chip, c), sibling)
                fwd.start()
                passed.append(fwd)
        for a in range(n_arr):
            copy(a, 0, sibling, me).wait_recv()
            for j, chip in enumerate(chips):
                copy(a, 4 + j, (*chip, 1 - c), me).wait_recv()
        for cp in first + passed:
            cp.wait_send()
        for cp in mine:
            cp.wait()

    any_spec = pl.BlockSpec(memory_space=pl.ANY)
    return pl.pallas_call(
        body, name="all_gather_weights", in_specs=[any_spec] * n_arr, out_specs=[any_spec] * n_arr,
        out_shape=[jax.ShapeDtypeStruct((N_DEV,) + s.shape, s.dtype) for s in shards],
        scratch_shapes=[pltpu.SemaphoreType.DMA((7 * n_arr,)), pltpu.SemaphoreType.DMA((7 * n_arr,)),
                        pltpu.SemaphoreType.DMA((n_arr,))])(*shards)


def _peer(x, y, c, mask):
    return x ^ (mask >> 2), y ^ ((mask >> 1) & 1), c ^ (mask & 1)


def _gather_small(small):
    def body(in_ref, out_ref, send_sems, recv_sems, local_sem):
        x, y, c = _place()
        me = 4 * x + 2 * y + c
        copies = [pltpu.make_async_copy(in_ref, out_ref.at[me], local_sem)]
        for mask in range(1, N_DEV):
            copies.append(pltpu.make_async_remote_copy(
                src_ref=in_ref, dst_ref=out_ref.at[me], send_sem=send_sems.at[mask - 1],
                recv_sem=recv_sems.at[mask - 1], device_id=_peer(x, y, c, mask), device_id_type=MESH))
        for cp in copies:
            cp.start()
        for cp in copies:
            cp.wait()

    any_spec = pl.BlockSpec(memory_space=pl.ANY)
    return pl.pallas_call(
        body, name="gather_small", in_specs=[any_spec], out_specs=any_spec,
        out_shape=jax.ShapeDtypeStruct((N_DEV,) + small.shape, small.dtype),
        scratch_shapes=[pltpu.SemaphoreType.DMA((7,)), pltpu.SemaphoreType.DMA((7,)), pltpu.SemaphoreType.DMA])(small)


_HBM = pl.BlockSpec(memory_space=pltpu.HBM)
_SEM = pl.BlockSpec(memory_space=pltpu.SEMAPHORE)
_EFFECT = pltpu.SideEffectType.DATAFLOW_SIDE_EFFECTING


def _send_copy(src_refs, land_refs, send_sems, recv_sems, a, mask, scatter, x, y, c):
    px, py, pc = _peer(x, y, c, mask)
    src = src_refs[a].at[4 * px + 2 * py + pc] if scatter else src_refs[a]
    return pltpu.make_async_remote_copy(
        src_ref=src, dst_ref=land_refs[a].at[4 * x + 2 * y + c], send_sem=send_sems[a].at[mask - 1],
        recv_sem=recv_sems[a].at[mask - 1], device_id=(px, py, pc), device_id_type=MESH)


def _local_copy(src_refs, land_refs, local_sems, a, scatter, x, y, c):
    me = 4 * x + 2 * y + c
    return pltpu.make_async_copy(src_refs[a].at[me] if scatter else src_refs[a], land_refs[a].at[me], local_sems[a].at[0])


def _send_start(name, srcs, scatter, after=None):
    n = len(srcs)
    n_in = 2 * n + (after is not None)
    shapes = [s.shape[1:] if scatter else s.shape for s in srcs]

    def body(*refs):
        src_refs, land_refs = refs[:n], refs[n:2 * n]
        outs = refs[n_in:]
        send_sems, recv_sems, local_sems = outs[:n], outs[n:2 * n], outs[2 * n:3 * n]
        token_ref = outs[5 * n]
        x, y, c = _place()
        for a in range(n):
            for mask in range(1, N_DEV):
                _send_copy(src_refs, land_refs, send_sems, recv_sems, a, mask, scatter, x, y, c).start()
        for a in range(n):
            _local_copy(src_refs, land_refs, local_sems, a, scatter, x, y, c).start()
        token_ref[...] = jnp.zeros_like(token_ref)

    lands = [pltpu.with_memory_space_constraint(lax.empty((N_DEV,) + sh, s.dtype), pltpu.HBM) for sh, s in zip(shapes, srcs)]
    srcs = [pltpu.with_memory_space_constraint(s, pltpu.HBM) for s in srcs]
    res = pl.pallas_call(
        body, name=name,
        out_shape=[pltpu.SemaphoreType.DMA((7,))] * (2 * n) + [pltpu.SemaphoreType.DMA((1,))] * n
        + [pltpu.HBM(s.shape, s.dtype) for s in srcs] + [pltpu.HBM(l.shape, l.dtype) for l in lands]
        + [jax.ShapeDtypeStruct((8, 128), F32)],
        in_specs=[_HBM] * (2 * n) + [pl.BlockSpec(memory_space=pl.ANY)] * (n_in - 2 * n),
        out_specs=[_SEM] * (3 * n) + [_HBM] * (2 * n) + [pl.BlockSpec(memory_space=pltpu.VMEM)],
        input_output_aliases={i: 3 * n + i for i in range(2 * n)},
        compiler_params=pltpu.CompilerParams(has_side_effects=_EFFECT))(*srcs, *lands, *([] if after is None else [after]))
    handles = [tuple(res[k * n + a] for k in range(5)) for a in range(n)]
    return handles, res[5 * n]


def _send_wait(name, handles, scatter, after):
    n = len(handles)

    def body(*refs):
        src_refs, land_refs = refs[:n], refs[n:2 * n]
        send_sems, recv_sems, local_sems = refs[2 * n:3 * n], refs[3 * n:4 * n], refs[4 * n:5 * n]
        x, y, c = _place()
        for a in range(n):
            for mask in range(1, N_DEV):
                cp = _send_copy(src_refs, land_refs, send_sems, recv_sems, a, mask, scatter, x, y, c)
                cp.wait_send()
                cp.wait_recv()
            _local_copy(src_refs, land_refs, local_sems, a, scatter, x, y, c).wait()

    srcs = [h[3] for h in handles]
    lands = [h[4] for h in handles]
    res = pl.pallas_call(
        body, name=name, out_shape=[pltpu.HBM(s.shape, s.dtype) for s in srcs] + [pltpu.HBM(l.shape, l.dtype) for l in lands],
        in_specs=[_HBM] * (2 * n) + [_SEM] * (3 * n) + [pl.BlockSpec(memory_space=pl.ANY)], out_specs=[_HBM] * (2 * n),
        input_output_aliases={i: i for i in range(2 * n)},
        compiler_params=pltpu.CompilerParams(has_side_effects=_EFFECT))(
            *srcs, *lands, *[h[0] for h in handles], *[h[1] for h in handles], *[h[2] for h in handles], after)
    return list(res[n:])


def _adamw(name, parts, w, m, v):
    K, n = w.shape
    tr = min(K, 128)

    def body(p_ref, w_ref, m_ref, v_ref, g_ref, d_ref, nm_ref, nv_ref):
        g = p_ref[0].astype(F32)
        for i in range(1, N_DEV):
            g = g + p_ref[i].astype(F32)
        nm = ADAM_B1 * m_ref[...] + (1.0 - ADAM_B1) * g
        nv = ADAM_B2 * v_ref[...] + (1.0 - ADAM_B2) * jnp.square(g)
        m_hat = nm / (1.0 - ADAM_B1 ** ADAM_STEP)
        v_hat = nv / (1.0 - ADAM_B2 ** ADAM_STEP)
        g_ref[...] = g
        d_ref[...] = -ADAM_LR * (m_hat / (jnp.sqrt(v_hat) + ADAM_EPS) + ADAM_WD * w_ref[...])
        nm_ref[...] = nm
        nv_ref[...] = nv

    blk = pl.BlockSpec((tr, n), lambda i: (i, 0))
    return _pc(body, name=name, grid=(K // tr,), in_specs=[pl.BlockSpec((N_DEV, tr, n), lambda i: (0, i, 0)), blk, blk, blk],
               out_specs=[blk] * 4, out_shape=[jax.ShapeDtypeStruct((K, n), F32)] * 4, sem=("parallel",))(parts, w, m, v)


def _pack_small(vals):
    flat = jnp.concatenate([v.reshape(-1) for v in vals])
    return jnp.pad(flat, (0, SMALL_ROWS * 128 - flat.shape[0])).reshape(SMALL_ROWS, 128)


def _unpack_small(packed, shapes):
    flat, out, off = packed.reshape(-1), [], 0
    for s in shapes:
        size = int(np.prod(s))
        out.append(flat[off:off + size].reshape(s))
        off += size
    return out


def kernel(x, norm_mix, w_qkv, w_gate, b_gate, rpb, w_proj_a, w_proj_b, w_out, norm_mlp, w_up, w_down, norm_final, loss_target, m_norm_mix, m_w_qkv, m_w_gate, m_b_gate, m_rpb, m_w_proj_a, m_w_proj_b, m_w_out, m_norm_mlp, m_w_up, m_w_down, m_norm_final, v_norm_mix, v_w_qkv, v_w_gate, v_b_gate, v_rpb, v_w_proj_a, v_w_proj_b, v_w_out, v_norm_mlp, v_w_up, v_w_down, v_norm_final):
    w = dict(norm_mix=norm_mix, w_qkv=w_qkv, w_gate=w_gate, b_gate=b_gate, rpb=rpb, w_proj_a=w_proj_a, w_proj_b=w_proj_b,
             w_out=w_out, norm_mlp=norm_mlp, w_up=w_up, w_down=w_down, norm_final=norm_final)
    m = dict(norm_mix=m_norm_mix, w_qkv=m_w_qkv, w_gate=m_w_gate, b_gate=m_b_gate, rpb=m_rpb, w_proj_a=m_w_proj_a,
             w_proj_b=m_w_proj_b, w_out=m_w_out, norm_mlp=m_norm_mlp, w_up=m_w_up, w_down=m_w_down, norm_final=m_norm_final)
    v = dict(norm_mix=v_norm_mix, w_qkv=v_w_qkv, w_gate=v_w_gate, b_gate=v_b_gate, rpb=v_rpb, w_proj_a=v_w_proj_a,
             w_proj_b=v_w_proj_b, w_out=v_w_out, norm_mlp=v_norm_mlp, w_up=v_w_up, w_down=v_w_down, norm_final=v_norm_final)
    order = ("norm_mix", "w_qkv", "w_gate", "b_gate", "rpb", "w_proj_a", "w_proj_b", "w_out", "norm_mlp", "w_up",
             "w_down", "norm_final")

    shard = {n: w[n][0].astype(BF16) for n in BIG}
    (wq,) = _all_gather_weights([shard["w_qkv"]])
    late_names = ("w_gate", "w_proj_a", "w_proj_b", "w_out", "w_up", "w_down")
    late, token = _send_start("gather_late_start", [shard[n] for n in late_names], False, wq)
    late = dict(zip(late_names, late))

    def late_weights(stage, after):
        names = {"gate": ("w_gate",), "proj": ("w_proj_a", "w_proj_b", "w_out"), "mlp": ("w_up", "w_down")}[stage]
        return _send_wait("gather_late_wait_" + stage, [late[n] for n in names], False, after)

    sent = {}

    def send(stage, grads):
        handles, tok = _send_start("grads_start_" + stage, list(grads.values()), True)
        sent[stage] = (tuple(grads), handles)
        return tok

    loss, grad_x, small = _local_step(x[0], loss_target[0], _behind(norm_mix, token), b_gate, rpb[0], norm_mlp, norm_final,
                                      wq, late_weights, send)

    grad, delta, new_m, new_v = {}, {}, {}, {}
    after = grad_x
    for stage in ("mlp", "proj", "in"):
        names, handles = sent[stage]
        for n, parts in zip(names, _send_wait("grads_wait_" + stage, handles, True, after)):
            res = _adamw("adamw_" + n, parts, w[n][0], m[n][0], v[n][0])
            grad[n], delta[n], new_m[n], new_v[n] = (r[None] for r in res)
            after = res[1]
    res = _adamw("adamw_small", _gather_small(_pack_small([small[n] for n in SMALL])),
                 _pack_small([w[n] for n in SMALL]), _pack_small([m[n] for n in SMALL]), _pack_small([v[n] for n in SMALL]))
    shapes = [w[n].shape for n in SMALL]
    for tree, packed in zip((grad, delta, new_m, new_v), res):
        for n, val in zip(SMALL, _unpack_small(packed, shapes)):
            tree[n] = val

    total = lax.psum(loss[0, 0], ("x", "y", "c"))
    return (total, grad_x[None], *[grad[n] for n in order], *[delta[n] for n in order],
            *[new_m[n] for n in order], *[new_v[n] for n in order])
```

```python
import numpy as np
import jax
import jax.numpy as jnp
from jax import lax
from jax.experimental import pallas as pl
from jax.experimental.pallas import tpu as pltpu

F32 = jnp.float32
BF16 = jnp.bfloat16
MESH = pl.DeviceIdType.MESH

D_MODEL = 2048
HEAD_DIM = 128
N_HEADS = 16
N_HEADS_A = 12
N_HEADS_B = 4
DILATIONS = (1, 4, 16)
HALF_WINDOW = 64
GRID_W = 64
NA_ROWS = 8
NA_COLS = 16
D_FF = 4 * D_MODEL
N_DEV = 8
EPS = 1e-6
NEG = -1e30
SCALE = HEAD_DIM ** -0.5
ADAM_LR, ADAM_B1, ADAM_B2, ADAM_EPS, ADAM_WD, ADAM_STEP = 0.001, 0.9, 0.999, 1e-08, 0.01, 10

VMEM_LIMIT = 56 * 1024 * 1024
TM = 512
TL = 1024
TS = 256
BQ_A = 128


def _pc(body, *, name, grid, in_specs, out_specs, out_shape, scratch=(), sem=None):
    return pl.pallas_call(
        body, name=name, grid=grid, in_specs=in_specs, out_specs=out_specs, out_shape=out_shape,
        scratch_shapes=list(scratch),
        compiler_params=pltpu.CompilerParams(dimension_semantics=sem, vmem_limit_bytes=VMEM_LIMIT))


_DOT_DIMS = {"nn": (((1,), (0,)), ((), ())), "nt": (((1,), (1,)), ((), ())), "tn": (((0,), (0,)), ((), ()))}


def _dot(a, b, mode, precision=None):
    if mode == "nt_shards":
        n = b.shape[2]
        out = _dot(a[:, :n], b[0], "nt")
        for s in range(1, b.shape[0]):
            out = out + _dot(a[:, s * n:(s + 1) * n], b[s], "nt")
        return out
    return lax.dot_general(a, b, _DOT_DIMS[mode], preferred_element_type=F32, precision=precision)


def _mm(name, grid, pairs, extras, outs, epilogue, acc_shape, sem):
    n_pairs, n_extra, n_out = len(pairs), len(extras), len(outs)
    k_axis = len(grid) - 1
    nk = grid[k_axis]

    def body(*refs):
        pids = [pl.program_id(ax) for ax in range(len(grid))]
        ab = refs[:2 * n_pairs]
        ex = refs[2 * n_pairs:2 * n_pairs + n_extra]
        o = refs[2 * n_pairs + n_extra:2 * n_pairs + n_extra + n_out]
        part = None
        for p, pair in enumerate(pairs):
            a = ab[2 * p][...]
            if pair[5] is not None:
                a = pair[5](a)
            d = _dot(a, ab[2 * p + 1][...], pair[4])
            part = d if part is None else part + d
        if nk == 1:
            epilogue(part, ex, o, pids)
        else:
            acc = refs[-1]
            k = pids[k_axis]

            @pl.when(k == 0)
            def _():
                acc[...] = part

            @pl.when(k > 0)
            def _():
                acc[...] += part

            @pl.when(k == nk - 1)
            def _():
                epilogue(acc[...], ex, o, pids)

    operands, in_specs = [], []
    for a, a_spec, b, b_spec, _, _ in pairs:
        operands += [a, b]
        in_specs += [a_spec, b_spec]
    for e, e_spec in extras:
        operands.append(e)
        in_specs.append(e_spec)
    res = _pc(body, name=name, grid=grid, in_specs=in_specs, out_specs=[s for _, s in outs],
              out_shape=[o for o, _ in outs], scratch=[pltpu.VMEM(acc_shape, F32)] if nk > 1 else [], sem=sem)(*operands)
    return res


def _rmsnorm_fwd(x, g, name):
    T, D = x.shape

    def body(x_ref, g_ref, h_ref, r_ref):
        xv = x_ref[...]
        r = lax.rsqrt(jnp.mean(xv * xv, axis=-1, keepdims=True) + EPS)
        h_ref[...] = (xv * r * g_ref[...]).astype(BF16)
        r_ref[...] = r

    return _pc(body, name=name, grid=(T // TM,),
               in_specs=[pl.BlockSpec((TM, D), lambda i: (i, 0)), pl.BlockSpec((1, D), lambda i: (0, 0))],
               out_specs=[pl.BlockSpec((TM, D), lambda i: (i, 0)), pl.BlockSpec((TM, 1), lambda i: (i, 0))],
               out_shape=[jax.ShapeDtypeStruct((T, D), BF16), jax.ShapeDtypeStruct((T, 1), F32)],
               sem=("parallel",))(x, g)


def _rmsnorm_bwd_block(dh, x, r, g, dres):
    xh = x * r
    dxh = dh * g
    dx = dres + r * (dxh - xh * jnp.mean(dxh * xh, axis=-1, keepdims=True))
    return dx, jnp.sum(dh * xh, axis=0, keepdims=True)


def _rmsnorm_bwd(name, dh, x, r, g, dres, want_bf16):
    T, D = x.shape

    def body(dh_ref, x_ref, r_ref, g_ref, dres_ref, *outs):
        dx, dgp = _rmsnorm_bwd_block(dh_ref[...], x_ref[...], r_ref[...], g_ref[...], dres_ref[...])
        outs[0][...] = dx
        if want_bf16:
            outs[1][...] = dx.astype(BF16)
        first = pl.program_id(0) == 0

        @pl.when(first)
        def _():
            outs[-1][...] = dgp

        @pl.when(jnp.logical_not(first))
        def _():
            outs[-1][...] += dgp

    row = pl.BlockSpec((TS, D), lambda i: (i, 0))
    vec = pl.BlockSpec((1, D), lambda i: (0, 0))
    n_row = 2 if want_bf16 else 1
    return _pc(body, name=name, grid=(T // TS,), in_specs=[row, row, pl.BlockSpec((TS, 1), lambda i: (i, 0)), vec, row],
               out_specs=[row] * n_row + [vec],
               out_shape=[jax.ShapeDtypeStruct((T, D), F32)] + [jax.ShapeDtypeStruct((T, D), BF16)] * (n_row - 1)
               + [jax.ShapeDtypeStruct((1, D), F32)], sem=("arbitrary",))(dh, x, r, g, dres)


def _final_norm_loss(x3, g, tgt):
    T, D = x3.shape

    def body(x_ref, g_ref, t_ref, loss_ref, dx_ref, dxb_ref, dg_ref):
        i = pl.program_id(0)
        xv = x_ref[...]
        gv = g_ref[...]
        r = lax.rsqrt(jnp.mean(xv * xv, axis=-1, keepdims=True) + EPS)
        xh = xv * r
        e = xh * gv - t_ref[...]
        part = 0.5 * jnp.sum(jnp.mean(e * e, axis=-1, keepdims=True))
        dy = e * (1.0 / D)
        dxh = dy * gv
        dx = r * (dxh - xh * jnp.mean(dxh * xh, axis=-1, keepdims=True))
        dx_ref[...] = dx
        dxb_ref[...] = dx.astype(BF16)
        dgp = jnp.sum(dy * xh, axis=0, keepdims=True)

        @pl.when(i == 0)
        def _():
            loss_ref[...] = jnp.zeros(loss_ref.shape, F32) + part
            dg_ref[...] = dgp

        @pl.when(i > 0)
        def _():
            loss_ref[...] += part
            dg_ref[...] += dgp

    row = pl.BlockSpec((TM, D), lambda i: (i, 0))
    vec = pl.BlockSpec((1, D), lambda i: (0, 0))
    return _pc(body, name="final_norm_loss", grid=(T // TM,), in_specs=[row, vec, row],
               out_specs=[pl.BlockSpec((8, 128), lambda i: (0, 0)), row, row, vec],
               out_shape=[jax.ShapeDtypeStruct((8, 128), F32), jax.ShapeDtypeStruct((T, D), F32),
                          jax.ShapeDtypeStruct((T, D), BF16), jax.ShapeDtypeStruct((1, D), F32)],
               sem=("arbitrary",))(x3, g, tgt)


def _alibi_slopes():
    return np.asarray(2.0 ** (-8.0 * np.arange(1, N_HEADS_A + 1) / N_HEADS_A), dtype=np.float32)


def _a_windows(T):
    half = [HALF_WINDOW * d for d in DILATIONS]
    return half, [min(BQ_A + 2 * w, T) for w in half]


def _a_scores(q, kw, slope, g, t0, ws, wk, half):
    d = DILATIONS[g]
    s = _dot(q, kw, "nt") * SCALE
    qi = t0 + lax.broadcasted_iota(jnp.int32, (BQ_A, wk), 0)
    ki = ws + lax.broadcasted_iota(jnp.int32, (BQ_A, wk), 1)
    dlt = ki - qi
    ad = jnp.maximum(dlt, -dlt)
    ok = (ad <= half) & ((ad & (d - 1)) == 0)
    return jnp.where(ok, s - slope * ad.astype(F32), NEG)


def _a_qkv_specs(T, inner):
    specs = []
    for g in range(3):
        specs.append(pl.BlockSpec((BQ_A, HEAD_DIM), lambda j, i, g=g: (i, 4 * g + j)))
        specs.append(pl.BlockSpec((T, HEAD_DIM), lambda j, i, g=g: (0, N_HEADS + 4 * g + j)))
        specs.append(pl.BlockSpec((T, HEAD_DIM), lambda j, i, g=g: (0, 2 * N_HEADS + 4 * g + j)))
    return specs


def _attn_a_fwd(qkv, slopes):
    T = qkv.shape[0]
    nq = T // BQ_A
    half, wks = _a_windows(T)

    def body(sl_ref, *refs):
        qkv_refs, (y_ref, lse_ref) = refs[:9], refs[9:]
        j = pl.program_id(0)
        t0 = pl.program_id(1) * BQ_A
        ss, vws = [], []
        for g in range(3):
            q_ref, k_ref, v_ref = qkv_refs[3 * g:3 * g + 3]
            ws = pl.multiple_of(jnp.clip(t0 - half[g], 0, T - wks[g]), 64)
            ss.append(_a_scores(q_ref[...], k_ref[pl.ds(ws, wks[g]), :], sl_ref[4 * g + j], g, t0, ws, wks[g], half[g]))
            vws.append(v_ref[pl.ds(ws, wks[g]), :])
        m = jnp.maximum(jnp.maximum(jnp.max(ss[0], axis=-1, keepdims=True), jnp.max(ss[1], axis=-1, keepdims=True)),
                        jnp.max(ss[2], axis=-1, keepdims=True))
        den = jnp.zeros((BQ_A, 1), F32)
        acc = jnp.zeros((BQ_A, HEAD_DIM), F32)
        for g in range(3):
            p = jnp.exp(ss[g] - m)
            den = den + jnp.sum(p, axis=-1, keepdims=True)
            acc = acc + _dot(p.astype(BF16), vws[g], "nn")
        y_ref[...] = (acc / den).astype(BF16)
        lse_ref[...] = m + jnp.log(den)

    return _pc(body, name="attn_a_fwd", grid=(4, nq),
               in_specs=[pl.BlockSpec(memory_space=pltpu.SMEM)] + _a_qkv_specs(T, nq),
               out_specs=[pl.BlockSpec((BQ_A, HEAD_DIM), lambda j, i: (i, j)),
                          pl.BlockSpec((None, BQ_A, 1), lambda j, i: (j, i, 0))],
               out_shape=[jax.ShapeDtypeStruct((T, 4 * HEAD_DIM), BF16), jax.ShapeDtypeStruct((4, T, 1), F32)],
               sem=("parallel", "parallel"))(slopes, *([qkv] * 9))


def _attn_a_bwd(qkv, slopes, dy, y, lse):
    T = qkv.shape[0]
    nq = T // BQ_A
    half, wks = _a_windows(T)

    def body(sl_ref, *refs):
        qkv_refs = refs[:9]
        dy_ref, y_ref, lse_ref = refs[9:12]
        dq_ref, dk_ref, dv_ref = refs[12:15]
        dk_acc, dv_acc = refs[15:]
        j = pl.program_id(0)
        i = pl.program_id(1)
        t0 = i * BQ_A

        @pl.when(i == 0)
        def _():
            dk_acc[...] = jnp.zeros_like(dk_acc)
            dv_acc[...] = jnp.zeros_like(dv_acc)

        dyv = dy_ref[...]
        delta = jnp.sum(dyv.astype(F32) * y_ref[...].astype(F32), axis=-1, keepdims=True)
        lse_v = lse_ref[...]
        for g in range(3):
            q_ref, k_ref, v_ref = qkv_refs[3 * g:3 * g + 3]
            ws = pl.multiple_of(jnp.clip(t0 - half[g], 0, T - wks[g]), 64)
            win = pl.ds(ws, wks[g])
            qv = q_ref[...]
            kw = k_ref[win, :]
            s = _a_scores(qv, kw, sl_ref[4 * g + j], g, t0, ws, wks[g], half[g])
            p = jnp.exp(s - lse_v)
            dp = _dot(dyv, v_ref[win, :], "nt")
            ds = (p * (dp - delta)).astype(BF16)
            dq_ref[g] = (_dot(ds, kw, "nn") * SCALE).astype(BF16)
            dk_acc[g, win, :] += _dot(ds, qv, "tn") * SCALE
            dv_acc[g, win, :] += _dot(p.astype(BF16), dyv, "tn")

        @pl.when(i == nq - 1)
        def _():
            dk_ref[...] = dk_acc[...].astype(BF16)
            dv_ref[...] = dv_acc[...].astype(BF16)

    blk = pl.BlockSpec((BQ_A, HEAD_DIM), lambda j, i: (i, j))
    whole = pl.BlockSpec((3, T, HEAD_DIM), lambda j, i: (0, 0, j))
    full = jax.ShapeDtypeStruct((3, T, 4 * HEAD_DIM), BF16)
    return _pc(body, name="attn_a_bwd", grid=(4, nq),
               in_specs=[pl.BlockSpec(memory_space=pltpu.SMEM)] + _a_qkv_specs(T, nq)
               + [blk, blk, pl.BlockSpec((None, BQ_A, 1), lambda j, i: (j, i, 0))],
               out_specs=[pl.BlockSpec((3, BQ_A, HEAD_DIM), lambda j, i: (0, i, j)), whole, whole],
               out_shape=[full, full, full],
               scratch=[pltpu.VMEM((3, T, HEAD_DIM), F32), pltpu.VMEM((3, T, HEAD_DIM), F32)],
               sem=("parallel", "arbitrary"))(slopes, *([qkv] * 9), dy, y, lse)


N_DR = 2 * NA_ROWS - 1
N_DC = 2 * NA_COLS - 1
WK_B = NA_ROWS * GRID_W


def _toeplitz_consts():
    col = np.arange(GRID_W)
    start = np.clip(col - NA_COLS // 2, 0, GRID_W - NA_COLS)
    ok = (col[None, :] >= start[:, None]) & (col[None, :] < start[:, None] + NA_COLS)
    dc = np.clip(col[None, :] - col[:, None], -(NA_COLS - 1), NA_COLS - 1) + NA_COLS - 1
    sel = np.zeros((128, GRID_W * GRID_W), np.float32)
    sel[dc.reshape(-1), np.arange(GRID_W * GRID_W)] = ok.reshape(-1).astype(np.float32)
    mask = np.where(ok, 0.0, NEG).astype(np.float32).reshape(1, -1)
    return sel, mask


def _rpb_expand(rpb_rows, sel, mask):
    def body(r_ref, s_ref, m_ref, o_ref):
        o_ref[...] = _dot(r_ref[...], s_ref[...], "nn", precision=lax.Precision.HIGHEST) + m_ref[...]

    n = rpb_rows.shape[0]
    return _pc(body, name="rpb_expand", grid=(1,),
               in_specs=[pl.BlockSpec((n, 128), lambda i: (0, 0)), pl.BlockSpec(sel.shape, lambda i: (0, 0)),
                         pl.BlockSpec(mask.shape, lambda i: (0, 0))],
               out_specs=pl.BlockSpec((n, sel.shape[1]), lambda i: (0, 0)),
               out_shape=jax.ShapeDtypeStruct((n, sel.shape[1]), F32), sem=("arbitrary",))(rpb_rows, sel, mask)


def _rpb_reduce(dtiles, sel):
    def body(t_ref, s_ref, o_ref):
        o_ref[...] = _dot(t_ref[...], s_ref[...], "nt", precision=lax.Precision.HIGHEST)

    n = dtiles.shape[0]
    return _pc(body, name="rpb_reduce", grid=(1,),
               in_specs=[pl.BlockSpec(dtiles.shape, lambda i: (0, 0)), pl.BlockSpec(sel.shape, lambda i: (0, 0))],
               out_specs=pl.BlockSpec((n, 128), lambda i: (0, 0)),
               out_shape=jax.ShapeDtypeStruct((n, 128), F32), sem=("arbitrary",))(dtiles, sel)


def _b_specs(T):
    return [pl.BlockSpec((GRID_W, HEAD_DIM), lambda h, r: (r, N_HEADS_A + h)),
            pl.BlockSpec((T, HEAD_DIM), lambda h, r: (0, N_HEADS + N_HEADS_A + h)),
            pl.BlockSpec((T, HEAD_DIM), lambda h, r: (0, 2 * N_HEADS + N_HEADS_A + h)),
            pl.BlockSpec((None, N_DR, GRID_W, GRID_W), lambda h, r: (h, 0, 0, 0))]


def _b_row_start(r, rows):
    return jnp.clip(r - NA_ROWS // 2, 0, rows - NA_ROWS)


def _attn_b_fwd(qkv, tiles):
    T = qkv.shape[0]
    rows = T // GRID_W

    def body(q_ref, k_ref, v_ref, t_ref, y_ref, lse_ref, bias_ref):
        r = pl.program_id(1)
        rs = _b_row_start(r, rows)
        win = pl.ds(pl.multiple_of(rs * GRID_W, GRID_W), WK_B)
        for a in range(NA_ROWS):
            bias_ref[:, a * GRID_W:(a + 1) * GRID_W] = t_ref[rs + a - r + NA_ROWS - 1]
        s = _dot(q_ref[...], k_ref[win, :], "nt") * SCALE + bias_ref[...]
        m = jnp.max(s, axis=-1, keepdims=True)
        p = jnp.exp(s - m)
        den = jnp.sum(p, axis=-1, keepdims=True)
        y_ref[...] = (_dot(p.astype(BF16), v_ref[win, :], "nn") / den).astype(BF16)
        lse_ref[...] = m + jnp.log(den)

    return _pc(body, name="attn_b_fwd", grid=(N_HEADS_B, rows), in_specs=_b_specs(T),
               out_specs=[pl.BlockSpec((GRID_W, HEAD_DIM), lambda h, r: (r, h)),
                          pl.BlockSpec((None, GRID_W, 1), lambda h, r: (h, r, 0))],
               out_shape=[jax.ShapeDtypeStruct((T, N_HEADS_B * HEAD_DIM), BF16),
                          jax.ShapeDtypeStruct((N_HEADS_B, T, 1), F32)],
               scratch=[pltpu.VMEM((GRID_W, WK_B), F32)], sem=("parallel", "parallel"))(qkv, qkv, qkv, tiles)


def _attn_b_bwd(qkv, tiles, dy, y, lse):
    T = qkv.shape[0]
    rows = T // GRID_W

    def body(q_ref, k_ref, v_ref, t_ref, dy_ref, y_ref, lse_ref, dq_ref, dk_ref, dv_ref, dt_ref,
             bias_ref, ds_ref, dk_acc, dv_acc):
        r = pl.program_id(1)
        rs = _b_row_start(r, rows)
        win = pl.ds(pl.multiple_of(rs * GRID_W, GRID_W), WK_B)

        @pl.when(r == 0)
        def _():
            dk_acc[...] = jnp.zeros_like(dk_acc)
            dv_acc[...] = jnp.zeros_like(dv_acc)
            dt_ref[...] = jnp.zeros_like(dt_ref)

        for a in range(NA_ROWS):
            bias_ref[:, a * GRID_W:(a + 1) * GRID_W] = t_ref[rs + a - r + NA_ROWS - 1]
        qv = q_ref[...]
        kw = k_ref[win, :]
        dyv = dy_ref[...]
        s = _dot(qv, kw, "nt") * SCALE + bias_ref[...]
        p = jnp.exp(s - lse_ref[...])
        delta = jnp.sum(dyv.astype(F32) * y_ref[...].astype(F32), axis=-1, keepdims=True)
        ds = p * (_dot(dyv, v_ref[win, :], "nt") - delta)
        ds_ref[...] = ds
        for a in range(NA_ROWS):
            dt_ref[rs + a - r + NA_ROWS - 1] += ds_ref[:, a * GRID_W:(a + 1) * GRID_W]
        dsb = ds.astype(BF16)
        dq_ref[...] = (_dot(dsb, kw, "nn") * SCALE).astype(BF16)
        dk_acc[win, :] += _dot(dsb, qv, "tn") * SCALE
        dv_acc[win, :] += _dot(p.astype(BF16), dyv, "tn")

        @pl.when(r == rows - 1)
        def _():
            dk_ref[...] = dk_acc[...].astype(BF16)
            dv_ref[...] = dv_acc[...].astype(BF16)

    blk = pl.BlockSpec((GRID_W, HEAD_DIM), lambda h, r: (r, h))
    whole = pl.BlockSpec((T, HEAD_DIM), lambda h, r: (0, h))
    full = jax.ShapeDtypeStruct((T, N_HEADS_B * HEAD_DIM), BF16)
    return _pc(body, name="attn_b_bwd", grid=(N_HEADS_B, rows),
               in_specs=_b_specs(T) + [blk, blk, pl.BlockSpec((None, GRID_W, 1), lambda h, r: (h, r, 0))],
               out_specs=[blk, whole, whole, pl.BlockSpec((None, N_DR, GRID_W, GRID_W), lambda h, r: (h, 0, 0, 0))],
               out_shape=[full, full, full, jax.ShapeDtypeStruct((N_HEADS_B, N_DR, GRID_W, GRID_W), F32)],
               scratch=[pltpu.VMEM((GRID_W, WK_B), F32), pltpu.VMEM((GRID_W, WK_B), F32),
                        pltpu.VMEM((T, HEAD_DIM), F32), pltpu.VMEM((T, HEAD_DIM), F32)],
               sem=("parallel", "arbitrary"))(qkv, qkv, qkv, tiles, dy, y, lse)


def _sds(shape, dtype):
    return jax.ShapeDtypeStruct(shape, dtype)


def _behind(value, token):
    return value if token is None else value + token[0, 0]


def _local_step(x, tgt, norm_mix, b_gate, rpb, norm_mlp, norm_final, late_weights, send):
    T, D = x.shape
    nt = T // TM
    PP, PA = ("parallel", "parallel", "arbitrary"), ("arbitrary", "arbitrary", "arbitrary")
    slopes = jnp.asarray(_alibi_slopes())
    sel_np, mask_np = _toeplitz_consts()
    sel, mask = jnp.asarray(sel_np), jnp.asarray(mask_np)

    def rows(w, tm=TM):
        return pl.BlockSpec((tm, w), lambda i, j, k: (i, j))

    def full_rows(tm, w=D):
        return pl.BlockSpec((tm, w), lambda i, j, k: (i, 0))

    def tall(w):
        return pl.BlockSpec((T, w), lambda j, i, k: (0, i))

    def tall_j(w):
        return pl.BlockSpec((T, w), lambda j, i, k: (0, j))

    rvec = pl.BlockSpec((TS, 1), lambda i, j, k: (i, 0))
    gvec = pl.BlockSpec((1, D), lambda i, j, k: (0, 0))

    h1, r1 = _rmsnorm_fwd(x, norm_mix, "rmsnorm_mix")
    (wq,) = late_weights("qkv", h1)

    def store_bf16(acc, ex, o, pids):
        o[0][...] = acc.astype(BF16)

    (qkv,) = _mm("qkv_proj", (T // TL, N_DEV, 1),
                 [(h1, full_rows(TL), wq, pl.BlockSpec((None, D, 768), lambda i, j, k: (j, 0, 0)), "nn", None)],
                 [], [(_sds((T, 3 * D), BF16), rows(768, TL))], store_bf16, None, PP)

    def gate_epi(acc, ex, o, pids):
        o[0][...] = jax.nn.sigmoid(acc + ex[0][...])

    late_weights("attention", qkv)
    rpb_rows = jnp.pad(rpb.reshape(N_HEADS_B * N_DR, N_DC), ((0, 0), (0, 128 - N_DC)))
    tiles = _rpb_expand(rpb_rows, sel, mask).reshape(N_HEADS_B, N_DR, GRID_W, GRID_W)
    ya, lse_a = _attn_a_fwd(qkv, slopes)
    yb, lse_b = _attn_b_fwd(qkv, tiles)

    wg, wpa, wpb, wo = late_weights("mixer", yb)
    (gates,) = _mm("gate_proj", (T // TL, N_DEV, 1),
                   [(h1, full_rows(TL), wg, pl.BlockSpec((None, D, 512), lambda i, j, k: (j, 0, 0)), "nn", None)],
                   [(b_gate, pl.BlockSpec((1, 512), lambda i, j, k: (0, j)))],
                   [(_sds((T, 2 * D), F32), rows(512, TL))], gate_epi, None, PP)

    def merge_body(ya_ref, wa_ref, yb_ref, wb_ref, ga_ref, gb_ref, pa_ref, pb_ref, mg_ref):
        ya_v, yb_v = ya_ref[...], yb_ref[...]
        for s in range(N_DEV):
            cols = slice(s * 256, (s + 1) * 256)
            pa = _dot(ya_v, wa_ref[s], "nn")
            pb = _dot(yb_v, wb_ref[s], "nn")
            pa_ref[:, cols] = pa
            pb_ref[:, cols] = pb
            mg_ref[:, cols] = (ga_ref[:, cols] * pa + gb_ref[:, cols] * pb).astype(BF16)

    wo2 = wo.reshape(D, D)
    y_spec = pl.BlockSpec((TS, 512), lambda i: (i, 0))
    wp_spec = pl.BlockSpec((N_DEV, 512, 256), lambda i: (0, 0, 0))
    c_spec = pl.BlockSpec((TS, D), lambda i: (i, 0))
    pa, pb, merged = _pc(merge_body, name="merge_proj", grid=(T // TS,),
                         in_specs=[y_spec, wp_spec, y_spec, wp_spec, c_spec, pl.BlockSpec((TS, D), lambda i: (i, 1))],
                         out_specs=[c_spec, c_spec, c_spec],
                         out_shape=[_sds((T, D), F32), _sds((T, D), F32), _sds((T, D), BF16)],
                         sem=("parallel",))(ya, wpa, yb, wpb, gates, gates)

    def add_res(acc, ex, o, pids):
        o[0][...] = ex[0][...] + acc

    (x2,) = _mm("out_proj", (T // TL, 4, 1),
                [(merged, full_rows(TL), wo2, pl.BlockSpec((D, 512), lambda i, j, k: (0, j)), "nn", None)],
                [(x, rows(512, TL))], [(_sds((T, D), F32), rows(512, TL))], add_res, None, PP)

    h2, r2 = _rmsnorm_fwd(x2, norm_mlp, "rmsnorm_mlp")
    wu, wd = late_weights("mlp", h2)
    wd2 = wd.reshape(D_FF, D)

    def up_epi(acc, ex, o, pids):
        o[0][...] = acc.astype(BF16)
        r = jnp.maximum(acc, 0.0)
        o[1][...] = (r * r).astype(BF16)

    u, act = _mm("mlp_up", (T // TL, N_DEV, 1),
                 [(h2, full_rows(TL), wu, pl.BlockSpec((None, D, 1024), lambda i, j, k: (j, 0, 0)), "nn", None)],
                 [], [(_sds((T, D_FF), BF16), rows(1024, TL))] * 2, up_epi, None, PP)
    (x3,) = _mm("mlp_down", (nt, D // 256, 1),
                [(act, full_rows(TM, D_FF), wd2, pl.BlockSpec((D_FF, 256), lambda i, j, k: (0, j)), "nn", None)],
                [(x2, rows(256))], [(_sds((T, D), F32), rows(256))], add_res, None, PP)

    loss, dx3, dx3b, dg3 = _final_norm_loss(x3, norm_final.reshape(1, D), tgt)

    def du_epi(acc, ex, o, pids):
        o[0][...] = (acc * (2.0 * jnp.maximum(ex[0][...].astype(F32), 0.0))).astype(BF16)

    (du,) = _mm("mlp_down_dx", (T // TL, N_DEV, 1),
                [(dx3b, full_rows(TL), wd2, pl.BlockSpec((1024, D), lambda i, j, k: (j, 0)), "nt", None)],
                [(u, rows(1024, TL))], [(_sds((T, D_FF), BF16), rows(1024, TL))], du_epi, None, PP)
    (dwd,) = _mm("mlp_down_dw", (D // 1024, D_FF // 512, 1), [(act, tall(512), dx3b, tall_j(1024), "tn", None)],
                 [], [(_sds((D_FF, D), BF16), pl.BlockSpec((512, 1024), lambda j, i, k: (i, j)))], store_bf16, None, PP)
    dwd = dwd.reshape(N_DEV, D_FF // N_DEV, D)
    (dwu,) = _mm("mlp_up_dw", (N_DEV, D // 512, 1), [(h2, tall(512), du, tall_j(1024), "tn", None)],
                 [], [(_sds((N_DEV, D, 1024), BF16), pl.BlockSpec((None, 512, 1024), lambda j, i, k: (j, i, 0)))],
                 store_bf16, None, PP)

    def store_f32(acc, ex, o, pids):
        o[0][...] = acc

    out_ji = pl.BlockSpec((TM, 512), lambda j, i, k: (i, j))
    a_ji = lambda w: pl.BlockSpec((TM, w), lambda j, i, k: (i, 0))
    w_ji = lambda n: pl.BlockSpec((N_DEV, 512, n), lambda j, i, k: (0, j, 0))
    behind = lambda token: [] if token is None else [(token, pl.BlockSpec(memory_space=pl.ANY))]
    (dh2,) = _mm("mlp_up_dx", (D // 512, nt, 1), [(du, a_ji(D_FF), wu, w_ji(1024), "nt_shards", None)],
                 behind(send("mlp", dict(w_down=dwd, w_up=dwu))), [(_sds((T, D), F32), out_ji)], store_f32, None, PP)
    dx2, dx2b, dg2 = _rmsnorm_bwd("rmsnorm_mlp_bwd", dh2, x2, r2, norm_mlp, dx3, True)

    (dwo,) = _mm("out_proj_dw", (D // 1024, D // 512, 1), [(merged, tall(512), dx2b, tall_j(1024), "tn", None)],
                 [], [(_sds((D, D), BF16), pl.BlockSpec((512, 1024), lambda j, i, k: (i, j)))], store_bf16, None, PP)

    def dmerge_epi(acc, ex, o, pids):
        ga, gb, pa_v, pb_v = (e[...] for e in ex)
        o[0][...] = (acc * ga).astype(BF16)
        o[1][...] = (acc * gb).astype(BF16)
        dga = acc * pa_v * ga * (1.0 - ga)
        dgb = acc * pb_v * gb * (1.0 - gb)
        o[2][...] = dga.astype(BF16)
        o[3][...] = dgb.astype(BF16)
        sa = jnp.sum(dga, axis=0, keepdims=True)
        sb = jnp.sum(dgb, axis=0, keepdims=True)
        first = pids[1] == 0

        @pl.when(first)
        def _():
            o[4][...] = sa
            o[5][...] = sb

        @pl.when(jnp.logical_not(first))
        def _():
            o[4][...] += sa
            o[5][...] += sb

    cj = pl.BlockSpec((TM, 512), lambda j, i, k: (i, j))
    bj = pl.BlockSpec((1, 512), lambda j, i, k: (0, j))
    dpa, dpb, dga, dgb, dba, dbb = _mm(
        "out_proj_dx", (4, nt, 1),
        [(dx2b, pl.BlockSpec((TM, D), lambda j, i, k: (i, 0)), wo2, pl.BlockSpec((512, D), lambda j, i, k: (j, 0)), "nt", None)],
        [(gates, cj), (gates, pl.BlockSpec((TM, 512), lambda j, i, k: (i, 4 + j))), (pa, cj), (pb, cj)],
        [(_sds((T, D), BF16), cj)] * 4 + [(_sds((1, D), F32), bj)] * 2, dmerge_epi, None,
        ("parallel", "arbitrary", "arbitrary"))

    def store_shards(acc, ex, o, pids):
        for s in range(4):
            o[0][s] = acc[:, s * 256:(s + 1) * 256].astype(BF16)

    def proj_dw(name, y, dp):
        (dw,) = _mm(name, (2, 1, 1), [(y, tall(512), dp, tall_j(1024), "tn", None)],
                    [], [(_sds((N_DEV, 512, 256), BF16), pl.BlockSpec((4, 512, 256), lambda j, i, k: (j, 0, 0)))],
                    store_shards, None, PP)
        return dw

    def proj_dx(name, dp, w):
        (dy,) = _mm(name, (nt, 1, 1),
                    [(dp, full_rows(TM), w, pl.BlockSpec((N_DEV, 512, 256), lambda i, j, k: (0, 0, 0)), "nt_shards", None)],
                    [], [(_sds((T, 512), BF16), full_rows(TM, 512))], store_bf16, None, PP)
        return dy

    def in_dw(name, dact, n):
        (dw,) = _mm(name, (N_DEV, D // 512, 1), [(h1, tall(512), dact, tall_j(n), "tn", None)],
                    [], [(_sds((N_DEV, D, n), BF16), pl.BlockSpec((None, 512, n), lambda j, i, k: (j, i, 0)))],
                    store_bf16, None, PP)
        return dw

    dgate = jnp.concatenate([dga, dgb], axis=1)
    dwg = in_dw("gate_proj_dw", dgate, 512)
    dwpa = proj_dw("proj_a_dw", ya, dpa)
    dwpb = proj_dw("proj_b_dw", yb, dpb)
    token = send("proj", dict(w_gate=dwg, w_out=dwo.reshape(N_DEV, D // N_DEV, D), w_proj_a=dwpa, w_proj_b=dwpb))
    dya = proj_dx("proj_a_dx", dpa, wpa)
    dyb = proj_dx("proj_b_dx", dpb, wpb)

    dq_a, dk_a, dv_a = _attn_a_bwd(qkv, _behind(slopes, token), dya, ya, lse_a)
    dq_b, dk_b, dv_b, dtiles = _attn_b_bwd(qkv, tiles, dyb, yb, lse_b)
    drpb = _rpb_reduce(dtiles.reshape(N_HEADS_B * N_DR, GRID_W * GRID_W), sel)[:, :N_DC]
    dqkv = jnp.concatenate([dq_a[0], dq_a[1], dq_a[2], dq_b, dk_a[0], dk_a[1], dk_a[2], dk_b,
                            dv_a[0], dv_a[1], dv_a[2], dv_b], axis=1)
    dwq = in_dw("qkv_proj_dw", dqkv, 768)
    (dh1,) = _mm("in_proj_dx", (D // 512, nt, 1),
                 [(dqkv, a_ji(3 * D), wq, w_ji(768), "nt_shards", None), (dgate, a_ji(2 * D), wg, w_ji(512), "nt_shards", None)],
                 behind(send("in", dict(w_qkv=dwq))), [(_sds((T, D), F32), out_ji)], store_f32, None, PP)
    grad_x, dg1 = _rmsnorm_bwd("rmsnorm_mix_bwd", dh1, x, r1, norm_mix, dx2, False)

    small = dict(norm_mix=dg1, b_gate=jnp.concatenate([dba, dbb], axis=1), rpb=drpb, norm_mlp=dg2, norm_final=dg3)
    return loss, grad_x, small


BIG = ("w_qkv", "w_gate", "w_proj_a", "w_proj_b", "w_out", "w_up", "w_down")
SMALL = ("norm_mix", "b_gate", "rpb", "norm_mlp", "norm_final")
SMALL_ROWS = 96


def _place():
    return lax.axis_index("x"), lax.axis_index("y"), lax.axis_index("c")


def _peer(x, y, c, mask):
    return x ^ (mask >> 2), y ^ ((mask >> 1) & 1), c ^ (mask & 1)


_HBM = pl.BlockSpec(memory_space=pltpu.HBM)
_SEM = pl.BlockSpec(memory_space=pltpu.SEMAPHORE)
_ANY = pl.BlockSpec(memory_space=pl.ANY)
_EFFECT = pltpu.SideEffectType.DATAFLOW_SIDE_EFFECTING


def _hbm(a):
    return pltpu.with_memory_space_constraint(a, pltpu.HBM)


def _other_chips(x, y):
    return [(1 - x, y), (x, 1 - y), (1 - x, 1 - y)]


def _block(x, y, c):
    return 4 * x + 2 * y + c


def _rdma(src, dst, send_sem, recv_sem, to):
    return pltpu.make_async_remote_copy(src_ref=src, dst_ref=dst, send_sem=send_sem, recv_sem=recv_sem, device_id=to,
                                        device_id_type=MESH)


def _gather_start(name, shards):
    n = len(shards)

    def body(*refs):
        srcs, lands = refs[:n], refs[n:2 * n]
        outs = refs[2 * n:]
        send, recv_d2d, recv_ici, local = outs[:n], outs[n:2 * n], outs[2 * n:3 * n], outs[3 * n:4 * n]
        token_ref = outs[6 * n]
        x, y, c = _place()
        for a in range(n):
            mine = lands[a].at[_block(x, y, c)]
            _rdma(srcs[a], mine, send[a].at[0], recv_d2d[a].at[0], (x, y, 1 - c)).start()
            for j, chip in enumerate(_other_chips(x, y)):
                _rdma(srcs[a], mine, send[a].at[1 + j], recv_ici[a].at[j], (*chip, c)).start()
        for a in range(n):
            pltpu.make_async_copy(srcs[a], lands[a].at[_block(x, y, c)], local[a].at[0]).start()
        token_ref[...] = jnp.zeros_like(token_ref)

    lands = [_hbm(lax.empty((N_DEV,) + s.shape, s.dtype)) for s in shards]
    srcs = [_hbm(s) for s in shards]
    dma = pltpu.SemaphoreType.DMA
    res = pl.pallas_call(
        body, name=name,
        out_shape=[dma((4,))] * n + [dma((1,))] * n + [dma((3,))] * n + [dma((1,))] * n
        + [pltpu.HBM(s.shape, s.dtype) for s in srcs] + [pltpu.HBM(l.shape, l.dtype) for l in lands]
        + [jax.ShapeDtypeStruct((8, 128), F32)],
        in_specs=[_HBM] * (2 * n), out_specs=[_SEM] * (4 * n) + [_HBM] * (2 * n) + [pl.BlockSpec(memory_space=pltpu.VMEM)],
        input_output_aliases={i: 4 * n + i for i in range(2 * n)},
        compiler_params=pltpu.CompilerParams(has_side_effects=_EFFECT))(*srcs, *lands)
    keys = ("send", "recv_d2d", "recv_ici", "local", "src", "land")
    return [dict(zip(keys, (res[k * n + a] for k in range(6)))) for a in range(n)], res[6 * n]


def _gather_forward(name, handles, after):
    n = len(handles)

    def body(*refs):
        lands, recv_ici = refs[:n], refs[n:2 * n]
        outs = refs[2 * n + 1:]
        fwd_send, fwd_recv = outs[n:2 * n], outs[2 * n:3 * n]
        x, y, c = _place()
        for a in range(n):
            for j, chip in enumerate(_other_chips(x, y)):
                blk = lands[a].at[_block(*chip, c)]
                _rdma(blk, blk, fwd_send[a].at[j], recv_ici[a].at[j], (*chip, c)).wait_recv()
                _rdma(blk, blk, fwd_send[a].at[j], fwd_recv[a].at[j], (x, y, 1 - c)).start()

    dma = pltpu.SemaphoreType.DMA
    lands = [h["land"] for h in handles]
    res = pl.pallas_call(
        body, name=name, out_shape=[pltpu.HBM(l.shape, l.dtype) for l in lands] + [dma((3,))] * (2 * n),
        in_specs=[_HBM] * n + [_SEM] * n + [_ANY], out_specs=[_HBM] * n + [_SEM] * (2 * n),
        input_output_aliases={i: i for i in range(n)},
        compiler_params=pltpu.CompilerParams(has_side_effects=_EFFECT))(*lands, *[h["recv_ici"] for h in handles], after)
    return [dict(h, land=res[a], fwd_send=res[n + a], fwd_recv=res[2 * n + a]) for a, h in enumerate(handles)]


def _gather_wait(name, handles, after):
    n = len(handles)

    def body(*refs):
        srcs, lands = refs[:n], refs[n:2 * n]
        send, recv_d2d, local, fwd_send, fwd_recv = (refs[(2 + k) * n:(3 + k) * n] for k in range(5))
        x, y, c = _place()
        sibling = (x, y, 1 - c)
        for a in range(n):
            mine = lands[a].at[_block(x, y, c)]
            _rdma(srcs[a], mine, send[a].at[0], recv_d2d[a].at[0], sibling).wait_send()
            _rdma(srcs[a], lands[a].at[_block(*sibling)], send[a].at[0], recv_d2d[a].at[0], sibling).wait_recv()
            pltpu.make_async_copy(srcs[a], mine, local[a].at[0]).wait()
            for j, chip in enumerate(_other_chips(x, y)):
                _rdma(srcs[a], mine, send[a].at[1 + j], fwd_recv[a].at[j], (*chip, c)).wait_send()
                blk = lands[a].at[_block(*chip, c)]
                _rdma(blk, blk, fwd_send[a].at[j], fwd_recv[a].at[j], sibling).wait_send()
                got = lands[a].at[_block(*chip, 1 - c)]
                _rdma(got, got, fwd_send[a].at[j], fwd_recv[a].at[j], sibling).wait_recv()

    srcs = [h["src"] for h in handles]
    lands = [h["land"] for h in handles]
    sems = [h[k] for k in ("send", "recv_d2d", "local", "fwd_send", "fwd_recv") for h in handles]
    res = pl.pallas_call(
        body, name=name, out_shape=[pltpu.HBM(s.shape, s.dtype) for s in srcs] + [pltpu.HBM(l.shape, l.dtype) for l in lands],
        in_specs=[_HBM] * (2 * n) + [_SEM] * (5 * n) + [_ANY], out_specs=[_HBM] * (2 * n),
        input_output_aliases={i: i for i in range(2 * n)},
        compiler_params=pltpu.CompilerParams(has_side_effects=_EFFECT))(*srcs, *lands, *sems, after)
    return list(res[n:])


def _gather_small(small):
    def body(in_ref, out_ref, send_sems, recv_sems, local_sem):
        x, y, c = _place()
        me = 4 * x + 2 * y + c
        copies = [pltpu.make_async_copy(in_ref, out_ref.at[me], local_sem)]
        for mask in range(1, N_DEV):
            copies.append(pltpu.make_async_remote_copy(
                src_ref=in_ref, dst_ref=out_ref.at[me], send_sem=send_sems.at[mask - 1],
                recv_sem=recv_sems.at[mask - 1], device_id=_peer(x, y, c, mask), device_id_type=MESH))
        for cp in copies:
            cp.start()
        for cp in copies:
            cp.wait()

    any_spec = pl.BlockSpec(memory_space=pl.ANY)
    return pl.pallas_call(
        body, name="gather_small", in_specs=[any_spec], out_specs=any_spec,
        out_shape=jax.ShapeDtypeStruct((N_DEV,) + small.shape, small.dtype),
        scratch_shapes=[pltpu.SemaphoreType.DMA((7,)), pltpu.SemaphoreType.DMA((7,)), pltpu.SemaphoreType.DMA])(small)


def _send_copy(src_refs, land_refs, send_sems, recv_sems, a, mask, scatter, x, y, c):
    px, py, pc = _peer(x, y, c, mask)
    src = src_refs[a].at[4 * px + 2 * py + pc] if scatter else src_refs[a]
    return pltpu.make_async_remote_copy(
        src_ref=src, dst_ref=land_refs[a].at[4 * x + 2 * y + c], send_sem=send_sems[a].at[mask - 1],
        recv_sem=recv_sems[a].at[mask - 1], device_id=(px, py, pc), device_id_type=MESH)


def _local_copy(src_refs, land_refs, local_sems, a, scatter, x, y, c):
    me = 4 * x + 2 * y + c
    return pltpu.make_async_copy(src_refs[a].at[me] if scatter else src_refs[a], land_refs[a].at[me], local_sems[a].at[0])


def _send_start(name, srcs, scatter, after=None):
    n = len(srcs)
    n_in = 2 * n + (after is not None)
    shapes = [s.shape[1:] if scatter else s.shape for s in srcs]

    def body(*refs):
        src_refs, land_refs = refs[:n], refs[n:2 * n]
        outs = refs[n_in:]
        send_sems, recv_sems, local_sems = outs[:n], outs[n:2 * n], outs[2 * n:3 * n]
        token_ref = outs[5 * n]
        x, y, c = _place()
        for a in range(n):
            for mask in range(1, N_DEV):
                _send_copy(src_refs, land_refs, send_sems, recv_sems, a, mask, scatter, x, y, c).start()
        for a in range(n):
            _local_copy(src_refs, land_refs, local_sems, a, scatter, x, y, c).start()
        token_ref[...] = jnp.zeros_like(token_ref)

    lands = [pltpu.with_memory_space_constraint(lax.empty((N_DEV,) + sh, s.dtype), pltpu.HBM) for sh, s in zip(shapes, srcs)]
    srcs = [pltpu.with_memory_space_constraint(s, pltpu.HBM) for s in srcs]
    res = pl.pallas_call(
        body, name=name,
        out_shape=[pltpu.SemaphoreType.DMA((7,))] * (2 * n) + [pltpu.SemaphoreType.DMA((1,))] * n
        + [pltpu.HBM(s.shape, s.dtype) for s in srcs] + [pltpu.HBM(l.shape, l.dtype) for l in lands]
        + [jax.ShapeDtypeStruct((8, 128), F32)],
        in_specs=[_HBM] * (2 * n) + [pl.BlockSpec(memory_space=pl.ANY)] * (n_in - 2 * n),
        out_specs=[_SEM] * (3 * n) + [_HBM] * (2 * n) + [pl.BlockSpec(memory_space=pltpu.VMEM)],
        input_output_aliases={i: 3 * n + i for i in range(2 * n)},
        compiler_params=pltpu.CompilerParams(has_side_effects=_EFFECT))(*srcs, *lands, *([] if after is None else [after]))
    handles = [tuple(res[k * n + a] for k in range(5)) for a in range(n)]
    return handles, res[5 * n]


def _send_wait(name, handles, scatter, after):
    n = len(handles)

    def body(*refs):
        src_refs, land_refs = refs[:n], refs[n:2 * n]
        send_sems, recv_sems, local_sems = refs[2 * n:3 * n], refs[3 * n:4 * n], refs[4 * n:5 * n]
        x, y, c = _place()
        for a in range(n):
            for mask in range(1, N_DEV):
                cp = _send_copy(src_refs, land_refs, send_sems, recv_sems, a, mask, scatter, x, y, c)
                cp.wait_send()
                cp.wait_recv()
            _local_copy(src_refs, land_refs, local_sems, a, scatter, x, y, c).wait()

    srcs = [h[3] for h in handles]
    lands = [h[4] for h in handles]
    res = pl.pallas_call(
        body, name=name, out_shape=[pltpu.HBM(s.shape, s.dtype) for s in srcs] + [pltpu.HBM(l.shape, l.dtype) for l in lands],
        in_specs=[_HBM] * (2 * n) + [_SEM] * (3 * n) + [pl.BlockSpec(memory_space=pl.ANY)], out_specs=[_HBM] * (2 * n),
        input_output_aliases={i: i for i in range(2 * n)},
        compiler_params=pltpu.CompilerParams(has_side_effects=_EFFECT))(
            *srcs, *lands, *[h[0] for h in handles], *[h[1] for h in handles], *[h[2] for h in handles], after)
    return list(res[n:])


def _adamw(name, parts, w, m, v):
    K, n = w.shape
    tr = min(K, 128)

    def body(p_ref, w_ref, m_ref, v_ref, g_ref, d_ref, nm_ref, nv_ref):
        g = p_ref[0].astype(F32)
        for i in range(1, N_DEV):
            g = g + p_ref[i].astype(F32)
        nm = ADAM_B1 * m_ref[...] + (1.0 - ADAM_B1) * g
        nv = ADAM_B2 * v_ref[...] + (1.0 - ADAM_B2) * jnp.square(g)
        m_hat = nm / (1.0 - ADAM_B1 ** ADAM_STEP)
        v_hat = nv / (1.0 - ADAM_B2 ** ADAM_STEP)
        g_ref[...] = g
        d_ref[...] = -ADAM_LR * (m_hat / (jnp.sqrt(v_hat) + ADAM_EPS) + ADAM_WD * w_ref[...])
        nm_ref[...] = nm
        nv_ref[...] = nv

    blk = pl.BlockSpec((tr, n), lambda i: (i, 0))
    return _pc(body, name=name, grid=(K // tr,), in_specs=[pl.BlockSpec((N_DEV, tr, n), lambda i: (0, i, 0)), blk, blk, blk],
               out_specs=[blk] * 4, out_shape=[jax.ShapeDtypeStruct((K, n), F32)] * 4, sem=("parallel",))(parts, w, m, v)


def _pack_small(vals):
    flat = jnp.concatenate([v.reshape(-1) for v in vals])
    return jnp.pad(flat, (0, SMALL_ROWS * 128 - flat.shape[0])).reshape(SMALL_ROWS, 128)


def _unpack_small(packed, shapes):
    flat, out, off = packed.reshape(-1), [], 0
    for s in shapes:
        size = int(np.prod(s))
        out.append(flat[off:off + size].reshape(s))
        off += size
    return out


def kernel(x, norm_mix, w_qkv, w_gate, b_gate, rpb, w_proj_a, w_proj_b, w_out, norm_mlp, w_up, w_down, norm_final, loss_target, m_norm_mix, m_w_qkv, m_w_gate, m_b_gate, m_rpb, m_w_proj_a, m_w_proj_b, m_w_out, m_norm_mlp, m_w_up, m_w_down, m_norm_final, v_norm_mix, v_w_qkv, v_w_gate, v_b_gate, v_rpb, v_w_proj_a, v_w_proj_b, v_w_out, v_norm_mlp, v_w_up, v_w_down, v_norm_final):
    w = dict(norm_mix=norm_mix, w_qkv=w_qkv, w_gate=w_gate, b_gate=b_gate, rpb=rpb, w_proj_a=w_proj_a, w_proj_b=w_proj_b,
             w_out=w_out, norm_mlp=norm_mlp, w_up=w_up, w_down=w_down, norm_final=norm_final)
    m = dict(norm_mix=m_norm_mix, w_qkv=m_w_qkv, w_gate=m_w_gate, b_gate=m_b_gate, rpb=m_rpb, w_proj_a=m_w_proj_a,
             w_proj_b=m_w_proj_b, w_out=m_w_out, norm_mlp=m_norm_mlp, w_up=m_w_up, w_down=m_w_down, norm_final=m_norm_final)
    v = dict(norm_mix=v_norm_mix, w_qkv=v_w_qkv, w_gate=v_w_gate, b_gate=v_b_gate, rpb=v_rpb, w_proj_a=v_w_proj_a,
             w_proj_b=v_w_proj_b, w_out=v_w_out, norm_mlp=v_norm_mlp, w_up=v_w_up, w_down=v_w_down, norm_final=v_norm_final)
    order = ("norm_mix", "w_qkv", "w_gate", "b_gate", "rpb", "w_proj_a", "w_proj_b", "w_out", "norm_mlp", "w_up",
             "w_down", "norm_final")

    shard = {n: w[n][0].astype(BF16) for n in BIG}
    stages = dict(qkv=("w_qkv",), gate=("w_gate", "w_proj_a", "w_proj_b", "w_out"), mlp=("w_up", "w_down"))
    names = [n for s in stages.values() for n in s]
    handles, token = _gather_start("gather_start", [shard[n] for n in names])
    handles = dict(zip(names, handles))

    def late_weights(stage, after):
        if stage == "qkv":
            return tuple(_gather_wait("gather_wait_qkv", _gather_forward("gather_forward_qkv", [handles["w_qkv"]], after), after))
        if stage == "attention":
            handles["gate"] = _gather_forward("gather_forward_gate", [handles[n] for n in stages["gate"]], after)
            return ()
        if stage == "mixer":
            handles["mlp"] = _gather_forward("gather_forward_mlp", [handles[n] for n in stages["mlp"]], after)
            return tuple(_gather_wait("gather_wait_gate", handles["gate"], after))
        return tuple(_gather_wait("gather_wait_mlp", handles["mlp"], after))

    sent = {}

    def send(stage, grads):
        handles, tok = _send_start("grads_start_" + stage, list(grads.values()), True)
        sent[stage] = (tuple(grads), handles)
        return tok

    loss, grad_x, small = _local_step(x[0], loss_target[0], _behind(norm_mix, token), b_gate, rpb[0], norm_mlp, norm_final,
                                      late_weights, send)

    grad, delta, new_m, new_v = {}, {}, {}, {}
    after = grad_x
    for stage in ("mlp", "proj", "in"):
        names, handles = sent[stage]
        for n, parts in zip(names, _send_wait("grads_wait_" + stage, handles, True, after)):
            res = _adamw("adamw_" + n, parts, w[n][0], m[n][0], v[n][0])
            grad[n], delta[n], new_m[n], new_v[n] = (r[None] for r in res)
            after = res[1]
    res = _adamw("adamw_small", _gather_small(_pack_small([small[n] for n in SMALL])),
                 _pack_small([w[n] for n in SMALL]), _pack_small([m[n] for n in SMALL]), _pack_small([v[n] for n in SMALL]))
    shapes = [w[n].shape for n in SMALL]
    for tree, packed in zip((grad, delta, new_m, new_v), res):
        for n, val in zip(SMALL, _unpack_small(packed, shapes)):
            tree[n] = val

    total = lax.psum(loss[0, 0], ("x", "y", "c"))
    return (total, grad_x[None], *[grad[n] for n in order], *[delta[n] for n in order],
            *[new_m[n] for n in order], *[new_v[n] for n in order])
```

```python
import numpy as np
import jax
import jax.numpy as jnp
from jax import lax
from jax.experimental import pallas as pl
from jax.experimental.pallas import tpu as pltpu

F32 = jnp.float32
BF16 = jnp.bfloat16
MESH = pl.DeviceIdType.MESH

D_MODEL = 2048
HEAD_DIM = 128
N_HEADS = 16
N_HEADS_A = 12
N_HEADS_B = 4
DILATIONS = (1, 4, 16)
HALF_WINDOW = 64
GRID_W = 64
NA_ROWS = 8
NA_COLS = 16
D_FF = 4 * D_MODEL
N_DEV = 8
EPS = 1e-6
NEG = -1e30
SCALE = HEAD_DIM ** -0.5
ADAM_LR, ADAM_B1, ADAM_B2, ADAM_EPS, ADAM_WD, ADAM_STEP = 0.001, 0.9, 0.999, 1e-08, 0.01, 10

VMEM_LIMIT = 56 * 1024 * 1024
TM = 512
TL = 1024
TS = 256
BQ_A = 128


def _pc(body, *, name, grid, in_specs, out_specs, out_shape, scratch=(), sem=None):
    return pl.pallas_call(
        body, name=name, grid=grid, in_specs=in_specs, out_specs=out_specs, out_shape=out_shape,
        scratch_shapes=list(scratch),
        compiler_params=pltpu.CompilerParams(dimension_semantics=sem, vmem_limit_bytes=VMEM_LIMIT))


_DOT_DIMS = {"nn": (((1,), (0,)), ((), ())), "nt": (((1,), (1,)), ((), ())), "tn": (((0,), (0,)), ((), ()))}


def _dot(a, b, mode, precision=None):
    if mode == "nt_shards":
        n = b.shape[2]
        out = _dot(a[:, :n], b[0], "nt")
        for s in range(1, b.shape[0]):
            out = out + _dot(a[:, s * n:(s + 1) * n], b[s], "nt")
        return out
    return lax.dot_general(a, b, _DOT_DIMS[mode], preferred_element_type=F32, precision=precision)


def _mm(name, grid, pairs, extras, outs, epilogue, acc_shape, sem):
    n_pairs, n_extra, n_out = len(pairs), len(extras), len(outs)
    k_axis = len(grid) - 1
    nk = grid[k_axis]

    def body(*refs):
        pids = [pl.program_id(ax) for ax in range(len(grid))]
        ab = refs[:2 * n_pairs]
        ex = refs[2 * n_pairs:2 * n_pairs + n_extra]
        o = refs[2 * n_pairs + n_extra:2 * n_pairs + n_extra + n_out]
        part = None
        for p, pair in enumerate(pairs):
            a = ab[2 * p][...]
            if pair[5] is not None:
                a = pair[5](a)
            d = _dot(a, ab[2 * p + 1][...], pair[4])
            part = d if part is None else part + d
        if nk == 1:
            epilogue(part, ex, o, pids)
        else:
            acc = refs[-1]
            k = pids[k_axis]

            @pl.when(k == 0)
            def _():
                acc[...] = part

            @pl.when(k > 0)
            def _():
                acc[...] += part

            @pl.when(k == nk - 1)
            def _():
                epilogue(acc[...], ex, o, pids)

    operands, in_specs = [], []
    for a, a_spec, b, b_spec, _, _ in pairs:
        operands += [a, b]
        in_specs += [a_spec, b_spec]
    for e, e_spec in extras:
        operands.append(e)
        in_specs.append(e_spec)
    res = _pc(body, name=name, grid=grid, in_specs=in_specs, out_specs=[s for _, s in outs],
              out_shape=[o for o, _ in outs], scratch=[pltpu.VMEM(acc_shape, F32)] if nk > 1 else [], sem=sem)(*operands)
    return res


def _rmsnorm_fwd(x, g, name):
    T, D = x.shape

    def body(x_ref, g_ref, h_ref, r_ref):
        xv = x_ref[...]
        r = lax.rsqrt(jnp.mean(xv * xv, axis=-1, keepdims=True) + EPS)
        h_ref[...] = (xv * r * g_ref[...]).astype(BF16)
        r_ref[...] = r

    return _pc(body, name=name, grid=(T // TM,),
               in_specs=[pl.BlockSpec((TM, D), lambda i: (i, 0)), pl.BlockSpec((1, D), lambda i: (0, 0))],
               out_specs=[pl.BlockSpec((TM, D), lambda i: (i, 0)), pl.BlockSpec((TM, 1), lambda i: (i, 0))],
               out_shape=[jax.ShapeDtypeStruct((T, D), BF16), jax.ShapeDtypeStruct((T, 1), F32)],
               sem=("parallel",))(x, g)


def _rmsnorm_bwd_block(dh, x, r, g, dres):
    xh = x * r
    dxh = dh * g
    dx = dres + r * (dxh - xh * jnp.mean(dxh * xh, axis=-1, keepdims=True))
    return dx, jnp.sum(dh * xh, axis=0, keepdims=True)


def _rmsnorm_bwd(name, dh, x, r, g, dres, want_bf16):
    T, D = x.shape

    def body(dh_ref, x_ref, r_ref, g_ref, dres_ref, *outs):
        dx, dgp = _rmsnorm_bwd_block(dh_ref[...], x_ref[...], r_ref[...], g_ref[...], dres_ref[...])
        outs[0][...] = dx
        if want_bf16:
            outs[1][...] = dx.astype(BF16)
        first = pl.program_id(0) == 0

        @pl.when(first)
        def _():
            outs[-1][...] = dgp

        @pl.when(jnp.logical_not(first))
        def _():
            outs[-1][...] += dgp

    row = pl.BlockSpec((TS, D), lambda i: (i, 0))
    vec = pl.BlockSpec((1, D), lambda i: (0, 0))
    n_row = 2 if want_bf16 else 1
    return _pc(body, name=name, grid=(T // TS,), in_specs=[row, row, pl.BlockSpec((TS, 1), lambda i: (i, 0)), vec, row],
               out_specs=[row] * n_row + [vec],
               out_shape=[jax.ShapeDtypeStruct((T, D), F32)] + [jax.ShapeDtypeStruct((T, D), BF16)] * (n_row - 1)
               + [jax.ShapeDtypeStruct((1, D), F32)], sem=("arbitrary",))(dh, x, r, g, dres)


def _final_norm_loss(x3, g, tgt):
    T, D = x3.shape

    def body(x_ref, g_ref, t_ref, loss_ref, dx_ref, dxb_ref, dg_ref):
        i = pl.program_id(0)
        xv = x_ref[...]
        gv = g_ref[...]
        r = lax.rsqrt(jnp.mean(xv * xv, axis=-1, keepdims=True) + EPS)
        xh = xv * r
        e = xh * gv - t_ref[...]
        part = 0.5 * jnp.sum(jnp.mean(e * e, axis=-1, keepdims=True))
        dy = e * (1.0 / D)
        dxh = dy * gv
        dx = r * (dxh - xh * jnp.mean(dxh * xh, axis=-1, keepdims=True))
        dx_ref[...] = dx
        dxb_ref[...] = dx.astype(BF16)
        dgp = jnp.sum(dy * xh, axis=0, keepdims=True)

        @pl.when(i == 0)
        def _():
            loss_ref[...] = jnp.zeros(loss_ref.shape, F32) + part
            dg_ref[...] = dgp

        @pl.when(i > 0)
        def _():
            loss_ref[...] += part
            dg_ref[...] += dgp

    row = pl.BlockSpec((TM, D), lambda i: (i, 0))
    vec = pl.BlockSpec((1, D), lambda i: (0, 0))
    return _pc(body, name="final_norm_loss", grid=(T // TM,), in_specs=[row, vec, row],
               out_specs=[pl.BlockSpec((8, 128), lambda i: (0, 0)), row, row, vec],
               out_shape=[jax.ShapeDtypeStruct((8, 128), F32), jax.ShapeDtypeStruct((T, D), F32),
                          jax.ShapeDtypeStruct((T, D), BF16), jax.ShapeDtypeStruct((1, D), F32)],
               sem=("arbitrary",))(x3, g, tgt)


def _alibi_slopes():
    return np.asarray(2.0 ** (-8.0 * np.arange(1, N_HEADS_A + 1) / N_HEADS_A), dtype=np.float32)


def _a_windows(T):
    half = [HALF_WINDOW * d for d in DILATIONS]
    return half, [min(BQ_A + 2 * w, T) for w in half]


def _a_bias(slope, g, offset, wk, half):
    d = DILATIONS[g]
    dlt = offset + lax.broadcasted_iota(jnp.int32, (BQ_A, wk), 1) - lax.broadcasted_iota(jnp.int32, (BQ_A, wk), 0)
    ad = jnp.maximum(dlt, -dlt)
    ok = (ad <= half) & ((ad & (d - 1)) == 0)
    return jnp.where(ok, -slope * ad.astype(F32), NEG)


def _a_scores(q, kw, slope, g, t0, ws, wk, half, tile_ref):
    s = _dot(q, kw, "nt") * SCALE
    return lax.cond(ws - t0 == -half, lambda: s + tile_ref[...], lambda: s + _a_bias(slope, g, ws - t0, wk, half))


def _a_fill_tiles(first_block, sl_ref, j, tiles, wks, half):
    @pl.when(first_block)
    def _():
        for g in range(3):
            tiles[g][...] = _a_bias(sl_ref[4 * g + j], g, -half[g], wks[g], half[g])


def _a_qkv_specs(T, inner):
    specs = []
    for g in range(3):
        specs.append(pl.BlockSpec((BQ_A, HEAD_DIM), lambda j, i, g=g: (i, 4 * g + j)))
        specs.append(pl.BlockSpec((T, HEAD_DIM), lambda j, i, g=g: (0, N_HEADS + 4 * g + j)))
        specs.append(pl.BlockSpec((T, HEAD_DIM), lambda j, i, g=g: (0, 2 * N_HEADS + 4 * g + j)))
    return specs


def _attn_a_fwd(qkv, slopes):
    T = qkv.shape[0]
    nq = T // BQ_A
    half, wks = _a_windows(T)

    def body(sl_ref, *refs):
        qkv_refs, (y_ref, lse_ref), tiles = refs[:9], refs[9:11], refs[11:]
        j = pl.program_id(0)
        t0 = pl.program_id(1) * BQ_A
        _a_fill_tiles(pl.program_id(1) == 0, sl_ref, j, tiles, wks, half)
        ss, vws = [], []
        for g in range(3):
            q_ref, k_ref, v_ref = qkv_refs[3 * g:3 * g + 3]
            ws = pl.multiple_of(jnp.clip(t0 - half[g], 0, T - wks[g]), 64)
            ss.append(_a_scores(q_ref[...], k_ref[pl.ds(ws, wks[g]), :], sl_ref[4 * g + j], g, t0, ws, wks[g], half[g],
                                tiles[g]))
            vws.append(v_ref[pl.ds(ws, wks[g]), :])
        m = jnp.maximum(jnp.maximum(jnp.max(ss[0], axis=-1, keepdims=True), jnp.max(ss[1], axis=-1, keepdims=True)),
                        jnp.max(ss[2], axis=-1, keepdims=True))
        den = jnp.zeros((BQ_A, 1), F32)
        acc = jnp.zeros((BQ_A, HEAD_DIM), F32)
        for g in range(3):
            p = jnp.exp(ss[g] - m)
            den = den + jnp.sum(p, axis=-1, keepdims=True)
            acc = acc + _dot(p.astype(BF16), vws[g], "nn")
        y_ref[...] = (acc / den).astype(BF16)
        lse_ref[...] = m + jnp.log(den)

    return _pc(body, name="attn_a_fwd", grid=(4, nq),
               in_specs=[pl.BlockSpec(memory_space=pltpu.SMEM)] + _a_qkv_specs(T, nq),
               out_specs=[pl.BlockSpec((BQ_A, HEAD_DIM), lambda j, i: (i, j)),
                          pl.BlockSpec((None, BQ_A, 1), lambda j, i: (j, i, 0))],
               out_shape=[jax.ShapeDtypeStruct((T, 4 * HEAD_DIM), BF16), jax.ShapeDtypeStruct((4, T, 1), F32)],
               scratch=[pltpu.VMEM((BQ_A, wk), F32) for wk in wks], sem=("parallel", "arbitrary"))(slopes, *([qkv] * 9))


def _attn_a_bwd(qkv, slopes, dy, y, lse):
    T = qkv.shape[0]
    nq = T // BQ_A
    half, wks = _a_windows(T)

    def body(sl_ref, *refs):
        qkv_refs = refs[:9]
        dy_ref, y_ref, lse_ref = refs[9:12]
        dq_ref, dk_ref, dv_ref = refs[12:15]
        dk_acc, dv_acc = refs[15:17]
        tiles = refs[17:]
        j = pl.program_id(0)
        i = pl.program_id(1)
        t0 = i * BQ_A
        _a_fill_tiles(i == 0, sl_ref, j, tiles, wks, half)

        @pl.when(i == 0)
        def _():
            dk_acc[...] = jnp.zeros_like(dk_acc)
            dv_acc[...] = jnp.zeros_like(dv_acc)

        dyv = dy_ref[...]
        delta = jnp.sum(dyv.astype(F32) * y_ref[...].astype(F32), axis=-1, keepdims=True)
        lse_v = lse_ref[...]
        for g in range(3):
            q_ref, k_ref, v_ref = qkv_refs[3 * g:3 * g + 3]
            ws = pl.multiple_of(jnp.clip(t0 - half[g], 0, T - wks[g]), 64)
            win = pl.ds(ws, wks[g])
            qv = q_ref[...]
            kw = k_ref[win, :]
            s = _a_scores(qv, kw, sl_ref[4 * g + j], g, t0, ws, wks[g], half[g], tiles[g])
            p = jnp.exp(s - lse_v)
            dp = _dot(dyv, v_ref[win, :], "nt")
            ds = (p * (dp - delta)).astype(BF16)
            dq_ref[g] = (_dot(ds, kw, "nn") * SCALE).astype(BF16)
            dk_acc[g, win, :] += _dot(ds, qv, "tn") * SCALE
            dv_acc[g, win, :] += _dot(p.astype(BF16), dyv, "tn")

        @pl.when(i == nq - 1)
        def _():
            dk_ref[...] = dk_acc[...].astype(BF16)
            dv_ref[...] = dv_acc[...].astype(BF16)

    blk = pl.BlockSpec((BQ_A, HEAD_DIM), lambda j, i: (i, j))
    whole = pl.BlockSpec((3, T, HEAD_DIM), lambda j, i: (0, 0, j))
    full = jax.ShapeDtypeStruct((3, T, 4 * HEAD_DIM), BF16)
    return _pc(body, name="attn_a_bwd", grid=(4, nq),
               in_specs=[pl.BlockSpec(memory_space=pltpu.SMEM)] + _a_qkv_specs(T, nq)
               + [blk, blk, pl.BlockSpec((None, BQ_A, 1), lambda j, i: (j, i, 0))],
               out_specs=[pl.BlockSpec((3, BQ_A, HEAD_DIM), lambda j, i: (0, i, j)), whole, whole],
               out_shape=[full, full, full],
               scratch=[pltpu.VMEM((3, T, HEAD_DIM), F32), pltpu.VMEM((3, T, HEAD_DIM), F32)]
               + [pltpu.VMEM((BQ_A, wk), F32) for wk in wks],
               sem=("parallel", "arbitrary"))(slopes, *([qkv] * 9), dy, y, lse)


N_DR = 2 * NA_ROWS - 1
N_DC = 2 * NA_COLS - 1
WK_B = NA_ROWS * GRID_W


def _toeplitz_consts():
    col = np.arange(GRID_W)
    start = np.clip(col - NA_COLS // 2, 0, GRID_W - NA_COLS)
    ok = (col[None, :] >= start[:, None]) & (col[None, :] < start[:, None] + NA_COLS)
    dc = np.clip(col[None, :] - col[:, None], -(NA_COLS - 1), NA_COLS - 1) + NA_COLS - 1
    sel = np.zeros((128, GRID_W * GRID_W), np.float32)
    sel[dc.reshape(-1), np.arange(GRID_W * GRID_W)] = ok.reshape(-1).astype(np.float32)
    mask = np.where(ok, 0.0, NEG).astype(np.float32).reshape(1, -1)
    return sel, mask


def _rpb_expand(rpb_rows, sel, mask):
    def body(r_ref, s_ref, m_ref, o_ref):
        o_ref[...] = _dot(r_ref[...], s_ref[...], "nn", precision=lax.Precision.HIGHEST) + m_ref[...]

    n = rpb_rows.shape[0]
    return _pc(body, name="rpb_expand", grid=(1,),
               in_specs=[pl.BlockSpec((n, 128), lambda i: (0, 0)), pl.BlockSpec(sel.shape, lambda i: (0, 0)),
                         pl.BlockSpec(mask.shape, lambda i: (0, 0))],
               out_specs=pl.BlockSpec((n, sel.shape[1]), lambda i: (0, 0)),
               out_shape=jax.ShapeDtypeStruct((n, sel.shape[1]), F32), sem=("arbitrary",))(rpb_rows, sel, mask)


def _rpb_reduce(dtiles, sel):
    def body(t_ref, s_ref, o_ref):
        o_ref[...] = _dot(t_ref[...], s_ref[...], "nt", precision=lax.Precision.HIGHEST)

    n = dtiles.shape[0]
    return _pc(body, name="rpb_reduce", grid=(1,),
               in_specs=[pl.BlockSpec(dtiles.shape, lambda i: (0, 0)), pl.BlockSpec(sel.shape, lambda i: (0, 0))],
               out_specs=pl.BlockSpec((n, 128), lambda i: (0, 0)),
               out_shape=jax.ShapeDtypeStruct((n, 128), F32), sem=("arbitrary",))(dtiles, sel)


ROWS_B = 4
BQ_B = ROWS_B * GRID_W


def _b_specs(T):
    return [pl.BlockSpec((BQ_B, HEAD_DIM), lambda h, r: (r, N_HEADS_A + h)),
            pl.BlockSpec((T, HEAD_DIM), lambda h, r: (0, N_HEADS + N_HEADS_A + h)),
            pl.BlockSpec((T, HEAD_DIM), lambda h, r: (0, 2 * N_HEADS + N_HEADS_A + h)),
            pl.BlockSpec((None, N_DR, GRID_W, GRID_W), lambda h, r: (h, 0, 0, 0))]


def _b_row(step, u, rows, t_ref, bias_ref):
    r = step * ROWS_B + u
    rs = jnp.clip(r - NA_ROWS // 2, 0, rows - NA_ROWS)
    first_tile = rs - r + NA_ROWS - 1
    for a in range(NA_ROWS):
        bias_ref[u, :, a * GRID_W:(a + 1) * GRID_W] = t_ref[first_tile + a]
    return slice(u * GRID_W, (u + 1) * GRID_W), pl.ds(pl.multiple_of(rs * GRID_W, GRID_W), WK_B), first_tile


def _attn_b_fwd(qkv, tiles):
    T = qkv.shape[0]
    rows = T // GRID_W

    def body(q_ref, k_ref, v_ref, t_ref, y_ref, lse_ref, bias_ref):
        for u in range(ROWS_B):
            blk, win, _ = _b_row(pl.program_id(1), u, rows, t_ref, bias_ref)
            s = _dot(q_ref[blk, :], k_ref[win, :], "nt") * SCALE + bias_ref[u]
            m = jnp.max(s, axis=-1, keepdims=True)
            p = jnp.exp(s - m)
            den = jnp.sum(p, axis=-1, keepdims=True)
            y_ref[blk, :] = (_dot(p.astype(BF16), v_ref[win, :], "nn") / den).astype(BF16)
            lse_ref[blk, :] = m + jnp.log(den)

    return _pc(body, name="attn_b_fwd", grid=(N_HEADS_B, rows // ROWS_B), in_specs=_b_specs(T),
               out_specs=[pl.BlockSpec((BQ_B, HEAD_DIM), lambda h, r: (r, h)),
                          pl.BlockSpec((None, BQ_B, 1), lambda h, r: (h, r, 0))],
               out_shape=[jax.ShapeDtypeStruct((T, N_HEADS_B * HEAD_DIM), BF16),
                          jax.ShapeDtypeStruct((N_HEADS_B, T, 1), F32)],
               scratch=[pltpu.VMEM((ROWS_B, GRID_W, WK_B), F32)], sem=("parallel", "parallel"))(qkv, qkv, qkv, tiles)


def _attn_b_bwd(qkv, tiles, dy, y, lse):
    T = qkv.shape[0]
    rows = T // GRID_W
    steps = rows // ROWS_B

    def body(q_ref, k_ref, v_ref, t_ref, dy_ref, y_ref, lse_ref, dq_ref, dk_ref, dv_ref, dt_ref,
             bias_ref, ds_ref, dk_acc, dv_acc):
        step = pl.program_id(1)

        @pl.when(step == 0)
        def _():
            dk_acc[...] = jnp.zeros_like(dk_acc)
            dv_acc[...] = jnp.zeros_like(dv_acc)
            dt_ref[...] = jnp.zeros_like(dt_ref)

        for u in range(ROWS_B):
            blk, win, first_tile = _b_row(step, u, rows, t_ref, bias_ref)
            qv = q_ref[blk, :]
            kw = k_ref[win, :]
            dyv = dy_ref[blk, :]
            s = _dot(qv, kw, "nt") * SCALE + bias_ref[u]
            p = jnp.exp(s - lse_ref[blk, :])
            delta = jnp.sum(dyv.astype(F32) * y_ref[blk, :].astype(F32), axis=-1, keepdims=True)
            ds = p * (_dot(dyv, v_ref[win, :], "nt") - delta)
            ds_ref[u] = ds
            for a in range(NA_ROWS):
                dt_ref[first_tile + a] += ds_ref[u, :, a * GRID_W:(a + 1) * GRID_W]
            dsb = ds.astype(BF16)
            dq_ref[blk, :] = (_dot(dsb, kw, "nn") * SCALE).astype(BF16)
            dk_acc[win, :] += _dot(dsb, qv, "tn") * SCALE
            dv_acc[win, :] += _dot(p.astype(BF16), dyv, "tn")

        @pl.when(step == steps - 1)
        def _():
            dk_ref[...] = dk_acc[...].astype(BF16)
            dv_ref[...] = dv_acc[...].astype(BF16)

    blk_spec = pl.BlockSpec((BQ_B, HEAD_DIM), lambda h, r: (r, h))
    whole = pl.BlockSpec((T, HEAD_DIM), lambda h, r: (0, h))
    full = jax.ShapeDtypeStruct((T, N_HEADS_B * HEAD_DIM), BF16)
    return _pc(body, name="attn_b_bwd", grid=(N_HEADS_B, steps),
               in_specs=_b_specs(T) + [blk_spec, blk_spec, pl.BlockSpec((None, BQ_B, 1), lambda h, r: (h, r, 0))],
               out_specs=[blk_spec, whole, whole, pl.BlockSpec((None, N_DR, GRID_W, GRID_W), lambda h, r: (h, 0, 0, 0))],
               out_shape=[full, full, full, jax.ShapeDtypeStruct((N_HEADS_B, N_DR, GRID_W, GRID_W), F32)],
               scratch=[pltpu.VMEM((ROWS_B, GRID_W, WK_B), F32), pltpu.VMEM((ROWS_B, GRID_W, WK_B), F32),
                        pltpu.VMEM((T, HEAD_DIM), F32), pltpu.VMEM((T, HEAD_DIM), F32)],
               sem=("parallel", "arbitrary"))(qkv, qkv, qkv, tiles, dy, y, lse)


def _sds(shape, dtype):
    return jax.ShapeDtypeStruct(shape, dtype)


def _behind(value, token):
    return value if token is None else value + token[0, 0]


def _local_step(x, tgt, norm_mix, b_gate, rpb, norm_mlp, norm_final, late_weights, send):
    T, D = x.shape
    nt = T // TM
    PP, PA = ("parallel", "parallel", "arbitrary"), ("arbitrary", "arbitrary", "arbitrary")
    slopes = jnp.asarray(_alibi_slopes())
    sel_np, mask_np = _toeplitz_consts()
    sel, mask = jnp.asarray(sel_np), jnp.asarray(mask_np)

    def rows(w, tm=TM):
        return pl.BlockSpec((tm, w), lambda i, j, k: (i, j))

    def full_rows(tm, w=D):
        return pl.BlockSpec((tm, w), lambda i, j, k: (i, 0))

    def tall(w):
        return pl.BlockSpec((T, w), lambda j, i, k: (0, i))

    def tall_j(w):
        return pl.BlockSpec((T, w), lambda j, i, k: (0, j))

    rvec = pl.BlockSpec((TS, 1), lambda i, j, k: (i, 0))
    gvec = pl.BlockSpec((1, D), lambda i, j, k: (0, 0))

    h1, r1 = _rmsnorm_fwd(x, norm_mix, "rmsnorm_mix")
    (wq,) = late_weights("qkv", h1)

    def store_bf16(acc, ex, o, pids):
        o[0][...] = acc.astype(BF16)

    (qkv,) = _mm("qkv_proj", (T // TL, N_DEV, 1),
                 [(h1, full_rows(TL), wq, pl.BlockSpec((None, D, 768), lambda i, j, k: (j, 0, 0)), "nn", None)],
                 [], [(_sds((T, 3 * D), BF16), rows(768, TL))], store_bf16, None, PP)

    def gate_epi(acc, ex, o, pids):
        o[0][...] = jax.nn.sigmoid(acc + ex[0][...])

    late_weights("attention", qkv)
    rpb_rows = jnp.pad(rpb.reshape(N_HEADS_B * N_DR, N_DC), ((0, 0), (0, 128 - N_DC)))
    tiles = _rpb_expand(rpb_rows, sel, mask).reshape(N_HEADS_B, N_DR, GRID_W, GRID_W)
    ya, lse_a = _attn_a_fwd(qkv, slopes)
    yb, lse_b = _attn_b_fwd(qkv, tiles)

    wg, wpa, wpb, wo = late_weights("mixer", yb)
    (gates,) = _mm("gate_proj", (T // TL, N_DEV, 1),
                   [(h1, full_rows(TL), wg, pl.BlockSpec((None, D, 512), lambda i, j, k: (j, 0, 0)), "nn", None)],
                   [(b_gate, pl.BlockSpec((1, 512), lambda i, j, k: (0, j)))],
                   [(_sds((T, 2 * D), F32), rows(512, TL))], gate_epi, None, PP)

    def merge_body(ya_ref, wa_ref, yb_ref, wb_ref, ga_ref, gb_ref, pa_ref, pb_ref, mg_ref):
        ya_v, yb_v = ya_ref[...], yb_ref[...]
        for s in range(N_DEV):
            cols = slice(s * 256, (s + 1) * 256)
            pa = _dot(ya_v, wa_ref[s], "nn")
            pb = _dot(yb_v, wb_ref[s], "nn")
            pa_ref[:, cols] = pa
            pb_ref[:, cols] = pb
            mg_ref[:, cols] = (ga_ref[:, cols] * pa + gb_ref[:, cols] * pb).astype(BF16)

    wo2 = wo.reshape(D, D)
    y_spec = pl.BlockSpec((TS, 512), lambda i: (i, 0))
    wp_spec = pl.BlockSpec((N_DEV, 512, 256), lambda i: (0, 0, 0))
    c_spec = pl.BlockSpec((TS, D), lambda i: (i, 0))
    pa, pb, merged = _pc(merge_body, name="merge_proj", grid=(T // TS,),
                         in_specs=[y_spec, wp_spec, y_spec, wp_spec, c_spec, pl.BlockSpec((TS, D), lambda i: (i, 1))],
                         out_specs=[c_spec, c_spec, c_spec],
                         out_shape=[_sds((T, D), F32), _sds((T, D), F32), _sds((T, D), BF16)],
                         sem=("parallel",))(ya, wpa, yb, wpb, gates, gates)

    def add_res(acc, ex, o, pids):
        o[0][...] = ex[0][...] + acc

    (x2,) = _mm("out_proj", (T // TL, 4, 1),
                [(merged, full_rows(TL), wo2, pl.BlockSpec((D, 512), lambda i, j, k: (0, j)), "nn", None)],
                [(x, rows(512, TL))], [(_sds((T, D), F32), rows(512, TL))], add_res, None, PP)

    h2, r2 = _rmsnorm_fwd(x2, norm_mlp, "rmsnorm_mlp")
    wu, wd = late_weights("mlp", h2)
    wd2 = wd.reshape(D_FF, D)

    def up_epi(acc, ex, o, pids):
        o[0][...] = acc.astype(BF16)
        r = jnp.maximum(acc, 0.0)
        o[1][...] = (r * r).astype(BF16)

    u, act = _mm("mlp_up", (T // TL, N_DEV, 1),
                 [(h2, full_rows(TL), wu, pl.BlockSpec((None, D, 1024), lambda i, j, k: (j, 0, 0)), "nn", None)],
                 [], [(_sds((T, D_FF), BF16), rows(1024, TL))] * 2, up_epi, None, PP)
    (x3,) = _mm("mlp_down", (nt, D // 512, 1),
                [(act, full_rows(TM, D_FF), wd2, pl.BlockSpec((D_FF, 512), lambda i, j, k: (0, j)), "nn", None)],
                [(x2, rows(512))], [(_sds((T, D), F32), rows(512))], add_res, None, PP)

    loss, dx3, dx3b, dg3 = _final_norm_loss(x3, norm_final.reshape(1, D), tgt)

    def du_epi(acc, ex, o, pids):
        o[0][...] = (acc * (2.0 * jnp.maximum(ex[0][...].astype(F32), 0.0))).astype(BF16)

    (du,) = _mm("mlp_down_dx", (T // TL, N_DEV, 1),
                [(dx3b, full_rows(TL), wd2, pl.BlockSpec((1024, D), lambda i, j, k: (j, 0)), "nt", None)],
                [(u, rows(1024, TL))], [(_sds((T, D_FF), BF16), rows(1024, TL))], du_epi, None, PP)
    (dwd,) = _mm("mlp_down_dw", (D // 1024, D_FF // 512, 1), [(act, tall(512), dx3b, tall_j(1024), "tn", None)],
                 [], [(_sds((D_FF, D), BF16), pl.BlockSpec((512, 1024), lambda j, i, k: (i, j)))], store_bf16, None, PP)
    dwd = dwd.reshape(N_DEV, D_FF // N_DEV, D)
    (dwu,) = _mm("mlp_up_dw", (N_DEV, D // 512, 1), [(h2, tall(512), du, tall_j(1024), "tn", None)],
                 [], [(_sds((N_DEV, D, 1024), BF16), pl.BlockSpec((None, 512, 1024), lambda j, i, k: (j, i, 0)))],
                 store_bf16, None, PP)

    def store_f32(acc, ex, o, pids):
        o[0][...] = acc

    out_ji = pl.BlockSpec((TM, 512), lambda j, i, k: (i, j))
    a_ji = lambda w: pl.BlockSpec((TM, w), lambda j, i, k: (i, 0))
    w_ji = lambda n: pl.BlockSpec((N_DEV, 512, n), lambda j, i, k: (0, j, 0))
    behind = lambda token: [] if token is None else [(token, pl.BlockSpec(memory_space=pl.ANY))]
    (dh2,) = _mm("mlp_up_dx", (D // 512, nt, 1), [(du, a_ji(D_FF), wu, w_ji(1024), "nt_shards", None)],
                 behind(send("mlp", dict(w_down=dwd, w_up=dwu))), [(_sds((T, D), F32), out_ji)], store_f32, None, PP)
    dx2, dx2b, dg2 = _rmsnorm_bwd("rmsnorm_mlp_bwd", dh2, x2, r2, norm_mlp, dx3, True)

    (dwo,) = _mm("out_proj_dw", (D // 1024, D // 512, 1), [(merged, tall(512), dx2b, tall_j(1024), "tn", None)],
                 [], [(_sds((D, D), BF16), pl.BlockSpec((512, 1024), lambda j, i, k: (i, j)))], store_bf16, None, PP)

    def dmerge_epi(acc, ex, o, pids):
        ga, gb, pa_v, pb_v = (e[...] for e in ex)
        o[0][...] = (acc * ga).astype(BF16)
        o[1][...] = (acc * gb).astype(BF16)
        dga = acc * pa_v * ga * (1.0 - ga)
        dgb = acc * pb_v * gb * (1.0 - gb)
        o[2][...] = dga.astype(BF16)
        o[3][...] = dgb.astype(BF16)
        sa = jnp.sum(dga, axis=0, keepdims=True)
        sb = jnp.sum(dgb, axis=0, keepdims=True)
        first = pids[1] == 0

        @pl.when(first)
        def _():
            o[4][...] = sa
            o[5][...] = sb

        @pl.when(jnp.logical_not(first))
        def _():
            o[4][...] += sa
            o[5][...] += sb

    cj = pl.BlockSpec((TM, 512), lambda j, i, k: (i, j))
    bj = pl.BlockSpec((1, 512), lambda j, i, k: (0, j))
    dpa, dpb, dga, dgb, dba, dbb = _mm(
        "out_proj_dx", (4, nt, 1),
        [(dx2b, pl.BlockSpec((TM, D), lambda j, i, k: (i, 0)), wo2, pl.BlockSpec((512, D), lambda j, i, k: (j, 0)), "nt", None)],
        [(gates, cj), (gates, pl.BlockSpec((TM, 512), lambda j, i, k: (i, 4 + j))), (pa, cj), (pb, cj)],
        [(_sds((T, D), BF16), cj)] * 4 + [(_sds((1, D), F32), bj)] * 2, dmerge_epi, None,
        ("parallel", "arbitrary", "arbitrary"))

    def store_shards(acc, ex, o, pids):
        for s in range(4):
            o[0][s] = acc[:, s * 256:(s + 1) * 256].astype(BF16)

    def proj_dw(name, y, dp):
        (dw,) = _mm(name, (2, 1, 1), [(y, tall(512), dp, tall_j(1024), "tn", None)],
                    [], [(_sds((N_DEV, 512, 256), BF16), pl.BlockSpec((4, 512, 256), lambda j, i, k: (j, 0, 0)))],
                    store_shards, None, PP)
        return dw

    def proj_dx(name, dp, w):
        (dy,) = _mm(name, (nt, 1, 1),
                    [(dp, full_rows(TM), w, pl.BlockSpec((N_DEV, 512, 256), lambda i, j, k: (0, 0, 0)), "nt_shards", None)],
                    [], [(_sds((T, 512), BF16), full_rows(TM, 512))], store_bf16, None, PP)
        return dy

    def in_dw(name, dact, n):
        (dw,) = _mm(name, (N_DEV, D // 512, 1), [(h1, tall(512), dact, tall_j(n), "tn", None)],
                    [], [(_sds((N_DEV, D, n), BF16), pl.BlockSpec((None, 512, n), lambda j, i, k: (j, i, 0)))],
                    store_bf16, None, PP)
        return dw

    dgate = jnp.concatenate([dga, dgb], axis=1)
    dwg = in_dw("gate_proj_dw", dgate, 512)
    dwpa = proj_dw("proj_a_dw", ya, dpa)
    dwpb = proj_dw("proj_b_dw", yb, dpb)
    token = send("proj", dict(w_gate=dwg, w_out=dwo.reshape(N_DEV, D // N_DEV, D), w_proj_a=dwpa, w_proj_b=dwpb))
    dya = proj_dx("proj_a_dx", dpa, wpa)
    dyb = proj_dx("proj_b_dx", dpb, wpb)

    dq_a, dk_a, dv_a = _attn_a_bwd(qkv, _behind(slopes, token), dya, ya, lse_a)
    dq_b, dk_b, dv_b, dtiles = _attn_b_bwd(qkv, tiles, dyb, yb, lse_b)
    drpb = _rpb_reduce(dtiles.reshape(N_HEADS_B * N_DR, GRID_W * GRID_W), sel)[:, :N_DC]
    dqkv = jnp.concatenate([dq_a[0], dq_a[1], dq_a[2], dq_b, dk_a[0], dk_a[1], dk_a[2], dk_b,
                            dv_a[0], dv_a[1], dv_a[2], dv_b], axis=1)
    dwq = in_dw("qkv_proj_dw", dqkv, 768)
    (dh1,) = _mm("in_proj_dx", (D // 512, nt, 1),
                 [(dqkv, a_ji(3 * D), wq, w_ji(768), "nt_shards", None), (dgate, a_ji(2 * D), wg, w_ji(512), "nt_shards", None)],
                 behind(send("in", dict(w_qkv=dwq))), [(_sds((T, D), F32), out_ji)], store_f32, None, PP)
    grad_x, dg1 = _rmsnorm_bwd("rmsnorm_mix_bwd", dh1, x, r1, norm_mix, dx2, False)

    small = dict(norm_mix=dg1, b_gate=jnp.concatenate([dba, dbb], axis=1), rpb=drpb, norm_mlp=dg2, norm_final=dg3)
    return loss, grad_x, small


BIG = ("w_qkv", "w_gate", "w_proj_a", "w_proj_b", "w_out", "w_up", "w_down")
SMALL = ("norm_mix", "b_gate", "rpb", "norm_mlp", "norm_final")
SMALL_ROWS = 96


def _place():
    return lax.axis_index("x"), lax.axis_index("y"), lax.axis_index("c")


def _peer(x, y, c, mask):
    return x ^ (mask >> 2), y ^ ((mask >> 1) & 1), c ^ (mask & 1)


_HBM = pl.BlockSpec(memory_space=pltpu.HBM)
_SEM = pl.BlockSpec(memory_space=pltpu.SEMAPHORE)
_ANY = pl.BlockSpec(memory_space=pl.ANY)
_EFFECT = pltpu.SideEffectType.DATAFLOW_SIDE_EFFECTING


def _hbm(a):
    return pltpu.with_memory_space_constraint(a, pltpu.HBM)


def _other_chips(x, y):
    return [(1 - x, y), (x, 1 - y), (1 - x, 1 - y)]


def _block(x, y, c):
    return 4 * x + 2 * y + c


def _rdma(src, dst, send_sem, recv_sem, to):
    return pltpu.make_async_remote_copy(src_ref=src, dst_ref=dst, send_sem=send_sem, recv_sem=recv_sem, device_id=to,
                                        device_id_type=MESH)


def _gather_start(name, shards):
    n = len(shards)

    def body(*refs):
        srcs, lands = refs[:n], refs[n:2 * n]
        outs = refs[2 * n:]
        send, recv_d2d, recv_ici, local = outs[:n], outs[n:2 * n], outs[2 * n:3 * n], outs[3 * n:4 * n]
        token_ref = outs[6 * n]
        x, y, c = _place()
        for a in range(n):
            mine = lands[a].at[_block(x, y, c)]
            _rdma(srcs[a], mine, send[a].at[0], recv_d2d[a].at[0], (x, y, 1 - c)).start()
            for j, chip in enumerate(_other_chips(x, y)):
                _rdma(srcs[a], mine, send[a].at[1 + j], recv_ici[a].at[j], (*chip, c)).start()
        for a in range(n):
            pltpu.make_async_copy(srcs[a], lands[a].at[_block(x, y, c)], local[a].at[0]).start()
        token_ref[...] = jnp.zeros_like(token_ref)

    lands = [_hbm(lax.empty((N_DEV,) + s.shape, s.dtype)) for s in shards]
    srcs = [_hbm(s) for s in shards]
    dma = pltpu.SemaphoreType.DMA
    res = pl.pallas_call(
        body, name=name,
        out_shape=[dma((4,))] * n + [dma((1,))] * n + [dma((3,))] * n + [dma((1,))] * n
        + [pltpu.HBM(s.shape, s.dtype) for s in srcs] + [pltpu.HBM(l.shape, l.dtype) for l in lands]
        + [jax.ShapeDtypeStruct((8, 128), F32)],
        in_specs=[_HBM] * (2 * n), out_specs=[_SEM] * (4 * n) + [_HBM] * (2 * n) + [pl.BlockSpec(memory_space=pltpu.VMEM)],
        input_output_aliases={i: 4 * n + i for i in range(2 * n)},
        compiler_params=pltpu.CompilerParams(has_side_effects=_EFFECT))(*srcs, *lands)
    keys = ("send", "recv_d2d", "recv_ici", "local", "src", "land")
    return [dict(zip(keys, (res[k * n + a] for k in range(6)))) for a in range(n)], res[6 * n]


def _gather_forward(name, handles, after):
    n = len(handles)

    def body(*refs):
        lands, recv_ici = refs[:n], refs[n:2 * n]
        outs = refs[2 * n + 1:]
        fwd_send, fwd_recv = outs[n:2 * n], outs[2 * n:3 * n]
        x, y, c = _place()
        for a in range(n):
            for j, chip in enumerate(_other_chips(x, y)):
                blk = lands[a].at[_block(*chip, c)]
                _rdma(blk, blk, fwd_send[a].at[j], recv_ici[a].at[j], (*chip, c)).wait_recv()
                _rdma(blk, blk, fwd_send[a].at[j], fwd_recv[a].at[j], (x, y, 1 - c)).start()

    dma = pltpu.SemaphoreType.DMA
    lands = [h["land"] for h in handles]
    res = pl.pallas_call(
        body, name=name, out_shape=[pltpu.HBM(l.shape, l.dtype) for l in lands] + [dma((3,))] * (2 * n),
        in_specs=[_HBM] * n + [_SEM] * n + [_ANY], out_specs=[_HBM] * n + [_SEM] * (2 * n),
        input_output_aliases={i: i for i in range(n)},
        compiler_params=pltpu.CompilerParams(has_side_effects=_EFFECT))(*lands, *[h["recv_ici"] for h in handles], after)
    return [dict(h, land=res[a], fwd_send=res[n + a], fwd_recv=res[2 * n + a]) for a, h in enumerate(handles)]


def _gather_wait(name, handles, after):
    n = len(handles)

    def body(*refs):
        srcs, lands = refs[:n], refs[n:2 * n]
        send, recv_d2d, local, fwd_send, fwd_recv = (refs[(2 + k) * n:(3 + k) * n] for k in range(5))
        x, y, c = _place()
        sibling = (x, y, 1 - c)
        for a in range(n):
            mine = lands[a].at[_block(x, y, c)]
            _rdma(srcs[a], mine, send[a].at[0], recv_d2d[a].at[0], sibling).wait_send()
            _rdma(srcs[a], lands[a].at[_block(*sibling)], send[a].at[0], recv_d2d[a].at[0], sibling).wait_recv()
            pltpu.make_async_copy(srcs[a], mine, local[a].at[0]).wait()
            for j, chip in enumerate(_other_chips(x, y)):
                _rdma(srcs[a], mine, send[a].at[1 + j], fwd_recv[a].at[j], (*chip, c)).wait_send()
                blk = lands[a].at[_block(*chip, c)]
                _rdma(blk, blk, fwd_send[a].at[j], fwd_recv[a].at[j], sibling).wait_send()
                got = lands[a].at[_block(*chip, 1 - c)]
                _rdma(got, got, fwd_send[a].at[j], fwd_recv[a].at[j], sibling).wait_recv()

    srcs = [h["src"] for h in handles]
    lands = [h["land"] for h in handles]
    sems = [h[k] for k in ("send", "recv_d2d", "local", "fwd_send", "fwd_recv") for h in handles]
    res = pl.pallas_call(
        body, name=name, out_shape=[pltpu.HBM(s.shape, s.dtype) for s in srcs] + [pltpu.HBM(l.shape, l.dtype) for l in lands],
        in_specs=[_HBM] * (2 * n) + [_SEM] * (5 * n) + [_ANY], out_specs=[_HBM] * (2 * n),
        input_output_aliases={i: i for i in range(2 * n)},
        compiler_params=pltpu.CompilerParams(has_side_effects=_EFFECT))(*srcs, *lands, *sems, after)
    return list(res[n:])


def _gather_small(small):
    def body(in_ref, out_ref, send_sems, recv_sems, local_sem):
        x, y, c = _place()
        me = 4 * x + 2 * y + c
        copies = [pltpu.make_async_copy(in_ref, out_ref.at[me], local_sem)]
        for mask in range(1, N_DEV):
            copies.append(pltpu.make_async_remote_copy(
                src_ref=in_ref, dst_ref=out_ref.at[me], send_sem=send_sems.at[mask - 1],
                recv_sem=recv_sems.at[mask - 1], device_id=_peer(x, y, c, mask), device_id_type=MESH))
        for cp in copies:
            cp.start()
        for cp in copies:
            cp.wait()

    any_spec = pl.BlockSpec(memory_space=pl.ANY)
    return pl.pallas_call(
        body, name="gather_small", in_specs=[any_spec], out_specs=any_spec,
        out_shape=jax.ShapeDtypeStruct((N_DEV,) + small.shape, small.dtype),
        scratch_shapes=[pltpu.SemaphoreType.DMA((7,)), pltpu.SemaphoreType.DMA((7,)), pltpu.SemaphoreType.DMA])(small)


def _send_copy(src_refs, land_refs, send_sems, recv_sems, a, mask, scatter, x, y, c):
    px, py, pc = _peer(x, y, c, mask)
    src = src_refs[a].at[4 * px + 2 * py + pc] if scatter else src_refs[a]
    return pltpu.make_async_remote_copy(
        src_ref=src, dst_ref=land_refs[a].at[4 * x + 2 * y + c], send_sem=send_sems[a].at[mask - 1],
        recv_sem=recv_sems[a].at[mask - 1], device_id=(px, py, pc), device_id_type=MESH)


def _local_copy(src_refs, land_refs, local_sems, a, scatter, x, y, c):
    me = 4 * x + 2 * y + c
    return pltpu.make_async_copy(src_refs[a].at[me] if scatter else src_refs[a], land_refs[a].at[me], local_sems[a].at[0])


def _send_start(name, srcs, scatter, after=None):
    n = len(srcs)
    n_in = 2 * n + (after is not None)
    shapes = [s.shape[1:] if scatter else s.shape for s in srcs]

    def body(*refs):
        src_refs, land_refs = refs[:n], refs[n:2 * n]
        outs = refs[n_in:]
        send_sems, recv_sems, local_sems = outs[:n], outs[n:2 * n], outs[2 * n:3 * n]
        token_ref = outs[5 * n]
        x, y, c = _place()
        for a in range(n):
            for mask in range(1, N_DEV):
                _send_copy(src_refs, land_refs, send_sems, recv_sems, a, mask, scatter, x, y, c).start()
        for a in range(n):
            _local_copy(src_refs, land_refs, local_sems, a, scatter, x, y, c).start()
        token_ref[...] = jnp.zeros_like(token_ref)

    lands = [pltpu.with_memory_space_constraint(lax.empty((N_DEV,) + sh, s.dtype), pltpu.HBM) for sh, s in zip(shapes, srcs)]
    srcs = [pltpu.with_memory_space_constraint(s, pltpu.HBM) for s in srcs]
    res = pl.pallas_call(
        body, name=name,
        out_shape=[pltpu.SemaphoreType.DMA((7,))] * (2 * n) + [pltpu.SemaphoreType.DMA((1,))] * n
        + [pltpu.HBM(s.shape, s.dtype) for s in srcs] + [pltpu.HBM(l.shape, l.dtype) for l in lands]
        + [jax.ShapeDtypeStruct((8, 128), F32)],
        in_specs=[_HBM] * (2 * n) + [pl.BlockSpec(memory_space=pl.ANY)] * (n_in - 2 * n),
        out_specs=[_SEM] * (3 * n) + [_HBM] * (2 * n) + [pl.BlockSpec(memory_space=pltpu.VMEM)],
        input_output_aliases={i: 3 * n + i for i in range(2 * n)},
        compiler_params=pltpu.CompilerParams(has_side_effects=_EFFECT))(*srcs, *lands, *([] if after is None else [after]))
    handles = [tuple(res[k * n + a] for k in range(5)) for a in range(n)]
    return handles, res[5 * n]


def _send_wait(name, handles, scatter, after):
    n = len(handles)

    def body(*refs):
        src_refs, land_refs = refs[:n], refs[n:2 * n]
        send_sems, recv_sems, local_sems = refs[2 * n:3 * n], refs[3 * n:4 * n], refs[4 * n:5 * n]
        x, y, c = _place()
        for a in range(n):
            for mask in range(1, N_DEV):
                cp = _send_copy(src_refs, land_refs, send_sems, recv_sems, a, mask, scatter, x, y, c)
                cp.wait_send()
                cp.wait_recv()
            _local_copy(src_refs, land_refs, local_sems, a, scatter, x, y, c).wait()

    srcs = [h[3] for h in handles]
    lands = [h[4] for h in handles]
    res = pl.pallas_call(
        body, name=name, out_shape=[pltpu.HBM(s.shape, s.dtype) for s in srcs] + [pltpu.HBM(l.shape, l.dtype) for l in lands],
        in_specs=[_HBM] * (2 * n) + [_SEM] * (3 * n) + [pl.BlockSpec(memory_space=pl.ANY)], out_specs=[_HBM] * (2 * n),
        input_output_aliases={i: i for i in range(2 * n)},
        compiler_params=pltpu.CompilerParams(has_side_effects=_EFFECT))(
            *srcs, *lands, *[h[0] for h in handles], *[h[1] for h in handles], *[h[2] for h in handles], after)
    return list(res[n:])


def _adamw(name, parts, w, m, v):
    K, n = w.shape
    tr = min(K, 128)

    def body(p_ref, w_ref, m_ref, v_ref, g_ref, d_ref, nm_ref, nv_ref):
        g = p_ref[0].astype(F32)
        for i in range(1, N_DEV):
            g = g + p_ref[i].astype(F32)
        nm = ADAM_B1 * m_ref[...] + (1.0 - ADAM_B1) * g
        nv = ADAM_B2 * v_ref[...] + (1.0 - ADAM_B2) * jnp.square(g)
        m_hat = nm / (1.0 - ADAM_B1 ** ADAM_STEP)
        v_hat = nv / (1.0 - ADAM_B2 ** ADAM_STEP)
        g_ref[...] = g
        d_ref[...] = -ADAM_LR * (m_hat / (jnp.sqrt(v_hat) + ADAM_EPS) + ADAM_WD * w_ref[...])
        nm_ref[...] = nm
        nv_ref[...] = nv

    blk = pl.BlockSpec((tr, n), lambda i: (i, 0))
    return _pc(body, name=name, grid=(K // tr,), in_specs=[pl.BlockSpec((N_DEV, tr, n), lambda i: (0, i, 0)), blk, blk, blk],
               out_specs=[blk] * 4, out_shape=[jax.ShapeDtypeStruct((K, n), F32)] * 4, sem=("parallel",))(parts, w, m, v)


def _pack_small(vals):
    flat = jnp.concatenate([v.reshape(-1) for v in vals])
    return jnp.pad(flat, (0, SMALL_ROWS * 128 - flat.shape[0])).reshape(SMALL_ROWS, 128)


def _unpack_small(packed, shapes):
    flat, out, off = packed.reshape(-1), [], 0
    for s in shapes:
        size = int(np.prod(s))
        out.append(flat[off:off + size].reshape(s))
        off += size
    return out


def kernel(x, norm_mix, w_qkv, w_gate, b_gate, rpb, w_proj_a, w_proj_b, w_out, norm_mlp, w_up, w_down, norm_final, loss_target, m_norm_mix, m_w_qkv, m_w_gate, m_b_gate, m_rpb, m_w_proj_a, m_w_proj_b, m_w_out, m_norm_mlp, m_w_up, m_w_down, m_norm_final, v_norm_mix, v_w_qkv, v_w_gate, v_b_gate, v_rpb, v_w_proj_a, v_w_proj_b, v_w_out, v_norm_mlp, v_w_up, v_w_down, v_norm_final):
    w = dict(norm_mix=norm_mix, w_qkv=w_qkv, w_gate=w_gate, b_gate=b_gate, rpb=rpb, w_proj_a=w_proj_a, w_proj_b=w_proj_b,
             w_out=w_out, norm_mlp=norm_mlp, w_up=w_up, w_down=w_down, norm_final=norm_final)
    m = dict(norm_mix=m_norm_mix, w_qkv=m_w_qkv, w_gate=m_w_gate, b_gate=m_b_gate, rpb=m_rpb, w_proj_a=m_w_proj_a,
             w_proj_b=m_w_proj_b, w_out=m_w_out, norm_mlp=m_norm_mlp, w_up=m_w_up, w_down=m_w_down, norm_final=m_norm_final)
    v = dict(norm_mix=v_norm_mix, w_qkv=v_w_qkv, w_gate=v_w_gate, b_gate=v_b_gate, rpb=v_rpb, w_proj_a=v_w_proj_a,
             w_proj_b=v_w_proj_b, w_out=v_w_out, norm_mlp=v_norm_mlp, w_up=v_w_up, w_down=v_w_down, norm_final=v_norm_final)
    order = ("norm_mix", "w_qkv", "w_gate", "b_gate", "rpb", "w_proj_a", "w_proj_b", "w_out", "norm_mlp", "w_up",
             "w_down", "norm_final")

    shard = {n: w[n][0].astype(BF16) for n in BIG}
    stages = dict(qkv=("w_qkv",), gate=("w_gate", "w_proj_a", "w_proj_b", "w_out"), mlp=("w_up", "w_down"))
    names = [n for s in stages.values() for n in s]
    handles, token = _gather_start("gather_start", [shard[n] for n in names])
    handles = dict(zip(names, handles))

    def late_weights(stage, after):
        if stage == "qkv":
            return tuple(_gather_wait("gather_wait_qkv", _gather_forward("gather_forward_qkv", [handles["w_qkv"]], after), after))
        if stage == "attention":
            handles["gate"] = _gather_forward("gather_forward_gate", [handles[n] for n in stages["gate"]], after)
            return ()
        if stage == "mixer":
            handles["mlp"] = _gather_forward("gather_forward_mlp", [handles[n] for n in stages["mlp"]], after)
            return tuple(_gather_wait("gather_wait_gate", handles["gate"], after))
        return tuple(_gather_wait("gather_wait_mlp", handles["mlp"], after))

    sent = {}

    def send(stage, grads):
        handles, tok = _send_start("grads_start_" + stage, list(grads.values()), True)
        sent[stage] = (tuple(grads), handles)
        return tok

    loss, grad_x, small = _local_step(x[0], loss_target[0], _behind(norm_mix, token), b_gate, rpb[0], norm_mlp, norm_final,
                                      late_weights, send)

    grad, delta, new_m, new_v = {}, {}, {}, {}
    after = grad_x
    for stage in ("mlp", "proj", "in"):
        names, handles = sent[stage]
        for n, parts in zip(names, _send_wait("grads_wait_" + stage, handles, True, after)):
            res = _adamw("adamw_" + n, parts, w[n][0], m[n][0], v[n][0])
            grad[n], delta[n], new_m[n], new_v[n] = (r[None] for r in res)
            after = res[1]
    res = _adamw("adamw_small", _gather_small(_pack_small([small[n] for n in SMALL])),
                 _pack_small([w[n] for n in SMALL]), _pack_small([m[n] for n in SMALL]), _pack_small([v[n] for n in SMALL]))
    shapes = [w[n].shape for n in SMALL]
    for tree, packed in zip((grad, delta, new_m, new_v), res):
        for n, val in zip(SMALL, _unpack_small(packed, shapes)):
            tree[n] = val

    total = lax.psum(loss[0, 0], ("x", "y", "c"))
    return (total, grad_x[None], *[grad[n] for n in order], *[delta[n] for n in order],
            *[new_m[n] for n in order], *[new_v[n] for n in order])
```

```python
import numpy as np
import jax
import jax.numpy as jnp
from jax import lax
from jax.experimental import pallas as pl
from jax.experimental.pallas import tpu as pltpu

F32 = jnp.float32
BF16 = jnp.bfloat16
MESH = pl.DeviceIdType.MESH

D_MODEL = 2048
HEAD_DIM = 128
N_HEADS = 16
N_HEADS_A = 12
N_HEADS_B = 4
DILATIONS = (1, 4, 16)
HALF_WINDOW = 64
GRID_W = 64
NA_ROWS = 8
NA_COLS = 16
D_FF = 4 * D_MODEL
N_DEV = 8
EPS = 1e-6
NEG = -1e30
SCALE = HEAD_DIM ** -0.5
ADAM_LR, ADAM_B1, ADAM_B2, ADAM_EPS, ADAM_WD, ADAM_STEP = 0.001, 0.9, 0.999, 1e-08, 0.01, 10

VMEM_LIMIT = 56 * 1024 * 1024
TM = 512
TL = 1024
TS = 256
BQ_A = 128


def _pc(body, *, name, grid, in_specs, out_specs, out_shape, scratch=(), sem=None):
    return pl.pallas_call(
        body, name=name, grid=grid, in_specs=in_specs, out_specs=out_specs, out_shape=out_shape,
        scratch_shapes=list(scratch),
        compiler_params=pltpu.CompilerParams(dimension_semantics=sem, vmem_limit_bytes=VMEM_LIMIT))


_DOT_DIMS = {"nn": (((1,), (0,)), ((), ())), "nt": (((1,), (1,)), ((), ())), "tn": (((0,), (0,)), ((), ()))}


def _dot(a, b, mode, precision=None):
    if mode == "nt_shards":
        n = b.shape[2]
        out = _dot(a[:, :n], b[0], "nt")
        for s in range(1, b.shape[0]):
            out = out + _dot(a[:, s * n:(s + 1) * n], b[s], "nt")
        return out
    return lax.dot_general(a, b, _DOT_DIMS[mode], preferred_element_type=F32, precision=precision)


def _mm(name, grid, pairs, extras, outs, epilogue, acc_shape, sem):
    n_pairs, n_extra, n_out = len(pairs), len(extras), len(outs)
    k_axis = len(grid) - 1
    nk = grid[k_axis]

    def body(*refs):
        pids = [pl.program_id(ax) for ax in range(len(grid))]
        ab = refs[:2 * n_pairs]
        ex = refs[2 * n_pairs:2 * n_pairs + n_extra]
        o = refs[2 * n_pairs + n_extra:2 * n_pairs + n_extra + n_out]
        part = None
        for p, pair in enumerate(pairs):
            a = ab[2 * p][...]
            if pair[5] is not None:
                a = pair[5](a)
            d = _dot(a, ab[2 * p + 1][...], pair[4])
            part = d if part is None else part + d
        if nk == 1:
            epilogue(part, ex, o, pids)
        else:
            acc = refs[-1]
            k = pids[k_axis]

            @pl.when(k == 0)
            def _():
                acc[...] = part

            @pl.when(k > 0)
            def _():
                acc[...] += part

            @pl.when(k == nk - 1)
            def _():
                epilogue(acc[...], ex, o, pids)

    operands, in_specs = [], []
    for a, a_spec, b, b_spec, _, _ in pairs:
        operands += [a, b]
        in_specs += [a_spec, b_spec]
    for e, e_spec in extras:
        operands.append(e)
        in_specs.append(e_spec)
    res = _pc(body, name=name, grid=grid, in_specs=in_specs, out_specs=[s for _, s in outs],
              out_shape=[o for o, _ in outs], scratch=[pltpu.VMEM(acc_shape, F32)] if nk > 1 else [], sem=sem)(*operands)
    return res


def _rmsnorm_fwd(x, g, name):
    T, D = x.shape

    def body(x_ref, g_ref, h_ref, r_ref):
        xv = x_ref[...]
        r = lax.rsqrt(jnp.mean(xv * xv, axis=-1, keepdims=True) + EPS)
        h_ref[...] = (xv * r * g_ref[...]).astype(BF16)
        r_ref[...] = r

    return _pc(body, name=name, grid=(T // TM,),
               in_specs=[pl.BlockSpec((TM, D), lambda i: (i, 0)), pl.BlockSpec((1, D), lambda i: (0, 0))],
               out_specs=[pl.BlockSpec((TM, D), lambda i: (i, 0)), pl.BlockSpec((TM, 1), lambda i: (i, 0))],
               out_shape=[jax.ShapeDtypeStruct((T, D), BF16), jax.ShapeDtypeStruct((T, 1), F32)],
               sem=("parallel",))(x, g)


def _rmsnorm_bwd_block(dh, x, r, g, dres):
    xh = x * r
    dxh = dh * g
    dx = dres + r * (dxh - xh * jnp.mean(dxh * xh, axis=-1, keepdims=True))
    return dx, jnp.sum(dh * xh, axis=0, keepdims=True)


def _rmsnorm_bwd(name, dh, x, r, g, dres, want_bf16):
    T, D = x.shape

    def body(dh_ref, x_ref, r_ref, g_ref, dres_ref, *outs):
        dx, dgp = _rmsnorm_bwd_block(dh_ref[...], x_ref[...], r_ref[...], g_ref[...], dres_ref[...])
        outs[0][...] = dx
        if want_bf16:
            outs[1][...] = dx.astype(BF16)
        first = pl.program_id(0) == 0

        @pl.when(first)
        def _():
            outs[-1][...] = dgp

        @pl.when(jnp.logical_not(first))
        def _():
            outs[-1][...] += dgp

    row = pl.BlockSpec((TS, D), lambda i: (i, 0))
    vec = pl.BlockSpec((1, D), lambda i: (0, 0))
    n_row = 2 if want_bf16 else 1
    return _pc(body, name=name, grid=(T // TS,), in_specs=[row, row, pl.BlockSpec((TS, 1), lambda i: (i, 0)), vec, row],
               out_specs=[row] * n_row + [vec],
               out_shape=[jax.ShapeDtypeStruct((T, D), F32)] + [jax.ShapeDtypeStruct((T, D), BF16)] * (n_row - 1)
               + [jax.ShapeDtypeStruct((1, D), F32)], sem=("arbitrary",))(dh, x, r, g, dres)


def _final_norm_loss(x3, g, tgt):
    T, D = x3.shape

    def body(x_ref, g_ref, t_ref, loss_ref, dx_ref, dxb_ref, dg_ref):
        i = pl.program_id(0)
        xv = x_ref[...]
        gv = g_ref[...]
        r = lax.rsqrt(jnp.mean(xv * xv, axis=-1, keepdims=True) + EPS)
        xh = xv * r
        e = xh * gv - t_ref[...]
        part = 0.5 * jnp.sum(jnp.mean(e * e, axis=-1, keepdims=True))
        dy = e * (1.0 / D)
        dxh = dy * gv
        dx = r * (dxh - xh * jnp.mean(dxh * xh, axis=-1, keepdims=True))
        dx_ref[...] = dx
        dxb_ref[...] = dx.astype(BF16)
        dgp = jnp.sum(dy * xh, axis=0, keepdims=True)

        @pl.when(i == 0)
        def _():
            loss_ref[...] = jnp.zeros(loss_ref.shape, F32) + part
            dg_ref[...] = dgp

        @pl.when(i > 0)
        def _():
            loss_ref[...] += part
            dg_ref[...] += dgp

    row = pl.BlockSpec((TM, D), lambda i: (i, 0))
    vec = pl.BlockSpec((1, D), lambda i: (0, 0))
    return _pc(body, name="final_norm_loss", grid=(T // TM,), in_specs=[row, vec, row],
               out_specs=[pl.BlockSpec((8, 128), lambda i: (0, 0)), row, row, vec],
               out_shape=[jax.ShapeDtypeStruct((8, 128), F32), jax.ShapeDtypeStruct((T, D), F32),
                          jax.ShapeDtypeStruct((T, D), BF16), jax.ShapeDtypeStruct((1, D), F32)],
               sem=("arbitrary",))(x3, g, tgt)


def _alibi_slopes():
    return np.asarray(2.0 ** (-8.0 * np.arange(1, N_HEADS_A + 1) / N_HEADS_A), dtype=np.float32)


def _a_windows(T):
    half = [HALF_WINDOW * d for d in DILATIONS]
    return half, [min(BQ_A + 2 * w, T) for w in half]


def _a_bias(slope, g, offset, wk, half):
    d = DILATIONS[g]
    dlt = offset + lax.broadcasted_iota(jnp.int32, (BQ_A, wk), 1) - lax.broadcasted_iota(jnp.int32, (BQ_A, wk), 0)
    ad = jnp.maximum(dlt, -dlt)
    ok = (ad <= half) & ((ad & (d - 1)) == 0)
    return jnp.where(ok, -slope * ad.astype(F32), NEG)


def _a_scores(q, kw, slope, g, offset, wk, half):
    return _dot(q, kw, "nt") * SCALE + _a_bias(slope, g, offset, wk, half)


def _a_qkv_specs(T, inner):
    specs = []
    for g in range(3):
        specs.append(pl.BlockSpec((BQ_A, HEAD_DIM), lambda j, i, g=g: (i, 4 * g + j)))
        specs.append(pl.BlockSpec((T, HEAD_DIM), lambda j, i, g=g: (0, N_HEADS + 4 * g + j)))
        specs.append(pl.BlockSpec((T, HEAD_DIM), lambda j, i, g=g: (0, 2 * N_HEADS + 4 * g + j)))
    return specs


def _attn_a_fwd(qkv, slopes):
    T = qkv.shape[0]
    nq = T // BQ_A
    half, wks = _a_windows(T)

    def body(sl_ref, *refs):
        qkv_refs, (y_ref, lse_ref) = refs[:9], refs[9:]
        j = pl.program_id(0)
        t0 = pl.program_id(1) * BQ_A
        ss, vws = [], []
        for g in range(3):
            q_ref, k_ref, v_ref = qkv_refs[3 * g:3 * g + 3]
            ws = pl.multiple_of(jnp.clip(t0 - half[g], 0, T - wks[g]), 64)
            ss.append(_a_scores(q_ref[...], k_ref[pl.ds(ws, wks[g]), :], sl_ref[4 * g + j], g, ws - t0, wks[g], half[g]))
            vws.append(v_ref[pl.ds(ws, wks[g]), :])
        m = jnp.maximum(jnp.maximum(jnp.max(ss[0], axis=-1, keepdims=True), jnp.max(ss[1], axis=-1, keepdims=True)),
                        jnp.max(ss[2], axis=-1, keepdims=True))
        den = jnp.zeros((BQ_A, 1), F32)
        acc = jnp.zeros((BQ_A, HEAD_DIM), F32)
        for g in range(3):
            p = jnp.exp(ss[g] - m)
            den = den + jnp.sum(p, axis=-1, keepdims=True)
            acc = acc + _dot(p.astype(BF16), vws[g], "nn")
        y_ref[...] = (acc / den).astype(BF16)
        lse_ref[...] = m + jnp.log(den)

    return _pc(body, name="attn_a_fwd", grid=(4, nq),
               in_specs=[pl.BlockSpec(memory_space=pltpu.SMEM)] + _a_qkv_specs(T, nq),
               out_specs=[pl.BlockSpec((BQ_A, HEAD_DIM), lambda j, i: (i, j)),
                          pl.BlockSpec((None, BQ_A, 1), lambda j, i: (j, i, 0))],
               out_shape=[jax.ShapeDtypeStruct((T, 4 * HEAD_DIM), BF16), jax.ShapeDtypeStruct((4, T, 1), F32)],
               sem=("parallel", "parallel"))(slopes, *([qkv] * 9))


def _attn_a_bwd(qkv, slopes, dy, y, lse):
    T = qkv.shape[0]
    nq = T // BQ_A
    half, wks = _a_windows(T)

    def body(sl_ref, *refs):
        qkv_refs = refs[:9]
        dy_ref, y_ref, lse_ref = refs[9:12]
        dq_ref, dk_ref, dv_ref = refs[12:15]
        dk_acc, dv_acc = refs[15:]
        j = pl.program_id(0)
        i = pl.program_id(1)
        t0 = i * BQ_A

        @pl.when(i == 0)
        def _():
            dk_acc[...] = jnp.zeros_like(dk_acc)
            dv_acc[...] = jnp.zeros_like(dv_acc)

        dyv = dy_ref[...]
        delta = jnp.sum(dyv.astype(F32) * y_ref[...].astype(F32), axis=-1, keepdims=True)
        lse_v = lse_ref[...]
        for g in range(3):
            q_ref, k_ref, v_ref = qkv_refs[3 * g:3 * g + 3]
            ws = pl.multiple_of(jnp.clip(t0 - half[g], 0, T - wks[g]), 64)
            win = pl.ds(ws, wks[g])
            qv = q_ref[...]
            kw = k_ref[win, :]
            s = _a_scores(qv, kw, sl_ref[4 * g + j], g, ws - t0, wks[g], half[g])
            p = jnp.exp(s - lse_v)
            dp = _dot(dyv, v_ref[win, :], "nt")
            ds = (p * (dp - delta)).astype(BF16)
            dq_ref[g] = (_dot(ds, kw, "nn") * SCALE).astype(BF16)
            dk_acc[g, win, :] += _dot(ds, qv, "tn") * SCALE
            dv_acc[g, win, :] += _dot(p.astype(BF16), dyv, "tn")

        @pl.when(i == nq - 1)
        def _():
            dk_ref[...] = dk_acc[...].astype(BF16)
            dv_ref[...] = dv_acc[...].astype(BF16)

    blk = pl.BlockSpec((BQ_A, HEAD_DIM), lambda j, i: (i, j))
    whole = pl.BlockSpec((3, T, HEAD_DIM), lambda j, i: (0, 0, j))
    full = jax.ShapeDtypeStruct((3, T, 4 * HEAD_DIM), BF16)
    return _pc(body, name="attn_a_bwd", grid=(4, nq),
               in_specs=[pl.BlockSpec(memory_space=pltpu.SMEM)] + _a_qkv_specs(T, nq)
               + [blk, blk, pl.BlockSpec((None, BQ_A, 1), lambda j, i: (j, i, 0))],
               out_specs=[pl.BlockSpec((3, BQ_A, HEAD_DIM), lambda j, i: (0, i, j)), whole, whole],
               out_shape=[full, full, full],
               scratch=[pltpu.VMEM((3, T, HEAD_DIM), F32), pltpu.VMEM((3, T, HEAD_DIM), F32)],
               sem=("parallel", "arbitrary"))(slopes, *([qkv] * 9), dy, y, lse)


N_DR = 2 * NA_ROWS - 1
N_DC = 2 * NA_COLS - 1
WK_B = NA_ROWS * GRID_W


def _toeplitz_consts():
    col = np.arange(GRID_W)
    start = np.clip(col - NA_COLS // 2, 0, GRID_W - NA_COLS)
    ok = (col[None, :] >= start[:, None]) & (col[None, :] < start[:, None] + NA_COLS)
    dc = np.clip(col[None, :] - col[:, None], -(NA_COLS - 1), NA_COLS - 1) + NA_COLS - 1
    sel = np.zeros((128, GRID_W * GRID_W), np.float32)
    sel[dc.reshape(-1), np.arange(GRID_W * GRID_W)] = ok.reshape(-1).astype(np.float32)
    mask = np.where(ok, 0.0, NEG).astype(np.float32).reshape(1, -1)
    return sel, mask


def _rpb_expand(rpb_rows, sel, mask):
    def body(r_ref, s_ref, m_ref, o_ref):
        o_ref[...] = _dot(r_ref[...], s_ref[...], "nn", precision=lax.Precision.HIGHEST) + m_ref[...]

    n = rpb_rows.shape[0]
    return _pc(body, name="rpb_expand", grid=(1,),
               in_specs=[pl.BlockSpec((n, 128), lambda i: (0, 0)), pl.BlockSpec(sel.shape, lambda i: (0, 0)),
                         pl.BlockSpec(mask.shape, lambda i: (0, 0))],
               out_specs=pl.BlockSpec((n, sel.shape[1]), lambda i: (0, 0)),
               out_shape=jax.ShapeDtypeStruct((n, sel.shape[1]), F32), sem=("arbitrary",))(rpb_rows, sel, mask)


def _rpb_reduce(dtiles, sel):
    def body(t_ref, s_ref, o_ref):
        o_ref[...] = _dot(t_ref[...], s_ref[...], "nt", precision=lax.Precision.HIGHEST)

    n = dtiles.shape[0]
    return _pc(body, name="rpb_reduce", grid=(1,),
               in_specs=[pl.BlockSpec(dtiles.shape, lambda i: (0, 0)), pl.BlockSpec(sel.shape, lambda i: (0, 0))],
               out_specs=pl.BlockSpec((n, 128), lambda i: (0, 0)),
               out_shape=jax.ShapeDtypeStruct((n, 128), F32), sem=("arbitrary",))(dtiles, sel)


ROWS_B = 4
BQ_B = ROWS_B * GRID_W


def _b_specs(T):
    return [pl.BlockSpec((BQ_B, HEAD_DIM), lambda h, r: (r, N_HEADS_A + h)),
            pl.BlockSpec((T, HEAD_DIM), lambda h, r: (0, N_HEADS + N_HEADS_A + h)),
            pl.BlockSpec((T, HEAD_DIM), lambda h, r: (0, 2 * N_HEADS + N_HEADS_A + h)),
            pl.BlockSpec((None, N_DR, GRID_W, GRID_W), lambda h, r: (h, 0, 0, 0))]


def _b_row(step, u, rows, t_ref, bias_ref):
    r = step * ROWS_B + u
    rs = jnp.clip(r - NA_ROWS // 2, 0, rows - NA_ROWS)
    first_tile = rs - r + NA_ROWS - 1
    for a in range(NA_ROWS):
        bias_ref[u, :, a * GRID_W:(a + 1) * GRID_W] = t_ref[first_tile + a]
    return slice(u * GRID_W, (u + 1) * GRID_W), pl.ds(pl.multiple_of(rs * GRID_W, GRID_W), WK_B), first_tile


def _attn_b_fwd(qkv, tiles):
    T = qkv.shape[0]
    rows = T // GRID_W

    def body(q_ref, k_ref, v_ref, t_ref, y_ref, lse_ref, bias_ref):
        for u in range(ROWS_B):
            blk, win, _ = _b_row(pl.program_id(1), u, rows, t_ref, bias_ref)
            s = _dot(q_ref[blk, :], k_ref[win, :], "nt") * SCALE + bias_ref[u]
            m = jnp.max(s, axis=-1, keepdims=True)
            p = jnp.exp(s - m)
            den = jnp.sum(p, axis=-1, keepdims=True)
            y_ref[blk, :] = (_dot(p.astype(BF16), v_ref[win, :], "nn") / den).astype(BF16)
            lse_ref[blk, :] = m + jnp.log(den)

    return _pc(body, name="attn_b_fwd", grid=(N_HEADS_B, rows // ROWS_B), in_specs=_b_specs(T),
               out_specs=[pl.BlockSpec((BQ_B, HEAD_DIM), lambda h, r: (r, h)),
                          pl.BlockSpec((None, BQ_B, 1), lambda h, r: (h, r, 0))],
               out_shape=[jax.ShapeDtypeStruct((T, N_HEADS_B * HEAD_DIM), BF16),
                          jax.ShapeDtypeStruct((N_HEADS_B, T, 1), F32)],
               scratch=[pltpu.VMEM((ROWS_B, GRID_W, WK_B), F32)], sem=("parallel", "parallel"))(qkv, qkv, qkv, tiles)


def _attn_b_bwd(qkv, tiles, dy, y, lse):
    T = qkv.shape[0]
    rows = T // GRID_W
    steps = rows // ROWS_B

    def body(q_ref, k_ref, v_ref, t_ref, dy_ref, y_ref, lse_ref, dq_ref, dk_ref, dv_ref, dt_ref,
             bias_ref, ds_ref, dk_acc, dv_acc):
        step = pl.program_id(1)

        @pl.when(step == 0)
        def _():
            dk_acc[...] = jnp.zeros_like(dk_acc)
            dv_acc[...] = jnp.zeros_like(dv_acc)
            dt_ref[...] = jnp.zeros_like(dt_ref)

        for u in range(ROWS_B):
            blk, win, first_tile = _b_row(step, u, rows, t_ref, bias_ref)
            qv = q_ref[blk, :]
            kw = k_ref[win, :]
            dyv = dy_ref[blk, :]
            s = _dot(qv, kw, "nt") * SCALE + bias_ref[u]
            p = jnp.exp(s - lse_ref[blk, :])
            delta = jnp.sum(dyv.astype(F32) * y_ref[blk, :].astype(F32), axis=-1, keepdims=True)
            ds = p * (_dot(dyv, v_ref[win, :], "nt") - delta)
            ds_ref[u] = ds
            for a in range(NA_ROWS):
                dt_ref[first_tile + a] += ds_ref[u, :, a * GRID_W:(a + 1) * GRID_W]
            dsb = ds.astype(BF16)
            dq_ref[blk, :] = (_dot(dsb, kw, "nn") * SCALE).astype(BF16)
            dk_acc[win, :] += _dot(dsb, qv, "tn") * SCALE
            dv_acc[win, :] += _dot(p.astype(BF16), dyv, "tn")

        @pl.when(step == steps - 1)
        def _():
            dk_ref[...] = dk_acc[...].astype(BF16)
            dv_ref[...] = dv_acc[...].astype(BF16)

    blk_spec = pl.BlockSpec((BQ_B, HEAD_DIM), lambda h, r: (r, h))
    whole = pl.BlockSpec((T, HEAD_DIM), lambda h, r: (0, h))
    full = jax.ShapeDtypeStruct((T, N_HEADS_B * HEAD_DIM), BF16)
    return _pc(body, name="attn_b_bwd", grid=(N_HEADS_B, steps),
               in_specs=_b_specs(T) + [blk_spec, blk_spec, pl.BlockSpec((None, BQ_B, 1), lambda h, r: (h, r, 0))],
               out_specs=[blk_spec, whole, whole, pl.BlockSpec((None, N_DR, GRID_W, GRID_W), lambda h, r: (h, 0, 0, 0))],
               out_shape=[full, full, full, jax.ShapeDtypeStruct((N_HEADS_B, N_DR, GRID_W, GRID_W), F32)],
               scratch=[pltpu.VMEM((ROWS_B, GRID_W, WK_B), F32), pltpu.VMEM((ROWS_B, GRID_W, WK_B), F32),
                        pltpu.VMEM((T, HEAD_DIM), F32), pltpu.VMEM((T, HEAD_DIM), F32)],
               sem=("parallel", "arbitrary"))(qkv, qkv, qkv, tiles, dy, y, lse)


def _sds(shape, dtype):
    return jax.ShapeDtypeStruct(shape, dtype)


def _behind(value, token):
    return value if token is None else value + token[0, 0]


def _local_step(x, tgt, norm_mix, b_gate, rpb, norm_mlp, norm_final, late_weights, send):
    T, D = x.shape
    nt = T // TM
    PP, PA = ("parallel", "parallel", "arbitrary"), ("arbitrary", "arbitrary", "arbitrary")
    slopes = jnp.asarray(_alibi_slopes())
    sel_np, mask_np = _toeplitz_consts()
    sel, mask = jnp.asarray(sel_np), jnp.asarray(mask_np)

    def rows(w, tm=TM):
        return pl.BlockSpec((tm, w), lambda i, j, k: (i, j))

    def full_rows(tm, w=D):
        return pl.BlockSpec((tm, w), lambda i, j, k: (i, 0))

    def tall(w):
        return pl.BlockSpec((T, w), lambda j, i, k: (0, i))

    def tall_j(w):
        return pl.BlockSpec((T, w), lambda j, i, k: (0, j))

    rvec = pl.BlockSpec((TS, 1), lambda i, j, k: (i, 0))
    gvec = pl.BlockSpec((1, D), lambda i, j, k: (0, 0))

    h1, r1 = _rmsnorm_fwd(x, norm_mix, "rmsnorm_mix")
    (wq,) = late_weights("qkv", h1)

    def store_bf16(acc, ex, o, pids):
        o[0][...] = acc.astype(BF16)

    (qkv,) = _mm("qkv_proj", (T // TL, N_DEV, 1),
                 [(h1, full_rows(TL), wq, pl.BlockSpec((None, D, 768), lambda i, j, k: (j, 0, 0)), "nn", None)],
                 [], [(_sds((T, 3 * D), BF16), rows(768, TL))], store_bf16, None, PP)

    def gate_epi(acc, ex, o, pids):
        o[0][...] = jax.nn.sigmoid(acc + ex[0][...])

    late_weights("attention", qkv)
    rpb_rows = jnp.pad(rpb.reshape(N_HEADS_B * N_DR, N_DC), ((0, 0), (0, 128 - N_DC)))
    tiles = _rpb_expand(rpb_rows, sel, mask).reshape(N_HEADS_B, N_DR, GRID_W, GRID_W)
    ya, lse_a = _attn_a_fwd(qkv, slopes)
    yb, lse_b = _attn_b_fwd(qkv, tiles)

    wg, wpa, wpb, wo = late_weights("mixer", yb)
    (gates,) = _mm("gate_proj", (T // TL, N_DEV, 1),
                   [(h1, full_rows(TL), wg, pl.BlockSpec((None, D, 512), lambda i, j, k: (j, 0, 0)), "nn", None)],
                   [(b_gate, pl.BlockSpec((1, 512), lambda i, j, k: (0, j)))],
                   [(_sds((T, 2 * D), F32), rows(512, TL))], gate_epi, None, PP)

    def merge_body(ya_ref, wa_ref, yb_ref, wb_ref, ga_ref, gb_ref, pa_ref, pb_ref, mg_ref):
        ya_v, yb_v = ya_ref[...], yb_ref[...]
        for s in range(N_DEV):
            cols = slice(s * 256, (s + 1) * 256)
            pa = _dot(ya_v, wa_ref[s], "nn")
            pb = _dot(yb_v, wb_ref[s], "nn")
            pa_ref[:, cols] = pa
            pb_ref[:, cols] = pb
            mg_ref[:, cols] = (ga_ref[:, cols] * pa + gb_ref[:, cols] * pb).astype(BF16)

    wo2 = wo.reshape(D, D)
    y_spec = pl.BlockSpec((TS, 512), lambda i: (i, 0))
    wp_spec = pl.BlockSpec((N_DEV, 512, 256), lambda i: (0, 0, 0))
    c_spec = pl.BlockSpec((TS, D), lambda i: (i, 0))
    pa, pb, merged = _pc(merge_body, name="merge_proj", grid=(T // TS,),
                         in_specs=[y_spec, wp_spec, y_spec, wp_spec, c_spec, pl.BlockSpec((TS, D), lambda i: (i, 1))],
                         out_specs=[c_spec, c_spec, c_spec],
                         out_shape=[_sds((T, D), F32), _sds((T, D), F32), _sds((T, D), BF16)],
                         sem=("parallel",))(ya, wpa, yb, wpb, gates, gates)

    def add_res(acc, ex, o, pids):
        o[0][...] = ex[0][...] + acc

    (x2,) = _mm("out_proj", (T // TL, 4, 1),
                [(merged, full_rows(TL), wo2, pl.BlockSpec((D, 512), lambda i, j, k: (0, j)), "nn", None)],
                [(x, rows(512, TL))], [(_sds((T, D), F32), rows(512, TL))], add_res, None, PP)

    h2, r2 = _rmsnorm_fwd(x2, norm_mlp, "rmsnorm_mlp")
    wu, wd = late_weights("mlp", h2)
    wd2 = wd.reshape(D_FF, D)

    def up_epi(acc, ex, o, pids):
        o[0][...] = acc.astype(BF16)
        r = jnp.maximum(acc, 0.0)
        o[1][...] = (r * r).astype(BF16)

    u, act = _mm("mlp_up", (T // TL, N_DEV, 1),
                 [(h2, full_rows(TL), wu, pl.BlockSpec((None, D, 1024), lambda i, j, k: (j, 0, 0)), "nn", None)],
                 [], [(_sds((T, D_FF), BF16), rows(1024, TL))] * 2, up_epi, None, PP)
    (x3,) = _mm("mlp_down", (nt, D // 512, 1),
                [(act, full_rows(TM, D_FF), wd2, pl.BlockSpec((D_FF, 512), lambda i, j, k: (0, j)), "nn", None)],
                [(x2, rows(512))], [(_sds((T, D), F32), rows(512))], add_res, None, PP)

    loss, dx3, dx3b, dg3 = _final_norm_loss(x3, norm_final.reshape(1, D), tgt)

    def du_epi(acc, ex, o, pids):
        o[0][...] = (acc * (2.0 * jnp.maximum(ex[0][...].astype(F32), 0.0))).astype(BF16)

    (du,) = _mm("mlp_down_dx", (T // TL, N_DEV, 1),
                [(dx3b, full_rows(TL), wd2, pl.BlockSpec((1024, D), lambda i, j, k: (j, 0)), "nt", None)],
                [(u, rows(1024, TL))], [(_sds((T, D_FF), BF16), rows(1024, TL))], du_epi, None, PP)
    (dwd,) = _mm("mlp_down_dw", (D // 1024, D_FF // 512, 1), [(act, tall(512), dx3b, tall_j(1024), "tn", None)],
                 [], [(_sds((D_FF, D), BF16), pl.BlockSpec((512, 1024), lambda j, i, k: (i, j)))], store_bf16, None, PP)
    dwd = dwd.reshape(N_DEV, D_FF // N_DEV, D)
    (dwu,) = _mm("mlp_up_dw", (N_DEV, D // 512, 1), [(h2, tall(512), du, tall_j(1024), "tn", None)],
                 [], [(_sds((N_DEV, D, 1024), BF16), pl.BlockSpec((None, 512, 1024), lambda j, i, k: (j, i, 0)))],
                 store_bf16, None, PP)

    def store_f32(acc, ex, o, pids):
        o[0][...] = acc

    out_ji = pl.BlockSpec((TM, 512), lambda j, i, k: (i, j))
    a_ji = lambda w: pl.BlockSpec((TM, w), lambda j, i, k: (i, 0))
    w_ji = lambda n: pl.BlockSpec((N_DEV, 512, n), lambda j, i, k: (0, j, 0))
    behind = lambda token: [] if token is None else [(token, pl.BlockSpec(memory_space=pl.ANY))]
    (dh2,) = _mm("mlp_up_dx", (D // 512, nt, 1), [(du, a_ji(D_FF), wu, w_ji(1024), "nt_shards", None)],
                 behind(send("mlp", dict(w_down=dwd, w_up=dwu))), [(_sds((T, D), F32), out_ji)], store_f32, None, PP)
    dx2, dx2b, dg2 = _rmsnorm_bwd("rmsnorm_mlp_bwd", dh2, x2, r2, norm_mlp, dx3, True)

    (dwo,) = _mm("out_proj_dw", (D // 1024, D // 512, 1), [(merged, tall(512), dx2b, tall_j(1024), "tn", None)],
                 [], [(_sds((D, D), BF16), pl.BlockSpec((512, 1024), lambda j, i, k: (i, j)))], store_bf16, None, PP)

    def dmerge_epi(acc, ex, o, pids):
        ga, gb, pa_v, pb_v = (e[...] for e in ex)
        o[0][...] = (acc * ga).astype(BF16)
        o[1][...] = (acc * gb).astype(BF16)
        dga = acc * pa_v * ga * (1.0 - ga)
        dgb = acc * pb_v * gb * (1.0 - gb)
        o[2][...] = dga.astype(BF16)
        o[3][...] = dgb.astype(BF16)
        sa = jnp.sum(dga, axis=0, keepdims=True)
        sb = jnp.sum(dgb, axis=0, keepdims=True)
        first = pids[1] == 0

        @pl.when(first)
        def _():
            o[4][...] = sa
            o[5][...] = sb

        @pl.when(jnp.logical_not(first))
        def _():
            o[4][...] += sa
            o[5][...] += sb

    cj = pl.BlockSpec((TM, 512), lambda j, i, k: (i, j))
    bj = pl.BlockSpec((1, 512), lambda j, i, k: (0, j))
    dpa, dpb, dga, dgb, dba, dbb = _mm(
        "out_proj_dx", (4, nt, 1),
        [(dx2b, pl.BlockSpec((TM, D), lambda j, i, k: (i, 0)), wo2, pl.BlockSpec((512, D), lambda j, i, k: (j, 0)), "nt", None)],
        [(gates, cj), (gates, pl.BlockSpec((TM, 512), lambda j, i, k: (i, 4 + j))), (pa, cj), (pb, cj)],
        [(_sds((T, D), BF16), cj)] * 4 + [(_sds((1, D), F32), bj)] * 2, dmerge_epi, None,
        ("parallel", "arbitrary", "arbitrary"))

    def store_shards(acc, ex, o, pids):
        for s in range(4):
            o[0][s] = acc[:, s * 256:(s + 1) * 256].astype(BF16)

    def proj_dw(name, y, dp):
        (dw,) = _mm(name, (2, 1, 1), [(y, tall(512), dp, tall_j(1024), "tn", None)],
                    [], [(_sds((N_DEV, 512, 256), BF16), pl.BlockSpec((4, 512, 256), lambda j, i, k: (j, 0, 0)))],
                    store_shards, None, PP)
        return dw

    def proj_dx(name, dp, w):
        (dy,) = _mm(name, (nt, 1, 1),
                    [(dp, full_rows(TM), w, pl.BlockSpec((N_DEV, 512, 256), lambda i, j, k: (0, 0, 0)), "nt_shards", None)],
                    [], [(_sds((T, 512), BF16), full_rows(TM, 512))], store_bf16, None, PP)
        return dy

    def in_dw(name, dact, n):
        (dw,) = _mm(name, (N_DEV, D // 512, 1), [(h1, tall(512), dact, tall_j(n), "tn", None)],
                    [], [(_sds((N_DEV, D, n), BF16), pl.BlockSpec((None, 512, n), lambda j, i, k: (j, i, 0)))],
                    store_bf16, None, PP)
        return dw

    dgate = jnp.concatenate([dga, dgb], axis=1)
    dwg = in_dw("gate_proj_dw", dgate, 512)
    dwpa = proj_dw("proj_a_dw", ya, dpa)
    dwpb = proj_dw("proj_b_dw", yb, dpb)
    token = send("proj", dict(w_gate=dwg, w_out=dwo.reshape(N_DEV, D // N_DEV, D), w_proj_a=dwpa, w_proj_b=dwpb))
    dya = proj_dx("proj_a_dx", dpa, wpa)
    dyb = proj_dx("proj_b_dx", dpb, wpb)

    dq_a, dk_a, dv_a = _attn_a_bwd(qkv, _behind(slopes, token), dya, ya, lse_a)
    dq_b, dk_b, dv_b, dtiles = _attn_b_bwd(qkv, tiles, dyb, yb, lse_b)
    drpb = _rpb_reduce(dtiles.reshape(N_HEADS_B * N_DR, GRID_W * GRID_W), sel)[:, :N_DC]
    dqkv = jnp.concatenate([dq_a[0], dq_a[1], dq_a[2], dq_b, dk_a[0], dk_a[1], dk_a[2], dk_b,
                            dv_a[0], dv_a[1], dv_a[2], dv_b], axis=1)
    dwq = in_dw("qkv_proj_dw", dqkv, 768)
    (dh1,) = _mm("in_proj_dx", (D // 512, nt, 1),
                 [(dqkv, a_ji(3 * D), wq, w_ji(768), "nt_shards", None), (dgate, a_ji(2 * D), wg, w_ji(512), "nt_shards", None)],
                 behind(send("in", dict(w_qkv=dwq))), [(_sds((T, D), F32), out_ji)], store_f32, None, PP)
    grad_x, dg1 = _rmsnorm_bwd("rmsnorm_mix_bwd", dh1, x, r1, norm_mix, dx2, False)

    small = dict(norm_mix=dg1, b_gate=jnp.concatenate([dba, dbb], axis=1), rpb=drpb, norm_mlp=dg2, norm_final=dg3)
    return loss, grad_x, small


BIG = ("w_qkv", "w_gate", "w_proj_a", "w_proj_b", "w_out", "w_up", "w_down")
SMALL = ("norm_mix", "b_gate", "rpb", "norm_mlp", "norm_final")
SMALL_ROWS = 96


def _place():
    return lax.axis_index("x"), lax.axis_index("y"), lax.axis_index("c")


def _peer(x, y, c, mask):
    return x ^ (mask >> 2), y ^ ((mask >> 1) & 1), c ^ (mask & 1)


_HBM = pl.BlockSpec(memory_space=pltpu.HBM)
_SEM = pl.BlockSpec(memory_space=pltpu.SEMAPHORE)
_ANY = pl.BlockSpec(memory_space=pl.ANY)
_EFFECT = pltpu.SideEffectType.DATAFLOW_SIDE_EFFECTING


def _hbm(a):
    return pltpu.with_memory_space_constraint(a, pltpu.HBM)


def _other_chips(x, y):
    return [(1 - x, y), (x, 1 - y), (1 - x, 1 - y)]


def _block(x, y, c):
    return 4 * x + 2 * y + c


def _rdma(src, dst, send_sem, recv_sem, to):
    return pltpu.make_async_remote_copy(src_ref=src, dst_ref=dst, send_sem=send_sem, recv_sem=recv_sem, device_id=to,
                                        device_id_type=MESH)


def _gather_start(name, shards):
    n = len(shards)

    def body(*refs):
        srcs, lands = refs[:n], refs[n:2 * n]
        outs = refs[2 * n:]
        send, recv_d2d, recv_ici, local = outs[:n], outs[n:2 * n], outs[2 * n:3 * n], outs[3 * n:4 * n]
        token_ref = outs[6 * n]
        x, y, c = _place()
        for a in range(n):
            mine = lands[a].at[_block(x, y, c)]
            _rdma(srcs[a], mine, send[a].at[0], recv_d2d[a].at[0], (x, y, 1 - c)).start()
            for j, chip in enumerate(_other_chips(x, y)):
                _rdma(srcs[a], mine, send[a].at[1 + j], recv_ici[a].at[j], (*chip, c)).start()
        for a in range(n):
            pltpu.make_async_copy(srcs[a], lands[a].at[_block(x, y, c)], local[a].at[0]).start()
        token_ref[...] = jnp.zeros_like(token_ref)

    lands = [_hbm(lax.empty((N_DEV,) + s.shape, s.dtype)) for s in shards]
    srcs = [_hbm(s) for s in shards]
    dma = pltpu.SemaphoreType.DMA
    res = pl.pallas_call(
        body, name=name,
        out_shape=[dma((4,))] * n + [dma((1,))] * n + [dma((3,))] * n + [dma((1,))] * n
        + [pltpu.HBM(s.shape, s.dtype) for s in srcs] + [pltpu.HBM(l.shape, l.dtype) for l in lands]
        + [jax.ShapeDtypeStruct((8, 128), F32)],
        in_specs=[_HBM] * (2 * n), out_specs=[_SEM] * (4 * n) + [_HBM] * (2 * n) + [pl.BlockSpec(memory_space=pltpu.VMEM)],
        input_output_aliases={i: 4 * n + i for i in range(2 * n)},
        compiler_params=pltpu.CompilerParams(has_side_effects=_EFFECT))(*srcs, *lands)
    keys = ("send", "recv_d2d", "recv_ici", "local", "src", "land")
    return [dict(zip(keys, (res[k * n + a] for k in range(6)))) for a in range(n)], res[6 * n]


def _gather_forward(name, handles, after):
    n = len(handles)

    def body(*refs):
        lands, recv_ici = refs[:n], refs[n:2 * n]
        outs = refs[2 * n + 1:]
        fwd_send, fwd_recv = outs[n:2 * n], outs[2 * n:3 * n]
        x, y, c = _place()
        for a in range(n):
            for j, chip in enumerate(_other_chips(x, y)):
                blk = lands[a].at[_block(*chip, c)]
                _rdma(blk, blk, fwd_send[a].at[j], recv_ici[a].at[j], (*chip, c)).wait_recv()
                _rdma(blk, blk, fwd_send[a].at[j], fwd_recv[a].at[j], (x, y, 1 - c)).start()

    dma = pltpu.SemaphoreType.DMA
    lands = [h["land"] for h in handles]
    res = pl.pallas_call(
        body, name=name, out_shape=[pltpu.HBM(l.shape, l.dtype) for l in lands] + [dma((3,))] * (2 * n),
        in_specs=[_HBM] * n + [_SEM] * n + [_ANY], out_specs=[_HBM] * n + [_SEM] * (2 * n),
        input_output_aliases={i: i for i in range(n)},
        compiler_params=pltpu.CompilerParams(has_side_effects=_EFFECT))(*lands, *[h["recv_ici"] for h in handles], after)
    return [dict(h, land=res[a], fwd_send=res[n + a], fwd_recv=res[2 * n + a]) for a, h in enumerate(handles)]


def _gather_wait(name, handles, after):
    n = len(handles)

    def body(*refs):
        srcs, lands = refs[:n], refs[n:2 * n]
        send, recv_d2d, local, fwd_send, fwd_recv = (refs[(2 + k) * n:(3 + k) * n] for k in range(5))
        x, y, c = _place()
        sibling = (x, y, 1 - c)
        for a in range(n):
            mine = lands[a].at[_block(x, y, c)]
            _rdma(srcs[a], mine, send[a].at[0], recv_d2d[a].at[0], sibling).wait_send()
            _rdma(srcs[a], lands[a].at[_block(*sibling)], send[a].at[0], recv_d2d[a].at[0], sibling).wait_recv()
            pltpu.make_async_copy(srcs[a], mine, local[a].at[0]).wait()
            for j, chip in enumerate(_other_chips(x, y)):
                _rdma(srcs[a], mine, send[a].at[1 + j], fwd_recv[a].at[j], (*chip, c)).wait_send()
                blk = lands[a].at[_block(*chip, c)]
                _rdma(blk, blk, fwd_send[a].at[j], fwd_recv[a].at[j], sibling).wait_send()
                got = lands[a].at[_block(*chip, 1 - c)]
                _rdma(got, got, fwd_send[a].at[j], fwd_recv[a].at[j], sibling).wait_recv()

    srcs = [h["src"] for h in handles]
    lands = [h["land"] for h in handles]
    sems = [h[k] for k in ("send", "recv_d2d", "local", "fwd_send", "fwd_recv") for h in handles]
    res = pl.pallas_call(
        body, name=name, out_shape=[pltpu.HBM(s.shape, s.dtype) for s in srcs] + [pltpu.HBM(l.shape, l.dtype) for l in lands],
        in_specs=[_HBM] * (2 * n) + [_SEM] * (5 * n) + [_ANY], out_specs=[_HBM] * (2 * n),
        input_output_aliases={i: i for i in range(2 * n)},
        compiler_params=pltpu.CompilerParams(has_side_effects=_EFFECT))(*srcs, *lands, *sems, after)
    return list(res[n:])


def _gather_small(small):
    def body(in_ref, out_ref, send_sems, recv_sems, local_sem):
        x, y, c = _place()
        me = 4 * x + 2 * y + c
        copies = [pltpu.make_async_copy(in_ref, out_ref.at[me], local_sem)]
        for mask in range(1, N_DEV):
            copies.append(pltpu.make_async_remote_copy(
                src_ref=in_ref, dst_ref=out_ref.at[me], send_sem=send_sems.at[mask - 1],
                recv_sem=recv_sems.at[mask - 1], device_id=_peer(x, y, c, mask), device_id_type=MESH))
        for cp in copies:
            cp.start()
        for cp in copies:
            cp.wait()

    any_spec = pl.BlockSpec(memory_space=pl.ANY)
    return pl.pallas_call(
        body, name="gather_small", in_specs=[any_spec], out_specs=any_spec,
        out_shape=jax.ShapeDtypeStruct((N_DEV,) + small.shape, small.dtype),
        scratch_shapes=[pltpu.SemaphoreType.DMA((7,)), pltpu.SemaphoreType.DMA((7,)), pltpu.SemaphoreType.DMA])(small)


def _send_copy(src_refs, land_refs, send_sems, recv_sems, a, mask, scatter, x, y, c):
    px, py, pc = _peer(x, y, c, mask)
    src = src_refs[a].at[4 * px + 2 * py + pc] if scatter else src_refs[a]
    return pltpu.make_async_remote_copy(
        src_ref=src, dst_ref=land_refs[a].at[4 * x + 2 * y + c], send_sem=send_sems[a].at[mask - 1],
        recv_sem=recv_sems[a].at[mask - 1], device_id=(px, py, pc), device_id_type=MESH)


def _local_copy(src_refs, land_refs, local_sems, a, scatter, x, y, c):
    me = 4 * x + 2 * y + c
    return pltpu.make_async_copy(src_refs[a].at[me] if scatter else src_refs[a], land_refs[a].at[me], local_sems[a].at[0])


def _send_start(name, srcs, scatter, after=None):
    n = len(srcs)
    n_in = 2 * n + (after is not None)
    shapes = [s.shape[1:] if scatter else s.shape for s in srcs]

    def body(*refs):
        src_refs, land_refs = refs[:n], refs[n:2 * n]
        outs = refs[n_in:]
        send_sems, recv_sems, local_sems = outs[:n], outs[n:2 * n], outs[2 * n:3 * n]
        token_ref = outs[5 * n]
        x, y, c = _place()
        for a in range(n):
            for mask in range(1, N_DEV):
                _send_copy(src_refs, land_refs, send_sems, recv_sems, a, mask, scatter, x, y, c).start()
        for a in range(n):
            _local_copy(src_refs, land_refs, local_sems, a, scatter, x, y, c).start()
        token_ref[...] = jnp.zeros_like(token_ref)

    lands = [pltpu.with_memory_space_constraint(lax.empty((N_DEV,) + sh, s.dtype), pltpu.HBM) for sh, s in zip(shapes, srcs)]
    srcs = [pltpu.with_memory_space_constraint(s, pltpu.HBM) for s in srcs]
    res = pl.pallas_call(
        body, name=name,
        out_shape=[pltpu.SemaphoreType.DMA((7,))] * (2 * n) + [pltpu.SemaphoreType.DMA((1,))] * n
        + [pltpu.HBM(s.shape, s.dtype) for s in srcs] + [pltpu.HBM(l.shape, l.dtype) for l in lands]
        + [jax.ShapeDtypeStruct((8, 128), F32)],
        in_specs=[_HBM] * (2 * n) + [pl.BlockSpec(memory_space=pl.ANY)] * (n_in - 2 * n),
        out_specs=[_SEM] * (3 * n) + [_HBM] * (2 * n) + [pl.BlockSpec(memory_space=pltpu.VMEM)],
        input_output_aliases={i: 3 * n + i for i in range(2 * n)},
        compiler_params=pltpu.CompilerParams(has_side_effects=_EFFECT))(*srcs, *lands, *([] if after is None else [after]))
    handles = [tuple(res[k * n + a] for k in range(5)) for a in range(n)]
    return handles, res[5 * n]


def _send_wait(name, handles, scatter, after):
    n = len(handles)

    def body(*refs):
        src_refs, land_refs = refs[:n], refs[n:2 * n]
        send_sems, recv_sems, local_sems = refs[2 * n:3 * n], refs[3 * n:4 * n], refs[4 * n:5 * n]
        x, y, c = _place()
        for a in range(n):
            for mask in range(1, N_DEV):
                cp = _send_copy(src_refs, land_refs, send_sems, recv_sems, a, mask, scatter, x, y, c)
                cp.wait_send()
                cp.wait_recv()
            _local_copy(src_refs, land_refs, local_sems, a, scatter, x, y, c).wait()

    srcs = [h[3] for h in handles]
    lands = [h[4] for h in handles]
    res = pl.pallas_call(
        body, name=name, out_shape=[pltpu.HBM(s.shape, s.dtype) for s in srcs] + [pltpu.HBM(l.shape, l.dtype) for l in lands],
        in_specs=[_HBM] * (2 * n) + [_SEM] * (3 * n) + [pl.BlockSpec(memory_space=pl.ANY)], out_specs=[_HBM] * (2 * n),
        input_output_aliases={i: i for i in range(2 * n)},
        compiler_params=pltpu.CompilerParams(has_side_effects=_EFFECT))(
            *srcs, *lands, *[h[0] for h in handles], *[h[1] for h in handles], *[h[2] for h in handles], after)
    return list(res[n:])


def _adamw(name, parts, w, m, v):
    K, n = w.shape
    tr = min(K, 128)

    def body(p_ref, w_ref, m_ref, v_ref, g_ref, d_ref, nm_ref, nv_ref):
        g = p_ref[0].astype(F32)
        for i in range(1, N_DEV):
            g = g + p_ref[i].astype(F32)
        nm = ADAM_B1 * m_ref[...] + (1.0 - ADAM_B1) * g
        nv = ADAM_B2 * v_ref[...] + (1.0 - ADAM_B2) * jnp.square(g)
        m_hat = nm / (1.0 - ADAM_B1 ** ADAM_STEP)
        v_hat = nv / (1.0 - ADAM_B2 ** ADAM_STEP)
        g_ref[...] = g
        d_ref[...] = -ADAM_LR * (m_hat / (jnp.sqrt(v_hat) + ADAM_EPS) + ADAM_WD * w_ref[...])
        nm_ref[...] = nm
        nv_ref[...] = nv

    blk = pl.BlockSpec((tr, n), lambda i: (i, 0))
    return _pc(body, name=name, grid=(K // tr,), in_specs=[pl.BlockSpec((N_DEV, tr, n), lambda i: (0, i, 0)), blk, blk, blk],
               out_specs=[blk] * 4, out_shape=[jax.ShapeDtypeStruct((K, n), F32)] * 4, sem=("parallel",))(parts, w, m, v)


def _pack_small(vals):
    flat = jnp.concatenate([v.reshape(-1) for v in vals])
    return jnp.pad(flat, (0, SMALL_ROWS * 128 - flat.shape[0])).reshape(SMALL_ROWS, 128)


def _unpack_small(packed, shapes):
    flat, out, off = packed.reshape(-1), [], 0
    for s in shapes:
        size = int(np.prod(s))
        out.append(flat[off:off + size].reshape(s))
        off += size
    return out


def kernel(x, norm_mix, w_qkv, w_gate, b_gate, rpb, w_proj_a, w_proj_b, w_out, norm_mlp, w_up, w_down, norm_final, loss_target, m_norm_mix, m_w_qkv, m_w_gate, m_b_gate, m_rpb, m_w_proj_a, m_w_proj_b, m_w_out, m_norm_mlp, m_w_up, m_w_down, m_norm_final, v_norm_mix, v_w_qkv, v_w_gate, v_b_gate, v_rpb, v_w_proj_a, v_w_proj_b, v_w_out, v_norm_mlp, v_w_up, v_w_down, v_norm_final):
    w = dict(norm_mix=norm_mix, w_qkv=w_qkv, w_gate=w_gate, b_gate=b_gate, rpb=rpb, w_proj_a=w_proj_a, w_proj_b=w_proj_b,
             w_out=w_out, norm_mlp=norm_mlp, w_up=w_up, w_down=w_down, norm_final=norm_final)
    m = dict(norm_mix=m_norm_mix, w_qkv=m_w_qkv, w_gate=m_w_gate, b_gate=m_b_gate, rpb=m_rpb, w_proj_a=m_w_proj_a,
             w_proj_b=m_w_proj_b, w_out=m_w_out, norm_mlp=m_norm_mlp, w_up=m_w_up, w_down=m_w_down, norm_final=m_norm_final)
    v = dict(norm_mix=v_norm_mix, w_qkv=v_w_qkv, w_gate=v_w_gate, b_gate=v_b_gate, rpb=v_rpb, w_proj_a=v_w_proj_a,
             w_proj_b=v_w_proj_b, w_out=v_w_out, norm_mlp=v_norm_mlp, w_up=v_w_up, w_down=v_w_down, norm_final=v_norm_final)
    order = ("norm_mix", "w_qkv", "w_gate", "b_gate", "rpb", "w_proj_a", "w_proj_b", "w_out", "norm_mlp", "w_up",
             "w_down", "norm_final")

    shard = {n: w[n][0].astype(BF16) for n in BIG}
    stages = dict(qkv=("w_qkv",), gate=("w_gate", "w_proj_a", "w_proj_b", "w_out"), mlp=("w_up", "w_down"))
    names = [n for s in stages.values() for n in s]
    handles, token = _gather_start("gather_start", [shard[n] for n in names])
    handles = dict(zip(names, handles))

    def late_weights(stage, after):
        if stage == "qkv":
            return tuple(_gather_wait("gather_wait_qkv", _gather_forward("gather_forward_qkv", [handles["w_qkv"]], after), after))
        if stage == "attention":
            handles["gate"] = _gather_forward("gather_forward_gate", [handles[n] for n in stages["gate"]], after)
            return ()
        if stage == "mixer":
            handles["mlp"] = _gather_forward("gather_forward_mlp", [handles[n] for n in stages["mlp"]], after)
            return tuple(_gather_wait("gather_wait_gate", handles["gate"], after))
        return tuple(_gather_wait("gather_wait_mlp", handles["mlp"], after))

    sent = {}

    def send(stage, grads):
        handles, tok = _send_start("grads_start_" + stage, list(grads.values()), True)
        sent[stage] = (tuple(grads), handles)
        return tok

    loss, grad_x, small = _local_step(x[0], loss_target[0], _behind(norm_mix, token), b_gate, rpb[0], norm_mlp, norm_final,
                                      late_weights, send)

    grad, delta, new_m, new_v = {}, {}, {}, {}
    after = grad_x
    for stage in ("mlp", "proj", "in"):
        names, handles = sent[stage]
        for n, parts in zip(names, _send_wait("grads_wait_" + stage, handles, True, after)):
            res = _adamw("adamw_" + n, parts, w[n][0], m[n][0], v[n][0])
            grad[n], delta[n], new_m[n], new_v[n] = (r[None] for r in res)
            after = res[1]
    res = _adamw("adamw_small", _gather_small(_pack_small([small[n] for n in SMALL])),
                 _pack_small([w[n] for n in SMALL]), _pack_small([m[n] for n in SMALL]), _pack_small([v[n] for n in SMALL]))
    shapes = [w[n].shape for n in SMALL]
    for tree, packed in zip((grad, delta, new_m, new_v), res):
        for n, val in zip(SMALL, _unpack_small(packed, shapes)):
            tree[n] = val

    total = lax.psum(loss[0, 0], ("x", "y", "c"))
    return (total, grad_x[None], *[grad[n] for n in order], *[delta[n] for n in order],
            *[new_m[n] for n in order], *[new_v[n] for n in order])
```

```python
import numpy as np
import jax
import jax.numpy as jnp
from jax import lax
from jax.experimental import pallas as pl
from jax.experimental.pallas import tpu as pltpu

F32 = jnp.float32
BF16 = jnp.bfloat16
MESH = pl.DeviceIdType.MESH

D_MODEL = 2048
HEAD_DIM = 128
N_HEADS = 16
N_HEADS_A = 12
N_HEADS_B = 4
DILATIONS = (1, 4, 16)
HALF_WINDOW = 64
GRID_W = 64
NA_ROWS = 8
NA_COLS = 16
D_FF = 4 * D_MODEL
N_DEV = 8
EPS = 1e-6
NEG = -1e30
SCALE = HEAD_DIM ** -0.5
ADAM_LR, ADAM_B1, ADAM_B2, ADAM_EPS, ADAM_WD, ADAM_STEP = 0.001, 0.9, 0.999, 1e-08, 0.01, 10

VMEM_LIMIT = 56 * 1024 * 1024
TM = 512
TL = 1024
TS = 256
BQ_A = 128


def _pc(body, *, name, grid, in_specs, out_specs, out_shape, scratch=(), sem=None):
    return pl.pallas_call(
        body, name=name, grid=grid, in_specs=in_specs, out_specs=out_specs, out_shape=out_shape,
        scratch_shapes=list(scratch),
        compiler_params=pltpu.CompilerParams(dimension_semantics=sem, vmem_limit_bytes=VMEM_LIMIT))


_DOT_DIMS = {"nn": (((1,), (0,)), ((), ())), "nt": (((1,), (1,)), ((), ())), "tn": (((0,), (0,)), ((), ()))}


def _dot(a, b, mode, precision=None):
    if mode == "nt_shards":
        n = b.shape[2]
        out = _dot(a[:, :n], b[0], "nt")
        for s in range(1, b.shape[0]):
            out = out + _dot(a[:, s * n:(s + 1) * n], b[s], "nt")
        return out
    return lax.dot_general(a, b, _DOT_DIMS[mode], preferred_element_type=F32, precision=precision)


def _mm(name, grid, pairs, extras, outs, epilogue, acc_shape, sem):
    n_pairs, n_extra, n_out = len(pairs), len(extras), len(outs)
    k_axis = len(grid) - 1
    nk = grid[k_axis]

    def body(*refs):
        pids = [pl.program_id(ax) for ax in range(len(grid))]
        ab = refs[:2 * n_pairs]
        ex = refs[2 * n_pairs:2 * n_pairs + n_extra]
        o = refs[2 * n_pairs + n_extra:2 * n_pairs + n_extra + n_out]
        part = None
        for p, pair in enumerate(pairs):
            a = ab[2 * p][...]
            if pair[5] is not None:
                a = pair[5](a)
            d = _dot(a, ab[2 * p + 1][...], pair[4])
            part = d if part is None else part + d
        if nk == 1:
            epilogue(part, ex, o, pids)
        else:
            acc = refs[-1]
            k = pids[k_axis]

            @pl.when(k == 0)
            def _():
                acc[...] = part

            @pl.when(k > 0)
            def _():
                acc[...] += part

            @pl.when(k == nk - 1)
            def _():
                epilogue(acc[...], ex, o, pids)

    operands, in_specs = [], []
    for a, a_spec, b, b_spec, _, _ in pairs:
        operands += [a, b]
        in_specs += [a_spec, b_spec]
    for e, e_spec in extras:
        operands.append(e)
        in_specs.append(e_spec)
    res = _pc(body, name=name, grid=grid, in_specs=in_specs, out_specs=[s for _, s in outs],
              out_shape=[o for o, _ in outs], scratch=[pltpu.VMEM(acc_shape, F32)] if nk > 1 else [], sem=sem)(*operands)
    return res


def _rmsnorm_fwd(x, g, name):
    T, D = x.shape

    def body(x_ref, g_ref, h_ref, r_ref):
        xv = x_ref[...]
        r = lax.rsqrt(jnp.mean(xv * xv, axis=-1, keepdims=True) + EPS)
        h_ref[...] = (xv * r * g_ref[...]).astype(BF16)
        r_ref[...] = r

    return _pc(body, name=name, grid=(T // TM,),
               in_specs=[pl.BlockSpec((TM, D), lambda i: (i, 0)), pl.BlockSpec((1, D), lambda i: (0, 0))],
               out_specs=[pl.BlockSpec((TM, D), lambda i: (i, 0)), pl.BlockSpec((TM, 1), lambda i: (i, 0))],
               out_shape=[jax.ShapeDtypeStruct((T, D), BF16), jax.ShapeDtypeStruct((T, 1), F32)],
               sem=("parallel",))(x, g)


def _rmsnorm_bwd_block(dh, x, r, g, dres):
    xh = x * r
    dxh = dh * g
    dx = dres + r * (dxh - xh * jnp.mean(dxh * xh, axis=-1, keepdims=True))
    return dx, jnp.sum(dh * xh, axis=0, keepdims=True)


def _rmsnorm_bwd(name, dh, x, r, g, dres, want_bf16):
    T, D = x.shape

    def body(dh_ref, x_ref, r_ref, g_ref, dres_ref, *outs):
        dx, dgp = _rmsnorm_bwd_block(dh_ref[...], x_ref[...], r_ref[...], g_ref[...], dres_ref[...])
        outs[0][...] = dx
        if want_bf16:
            outs[1][...] = dx.astype(BF16)
        first = pl.program_id(0) == 0

        @pl.when(first)
        def _():
            outs[-1][...] = dgp

        @pl.when(jnp.logical_not(first))
        def _():
            outs[-1][...] += dgp

    row = pl.BlockSpec((TS, D), lambda i: (i, 0))
    vec = pl.BlockSpec((1, D), lambda i: (0, 0))
    n_row = 2 if want_bf16 else 1
    return _pc(body, name=name, grid=(T // TS,), in_specs=[row, row, pl.BlockSpec((TS, 1), lambda i: (i, 0)), vec, row],
               out_specs=[row] * n_row + [vec],
               out_shape=[jax.ShapeDtypeStruct((T, D), F32)] + [jax.ShapeDtypeStruct((T, D), BF16)] * (n_row - 1)
               + [jax.ShapeDtypeStruct((1, D), F32)], sem=("arbitrary",))(dh, x, r, g, dres)


def _final_norm_loss(x3, g, tgt):
    T, D = x3.shape

    def body(x_ref, g_ref, t_ref, loss_ref, dx_ref, dxb_ref, dg_ref):
        i = pl.program_id(0)
        xv = x_ref[...]
        gv = g_ref[...]
        r = lax.rsqrt(jnp.mean(xv * xv, axis=-1, keepdims=True) + EPS)
        xh = xv * r
        e = xh * gv - t_ref[...]
        part = 0.5 * jnp.sum(jnp.mean(e * e, axis=-1, keepdims=True))
        dy = e * (1.0 / D)
        dxh = dy * gv
        dx = r * (dxh - xh * jnp.mean(dxh * xh, axis=-1, keepdims=True))
        dx_ref[...] = dx
        dxb_ref[...] = dx.astype(BF16)
        dgp = jnp.sum(dy * xh, axis=0, keepdims=True)

        @pl.when(i == 0)
        def _():
            loss_ref[...] = jnp.zeros(loss_ref.shape, F32) + part
            dg_ref[...] = dgp

        @pl.when(i > 0)
        def _():
            loss_ref[...] += part
            dg_ref[...] += dgp

    row = pl.BlockSpec((TM, D), lambda i: (i, 0))
    vec = pl.BlockSpec((1, D), lambda i: (0, 0))
    return _pc(body, name="final_norm_loss", grid=(T // TM,), in_specs=[row, vec, row],
               out_specs=[pl.BlockSpec((8, 128), lambda i: (0, 0)), row, row, vec],
               out_shape=[jax.ShapeDtypeStruct((8, 128), F32), jax.ShapeDtypeStruct((T, D), F32),
                          jax.ShapeDtypeStruct((T, D), BF16), jax.ShapeDtypeStruct((1, D), F32)],
               sem=("arbitrary",))(x3, g, tgt)


def _alibi_slopes():
    return np.asarray(2.0 ** (-8.0 * np.arange(1, N_HEADS_A + 1) / N_HEADS_A), dtype=np.float32)


def _a_windows(T):
    half = [HALF_WINDOW * d for d in DILATIONS]
    return half, [min(BQ_A + 2 * w, T) for w in half]


def _a_bias(slope, g, offset, wk, half):
    d = DILATIONS[g]
    dlt = offset + lax.broadcasted_iota(jnp.int32, (BQ_A, wk), 1) - lax.broadcasted_iota(jnp.int32, (BQ_A, wk), 0)
    ad = jnp.maximum(dlt, -dlt)
    ok = (ad <= half) & ((ad & (d - 1)) == 0)
    return jnp.where(ok, -slope * ad.astype(F32), NEG)


def _a_scores(q, kw, slope, g, offset, wk, half):
    return _dot(q, kw, "nt") * SCALE + _a_bias(slope, g, offset, wk, half)


N_JOINT = 2


def _a_qkv_specs(T, inner):
    specs = []
    for g in range(N_JOINT):
        specs.append(pl.BlockSpec((BQ_A, HEAD_DIM), lambda j, i, g=g: (i, 4 * g + j)))
        specs.append(pl.BlockSpec((T, HEAD_DIM), lambda j, i, g=g: (0, N_HEADS + 4 * g + j)))
        specs.append(pl.BlockSpec((T, HEAD_DIM), lambda j, i, g=g: (0, 2 * N_HEADS + 4 * g + j)))
    return specs


def _attn_a_fwd(qkv, slopes, o_last, lse_last):
    T = qkv.shape[0]
    nq = T // BQ_A
    half, wks = _a_windows(T)

    def body(sl_ref, *refs):
        qkv_refs, (o_ref, l_ref, y_ref, lse_ref) = refs[:3 * N_JOINT], refs[3 * N_JOINT:]
        j = pl.program_id(0)
        t0 = pl.program_id(1) * BQ_A
        ss, vws = [], []
        m = l_ref[...]
        for g in range(N_JOINT):
            q_ref, k_ref, v_ref = qkv_refs[3 * g:3 * g + 3]
            ws = pl.multiple_of(jnp.clip(t0 - half[g], 0, T - wks[g]), 64)
            ss.append(_a_scores(q_ref[...], k_ref[pl.ds(ws, wks[g]), :], sl_ref[4 * g + j], g, ws - t0, wks[g], half[g]))
            vws.append(v_ref[pl.ds(ws, wks[g]), :])
            m = jnp.maximum(m, jnp.max(ss[g], axis=-1, keepdims=True))
        den = jnp.exp(l_ref[...] - m)
        acc = den * o_ref[...].astype(F32)
        for g in range(N_JOINT):
            p = jnp.exp(ss[g] - m)
            den = den + jnp.sum(p, axis=-1, keepdims=True)
            acc = acc + _dot(p.astype(BF16), vws[g], "nn")
        y_ref[...] = (acc / den).astype(BF16)
        lse_ref[...] = m + jnp.log(den)

    blk = pl.BlockSpec((BQ_A, HEAD_DIM), lambda j, i: (i, j))
    col = pl.BlockSpec((None, BQ_A, 1), lambda j, i: (j, i, 0))
    return _pc(body, name="attn_a_fwd", grid=(4, nq),
               in_specs=[pl.BlockSpec(memory_space=pltpu.SMEM)] + _a_qkv_specs(T, nq) + [blk, col], out_specs=[blk, col],
               out_shape=[jax.ShapeDtypeStruct((T, 4 * HEAD_DIM), BF16), jax.ShapeDtypeStruct((4, T, 1), F32)],
               sem=("parallel", "parallel"))(slopes, *([qkv] * (3 * N_JOINT)), o_last, lse_last)


def _attn_a_bwd(qkv, slopes, dy, y, lse):
    T = qkv.shape[0]
    nq = T // BQ_A
    half, wks = _a_windows(T)

    def body(sl_ref, *refs):
        qkv_refs = refs[:3 * N_JOINT]
        dy_ref, y_ref, lse_ref, dq_ref, dk_ref, dv_ref, delta_ref, dk_acc, dv_acc = refs[3 * N_JOINT:]
        j = pl.program_id(0)
        i = pl.program_id(1)
        t0 = i * BQ_A

        @pl.when(i == 0)
        def _():
            dk_acc[...] = jnp.zeros_like(dk_acc)
            dv_acc[...] = jnp.zeros_like(dv_acc)

        dyv = dy_ref[...]
        delta = jnp.sum(dyv.astype(F32) * y_ref[...].astype(F32), axis=-1, keepdims=True)
        delta_ref[...] = delta
        lse_v = lse_ref[...]
        for g in range(N_JOINT):
            q_ref, k_ref, v_ref = qkv_refs[3 * g:3 * g + 3]
            ws = pl.multiple_of(jnp.clip(t0 - half[g], 0, T - wks[g]), 64)
            win = pl.ds(ws, wks[g])
            qv = q_ref[...]
            kw = k_ref[win, :]
            s = _a_scores(qv, kw, sl_ref[4 * g + j], g, ws - t0, wks[g], half[g])
            p = jnp.exp(s - lse_v)
            dp = _dot(dyv, v_ref[win, :], "nt")
            ds = (p * (dp - delta)).astype(BF16)
            dq_ref[g] = (_dot(ds, kw, "nn") * SCALE).astype(BF16)
            dk_acc[g, win, :] += _dot(ds, qv, "tn") * SCALE
            dv_acc[g, win, :] += _dot(p.astype(BF16), dyv, "tn")

        @pl.when(i == nq - 1)
        def _():
            dk_ref[...] = dk_acc[...].astype(BF16)
            dv_ref[...] = dv_acc[...].astype(BF16)

    blk = pl.BlockSpec((BQ_A, HEAD_DIM), lambda j, i: (i, j))
    col = pl.BlockSpec((None, BQ_A, 1), lambda j, i: (j, i, 0))
    whole = pl.BlockSpec((N_JOINT, T, HEAD_DIM), lambda j, i: (0, 0, j))
    full = jax.ShapeDtypeStruct((N_JOINT, T, 4 * HEAD_DIM), BF16)
    return _pc(body, name="attn_a_bwd", grid=(4, nq),
               in_specs=[pl.BlockSpec(memory_space=pltpu.SMEM)] + _a_qkv_specs(T, nq) + [blk, blk, col],
               out_specs=[pl.BlockSpec((N_JOINT, BQ_A, HEAD_DIM), lambda j, i: (0, i, j)), whole, whole, col],
               out_shape=[full, full, full, jax.ShapeDtypeStruct((4, T, 1), F32)],
               scratch=[pltpu.VMEM((N_JOINT, T, HEAD_DIM), F32), pltpu.VMEM((N_JOINT, T, HEAD_DIM), F32)],
               sem=("parallel", "arbitrary"))(slopes, *([qkv] * (3 * N_JOINT)), dy, y, lse)


SUB_A = 4


def _to_residue_major(a, d, axis=0):
    if d == 1:
        return a
    sh = a.shape
    a = a.reshape(sh[:axis] + (sh[axis] // d, d) + sh[axis + 1:])
    return jnp.swapaxes(a, axis, axis + 1).reshape(sh)


def _to_token_order(a, d, axis=0):
    if d == 1:
        return a
    sh = a.shape
    a = a.reshape(sh[:axis] + (d, sh[axis] // d) + sh[axis + 1:])
    return jnp.swapaxes(a, axis, axis + 1).reshape(sh)


def _band_bias(slope, offset, wk):
    dlt = offset + lax.broadcasted_iota(jnp.int32, (BQ_A, wk), 1) - lax.broadcasted_iota(jnp.int32, (BQ_A, wk), 0)
    ad = jnp.maximum(dlt, -dlt)
    return jnp.where(ad <= HALF_WINDOW, -slope * ad.astype(F32), NEG)


def _band_window(step, u, seg, wk):
    t0 = (step * SUB_A + u) * BQ_A
    lo = (t0 // seg) * seg
    ws = pl.multiple_of(jnp.clip(t0 - HALF_WINDOW, lo, lo + seg - wk), 64)
    return slice(u * BQ_A, (u + 1) * BQ_A), pl.ds(ws, wk), ws - t0


def _band_specs(T, cols):
    bs = SUB_A * BQ_A
    assert T % bs == 0 and T // max(DILATIONS) >= BQ_A, "a query sub-block must lie inside one segment"
    return [pl.BlockSpec(memory_space=pltpu.SMEM),
            pl.BlockSpec((bs, HEAD_DIM), lambda j, i: (i, cols[0] + j)),
            pl.BlockSpec((T, HEAD_DIM), lambda j, i: (0, cols[1] + j)),
            pl.BlockSpec((T, HEAD_DIM), lambda j, i: (0, cols[2] + j))]


def _banded_fwd(name, src, cols, seg, slopes):
    T = src.shape[0]
    bs = SUB_A * BQ_A
    wk = min(BQ_A + 2 * HALF_WINDOW, seg)

    def body(sl_ref, q_ref, k_ref, v_ref, o_ref, lse_ref):
        slope = sl_ref[pl.program_id(0)]
        for u in range(SUB_A):
            rows, win, offset = _band_window(pl.program_id(1), u, seg, wk)
            s = _dot(q_ref[rows, :], k_ref[win, :], "nt") * SCALE + _band_bias(slope, offset, wk)
            m = jnp.max(s, axis=-1, keepdims=True)
            p = jnp.exp(s - m)
            den = jnp.sum(p, axis=-1, keepdims=True)
            o_ref[rows, :] = (_dot(p.astype(BF16), v_ref[win, :], "nn") / den).astype(BF16)
            lse_ref[rows, :] = m + jnp.log(den)

    return _pc(body, name=name, grid=(4, T // bs), in_specs=_band_specs(T, cols),
               out_specs=[pl.BlockSpec((bs, HEAD_DIM), lambda j, i: (i, j)), pl.BlockSpec((None, bs, 1), lambda j, i: (j, i, 0))],
               out_shape=[jax.ShapeDtypeStruct((T, 4 * HEAD_DIM), BF16), jax.ShapeDtypeStruct((4, T, 1), F32)],
               sem=("parallel", "parallel"))(slopes, src, src, src)


def _banded_bwd(name, src, cols, seg, slopes, dy, delta_lse):
    T = src.shape[0]
    bs = SUB_A * BQ_A
    steps = T // bs
    wk = min(BQ_A + 2 * HALF_WINDOW, seg)

    def body(sl_ref, q_ref, k_ref, v_ref, dy_ref, delta_ref, lse_ref, dq_ref, dk_ref, dv_ref, dk_acc, dv_acc):
        slope = sl_ref[pl.program_id(0)]
        step = pl.program_id(1)

        @pl.when(step == 0)
        def _():
            dk_acc[...] = jnp.zeros_like(dk_acc)
            dv_acc[...] = jnp.zeros_like(dv_acc)

        for u in range(SUB_A):
            rows, win, offset = _band_window(step, u, seg, wk)
            qv, kw, dyv = q_ref[rows, :], k_ref[win, :], dy_ref[rows, :]
            s = _dot(qv, kw, "nt") * SCALE + _band_bias(slope, offset, wk)
            p = jnp.exp(s - lse_ref[rows, :])
            ds = (p * (_dot(dyv, v_ref[win, :], "nt") - delta_ref[rows, :])).astype(BF16)
            dq_ref[rows, :] = (_dot(ds, kw, "nn") * SCALE).astype(BF16)
            dk_acc[win, :] += _dot(ds, qv, "tn") * SCALE
            dv_acc[win, :] += _dot(p.astype(BF16), dyv, "tn")

        @pl.when(step == steps - 1)
        def _():
            dk_ref[...] = dk_acc[...].astype(BF16)
            dv_ref[...] = dv_acc[...].astype(BF16)

    blk = pl.BlockSpec((bs, HEAD_DIM), lambda j, i: (i, j))
    whole = pl.BlockSpec((T, HEAD_DIM), lambda j, i: (0, j))
    full = jax.ShapeDtypeStruct((T, 4 * HEAD_DIM), BF16)
    return _pc(body, name=name, grid=(4, steps),
               in_specs=_band_specs(T, cols) + [blk, pl.BlockSpec((None, bs, 1), lambda j, i: (j, i, 0)),
                                               pl.BlockSpec((None, bs, 1), lambda j, i: (4 + j, i, 0))],
               out_specs=[blk, whole, whole], out_shape=[full, full, full],
               scratch=[pltpu.VMEM((T, HEAD_DIM), F32), pltpu.VMEM((T, HEAD_DIM), F32)],
               sem=("parallel", "arbitrary"))(slopes, src, src, src, dy, delta_lse, delta_lse)


def _last_group_source(qkv):
    third = qkv.shape[1] // 3
    own = jnp.concatenate([qkv[:, t * third + 512 * N_JOINT:t * third + 512 * (N_JOINT + 1)] for t in range(3)], axis=1)
    return _to_residue_major(own, DILATIONS[N_JOINT])


def _dilated_fwd(qkv, slopes):
    d = DILATIONS[N_JOINT]
    o, lse = _banded_fwd("attn_a_last_fwd", _last_group_source(qkv), (0, 4, 8), qkv.shape[0] // d,
                         slopes[4 * N_JOINT:] * float(d))
    return _attn_a_fwd(qkv, slopes, _to_token_order(o, d), _to_token_order(lse, d, axis=1))


def _dilated_bwd(qkv, slopes, dy, y, lse):
    d = DILATIONS[N_JOINT]
    dq, dk, dv, delta = _attn_a_bwd(qkv, slopes, dy, y, lse)
    delta_lse = _to_residue_major(jnp.concatenate([delta, lse], axis=0), d, axis=1)
    last = _banded_bwd("attn_a_last_bwd", _last_group_source(qkv), (0, 4, 8), qkv.shape[0] // d,
                       slopes[4 * N_JOINT:] * float(d), _to_residue_major(dy, d), delta_lse)
    return [jnp.concatenate([joint[g] for g in range(N_JOINT)] + [_to_token_order(own, d)], axis=1)
            for joint, own in zip((dq, dk, dv), last)]


N_DR = 2 * NA_ROWS - 1
N_DC = 2 * NA_COLS - 1
WK_B = NA_ROWS * GRID_W


def _toeplitz_consts():
    col = np.arange(GRID_W)
    start = np.clip(col - NA_COLS // 2, 0, GRID_W - NA_COLS)
    ok = (col[None, :] >= start[:, None]) & (col[None, :] < start[:, None] + NA_COLS)
    dc = np.clip(col[None, :] - col[:, None], -(NA_COLS - 1), NA_COLS - 1) + NA_COLS - 1
    sel = np.zeros((128, GRID_W * GRID_W), np.float32)
    sel[dc.reshape(-1), np.arange(GRID_W * GRID_W)] = ok.reshape(-1).astype(np.float32)
    mask = np.where(ok, 0.0, NEG).astype(np.float32).reshape(1, -1)
    return sel, mask


def _rpb_expand(rpb_rows, sel, mask):
    def body(r_ref, s_ref, m_ref, o_ref):
        o_ref[...] = _dot(r_ref[...], s_ref[...], "nn", precision=lax.Precision.HIGHEST) + m_ref[...]

    n = rpb_rows.shape[0]
    return _pc(body, name="rpb_expand", grid=(1,),
               in_specs=[pl.BlockSpec((n, 128), lambda i: (0, 0)), pl.BlockSpec(sel.shape, lambda i: (0, 0)),
                         pl.BlockSpec(mask.shape, lambda i: (0, 0))],
               out_specs=pl.BlockSpec((n, sel.shape[1]), lambda i: (0, 0)),
               out_shape=jax.ShapeDtypeStruct((n, sel.shape[1]), F32), sem=("arbitrary",))(rpb_rows, sel, mask)


def _rpb_reduce(dtiles, sel):
    def body(t_ref, s_ref, o_ref):
        o_ref[...] = _dot(t_ref[...], s_ref[...], "nt", precision=lax.Precision.HIGHEST)

    n = dtiles.shape[0]
    return _pc(body, name="rpb_reduce", grid=(1,),
               in_specs=[pl.BlockSpec(dtiles.shape, lambda i: (0, 0)), pl.BlockSpec(sel.shape, lambda i: (0, 0))],
               out_specs=pl.BlockSpec((n, 128), lambda i: (0, 0)),
               out_shape=jax.ShapeDtypeStruct((n, 128), F32), sem=("arbitrary",))(dtiles, sel)


ROWS_B = 4
BQ_B = ROWS_B * GRID_W


def _b_specs(T):
    return [pl.BlockSpec((BQ_B, HEAD_DIM), lambda h, r: (r, N_HEADS_A + h)),
            pl.BlockSpec((T, HEAD_DIM), lambda h, r: (0, N_HEADS + N_HEADS_A + h)),
            pl.BlockSpec((T, HEAD_DIM), lambda h, r: (0, 2 * N_HEADS + N_HEADS_A + h)),
            pl.BlockSpec((None, N_DR, GRID_W, GRID_W), lambda h, r: (h, 0, 0, 0))]


def _b_row(step, u, rows, t_ref, bias_ref):
    r = step * ROWS_B + u
    rs = jnp.clip(r - NA_ROWS // 2, 0, rows - NA_ROWS)
    first_tile = rs - r + NA_ROWS - 1
    for a in range(NA_ROWS):
        bias_ref[u, :, a * GRID_W:(a + 1) * GRID_W] = t_ref[first_tile + a]
    return slice(u * GRID_W, (u + 1) * GRID_W), pl.ds(pl.multiple_of(rs * GRID_W, GRID_W), WK_B), first_tile


def _attn_b_fwd(qkv, tiles):
    T = qkv.shape[0]
    rows = T // GRID_W

    def body(q_ref, k_ref, v_ref, t_ref, y_ref, lse_ref, bias_ref):
        for u in range(ROWS_B):
            blk, win, _ = _b_row(pl.program_id(1), u, rows, t_ref, bias_ref)
            s = _dot(q_ref[blk, :], k_ref[win, :], "nt") * SCALE + bias_ref[u]
            m = jnp.max(s, axis=-1, keepdims=True)
            p = jnp.exp(s - m)
            den = jnp.sum(p, axis=-1, keepdims=True)
            y_ref[blk, :] = (_dot(p.astype(BF16), v_ref[win, :], "nn") / den).astype(BF16)
            lse_ref[blk, :] = m + jnp.log(den)

    return _pc(body, name="attn_b_fwd", grid=(N_HEADS_B, rows // ROWS_B), in_specs=_b_specs(T),
               out_specs=[pl.BlockSpec((BQ_B, HEAD_DIM), lambda h, r: (r, h)),
                          pl.BlockSpec((None, BQ_B, 1), lambda h, r: (h, r, 0))],
               out_shape=[jax.ShapeDtypeStruct((T, N_HEADS_B * HEAD_DIM), BF16),
                          jax.ShapeDtypeStruct((N_HEADS_B, T, 1), F32)],
               scratch=[pltpu.VMEM((ROWS_B, GRID_W, WK_B), F32)], sem=("parallel", "parallel"))(qkv, qkv, qkv, tiles)


def _attn_b_bwd(qkv, tiles, dy, y, lse):
    T = qkv.shape[0]
    rows = T // GRID_W
    steps = rows // ROWS_B

    def body(q_ref, k_ref, v_ref, t_ref, dy_ref, y_ref, lse_ref, dq_ref, dk_ref, dv_ref, dt_ref,
             bias_ref, ds_ref, dk_acc, dv_acc):
        step = pl.program_id(1)

        @pl.when(step == 0)
        def _():
            dk_acc[...] = jnp.zeros_like(dk_acc)
            dv_acc[...] = jnp.zeros_like(dv_acc)
            dt_ref[...] = jnp.zeros_like(dt_ref)

        for u in range(ROWS_B):
            blk, win, first_tile = _b_row(step, u, rows, t_ref, bias_ref)
            qv = q_ref[blk, :]
            kw = k_ref[win, :]
            dyv = dy_ref[blk, :]
            s = _dot(qv, kw, "nt") * SCALE + bias_ref[u]
            p = jnp.exp(s - lse_ref[blk, :])
            delta = jnp.sum(dyv.astype(F32) * y_ref[blk, :].astype(F32), axis=-1, keepdims=True)
            ds = p * (_dot(dyv, v_ref[win, :], "nt") - delta)
            ds_ref[u] = ds
            for a in range(NA_ROWS):
                dt_ref[first_tile + a] += ds_ref[u, :, a * GRID_W:(a + 1) * GRID_W]
            dsb = ds.astype(BF16)
            dq_ref[blk, :] = (_dot(dsb, kw, "nn") * SCALE).astype(BF16)
            dk_acc[win, :] += _dot(dsb, qv, "tn") * SCALE
            dv_acc[win, :] += _dot(p.astype(BF16), dyv, "tn")

        @pl.when(step == steps - 1)
        def _():
            dk_ref[...] = dk_acc[...].astype(BF16)
            dv_ref[...] = dv_acc[...].astype(BF16)

    blk_spec = pl.BlockSpec((BQ_B, HEAD_DIM), lambda h, r: (r, h))
    whole = pl.BlockSpec((T, HEAD_DIM), lambda h, r: (0, h))
    full = jax.ShapeDtypeStruct((T, N_HEADS_B * HEAD_DIM), BF16)
    return _pc(body, name="attn_b_bwd", grid=(N_HEADS_B, steps),
               in_specs=_b_specs(T) + [blk_spec, blk_spec, pl.BlockSpec((None, BQ_B, 1), lambda h, r: (h, r, 0))],
               out_specs=[blk_spec, whole, whole, pl.BlockSpec((None, N_DR, GRID_W, GRID_W), lambda h, r: (h, 0, 0, 0))],
               out_shape=[full, full, full, jax.ShapeDtypeStruct((N_HEADS_B, N_DR, GRID_W, GRID_W), F32)],
               scratch=[pltpu.VMEM((ROWS_B, GRID_W, WK_B), F32), pltpu.VMEM((ROWS_B, GRID_W, WK_B), F32),
                        pltpu.VMEM((T, HEAD_DIM), F32), pltpu.VMEM((T, HEAD_DIM), F32)],
               sem=("parallel", "arbitrary"))(qkv, qkv, qkv, tiles, dy, y, lse)


def _sds(shape, dtype):
    return jax.ShapeDtypeStruct(shape, dtype)


def _behind(value, token):
    return value if token is None else value + token[0, 0]


def _local_step(x, tgt, norm_mix, b_gate, rpb, norm_mlp, norm_final, late_weights, send):
    T, D = x.shape
    nt = T // TM
    PP, PA = ("parallel", "parallel", "arbitrary"), ("arbitrary", "arbitrary", "arbitrary")
    slopes = jnp.asarray(_alibi_slopes())
    sel_np, mask_np = _toeplitz_consts()
    sel, mask = jnp.asarray(sel_np), jnp.asarray(mask_np)

    def rows(w, tm=TM):
        return pl.BlockSpec((tm, w), lambda i, j, k: (i, j))

    def full_rows(tm, w=D):
        return pl.BlockSpec((tm, w), lambda i, j, k: (i, 0))

    def tall(w):
        return pl.BlockSpec((T, w), lambda j, i, k: (0, i))

    def tall_j(w):
        return pl.BlockSpec((T, w), lambda j, i, k: (0, j))

    rvec = pl.BlockSpec((TS, 1), lambda i, j, k: (i, 0))
    gvec = pl.BlockSpec((1, D), lambda i, j, k: (0, 0))

    h1, r1 = _rmsnorm_fwd(x, norm_mix, "rmsnorm_mix")
    (wq,) = late_weights("qkv", h1)

    def store_bf16(acc, ex, o, pids):
        o[0][...] = acc.astype(BF16)

    (qkv,) = _mm("qkv_proj", (T // TL, N_DEV, 1),
                 [(h1, full_rows(TL), wq, pl.BlockSpec((None, D, 768), lambda i, j, k: (j, 0, 0)), "nn", None)],
                 [], [(_sds((T, 3 * D), BF16), rows(768, TL))], store_bf16, None, PP)

    def gate_epi(acc, ex, o, pids):
        o[0][...] = jax.nn.sigmoid(acc + ex[0][...])

    late_weights("attention", qkv)
    rpb_rows = jnp.pad(rpb.reshape(N_HEADS_B * N_DR, N_DC), ((0, 0), (0, 128 - N_DC)))
    tiles = _rpb_expand(rpb_rows, sel, mask).reshape(N_HEADS_B, N_DR, GRID_W, GRID_W)
    ya, lse_a = _dilated_fwd(qkv, slopes)
    yb, lse_b = _attn_b_fwd(qkv, tiles)

    wg, wpa, wpb, wo = late_weights("mixer", ya)
    (gates,) = _mm("gate_proj", (T // TL, N_DEV, 1),
                   [(h1, full_rows(TL), wg, pl.BlockSpec((None, D, 512), lambda i, j, k: (j, 0, 0)), "nn", None)],
                   [(b_gate, pl.BlockSpec((1, 512), lambda i, j, k: (0, j)))],
                   [(_sds((T, 2 * D), F32), rows(512, TL))], gate_epi, None, PP)

    def merge_body(ya_ref, wa_ref, yb_ref, wb_ref, ga_ref, gb_ref, pa_ref, pb_ref, mg_ref):
        ya_v, yb_v = ya_ref[...], yb_ref[...]
        for s in range(N_DEV):
            cols = slice(s * 256, (s + 1) * 256)
            pa = _dot(ya_v, wa_ref[s], "nn")
            pb = _dot(yb_v, wb_ref[s], "nn")
            pa_ref[:, cols] = pa
            pb_ref[:, cols] = pb
            mg_ref[:, cols] = (ga_ref[:, cols] * pa + gb_ref[:, cols] * pb).astype(BF16)

    wo2 = wo.reshape(D, D)
    y_spec = pl.BlockSpec((TS, 512), lambda i: (i, 0))
    wp_spec = pl.BlockSpec((N_DEV, 512, 256), lambda i: (0, 0, 0))
    c_spec = pl.BlockSpec((TS, D), lambda i: (i, 0))
    pa, pb, merged = _pc(merge_body, name="merge_proj", grid=(T // TS,),
                         in_specs=[y_spec, wp_spec, y_spec, wp_spec, c_spec, pl.BlockSpec((TS, D), lambda i: (i, 1))],
                         out_specs=[c_spec, c_spec, c_spec],
                         out_shape=[_sds((T, D), F32), _sds((T, D), F32), _sds((T, D), BF16)],
                         sem=("parallel",))(ya, wpa, yb, wpb, gates, gates)

    def add_res(acc, ex, o, pids):
        o[0][...] = ex[0][...] + acc

    (x2,) = _mm("out_proj", (T // TL, 4, 1),
                [(merged, full_rows(TL), wo2, pl.BlockSpec((D, 512), lambda i, j, k: (0, j)), "nn", None)],
                [(x, rows(512, TL))], [(_sds((T, D), F32), rows(512, TL))], add_res, None, PP)

    h2, r2 = _rmsnorm_fwd(x2, norm_mlp, "rmsnorm_mlp")
    wu, wd = late_weights("mlp", h2)
    wd2 = wd.reshape(D_FF, D)

    def up_epi(acc, ex, o, pids):
        o[0][...] = acc.astype(BF16)
        r = jnp.maximum(acc, 0.0)
        o[1][...] = (r * r).astype(BF16)

    u, act = _mm("mlp_up", (T // TL, N_DEV, 1),
                 [(h2, full_rows(TL), wu, pl.BlockSpec((None, D, 1024), lambda i, j, k: (j, 0, 0)), "nn", None)],
                 [], [(_sds((T, D_FF), BF16), rows(1024, TL))] * 2, up_epi, None, PP)
    (x3,) = _mm("mlp_down", (nt, D // 512, 1),
                [(act, full_rows(TM, D_FF), wd2, pl.BlockSpec((D_FF, 512), lambda i, j, k: (0, j)), "nn", None)],
                [(x2, rows(512))], [(_sds((T, D), F32), rows(512))], add_res, None, PP)

    loss, dx3, dx3b, dg3 = _final_norm_loss(x3, norm_final.reshape(1, D), tgt)

    def du_epi(acc, ex, o, pids):
        o[0][...] = (acc * (2.0 * jnp.maximum(ex[0][...].astype(F32), 0.0))).astype(BF16)

    (du,) = _mm("mlp_down_dx", (T // TL, N_DEV, 1),
                [(dx3b, full_rows(TL), wd2, pl.BlockSpec((1024, D), lambda i, j, k: (j, 0)), "nt", None)],
                [(u, rows(1024, TL))], [(_sds((T, D_FF), BF16), rows(1024, TL))], du_epi, None, PP)
    (dwd,) = _mm("mlp_down_dw", (D // 1024, D_FF // 512, 1), [(act, tall(512), dx3b, tall_j(1024), "tn", None)],
                 [], [(_sds((D_FF, D), BF16), pl.BlockSpec((512, 1024), lambda j, i, k: (i, j)))], store_bf16, None, PP)
    dwd = dwd.reshape(N_DEV, D_FF // N_DEV, D)
    (dwu,) = _mm("mlp_up_dw", (N_DEV, D // 512, 1), [(h2, tall(512), du, tall_j(1024), "tn", None)],
                 [], [(_sds((N_DEV, D, 1024), BF16), pl.BlockSpec((None, 512, 1024), lambda j, i, k: (j, i, 0)))],
                 store_bf16, None, PP)

    def store_f32(acc, ex, o, pids):
        o[0][...] = acc

    out_ji = pl.BlockSpec((TM, 512), lambda j, i, k: (i, j))
    a_ji = lambda w: pl.BlockSpec((TM, w), lambda j, i, k: (i, 0))
    w_ji = lambda n: pl.BlockSpec((N_DEV, 512, n), lambda j, i, k: (0, j, 0))
    behind = lambda token: [] if token is None else [(token, pl.BlockSpec(memory_space=pl.ANY))]
    (dh2,) = _mm("mlp_up_dx", (D // 512, nt, 1), [(du, a_ji(D_FF), wu, w_ji(1024), "nt_shards", None)],
                 behind(send("mlp", dict(w_down=dwd, w_up=dwu))), [(_sds((T, D), F32), out_ji)], store_f32, None, PP)
    dx2, dx2b, dg2 = _rmsnorm_bwd("rmsnorm_mlp_bwd", dh2, x2, r2, norm_mlp, dx3, True)

    (dwo,) = _mm("out_proj_dw", (D // 1024, D // 512, 1), [(merged, tall(512), dx2b, tall_j(1024), "tn", None)],
                 [], [(_sds((D, D), BF16), pl.BlockSpec((512, 1024), lambda j, i, k: (i, j)))], store_bf16, None, PP)

    def dmerge_epi(acc, ex, o, pids):
        ga, gb, pa_v, pb_v = (e[...] for e in ex)
        o[0][...] = (acc * ga).astype(BF16)
        o[1][...] = (acc * gb).astype(BF16)
        dga = acc * pa_v * ga * (1.0 - ga)
        dgb = acc * pb_v * gb * (1.0 - gb)
        o[2][...] = dga.astype(BF16)
        o[3][...] = dgb.astype(BF16)
        sa = jnp.sum(dga, axis=0, keepdims=True)
        sb = jnp.sum(dgb, axis=0, keepdims=True)
        first = pids[1] == 0

        @pl.when(first)
        def _():
            o[4][...] = sa
            o[5][...] = sb

        @pl.when(jnp.logical_not(first))
        def _():
            o[4][...] += sa
            o[5][...] += sb

    cj = pl.BlockSpec((TM, 512), lambda j, i, k: (i, j))
    bj = pl.BlockSpec((1, 512), lambda j, i, k: (0, j))
    dpa, dpb, dga, dgb, dba, dbb = _mm(
        "out_proj_dx", (4, nt, 1),
        [(dx2b, pl.BlockSpec((TM, D), lambda j, i, k: (i, 0)), wo2, pl.BlockSpec((512, D), lambda j, i, k: (j, 0)), "nt", None)],
        [(gates, cj), (gates, pl.BlockSpec((TM, 512), lambda j, i, k: (i, 4 + j))), (pa, cj), (pb, cj)],
        [(_sds((T, D), BF16), cj)] * 4 + [(_sds((1, D), F32), bj)] * 2, dmerge_epi, None,
        ("parallel", "arbitrary", "arbitrary"))

    def store_shards(acc, ex, o, pids):
        for s in range(4):
            o[0][s] = acc[:, s * 256:(s + 1) * 256].astype(BF16)

    def proj_dw(name, y, dp):
        (dw,) = _mm(name, (2, 1, 1), [(y, tall(512), dp, tall_j(1024), "tn", None)],
                    [], [(_sds((N_DEV, 512, 256), BF16), pl.BlockSpec((4, 512, 256), lambda j, i, k: (j, 0, 0)))],
                    store_shards, None, PP)
        return dw

    def proj_dx(name, dp, w):
        (dy,) = _mm(name, (nt, 1, 1),
                    [(dp, full_rows(TM), w, pl.BlockSpec((N_DEV, 512, 256), lambda i, j, k: (0, 0, 0)), "nt_shards", None)],
                    [], [(_sds((T, 512), BF16), full_rows(TM, 512))], store_bf16, None, PP)
        return dy

    def in_dw(name, dact, n):
        (dw,) = _mm(name, (N_DEV, D // 512, 1), [(h1, tall(512), dact, tall_j(n), "tn", None)],
                    [], [(_sds((N_DEV, D, n), BF16), pl.BlockSpec((None, 512, n), lambda j, i, k: (j, i, 0)))],
                    store_bf16, None, PP)
        return dw

    dgate = jnp.concatenate([dga, dgb], axis=1)
    dwg = in_dw("gate_proj_dw", dgate, 512)
    dwpa = proj_dw("proj_a_dw", ya, dpa)
    dwpb = proj_dw("proj_b_dw", yb, dpb)
    token = send("proj", dict(w_gate=dwg, w_out=dwo.reshape(N_DEV, D // N_DEV, D), w_proj_a=dwpa, w_proj_b=dwpb))
    dya = proj_dx("proj_a_dx", dpa, wpa)
    dyb = proj_dx("proj_b_dx", dpb, wpb)

    dq_a, dk_a, dv_a = _dilated_bwd(qkv, _behind(slopes, token), dya, ya, lse_a)
    dq_b, dk_b, dv_b, dtiles = _attn_b_bwd(qkv, tiles, dyb, yb, lse_b)
    drpb = _rpb_reduce(dtiles.reshape(N_HEADS_B * N_DR, GRID_W * GRID_W), sel)[:, :N_DC]
    dqkv = jnp.concatenate([dq_a, dq_b, dk_a, dk_b, dv_a, dv_b], axis=1)
    dwq = in_dw("qkv_proj_dw", dqkv, 768)
    (dh1,) = _mm("in_proj_dx", (D // 512, nt, 1),
                 [(dqkv, a_ji(3 * D), wq, w_ji(768), "nt_shards", None), (dgate, a_ji(2 * D), wg, w_ji(512), "nt_shards", None)],
                 behind(send("in", dict(w_qkv=dwq))), [(_sds((T, D), F32), out_ji)], store_f32, None, PP)
    grad_x, dg1 = _rmsnorm_bwd("rmsnorm_mix_bwd", dh1, x, r1, norm_mix, dx2, False)

    small = dict(norm_mix=dg1, b_gate=jnp.concatenate([dba, dbb], axis=1), rpb=drpb, norm_mlp=dg2, norm_final=dg3)
    return loss, grad_x, small


BIG = ("w_qkv", "w_gate", "w_proj_a", "w_proj_b", "w_out", "w_up", "w_down")
SMALL = ("norm_mix", "b_gate", "rpb", "norm_mlp", "norm_final")
SMALL_ROWS = 96


def _place():
    return lax.axis_index("x"), lax.axis_index("y"), lax.axis_index("c")


def _peer(x, y, c, mask):
    return x ^ (mask >> 2), y ^ ((mask >> 1) & 1), c ^ (mask & 1)


_HBM = pl.BlockSpec(memory_space=pltpu.HBM)
_SEM = pl.BlockSpec(memory_space=pltpu.SEMAPHORE)
_ANY = pl.BlockSpec(memory_space=pl.ANY)
_EFFECT = pltpu.SideEffectType.DATAFLOW_SIDE_EFFECTING


def _hbm(a):
    return pltpu.with_memory_space_constraint(a, pltpu.HBM)


def _other_chips(x, y):
    return [(1 - x, y), (x, 1 - y), (1 - x, 1 - y)]


def _block(x, y, c):
    return 4 * x + 2 * y + c


def _rdma(src, dst, send_sem, recv_sem, to):
    return pltpu.make_async_remote_copy(src_ref=src, dst_ref=dst, send_sem=send_sem, recv_sem=recv_sem, device_id=to,
                                        device_id_type=MESH)


def _gather_start(name, shards):
    n = len(shards)

    def body(*refs):
        srcs, lands = refs[:n], refs[n:2 * n]
        outs = refs[2 * n:]
        send, recv_d2d, recv_ici, local = outs[:n], outs[n:2 * n], outs[2 * n:3 * n], outs[3 * n:4 * n]
        token_ref = outs[6 * n]
        x, y, c = _place()
        for a in range(n):
            mine = lands[a].at[_block(x, y, c)]
            _rdma(srcs[a], mine, send[a].at[0], recv_d2d[a].at[0], (x, y, 1 - c)).start()
            for j, chip in enumerate(_other_chips(x, y)):
                _rdma(srcs[a], mine, send[a].at[1 + j], recv_ici[a].at[j], (*chip, c)).start()
        for a in range(n):
            pltpu.make_async_copy(srcs[a], lands[a].at[_block(x, y, c)], local[a].at[0]).start()
        token_ref[...] = jnp.zeros_like(token_ref)

    lands = [_hbm(lax.empty((N_DEV,) + s.shape, s.dtype)) for s in shards]
    srcs = [_hbm(s) for s in shards]
    dma = pltpu.SemaphoreType.DMA
    res = pl.pallas_call(
        body, name=name,
        out_shape=[dma((4,))] * n + [dma((1,))] * n + [dma((3,))] * n + [dma((1,))] * n
        + [pltpu.HBM(s.shape, s.dtype) for s in srcs] + [pltpu.HBM(l.shape, l.dtype) for l in lands]
        + [jax.ShapeDtypeStruct((8, 128), F32)],
        in_specs=[_HBM] * (2 * n), out_specs=[_SEM] * (4 * n) + [_HBM] * (2 * n) + [pl.BlockSpec(memory_space=pltpu.VMEM)],
        input_output_aliases={i: 4 * n + i for i in range(2 * n)},
        compiler_params=pltpu.CompilerParams(has_side_effects=_EFFECT))(*srcs, *lands)
    keys = ("send", "recv_d2d", "recv_ici", "local", "src", "land")
    return [dict(zip(keys, (res[k * n + a] for k in range(6)))) for a in range(n)], res[6 * n]


def _gather_forward(name, handles, after):
    n = len(handles)

    def body(*refs):
        lands, recv_ici = refs[:n], refs[n:2 * n]
        outs = refs[2 * n + 1:]
        fwd_send, fwd_recv = outs[n:2 * n], outs[2 * n:3 * n]
        x, y, c = _place()
        for a in range(n):
            for j, chip in enumerate(_other_chips(x, y)):
                blk = lands[a].at[_block(*chip, c)]
                _rdma(blk, blk, fwd_send[a].at[j], recv_ici[a].at[j], (*chip, c)).wait_recv()
                _rdma(blk, blk, fwd_send[a].at[j], fwd_recv[a].at[j], (x, y, 1 - c)).start()

    dma = pltpu.SemaphoreType.DMA
    lands = [h["land"] for h in handles]
    res = pl.pallas_call(
        body, name=name, out_shape=[pltpu.HBM(l.shape, l.dtype) for l in lands] + [dma((3,))] * (2 * n),
        in_specs=[_HBM] * n + [_SEM] * n + [_ANY], out_specs=[_HBM] * n + [_SEM] * (2 * n),
        input_output_aliases={i: i for i in range(n)},
        compiler_params=pltpu.CompilerParams(has_side_effects=_EFFECT))(*lands, *[h["recv_ici"] for h in handles], after)
    return [dict(h, land=res[a], fwd_send=res[n + a], fwd_recv=res[2 * n + a]) for a, h in enumerate(handles)]


def _gather_wait(name, handles, after):
    n = len(handles)

    def body(*refs):
        srcs, lands = refs[:n], refs[n:2 * n]
        send, recv_d2d, local, fwd_send, fwd_recv = (refs[(2 + k) * n:(3 + k) * n] for k in range(5))
        x, y, c = _place()
        sibling = (x, y, 1 - c)
        for a in range(n):
            mine = lands[a].at[_block(x, y, c)]
            _rdma(srcs[a], mine, send[a].at[0], recv_d2d[a].at[0], sibling).wait_send()
            _rdma(srcs[a], lands[a].at[_block(*sibling)], send[a].at[0], recv_d2d[a].at[0], sibling).wait_recv()
            pltpu.make_async_copy(srcs[a], mine, local[a].at[0]).wait()
            for j, chip in enumerate(_other_chips(x, y)):
                _rdma(srcs[a], mine, send[a].at[1 + j], fwd_recv[a].at[j], (*chip, c)).wait_send()
                blk = lands[a].at[_block(*chip, c)]
                _rdma(blk, blk, fwd_send[a].at[j], fwd_recv[a].at[j], sibling).wait_send()
                got = lands[a].at[_block(*chip, 1 - c)]
                _rdma(got, got, fwd_send[a].at[j], fwd_recv[a].at[j], sibling).wait_recv()

    srcs = [h["src"] for h in handles]
    lands = [h["land"] for h in handles]
    sems = [h[k] for k in ("send", "recv_d2d", "local", "fwd_send", "fwd_recv") for h in handles]
    res = pl.pallas_call(
        body, name=name, out_shape=[pltpu.HBM(s.shape, s.dtype) for s in srcs] + [pltpu.HBM(l.shape, l.dtype) for l in lands],
        in_specs=[_HBM] * (2 * n) + [_SEM] * (5 * n) + [_ANY], out_specs=[_HBM] * (2 * n),
        input_output_aliases={i: i for i in range(2 * n)},
        compiler_params=pltpu.CompilerParams(has_side_effects=_EFFECT))(*srcs, *lands, *sems, after)
    return list(res[n:])


def _gather_small(small):
    def body(in_ref, out_ref, send_sems, recv_sems, local_sem):
        x, y, c = _place()
        me = 4 * x + 2 * y + c
        copies = [pltpu.make_async_copy(in_ref, out_ref.at[me], local_sem)]
        for mask in range(1, N_DEV):
            copies.append(pltpu.make_async_remote_copy(
                src_ref=in_ref, dst_ref=out_ref.at[me], send_sem=send_sems.at[mask - 1],
                recv_sem=recv_sems.at[mask - 1], device_id=_peer(x, y, c, mask), device_id_type=MESH))
        for cp in copies:
            cp.start()
        for cp in copies:
            cp.wait()

    any_spec = pl.BlockSpec(memory_space=pl.ANY)
    return pl.pallas_call(
        body, name="gather_small", in_specs=[any_spec], out_specs=any_spec,
        out_shape=jax.ShapeDtypeStruct((N_DEV,) + small.shape, small.dtype),
        scratch_shapes=[pltpu.SemaphoreType.DMA((7,)), pltpu.SemaphoreType.DMA((7,)), pltpu.SemaphoreType.DMA])(small)


def _send_copy(src_refs, land_refs, send_sems, recv_sems, a, mask, scatter, x, y, c):
    px, py, pc = _peer(x, y, c, mask)
    src = src_refs[a].at[4 * px + 2 * py + pc] if scatter else src_refs[a]
    return pltpu.make_async_remote_copy(
        src_ref=src, dst_ref=land_refs[a].at[4 * x + 2 * y + c], send_sem=send_sems[a].at[mask - 1],
        recv_sem=recv_sems[a].at[mask - 1], device_id=(px, py, pc), device_id_type=MESH)


def _local_copy(src_refs, land_refs, local_sems, a, scatter, x, y, c):
    me = 4 * x + 2 * y + c
    return pltpu.make_async_copy(src_refs[a].at[me] if scatter else src_refs[a], land_refs[a].at[me], local_sems[a].at[0])


def _send_start(name, srcs, scatter, after=None):
    n = len(srcs)
    n_in = 2 * n + (after is not None)
    shapes = [s.shape[1:] if scatter else s.shape for s in srcs]

    def body(*refs):
        src_refs, land_refs = refs[:n], refs[n:2 * n]
        outs = refs[n_in:]
        send_sems, recv_sems, local_sems = outs[:n], outs[n:2 * n], outs[2 * n:3 * n]
        token_ref = outs[5 * n]
        x, y, c = _place()
        for a in range(n):
            for mask in range(1, N_DEV):
                _send_copy(src_refs, land_refs, send_sems, recv_sems, a, mask, scatter, x, y, c).start()
        for a in range(n):
            _local_copy(src_refs, land_refs, local_sems, a, scatter, x, y, c).start()
        token_ref[...] = jnp.zeros_like(token_ref)

    lands = [pltpu.with_memory_space_constraint(lax.empty((N_DEV,) + sh, s.dtype), pltpu.HBM) for sh, s in zip(shapes, srcs)]
    srcs = [pltpu.with_memory_space_constraint(s, pltpu.HBM) for s in srcs]
    res = pl.pallas_call(
        body, name=name,
        out_shape=[pltpu.SemaphoreType.DMA((7,))] * (2 * n) + [pltpu.SemaphoreType.DMA((1,))] * n
        + [pltpu.HBM(s.shape, s.dtype) for s in srcs] + [pltpu.HBM(l.shape, l.dtype) for l in lands]
        + [jax.ShapeDtypeStruct((8, 128), F32)],
        in_specs=[_HBM] * (2 * n) + [pl.BlockSpec(memory_space=pl.ANY)] * (n_in - 2 * n),
        out_specs=[_SEM] * (3 * n) + [_HBM] * (2 * n) + [pl.BlockSpec(memory_space=pltpu.VMEM)],
        input_output_aliases={i: 3 * n + i for i in range(2 * n)},
        compiler_params=pltpu.CompilerParams(has_side_effects=_EFFECT))(*srcs, *lands, *([] if after is None else [after]))
    handles = [tuple(res[k * n + a] for k in range(5)) for a in range(n)]
    return handles, res[5 * n]


def _send_wait(name, handles, scatter, after):
    n = len(handles)

    def body(*refs):
        src_refs, land_refs = refs[:n], refs[n:2 * n]
        send_sems, recv_sems, local_sems = refs[2 * n:3 * n], refs[3 * n:4 * n], refs[4 * n:5 * n]
        x, y, c = _place()
        for a in range(n):
            for mask in range(1, N_DEV):
                cp = _send_copy(src_refs, land_refs, send_sems, recv_sems, a, mask, scatter, x, y, c)
                cp.wait_send()
                cp.wait_recv()
            _local_copy(src_refs, land_refs, local_sems, a, scatter, x, y, c).wait()

    srcs = [h[3] for h in handles]
    lands = [h[4] for h in handles]
    res = pl.pallas_call(
        body, name=name, out_shape=[pltpu.HBM(s.shape, s.dtype) for s in srcs] + [pltpu.HBM(l.shape, l.dtype) for l in lands],
        in_specs=[_HBM] * (2 * n) + [_SEM] * (3 * n) + [pl.BlockSpec(memory_space=pl.ANY)], out_specs=[_HBM] * (2 * n),
        input_output_aliases={i: i for i in range(2 * n)},
        compiler_params=pltpu.CompilerParams(has_side_effects=_EFFECT))(
            *srcs, *lands, *[h[0] for h in handles], *[h[1] for h in handles], *[h[2] for h in handles], after)
    return list(res[n:])


def _adamw(name, parts, w, m, v):
    K, n = w.shape
    tr = min(K, 128)

    def body(p_ref, w_ref, m_ref, v_ref, g_ref, d_ref, nm_ref, nv_ref):
        g = p_ref[0].astype(F32)
        for i in range(1, N_DEV):
            g = g + p_ref[i].astype(F32)
        nm = ADAM_B1 * m_ref[...] + (1.0 - ADAM_B1) * g
        nv = ADAM_B2 * v_ref[...] + (1.0 - ADAM_B2) * jnp.square(g)
        m_hat = nm / (1.0 - ADAM_B1 ** ADAM_STEP)
        v_hat = nv / (1.0 - ADAM_B2 ** ADAM_STEP)
        g_ref[...] = g
        d_ref[...] = -ADAM_LR * (m_hat / (jnp.sqrt(v_hat) + ADAM_EPS) + ADAM_WD * w_ref[...])
        nm_ref[...] = nm
        nv_ref[...] = nv

    blk = pl.BlockSpec((tr, n), lambda i: (i, 0))
    return _pc(body, name=name, grid=(K // tr,), in_specs=[pl.BlockSpec((N_DEV, tr, n), lambda i: (0, i, 0)), blk, blk, blk],
               out_specs=[blk] * 4, out_shape=[jax.ShapeDtypeStruct((K, n), F32)] * 4, sem=("parallel",))(parts, w, m, v)


def _pack_small(vals):
    flat = jnp.concatenate([v.reshape(-1) for v in vals])
    return jnp.pad(flat, (0, SMALL_ROWS * 128 - flat.shape[0])).reshape(SMALL_ROWS, 128)


def _unpack_small(packed, shapes):
    flat, out, off = packed.reshape(-1), [], 0
    for s in shapes:
        size = int(np.prod(s))
        out.append(flat[off:off + size].reshape(s))
        off += size
    return out


def kernel(x, norm_mix, w_qkv, w_gate, b_gate, rpb, w_proj_a, w_proj_b, w_out, norm_mlp, w_up, w_down, norm_final, loss_target, m_norm_mix, m_w_qkv, m_w_gate, m_b_gate, m_rpb, m_w_proj_a, m_w_proj_b, m_w_out, m_norm_mlp, m_w_up, m_w_down, m_norm_final, v_norm_mix, v_w_qkv, v_w_gate, v_b_gate, v_rpb, v_w_proj_a, v_w_proj_b, v_w_out, v_norm_mlp, v_w_up, v_w_down, v_norm_final):
    w = dict(norm_mix=norm_mix, w_qkv=w_qkv, w_gate=w_gate, b_gate=b_gate, rpb=rpb, w_proj_a=w_proj_a, w_proj_b=w_proj_b,
             w_out=w_out, norm_mlp=norm_mlp, w_up=w_up, w_down=w_down, norm_final=norm_final)
    m = dict(norm_mix=m_norm_mix, w_qkv=m_w_qkv, w_gate=m_w_gate, b_gate=m_b_gate, rpb=m_rpb, w_proj_a=m_w_proj_a,
             w_proj_b=m_w_proj_b, w_out=m_w_out, norm_mlp=m_norm_mlp, w_up=m_w_up, w_down=m_w_down, norm_final=m_norm_final)
    v = dict(norm_mix=v_norm_mix, w_qkv=v_w_qkv, w_gate=v_w_gate, b_gate=v_b_gate, rpb=v_rpb, w_proj_a=v_w_proj_a,
             w_proj_b=v_w_proj_b, w_out=v_w_out, norm_mlp=v_norm_mlp, w_up=v_w_up, w_down=v_w_down, norm_final=v_norm_final)
    order = ("norm_mix", "w_qkv", "w_gate", "b_gate", "rpb", "w_proj_a", "w_proj_b", "w_out", "norm_mlp", "w_up",
             "w_down", "norm_final")

    shard = {n: w[n][0].astype(BF16) for n in BIG}
    stages = dict(qkv=("w_qkv",), gate=("w_gate", "w_proj_a", "w_proj_b", "w_out"), mlp=("w_up", "w_down"))
    names = [n for s in stages.values() for n in s]
    handles, token = _gather_start("gather_start", [shard[n] for n in names])
    handles = dict(zip(names, handles))

    def late_weights(stage, after):
        if stage == "qkv":
            return tuple(_gather_wait("gather_wait_qkv", _gather_forward("gather_forward_qkv", [handles["w_qkv"]], after), after))
        if stage == "attention":
            handles["gate"] = _gather_forward("gather_forward_gate", [handles[n] for n in stages["gate"]], after)
            return ()
        if stage == "mixer":
            handles["mlp"] = _gather_forward("gather_forward_mlp", [handles[n] for n in stages["mlp"]], after)
            return tuple(_gather_wait("gather_wait_gate", handles["gate"], after))
        return tuple(_gather_wait("gather_wait_mlp", handles["mlp"], after))

    sent = {}

    def send(stage, grads):
        handles, tok = _send_start("grads_start_" + stage, list(grads.values()), True)
        sent[stage] = (tuple(grads), handles)
        return tok

    loss, grad_x, small = _local_step(x[0], loss_target[0], _behind(norm_mix, token), b_gate, rpb[0], norm_mlp, norm_final,
                                      late_weights, send)

    grad, delta, new_m, new_v = {}, {}, {}, {}
    after = grad_x
    for stage in ("mlp", "proj", "in"):
        names, handles = sent[stage]
        for n, parts in zip(names, _send_wait("grads_wait_" + stage, handles, True, after)):
            res = _adamw("adamw_" + n, parts, w[n][0], m[n][0], v[n][0])
            grad[n], delta[n], new_m[n], new_v[n] = (r[None] for r in res)
            after = res[1]
    res = _adamw("adamw_small", _gather_small(_pack_small([small[n] for n in SMALL])),
                 _pack_small([w[n] for n in SMALL]), _pack_small([m[n] for n in SMALL]), _pack_small([v[n] for n in SMALL]))
    shapes = [w[n].shape for n in SMALL]
    for tree, packed in zip((grad, delta, new_m, new_v), res):
        for n, val in zip(SMALL, _unpack_small(packed, shapes)):
            tree[n] = val

    total = lax.psum(loss[0, 0], ("x", "y", "c"))
    return (total, grad_x[None], *[grad[n] for n in order], *[delta[n] for n in order],
            *[new_m[n] for n in order], *[new_v[n] for n in order])
```

```python
import numpy as np
import jax
import jax.numpy as jnp
from jax import lax
from jax.experimental import pallas as pl
from jax.experimental.pallas import tpu as pltpu

F32 = jnp.float32
BF16 = jnp.bfloat16
MESH = pl.DeviceIdType.MESH

D_MODEL = 2048
HEAD_DIM = 128
N_HEADS = 16
N_HEADS_A = 12
N_HEADS_B = 4
DILATIONS = (1, 4, 16)
HALF_WINDOW = 64
GRID_W = 64
NA_ROWS = 8
NA_COLS = 16
D_FF = 4 * D_MODEL
N_DEV = 8
EPS = 1e-6
NEG = -1e30
SCALE = HEAD_DIM ** -0.5
ADAM_LR, ADAM_B1, ADAM_B2, ADAM_EPS, ADAM_WD, ADAM_STEP = 0.001, 0.9, 0.999, 1e-08, 0.01, 10

VMEM_LIMIT = 56 * 1024 * 1024
TM = 512
TL = 1024
TS = 256
BQ_A = 128


def _pc(body, *, name, grid, in_specs, out_specs, out_shape, scratch=(), sem=None):
    return pl.pallas_call(
        body, name=name, grid=grid, in_specs=in_specs, out_specs=out_specs, out_shape=out_shape,
        scratch_shapes=list(scratch),
        compiler_params=pltpu.CompilerParams(dimension_semantics=sem, vmem_limit_bytes=VMEM_LIMIT))


_DOT_DIMS = {"nn": (((1,), (0,)), ((), ())), "nt": (((1,), (1,)), ((), ())), "tn": (((0,), (0,)), ((), ()))}


def _dot(a, b, mode, precision=None):
    if mode == "nt_shards":
        n = b.shape[2]
        out = _dot(a[:, :n], b[0], "nt")
        for s in range(1, b.shape[0]):
            out = out + _dot(a[:, s * n:(s + 1) * n], b[s], "nt")
        return out
    return lax.dot_general(a, b, _DOT_DIMS[mode], preferred_element_type=F32, precision=precision)


def _mm(name, grid, pairs, extras, outs, epilogue, acc_shape, sem):
    n_pairs, n_extra, n_out = len(pairs), len(extras), len(outs)
    k_axis = len(grid) - 1
    nk = grid[k_axis]

    def body(*refs):
        pids = [pl.program_id(ax) for ax in range(len(grid))]
        ab = refs[:2 * n_pairs]
        ex = refs[2 * n_pairs:2 * n_pairs + n_extra]
        o = refs[2 * n_pairs + n_extra:2 * n_pairs + n_extra + n_out]
        part = None
        for p, pair in enumerate(pairs):
            a = ab[2 * p][...]
            if pair[5] is not None:
                a = pair[5](a)
            d = _dot(a, ab[2 * p + 1][...], pair[4])
            part = d if part is None else part + d
        if nk == 1:
            epilogue(part, ex, o, pids)
        else:
            acc = refs[-1]
            k = pids[k_axis]

            @pl.when(k == 0)
            def _():
                acc[...] = part

            @pl.when(k > 0)
            def _():
                acc[...] += part

            @pl.when(k == nk - 1)
            def _():
                epilogue(acc[...], ex, o, pids)

    operands, in_specs = [], []
    for a, a_spec, b, b_spec, _, _ in pairs:
        operands += [a, b]
        in_specs += [a_spec, b_spec]
    for e, e_spec in extras:
        operands.append(e)
        in_specs.append(e_spec)
    res = _pc(body, name=name, grid=grid, in_specs=in_specs, out_specs=[s for _, s in outs],
              out_shape=[o for o, _ in outs], scratch=[pltpu.VMEM(acc_shape, F32)] if nk > 1 else [], sem=sem)(*operands)
    return res


def _rmsnorm_fwd(x, g, name):
    T, D = x.shape

    def body(x_ref, g_ref, h_ref, r_ref):
        xv = x_ref[...]
        r = lax.rsqrt(jnp.mean(xv * xv, axis=-1, keepdims=True) + EPS)
        h_ref[...] = (xv * r * g_ref[...]).astype(BF16)
        r_ref[...] = r

    return _pc(body, name=name, grid=(T // TM,),
               in_specs=[pl.BlockSpec((TM, D), lambda i: (i, 0)), pl.BlockSpec((1, D), lambda i: (0, 0))],
               out_specs=[pl.BlockSpec((TM, D), lambda i: (i, 0)), pl.BlockSpec((TM, 1), lambda i: (i, 0))],
               out_shape=[jax.ShapeDtypeStruct((T, D), BF16), jax.ShapeDtypeStruct((T, 1), F32)],
               sem=("parallel",))(x, g)


def _rmsnorm_bwd_block(dh, x, r, g, dres):
    xh = x * r
    dxh = dh * g
    dx = dres + r * (dxh - xh * jnp.mean(dxh * xh, axis=-1, keepdims=True))
    return dx, jnp.sum(dh * xh, axis=0, keepdims=True)


def _rmsnorm_bwd(name, dh, x, r, g, dres, want_bf16):
    T, D = x.shape

    def body(dh_ref, x_ref, r_ref, g_ref, dres_ref, *outs):
        dx, dgp = _rmsnorm_bwd_block(dh_ref[...], x_ref[...], r_ref[...], g_ref[...], dres_ref[...])
        outs[0][...] = dx
        if want_bf16:
            outs[1][...] = dx.astype(BF16)
        first = pl.program_id(0) == 0

        @pl.when(first)
        def _():
            outs[-1][...] = dgp

        @pl.when(jnp.logical_not(first))
        def _():
            outs[-1][...] += dgp

    row = pl.BlockSpec((TS, D), lambda i: (i, 0))
    vec = pl.BlockSpec((1, D), lambda i: (0, 0))
    n_row = 2 if want_bf16 else 1
    return _pc(body, name=name, grid=(T // TS,), in_specs=[row, row, pl.BlockSpec((TS, 1), lambda i: (i, 0)), vec, row],
               out_specs=[row] * n_row + [vec],
               out_shape=[jax.ShapeDtypeStruct((T, D), F32)] + [jax.ShapeDtypeStruct((T, D), BF16)] * (n_row - 1)
               + [jax.ShapeDtypeStruct((1, D), F32)], sem=("arbitrary",))(dh, x, r, g, dres)


def _final_norm_loss(x3, g, tgt):
    T, D = x3.shape

    def body(x_ref, g_ref, t_ref, loss_ref, dx_ref, dxb_ref, dg_ref):
        i = pl.program_id(0)
        xv = x_ref[...]
        gv = g_ref[...]
        r = lax.rsqrt(jnp.mean(xv * xv, axis=-1, keepdims=True) + EPS)
        xh = xv * r
        e = xh * gv - t_ref[...]
        part = 0.5 * jnp.sum(jnp.mean(e * e, axis=-1, keepdims=True))
        dy = e * (1.0 / D)
        dxh = dy * gv
        dx = r * (dxh - xh * jnp.mean(dxh * xh, axis=-1, keepdims=True))
        dx_ref[...] = dx
        dxb_ref[...] = dx.astype(BF16)
        dgp = jnp.sum(dy * xh, axis=0, keepdims=True)

        @pl.when(i == 0)
        def _():
            loss_ref[...] = jnp.zeros(loss_ref.shape, F32) + part
            dg_ref[...] = dgp

        @pl.when(i > 0)
        def _():
            loss_ref[...] += part
            dg_ref[...] += dgp

    row = pl.BlockSpec((TM, D), lambda i: (i, 0))
    vec = pl.BlockSpec((1, D), lambda i: (0, 0))
    return _pc(body, name="final_norm_loss", grid=(T // TM,), in_specs=[row, vec, row],
               out_specs=[pl.BlockSpec((8, 128), lambda i: (0, 0)), row, row, vec],
               out_shape=[jax.ShapeDtypeStruct((8, 128), F32), jax.ShapeDtypeStruct((T, D), F32),
                          jax.ShapeDtypeStruct((T, D), BF16), jax.ShapeDtypeStruct((1, D), F32)],
               sem=("arbitrary",))(x3, g, tgt)


def _alibi_slopes():
    return np.asarray(2.0 ** (-8.0 * np.arange(1, N_HEADS_A + 1) / N_HEADS_A), dtype=np.float32)


def _a_windows(T):
    half = [HALF_WINDOW * d for d in DILATIONS]
    return half, [min(BQ_A + 2 * w, T) for w in half]


def _a_bias(slope, g, offset, wk, half):
    d = DILATIONS[g]
    dlt = offset + lax.broadcasted_iota(jnp.int32, (BQ_A, wk), 1) - lax.broadcasted_iota(jnp.int32, (BQ_A, wk), 0)
    ad = jnp.maximum(dlt, -dlt)
    ok = (ad <= half) & ((ad & (d - 1)) == 0)
    return jnp.where(ok, -slope * ad.astype(F32), NEG)


def _a_scores(q, kw, slope, g, offset, wk, half):
    return _dot(q, kw, "nt") * SCALE + _a_bias(slope, g, offset, wk, half)


N_JOINT = 2


def _a_qkv_specs(T, inner):
    specs = []
    for g in range(N_JOINT):
        specs.append(pl.BlockSpec((BQ_A, HEAD_DIM), lambda j, i, g=g: (i, 4 * g + j)))
        specs.append(pl.BlockSpec((T, HEAD_DIM), lambda j, i, g=g: (0, N_HEADS + 4 * g + j)))
        specs.append(pl.BlockSpec((T, HEAD_DIM), lambda j, i, g=g: (0, 2 * N_HEADS + 4 * g + j)))
    return specs


def _attn_a_fwd(qkv, slopes, o_last, lse_last):
    T = qkv.shape[0]
    nq = T // BQ_A
    half, wks = _a_windows(T)

    def body(sl_ref, *refs):
        qkv_refs, (o_ref, l_ref, y_ref, lse_ref) = refs[:3 * N_JOINT], refs[3 * N_JOINT:]
        j = pl.program_id(0)
        t0 = pl.program_id(1) * BQ_A
        ss, vws = [], []
        m = l_ref[...]
        for g in range(N_JOINT):
            q_ref, k_ref, v_ref = qkv_refs[3 * g:3 * g + 3]
            ws = pl.multiple_of(jnp.clip(t0 - half[g], 0, T - wks[g]), 64)
            ss.append(_a_scores(q_ref[...], k_ref[pl.ds(ws, wks[g]), :], sl_ref[4 * g + j], g, ws - t0, wks[g], half[g]))
            vws.append(v_ref[pl.ds(ws, wks[g]), :])
            m = jnp.maximum(m, jnp.max(ss[g], axis=-1, keepdims=True))
        den = jnp.exp(l_ref[...] - m)
        acc = den * o_ref[...].astype(F32)
        for g in range(N_JOINT):
            p = jnp.exp(ss[g] - m)
            den = den + jnp.sum(p, axis=-1, keepdims=True)
            acc = acc + _dot(p.astype(BF16), vws[g], "nn")
        y_ref[...] = (acc / den).astype(BF16)
        lse_ref[...] = m + jnp.log(den)

    blk = pl.BlockSpec((BQ_A, HEAD_DIM), lambda j, i: (i, j))
    col = pl.BlockSpec((None, BQ_A, 1), lambda j, i: (j, i, 0))
    return _pc(body, name="attn_a_fwd", grid=(4, nq),
               in_specs=[pl.BlockSpec(memory_space=pltpu.SMEM)] + _a_qkv_specs(T, nq) + [blk, col], out_specs=[blk, col],
               out_shape=[jax.ShapeDtypeStruct((T, 4 * HEAD_DIM), BF16), jax.ShapeDtypeStruct((4, T, 1), F32)],
               sem=("parallel", "parallel"))(slopes, *([qkv] * (3 * N_JOINT)), o_last, lse_last)


def _attn_a_bwd(qkv, slopes, dy, y, lse):
    T = qkv.shape[0]
    nq = T // BQ_A
    half, wks = _a_windows(T)

    def body(sl_ref, *refs):
        qkv_refs = refs[:3 * N_JOINT]
        dy_ref, y_ref, lse_ref, dq_ref, dk_ref, dv_ref, delta_ref, dk_acc, dv_acc = refs[3 * N_JOINT:]
        j = pl.program_id(0)
        i = pl.program_id(1)
        t0 = i * BQ_A

        @pl.when(i == 0)
        def _():
            dk_acc[...] = jnp.zeros_like(dk_acc)
            dv_acc[...] = jnp.zeros_like(dv_acc)

        dyv = dy_ref[...]
        delta = jnp.sum(dyv.astype(F32) * y_ref[...].astype(F32), axis=-1, keepdims=True)
        delta_ref[...] = delta
        lse_v = lse_ref[...]
        for g in range(N_JOINT):
            q_ref, k_ref, v_ref = qkv_refs[3 * g:3 * g + 3]
            ws = pl.multiple_of(jnp.clip(t0 - half[g], 0, T - wks[g]), 64)
            win = pl.ds(ws, wks[g])
            qv = q_ref[...]
            kw = k_ref[win, :]
            s = _a_scores(qv, kw, sl_ref[4 * g + j], g, ws - t0, wks[g], half[g])
            p = jnp.exp(s - lse_v)
            dp = _dot(dyv, v_ref[win, :], "nt")
            ds = (p * (dp - delta)).astype(BF16)
            dq_ref[g] = (_dot(ds, kw, "nn") * SCALE).astype(BF16)
            dk_acc[g, win, :] += _dot(ds, qv, "tn") * SCALE
            dv_acc[g, win, :] += _dot(p.astype(BF16), dyv, "tn")

        @pl.when(i == nq - 1)
        def _():
            dk_ref[...] = dk_acc[...].astype(BF16)
            dv_ref[...] = dv_acc[...].astype(BF16)

    blk = pl.BlockSpec((BQ_A, HEAD_DIM), lambda j, i: (i, j))
    col = pl.BlockSpec((None, BQ_A, 1), lambda j, i: (j, i, 0))
    whole = pl.BlockSpec((N_JOINT, T, HEAD_DIM), lambda j, i: (0, 0, j))
    full = jax.ShapeDtypeStruct((N_JOINT, T, 4 * HEAD_DIM), BF16)
    return _pc(body, name="attn_a_bwd", grid=(4, nq),
               in_specs=[pl.BlockSpec(memory_space=pltpu.SMEM)] + _a_qkv_specs(T, nq) + [blk, blk, col],
               out_specs=[pl.BlockSpec((N_JOINT, BQ_A, HEAD_DIM), lambda j, i: (0, i, j)), whole, whole, col],
               out_shape=[full, full, full, jax.ShapeDtypeStruct((4, T, 1), F32)],
               scratch=[pltpu.VMEM((N_JOINT, T, HEAD_DIM), F32), pltpu.VMEM((N_JOINT, T, HEAD_DIM), F32)],
               sem=("parallel", "arbitrary"))(slopes, *([qkv] * (3 * N_JOINT)), dy, y, lse)


SUB_A = 4


def _to_residue_major(a, d, axis=0):
    if d == 1:
        return a
    sh = a.shape
    a = a.reshape(sh[:axis] + (sh[axis] // d, d) + sh[axis + 1:])
    return jnp.swapaxes(a, axis, axis + 1).reshape(sh)


def _to_token_order(a, d, axis=0):
    if d == 1:
        return a
    sh = a.shape
    a = a.reshape(sh[:axis] + (d, sh[axis] // d) + sh[axis + 1:])
    return jnp.swapaxes(a, axis, axis + 1).reshape(sh)


def _band_bias(slope, offset, wk):
    dlt = offset + lax.broadcasted_iota(jnp.int32, (BQ_A, wk), 1) - lax.broadcasted_iota(jnp.int32, (BQ_A, wk), 0)
    ad = jnp.maximum(dlt, -dlt)
    return jnp.where(ad <= HALF_WINDOW, -slope * ad.astype(F32), NEG)


def _band_window(step, u, seg, wk):
    t0 = (step * SUB_A + u) * BQ_A
    lo = (t0 // seg) * seg
    ws = pl.multiple_of(jnp.clip(t0 - HALF_WINDOW, lo, lo + seg - wk), 64)
    return slice(u * BQ_A, (u + 1) * BQ_A), pl.ds(ws, wk), ws - t0


def _band_specs(T, cols):
    bs = SUB_A * BQ_A
    assert T % bs == 0 and T // max(DILATIONS) >= BQ_A, "a query sub-block must lie inside one segment"
    return [pl.BlockSpec(memory_space=pltpu.SMEM),
            pl.BlockSpec((bs, HEAD_DIM), lambda j, i: (i, cols[0] + j)),
            pl.BlockSpec((T, HEAD_DIM), lambda j, i: (0, cols[1] + j)),
            pl.BlockSpec((T, HEAD_DIM), lambda j, i: (0, cols[2] + j))]


def _banded_fwd(name, src, cols, seg, slopes):
    T = src.shape[0]
    bs = SUB_A * BQ_A
    wk = min(BQ_A + 2 * HALF_WINDOW, seg)

    def body(sl_ref, q_ref, k_ref, v_ref, o_ref, lse_ref):
        slope = sl_ref[pl.program_id(0)]
        for u in range(SUB_A):
            rows, win, offset = _band_window(pl.program_id(1), u, seg, wk)
            s = _dot(q_ref[rows, :], k_ref[win, :], "nt") * SCALE + _band_bias(slope, offset, wk)
            m = jnp.max(s, axis=-1, keepdims=True)
            p = jnp.exp(s - m)
            den = jnp.sum(p, axis=-1, keepdims=True)
            o_ref[rows, :] = (_dot(p.astype(BF16), v_ref[win, :], "nn") / den).astype(BF16)
            lse_ref[rows, :] = m + jnp.log(den)

    return _pc(body, name=name, grid=(4, T // bs), in_specs=_band_specs(T, cols),
               out_specs=[pl.BlockSpec((bs, HEAD_DIM), lambda j, i: (i, j)), pl.BlockSpec((None, bs, 1), lambda j, i: (j, i, 0))],
               out_shape=[jax.ShapeDtypeStruct((T, 4 * HEAD_DIM), BF16), jax.ShapeDtypeStruct((4, T, 1), F32)],
               sem=("parallel", "parallel"))(slopes, src, src, src)


def _banded_bwd(name, src, cols, seg, slopes, dy, delta_lse):
    T = src.shape[0]
    bs = SUB_A * BQ_A
    steps = T // bs
    wk = min(BQ_A + 2 * HALF_WINDOW, seg)

    def body(sl_ref, q_ref, k_ref, v_ref, dy_ref, delta_ref, lse_ref, dq_ref, dk_ref, dv_ref, dk_acc, dv_acc):
        slope = sl_ref[pl.program_id(0)]
        step = pl.program_id(1)

        @pl.when(step == 0)
        def _():
            dk_acc[...] = jnp.zeros_like(dk_acc)
            dv_acc[...] = jnp.zeros_like(dv_acc)

        for u in range(SUB_A):
            rows, win, offset = _band_window(step, u, seg, wk)
            qv, kw, dyv = q_ref[rows, :], k_ref[win, :], dy_ref[rows, :]
            s = _dot(qv, kw, "nt") * SCALE + _band_bias(slope, offset, wk)
            p = jnp.exp(s - lse_ref[rows, :])
            ds = (p * (_dot(dyv, v_ref[win, :], "nt") - delta_ref[rows, :])).astype(BF16)
            dq_ref[rows, :] = (_dot(ds, kw, "nn") * SCALE).astype(BF16)
            dk_acc[win, :] += _dot(ds, qv, "tn") * SCALE
            dv_acc[win, :] += _dot(p.astype(BF16), dyv, "tn")

        @pl.when(step == steps - 1)
        def _():
            dk_ref[...] = dk_acc[...].astype(BF16)
            dv_ref[...] = dv_acc[...].astype(BF16)

    blk = pl.BlockSpec((bs, HEAD_DIM), lambda j, i: (i, j))
    whole = pl.BlockSpec((T, HEAD_DIM), lambda j, i: (0, j))
    full = jax.ShapeDtypeStruct((T, 4 * HEAD_DIM), BF16)
    return _pc(body, name=name, grid=(4, steps),
               in_specs=_band_specs(T, cols) + [blk, pl.BlockSpec((None, bs, 1), lambda j, i: (j, i, 0)),
                                               pl.BlockSpec((None, bs, 1), lambda j, i: (4 + j, i, 0))],
               out_specs=[blk, whole, whole], out_shape=[full, full, full],
               scratch=[pltpu.VMEM((T, HEAD_DIM), F32), pltpu.VMEM((T, HEAD_DIM), F32)],
               sem=("parallel", "arbitrary"))(slopes, src, src, src, dy, delta_lse, delta_lse)


def _last_group_source(qkv):
    third = qkv.shape[1] // 3
    own = jnp.concatenate([qkv[:, t * third + 512 * N_JOINT:t * third + 512 * (N_JOINT + 1)] for t in range(3)], axis=1)
    return _to_residue_major(own, DILATIONS[N_JOINT])


def _dilated_fwd(qkv, slopes):
    d = DILATIONS[N_JOINT]
    o, lse = _banded_fwd("attn_a_last_fwd", _last_group_source(qkv), (0, 4, 8), qkv.shape[0] // d,
                         slopes[4 * N_JOINT:] * float(d))
    return _attn_a_fwd(qkv, slopes, _to_token_order(o, d), _to_token_order(lse, d, axis=1))


def _dilated_bwd(qkv, slopes, dy, y, lse):
    d = DILATIONS[N_JOINT]
    dq, dk, dv, delta = _attn_a_bwd(qkv, slopes, dy, y, lse)
    delta_lse = _to_residue_major(jnp.concatenate([delta, lse], axis=0), d, axis=1)
    last = _banded_bwd("attn_a_last_bwd", _last_group_source(qkv), (0, 4, 8), qkv.shape[0] // d,
                       slopes[4 * N_JOINT:] * float(d), _to_residue_major(dy, d), delta_lse)
    return [jnp.concatenate([joint[g] for g in range(N_JOINT)] + [_to_token_order(own, d)], axis=1)
            for joint, own in zip((dq, dk, dv), last)]


N_DR = 2 * NA_ROWS - 1
N_DC = 2 * NA_COLS - 1
WK_B = NA_ROWS * GRID_W


def _toeplitz_consts():
    col = np.arange(GRID_W)
    start = np.clip(col - NA_COLS // 2, 0, GRID_W - NA_COLS)
    ok = (col[None, :] >= start[:, None]) & (col[None, :] < start[:, None] + NA_COLS)
    dc = np.clip(col[None, :] - col[:, None], -(NA_COLS - 1), NA_COLS - 1) + NA_COLS - 1
    sel = np.zeros((128, GRID_W * GRID_W), np.float32)
    sel[dc.reshape(-1), np.arange(GRID_W * GRID_W)] = ok.reshape(-1).astype(np.float32)
    mask = np.where(ok, 0.0, NEG).astype(np.float32).reshape(1, -1)
    return sel, mask


def _rpb_expand(rpb_rows, sel, mask):
    def body(r_ref, s_ref, m_ref, o_ref):
        o_ref[...] = _dot(r_ref[...], s_ref[...], "nn", precision=lax.Precision.HIGHEST) + m_ref[...]

    n = rpb_rows.shape[0]
    return _pc(body, name="rpb_expand", grid=(1,),
               in_specs=[pl.BlockSpec((n, 128), lambda i: (0, 0)), pl.BlockSpec(sel.shape, lambda i: (0, 0)),
                         pl.BlockSpec(mask.shape, lambda i: (0, 0))],
               out_specs=pl.BlockSpec((n, sel.shape[1]), lambda i: (0, 0)),
               out_shape=jax.ShapeDtypeStruct((n, sel.shape[1]), F32), sem=("arbitrary",))(rpb_rows, sel, mask)


def _rpb_reduce(dtiles, sel):
    def body(t_ref, s_ref, o_ref):
        o_ref[...] = _dot(t_ref[...], s_ref[...], "nt", precision=lax.Precision.HIGHEST)

    n = dtiles.shape[0]
    return _pc(body, name="rpb_reduce", grid=(1,),
               in_specs=[pl.BlockSpec(dtiles.shape, lambda i: (0, 0)), pl.BlockSpec(sel.shape, lambda i: (0, 0))],
               out_specs=pl.BlockSpec((n, 128), lambda i: (0, 0)),
               out_shape=jax.ShapeDtypeStruct((n, 128), F32), sem=("arbitrary",))(dtiles, sel)


ROWS_B = 8
BQ_B = ROWS_B * GRID_W


def _b_specs(T):
    return [pl.BlockSpec((BQ_B, HEAD_DIM), lambda h, r: (r, N_HEADS_A + h)),
            pl.BlockSpec((T, HEAD_DIM), lambda h, r: (0, N_HEADS + N_HEADS_A + h)),
            pl.BlockSpec((T, HEAD_DIM), lambda h, r: (0, 2 * N_HEADS + N_HEADS_A + h)),
            pl.BlockSpec((None, N_DR, GRID_W, GRID_W), lambda h, r: (h, 0, 0, 0))]


def _b_row(step, u, rows, t_ref, bias_ref):
    r = step * ROWS_B + u
    rs = jnp.clip(r - NA_ROWS // 2, 0, rows - NA_ROWS)
    first_tile = rs - r + NA_ROWS - 1
    for a in range(NA_ROWS):
        bias_ref[u, :, a * GRID_W:(a + 1) * GRID_W] = t_ref[first_tile + a]
    return slice(u * GRID_W, (u + 1) * GRID_W), pl.ds(pl.multiple_of(rs * GRID_W, GRID_W), WK_B), first_tile


def _attn_b_fwd(qkv, tiles):
    T = qkv.shape[0]
    rows = T // GRID_W

    def body(q_ref, k_ref, v_ref, t_ref, y_ref, lse_ref, bias_ref):
        for u in range(ROWS_B):
            blk, win, _ = _b_row(pl.program_id(1), u, rows, t_ref, bias_ref)
            s = _dot(q_ref[blk, :], k_ref[win, :], "nt") * SCALE + bias_ref[u]
            m = jnp.max(s, axis=-1, keepdims=True)
            p = jnp.exp(s - m)
            den = jnp.sum(p, axis=-1, keepdims=True)
            y_ref[blk, :] = (_dot(p.astype(BF16), v_ref[win, :], "nn") / den).astype(BF16)
            lse_ref[blk, :] = m + jnp.log(den)

    return _pc(body, name="attn_b_fwd", grid=(N_HEADS_B, rows // ROWS_B), in_specs=_b_specs(T),
               out_specs=[pl.BlockSpec((BQ_B, HEAD_DIM), lambda h, r: (r, h)),
                          pl.BlockSpec((None, BQ_B, 1), lambda h, r: (h, r, 0))],
               out_shape=[jax.ShapeDtypeStruct((T, N_HEADS_B * HEAD_DIM), BF16),
                          jax.ShapeDtypeStruct((N_HEADS_B, T, 1), F32)],
               scratch=[pltpu.VMEM((ROWS_B, GRID_W, WK_B), F32)], sem=("parallel", "parallel"))(qkv, qkv, qkv, tiles)


def _attn_b_bwd(qkv, tiles, dy, y, lse):
    T = qkv.shape[0]
    rows = T // GRID_W
    steps = rows // ROWS_B

    def body(q_ref, k_ref, v_ref, t_ref, dy_ref, y_ref, lse_ref, dq_ref, dk_ref, dv_ref, dt_ref,
             bias_ref, ds_ref, dk_acc, dv_acc):
        step = pl.program_id(1)

        @pl.when(step == 0)
        def _():
            dk_acc[...] = jnp.zeros_like(dk_acc)
            dv_acc[...] = jnp.zeros_like(dv_acc)
            dt_ref[...] = jnp.zeros_like(dt_ref)

        for u in range(ROWS_B):
            blk, win, first_tile = _b_row(step, u, rows, t_ref, bias_ref)
            qv = q_ref[blk, :]
            kw = k_ref[win, :]
            dyv = dy_ref[blk, :]
            s = _dot(qv, kw, "nt") * SCALE + bias_ref[u]
            p = jnp.exp(s - lse_ref[blk, :])
            delta = jnp.sum(dyv.astype(F32) * y_ref[blk, :].astype(F32), axis=-1, keepdims=True)
            ds = p * (_dot(dyv, v_ref[win, :], "nt") - delta)
            ds_ref[u] = ds
            for a in range(NA_ROWS):
                dt_ref[first_tile + a] += ds_ref[u, :, a * GRID_W:(a + 1) * GRID_W]
            dsb = ds.astype(BF16)
            dq_ref[blk, :] = (_dot(dsb, kw, "nn") * SCALE).astype(BF16)
            dk_acc[win, :] += _dot(dsb, qv, "tn") * SCALE
            dv_acc[win, :] += _dot(p.astype(BF16), dyv, "tn")

        @pl.when(step == steps - 1)
        def _():
            dk_ref[...] = dk_acc[...].astype(BF16)
            dv_ref[...] = dv_acc[...].astype(BF16)

    blk_spec = pl.BlockSpec((BQ_B, HEAD_DIM), lambda h, r: (r, h))
    whole = pl.BlockSpec((T, HEAD_DIM), lambda h, r: (0, h))
    full = jax.ShapeDtypeStruct((T, N_HEADS_B * HEAD_DIM), BF16)
    return _pc(body, name="attn_b_bwd", grid=(N_HEADS_B, steps),
               in_specs=_b_specs(T) + [blk_spec, blk_spec, pl.BlockSpec((None, BQ_B, 1), lambda h, r: (h, r, 0))],
               out_specs=[blk_spec, whole, whole, pl.BlockSpec((None, N_DR, GRID_W, GRID_W), lambda h, r: (h, 0, 0, 0))],
               out_shape=[full, full, full, jax.ShapeDtypeStruct((N_HEADS_B, N_DR, GRID_W, GRID_W), F32)],
               scratch=[pltpu.VMEM((ROWS_B, GRID_W, WK_B), F32), pltpu.VMEM((ROWS_B, GRID_W, WK_B), F32),
                        pltpu.VMEM((T, HEAD_DIM), F32), pltpu.VMEM((T, HEAD_DIM), F32)],
               sem=("parallel", "arbitrary"))(qkv, qkv, qkv, tiles, dy, y, lse)


def _sds(shape, dtype):
    return jax.ShapeDtypeStruct(shape, dtype)


def _behind(value, token):
    return value if token is None else value + token[0, 0]


def _local_step(x, tgt, norm_mix, b_gate, rpb, norm_mlp, norm_final, late_weights, send):
    T, D = x.shape
    nt = T // TM
    PP, PA = ("parallel", "parallel", "arbitrary"), ("arbitrary", "arbitrary", "arbitrary")
    slopes = jnp.asarray(_alibi_slopes())
    sel_np, mask_np = _toeplitz_consts()
    sel, mask = jnp.asarray(sel_np), jnp.asarray(mask_np)

    def rows(w, tm=TM):
        return pl.BlockSpec((tm, w), lambda i, j, k: (i, j))

    def full_rows(tm, w=D):
        return pl.BlockSpec((tm, w), lambda i, j, k: (i, 0))

    def tall(w):
        return pl.BlockSpec((T, w), lambda j, i, k: (0, i))

    def tall_j(w):
        return pl.BlockSpec((T, w), lambda j, i, k: (0, j))

    rvec = pl.BlockSpec((TS, 1), lambda i, j, k: (i, 0))
    gvec = pl.BlockSpec((1, D), lambda i, j, k: (0, 0))

    h1, r1 = _rmsnorm_fwd(x, norm_mix, "rmsnorm_mix")
    (wq,) = late_weights("qkv", h1)

    def store_bf16(acc, ex, o, pids):
        o[0][...] = acc.astype(BF16)

    (qkv,) = _mm("qkv_proj", (T // TL, N_DEV, 1),
                 [(h1, full_rows(TL), wq, pl.BlockSpec((None, D, 768), lambda i, j, k: (j, 0, 0)), "nn", None)],
                 [], [(_sds((T, 3 * D), BF16), rows(768, TL))], store_bf16, None, PP)

    def gate_epi(acc, ex, o, pids):
        o[0][...] = jax.nn.sigmoid(acc + ex[0][...])

    late_weights("attention", qkv)
    rpb_rows = jnp.pad(rpb.reshape(N_HEADS_B * N_DR, N_DC), ((0, 0), (0, 128 - N_DC)))
    tiles = _rpb_expand(rpb_rows, sel, mask).reshape(N_HEADS_B, N_DR, GRID_W, GRID_W)
    ya, lse_a = _dilated_fwd(qkv, slopes)
    yb, lse_b = _attn_b_fwd(qkv, tiles)

    wg, wpa, wpb, wo, token = late_weights("mixer", ya)
    (gates,) = _mm("gate_proj", (T // TL, N_DEV, 1),
                   [(h1, full_rows(TL), wg, pl.BlockSpec((None, D, 512), lambda i, j, k: (j, 0, 0)), "nn", None)],
                   [(_behind(b_gate, token), pl.BlockSpec((1, 512), lambda i, j, k: (0, j)))],
                   [(_sds((T, 2 * D), F32), rows(512, TL))], gate_epi, None, PP)

    def merge_body(ya_ref, wa_ref, yb_ref, wb_ref, ga_ref, gb_ref, pa_ref, pb_ref, mg_ref):
        ya_v, yb_v = ya_ref[...], yb_ref[...]
        for s in range(N_DEV):
            cols = slice(s * 256, (s + 1) * 256)
            pa = _dot(ya_v, wa_ref[s], "nn")
            pb = _dot(yb_v, wb_ref[s], "nn")
            pa_ref[:, cols] = pa.astype(BF16)
            pb_ref[:, cols] = pb.astype(BF16)
            mg_ref[:, cols] = (ga_ref[:, cols] * pa + gb_ref[:, cols] * pb).astype(BF16)

    wo2 = wo.reshape(D, D)
    y_spec = pl.BlockSpec((TS, 512), lambda i: (i, 0))
    wp_spec = pl.BlockSpec((N_DEV, 512, 256), lambda i: (0, 0, 0))
    c_spec = pl.BlockSpec((TS, D), lambda i: (i, 0))
    pa, pb, merged = _pc(merge_body, name="merge_proj", grid=(T // TS,),
                         in_specs=[y_spec, wp_spec, y_spec, wp_spec, c_spec, pl.BlockSpec((TS, D), lambda i: (i, 1))],
                         out_specs=[c_spec, c_spec, c_spec],
                         out_shape=[_sds((T, D), BF16)] * 3, sem=("parallel",))(ya, wpa, yb, wpb, gates, gates)

    def add_res(acc, ex, o, pids):
        o[0][...] = ex[0][...] + acc

    (x2,) = _mm("out_proj", (T // TL, 4, 1),
                [(merged, full_rows(TL), wo2, pl.BlockSpec((D, 512), lambda i, j, k: (0, j)), "nn", None)],
                [(x, rows(512, TL))], [(_sds((T, D), F32), rows(512, TL))], add_res, None, PP)

    (token,) = late_weights("residual", x2)
    h2, r2 = _rmsnorm_fwd(x2, _behind(norm_mlp, token), "rmsnorm_mlp")
    (wu,) = late_weights("up", h2)

    def up_epi(acc, ex, o, pids):
        o[0][...] = acc.astype(BF16)
        r = jnp.maximum(acc, 0.0)
        o[1][...] = (r * r).astype(BF16)

    u, act = _mm("mlp_up", (T // TL, N_DEV, 1),
                 [(h2, full_rows(TL), wu, pl.BlockSpec((None, D, 1024), lambda i, j, k: (j, 0, 0)), "nn", None)],
                 [], [(_sds((T, D_FF), BF16), rows(1024, TL))] * 2, up_epi, None, PP)
    (wd,) = late_weights("down", act)
    wd2 = wd.reshape(D_FF, D)
    (x3,) = _mm("mlp_down", (nt, D // 512, 1),
                [(act, full_rows(TM, D_FF), wd2, pl.BlockSpec((D_FF, 512), lambda i, j, k: (0, j)), "nn", None)],
                [(x2, rows(512))], [(_sds((T, D), F32), rows(512))], add_res, None, PP)

    loss, dx3, dx3b, dg3 = _final_norm_loss(x3, norm_final.reshape(1, D), tgt)

    def du_epi(acc, ex, o, pids):
        o[0][...] = (acc * (2.0 * jnp.maximum(ex[0][...].astype(F32), 0.0))).astype(BF16)

    (du,) = _mm("mlp_down_dx", (T // TL, N_DEV, 1),
                [(dx3b, full_rows(TL), wd2, pl.BlockSpec((1024, D), lambda i, j, k: (j, 0)), "nt", None)],
                [(u, rows(1024, TL))], [(_sds((T, D_FF), BF16), rows(1024, TL))], du_epi, None, PP)
    (dwd,) = _mm("mlp_down_dw", (D // 1024, D_FF // 512, 1), [(act, tall(512), dx3b, tall_j(1024), "tn", None)],
                 [], [(_sds((D_FF, D), BF16), pl.BlockSpec((512, 1024), lambda j, i, k: (i, j)))], store_bf16, None, PP)
    dwd = dwd.reshape(N_DEV, D_FF // N_DEV, D)
    (dwu,) = _mm("mlp_up_dw", (N_DEV, D // 512, 1), [(h2, tall(512), du, tall_j(1024), "tn", None)],
                 [], [(_sds((N_DEV, D, 1024), BF16), pl.BlockSpec((None, 512, 1024), lambda j, i, k: (j, i, 0)))],
                 store_bf16, None, PP)

    def store_f32(acc, ex, o, pids):
        o[0][...] = acc

    out_ji = pl.BlockSpec((TM, 512), lambda j, i, k: (i, j))
    a_ji = lambda w: pl.BlockSpec((TM, w), lambda j, i, k: (i, 0))
    w_ji = lambda n: pl.BlockSpec((N_DEV, 512, n), lambda j, i, k: (0, j, 0))
    behind = lambda token: [] if token is None else [(token, pl.BlockSpec(memory_space=pl.ANY))]
    (dh2,) = _mm("mlp_up_dx", (D // 512, nt, 1), [(du, a_ji(D_FF), wu, w_ji(1024), "nt_shards", None)],
                 behind(send("mlp", dict(w_down=dwd, w_up=dwu))), [(_sds((T, D), F32), out_ji)], store_f32, None, PP)
    dx2, dx2b, dg2 = _rmsnorm_bwd("rmsnorm_mlp_bwd", dh2, x2, r2, norm_mlp, dx3, True)

    (dwo,) = _mm("out_proj_dw", (D // 1024, D // 512, 1), [(merged, tall(512), dx2b, tall_j(1024), "tn", None)],
                 [], [(_sds((D, D), BF16), pl.BlockSpec((512, 1024), lambda j, i, k: (i, j)))], store_bf16, None, PP)

    def dmerge_epi(acc, ex, o, pids):
        ga, gb, pa_v, pb_v = (e[...] for e in ex)
        o[0][...] = (acc * ga).astype(BF16)
        o[1][...] = (acc * gb).astype(BF16)
        dga = acc * pa_v * ga * (1.0 - ga)
        dgb = acc * pb_v * gb * (1.0 - gb)
        o[2][...] = dga.astype(BF16)
        o[3][...] = dgb.astype(BF16)
        sa = jnp.sum(dga, axis=0, keepdims=True)
        sb = jnp.sum(dgb, axis=0, keepdims=True)
        first = pids[1] == 0

        @pl.when(first)
        def _():
            o[4][...] = sa
            o[5][...] = sb

        @pl.when(jnp.logical_not(first))
        def _():
            o[4][...] += sa
            o[5][...] += sb

    cj = pl.BlockSpec((TM, 512), lambda j, i, k: (i, j))
    bj = pl.BlockSpec((1, 512), lambda j, i, k: (0, j))
    dpa, dpb, dga, dgb, dba, dbb = _mm(
        "out_proj_dx", (4, nt, 1),
        [(dx2b, pl.BlockSpec((TM, D), lambda j, i, k: (i, 0)), wo2, pl.BlockSpec((512, D), lambda j, i, k: (j, 0)), "nt", None)],
        [(gates, cj), (gates, pl.BlockSpec((TM, 512), lambda j, i, k: (i, 4 + j))), (pa, cj), (pb, cj)],
        [(_sds((T, D), BF16), cj)] * 4 + [(_sds((1, D), F32), bj)] * 2, dmerge_epi, None,
        ("parallel", "arbitrary", "arbitrary"))

    def store_shards(acc, ex, o, pids):
        for s in range(4):
            o[0][s] = acc[:, s * 256:(s + 1) * 256].astype(BF16)

    def proj_dw(name, y, dp):
        (dw,) = _mm(name, (2, 1, 1), [(y, tall(512), dp, tall_j(1024), "tn", None)],
                    [], [(_sds((N_DEV, 512, 256), BF16), pl.BlockSpec((4, 512, 256), lambda j, i, k: (j, 0, 0)))],
                    store_shards, None, PP)
        return dw

    def proj_dx(name, dp, w):
        (dy,) = _mm(name, (nt, 1, 1),
                    [(dp, full_rows(TM), w, pl.BlockSpec((N_DEV, 512, 256), lambda i, j, k: (0, 0, 0)), "nt_shards", None)],
                    [], [(_sds((T, 512), BF16), full_rows(TM, 512))], store_bf16, None, PP)
        return dy

    def in_dw(name, dact, n):
        (dw,) = _mm(name, (N_DEV, D // 512, 1), [(h1, tall(512), dact, tall_j(n), "tn", None)],
                    [], [(_sds((N_DEV, D, n), BF16), pl.BlockSpec((None, 512, n), lambda j, i, k: (j, i, 0)))],
                    store_bf16, None, PP)
        return dw

    dgate = jnp.concatenate([dga, dgb], axis=1)
    dwg = in_dw("gate_proj_dw", dgate, 512)
    dwpa = proj_dw("proj_a_dw", ya, dpa)
    dwpb = proj_dw("proj_b_dw", yb, dpb)
    token = send("proj", dict(w_gate=dwg, w_out=dwo.reshape(N_DEV, D // N_DEV, D), w_proj_a=dwpa, w_proj_b=dwpb))
    dya = proj_dx("proj_a_dx", dpa, wpa)
    dyb = proj_dx("proj_b_dx", dpb, wpb)

    dq_a, dk_a, dv_a = _dilated_bwd(qkv, _behind(slopes, token), dya, ya, lse_a)
    dq_b, dk_b, dv_b, dtiles = _attn_b_bwd(qkv, tiles, dyb, yb, lse_b)
    drpb = _rpb_reduce(dtiles.reshape(N_HEADS_B * N_DR, GRID_W * GRID_W), sel)[:, :N_DC]
    dqkv = jnp.concatenate([dq_a, dq_b, dk_a, dk_b, dv_a, dv_b], axis=1)
    dwq = in_dw("qkv_proj_dw", dqkv, 768)
    (dh1,) = _mm("in_proj_dx", (D // 512, nt, 1),
                 [(dqkv, a_ji(3 * D), wq, w_ji(768), "nt_shards", None), (dgate, a_ji(2 * D), wg, w_ji(512), "nt_shards", None)],
                 behind(send("in", dict(w_qkv=dwq))), [(_sds((T, D), F32), out_ji)], store_f32, None, PP)
    grad_x, dg1 = _rmsnorm_bwd("rmsnorm_mix_bwd", dh1, x, r1, norm_mix, dx2, False)

    small = dict(norm_mix=dg1, b_gate=jnp.concatenate([dba, dbb], axis=1), rpb=drpb, norm_mlp=dg2, norm_final=dg3)
    return loss, grad_x, small


BIG = ("w_qkv", "w_gate", "w_proj_a", "w_proj_b", "w_out", "w_up", "w_down")
SMALL = ("norm_mix", "b_gate", "rpb", "norm_mlp", "norm_final")
SMALL_ROWS = 96


def _place():
    return lax.axis_index("x"), lax.axis_index("y"), lax.axis_index("c")


def _peer(x, y, c, mask):
    return x ^ (mask >> 2), y ^ ((mask >> 1) & 1), c ^ (mask & 1)


_HBM = pl.BlockSpec(memory_space=pltpu.HBM)
_SEM = pl.BlockSpec(memory_space=pltpu.SEMAPHORE)
_ANY = pl.BlockSpec(memory_space=pl.ANY)
_EFFECT = pltpu.SideEffectType.DATAFLOW_SIDE_EFFECTING


def _hbm(a):
    return pltpu.with_memory_space_constraint(a, pltpu.HBM)


def _other_chips(x, y):
    return [(1 - x, y), (x, 1 - y), (1 - x, 1 - y)]


def _block(x, y, c):
    return 4 * x + 2 * y + c


def _rdma(src, dst, send_sem, recv_sem, to):
    return pltpu.make_async_remote_copy(src_ref=src, dst_ref=dst, send_sem=send_sem, recv_sem=recv_sem, device_id=to,
                                        device_id_type=MESH)


def _gather_start(name, shards):
    n = len(shards)

    def body(*refs):
        srcs, lands = refs[:n], refs[n:2 * n]
        outs = refs[2 * n:]
        send, recv_d2d, recv_ici, local = outs[:n], outs[n:2 * n], outs[2 * n:3 * n], outs[3 * n:4 * n]
        token_ref = outs[6 * n]
        x, y, c = _place()
        for a in range(n):
            mine = lands[a].at[_block(x, y, c)]
            _rdma(srcs[a], mine, send[a].at[0], recv_d2d[a].at[0], (x, y, 1 - c)).start()
            for j, chip in enumerate(_other_chips(x, y)):
                _rdma(srcs[a], mine, send[a].at[1 + j], recv_ici[a].at[j], (*chip, c)).start()
        for a in range(n):
            pltpu.make_async_copy(srcs[a], lands[a].at[_block(x, y, c)], local[a].at[0]).start()
        token_ref[...] = jnp.zeros_like(token_ref)

    lands = [_hbm(lax.empty((N_DEV,) + s.shape, s.dtype)) for s in shards]
    srcs = [_hbm(s) for s in shards]
    dma = pltpu.SemaphoreType.DMA
    res = pl.pallas_call(
        body, name=name,
        out_shape=[dma((4,))] * n + [dma((1,))] * n + [dma((3,))] * n + [dma((1,))] * n
        + [pltpu.HBM(s.shape, s.dtype) for s in srcs] + [pltpu.HBM(l.shape, l.dtype) for l in lands]
        + [jax.ShapeDtypeStruct((8, 128), F32)],
        in_specs=[_HBM] * (2 * n), out_specs=[_SEM] * (4 * n) + [_HBM] * (2 * n) + [pl.BlockSpec(memory_space=pltpu.VMEM)],
        input_output_aliases={i: 4 * n + i for i in range(2 * n)},
        compiler_params=pltpu.CompilerParams(has_side_effects=_EFFECT))(*srcs, *lands)
    keys = ("send", "recv_d2d", "recv_ici", "local", "src", "land")
    return [dict(zip(keys, (res[k * n + a] for k in range(6)))) for a in range(n)], res[6 * n]


def _gather_forward(name, handles, after):
    n = len(handles)

    def body(*refs):
        lands, recv_ici = refs[:n], refs[n:2 * n]
        outs = refs[2 * n + 1:]
        fwd_send, fwd_recv, token_ref = outs[n:2 * n], outs[2 * n:3 * n], outs[3 * n]
        x, y, c = _place()
        for a in range(n):
            for j, chip in enumerate(_other_chips(x, y)):
                blk = lands[a].at[_block(*chip, c)]
                _rdma(blk, blk, fwd_send[a].at[j], recv_ici[a].at[j], (*chip, c)).wait_recv()
                _rdma(blk, blk, fwd_send[a].at[j], fwd_recv[a].at[j], (x, y, 1 - c)).start()
        token_ref[...] = jnp.zeros_like(token_ref)

    dma = pltpu.SemaphoreType.DMA
    lands = [h["land"] for h in handles]
    res = pl.pallas_call(
        body, name=name,
        out_shape=[pltpu.HBM(l.shape, l.dtype) for l in lands] + [dma((3,))] * (2 * n) + [jax.ShapeDtypeStruct((8, 128), F32)],
        in_specs=[_HBM] * n + [_SEM] * n + [_ANY],
        out_specs=[_HBM] * n + [_SEM] * (2 * n) + [pl.BlockSpec(memory_space=pltpu.VMEM)],
        input_output_aliases={i: i for i in range(n)},
        compiler_params=pltpu.CompilerParams(has_side_effects=_EFFECT))(*lands, *[h["recv_ici"] for h in handles], after)
    return [dict(h, land=res[a], fwd_send=res[n + a], fwd_recv=res[2 * n + a]) for a, h in enumerate(handles)], res[3 * n]


def _gather_wait(name, handles, after):
    n = len(handles)

    def body(*refs):
        srcs, lands = refs[:n], refs[n:2 * n]
        send, recv_d2d, local, fwd_send, fwd_recv = (refs[(2 + k) * n:(3 + k) * n] for k in range(5))
        x, y, c = _place()
        sibling = (x, y, 1 - c)
        for a in range(n):
            mine = lands[a].at[_block(x, y, c)]
            _rdma(srcs[a], mine, send[a].at[0], recv_d2d[a].at[0], sibling).wait_send()
            _rdma(srcs[a], lands[a].at[_block(*sibling)], send[a].at[0], recv_d2d[a].at[0], sibling).wait_recv()
            pltpu.make_async_copy(srcs[a], mine, local[a].at[0]).wait()
            for j, chip in enumerate(_other_chips(x, y)):
                _rdma(srcs[a], mine, send[a].at[1 + j], fwd_recv[a].at[j], (*chip, c)).wait_send()
                blk = lands[a].at[_block(*chip, c)]
                _rdma(blk, blk, fwd_send[a].at[j], fwd_recv[a].at[j], sibling).wait_send()
                got = lands[a].at[_block(*chip, 1 - c)]
                _rdma(got, got, fwd_send[a].at[j], fwd_recv[a].at[j], sibling).wait_recv()

    srcs = [h["src"] for h in handles]
    lands = [h["land"] for h in handles]
    sems = [h[k] for k in ("send", "recv_d2d", "local", "fwd_send", "fwd_recv") for h in handles]
    res = pl.pallas_call(
        body, name=name, out_shape=[pltpu.HBM(s.shape, s.dtype) for s in srcs] + [pltpu.HBM(l.shape, l.dtype) for l in lands],
        in_specs=[_HBM] * (2 * n) + [_SEM] * (5 * n) + [_ANY], out_specs=[_HBM] * (2 * n),
        input_output_aliases={i: i for i in range(2 * n)},
        compiler_params=pltpu.CompilerParams(has_side_effects=_EFFECT))(*srcs, *lands, *sems, after)
    return list(res[n:])


def _gather_small(small):
    def body(in_ref, out_ref, send_sems, recv_sems, local_sem):
        x, y, c = _place()
        me = 4 * x + 2 * y + c
        copies = [pltpu.make_async_copy(in_ref, out_ref.at[me], local_sem)]
        for mask in range(1, N_DEV):
            copies.append(pltpu.make_async_remote_copy(
                src_ref=in_ref, dst_ref=out_ref.at[me], send_sem=send_sems.at[mask - 1],
                recv_sem=recv_sems.at[mask - 1], device_id=_peer(x, y, c, mask), device_id_type=MESH))
        for cp in copies:
            cp.start()
        for cp in copies:
            cp.wait()

    any_spec = pl.BlockSpec(memory_space=pl.ANY)
    return pl.pallas_call(
        body, name="gather_small", in_specs=[any_spec], out_specs=any_spec,
        out_shape=jax.ShapeDtypeStruct((N_DEV,) + small.shape, small.dtype),
        scratch_shapes=[pltpu.SemaphoreType.DMA((7,)), pltpu.SemaphoreType.DMA((7,)), pltpu.SemaphoreType.DMA])(small)


def _send_copy(src_refs, land_refs, send_sems, recv_sems, a, mask, scatter, x, y, c):
    px, py, pc = _peer(x, y, c, mask)
    src = src_refs[a].at[4 * px + 2 * py + pc] if scatter else src_refs[a]
    return pltpu.make_async_remote_copy(
        src_ref=src, dst_ref=land_refs[a].at[4 * x + 2 * y + c], send_sem=send_sems[a].at[mask - 1],
        recv_sem=recv_sems[a].at[mask - 1], device_id=(px, py, pc), device_id_type=MESH)


def _local_copy(src_refs, land_refs, local_sems, a, scatter, x, y, c):
    me = 4 * x + 2 * y + c
    return pltpu.make_async_copy(src_refs[a].at[me] if scatter else src_refs[a], land_refs[a].at[me], local_sems[a].at[0])


def _send_start(name, srcs, scatter, after=None):
    n = len(srcs)
    n_in = 2 * n + (after is not None)
    shapes = [s.shape[1:] if scatter else s.shape for s in srcs]

    def body(*refs):
        src_refs, land_refs = refs[:n], refs[n:2 * n]
        outs = refs[n_in:]
        send_sems, recv_sems, local_sems = outs[:n], outs[n:2 * n], outs[2 * n:3 * n]
        token_ref = outs[5 * n]
        x, y, c = _place()
        for a in range(n):
            for mask in range(1, N_DEV):
                _send_copy(src_refs, land_refs, send_sems, recv_sems, a, mask, scatter, x, y, c).start()
        for a in range(n):
            _local_copy(src_refs, land_refs, local_sems, a, scatter, x, y, c).start()
        token_ref[...] = jnp.zeros_like(token_ref)

    lands = [pltpu.with_memory_space_constraint(lax.empty((N_DEV,) + sh, s.dtype), pltpu.HBM) for sh, s in zip(shapes, srcs)]
    srcs = [pltpu.with_memory_space_constraint(s, pltpu.HBM) for s in srcs]
    res = pl.pallas_call(
        body, name=name,
        out_shape=[pltpu.SemaphoreType.DMA((7,))] * (2 * n) + [pltpu.SemaphoreType.DMA((1,))] * n
        + [pltpu.HBM(s.shape, s.dtype) for s in srcs] + [pltpu.HBM(l.shape, l.dtype) for l in lands]
        + [jax.ShapeDtypeStruct((8, 128), F32)],
        in_specs=[_HBM] * (2 * n) + [pl.BlockSpec(memory_space=pl.ANY)] * (n_in - 2 * n),
        out_specs=[_SEM] * (3 * n) + [_HBM] * (2 * n) + [pl.BlockSpec(memory_space=pltpu.VMEM)],
        input_output_aliases={i: 3 * n + i for i in range(2 * n)},
        compiler_params=pltpu.CompilerParams(has_side_effects=_EFFECT))(*srcs, *lands, *([] if after is None else [after]))
    handles = [tuple(res[k * n + a] for k in range(5)) for a in range(n)]
    return handles, res[5 * n]


def _send_wait(name, handles, scatter, after):
    n = len(handles)

    def body(*refs):
        src_refs, land_refs = refs[:n], refs[n:2 * n]
        send_sems, recv_sems, local_sems = refs[2 * n:3 * n], refs[3 * n:4 * n], refs[4 * n:5 * n]
        x, y, c = _place()
        for a in range(n):
            for mask in range(1, N_DEV):
                cp = _send_copy(src_refs, land_refs, send_sems, recv_sems, a, mask, scatter, x, y, c)
                cp.wait_send()
                cp.wait_recv()
            _local_copy(src_refs, land_refs, local_sems, a, scatter, x, y, c).wait()

    srcs = [h[3] for h in handles]
    lands = [h[4] for h in handles]
    res = pl.pallas_call(
        body, name=name, out_shape=[pltpu.HBM(s.shape, s.dtype) for s in srcs] + [pltpu.HBM(l.shape, l.dtype) for l in lands],
        in_specs=[_HBM] * (2 * n) + [_SEM] * (3 * n) + [pl.BlockSpec(memory_space=pl.ANY)], out_specs=[_HBM] * (2 * n),
        input_output_aliases={i: i for i in range(2 * n)},
        compiler_params=pltpu.CompilerParams(has_side_effects=_EFFECT))(
            *srcs, *lands, *[h[0] for h in handles], *[h[1] for h in handles], *[h[2] for h in handles], after)
    return list(res[n:])


def _adamw(name, parts, w, m, v):
    K, n = w.shape
    tr = min(K, 128)

    def body(p_ref, w_ref, m_ref, v_ref, g_ref, d_ref, nm_ref, nv_ref):
        g = p_ref[0].astype(F32)
        for i in range(1, N_DEV):
            g = g + p_ref[i].astype(F32)
        nm = ADAM_B1 * m_ref[...] + (1.0 - ADAM_B1) * g
        nv = ADAM_B2 * v_ref[...] + (1.0 - ADAM_B2) * jnp.square(g)
        m_hat = nm / (1.0 - ADAM_B1 ** ADAM_STEP)
        v_hat = nv / (1.0 - ADAM_B2 ** ADAM_STEP)
        g_ref[...] = g
        d_ref[...] = -ADAM_LR * (m_hat / (jnp.sqrt(v_hat) + ADAM_EPS) + ADAM_WD * w_ref[...])
        nm_ref[...] = nm
        nv_ref[...] = nv

    blk = pl.BlockSpec((tr, n), lambda i: (i, 0))
    return _pc(body, name=name, grid=(K // tr,), in_specs=[pl.BlockSpec((N_DEV, tr, n), lambda i: (0, i, 0)), blk, blk, blk],
               out_specs=[blk] * 4, out_shape=[jax.ShapeDtypeStruct((K, n), F32)] * 4, sem=("parallel",))(parts, w, m, v)


def _pack_small(vals):
    flat = jnp.concatenate([v.reshape(-1) for v in vals])
    return jnp.pad(flat, (0, SMALL_ROWS * 128 - flat.shape[0])).reshape(SMALL_ROWS, 128)


def _unpack_small(packed, shapes):
    flat, out, off = packed.reshape(-1), [], 0
    for s in shapes:
        size = int(np.prod(s))
        out.append(flat[off:off + size].reshape(s))
        off += size
    return out


def kernel(x, norm_mix, w_qkv, w_gate, b_gate, rpb, w_proj_a, w_proj_b, w_out, norm_mlp, w_up, w_down, norm_final, loss_target, m_norm_mix, m_w_qkv, m_w_gate, m_b_gate, m_rpb, m_w_proj_a, m_w_proj_b, m_w_out, m_norm_mlp, m_w_up, m_w_down, m_norm_final, v_norm_mix, v_w_qkv, v_w_gate, v_b_gate, v_rpb, v_w_proj_a, v_w_proj_b, v_w_out, v_norm_mlp, v_w_up, v_w_down, v_norm_final):
    w = dict(norm_mix=norm_mix, w_qkv=w_qkv, w_gate=w_gate, b_gate=b_gate, rpb=rpb, w_proj_a=w_proj_a, w_proj_b=w_proj_b,
             w_out=w_out, norm_mlp=norm_mlp, w_up=w_up, w_down=w_down, norm_final=norm_final)
    m = dict(norm_mix=m_norm_mix, w_qkv=m_w_qkv, w_gate=m_w_gate, b_gate=m_b_gate, rpb=m_rpb, w_proj_a=m_w_proj_a,
             w_proj_b=m_w_proj_b, w_out=m_w_out, norm_mlp=m_norm_mlp, w_up=m_w_up, w_down=m_w_down, norm_final=m_norm_final)
    v = dict(norm_mix=v_norm_mix, w_qkv=v_w_qkv, w_gate=v_w_gate, b_gate=v_b_gate, rpb=v_rpb, w_proj_a=v_w_proj_a,
             w_proj_b=v_w_proj_b, w_out=v_w_out, norm_mlp=v_norm_mlp, w_up=v_w_up, w_down=v_w_down, norm_final=v_norm_final)
    order = ("norm_mix", "w_qkv", "w_gate", "b_gate", "rpb", "w_proj_a", "w_proj_b", "w_out", "norm_mlp", "w_up",
             "w_down", "norm_final")

    shard = {n: w[n][0].astype(BF16) for n in BIG}
    stages = dict(qkv=("w_qkv",), gate=("w_gate", "w_proj_a", "w_proj_b", "w_out"), mlp=("w_up", "w_down"))
    names = [n for s in stages.values() for n in s]
    handles, token = _gather_start("gather_start", [shard[n] for n in names])
    handles = dict(zip(names, handles))

    def late_weights(stage, after):
        if stage == "qkv":
            return tuple(_gather_wait("gather_wait_qkv", _gather_forward("gather_forward_qkv", [handles["w_qkv"]], after)[0], after))
        if stage == "attention":
            handles["gate"], tok = _gather_forward("gather_forward_gate", [handles[n] for n in stages["gate"]], after)
            return (tok,)
        if stage == "mixer":
            handles["up"], tok = _gather_forward("gather_forward_up", [handles["w_up"]], after)
            return tuple(_gather_wait("gather_wait_gate", handles["gate"], after)) + (tok,)
        if stage == "residual":
            handles["down"], tok = _gather_forward("gather_forward_down", [handles["w_down"]], after)
            return (tok,)
        return tuple(_gather_wait("gather_wait_" + stage, handles[stage], after))

    sent = {}

    def send(stage, grads):
        handles, tok = _send_start("grads_start_" + stage, list(grads.values()), True)
        sent[stage] = (tuple(grads), handles)
        return tok

    loss, grad_x, small = _local_step(x[0], loss_target[0], _behind(norm_mix, token), b_gate, rpb[0], norm_mlp, norm_final,
                                      late_weights, send)

    grad, delta, new_m, new_v = {}, {}, {}, {}
    after = grad_x
    for stage in ("mlp", "proj", "in"):
        names, handles = sent[stage]
        for n, parts in zip(names, _send_wait("grads_wait_" + stage, handles, True, after)):
            res = _adamw("adamw_" + n, parts, w[n][0], m[n][0], v[n][0])
            grad[n], delta[n], new_m[n], new_v[n] = (r[None] for r in res)
            after = res[1]
    res = _adamw("adamw_small", _gather_small(_pack_small([small[n] for n in SMALL])),
                 _pack_small([w[n] for n in SMALL]), _pack_small([m[n] for n in SMALL]), _pack_small([v[n] for n in SMALL]))
    shapes = [w[n].shape for n in SMALL]
    for tree, packed in zip((grad, delta, new_m, new_v), res):
        for n, val in zip(SMALL, _unpack_small(packed, shapes)):
            tree[n] = val

    total = lax.psum(loss[0, 0], ("x", "y", "c"))
    return (total, grad_x[None], *[grad[n] for n in order], *[delta[n] for n in order],
            *[new_m[n] for n in order], *[new_v[n] for n in order])
```

```python
import numpy as np
import jax
import jax.numpy as jnp
from jax import lax
from jax.experimental import pallas as pl
from jax.experimental.pallas import tpu as pltpu

F32 = jnp.float32
BF16 = jnp.bfloat16
MESH = pl.DeviceIdType.MESH

D_MODEL = 2048
HEAD_DIM = 128
N_HEADS = 16
N_HEADS_A = 12
N_HEADS_B = 4
DILATIONS = (1, 4, 16)
HALF_WINDOW = 64
GRID_W = 64
NA_ROWS = 8
NA_COLS = 16
D_FF = 4 * D_MODEL
N_DEV = 8
EPS = 1e-6
NEG = -1e30
SCALE = HEAD_DIM ** -0.5
ADAM_LR, ADAM_B1, ADAM_B2, ADAM_EPS, ADAM_WD, ADAM_STEP = 0.001, 0.9, 0.999, 1e-08, 0.01, 10

VMEM_LIMIT = 56 * 1024 * 1024
TM = 512
TL = 1024
TS = 256
BQ_A = 128


def _pc(body, *, name, grid, in_specs, out_specs, out_shape, scratch=(), sem=None):
    return pl.pallas_call(
        body, name=name, grid=grid, in_specs=in_specs, out_specs=out_specs, out_shape=out_shape,
        scratch_shapes=list(scratch),
        compiler_params=pltpu.CompilerParams(dimension_semantics=sem, vmem_limit_bytes=VMEM_LIMIT))


_DOT_DIMS = {"nn": (((1,), (0,)), ((), ())), "nt": (((1,), (1,)), ((), ())), "tn": (((0,), (0,)), ((), ()))}


def _dot(a, b, mode, precision=None):
    if mode == "nn_shards":
        return jnp.concatenate([_dot(a, b[s], "nn") for s in range(b.shape[0])], axis=1)
    if mode == "nt_shards":
        n = b.shape[2]
        out = _dot(a[:, :n], b[0], "nt")
        for s in range(1, b.shape[0]):
            out = out + _dot(a[:, s * n:(s + 1) * n], b[s], "nt")
        return out
    return lax.dot_general(a, b, _DOT_DIMS[mode], preferred_element_type=F32, precision=precision)


def _mm(name, grid, pairs, extras, outs, epilogue, acc_shape, sem):
    n_pairs, n_extra, n_out = len(pairs), len(extras), len(outs)
    k_axis = len(grid) - 1
    nk = grid[k_axis]

    def body(*refs):
        pids = [pl.program_id(ax) for ax in range(len(grid))]
        ab = refs[:2 * n_pairs]
        ex = refs[2 * n_pairs:2 * n_pairs + n_extra]
        o = refs[2 * n_pairs + n_extra:2 * n_pairs + n_extra + n_out]
        part = None
        for p, pair in enumerate(pairs):
            a = ab[2 * p][...]
            if pair[5] is not None:
                a = pair[5](a)
            d = _dot(a, ab[2 * p + 1][...], pair[4])
            part = d if part is None else part + d
        if nk == 1:
            epilogue(part, ex, o, pids)
        else:
            acc = refs[-1]
            k = pids[k_axis]

            @pl.when(k == 0)
            def _():
                acc[...] = part

            @pl.when(k > 0)
            def _():
                acc[...] += part

            @pl.when(k == nk - 1)
            def _():
                epilogue(acc[...], ex, o, pids)

    operands, in_specs = [], []
    for a, a_spec, b, b_spec, _, _ in pairs:
        operands += [a, b]
        in_specs += [a_spec, b_spec]
    for e, e_spec in extras:
        operands.append(e)
        in_specs.append(e_spec)
    res = _pc(body, name=name, grid=grid, in_specs=in_specs, out_specs=[s for _, s in outs],
              out_shape=[o for o, _ in outs], scratch=[pltpu.VMEM(acc_shape, F32)] if nk > 1 else [], sem=sem)(*operands)
    return res


def _rmsnorm_fwd(x, g, name):
    T, D = x.shape

    def body(x_ref, g_ref, h_ref, r_ref):
        xv = x_ref[...]
        r = lax.rsqrt(jnp.mean(xv * xv, axis=-1, keepdims=True) + EPS)
        h_ref[...] = (xv * r * g_ref[...]).astype(BF16)
        r_ref[...] = r

    return _pc(body, name=name, grid=(T // TM,),
               in_specs=[pl.BlockSpec((TM, D), lambda i: (i, 0)), pl.BlockSpec((1, D), lambda i: (0, 0))],
               out_specs=[pl.BlockSpec((TM, D), lambda i: (i, 0)), pl.BlockSpec((TM, 1), lambda i: (i, 0))],
               out_shape=[jax.ShapeDtypeStruct((T, D), BF16), jax.ShapeDtypeStruct((T, 1), F32)],
               sem=("parallel",))(x, g)


def _rmsnorm_bwd_block(dh, x, r, g, dres):
    xh = x * r
    dxh = dh * g
    dx = dres + r * (dxh - xh * jnp.mean(dxh * xh, axis=-1, keepdims=True))
    return dx, jnp.sum(dh * xh, axis=0, keepdims=True)


def _rmsnorm_bwd(name, dh, x, r, g, dres, want_bf16):
    T, D = x.shape

    def body(dh_ref, x_ref, r_ref, g_ref, dres_ref, *outs):
        dx, dgp = _rmsnorm_bwd_block(dh_ref[...], x_ref[...], r_ref[...], g_ref[...], dres_ref[...])
        outs[0][...] = dx
        if want_bf16:
            outs[1][...] = dx.astype(BF16)
        first = pl.program_id(0) == 0

        @pl.when(first)
        def _():
            outs[-1][...] = dgp

        @pl.when(jnp.logical_not(first))
        def _():
            outs[-1][...] += dgp

    row = pl.BlockSpec((TS, D), lambda i: (i, 0))
    vec = pl.BlockSpec((1, D), lambda i: (0, 0))
    n_row = 2 if want_bf16 else 1
    return _pc(body, name=name, grid=(T // TS,), in_specs=[row, row, pl.BlockSpec((TS, 1), lambda i: (i, 0)), vec, row],
               out_specs=[row] * n_row + [vec],
               out_shape=[jax.ShapeDtypeStruct((T, D), F32)] + [jax.ShapeDtypeStruct((T, D), BF16)] * (n_row - 1)
               + [jax.ShapeDtypeStruct((1, D), F32)], sem=("arbitrary",))(dh, x, r, g, dres)


def _final_norm_loss(x3, g, tgt):
    T, D = x3.shape

    def body(x_ref, g_ref, t_ref, loss_ref, dx_ref, dxb_ref, dg_ref):
        i = pl.program_id(0)
        xv = x_ref[...]
        gv = g_ref[...]
        r = lax.rsqrt(jnp.mean(xv * xv, axis=-1, keepdims=True) + EPS)
        xh = xv * r
        e = xh * gv - t_ref[...]
        part = 0.5 * jnp.sum(jnp.mean(e * e, axis=-1, keepdims=True))
        dy = e * (1.0 / D)
        dxh = dy * gv
        dx = r * (dxh - xh * jnp.mean(dxh * xh, axis=-1, keepdims=True))
        dx_ref[...] = dx
        dxb_ref[...] = dx.astype(BF16)
        dgp = jnp.sum(dy * xh, axis=0, keepdims=True)

        @pl.when(i == 0)
        def _():
            loss_ref[...] = jnp.zeros(loss_ref.shape, F32) + part
            dg_ref[...] = dgp

        @pl.when(i > 0)
        def _():
            loss_ref[...] += part
            dg_ref[...] += dgp

    row = pl.BlockSpec((TM, D), lambda i: (i, 0))
    vec = pl.BlockSpec((1, D), lambda i: (0, 0))
    return _pc(body, name="final_norm_loss", grid=(T // TM,), in_specs=[row, vec, row],
               out_specs=[pl.BlockSpec((8, 128), lambda i: (0, 0)), row, row, vec],
               out_shape=[jax.ShapeDtypeStruct((8, 128), F32), jax.ShapeDtypeStruct((T, D), F32),
                          jax.ShapeDtypeStruct((T, D), BF16), jax.ShapeDtypeStruct((1, D), F32)],
               sem=("arbitrary",))(x3, g, tgt)


def _alibi_slopes():
    return np.asarray(2.0 ** (-8.0 * np.arange(1, N_HEADS_A + 1) / N_HEADS_A), dtype=np.float32)


def _a_windows(T):
    half = [HALF_WINDOW * d for d in DILATIONS]
    return half, [min(BQ_A + 2 * w, T) for w in half]


def _a_bias(slope, g, offset, wk, half):
    d = DILATIONS[g]
    dlt = offset + lax.broadcasted_iota(jnp.int32, (BQ_A, wk), 1) - lax.broadcasted_iota(jnp.int32, (BQ_A, wk), 0)
    ad = jnp.maximum(dlt, -dlt)
    ok = (ad <= half) & ((ad & (d - 1)) == 0)
    return jnp.where(ok, -slope * ad.astype(F32), NEG)


def _a_scores(q, kw, slope, g, offset, wk, half):
    return _dot(q, kw, "nt") * SCALE + _a_bias(slope, g, offset, wk, half)


N_JOINT = 2


def _a_qkv_specs(T, inner):
    specs = []
    for g in range(N_JOINT):
        specs.append(pl.BlockSpec((BQ_A, HEAD_DIM), lambda j, i, g=g: (i, 4 * g + j)))
        specs.append(pl.BlockSpec((T, HEAD_DIM), lambda j, i, g=g: (0, N_HEADS + 4 * g + j)))
        specs.append(pl.BlockSpec((T, HEAD_DIM), lambda j, i, g=g: (0, 2 * N_HEADS + 4 * g + j)))
    return specs


def _attn_a_fwd(qkv, slopes, o_last, lse_last):
    T = qkv.shape[0]
    nq = T // BQ_A
    half, wks = _a_windows(T)

    def body(sl_ref, *refs):
        qkv_refs, (o_ref, l_ref, y_ref, lse_ref) = refs[:3 * N_JOINT], refs[3 * N_JOINT:]
        j = pl.program_id(0)
        t0 = pl.program_id(1) * BQ_A
        ss, vws = [], []
        m = l_ref[...]
        for g in range(N_JOINT):
            q_ref, k_ref, v_ref = qkv_refs[3 * g:3 * g + 3]
            ws = pl.multiple_of(jnp.clip(t0 - half[g], 0, T - wks[g]), 64)
            ss.append(_a_scores(q_ref[...], k_ref[pl.ds(ws, wks[g]), :], sl_ref[4 * g + j], g, ws - t0, wks[g], half[g]))
            vws.append(v_ref[pl.ds(ws, wks[g]), :])
            m = jnp.maximum(m, jnp.max(ss[g], axis=-1, keepdims=True))
        den = jnp.exp(l_ref[...] - m)
        acc = den * o_ref[...].astype(F32)
        for g in range(N_JOINT):
            p = jnp.exp(ss[g] - m)
            den = den + jnp.sum(p, axis=-1, keepdims=True)
            acc = acc + _dot(p.astype(BF16), vws[g], "nn")
        y_ref[...] = (acc / den).astype(BF16)
        lse_ref[...] = m + jnp.log(den)

    blk = pl.BlockSpec((BQ_A, HEAD_DIM), lambda j, i: (i, j))
    col = pl.BlockSpec((None, BQ_A, 1), lambda j, i: (j, i, 0))
    return _pc(body, name="attn_a_fwd", grid=(4, nq),
               in_specs=[pl.BlockSpec(memory_space=pltpu.SMEM)] + _a_qkv_specs(T, nq) + [blk, col], out_specs=[blk, col],
               out_shape=[jax.ShapeDtypeStruct((T, 4 * HEAD_DIM), BF16), jax.ShapeDtypeStruct((4, T, 1), F32)],
               sem=("parallel", "parallel"))(slopes, *([qkv] * (3 * N_JOINT)), o_last, lse_last)


def _attn_a_bwd(qkv, slopes, dy, y, lse):
    T = qkv.shape[0]
    nq = T // BQ_A
    half, wks = _a_windows(T)

    def body(sl_ref, *refs):
        qkv_refs = refs[:3 * N_JOINT]
        dy_ref, y_ref, lse_ref, dq_ref, dk_ref, dv_ref, delta_ref, dk_acc, dv_acc = refs[3 * N_JOINT:]
        j = pl.program_id(0)
        i = pl.program_id(1)
        t0 = i * BQ_A

        @pl.when(i == 0)
        def _():
            dk_acc[...] = jnp.zeros_like(dk_acc)
            dv_acc[...] = jnp.zeros_like(dv_acc)

        dyv = dy_ref[...]
        delta = jnp.sum(dyv.astype(F32) * y_ref[...].astype(F32), axis=-1, keepdims=True)
        delta_ref[...] = delta
        lse_v = lse_ref[...]
        for g in range(N_JOINT):
            q_ref, k_ref, v_ref = qkv_refs[3 * g:3 * g + 3]
            ws = pl.multiple_of(jnp.clip(t0 - half[g], 0, T - wks[g]), 64)
            win = pl.ds(ws, wks[g])
            qv = q_ref[...]
            kw = k_ref[win, :]
            s = _a_scores(qv, kw, sl_ref[4 * g + j], g, ws - t0, wks[g], half[g])
            p = jnp.exp(s - lse_v)
            dp = _dot(dyv, v_ref[win, :], "nt")
            ds = (p * (dp - delta)).astype(BF16)
            dq_ref[g] = (_dot(ds, kw, "nn") * SCALE).astype(BF16)
            dk_acc[g, win, :] += _dot(ds, qv, "tn") * SCALE
            dv_acc[g, win, :] += _dot(p.astype(BF16), dyv, "tn")

        @pl.when(i == nq - 1)
        def _():
            dk_ref[...] = dk_acc[...].astype(BF16)
            dv_ref[...] = dv_acc[...].astype(BF16)

    blk = pl.BlockSpec((BQ_A, HEAD_DIM), lambda j, i: (i, j))
    col = pl.BlockSpec((None, BQ_A, 1), lambda j, i: (j, i, 0))
    whole = pl.BlockSpec((N_JOINT, T, HEAD_DIM), lambda j, i: (0, 0, j))
    full = jax.ShapeDtypeStruct((N_JOINT, T, 4 * HEAD_DIM), BF16)
    return _pc(body, name="attn_a_bwd", grid=(4, nq),
               in_specs=[pl.BlockSpec(memory_space=pltpu.SMEM)] + _a_qkv_specs(T, nq) + [blk, blk, col],
               out_specs=[pl.BlockSpec((N_JOINT, BQ_A, HEAD_DIM), lambda j, i: (0, i, j)), whole, whole, col],
               out_shape=[full, full, full, jax.ShapeDtypeStruct((4, T, 1), F32)],
               scratch=[pltpu.VMEM((N_JOINT, T, HEAD_DIM), F32), pltpu.VMEM((N_JOINT, T, HEAD_DIM), F32)],
               sem=("parallel", "arbitrary"))(slopes, *([qkv] * (3 * N_JOINT)), dy, y, lse)


SUB_A = 4


def _to_residue_major(a, d, axis=0):
    if d == 1:
        return a
    sh = a.shape
    a = a.reshape(sh[:axis] + (sh[axis] // d, d) + sh[axis + 1:])
    return jnp.swapaxes(a, axis, axis + 1).reshape(sh)


def _to_token_order(a, d, axis=0):
    if d == 1:
        return a
    sh = a.shape
    a = a.reshape(sh[:axis] + (d, sh[axis] // d) + sh[axis + 1:])
    return jnp.swapaxes(a, axis, axis + 1).reshape(sh)


def _band_bias(slope, offset, wk):
    dlt = offset + lax.broadcasted_iota(jnp.int32, (BQ_A, wk), 1) - lax.broadcasted_iota(jnp.int32, (BQ_A, wk), 0)
    ad = jnp.maximum(dlt, -dlt)
    return jnp.where(ad <= HALF_WINDOW, -slope * ad.astype(F32), NEG)


def _band_window(step, u, seg, wk):
    t0 = (step * SUB_A + u) * BQ_A
    lo = (t0 // seg) * seg
    ws = pl.multiple_of(jnp.clip(t0 - HALF_WINDOW, lo, lo + seg - wk), 64)
    return slice(u * BQ_A, (u + 1) * BQ_A), pl.ds(ws, wk), ws - t0


def _band_specs(T, cols):
    bs = SUB_A * BQ_A
    assert T % bs == 0 and T // max(DILATIONS) >= BQ_A, "a query sub-block must lie inside one segment"
    return [pl.BlockSpec(memory_space=pltpu.SMEM),
            pl.BlockSpec((bs, HEAD_DIM), lambda j, i: (i, cols[0] + j)),
            pl.BlockSpec((T, HEAD_DIM), lambda j, i: (0, cols[1] + j)),
            pl.BlockSpec((T, HEAD_DIM), lambda j, i: (0, cols[2] + j))]


def _banded_fwd(name, src, cols, seg, slopes):
    T = src.shape[0]
    bs = SUB_A * BQ_A
    wk = min(BQ_A + 2 * HALF_WINDOW, seg)

    def body(sl_ref, q_ref, k_ref, v_ref, o_ref, lse_ref):
        slope = sl_ref[pl.program_id(0)]
        for u in range(SUB_A):
            rows, win, offset = _band_window(pl.program_id(1), u, seg, wk)
            s = _dot(q_ref[rows, :], k_ref[win, :], "nt") * SCALE + _band_bias(slope, offset, wk)
            m = jnp.max(s, axis=-1, keepdims=True)
            p = jnp.exp(s - m)
            den = jnp.sum(p, axis=-1, keepdims=True)
            o_ref[rows, :] = (_dot(p.astype(BF16), v_ref[win, :], "nn") / den).astype(BF16)
            lse_ref[rows, :] = m + jnp.log(den)

    return _pc(body, name=name, grid=(4, T // bs), in_specs=_band_specs(T, cols),
               out_specs=[pl.BlockSpec((bs, HEAD_DIM), lambda j, i: (i, j)), pl.BlockSpec((None, bs, 1), lambda j, i: (j, i, 0))],
               out_shape=[jax.ShapeDtypeStruct((T, 4 * HEAD_DIM), BF16), jax.ShapeDtypeStruct((4, T, 1), F32)],
               sem=("parallel", "parallel"))(slopes, src, src, src)


def _banded_bwd(name, src, cols, seg, slopes, dy, delta_lse):
    T = src.shape[0]
    bs = SUB_A * BQ_A
    steps = T // bs
    wk = min(BQ_A + 2 * HALF_WINDOW, seg)

    def body(sl_ref, q_ref, k_ref, v_ref, dy_ref, delta_ref, lse_ref, dq_ref, dk_ref, dv_ref, dk_acc, dv_acc):
        slope = sl_ref[pl.program_id(0)]
        step = pl.program_id(1)

        @pl.when(step == 0)
        def _():
            dk_acc[...] = jnp.zeros_like(dk_acc)
            dv_acc[...] = jnp.zeros_like(dv_acc)

        for u in range(SUB_A):
            rows, win, offset = _band_window(step, u, seg, wk)
            qv, kw, dyv = q_ref[rows, :], k_ref[win, :], dy_ref[rows, :]
            s = _dot(qv, kw, "nt") * SCALE + _band_bias(slope, offset, wk)
            p = jnp.exp(s - lse_ref[rows, :])
            ds = (p * (_dot(dyv, v_ref[win, :], "nt") - delta_ref[rows, :])).astype(BF16)
            dq_ref[rows, :] = (_dot(ds, kw, "nn") * SCALE).astype(BF16)
            dk_acc[win, :] += _dot(ds, qv, "tn") * SCALE
            dv_acc[win, :] += _dot(p.astype(BF16), dyv, "tn")

        @pl.when(step == steps - 1)
        def _():
            dk_ref[...] = dk_acc[...].astype(BF16)
            dv_ref[...] = dv_acc[...].astype(BF16)

    blk = pl.BlockSpec((bs, HEAD_DIM), lambda j, i: (i, j))
    whole = pl.BlockSpec((T, HEAD_DIM), lambda j, i: (0, j))
    full = jax.ShapeDtypeStruct((T, 4 * HEAD_DIM), BF16)
    return _pc(body, name=name, grid=(4, steps),
               in_specs=_band_specs(T, cols) + [blk, pl.BlockSpec((None, bs, 1), lambda j, i: (j, i, 0)),
                                               pl.BlockSpec((None, bs, 1), lambda j, i: (4 + j, i, 0))],
               out_specs=[blk, whole, whole], out_shape=[full, full, full],
               scratch=[pltpu.VMEM((T, HEAD_DIM), F32), pltpu.VMEM((T, HEAD_DIM), F32)],
               sem=("parallel", "arbitrary"))(slopes, src, src, src, dy, delta_lse, delta_lse)


def _last_group_source(qkv):
    third = qkv.shape[1] // 3
    own = jnp.concatenate([qkv[:, t * third + 512 * N_JOINT:t * third + 512 * (N_JOINT + 1)] for t in range(3)], axis=1)
    return _to_residue_major(own, DILATIONS[N_JOINT])


def _dilated_fwd(qkv, slopes):
    d = DILATIONS[N_JOINT]
    o, lse = _banded_fwd("attn_a_last_fwd", _last_group_source(qkv), (0, 4, 8), qkv.shape[0] // d,
                         slopes[4 * N_JOINT:] * float(d))
    return _attn_a_fwd(qkv, slopes, _to_token_order(o, d), _to_token_order(lse, d, axis=1))


def _dilated_bwd(qkv, slopes, dy, y, lse):
    d = DILATIONS[N_JOINT]
    dq, dk, dv, delta = _attn_a_bwd(qkv, slopes, dy, y, lse)
    delta_lse = _to_residue_major(jnp.concatenate([delta, lse], axis=0), d, axis=1)
    last = _banded_bwd("attn_a_last_bwd", _last_group_source(qkv), (0, 4, 8), qkv.shape[0] // d,
                       slopes[4 * N_JOINT:] * float(d), _to_residue_major(dy, d), delta_lse)
    return [jnp.concatenate([joint[g] for g in range(N_JOINT)] + [_to_token_order(own, d)], axis=1)
            for joint, own in zip((dq, dk, dv), last)]


N_DR = 2 * NA_ROWS - 1
N_DC = 2 * NA_COLS - 1
WK_B = NA_ROWS * GRID_W


def _toeplitz_consts():
    col = np.arange(GRID_W)
    start = np.clip(col - NA_COLS // 2, 0, GRID_W - NA_COLS)
    ok = (col[None, :] >= start[:, None]) & (col[None, :] < start[:, None] + NA_COLS)
    dc = np.clip(col[None, :] - col[:, None], -(NA_COLS - 1), NA_COLS - 1) + NA_COLS - 1
    sel = np.zeros((128, GRID_W * GRID_W), np.float32)
    sel[dc.reshape(-1), np.arange(GRID_W * GRID_W)] = ok.reshape(-1).astype(np.float32)
    mask = np.where(ok, 0.0, NEG).astype(np.float32).reshape(1, -1)
    return sel, mask


def _rpb_expand(rpb_rows, sel, mask):
    def body(r_ref, s_ref, m_ref, o_ref):
        o_ref[...] = _dot(r_ref[...], s_ref[...], "nn", precision=lax.Precision.HIGHEST) + m_ref[...]

    n = rpb_rows.shape[0]
    return _pc(body, name="rpb_expand", grid=(1,),
               in_specs=[pl.BlockSpec((n, 128), lambda i: (0, 0)), pl.BlockSpec(sel.shape, lambda i: (0, 0)),
                         pl.BlockSpec(mask.shape, lambda i: (0, 0))],
               out_specs=pl.BlockSpec((n, sel.shape[1]), lambda i: (0, 0)),
               out_shape=jax.ShapeDtypeStruct((n, sel.shape[1]), F32), sem=("arbitrary",))(rpb_rows, sel, mask)


def _rpb_reduce(dtiles, sel):
    def body(t_ref, s_ref, o_ref):
        o_ref[...] = _dot(t_ref[...], s_ref[...], "nt", precision=lax.Precision.HIGHEST)

    n = dtiles.shape[0]
    return _pc(body, name="rpb_reduce", grid=(1,),
               in_specs=[pl.BlockSpec(dtiles.shape, lambda i: (0, 0)), pl.BlockSpec(sel.shape, lambda i: (0, 0))],
               out_specs=pl.BlockSpec((n, 128), lambda i: (0, 0)),
               out_shape=jax.ShapeDtypeStruct((n, 128), F32), sem=("arbitrary",))(dtiles, sel)


ROWS_B = 8
BQ_B = ROWS_B * GRID_W


def _b_specs(T):
    return [pl.BlockSpec((BQ_B, HEAD_DIM), lambda h, r: (r, N_HEADS_A + h)),
            pl.BlockSpec((T, HEAD_DIM), lambda h, r: (0, N_HEADS + N_HEADS_A + h)),
            pl.BlockSpec((T, HEAD_DIM), lambda h, r: (0, 2 * N_HEADS + N_HEADS_A + h)),
            pl.BlockSpec((None, N_DR, GRID_W, GRID_W), lambda h, r: (h, 0, 0, 0))]


def _b_row(step, u, rows, t_ref, bias_ref):
    r = step * ROWS_B + u
    rs = jnp.clip(r - NA_ROWS // 2, 0, rows - NA_ROWS)
    first_tile = rs - r + NA_ROWS - 1
    for a in range(NA_ROWS):
        bias_ref[u, :, a * GRID_W:(a + 1) * GRID_W] = t_ref[first_tile + a]
    return slice(u * GRID_W, (u + 1) * GRID_W), pl.ds(pl.multiple_of(rs * GRID_W, GRID_W), WK_B), first_tile


def _attn_b_fwd(qkv, tiles):
    T = qkv.shape[0]
    rows = T // GRID_W

    def body(q_ref, k_ref, v_ref, t_ref, y_ref, lse_ref, bias_ref):
        for u in range(ROWS_B):
            blk, win, _ = _b_row(pl.program_id(1), u, rows, t_ref, bias_ref)
            s = _dot(q_ref[blk, :], k_ref[win, :], "nt") * SCALE + bias_ref[u]
            m = jnp.max(s, axis=-1, keepdims=True)
            p = jnp.exp(s - m)
            den = jnp.sum(p, axis=-1, keepdims=True)
            y_ref[blk, :] = (_dot(p.astype(BF16), v_ref[win, :], "nn") / den).astype(BF16)
            lse_ref[blk, :] = m + jnp.log(den)

    return _pc(body, name="attn_b_fwd", grid=(N_HEADS_B, rows // ROWS_B), in_specs=_b_specs(T),
               out_specs=[pl.BlockSpec((BQ_B, HEAD_DIM), lambda h, r: (r, h)),
                          pl.BlockSpec((None, BQ_B, 1), lambda h, r: (h, r, 0))],
               out_shape=[jax.ShapeDtypeStruct((T, N_HEADS_B * HEAD_DIM), BF16),
                          jax.ShapeDtypeStruct((N_HEADS_B, T, 1), F32)],
               scratch=[pltpu.VMEM((ROWS_B, GRID_W, WK_B), F32)], sem=("parallel", "parallel"))(qkv, qkv, qkv, tiles)


def _attn_b_bwd(qkv, tiles, dy, y, lse):
    T = qkv.shape[0]
    rows = T // GRID_W
    steps = rows // ROWS_B

    def body(q_ref, k_ref, v_ref, t_ref, dy_ref, y_ref, lse_ref, dq_ref, dk_ref, dv_ref, dt_ref,
             bias_ref, ds_ref, dk_acc, dv_acc):
        step = pl.program_id(1)

        @pl.when(step == 0)
        def _():
            dk_acc[...] = jnp.zeros_like(dk_acc)
            dv_acc[...] = jnp.zeros_like(dv_acc)
            dt_ref[...] = jnp.zeros_like(dt_ref)

        for u in range(ROWS_B):
            blk, win, first_tile = _b_row(step, u, rows, t_ref, bias_ref)
            qv = q_ref[blk, :]
            kw = k_ref[win, :]
            dyv = dy_ref[blk, :]
            s = _dot(qv, kw, "nt") * SCALE + bias_ref[u]
            p = jnp.exp(s - lse_ref[blk, :])
            delta = jnp.sum(dyv.astype(F32) * y_ref[blk, :].astype(F32), axis=-1, keepdims=True)
            ds = p * (_dot(dyv, v_ref[win, :], "nt") - delta)
            ds_ref[u] = ds
            for a in range(NA_ROWS):
                dt_ref[first_tile + a] += ds_ref[u, :, a * GRID_W:(a + 1) * GRID_W]
            dsb = ds.astype(BF16)
            dq_ref[blk, :] = (_dot(dsb, kw, "nn") * SCALE).astype(BF16)
            dk_acc[win, :] += _dot(dsb, qv, "tn") * SCALE
            dv_acc[win, :] += _dot(p.astype(BF16), dyv, "tn")

        @pl.when(step == steps - 1)
        def _():
            dk_ref[...] = dk_acc[...].astype(BF16)
            dv_ref[...] = dv_acc[...].astype(BF16)

    blk_spec = pl.BlockSpec((BQ_B, HEAD_DIM), lambda h, r: (r, h))
    whole = pl.BlockSpec((T, HEAD_DIM), lambda h, r: (0, h))
    full = jax.ShapeDtypeStruct((T, N_HEADS_B * HEAD_DIM), BF16)
    return _pc(body, name="attn_b_bwd", grid=(N_HEADS_B, steps),
               in_specs=_b_specs(T) + [blk_spec, blk_spec, pl.BlockSpec((None, BQ_B, 1), lambda h, r: (h, r, 0))],
               out_specs=[blk_spec, whole, whole, pl.BlockSpec((None, N_DR, GRID_W, GRID_W), lambda h, r: (h, 0, 0, 0))],
               out_shape=[full, full, full, jax.ShapeDtypeStruct((N_HEADS_B, N_DR, GRID_W, GRID_W), F32)],
               scratch=[pltpu.VMEM((ROWS_B, GRID_W, WK_B), F32), pltpu.VMEM((ROWS_B, GRID_W, WK_B), F32),
                        pltpu.VMEM((T, HEAD_DIM), F32), pltpu.VMEM((T, HEAD_DIM), F32)],
               sem=("parallel", "arbitrary"))(qkv, qkv, qkv, tiles, dy, y, lse)


def _sds(shape, dtype):
    return jax.ShapeDtypeStruct(shape, dtype)


def _behind(value, token):
    return value if token is None else value + token[0, 0]


def _local_step(x, tgt, norm_mix, b_gate, rpb, norm_mlp, norm_final, late_weights, send):
    T, D = x.shape
    nt = T // TM
    PP, PA = ("parallel", "parallel", "arbitrary"), ("arbitrary", "arbitrary", "arbitrary")
    slopes = jnp.asarray(_alibi_slopes())
    sel_np, mask_np = _toeplitz_consts()
    sel, mask = jnp.asarray(sel_np), jnp.asarray(mask_np)

    def rows(w, tm=TM):
        return pl.BlockSpec((tm, w), lambda i, j, k: (i, j))

    def full_rows(tm, w=D):
        return pl.BlockSpec((tm, w), lambda i, j, k: (i, 0))

    def tall(w):
        return pl.BlockSpec((T, w), lambda j, i, k: (0, i))

    def tall_j(w):
        return pl.BlockSpec((T, w), lambda j, i, k: (0, j))

    rvec = pl.BlockSpec((TS, 1), lambda i, j, k: (i, 0))
    gvec = pl.BlockSpec((1, D), lambda i, j, k: (0, 0))

    h1, r1 = _rmsnorm_fwd(x, norm_mix, "rmsnorm_mix")
    (wq,) = late_weights("qkv", h1)

    def store_bf16(acc, ex, o, pids):
        o[0][...] = acc.astype(BF16)

    (qkv,) = _mm("qkv_proj", (T // TL, N_DEV // 2, 1),
                 [(h1, full_rows(TL), wq, pl.BlockSpec((2, D, 768), lambda i, j, k: (j, 0, 0)), "nn_shards", None)],
                 [], [(_sds((T, 3 * D), BF16), rows(1536, TL))], store_bf16, None, PP)

    def gate_epi(acc, ex, o, pids):
        o[0][...] = jax.nn.sigmoid(acc + ex[0][...]).astype(BF16)

    late_weights("attention", qkv)
    rpb_rows = jnp.pad(rpb.reshape(N_HEADS_B * N_DR, N_DC), ((0, 0), (0, 128 - N_DC)))
    tiles = _rpb_expand(rpb_rows, sel, mask).reshape(N_HEADS_B, N_DR, GRID_W, GRID_W)
    ya, lse_a = _dilated_fwd(qkv, slopes)
    yb, lse_b = _attn_b_fwd(qkv, tiles)

    wg, wpa, wpb, wo, token = late_weights("mixer", ya)
    (gates,) = _mm("gate_proj", (T // TL, N_DEV // 2, 1),
                   [(h1, full_rows(TL), wg, pl.BlockSpec((2, D, 512), lambda i, j, k: (j, 0, 0)), "nn_shards", None)],
                   [(_behind(b_gate, token), pl.BlockSpec((1, 1024), lambda i, j, k: (0, j)))],
                   [(_sds((T, 2 * D), BF16), rows(1024, TL))], gate_epi, None, PP)

    def merge_body(ya_ref, wa_ref, yb_ref, wb_ref, ga_ref, gb_ref, pa_ref, pb_ref, mg_ref):
        ya_v, yb_v = ya_ref[...], yb_ref[...]
        for s in range(N_DEV):
            cols = slice(s * 256, (s + 1) * 256)
            pa = _dot(ya_v, wa_ref[s], "nn")
            pb = _dot(yb_v, wb_ref[s], "nn")
            pa_ref[:, cols] = pa.astype(BF16)
            pb_ref[:, cols] = pb.astype(BF16)
            mg_ref[:, cols] = (ga_ref[:, cols] * pa + gb_ref[:, cols] * pb).astype(BF16)

    wo2 = wo.reshape(D, D)
    y_spec = pl.BlockSpec((TS, 512), lambda i: (i, 0))
    wp_spec = pl.BlockSpec((N_DEV, 512, 256), lambda i: (0, 0, 0))
    c_spec = pl.BlockSpec((TS, D), lambda i: (i, 0))
    pa, pb, merged = _pc(merge_body, name="merge_proj", grid=(T // TS,),
                         in_specs=[y_spec, wp_spec, y_spec, wp_spec, c_spec, pl.BlockSpec((TS, D), lambda i: (i, 1))],
                         out_specs=[c_spec, c_spec, c_spec],
                         out_shape=[_sds((T, D), BF16)] * 3, sem=("parallel",))(ya, wpa, yb, wpb, gates, gates)

    def add_res(acc, ex, o, pids):
        o[0][...] = ex[0][...] + acc

    (x2,) = _mm("out_proj", (T // TL, 4, 1),
                [(merged, full_rows(TL), wo2, pl.BlockSpec((D, 512), lambda i, j, k: (0, j)), "nn", None)],
                [(x, rows(512, TL))], [(_sds((T, D), F32), rows(512, TL))], add_res, None, PP)

    (token,) = late_weights("residual", x2)
    h2, r2 = _rmsnorm_fwd(x2, _behind(norm_mlp, token), "rmsnorm_mlp")
    (wu,) = late_weights("up", h2)

    def up_epi(acc, ex, o, pids):
        o[0][...] = acc.astype(BF16)
        r = jnp.maximum(acc, 0.0)
        o[1][...] = (r * r).astype(BF16)

    u, act = _mm("mlp_up", (T // TL, N_DEV, 1),
                 [(h2, full_rows(TL), wu, pl.BlockSpec((None, D, 1024), lambda i, j, k: (j, 0, 0)), "nn", None)],
                 [], [(_sds((T, D_FF), BF16), rows(1024, TL))] * 2, up_epi, None, PP)
    (wd,) = late_weights("down", act)
    wd2 = wd.reshape(D_FF, D)
    (x3,) = _mm("mlp_down", (nt, D // 512, 1),
                [(act, full_rows(TM, D_FF), wd2, pl.BlockSpec((D_FF, 512), lambda i, j, k: (0, j)), "nn", None)],
                [(x2, rows(512))], [(_sds((T, D), F32), rows(512))], add_res, None, PP)

    loss, dx3, dx3b, dg3 = _final_norm_loss(x3, norm_final.reshape(1, D), tgt)

    def du_epi(acc, ex, o, pids):
        o[0][...] = (acc * (2.0 * jnp.maximum(ex[0][...].astype(F32), 0.0))).astype(BF16)

    (du,) = _mm("mlp_down_dx", (T // TL, N_DEV, 1),
                [(dx3b, full_rows(TL), wd2, pl.BlockSpec((1024, D), lambda i, j, k: (j, 0)), "nt", None)],
                [(u, rows(1024, TL))], [(_sds((T, D_FF), BF16), rows(1024, TL))], du_epi, None, PP)
    (dwd,) = _mm("mlp_down_dw", (D // 1024, D_FF // 512, 1), [(act, tall(512), dx3b, tall_j(1024), "tn", None)],
                 [], [(_sds((D_FF, D), BF16), pl.BlockSpec((512, 1024), lambda j, i, k: (i, j)))], store_bf16, None, PP)
    dwd = dwd.reshape(N_DEV, D_FF // N_DEV, D)
    (dwu,) = _mm("mlp_up_dw", (N_DEV, D // 512, 1), [(h2, tall(512), du, tall_j(1024), "tn", None)],
                 [], [(_sds((N_DEV, D, 1024), BF16), pl.BlockSpec((None, 512, 1024), lambda j, i, k: (j, i, 0)))],
                 store_bf16, None, PP)

    def store_f32(acc, ex, o, pids):
        o[0][...] = acc

    out_ji = pl.BlockSpec((TM, 512), lambda j, i, k: (i, j))
    a_ji = lambda w: pl.BlockSpec((TM, w), lambda j, i, k: (i, 0))
    w_ji = lambda n: pl.BlockSpec((N_DEV, 512, n), lambda j, i, k: (0, j, 0))
    behind = lambda token: [] if token is None else [(token, pl.BlockSpec(memory_space=pl.ANY))]
    (dh2,) = _mm("mlp_up_dx", (D // 512, nt, 1), [(du, a_ji(D_FF), wu, w_ji(1024), "nt_shards", None)],
                 behind(send("mlp", dict(w_down=dwd, w_up=dwu))), [(_sds((T, D), F32), out_ji)], store_f32, None, PP)
    dx2, dx2b, dg2 = _rmsnorm_bwd("rmsnorm_mlp_bwd", dh2, x2, r2, norm_mlp, dx3, True)

    (dwo,) = _mm("out_proj_dw", (D // 1024, D // 512, 1), [(merged, tall(512), dx2b, tall_j(1024), "tn", None)],
                 [], [(_sds((D, D), BF16), pl.BlockSpec((512, 1024), lambda j, i, k: (i, j)))], store_bf16, None, PP)

    def dmerge_epi(acc, ex, o, pids):
        ga, gb, pa_v, pb_v = (e[...].astype(F32) for e in ex)
        o[0][...] = (acc * ga).astype(BF16)
        o[1][...] = (acc * gb).astype(BF16)
        dga = acc * pa_v * ga * (1.0 - ga)
        dgb = acc * pb_v * gb * (1.0 - gb)
        o[2][...] = dga.astype(BF16)
        o[3][...] = dgb.astype(BF16)
        sa = jnp.sum(dga, axis=0, keepdims=True)
        sb = jnp.sum(dgb, axis=0, keepdims=True)
        first = pids[1] == 0

        @pl.when(first)
        def _():
            o[4][...] = sa
            o[5][...] = sb

        @pl.when(jnp.logical_not(first))
        def _():
            o[4][...] += sa
            o[5][...] += sb

    cj = pl.BlockSpec((TL, 512), lambda j, i, k: (i, j))
    bj = pl.BlockSpec((1, 512), lambda j, i, k: (0, j))
    dpa, dpb, dga, dgb, dba, dbb = _mm(
        "out_proj_dx", (4, T // TL, 1),
        [(dx2b, pl.BlockSpec((TL, D), lambda j, i, k: (i, 0)), wo2, pl.BlockSpec((512, D), lambda j, i, k: (j, 0)), "nt", None)],
        [(gates, cj), (gates, pl.BlockSpec((TL, 512), lambda j, i, k: (i, 4 + j))), (pa, cj), (pb, cj)],
        [(_sds((T, D), BF16), cj)] * 4 + [(_sds((1, D), F32), bj)] * 2, dmerge_epi, None,
        ("parallel", "arbitrary", "arbitrary"))

    def store_shards(acc, ex, o, pids):
        for s in range(4):
            o[0][s] = acc[:, s * 256:(s + 1) * 256].astype(BF16)

    def proj_dw(name, y, dp):
        (dw,) = _mm(name, (2, 1, 1), [(y, tall(512), dp, tall_j(1024), "tn", None)],
                    [], [(_sds((N_DEV, 512, 256), BF16), pl.BlockSpec((4, 512, 256), lambda j, i, k: (j, 0, 0)))],
                    store_shards, None, PP)
        return dw

    def proj_dx(name, dp, w):
        (dy,) = _mm(name, (nt, 1, 1),
                    [(dp, full_rows(TM), w, pl.BlockSpec((N_DEV, 512, 256), lambda i, j, k: (0, 0, 0)), "nt_shards", None)],
                    [], [(_sds((T, 512), BF16), full_rows(TM, 512))], store_bf16, None, PP)
        return dy

    def in_dw(name, dact, n):
        def store_pair(acc, ex, o, pids):
            o[0][0] = acc[:, :n].astype(BF16)
            o[0][1] = acc[:, n:].astype(BF16)

        (dw,) = _mm(name, (N_DEV // 2, D // 512, 1), [(h1, tall(512), dact, tall_j(2 * n), "tn", None)],
                    [], [(_sds((N_DEV, D, n), BF16), pl.BlockSpec((2, 512, n), lambda j, i, k: (j, i, 0)))],
                    store_pair, None, PP)
        return dw

    dgate = jnp.concatenate([dga, dgb], axis=1)
    dwg = in_dw("gate_proj_dw", dgate, 512)
    dwpa = proj_dw("proj_a_dw", ya, dpa)
    dwpb = proj_dw("proj_b_dw", yb, dpb)
    token = send("proj", dict(w_gate=dwg, w_out=dwo.reshape(N_DEV, D // N_DEV, D), w_proj_a=dwpa, w_proj_b=dwpb))
    dya = proj_dx("proj_a_dx", dpa, wpa)
    dyb = proj_dx("proj_b_dx", dpb, wpb)

    dq_a, dk_a, dv_a = _dilated_bwd(qkv, _behind(slopes, token), dya, ya, lse_a)
    dq_b, dk_b, dv_b, dtiles = _attn_b_bwd(qkv, tiles, dyb, yb, lse_b)
    drpb = _rpb_reduce(dtiles.reshape(N_HEADS_B * N_DR, GRID_W * GRID_W), sel)[:, :N_DC]
    dqkv = jnp.concatenate([dq_a, dq_b, dk_a, dk_b, dv_a, dv_b], axis=1)
    dwq = in_dw("qkv_proj_dw", dqkv, 768)
    (dh1,) = _mm("in_proj_dx", (D // 512, nt, 1),
                 [(dqkv, a_ji(3 * D), wq, w_ji(768), "nt_shards", None), (dgate, a_ji(2 * D), wg, w_ji(512), "nt_shards", None)],
                 behind(send("in", dict(w_qkv=dwq))), [(_sds((T, D), F32), out_ji)], store_f32, None, PP)
    grad_x, dg1 = _rmsnorm_bwd("rmsnorm_mix_bwd", dh1, x, r1, norm_mix, dx2, False)

    small = dict(norm_mix=dg1, b_gate=jnp.concatenate([dba, dbb], axis=1), rpb=drpb, norm_mlp=dg2, norm_final=dg3)
    return loss, grad_x, small


BIG = ("w_qkv", "w_gate", "w_proj_a", "w_proj_b", "w_out", "w_up", "w_down")
SMALL = ("norm_mix", "b_gate", "rpb", "norm_mlp", "norm_final")
SMALL_ROWS = 96


def _place():
    return lax.axis_index("x"), lax.axis_index("y"), lax.axis_index("c")


def _peer(x, y, c, mask):
    return x ^ (mask >> 2), y ^ ((mask >> 1) & 1), c ^ (mask & 1)


_HBM = pl.BlockSpec(memory_space=pltpu.HBM)
_SEM = pl.BlockSpec(memory_space=pltpu.SEMAPHORE)
_ANY = pl.BlockSpec(memory_space=pl.ANY)
_EFFECT = pltpu.SideEffectType.DATAFLOW_SIDE_EFFECTING


def _hbm(a):
    return pltpu.with_memory_space_constraint(a, pltpu.HBM)


def _other_chips(x, y):
    return [(1 - x, y), (x, 1 - y), (1 - x, 1 - y)]


def _block(x, y, c):
    return 4 * x + 2 * y + c


def _rdma(src, dst, send_sem, recv_sem, to):
    return pltpu.make_async_remote_copy(src_ref=src, dst_ref=dst, send_sem=send_sem, recv_sem=recv_sem, device_id=to,
                                        device_id_type=MESH)


def _gather_start(name, shards):
    n = len(shards)

    def body(*refs):
        srcs, lands = refs[:n], refs[n:2 * n]
        outs = refs[2 * n:]
        send, recv_d2d, recv_ici, local = outs[:n], outs[n:2 * n], outs[2 * n:3 * n], outs[3 * n:4 * n]
        token_ref = outs[6 * n]
        x, y, c = _place()
        for a in range(n):
            mine = lands[a].at[_block(x, y, c)]
            _rdma(srcs[a], mine, send[a].at[0], recv_d2d[a].at[0], (x, y, 1 - c)).start()
            for j, chip in enumerate(_other_chips(x, y)):
                _rdma(srcs[a], mine, send[a].at[1 + j], recv_ici[a].at[j], (*chip, c)).start()
        for a in range(n):
            pltpu.make_async_copy(srcs[a], lands[a].at[_block(x, y, c)], local[a].at[0]).start()
        token_ref[...] = jnp.zeros_like(token_ref)

    lands = [_hbm(lax.empty((N_DEV,) + s.shape, s.dtype)) for s in shards]
    srcs = [_hbm(s) for s in shards]
    dma = pltpu.SemaphoreType.DMA
    res = pl.pallas_call(
        body, name=name,
        out_shape=[dma((4,))] * n + [dma((1,))] * n + [dma((3,))] * n + [dma((1,))] * n
        + [pltpu.HBM(s.shape, s.dtype) for s in srcs] + [pltpu.HBM(l.shape, l.dtype) for l in lands]
        + [jax.ShapeDtypeStruct((8, 128), F32)],
        in_specs=[_HBM] * (2 * n), out_specs=[_SEM] * (4 * n) + [_HBM] * (2 * n) + [pl.BlockSpec(memory_space=pltpu.VMEM)],
        input_output_aliases={i: 4 * n + i for i in range(2 * n)},
        compiler_params=pltpu.CompilerParams(has_side_effects=_EFFECT))(*srcs, *lands)
    keys = ("send", "recv_d2d", "recv_ici", "local", "src", "land")
    return [dict(zip(keys, (res[k * n + a] for k in range(6)))) for a in range(n)], res[6 * n]


def _gather_forward(name, handles, after):
    n = len(handles)

    def body(*refs):
        lands, recv_ici = refs[:n], refs[n:2 * n]
        outs = refs[2 * n + 1:]
        fwd_send, fwd_recv, token_ref = outs[n:2 * n], outs[2 * n:3 * n], outs[3 * n]
        x, y, c = _place()
        for a in range(n):
            for j, chip in enumerate(_other_chips(x, y)):
                blk = lands[a].at[_block(*chip, c)]
                _rdma(blk, blk, fwd_send[a].at[j], recv_ici[a].at[j], (*chip, c)).wait_recv()
                _rdma(blk, blk, fwd_send[a].at[j], fwd_recv[a].at[j], (x, y, 1 - c)).start()
        token_ref[...] = jnp.zeros_like(token_ref)

    dma = pltpu.SemaphoreType.DMA
    lands = [h["land"] for h in handles]
    res = pl.pallas_call(
        body, name=name,
        out_shape=[pltpu.HBM(l.shape, l.dtype) for l in lands] + [dma((3,))] * (2 * n) + [jax.ShapeDtypeStruct((8, 128), F32)],
        in_specs=[_HBM] * n + [_SEM] * n + [_ANY],
        out_specs=[_HBM] * n + [_SEM] * (2 * n) + [pl.BlockSpec(memory_space=pltpu.VMEM)],
        input_output_aliases={i: i for i in range(n)},
        compiler_params=pltpu.CompilerParams(has_side_effects=_EFFECT))(*lands, *[h["recv_ici"] for h in handles], after)
    return [dict(h, land=res[a], fwd_send=res[n + a], fwd_recv=res[2 * n + a]) for a, h in enumerate(handles)], res[3 * n]


def _gather_wait(name, handles, after):
    n = len(handles)

    def body(*refs):
        srcs, lands = refs[:n], refs[n:2 * n]
        send, recv_d2d, local, fwd_send, fwd_recv = (refs[(2 + k) * n:(3 + k) * n] for k in range(5))
        x, y, c = _place()
        sibling = (x, y, 1 - c)
        for a in range(n):
            mine = lands[a].at[_block(x, y, c)]
            _rdma(srcs[a], mine, send[a].at[0], recv_d2d[a].at[0], sibling).wait_send()
            _rdma(srcs[a], lands[a].at[_block(*sibling)], send[a].at[0], recv_d2d[a].at[0], sibling).wait_recv()
            pltpu.make_async_copy(srcs[a], mine, local[a].at[0]).wait()
            for j, chip in enumerate(_other_chips(x, y)):
                _rdma(srcs[a], mine, send[a].at[1 + j], fwd_recv[a].at[j], (*chip, c)).wait_send()
                blk = lands[a].at[_block(*chip, c)]
                _rdma(blk, blk, fwd_send[a].at[j], fwd_recv[a].at[j], sibling).wait_send()
                got = lands[a].at[_block(*chip, 1 - c)]
                _rdma(got, got, fwd_send[a].at[j], fwd_recv[a].at[j], sibling).wait_recv()

    srcs = [h["src"] for h in handles]
    lands = [h["land"] for h in handles]
    sems = [h[k] for k in ("send", "recv_d2d", "local", "fwd_send", "fwd_recv") for h in handles]
    res = pl.pallas_call(
        body, name=name, out_shape=[pltpu.HBM(s.shape, s.dtype) for s in srcs] + [pltpu.HBM(l.shape, l.dtype) for l in lands],
        in_specs=[_HBM] * (2 * n) + [_SEM] * (5 * n) + [_ANY], out_specs=[_HBM] * (2 * n),
        input_output_aliases={i: i for i in range(2 * n)},
        compiler_params=pltpu.CompilerParams(has_side_effects=_EFFECT))(*srcs, *lands, *sems, after)
    return list(res[n:])


def _gather_small(small):
    def body(in_ref, out_ref, send_sems, recv_sems, local_sem):
        x, y, c = _place()
        me = 4 * x + 2 * y + c
        copies = [pltpu.make_async_copy(in_ref, out_ref.at[me], local_sem)]
        for mask in range(1, N_DEV):
            copies.append(pltpu.make_async_remote_copy(
                src_ref=in_ref, dst_ref=out_ref.at[me], send_sem=send_sems.at[mask - 1],
                recv_sem=recv_sems.at[mask - 1], device_id=_peer(x, y, c, mask), device_id_type=MESH))
        for cp in copies:
            cp.start()
        for cp in copies:
            cp.wait()

    any_spec = pl.BlockSpec(memory_space=pl.ANY)
    return pl.pallas_call(
        body, name="gather_small", in_specs=[any_spec], out_specs=any_spec,
        out_shape=jax.ShapeDtypeStruct((N_DEV,) + small.shape, small.dtype),
        scratch_shapes=[pltpu.SemaphoreType.DMA((7,)), pltpu.SemaphoreType.DMA((7,)), pltpu.SemaphoreType.DMA])(small)


def _send_copy(src_refs, land_refs, send_sems, recv_sems, a, mask, scatter, x, y, c):
    px, py, pc = _peer(x, y, c, mask)
    src = src_refs[a].at[4 * px + 2 * py + pc] if scatter else src_refs[a]
    return pltpu.make_async_remote_copy(
        src_ref=src, dst_ref=land_refs[a].at[4 * x + 2 * y + c], send_sem=send_sems[a].at[mask - 1],
        recv_sem=recv_sems[a].at[mask - 1], device_id=(px, py, pc), device_id_type=MESH)


def _local_copy(src_refs, land_refs, local_sems, a, scatter, x, y, c):
    me = 4 * x + 2 * y + c
    return pltpu.make_async_copy(src_refs[a].at[me] if scatter else src_refs[a], land_refs[a].at[me], local_sems[a].at[0])


def _send_start(name, srcs, scatter, after=None):
    n = len(srcs)
    n_in = 2 * n + (after is not None)
    shapes = [s.shape[1:] if scatter else s.shape for s in srcs]

    def body(*refs):
        src_refs, land_refs = refs[:n], refs[n:2 * n]
        outs = refs[n_in:]
        send_sems, recv_sems, local_sems = outs[:n], outs[n:2 * n], outs[2 * n:3 * n]
        token_ref = outs[5 * n]
        x, y, c = _place()
        for a in range(n):
            for mask in range(1, N_DEV):
                _send_copy(src_refs, land_refs, send_sems, recv_sems, a, mask, scatter, x, y, c).start()
        for a in range(n):
            _local_copy(src_refs, land_refs, local_sems, a, scatter, x, y, c).start()
        token_ref[...] = jnp.zeros_like(token_ref)

    lands = [pltpu.with_memory_space_constraint(lax.empty((N_DEV,) + sh, s.dtype), pltpu.HBM) for sh, s in zip(shapes, srcs)]
    srcs = [pltpu.with_memory_space_constraint(s, pltpu.HBM) for s in srcs]
    res = pl.pallas_call(
        body, name=name,
        out_shape=[pltpu.SemaphoreType.DMA((7,))] * (2 * n) + [pltpu.SemaphoreType.DMA((1,))] * n
        + [pltpu.HBM(s.shape, s.dtype) for s in srcs] + [pltpu.HBM(l.shape, l.dtype) for l in lands]
        + [jax.ShapeDtypeStruct((8, 128), F32)],
        in_specs=[_HBM] * (2 * n) + [pl.BlockSpec(memory_space=pl.ANY)] * (n_in - 2 * n),
        out_specs=[_SEM] * (3 * n) + [_HBM] * (2 * n) + [pl.BlockSpec(memory_space=pltpu.VMEM)],
        input_output_aliases={i: 3 * n + i for i in range(2 * n)},
        compiler_params=pltpu.CompilerParams(has_side_effects=_EFFECT))(*srcs, *lands, *([] if after is None else [after]))
    handles = [tuple(res[k * n + a] for k in range(5)) for a in range(n)]
    return handles, res[5 * n]


def _send_wait(name, handles, scatter, after):
    n = len(handles)

    def body(*refs):
        src_refs, land_refs = refs[:n], refs[n:2 * n]
        send_sems, recv_sems, local_sems = refs[2 * n:3 * n], refs[3 * n:4 * n], refs[4 * n:5 * n]
        x, y, c = _place()
        for a in range(n):
            for mask in range(1, N_DEV):
                cp = _send_copy(src_refs, land_refs, send_sems, recv_sems, a, mask, scatter, x, y, c)
                cp.wait_send()
                cp.wait_recv()
            _local_copy(src_refs, land_refs, local_sems, a, scatter, x, y, c).wait()

    srcs = [h[3] for h in handles]
    lands = [h[4] for h in handles]
    res = pl.pallas_call(
        body, name=name, out_shape=[pltpu.HBM(s.shape, s.dtype) for s in srcs] + [pltpu.HBM(l.shape, l.dtype) for l in lands],
        in_specs=[_HBM] * (2 * n) + [_SEM] * (3 * n) + [pl.BlockSpec(memory_space=pl.ANY)], out_specs=[_HBM] * (2 * n),
        input_output_aliases={i: i for i in range(2 * n)},
        compiler_params=pltpu.CompilerParams(has_side_effects=_EFFECT))(
            *srcs, *lands, *[h[0] for h in handles], *[h[1] for h in handles], *[h[2] for h in handles], after)
    return list(res[n:])


def _adamw(name, parts, w, m, v):
    K, n = w.shape
    tr = min(K, 128)

    def body(p_ref, w_ref, m_ref, v_ref, g_ref, d_ref, nm_ref, nv_ref):
        g = p_ref[0].astype(F32)
        for i in range(1, N_DEV):
            g = g + p_ref[i].astype(F32)
        nm = ADAM_B1 * m_ref[...] + (1.0 - ADAM_B1) * g
        nv = ADAM_B2 * v_ref[...] + (1.0 - ADAM_B2) * jnp.square(g)
        m_hat = nm / (1.0 - ADAM_B1 ** ADAM_STEP)
        v_hat = nv / (1.0 - ADAM_B2 ** ADAM_STEP)
        g_ref[...] = g
        d_ref[...] = -ADAM_LR * (m_hat / (jnp.sqrt(v_hat) + ADAM_EPS) + ADAM_WD * w_ref[...])
        nm_ref[...] = nm
        nv_ref[...] = nv

    blk = pl.BlockSpec((tr, n), lambda i: (i, 0))
    return _pc(body, name=name, grid=(K // tr,), in_specs=[pl.BlockSpec((N_DEV, tr, n), lambda i: (0, i, 0)), blk, blk, blk],
               out_specs=[blk] * 4, out_shape=[jax.ShapeDtypeStruct((K, n), F32)] * 4, sem=("parallel",))(parts, w, m, v)


def _pack_small(vals):
    flat = jnp.concatenate([v.reshape(-1) for v in vals])
    return jnp.pad(flat, (0, SMALL_ROWS * 128 - flat.shape[0])).reshape(SMALL_ROWS, 128)


def _unpack_small(packed, shapes):
    flat, out, off = packed.reshape(-1), [], 0
    for s in shapes:
        size = int(np.prod(s))
        out.append(flat[off:off + size].reshape(s))
        off += size
    return out


def kernel(x, norm_mix, w_qkv, w_gate, b_gate, rpb, w_proj_a, w_proj_b, w_out, norm_mlp, w_up, w_down, norm_final, loss_target, m_norm_mix, m_w_qkv, m_w_gate, m_b_gate, m_rpb, m_w_proj_a, m_w_proj_b, m_w_out, m_norm_mlp, m_w_up, m_w_down, m_norm_final, v_norm_mix, v_w_qkv, v_w_gate, v_b_gate, v_rpb, v_w_proj_a, v_w_proj_b, v_w_out, v_norm_mlp, v_w_up, v_w_down, v_norm_final):
    w = dict(norm_mix=norm_mix, w_qkv=w_qkv, w_gate=w_gate, b_gate=b_gate, rpb=rpb, w_proj_a=w_proj_a, w_proj_b=w_proj_b,
             w_out=w_out, norm_mlp=norm_mlp, w_up=w_up, w_down=w_down, norm_final=norm_final)
    m = dict(norm_mix=m_norm_mix, w_qkv=m_w_qkv, w_gate=m_w_gate, b_gate=m_b_gate, rpb=m_rpb, w_proj_a=m_w_proj_a,
             w_proj_b=m_w_proj_b, w_out=m_w_out, norm_mlp=m_norm_mlp, w_up=m_w_up, w_down=m_w_down, norm_final=m_norm_final)
    v = dict(norm_mix=v_norm_mix, w_qkv=v_w_qkv, w_gate=v_w_gate, b_gate=v_b_gate, rpb=v_rpb, w_proj_a=v_w_proj_a,
             w_proj_b=v_w_proj_b, w_out=v_w_out, norm_mlp=v_norm_mlp, w_up=v_w_up, w_down=v_w_down, norm_final=v_norm_final)
    order = ("norm_mix", "w_qkv", "w_gate", "b_gate", "rpb", "w_proj_a", "w_proj_b", "w_out", "norm_mlp", "w_up",
             "w_down", "norm_final")

    shard = {n: w[n][0].astype(BF16) for n in BIG}
    stages = dict(qkv=("w_qkv",), gate=("w_gate", "w_proj_a", "w_proj_b", "w_out"), mlp=("w_up", "w_down"))
    names = [n for s in stages.values() for n in s]
    handles, token = _gather_start("gather_start", [shard[n] for n in names])
    handles = dict(zip(names, handles))

    def late_weights(stage, after):
        if stage == "qkv":
            return tuple(_gather_wait("gather_wait_qkv", _gather_forward("gather_forward_qkv", [handles["w_qkv"]], after)[0], after))
        if stage == "attention":
            handles["gate"], tok = _gather_forward("gather_forward_gate", [handles[n] for n in stages["gate"]], after)
            return (tok,)
        if stage == "mixer":
            handles["up"], tok = _gather_forward("gather_forward_up", [handles["w_up"]], after)
            return tuple(_gather_wait("gather_wait_gate", handles["gate"], after)) + (tok,)
        if stage == "residual":
            handles["down"], tok = _gather_forward("gather_forward_down", [handles["w_down"]], after)
            return (tok,)
        return tuple(_gather_wait("gather_wait_" + stage, handles[stage], after))

    sent = {}

    def send(stage, grads):
        handles, tok = _send_start("grads_start_" + stage, list(grads.values()), True)
        sent[stage] = (tuple(grads), handles)
        return tok

    loss, grad_x, small = _local_step(x[0], loss_target[0], _behind(norm_mix, token), b_gate, rpb[0], norm_mlp, norm_final,
                                      late_weights, send)

    grad, delta, new_m, new_v = {}, {}, {}, {}
    after = grad_x
    for stage in ("mlp", "proj", "in"):
        names, handles = sent[stage]
        for n, parts in zip(names, _send_wait("grads_wait_" + stage, handles, True, after)):
            res = _adamw("adamw_" + n, parts, w[n][0], m[n][0], v[n][0])
            grad[n], delta[n], new_m[n], new_v[n] = (r[None] for r in res)
            after = res[1]
    res = _adamw("adamw_small", _gather_small(_pack_small([small[n] for n in SMALL])),
                 _pack_small([w[n] for n in SMALL]), _pack_small([m[n] for n in SMALL]), _pack_small([v[n] for n in SMALL]))
    shapes = [w[n].shape for n in SMALL]
    for tree, packed in zip((grad, delta, new_m, new_v), res):
        for n, val in zip(SMALL, _unpack_small(packed, shapes)):
            tree[n] = val

    total = lax.psum(loss[0, 0], ("x", "y", "c"))
    return (total, grad_x[None], *[grad[n] for n in order], *[delta[n] for n in order],
            *[new_m[n] for n in order], *[new_v[n] for n in order])
```

```python
import numpy as np
import jax
import jax.numpy as jnp
from jax import lax
from jax.experimental import pallas as pl
from jax.experimental.pallas import tpu as pltpu

F32 = jnp.float32
BF16 = jnp.bfloat16
MESH = pl.DeviceIdType.MESH

D_MODEL = 2048
HEAD_DIM = 128
N_HEADS = 16
N_HEADS_A = 12
N_HEADS_B = 4
DILATIONS = (1, 4, 16)
HALF_WINDOW = 64
GRID_W = 64
NA_ROWS = 8
NA_COLS = 16
D_FF = 4 * D_MODEL
N_DEV = 8
EPS = 1e-6
NEG = -1e30
SCALE = HEAD_DIM ** -0.5
ADAM_LR, ADAM_B1, ADAM_B2, ADAM_EPS, ADAM_WD, ADAM_STEP = 0.001, 0.9, 0.999, 1e-08, 0.01, 10

VMEM_LIMIT = 56 * 1024 * 1024
TM = 512
TL = 1024
TS = 256
BQ_A = 128


def _pc(body, *, name, grid, in_specs, out_specs, out_shape, scratch=(), sem=None):
    return pl.pallas_call(
        body, name=name, grid=grid, in_specs=in_specs, out_specs=out_specs, out_shape=out_shape,
        scratch_shapes=list(scratch),
        compiler_params=pltpu.CompilerParams(dimension_semantics=sem, vmem_limit_bytes=VMEM_LIMIT))


_DOT_DIMS = {"nn": (((1,), (0,)), ((), ())), "nt": (((1,), (1,)), ((), ())), "tn": (((0,), (0,)), ((), ()))}


def _dot(a, b, mode, precision=None):
    if mode == "nn_shards":
        return jnp.concatenate([_dot(a, b[s], "nn") for s in range(b.shape[0])], axis=1)
    if mode == "nt_shards":
        n = b.shape[2]
        out = _dot(a[:, :n], b[0], "nt")
        for s in range(1, b.shape[0]):
            out = out + _dot(a[:, s * n:(s + 1) * n], b[s], "nt")
        return out
    return lax.dot_general(a, b, _DOT_DIMS[mode], preferred_element_type=F32, precision=precision)


def _mm(name, grid, pairs, extras, outs, epilogue, acc_shape, sem):
    n_pairs, n_extra, n_out = len(pairs), len(extras), len(outs)
    k_axis = len(grid) - 1
    nk = grid[k_axis]

    def body(*refs):
        pids = [pl.program_id(ax) for ax in range(len(grid))]
        ab = refs[:2 * n_pairs]
        ex = refs[2 * n_pairs:2 * n_pairs + n_extra]
        o = refs[2 * n_pairs + n_extra:2 * n_pairs + n_extra + n_out]
        part = None
        for p, pair in enumerate(pairs):
            a = ab[2 * p][...]
            if pair[5] is not None:
                a = pair[5](a)
            d = _dot(a, ab[2 * p + 1][...], pair[4])
            part = d if part is None else part + d
        if nk == 1:
            epilogue(part, ex, o, pids)
        else:
            acc = refs[-1]
            k = pids[k_axis]

            @pl.when(k == 0)
            def _():
                acc[...] = part

            @pl.when(k > 0)
            def _():
                acc[...] += part

            @pl.when(k == nk - 1)
            def _():
                epilogue(acc[...], ex, o, pids)

    operands, in_specs = [], []
    for a, a_spec, b, b_spec, _, _ in pairs:
        operands += [a, b]
        in_specs += [a_spec, b_spec]
    for e, e_spec in extras:
        operands.append(e)
        in_specs.append(e_spec)
    res = _pc(body, name=name, grid=grid, in_specs=in_specs, out_specs=[s for _, s in outs],
              out_shape=[o for o, _ in outs], scratch=[pltpu.VMEM(acc_shape, F32)] if nk > 1 else [], sem=sem)(*operands)
    return res


def _rmsnorm_fwd(x, g, name):
    T, D = x.shape

    def body(x_ref, g_ref, h_ref, r_ref):
        xv = x_ref[...]
        r = lax.rsqrt(jnp.mean(xv * xv, axis=-1, keepdims=True) + EPS)
        h_ref[...] = (xv * r * g_ref[...]).astype(BF16)
        r_ref[...] = r

    return _pc(body, name=name, grid=(T // TM,),
               in_specs=[pl.BlockSpec((TM, D), lambda i: (i, 0)), pl.BlockSpec((1, D), lambda i: (0, 0))],
               out_specs=[pl.BlockSpec((TM, D), lambda i: (i, 0)), pl.BlockSpec((TM, 1), lambda i: (i, 0))],
               out_shape=[jax.ShapeDtypeStruct((T, D), BF16), jax.ShapeDtypeStruct((T, 1), F32)],
               sem=("parallel",))(x, g)


def _rmsnorm_bwd_block(dh, x, r, g, dres):
    xh = x * r
    dxh = dh * g
    dx = dres + r * (dxh - xh * jnp.mean(dxh * xh, axis=-1, keepdims=True))
    return dx, jnp.sum(dh * xh, axis=0, keepdims=True)


def _rmsnorm_bwd(name, dh, x, r, g, dres, want_bf16):
    T, D = x.shape

    def body(dh_ref, x_ref, r_ref, g_ref, dres_ref, *outs):
        dx, dgp = _rmsnorm_bwd_block(dh_ref[...], x_ref[...], r_ref[...], g_ref[...], dres_ref[...])
        outs[0][...] = dx
        if want_bf16:
            outs[1][...] = dx.astype(BF16)
        first = pl.program_id(0) == 0

        @pl.when(first)
        def _():
            outs[-1][...] = dgp

        @pl.when(jnp.logical_not(first))
        def _():
            outs[-1][...] += dgp

    row = pl.BlockSpec((TS, D), lambda i: (i, 0))
    vec = pl.BlockSpec((1, D), lambda i: (0, 0))
    n_row = 2 if want_bf16 else 1
    return _pc(body, name=name, grid=(T // TS,), in_specs=[row, row, pl.BlockSpec((TS, 1), lambda i: (i, 0)), vec, row],
               out_specs=[row] * n_row + [vec],
               out_shape=[jax.ShapeDtypeStruct((T, D), F32)] + [jax.ShapeDtypeStruct((T, D), BF16)] * (n_row - 1)
               + [jax.ShapeDtypeStruct((1, D), F32)], sem=("arbitrary",))(dh, x, r, g, dres)


def _final_norm_loss(x3, g, tgt):
    T, D = x3.shape

    def body(x_ref, g_ref, t_ref, loss_ref, dx_ref, dxb_ref, dg_ref):
        i = pl.program_id(0)
        xv = x_ref[...]
        gv = g_ref[...]
        r = lax.rsqrt(jnp.mean(xv * xv, axis=-1, keepdims=True) + EPS)
        xh = xv * r
        e = xh * gv - t_ref[...]
        part = 0.5 * jnp.sum(jnp.mean(e * e, axis=-1, keepdims=True))
        dy = e * (1.0 / D)
        dxh = dy * gv
        dx = r * (dxh - xh * jnp.mean(dxh * xh, axis=-1, keepdims=True))
        dx_ref[...] = dx
        dxb_ref[...] = dx.astype(BF16)
        dgp = jnp.sum(dy * xh, axis=0, keepdims=True)

        @pl.when(i == 0)
        def _():
            loss_ref[...] = jnp.zeros(loss_ref.shape, F32) + part
            dg_ref[...] = dgp

        @pl.when(i > 0)
        def _():
            loss_ref[...] += part
            dg_ref[...] += dgp

    row = pl.BlockSpec((TM, D), lambda i: (i, 0))
    vec = pl.BlockSpec((1, D), lambda i: (0, 0))
    return _pc(body, name="final_norm_loss", grid=(T // TM,), in_specs=[row, vec, row],
               out_specs=[pl.BlockSpec((8, 128), lambda i: (0, 0)), row, row, vec],
               out_shape=[jax.ShapeDtypeStruct((8, 128), F32), jax.ShapeDtypeStruct((T, D), F32),
                          jax.ShapeDtypeStruct((T, D), BF16), jax.ShapeDtypeStruct((1, D), F32)],
               sem=("arbitrary",))(x3, g, tgt)


def _alibi_slopes():
    return np.asarray(2.0 ** (-8.0 * np.arange(1, N_HEADS_A + 1) / N_HEADS_A), dtype=np.float32)


def _a_windows(T):
    half = [HALF_WINDOW * d for d in DILATIONS]
    return half, [min(BQ_A + 2 * w, T) for w in half]


def _a_bias(slope, g, offset, wk, half):
    d = DILATIONS[g]
    dlt = offset + lax.broadcasted_iota(jnp.int32, (BQ_A, wk), 1) - lax.broadcasted_iota(jnp.int32, (BQ_A, wk), 0)
    ad = jnp.maximum(dlt, -dlt)
    ok = (ad <= half) & ((ad & (d - 1)) == 0)
    return jnp.where(ok, -slope * ad.astype(F32), NEG)


def _a_scores(q, kw, slope, g, offset, wk, half):
    return _dot(q, kw, "nt") * SCALE + _a_bias(slope, g, offset, wk, half)


N_JOINT = 2


SUB_J = 4
BS_J = SUB_J * BQ_A


def _a_qkv_specs(T):
    specs = []
    for g in range(N_JOINT):
        specs.append(pl.BlockSpec((BS_J, HEAD_DIM), lambda j, i, g=g: (i, 4 * g + j)))
        specs.append(pl.BlockSpec((T, HEAD_DIM), lambda j, i, g=g: (0, N_HEADS + 4 * g + j)))
        specs.append(pl.BlockSpec((T, HEAD_DIM), lambda j, i, g=g: (0, 2 * N_HEADS + 4 * g + j)))
    return specs


def _a_window(step, u, g, T, half, wks):
    t0 = (step * SUB_J + u) * BQ_A
    ws = pl.multiple_of(jnp.clip(t0 - half[g], 0, T - wks[g]), 64)
    return pl.ds(ws, wks[g]), ws - t0


def _attn_a_fwd(qkv, slopes, o_last, lse_last):
    T = qkv.shape[0]
    half, wks = _a_windows(T)

    def body(sl_ref, *refs):
        qkv_refs, (o_ref, l_ref, y_ref, lse_ref) = refs[:3 * N_JOINT], refs[3 * N_JOINT:]
        j = pl.program_id(0)
        for u in range(SUB_J):
            rows = slice(u * BQ_A, (u + 1) * BQ_A)
            ss, vws = [], []
            m = l_ref[rows, :]
            for g in range(N_JOINT):
                q_ref, k_ref, v_ref = qkv_refs[3 * g:3 * g + 3]
                win, offset = _a_window(pl.program_id(1), u, g, T, half, wks)
                ss.append(_a_scores(q_ref[rows, :], k_ref[win, :], sl_ref[4 * g + j], g, offset, wks[g], half[g]))
                vws.append(v_ref[win, :])
                m = jnp.maximum(m, jnp.max(ss[g], axis=-1, keepdims=True))
            den = jnp.exp(l_ref[rows, :] - m)
            acc = den * o_ref[rows, :].astype(F32)
            for g in range(N_JOINT):
                p = jnp.exp(ss[g] - m)
                den = den + jnp.sum(p, axis=-1, keepdims=True)
                acc = acc + _dot(p.astype(BF16), vws[g], "nn")
            y_ref[rows, :] = (acc / den).astype(BF16)
            lse_ref[rows, :] = m + jnp.log(den)

    blk = pl.BlockSpec((BS_J, HEAD_DIM), lambda j, i: (i, j))
    col = pl.BlockSpec((None, BS_J, 1), lambda j, i: (j, i, 0))
    return _pc(body, name="attn_a_fwd", grid=(4, T // BS_J),
               in_specs=[pl.BlockSpec(memory_space=pltpu.SMEM)] + _a_qkv_specs(T) + [blk, col], out_specs=[blk, col],
               out_shape=[jax.ShapeDtypeStruct((T, 4 * HEAD_DIM), BF16), jax.ShapeDtypeStruct((4, T, 1), F32)],
               sem=("parallel", "parallel"))(slopes, *([qkv] * (3 * N_JOINT)), o_last, lse_last)


def _attn_a_bwd(qkv, slopes, dy, y, lse):
    T = qkv.shape[0]
    steps = T // BS_J
    half, wks = _a_windows(T)

    def body(sl_ref, *refs):
        qkv_refs = refs[:3 * N_JOINT]
        dy_ref, y_ref, lse_ref, dq_ref, dk_ref, dv_ref, delta_ref, dk_acc, dv_acc = refs[3 * N_JOINT:]
        j = pl.program_id(0)
        step = pl.program_id(1)

        @pl.when(step == 0)
        def _():
            dk_acc[...] = jnp.zeros_like(dk_acc)
            dv_acc[...] = jnp.zeros_like(dv_acc)

        for u in range(SUB_J):
            rows = slice(u * BQ_A, (u + 1) * BQ_A)
            dyv = dy_ref[rows, :]
            delta = jnp.sum(dyv.astype(F32) * y_ref[rows, :].astype(F32), axis=-1, keepdims=True)
            delta_ref[rows, :] = delta
            lse_v = lse_ref[rows, :]
            for g in range(N_JOINT):
                q_ref, k_ref, v_ref = qkv_refs[3 * g:3 * g + 3]
                win, offset = _a_window(step, u, g, T, half, wks)
                qv = q_ref[rows, :]
                kw = k_ref[win, :]
                s = _a_scores(qv, kw, sl_ref[4 * g + j], g, offset, wks[g], half[g])
                p = jnp.exp(s - lse_v)
                dp = _dot(dyv, v_ref[win, :], "nt")
                ds = (p * (dp - delta)).astype(BF16)
                dq_ref[g, rows, :] = (_dot(ds, kw, "nn") * SCALE).astype(BF16)
                dk_acc[g, win, :] += _dot(ds, qv, "tn") * SCALE
                dv_acc[g, win, :] += _dot(p.astype(BF16), dyv, "tn")

        @pl.when(step == steps - 1)
        def _():
            dk_ref[...] = dk_acc[...].astype(BF16)
            dv_ref[...] = dv_acc[...].astype(BF16)

    blk = pl.BlockSpec((BS_J, HEAD_DIM), lambda j, i: (i, j))
    col = pl.BlockSpec((None, BS_J, 1), lambda j, i: (j, i, 0))
    whole = pl.BlockSpec((N_JOINT, T, HEAD_DIM), lambda j, i: (0, 0, j))
    full = jax.ShapeDtypeStruct((N_JOINT, T, 4 * HEAD_DIM), BF16)
    return _pc(body, name="attn_a_bwd", grid=(4, steps),
               in_specs=[pl.BlockSpec(memory_space=pltpu.SMEM)] + _a_qkv_specs(T) + [blk, blk, col],
               out_specs=[pl.BlockSpec((N_JOINT, BS_J, HEAD_DIM), lambda j, i: (0, i, j)), whole, whole, col],
               out_shape=[full, full, full, jax.ShapeDtypeStruct((4, T, 1), F32)],
               scratch=[pltpu.VMEM((N_JOINT, T, HEAD_DIM), F32), pltpu.VMEM((N_JOINT, T, HEAD_DIM), F32)],
               sem=("parallel", "arbitrary"))(slopes, *([qkv] * (3 * N_JOINT)), dy, y, lse)


SUB_A = 4


def _to_residue_major(a, d, axis=0):
    if d == 1:
        return a
    sh = a.shape
    a = a.reshape(sh[:axis] + (sh[axis] // d, d) + sh[axis + 1:])
    return jnp.swapaxes(a, axis, axis + 1).reshape(sh)


def _to_token_order(a, d, axis=0):
    if d == 1:
        return a
    sh = a.shape
    a = a.reshape(sh[:axis] + (d, sh[axis] // d) + sh[axis + 1:])
    return jnp.swapaxes(a, axis, axis + 1).reshape(sh)


def _band_bias(slope, offset, wk):
    dlt = offset + lax.broadcasted_iota(jnp.int32, (BQ_A, wk), 1) - lax.broadcasted_iota(jnp.int32, (BQ_A, wk), 0)
    ad = jnp.maximum(dlt, -dlt)
    return jnp.where(ad <= HALF_WINDOW, -slope * ad.astype(F32), NEG)


def _band_window(step, u, seg, wk):
    t0 = (step * SUB_A + u) * BQ_A
    lo = (t0 // seg) * seg
    ws = pl.multiple_of(jnp.clip(t0 - HALF_WINDOW, lo, lo + seg - wk), 64)
    return slice(u * BQ_A, (u + 1) * BQ_A), pl.ds(ws, wk), ws - t0


def _band_specs(T, cols):
    bs = SUB_A * BQ_A
    assert T % bs == 0 and T // max(DILATIONS) >= BQ_A, "a query sub-block must lie inside one segment"
    return [pl.BlockSpec(memory_space=pltpu.SMEM),
            pl.BlockSpec((bs, HEAD_DIM), lambda j, i: (i, cols[0] + j)),
            pl.BlockSpec((T, HEAD_DIM), lambda j, i: (0, cols[1] + j)),
            pl.BlockSpec((T, HEAD_DIM), lambda j, i: (0, cols[2] + j))]


def _banded_fwd(name, src, cols, seg, slopes):
    T = src.shape[0]
    bs = SUB_A * BQ_A
    wk = min(BQ_A + 2 * HALF_WINDOW, seg)

    def body(sl_ref, q_ref, k_ref, v_ref, o_ref, lse_ref):
        slope = sl_ref[pl.program_id(0)]
        for u in range(SUB_A):
            rows, win, offset = _band_window(pl.program_id(1), u, seg, wk)
            s = _dot(q_ref[rows, :], k_ref[win, :], "nt") * SCALE + _band_bias(slope, offset, wk)
            m = jnp.max(s, axis=-1, keepdims=True)
            p = jnp.exp(s - m)
            den = jnp.sum(p, axis=-1, keepdims=True)
            o_ref[rows, :] = (_dot(p.astype(BF16), v_ref[win, :], "nn") / den).astype(BF16)
            lse_ref[rows, :] = m + jnp.log(den)

    return _pc(body, name=name, grid=(4, T // bs), in_specs=_band_specs(T, cols),
               out_specs=[pl.BlockSpec((bs, HEAD_DIM), lambda j, i: (i, j)), pl.BlockSpec((None, bs, 1), lambda j, i: (j, i, 0))],
               out_shape=[jax.ShapeDtypeStruct((T, 4 * HEAD_DIM), BF16), jax.ShapeDtypeStruct((4, T, 1), F32)],
               sem=("parallel", "parallel"))(slopes, src, src, src)


def _banded_bwd(name, src, cols, seg, slopes, dy, delta_lse):
    T = src.shape[0]
    bs = SUB_A * BQ_A
    steps = T // bs
    wk = min(BQ_A + 2 * HALF_WINDOW, seg)

    def body(sl_ref, q_ref, k_ref, v_ref, dy_ref, delta_ref, lse_ref, dq_ref, dk_ref, dv_ref, dk_acc, dv_acc):
        slope = sl_ref[pl.program_id(0)]
        step = pl.program_id(1)

        @pl.when(step == 0)
        def _():
            dk_acc[...] = jnp.zeros_like(dk_acc)
            dv_acc[...] = jnp.zeros_like(dv_acc)

        for u in range(SUB_A):
            rows, win, offset = _band_window(step, u, seg, wk)
            qv, kw, dyv = q_ref[rows, :], k_ref[win, :], dy_ref[rows, :]
            s = _dot(qv, kw, "nt") * SCALE + _band_bias(slope, offset, wk)
            p = jnp.exp(s - lse_ref[rows, :])
            ds = (p * (_dot(dyv, v_ref[win, :], "nt") - delta_ref[rows, :])).astype(BF16)
            dq_ref[rows, :] = (_dot(ds, kw, "nn") * SCALE).astype(BF16)
            dk_acc[win, :] += _dot(ds, qv, "tn") * SCALE
            dv_acc[win, :] += _dot(p.astype(BF16), dyv, "tn")

        @pl.when(step == steps - 1)
        def _():
            dk_ref[...] = dk_acc[...].astype(BF16)
            dv_ref[...] = dv_acc[...].astype(BF16)

    blk = pl.BlockSpec((bs, HEAD_DIM), lambda j, i: (i, j))
    whole = pl.BlockSpec((T, HEAD_DIM), lambda j, i: (0, j))
    full = jax.ShapeDtypeStruct((T, 4 * HEAD_DIM), BF16)
    return _pc(body, name=name, grid=(4, steps),
               in_specs=_band_specs(T, cols) + [blk, pl.BlockSpec((None, bs, 1), lambda j, i: (j, i, 0)),
                                               pl.BlockSpec((None, bs, 1), lambda j, i: (4 + j, i, 0))],
               out_specs=[blk, whole, whole], out_shape=[full, full, full],
               scratch=[pltpu.VMEM((T, HEAD_DIM), F32), pltpu.VMEM((T, HEAD_DIM), F32)],
               sem=("parallel", "arbitrary"))(slopes, src, src, src, dy, delta_lse, delta_lse)


def _last_group_source(qkv):
    third = qkv.shape[1] // 3
    own = jnp.concatenate([qkv[:, t * third + 512 * N_JOINT:t * third + 512 * (N_JOINT + 1)] for t in range(3)], axis=1)
    return _to_residue_major(own, DILATIONS[N_JOINT])


def _dilated_fwd(qkv, slopes):
    d = DILATIONS[N_JOINT]
    o, lse = _banded_fwd("attn_a_last_fwd", _last_group_source(qkv), (0, 4, 8), qkv.shape[0] // d,
                         slopes[4 * N_JOINT:] * float(d))
    return _attn_a_fwd(qkv, slopes, _to_token_order(o, d), _to_token_order(lse, d, axis=1))


def _dilated_bwd(qkv, slopes, dy, y, lse):
    d = DILATIONS[N_JOINT]
    dq, dk, dv, delta = _attn_a_bwd(qkv, slopes, dy, y, lse)
    delta_lse = _to_residue_major(jnp.concatenate([delta, lse], axis=0), d, axis=1)
    last = _banded_bwd("attn_a_last_bwd", _last_group_source(qkv), (0, 4, 8), qkv.shape[0] // d,
                       slopes[4 * N_JOINT:] * float(d), _to_residue_major(dy, d), delta_lse)
    return [jnp.concatenate([joint[g] for g in range(N_JOINT)] + [_to_token_order(own, d)], axis=1)
            for joint, own in zip((dq, dk, dv), last)]


N_DR = 2 * NA_ROWS - 1
N_DC = 2 * NA_COLS - 1
WK_B = NA_ROWS * GRID_W


def _toeplitz_consts():
    col = np.arange(GRID_W)
    start = np.clip(col - NA_COLS // 2, 0, GRID_W - NA_COLS)
    ok = (col[None, :] >= start[:, None]) & (col[None, :] < start[:, None] + NA_COLS)
    dc = np.clip(col[None, :] - col[:, None], -(NA_COLS - 1), NA_COLS - 1) + NA_COLS - 1
    sel = np.zeros((128, GRID_W * GRID_W), np.float32)
    sel[dc.reshape(-1), np.arange(GRID_W * GRID_W)] = ok.reshape(-1).astype(np.float32)
    mask = np.where(ok, 0.0, NEG).astype(np.float32).reshape(1, -1)
    return sel, mask


def _rpb_expand(rpb_rows, sel, mask):
    def body(r_ref, s_ref, m_ref, o_ref):
        o_ref[...] = _dot(r_ref[...], s_ref[...], "nn", precision=lax.Precision.HIGHEST) + m_ref[...]

    n = rpb_rows.shape[0]
    return _pc(body, name="rpb_expand", grid=(1,),
               in_specs=[pl.BlockSpec((n, 128), lambda i: (0, 0)), pl.BlockSpec(sel.shape, lambda i: (0, 0)),
                         pl.BlockSpec(mask.shape, lambda i: (0, 0))],
               out_specs=pl.BlockSpec((n, sel.shape[1]), lambda i: (0, 0)),
               out_shape=jax.ShapeDtypeStruct((n, sel.shape[1]), F32), sem=("arbitrary",))(rpb_rows, sel, mask)


def _rpb_reduce(dtiles, sel):
    def body(t_ref, s_ref, o_ref):
        o_ref[...] = _dot(t_ref[...], s_ref[...], "nt", precision=lax.Precision.HIGHEST)

    n = dtiles.shape[0]
    return _pc(body, name="rpb_reduce", grid=(1,),
               in_specs=[pl.BlockSpec(dtiles.shape, lambda i: (0, 0)), pl.BlockSpec(sel.shape, lambda i: (0, 0))],
               out_specs=pl.BlockSpec((n, 128), lambda i: (0, 0)),
               out_shape=jax.ShapeDtypeStruct((n, 128), F32), sem=("arbitrary",))(dtiles, sel)


ROWS_B = 8
BQ_B = ROWS_B * GRID_W


def _b_specs(T):
    return [pl.BlockSpec((BQ_B, HEAD_DIM), lambda h, r: (r, N_HEADS_A + h)),
            pl.BlockSpec((T, HEAD_DIM), lambda h, r: (0, N_HEADS + N_HEADS_A + h)),
            pl.BlockSpec((T, HEAD_DIM), lambda h, r: (0, 2 * N_HEADS + N_HEADS_A + h)),
            pl.BlockSpec((None, N_DR, GRID_W, GRID_W), lambda h, r: (h, 0, 0, 0))]


def _b_row(step, u, rows, t_ref, bias_ref):
    r = step * ROWS_B + u
    rs = jnp.clip(r - NA_ROWS // 2, 0, rows - NA_ROWS)
    first_tile = rs - r + NA_ROWS - 1
    for a in range(NA_ROWS):
        bias_ref[u, :, a * GRID_W:(a + 1) * GRID_W] = t_ref[first_tile + a]
    return slice(u * GRID_W, (u + 1) * GRID_W), pl.ds(pl.multiple_of(rs * GRID_W, GRID_W), WK_B), first_tile


def _attn_b_fwd(qkv, tiles):
    T = qkv.shape[0]
    rows = T // GRID_W

    def body(q_ref, k_ref, v_ref, t_ref, y_ref, lse_ref, bias_ref):
        for u in range(ROWS_B):
            blk, win, _ = _b_row(pl.program_id(1), u, rows, t_ref, bias_ref)
            s = _dot(q_ref[blk, :], k_ref[win, :], "nt") * SCALE + bias_ref[u]
            m = jnp.max(s, axis=-1, keepdims=True)
            p = jnp.exp(s - m)
            den = jnp.sum(p, axis=-1, keepdims=True)
            y_ref[blk, :] = (_dot(p.astype(BF16), v_ref[win, :], "nn") / den).astype(BF16)
            lse_ref[blk, :] = m + jnp.log(den)

    return _pc(body, name="attn_b_fwd", grid=(N_HEADS_B, rows // ROWS_B), in_specs=_b_specs(T),
               out_specs=[pl.BlockSpec((BQ_B, HEAD_DIM), lambda h, r: (r, h)),
                          pl.BlockSpec((None, BQ_B, 1), lambda h, r: (h, r, 0))],
               out_shape=[jax.ShapeDtypeStruct((T, N_HEADS_B * HEAD_DIM), BF16),
                          jax.ShapeDtypeStruct((N_HEADS_B, T, 1), F32)],
               scratch=[pltpu.VMEM((ROWS_B, GRID_W, WK_B), F32)], sem=("parallel", "parallel"))(qkv, qkv, qkv, tiles)


def _attn_b_bwd(qkv, tiles, dy, y, lse):
    T = qkv.shape[0]
    rows = T // GRID_W
    steps = rows // ROWS_B

    def body(q_ref, k_ref, v_ref, t_ref, dy_ref, y_ref, lse_ref, dq_ref, dk_ref, dv_ref, dt_ref,
             bias_ref, ds_ref, dk_acc, dv_acc):
        step = pl.program_id(1)

        @pl.when(step == 0)
        def _():
            dk_acc[...] = jnp.zeros_like(dk_acc)
            dv_acc[...] = jnp.zeros_like(dv_acc)
            dt_ref[...] = jnp.zeros_like(dt_ref)

        for u in range(ROWS_B):
            blk, win, first_tile = _b_row(step, u, rows, t_ref, bias_ref)
            qv = q_ref[blk, :]
            kw = k_ref[win, :]
            dyv = dy_ref[blk, :]
            s = _dot(qv, kw, "nt") * SCALE + bias_ref[u]
            p = jnp.exp(s - lse_ref[blk, :])
            delta = jnp.sum(dyv.astype(F32) * y_ref[blk, :].astype(F32), axis=-1, keepdims=True)
            ds = p * (_dot(dyv, v_ref[win, :], "nt") - delta)
            ds_ref[u] = ds
            for a in range(NA_ROWS):
                dt_ref[first_tile + a] += ds_ref[u, :, a * GRID_W:(a + 1) * GRID_W]
            dsb = ds.astype(BF16)
            dq_ref[blk, :] = (_dot(dsb, kw, "nn") * SCALE).astype(BF16)
            dk_acc[win, :] += _dot(dsb, qv, "tn") * SCALE
            dv_acc[win, :] += _dot(p.astype(BF16), dyv, "tn")

        @pl.when(step == steps - 1)
        def _():
            dk_ref[...] = dk_acc[...].astype(BF16)
            dv_ref[...] = dv_acc[...].astype(BF16)

    blk_spec = pl.BlockSpec((BQ_B, HEAD_DIM), lambda h, r: (r, h))
    whole = pl.BlockSpec((T, HEAD_DIM), lambda h, r: (0, h))
    full = jax.ShapeDtypeStruct((T, N_HEADS_B * HEAD_DIM), BF16)
    return _pc(body, name="attn_b_bwd", grid=(N_HEADS_B, steps),
               in_specs=_b_specs(T) + [blk_spec, blk_spec, pl.BlockSpec((None, BQ_B, 1), lambda h, r: (h, r, 0))],
               out_specs=[blk_spec, whole, whole, pl.BlockSpec((None, N_DR, GRID_W, GRID_W), lambda h, r: (h, 0, 0, 0))],
               out_shape=[full, full, full, jax.ShapeDtypeStruct((N_HEADS_B, N_DR, GRID_W, GRID_W), F32)],
               scratch=[pltpu.VMEM((ROWS_B, GRID_W, WK_B), F32), pltpu.VMEM((ROWS_B, GRID_W, WK_B), F32),
                        pltpu.VMEM((T, HEAD_DIM), F32), pltpu.VMEM((T, HEAD_DIM), F32)],
               sem=("parallel", "arbitrary"))(qkv, qkv, qkv, tiles, dy, y, lse)


def _sds(shape, dtype):
    return jax.ShapeDtypeStruct(shape, dtype)


def _behind(value, token):
    return value if token is None else value + token[0, 0]


def _local_step(x, tgt, norm_mix, b_gate, rpb, norm_mlp, norm_final, late_weights, send):
    T, D = x.shape
    nt = T // TM
    PP, PA = ("parallel", "parallel", "arbitrary"), ("arbitrary", "arbitrary", "arbitrary")
    slopes = jnp.asarray(_alibi_slopes())
    sel_np, mask_np = _toeplitz_consts()
    sel, mask = jnp.asarray(sel_np), jnp.asarray(mask_np)

    def rows(w, tm=TM):
        return pl.BlockSpec((tm, w), lambda i, j, k: (i, j))

    def full_rows(tm, w=D):
        return pl.BlockSpec((tm, w), lambda i, j, k: (i, 0))

    def tall(w):
        return pl.BlockSpec((T, w), lambda j, i, k: (0, i))

    def tall_j(w):
        return pl.BlockSpec((T, w), lambda j, i, k: (0, j))

    rvec = pl.BlockSpec((TS, 1), lambda i, j, k: (i, 0))
    gvec = pl.BlockSpec((1, D), lambda i, j, k: (0, 0))

    h1, r1 = _rmsnorm_fwd(x, norm_mix, "rmsnorm_mix")
    (wq,) = late_weights("qkv", h1)

    def store_bf16(acc, ex, o, pids):
        o[0][...] = acc.astype(BF16)

    (qkv,) = _mm("qkv_proj", (T // TL, N_DEV // 2, 1),
                 [(h1, full_rows(TL), wq, pl.BlockSpec((2, D, 768), lambda i, j, k: (j, 0, 0)), "nn_shards", None)],
                 [], [(_sds((T, 3 * D), BF16), rows(1536, TL))], store_bf16, None, PP)

    def gate_epi(acc, ex, o, pids):
        o[0][...] = jax.nn.sigmoid(acc + ex[0][...]).astype(BF16)

    late_weights("attention", qkv)
    rpb_rows = jnp.pad(rpb.reshape(N_HEADS_B * N_DR, N_DC), ((0, 0), (0, 128 - N_DC)))
    tiles = _rpb_expand(rpb_rows, sel, mask).reshape(N_HEADS_B, N_DR, GRID_W, GRID_W)
    ya, lse_a = _dilated_fwd(qkv, slopes)
    yb, lse_b = _attn_b_fwd(qkv, tiles)

    wg, wpa, wpb, wo, token = late_weights("mixer", ya)
    (gates,) = _mm("gate_proj", (T // TL, N_DEV // 2, 1),
                   [(h1, full_rows(TL), wg, pl.BlockSpec((2, D, 512), lambda i, j, k: (j, 0, 0)), "nn_shards", None)],
                   [(_behind(b_gate, token), pl.BlockSpec((1, 1024), lambda i, j, k: (0, j)))],
                   [(_sds((T, 2 * D), BF16), rows(1024, TL))], gate_epi, None, PP)

    def merge_body(ya_ref, wa_ref, yb_ref, wb_ref, ga_ref, gb_ref, pa_ref, pb_ref, mg_ref):
        ya_v, yb_v = ya_ref[...], yb_ref[...]
        for s in range(N_DEV):
            cols = slice(s * 256, (s + 1) * 256)
            pa = _dot(ya_v, wa_ref[s], "nn")
            pb = _dot(yb_v, wb_ref[s], "nn")
            pa_ref[:, cols] = pa.astype(BF16)
            pb_ref[:, cols] = pb.astype(BF16)
            mg_ref[:, cols] = (ga_ref[:, cols] * pa + gb_ref[:, cols] * pb).astype(BF16)

    wo2 = wo.reshape(D, D)
    y_spec = pl.BlockSpec((TS, 512), lambda i: (i, 0))
    wp_spec = pl.BlockSpec((N_DEV, 512, 256), lambda i: (0, 0, 0))
    c_spec = pl.BlockSpec((TS, D), lambda i: (i, 0))
    pa, pb, merged = _pc(merge_body, name="merge_proj", grid=(T // TS,),
                         in_specs=[y_spec, wp_spec, y_spec, wp_spec, c_spec, pl.BlockSpec((TS, D), lambda i: (i, 1))],
                         out_specs=[c_spec, c_spec, c_spec],
                         out_shape=[_sds((T, D), BF16)] * 3, sem=("parallel",))(ya, wpa, yb, wpb, gates, gates)

    def add_res(acc, ex, o, pids):
        o[0][...] = ex[0][...] + acc

    (x2,) = _mm("out_proj", (T // TL, 4, 1),
                [(merged, full_rows(TL), wo2, pl.BlockSpec((D, 512), lambda i, j, k: (0, j)), "nn", None)],
                [(x, rows(512, TL))], [(_sds((T, D), F32), rows(512, TL))], add_res, None, PP)

    (token,) = late_weights("residual", x2)
    h2, r2 = _rmsnorm_fwd(x2, _behind(norm_mlp, token), "rmsnorm_mlp")
    (wu,) = late_weights("up", h2)

    def up_epi(acc, ex, o, pids):
        o[0][...] = acc.astype(BF16)
        r = jnp.maximum(acc, 0.0)
        o[1][...] = (r * r).astype(BF16)

    u, act = _mm("mlp_up", (T // TL, N_DEV, 1),
                 [(h2, full_rows(TL), wu, pl.BlockSpec((None, D, 1024), lambda i, j, k: (j, 0, 0)), "nn", None)],
                 [], [(_sds((T, D_FF), BF16), rows(1024, TL))] * 2, up_epi, None, PP)
    (wd,) = late_weights("down", act)
    wd2 = wd.reshape(D_FF, D)
    half_ji = pl.BlockSpec((TS, D // 2), lambda j, i, k: (i, j))
    (x3,) = _mm("mlp_down", (2, T // TS, 1),
                [(act, pl.BlockSpec((TS, D_FF), lambda j, i, k: (i, 0)), wd2,
                  pl.BlockSpec((D_FF, D // 2), lambda j, i, k: (0, j)), "nn", None)],
                [(x2, half_ji)], [(_sds((T, D), F32), half_ji)], add_res, None, PP)

    loss, dx3, dx3b, dg3 = _final_norm_loss(x3, norm_final.reshape(1, D), tgt)

    def du_epi(acc, ex, o, pids):
        o[0][...] = (acc * (2.0 * jnp.maximum(ex[0][...].astype(F32), 0.0))).astype(BF16)

    (du,) = _mm("mlp_down_dx", (T // TL, N_DEV, 1),
                [(dx3b, full_rows(TL), wd2, pl.BlockSpec((1024, D), lambda i, j, k: (j, 0)), "nt", None)],
                [(u, rows(1024, TL))], [(_sds((T, D_FF), BF16), rows(1024, TL))], du_epi, None, PP)
    (dwd,) = _mm("mlp_down_dw", (D // 1024, D_FF // 512, 1), [(act, tall(512), dx3b, tall_j(1024), "tn", None)],
                 [], [(_sds((D_FF, D), BF16), pl.BlockSpec((512, 1024), lambda j, i, k: (i, j)))], store_bf16, None, PP)
    dwd = dwd.reshape(N_DEV, D_FF // N_DEV, D)
    (dwu,) = _mm("mlp_up_dw", (N_DEV, D // 512, 1), [(h2, tall(512), du, tall_j(1024), "tn", None)],
                 [], [(_sds((N_DEV, D, 1024), BF16), pl.BlockSpec((None, 512, 1024), lambda j, i, k: (j, i, 0)))],
                 store_bf16, None, PP)

    def store_f32(acc, ex, o, pids):
        o[0][...] = acc

    out_ji = pl.BlockSpec((TM, 512), lambda j, i, k: (i, j))
    a_ji = lambda w: pl.BlockSpec((TM, w), lambda j, i, k: (i, 0))
    w_ji = lambda n: pl.BlockSpec((N_DEV, 512, n), lambda j, i, k: (0, j, 0))
    behind = lambda token: [] if token is None else [(token, pl.BlockSpec(memory_space=pl.ANY))]
    (dh2,) = _mm("mlp_up_dx", (D // 512, nt, 1), [(du, a_ji(D_FF), wu, w_ji(1024), "nt_shards", None)],
                 behind(send("mlp", dict(w_down=dwd, w_up=dwu))), [(_sds((T, D), F32), out_ji)], store_f32, None, PP)
    dx2, dx2b, dg2 = _rmsnorm_bwd("rmsnorm_mlp_bwd", dh2, x2, r2, norm_mlp, dx3, True)

    (dwo,) = _mm("out_proj_dw", (D // 1024, D // 512, 1), [(merged, tall(512), dx2b, tall_j(1024), "tn", None)],
                 [], [(_sds((D, D), BF16), pl.BlockSpec((512, 1024), lambda j, i, k: (i, j)))], store_bf16, None, PP)

    def dmerge_epi(acc, ex, o, pids):
        ga, gb, pa_v, pb_v = (e[...].astype(F32) for e in ex)
        o[0][...] = (acc * ga).astype(BF16)
        o[1][...] = (acc * gb).astype(BF16)
        dga = acc * pa_v * ga * (1.0 - ga)
        dgb = acc * pb_v * gb * (1.0 - gb)
        o[2][...] = dga.astype(BF16)
        o[3][...] = dgb.astype(BF16)
        sa = jnp.sum(dga, axis=0, keepdims=True)
        sb = jnp.sum(dgb, axis=0, keepdims=True)
        first = pids[1] == 0

        @pl.when(first)
        def _():
            o[4][...] = sa
            o[5][...] = sb

        @pl.when(jnp.logical_not(first))
        def _():
            o[4][...] += sa
            o[5][...] += sb

    cj = pl.BlockSpec((TL, 512), lambda j, i, k: (i, j))
    bj = pl.BlockSpec((1, 512), lambda j, i, k: (0, j))
    dpa, dpb, dga, dgb, dba, dbb = _mm(
        "out_proj_dx", (4, T // TL, 1),
        [(dx2b, pl.BlockSpec((TL, D), lambda j, i, k: (i, 0)), wo2, pl.BlockSpec((512, D), lambda j, i, k: (j, 0)), "nt", None)],
        [(gates, cj), (gates, pl.BlockSpec((TL, 512), lambda j, i, k: (i, 4 + j))), (pa, cj), (pb, cj)],
        [(_sds((T, D), BF16), cj)] * 4 + [(_sds((1, D), F32), bj)] * 2, dmerge_epi, None,
        ("parallel", "arbitrary", "arbitrary"))

    def store_shards(acc, ex, o, pids):
        for s in range(4):
            o[0][s] = acc[:, s * 256:(s + 1) * 256].astype(BF16)

    def proj_dw(name, y, dp):
        (dw,) = _mm(name, (2, 1, 1), [(y, tall(512), dp, tall_j(1024), "tn", None)],
                    [], [(_sds((N_DEV, 512, 256), BF16), pl.BlockSpec((4, 512, 256), lambda j, i, k: (j, 0, 0)))],
                    store_shards, None, PP)
        return dw

    def proj_dx(name, dp, w):
        (dy,) = _mm(name, (nt, 1, 1),
                    [(dp, full_rows(TM), w, pl.BlockSpec((N_DEV, 512, 256), lambda i, j, k: (0, 0, 0)), "nt_shards", None)],
                    [], [(_sds((T, 512), BF16), full_rows(TM, 512))], store_bf16, None, PP)
        return dy

    def in_dw(name, dact, n):
        def store_pair(acc, ex, o, pids):
            o[0][0] = acc[:, :n].astype(BF16)
            o[0][1] = acc[:, n:].astype(BF16)

        (dw,) = _mm(name, (N_DEV // 2, D // 512, 1), [(h1, tall(512), dact, tall_j(2 * n), "tn", None)],
                    [], [(_sds((N_DEV, D, n), BF16), pl.BlockSpec((2, 512, n), lambda j, i, k: (j, i, 0)))],
                    store_pair, None, PP)
        return dw

    dgate = jnp.concatenate([dga, dgb], axis=1)
    dwg = in_dw("gate_proj_dw", dgate, 512)
    dwpa = proj_dw("proj_a_dw", ya, dpa)
    dwpb = proj_dw("proj_b_dw", yb, dpb)
    token = send("proj", dict(w_gate=dwg, w_out=dwo.reshape(N_DEV, D // N_DEV, D), w_proj_a=dwpa, w_proj_b=dwpb))
    dya = proj_dx("proj_a_dx", dpa, wpa)
    dyb = proj_dx("proj_b_dx", dpb, wpb)

    dq_a, dk_a, dv_a = _dilated_bwd(qkv, _behind(slopes, token), dya, ya, lse_a)
    dq_b, dk_b, dv_b, dtiles = _attn_b_bwd(qkv, tiles, dyb, yb, lse_b)
    drpb = _rpb_reduce(dtiles.reshape(N_HEADS_B * N_DR, GRID_W * GRID_W), sel)[:, :N_DC]
    dqkv = jnp.concatenate([dq_a, dq_b, dk_a, dk_b, dv_a, dv_b], axis=1)
    dwq = in_dw("qkv_proj_dw", dqkv, 768)
    (dh1,) = _mm("in_proj_dx", (D // 512, nt, 1),
                 [(dqkv, a_ji(3 * D), wq, w_ji(768), "nt_shards", None), (dgate, a_ji(2 * D), wg, w_ji(512), "nt_shards", None)],
                 behind(send("in", dict(w_qkv=dwq))), [(_sds((T, D), F32), out_ji)], store_f32, None, PP)
    grad_x, dg1 = _rmsnorm_bwd("rmsnorm_mix_bwd", dh1, x, r1, norm_mix, dx2, False)

    small = dict(norm_mix=dg1, b_gate=jnp.concatenate([dba, dbb], axis=1), rpb=drpb, norm_mlp=dg2, norm_final=dg3)
    return loss, grad_x, small


BIG = ("w_qkv", "w_gate", "w_proj_a", "w_proj_b", "w_out", "w_up", "w_down")
SMALL = ("norm_mix", "b_gate", "rpb", "norm_mlp", "norm_final")
SMALL_ROWS = 96


def _place():
    return lax.axis_index("x"), lax.axis_index("y"), lax.axis_index("c")


def _peer(x, y, c, mask):
    return x ^ (mask >> 2), y ^ ((mask >> 1) & 1), c ^ (mask & 1)


_HBM = pl.BlockSpec(memory_space=pltpu.HBM)
_SEM = pl.BlockSpec(memory_space=pltpu.SEMAPHORE)
_ANY = pl.BlockSpec(memory_space=pl.ANY)
_EFFECT = pltpu.SideEffectType.DATAFLOW_SIDE_EFFECTING


def _hbm(a):
    return pltpu.with_memory_space_constraint(a, pltpu.HBM)


def _other_chips(x, y):
    return [(1 - x, y), (x, 1 - y), (1 - x, 1 - y)]


def _block(x, y, c):
    return 4 * x + 2 * y + c


def _rdma(src, dst, send_sem, recv_sem, to):
    return pltpu.make_async_remote_copy(src_ref=src, dst_ref=dst, send_sem=send_sem, recv_sem=recv_sem, device_id=to,
                                        device_id_type=MESH)


def _gather_start(name, shards):
    n = len(shards)

    def body(*refs):
        srcs, lands = refs[:n], refs[n:2 * n]
        outs = refs[2 * n:]
        send, recv_d2d, recv_ici, local = outs[:n], outs[n:2 * n], outs[2 * n:3 * n], outs[3 * n:4 * n]
        token_ref = outs[6 * n]
        x, y, c = _place()
        for a in range(n):
            mine = lands[a].at[_block(x, y, c)]
            _rdma(srcs[a], mine, send[a].at[0], recv_d2d[a].at[0], (x, y, 1 - c)).start()
            for j, chip in enumerate(_other_chips(x, y)):
                _rdma(srcs[a], mine, send[a].at[1 + j], recv_ici[a].at[j], (*chip, c)).start()
        for a in range(n):
            pltpu.make_async_copy(srcs[a], lands[a].at[_block(x, y, c)], local[a].at[0]).start()
        token_ref[...] = jnp.zeros_like(token_ref)

    lands = [_hbm(lax.empty((N_DEV,) + s.shape, s.dtype)) for s in shards]
    srcs = [_hbm(s) for s in shards]
    dma = pltpu.SemaphoreType.DMA
    res = pl.pallas_call(
        body, name=name,
        out_shape=[dma((4,))] * n + [dma((1,))] * n + [dma((3,))] * n + [dma((1,))] * n
        + [pltpu.HBM(s.shape, s.dtype) for s in srcs] + [pltpu.HBM(l.shape, l.dtype) for l in lands]
        + [jax.ShapeDtypeStruct((8, 128), F32)],
        in_specs=[_HBM] * (2 * n), out_specs=[_SEM] * (4 * n) + [_HBM] * (2 * n) + [pl.BlockSpec(memory_space=pltpu.VMEM)],
        input_output_aliases={i: 4 * n + i for i in range(2 * n)},
        compiler_params=pltpu.CompilerParams(has_side_effects=_EFFECT))(*srcs, *lands)
    keys = ("send", "recv_d2d", "recv_ici", "local", "src", "land")
    return [dict(zip(keys, (res[k * n + a] for k in range(6)))) for a in range(n)], res[6 * n]


def _gather_forward(name, handles, after):
    n = len(handles)

    def body(*refs):
        lands, recv_ici = refs[:n], refs[n:2 * n]
        outs = refs[2 * n + 1:]
        fwd_send, fwd_recv, token_ref = outs[n:2 * n], outs[2 * n:3 * n], outs[3 * n]
        x, y, c = _place()
        for a in range(n):
            for j, chip in enumerate(_other_chips(x, y)):
                blk = lands[a].at[_block(*chip, c)]
                _rdma(blk, blk, fwd_send[a].at[j], recv_ici[a].at[j], (*chip, c)).wait_recv()
                _rdma(blk, blk, fwd_send[a].at[j], fwd_recv[a].at[j], (x, y, 1 - c)).start()
        token_ref[...] = jnp.zeros_like(token_ref)

    dma = pltpu.SemaphoreType.DMA
    lands = [h["land"] for h in handles]
    res = pl.pallas_call(
        body, name=name,
        out_shape=[pltpu.HBM(l.shape, l.dtype) for l in lands] + [dma((3,))] * (2 * n) + [jax.ShapeDtypeStruct((8, 128), F32)],
        in_specs=[_HBM] * n + [_SEM] * n + [_ANY],
        out_specs=[_HBM] * n + [_SEM] * (2 * n) + [pl.BlockSpec(memory_space=pltpu.VMEM)],
        input_output_aliases={i: i for i in range(n)},
        compiler_params=pltpu.CompilerParams(has_side_effects=_EFFECT))(*lands, *[h["recv_ici"] for h in handles], after)
    return [dict(h, land=res[a], fwd_send=res[n + a], fwd_recv=res[2 * n + a]) for a, h in enumerate(handles)], res[3 * n]


def _gather_wait(name, handles, after):
    n = len(handles)

    def body(*refs):
        srcs, lands = refs[:n], refs[n:2 * n]
        send, recv_d2d, local, fwd_send, fwd_recv = (refs[(2 + k) * n:(3 + k) * n] for k in range(5))
        x, y, c = _place()
        sibling = (x, y, 1 - c)
        for a in range(n):
            mine = lands[a].at[_block(x, y, c)]
            _rdma(srcs[a], mine, send[a].at[0], recv_d2d[a].at[0], sibling).wait_send()
            _rdma(srcs[a], lands[a].at[_block(*sibling)], send[a].at[0], recv_d2d[a].at[0], sibling).wait_recv()
            pltpu.make_async_copy(srcs[a], mine, local[a].at[0]).wait()
            for j, chip in enumerate(_other_chips(x, y)):
                _rdma(srcs[a], mine, send[a].at[1 + j], fwd_recv[a].at[j], (*chip, c)).wait_send()
                blk = lands[a].at[_block(*chip, c)]
                _rdma(blk, blk, fwd_send[a].at[j], fwd_recv[a].at[j], sibling).wait_send()
                got = lands[a].at[_block(*chip, 1 - c)]
                _rdma(got, got, fwd_send[a].at[j], fwd_recv[a].at[j], sibling).wait_recv()

    srcs = [h["src"] for h in handles]
    lands = [h["land"] for h in handles]
    sems = [h[k] for k in ("send", "recv_d2d", "local", "fwd_send", "fwd_recv") for h in handles]
    res = pl.pallas_call(
        body, name=name, out_shape=[pltpu.HBM(s.shape, s.dtype) for s in srcs] + [pltpu.HBM(l.shape, l.dtype) for l in lands],
        in_specs=[_HBM] * (2 * n) + [_SEM] * (5 * n) + [_ANY], out_specs=[_HBM] * (2 * n),
        input_output_aliases={i: i for i in range(2 * n)},
        compiler_params=pltpu.CompilerParams(has_side_effects=_EFFECT))(*srcs, *lands, *sems, after)
    return list(res[n:])


def _gather_small(small):
    def body(in_ref, out_ref, send_sems, recv_sems, local_sem):
        x, y, c = _place()
        me = 4 * x + 2 * y + c
        copies = [pltpu.make_async_copy(in_ref, out_ref.at[me], local_sem)]
        for mask in range(1, N_DEV):
            copies.append(pltpu.make_async_remote_copy(
                src_ref=in_ref, dst_ref=out_ref.at[me], send_sem=send_sems.at[mask - 1],
                recv_sem=recv_sems.at[mask - 1], device_id=_peer(x, y, c, mask), device_id_type=MESH))
        for cp in copies:
            cp.start()
        for cp in copies:
            cp.wait()

    any_spec = pl.BlockSpec(memory_space=pl.ANY)
    return pl.pallas_call(
        body, name="gather_small", in_specs=[any_spec], out_specs=any_spec,
        out_shape=jax.ShapeDtypeStruct((N_DEV,) + small.shape, small.dtype),
        scratch_shapes=[pltpu.SemaphoreType.DMA((7,)), pltpu.SemaphoreType.DMA((7,)), pltpu.SemaphoreType.DMA])(small)


def _send_copy(src_refs, land_refs, send_sems, recv_sems, a, mask, scatter, x, y, c):
    px, py, pc = _peer(x, y, c, mask)
    src = src_refs[a].at[4 * px + 2 * py + pc] if scatter else src_refs[a]
    return pltpu.make_async_remote_copy(
        src_ref=src, dst_ref=land_refs[a].at[4 * x + 2 * y + c], send_sem=send_sems[a].at[mask - 1],
        recv_sem=recv_sems[a].at[mask - 1], device_id=(px, py, pc), device_id_type=MESH)


def _local_copy(src_refs, land_refs, local_sems, a, scatter, x, y, c):
    me = 4 * x + 2 * y + c
    return pltpu.make_async_copy(src_refs[a].at[me] if scatter else src_refs[a], land_refs[a].at[me], local_sems[a].at[0])


def _send_start(name, srcs, scatter, after=None):
    n = len(srcs)
    n_in = 2 * n + (after is not None)
    shapes = [s.shape[1:] if scatter else s.shape for s in srcs]

    def body(*refs):
        src_refs, land_refs = refs[:n], refs[n:2 * n]
        outs = refs[n_in:]
        send_sems, recv_sems, local_sems = outs[:n], outs[n:2 * n], outs[2 * n:3 * n]
        token_ref = outs[5 * n]
        x, y, c = _place()
        for a in range(n):
            for mask in range(1, N_DEV):
                _send_copy(src_refs, land_refs, send_sems, recv_sems, a, mask, scatter, x, y, c).start()
        for a in range(n):
            _local_copy(src_refs, land_refs, local_sems, a, scatter, x, y, c).start()
        token_ref[...] = jnp.zeros_like(token_ref)

    lands = [pltpu.with_memory_space_constraint(lax.empty((N_DEV,) + sh, s.dtype), pltpu.HBM) for sh, s in zip(shapes, srcs)]
    srcs = [pltpu.with_memory_space_constraint(s, pltpu.HBM) for s in srcs]
    res = pl.pallas_call(
        body, name=name,
        out_shape=[pltpu.SemaphoreType.DMA((7,))] * (2 * n) + [pltpu.SemaphoreType.DMA((1,))] * n
        + [pltpu.HBM(s.shape, s.dtype) for s in srcs] + [pltpu.HBM(l.shape, l.dtype) for l in lands]
        + [jax.ShapeDtypeStruct((8, 128), F32)],
        in_specs=[_HBM] * (2 * n) + [pl.BlockSpec(memory_space=pl.ANY)] * (n_in - 2 * n),
        out_specs=[_SEM] * (3 * n) + [_HBM] * (2 * n) + [pl.BlockSpec(memory_space=pltpu.VMEM)],
        input_output_aliases={i: 3 * n + i for i in range(2 * n)},
        compiler_params=pltpu.CompilerParams(has_side_effects=_EFFECT))(*srcs, *lands, *([] if after is None else [after]))
    handles = [tuple(res[k * n + a] for k in range(5)) for a in range(n)]
    return handles, res[5 * n]


def _send_wait(name, handles, scatter, after):
    n = len(handles)

    def body(*refs):
        src_refs, land_refs = refs[:n], refs[n:2 * n]
        send_sems, recv_sems, local_sems = refs[2 * n:3 * n], refs[3 * n:4 * n], refs[4 * n:5 * n]
        x, y, c = _place()
        for a in range(n):
            for mask in range(1, N_DEV):
                cp = _send_copy(src_refs, land_refs, send_sems, recv_sems, a, mask, scatter, x, y, c)
                cp.wait_send()
                cp.wait_recv()
            _local_copy(src_refs, land_refs, local_sems, a, scatter, x, y, c).wait()

    srcs = [h[3] for h in handles]
    lands = [h[4] for h in handles]
    res = pl.pallas_call(
        body, name=name, out_shape=[pltpu.HBM(s.shape, s.dtype) for s in srcs] + [pltpu.HBM(l.shape, l.dtype) for l in lands],
        in_specs=[_HBM] * (2 * n) + [_SEM] * (3 * n) + [pl.BlockSpec(memory_space=pl.ANY)], out_specs=[_HBM] * (2 * n),
        input_output_aliases={i: i for i in range(2 * n)},
        compiler_params=pltpu.CompilerParams(has_side_effects=_EFFECT))(
            *srcs, *lands, *[h[0] for h in handles], *[h[1] for h in handles], *[h[2] for h in handles], after)
    return list(res[n:])


def _adamw(name, parts, w, m, v):
    K, n = w.shape
    tr = min(K, 128)

    def body(p_ref, w_ref, m_ref, v_ref, g_ref, d_ref, nm_ref, nv_ref):
        g = p_ref[0].astype(F32)
        for i in range(1, N_DEV):
            g = g + p_ref[i].astype(F32)
        nm = ADAM_B1 * m_ref[...] + (1.0 - ADAM_B1) * g
        nv = ADAM_B2 * v_ref[...] + (1.0 - ADAM_B2) * jnp.square(g)
        m_hat = nm / (1.0 - ADAM_B1 ** ADAM_STEP)
        v_hat = nv / (1.0 - ADAM_B2 ** ADAM_STEP)
        g_ref[...] = g
        d_ref[...] = -ADAM_LR * (m_hat / (jnp.sqrt(v_hat) + ADAM_EPS) + ADAM_WD * w_ref[...])
        nm_ref[...] = nm
        nv_ref[...] = nv

    blk = pl.BlockSpec((tr, n), lambda i: (i, 0))
    return _pc(body, name=name, grid=(K // tr,), in_specs=[pl.BlockSpec((N_DEV, tr, n), lambda i: (0, i, 0)), blk, blk, blk],
               out_specs=[blk] * 4, out_shape=[jax.ShapeDtypeStruct((K, n), F32)] * 4, sem=("parallel",))(parts, w, m, v)


def _pack_small(vals):
    flat = jnp.concatenate([v.reshape(-1) for v in vals])
    return jnp.pad(flat, (0, SMALL_ROWS * 128 - flat.shape[0])).reshape(SMALL_ROWS, 128)


def _unpack_small(packed, shapes):
    flat, out, off = packed.reshape(-1), [], 0
    for s in shapes:
        size = int(np.prod(s))
        out.append(flat[off:off + size].reshape(s))
        off += size
    return out


def kernel(x, norm_mix, w_qkv, w_gate, b_gate, rpb, w_proj_a, w_proj_b, w_out, norm_mlp, w_up, w_down, norm_final, loss_target, m_norm_mix, m_w_qkv, m_w_gate, m_b_gate, m_rpb, m_w_proj_a, m_w_proj_b, m_w_out, m_norm_mlp, m_w_up, m_w_down, m_norm_final, v_norm_mix, v_w_qkv, v_w_gate, v_b_gate, v_rpb, v_w_proj_a, v_w_proj_b, v_w_out, v_norm_mlp, v_w_up, v_w_down, v_norm_final):
    w = dict(norm_mix=norm_mix, w_qkv=w_qkv, w_gate=w_gate, b_gate=b_gate, rpb=rpb, w_proj_a=w_proj_a, w_proj_b=w_proj_b,
             w_out=w_out, norm_mlp=norm_mlp, w_up=w_up, w_down=w_down, norm_final=norm_final)
    m = dict(norm_mix=m_norm_mix, w_qkv=m_w_qkv, w_gate=m_w_gate, b_gate=m_b_gate, rpb=m_rpb, w_proj_a=m_w_proj_a,
             w_proj_b=m_w_proj_b, w_out=m_w_out, norm_mlp=m_norm_mlp, w_up=m_w_up, w_down=m_w_down, norm_final=m_norm_final)
    v = dict(norm_mix=v_norm_mix, w_qkv=v_w_qkv, w_gate=v_w_gate, b_gate=v_b_gate, rpb=v_rpb, w_proj_a=v_w_proj_a,
             w_proj_b=v_w_proj_b, w_out=v_w_out, norm_mlp=v_norm_mlp, w_up=v_w_up, w_down=v_w_down, norm_final=v_norm_final)
    order = ("norm_mix", "w_qkv", "w_gate", "b_gate", "rpb", "w_proj_a", "w_proj_b", "w_out", "norm_mlp", "w_up",
             "w_down", "norm_final")

    shard = {n: w[n][0].astype(BF16) for n in BIG}
    stages = dict(qkv=("w_qkv",), gate=("w_gate", "w_proj_a", "w_proj_b", "w_out"), mlp=("w_up", "w_down"))
    names = [n for s in stages.values() for n in s]
    handles, token = _gather_start("gather_start", [shard[n] for n in names])
    handles = dict(zip(names, handles))

    def late_weights(stage, after):
        if stage == "qkv":
            return tuple(_gather_wait("gather_wait_qkv", _gather_forward("gather_forward_qkv", [handles["w_qkv"]], after)[0], after))
        if stage == "attention":
            handles["gate"], tok = _gather_forward("gather_forward_gate", [handles[n] for n in stages["gate"]], after)
            return (tok,)
        if stage == "mixer":
            handles["up"], tok = _gather_forward("gather_forward_up", [handles["w_up"]], after)
            return tuple(_gather_wait("gather_wait_gate", handles["gate"], after)) + (tok,)
        if stage == "residual":
            handles["down"], tok = _gather_forward("gather_forward_down", [handles["w_down"]], after)
            return (tok,)
        return tuple(_gather_wait("gather_wait_" + stage, handles[stage], after))

    sent = {}

    def send(stage, grads):
        handles, tok = _send_start("grads_start_" + stage, list(grads.values()), True)
        sent[stage] = (tuple(grads), handles)
        return tok

    loss, grad_x, small = _local_step(x[0], loss_target[0], _behind(norm_mix, token), b_gate, rpb[0], norm_mlp, norm_final,
                                      late_weights, send)

    grad, delta, new_m, new_v = {}, {}, {}, {}
    after = grad_x
    for stage in ("mlp", "proj", "in"):
        names, handles = sent[stage]
        for n, parts in zip(names, _send_wait("grads_wait_" + stage, handles, True, after)):
            res = _adamw("adamw_" + n, parts, w[n][0], m[n][0], v[n][0])
            grad[n], delta[n], new_m[n], new_v[n] = (r[None] for r in res)
            after = res[1]
    res = _adamw("adamw_small", _gather_small(_pack_small([small[n] for n in SMALL])),
                 _pack_small([w[n] for n in SMALL]), _pack_small([m[n] for n in SMALL]), _pack_small([v[n] for n in SMALL]))
    shapes = [w[n].shape for n in SMALL]
    for tree, packed in zip((grad, delta, new_m, new_v), res):
        for n, val in zip(SMALL, _unpack_small(packed, shapes)):
            tree[n] = val

    total = lax.psum(loss[0, 0], ("x", "y", "c"))
    return (total, grad_x[None], *[grad[n] for n in order], *[delta[n] for n in order],
            *[new_m[n] for n in order], *[new_v[n] for n in order])
```

```python
import numpy as np
import jax
import jax.numpy as jnp
from jax import lax
from jax.experimental import pallas as pl
from jax.experimental.pallas import tpu as pltpu

F32 = jnp.float32
BF16 = jnp.bfloat16
MESH = pl.DeviceIdType.MESH

D_MODEL = 2048
HEAD_DIM = 128
N_HEADS = 16
N_HEADS_A = 12
N_HEADS_B = 4
DILATIONS = (1, 4, 16)
HALF_WINDOW = 64
GRID_W = 64
NA_ROWS = 8
NA_COLS = 16
D_FF = 4 * D_MODEL
N_DEV = 8
EPS = 1e-6
NEG = -1e30
SCALE = HEAD_DIM ** -0.5
ADAM_LR, ADAM_B1, ADAM_B2, ADAM_EPS, ADAM_WD, ADAM_STEP = 0.001, 0.9, 0.999, 1e-08, 0.01, 10

VMEM_LIMIT = 56 * 1024 * 1024
TM = 512
TL = 1024
TS = 256
BQ_A = 128


def _pc(body, *, name, grid, in_specs, out_specs, out_shape, scratch=(), sem=None):
    return pl.pallas_call(
        body, name=name, grid=grid, in_specs=in_specs, out_specs=out_specs, out_shape=out_shape,
        scratch_shapes=list(scratch),
        compiler_params=pltpu.CompilerParams(dimension_semantics=sem, vmem_limit_bytes=VMEM_LIMIT))


_DOT_DIMS = {"nn": (((1,), (0,)), ((), ())), "nt": (((1,), (1,)), ((), ())), "tn": (((0,), (0,)), ((), ()))}


def _dot(a, b, mode, precision=None):
    if mode == "nn_shards":
        return jnp.concatenate([_dot(a, b[s], "nn") for s in range(b.shape[0])], axis=1)
    if mode == "nt_shards":
        n = b.shape[2]
        out = _dot(a[:, :n], b[0], "nt")
        for s in range(1, b.shape[0]):
            out = out + _dot(a[:, s * n:(s + 1) * n], b[s], "nt")
        return out
    return lax.dot_general(a, b, _DOT_DIMS[mode], preferred_element_type=F32, precision=precision)


def _mm(name, grid, pairs, extras, outs, epilogue, acc_shape, sem):
    n_pairs, n_extra, n_out = len(pairs), len(extras), len(outs)
    k_axis = len(grid) - 1
    nk = grid[k_axis]

    def body(*refs):
        pids = [pl.program_id(ax) for ax in range(len(grid))]
        ab = refs[:2 * n_pairs]
        ex = refs[2 * n_pairs:2 * n_pairs + n_extra]
        o = refs[2 * n_pairs + n_extra:2 * n_pairs + n_extra + n_out]
        part = None
        for p, pair in enumerate(pairs):
            a = ab[2 * p][...]
            if pair[5] is not None:
                a = pair[5](a)
            d = _dot(a, ab[2 * p + 1][...], pair[4])
            part = d if part is None else part + d
        if nk == 1:
            epilogue(part, ex, o, pids)
        else:
            acc = refs[-1]
            k = pids[k_axis]

            @pl.when(k == 0)
            def _():
                acc[...] = part

            @pl.when(k > 0)
            def _():
                acc[...] += part

            @pl.when(k == nk - 1)
            def _():
                epilogue(acc[...], ex, o, pids)

    operands, in_specs = [], []
    for a, a_spec, b, b_spec, _, _ in pairs:
        operands += [a, b]
        in_specs += [a_spec, b_spec]
    for e, e_spec in extras:
        operands.append(e)
        in_specs.append(e_spec)
    res = _pc(body, name=name, grid=grid, in_specs=in_specs, out_specs=[s for _, s in outs],
              out_shape=[o for o, _ in outs], scratch=[pltpu.VMEM(acc_shape, F32)] if nk > 1 else [], sem=sem)(*operands)
    return res


def _rmsnorm_fwd(x, g, name):
    T, D = x.shape

    def body(x_ref, g_ref, h_ref, r_ref):
        xv = x_ref[...]
        r = lax.rsqrt(jnp.mean(xv * xv, axis=-1, keepdims=True) + EPS)
        h_ref[...] = (xv * r * g_ref[...]).astype(BF16)
        r_ref[...] = r

    return _pc(body, name=name, grid=(T // TM,),
               in_specs=[pl.BlockSpec((TM, D), lambda i: (i, 0)), pl.BlockSpec((1, D), lambda i: (0, 0))],
               out_specs=[pl.BlockSpec((TM, D), lambda i: (i, 0)), pl.BlockSpec((TM, 1), lambda i: (i, 0))],
               out_shape=[jax.ShapeDtypeStruct((T, D), BF16), jax.ShapeDtypeStruct((T, 1), F32)],
               sem=("parallel",))(x, g)


def _rmsnorm_bwd_block(dh, x, r, g, dres):
    xh = x * r
    dxh = dh * g
    dx = dres + r * (dxh - xh * jnp.mean(dxh * xh, axis=-1, keepdims=True))
    return dx, jnp.sum(dh * xh, axis=0, keepdims=True)


def _rmsnorm_bwd(name, dh, x, r, g, dres, want_bf16):
    T, D = x.shape

    def body(dh_ref, x_ref, r_ref, g_ref, dres_ref, *outs):
        dx, dgp = _rmsnorm_bwd_block(dh_ref[...], x_ref[...], r_ref[...], g_ref[...], dres_ref[...])
        outs[0][...] = dx
        if want_bf16:
            outs[1][...] = dx.astype(BF16)
        first = pl.program_id(0) == 0

        @pl.when(first)
        def _():
            outs[-1][...] = dgp

        @pl.when(jnp.logical_not(first))
        def _():
            outs[-1][...] += dgp

    row = pl.BlockSpec((TS, D), lambda i: (i, 0))
    vec = pl.BlockSpec((1, D), lambda i: (0, 0))
    n_row = 2 if want_bf16 else 1
    return _pc(body, name=name, grid=(T // TS,), in_specs=[row, row, pl.BlockSpec((TS, 1), lambda i: (i, 0)), vec, row],
               out_specs=[row] * n_row + [vec],
               out_shape=[jax.ShapeDtypeStruct((T, D), F32)] + [jax.ShapeDtypeStruct((T, D), BF16)] * (n_row - 1)
               + [jax.ShapeDtypeStruct((1, D), F32)], sem=("arbitrary",))(dh, x, r, g, dres)


def _final_norm_loss(x3, g, tgt):
    T, D = x3.shape

    def body(x_ref, g_ref, t_ref, loss_ref, dx_ref, dxb_ref, dg_ref):
        i = pl.program_id(0)
        xv = x_ref[...]
        gv = g_ref[...]
        r = lax.rsqrt(jnp.mean(xv * xv, axis=-1, keepdims=True) + EPS)
        xh = xv * r
        e = xh * gv - t_ref[...]
        part = 0.5 * jnp.sum(jnp.mean(e * e, axis=-1, keepdims=True))
        dy = e * (1.0 / D)
        dxh = dy * gv
        dx = r * (dxh - xh * jnp.mean(dxh * xh, axis=-1, keepdims=True))
        dx_ref[...] = dx
        dxb_ref[...] = dx.astype(BF16)
        dgp = jnp.sum(dy * xh, axis=0, keepdims=True)

        @pl.when(i == 0)
        def _():
            loss_ref[...] = jnp.zeros(loss_ref.shape, F32) + part
            dg_ref[...] = dgp

        @pl.when(i > 0)
        def _():
            loss_ref[...] += part
            dg_ref[...] += dgp

    row = pl.BlockSpec((TM, D), lambda i: (i, 0))
    vec = pl.BlockSpec((1, D), lambda i: (0, 0))
    return _pc(body, name="final_norm_loss", grid=(T // TM,), in_specs=[row, vec, row],
               out_specs=[pl.BlockSpec((8, 128), lambda i: (0, 0)), row, row, vec],
               out_shape=[jax.ShapeDtypeStruct((8, 128), F32), jax.ShapeDtypeStruct((T, D), F32),
                          jax.ShapeDtypeStruct((T, D), BF16), jax.ShapeDtypeStruct((1, D), F32)],
               sem=("arbitrary",))(x3, g, tgt)


def _alibi_slopes():
    return np.asarray(2.0 ** (-8.0 * np.arange(1, N_HEADS_A + 1) / N_HEADS_A), dtype=np.float32)


def _a_windows(T):
    half = [HALF_WINDOW * d for d in DILATIONS]
    return half, [min(BQ_A + 2 * w, T) for w in half]


def _a_bias(slope, g, offset, wk, half):
    d = DILATIONS[g]
    dlt = offset + lax.broadcasted_iota(jnp.int32, (BQ_A, wk), 1) - lax.broadcasted_iota(jnp.int32, (BQ_A, wk), 0)
    ad = jnp.maximum(dlt, -dlt)
    ok = (ad <= half) & ((ad & (d - 1)) == 0)
    return jnp.where(ok, -slope * ad.astype(F32), NEG)


def _a_scores(q, kw, slope, g, offset, wk, half):
    return _dot(q, kw, "nt") * SCALE + _a_bias(slope, g, offset, wk, half)


N_JOINT = 2


SUB_J = 8
BS_J = SUB_J * BQ_A


def _a_qkv_specs(T):
    specs = []
    for g in range(N_JOINT):
        specs.append(pl.BlockSpec((BS_J, HEAD_DIM), lambda j, i, g=g: (i, 4 * g + j)))
        specs.append(pl.BlockSpec((T, HEAD_DIM), lambda j, i, g=g: (0, N_HEADS + 4 * g + j)))
        specs.append(pl.BlockSpec((T, HEAD_DIM), lambda j, i, g=g: (0, 2 * N_HEADS + 4 * g + j)))
    return specs


def _a_window(step, u, g, T, half, wks):
    t0 = (step * SUB_J + u) * BQ_A
    ws = pl.multiple_of(jnp.clip(t0 - half[g], 0, T - wks[g]), 64)
    return pl.ds(ws, wks[g]), ws - t0


def _attn_a_fwd(qkv, slopes, o_last, lse_last):
    T = qkv.shape[0]
    half, wks = _a_windows(T)

    def body(sl_ref, *refs):
        qkv_refs, (o_ref, l_ref, y_ref, lse_ref) = refs[:3 * N_JOINT], refs[3 * N_JOINT:]
        j = pl.program_id(0)
        for u in range(SUB_J):
            rows = slice(u * BQ_A, (u + 1) * BQ_A)
            ss, vws = [], []
            m = l_ref[rows, :]
            for g in range(N_JOINT):
                q_ref, k_ref, v_ref = qkv_refs[3 * g:3 * g + 3]
                win, offset = _a_window(pl.program_id(1), u, g, T, half, wks)
                ss.append(_a_scores(q_ref[rows, :], k_ref[win, :], sl_ref[4 * g + j], g, offset, wks[g], half[g]))
                vws.append(v_ref[win, :])
                m = jnp.maximum(m, jnp.max(ss[g], axis=-1, keepdims=True))
            den = jnp.exp(l_ref[rows, :] - m)
            acc = den * o_ref[rows, :].astype(F32)
            for g in range(N_JOINT):
                p = jnp.exp(ss[g] - m)
                den = den + jnp.sum(p, axis=-1, keepdims=True)
                acc = acc + _dot(p.astype(BF16), vws[g], "nn")
            y_ref[rows, :] = (acc / den).astype(BF16)
            lse_ref[rows, :] = m + jnp.log(den)

    blk = pl.BlockSpec((BS_J, HEAD_DIM), lambda j, i: (i, j))
    col = pl.BlockSpec((None, BS_J, 1), lambda j, i: (j, i, 0))
    return _pc(body, name="attn_a_fwd", grid=(4, T // BS_J),
               in_specs=[pl.BlockSpec(memory_space=pltpu.SMEM)] + _a_qkv_specs(T) + [blk, col], out_specs=[blk, col],
               out_shape=[jax.ShapeDtypeStruct((T, 4 * HEAD_DIM), BF16), jax.ShapeDtypeStruct((4, T, 1), F32)],
               sem=("parallel", "parallel"))(slopes, *([qkv] * (3 * N_JOINT)), o_last, lse_last)


def _attn_a_bwd(qkv, slopes, dy, y, lse):
    T = qkv.shape[0]
    steps = T // BS_J
    half, wks = _a_windows(T)

    def body(sl_ref, *refs):
        qkv_refs = refs[:3 * N_JOINT]
        dy_ref, y_ref, lse_ref, dq_ref, dk_ref, dv_ref, delta_ref, dk_acc, dv_acc = refs[3 * N_JOINT:]
        j = pl.program_id(0)
        step = pl.program_id(1)

        @pl.when(step == 0)
        def _():
            dk_acc[...] = jnp.zeros_like(dk_acc)
            dv_acc[...] = jnp.zeros_like(dv_acc)

        for u in range(SUB_J):
            rows = slice(u * BQ_A, (u + 1) * BQ_A)
            dyv = dy_ref[rows, :]
            delta = jnp.sum(dyv.astype(F32) * y_ref[rows, :].astype(F32), axis=-1, keepdims=True)
            delta_ref[rows, :] = delta
            lse_v = lse_ref[rows, :]
            for g in range(N_JOINT):
                q_ref, k_ref, v_ref = qkv_refs[3 * g:3 * g + 3]
                win, offset = _a_window(step, u, g, T, half, wks)
                qv = q_ref[rows, :]
                kw = k_ref[win, :]
                s = _a_scores(qv, kw, sl_ref[4 * g + j], g, offset, wks[g], half[g])
                p = jnp.exp(s - lse_v)
                dp = _dot(dyv, v_ref[win, :], "nt")
                ds = (p * (dp - delta)).astype(BF16)
                dq_ref[g, rows, :] = (_dot(ds, kw, "nn") * SCALE).astype(BF16)
                dk_acc[g, win, :] += _dot(ds, qv, "tn") * SCALE
                dv_acc[g, win, :] += _dot(p.astype(BF16), dyv, "tn")

        @pl.when(step == steps - 1)
        def _():
            dk_ref[...] = dk_acc[...].astype(BF16)
            dv_ref[...] = dv_acc[...].astype(BF16)

    blk = pl.BlockSpec((BS_J, HEAD_DIM), lambda j, i: (i, j))
    col = pl.BlockSpec((None, BS_J, 1), lambda j, i: (j, i, 0))
    whole = pl.BlockSpec((N_JOINT, T, HEAD_DIM), lambda j, i: (0, 0, j))
    full = jax.ShapeDtypeStruct((N_JOINT, T, 4 * HEAD_DIM), BF16)
    return _pc(body, name="attn_a_bwd", grid=(4, steps),
               in_specs=[pl.BlockSpec(memory_space=pltpu.SMEM)] + _a_qkv_specs(T) + [blk, blk, col],
               out_specs=[pl.BlockSpec((N_JOINT, BS_J, HEAD_DIM), lambda j, i: (0, i, j)), whole, whole, col],
               out_shape=[full, full, full, jax.ShapeDtypeStruct((4, T, 1), F32)],
               scratch=[pltpu.VMEM((N_JOINT, T, HEAD_DIM), F32), pltpu.VMEM((N_JOINT, T, HEAD_DIM), F32)],
               sem=("parallel", "arbitrary"))(slopes, *([qkv] * (3 * N_JOINT)), dy, y, lse)


SUB_A = 8


def _to_residue_major(a, d, axis=0):
    if d == 1:
        return a
    sh = a.shape
    a = a.reshape(sh[:axis] + (sh[axis] // d, d) + sh[axis + 1:])
    return jnp.swapaxes(a, axis, axis + 1).reshape(sh)


def _to_token_order(a, d, axis=0):
    if d == 1:
        return a
    sh = a.shape
    a = a.reshape(sh[:axis] + (d, sh[axis] // d) + sh[axis + 1:])
    return jnp.swapaxes(a, axis, axis + 1).reshape(sh)


def _band_bias(slope, offset, wk):
    dlt = offset + lax.broadcasted_iota(jnp.int32, (BQ_A, wk), 1) - lax.broadcasted_iota(jnp.int32, (BQ_A, wk), 0)
    ad = jnp.maximum(dlt, -dlt)
    return jnp.where(ad <= HALF_WINDOW, -slope * ad.astype(F32), NEG)


def _band_window(step, u, seg, wk):
    t0 = (step * SUB_A + u) * BQ_A
    lo = (t0 // seg) * seg
    ws = pl.multiple_of(jnp.clip(t0 - HALF_WINDOW, lo, lo + seg - wk), 64)
    return slice(u * BQ_A, (u + 1) * BQ_A), pl.ds(ws, wk), ws - t0


def _band_specs(T, cols):
    bs = SUB_A * BQ_A
    assert T % bs == 0 and T // max(DILATIONS) >= BQ_A, "a query sub-block must lie inside one segment"
    return [pl.BlockSpec(memory_space=pltpu.SMEM),
            pl.BlockSpec((bs, HEAD_DIM), lambda j, i: (i, cols[0] + j)),
            pl.BlockSpec((T, HEAD_DIM), lambda j, i: (0, cols[1] + j)),
            pl.BlockSpec((T, HEAD_DIM), lambda j, i: (0, cols[2] + j))]


def _banded_fwd(name, src, cols, seg, slopes):
    T = src.shape[0]
    bs = SUB_A * BQ_A
    wk = min(BQ_A + 2 * HALF_WINDOW, seg)

    def body(sl_ref, q_ref, k_ref, v_ref, o_ref, lse_ref):
        slope = sl_ref[pl.program_id(0)]
        for u in range(SUB_A):
            rows, win, offset = _band_window(pl.program_id(1), u, seg, wk)
            s = _dot(q_ref[rows, :], k_ref[win, :], "nt") * SCALE + _band_bias(slope, offset, wk)
            m = jnp.max(s, axis=-1, keepdims=True)
            p = jnp.exp(s - m)
            den = jnp.sum(p, axis=-1, keepdims=True)
            o_ref[rows, :] = (_dot(p.astype(BF16), v_ref[win, :], "nn") / den).astype(BF16)
            lse_ref[rows, :] = m + jnp.log(den)

    return _pc(body, name=name, grid=(4, T // bs), in_specs=_band_specs(T, cols),
               out_specs=[pl.BlockSpec((bs, HEAD_DIM), lambda j, i: (i, j)), pl.BlockSpec((None, bs, 1), lambda j, i: (j, i, 0))],
               out_shape=[jax.ShapeDtypeStruct((T, 4 * HEAD_DIM), BF16), jax.ShapeDtypeStruct((4, T, 1), F32)],
               sem=("parallel", "parallel"))(slopes, src, src, src)


def _banded_bwd(name, src, cols, seg, slopes, dy, delta_lse):
    T = src.shape[0]
    bs = SUB_A * BQ_A
    steps = T // bs
    wk = min(BQ_A + 2 * HALF_WINDOW, seg)

    def body(sl_ref, q_ref, k_ref, v_ref, dy_ref, delta_ref, lse_ref, dq_ref, dk_ref, dv_ref, dk_acc, dv_acc):
        slope = sl_ref[pl.program_id(0)]
        step = pl.program_id(1)

        @pl.when(step == 0)
        def _():
            dk_acc[...] = jnp.zeros_like(dk_acc)
            dv_acc[...] = jnp.zeros_like(dv_acc)

        for u in range(SUB_A):
            rows, win, offset = _band_window(step, u, seg, wk)
            qv, kw, dyv = q_ref[rows, :], k_ref[win, :], dy_ref[rows, :]
            s = _dot(qv, kw, "nt") * SCALE + _band_bias(slope, offset, wk)
            p = jnp.exp(s - lse_ref[rows, :])
            ds = (p * (_dot(dyv, v_ref[win, :], "nt") - delta_ref[rows, :])).astype(BF16)
            dq_ref[rows, :] = (_dot(ds, kw, "nn") * SCALE).astype(BF16)
            dk_acc[win, :] += _dot(ds, qv, "tn") * SCALE
            dv_acc[win, :] += _dot(p.astype(BF16), dyv, "tn")

        @pl.when(step == steps - 1)
        def _():
            dk_ref[...] = dk_acc[...].astype(BF16)
            dv_ref[...] = dv_acc[...].astype(BF16)

    blk = pl.BlockSpec((bs, HEAD_DIM), lambda j, i: (i, j))
    whole = pl.BlockSpec((T, HEAD_DIM), lambda j, i: (0, j))
    full = jax.ShapeDtypeStruct((T, 4 * HEAD_DIM), BF16)
    return _pc(body, name=name, grid=(4, steps),
               in_specs=_band_specs(T, cols) + [blk, pl.BlockSpec((None, bs, 1), lambda j, i: (j, i, 0)),
                                               pl.BlockSpec((None, bs, 1), lambda j, i: (4 + j, i, 0))],
               out_specs=[blk, whole, whole], out_shape=[full, full, full],
               scratch=[pltpu.VMEM((T, HEAD_DIM), F32), pltpu.VMEM((T, HEAD_DIM), F32)],
               sem=("parallel", "arbitrary"))(slopes, src, src, src, dy, delta_lse, delta_lse)


def _last_group_source(qkv):
    third = qkv.shape[1] // 3
    own = jnp.concatenate([qkv[:, t * third + 512 * N_JOINT:t * third + 512 * (N_JOINT + 1)] for t in range(3)], axis=1)
    return _to_residue_major(own, DILATIONS[N_JOINT])


def _dilated_fwd(qkv, slopes):
    d = DILATIONS[N_JOINT]
    o, lse = _banded_fwd("attn_a_last_fwd", _last_group_source(qkv), (0, 4, 8), qkv.shape[0] // d,
                         slopes[4 * N_JOINT:] * float(d))
    return _attn_a_fwd(qkv, slopes, _to_token_order(o, d), _to_token_order(lse, d, axis=1))


def _dilated_bwd(qkv, slopes, dy, y, lse):
    d = DILATIONS[N_JOINT]
    dq, dk, dv, delta = _attn_a_bwd(qkv, slopes, dy, y, lse)
    delta_lse = _to_residue_major(jnp.concatenate([delta, lse], axis=0), d, axis=1)
    last = _banded_bwd("attn_a_last_bwd", _last_group_source(qkv), (0, 4, 8), qkv.shape[0] // d,
                       slopes[4 * N_JOINT:] * float(d), _to_residue_major(dy, d), delta_lse)
    return [jnp.concatenate([joint[g] for g in range(N_JOINT)] + [_to_token_order(own, d)], axis=1)
            for joint, own in zip((dq, dk, dv), last)]


N_DR = 2 * NA_ROWS - 1
N_DC = 2 * NA_COLS - 1
WK_B = NA_ROWS * GRID_W


def _toeplitz_consts():
    col = np.arange(GRID_W)
    start = np.clip(col - NA_COLS // 2, 0, GRID_W - NA_COLS)
    ok = (col[None, :] >= start[:, None]) & (col[None, :] < start[:, None] + NA_COLS)
    dc = np.clip(col[None, :] - col[:, None], -(NA_COLS - 1), NA_COLS - 1) + NA_COLS - 1
    sel = np.zeros((128, GRID_W * GRID_W), np.float32)
    sel[dc.reshape(-1), np.arange(GRID_W * GRID_W)] = ok.reshape(-1).astype(np.float32)
    mask = np.where(ok, 0.0, NEG).astype(np.float32).reshape(1, -1)
    return sel, mask


def _rpb_expand(rpb_rows, sel, mask):
    def body(r_ref, s_ref, m_ref, o_ref):
        o_ref[...] = _dot(r_ref[...], s_ref[...], "nn", precision=lax.Precision.HIGHEST) + m_ref[...]

    n = rpb_rows.shape[0]
    return _pc(body, name="rpb_expand", grid=(1,),
               in_specs=[pl.BlockSpec((n, 128), lambda i: (0, 0)), pl.BlockSpec(sel.shape, lambda i: (0, 0)),
                         pl.BlockSpec(mask.shape, lambda i: (0, 0))],
               out_specs=pl.BlockSpec((n, sel.shape[1]), lambda i: (0, 0)),
               out_shape=jax.ShapeDtypeStruct((n, sel.shape[1]), F32), sem=("arbitrary",))(rpb_rows, sel, mask)


def _rpb_reduce(dtiles, sel):
    def body(t_ref, s_ref, o_ref):
        o_ref[...] = _dot(t_ref[...], s_ref[...], "nt", precision=lax.Precision.HIGHEST)

    n = dtiles.shape[0]
    return _pc(body, name="rpb_reduce", grid=(1,),
               in_specs=[pl.BlockSpec(dtiles.shape, lambda i: (0, 0)), pl.BlockSpec(sel.shape, lambda i: (0, 0))],
               out_specs=pl.BlockSpec((n, 128), lambda i: (0, 0)),
               out_shape=jax.ShapeDtypeStruct((n, 128), F32), sem=("arbitrary",))(dtiles, sel)


ROWS_B = 16
BQ_B = ROWS_B * GRID_W


def _b_specs(T):
    return [pl.BlockSpec((BQ_B, HEAD_DIM), lambda h, r: (r, N_HEADS_A + h)),
            pl.BlockSpec((T, HEAD_DIM), lambda h, r: (0, N_HEADS + N_HEADS_A + h)),
            pl.BlockSpec((T, HEAD_DIM), lambda h, r: (0, 2 * N_HEADS + N_HEADS_A + h)),
            pl.BlockSpec((None, N_DR, GRID_W, GRID_W), lambda h, r: (h, 0, 0, 0))]


def _b_row(step, u, rows, t_ref, bias_ref):
    r = step * ROWS_B + u
    rs = jnp.clip(r - NA_ROWS // 2, 0, rows - NA_ROWS)
    first_tile = rs - r + NA_ROWS - 1
    for a in range(NA_ROWS):
        bias_ref[u, :, a * GRID_W:(a + 1) * GRID_W] = t_ref[first_tile + a]
    return slice(u * GRID_W, (u + 1) * GRID_W), pl.ds(pl.multiple_of(rs * GRID_W, GRID_W), WK_B), first_tile


def _attn_b_fwd(qkv, tiles):
    T = qkv.shape[0]
    rows = T // GRID_W

    def body(q_ref, k_ref, v_ref, t_ref, y_ref, lse_ref, bias_ref):
        for u in range(ROWS_B):
            blk, win, _ = _b_row(pl.program_id(1), u, rows, t_ref, bias_ref)
            s = _dot(q_ref[blk, :], k_ref[win, :], "nt") * SCALE + bias_ref[u]
            m = jnp.max(s, axis=-1, keepdims=True)
            p = jnp.exp(s - m)
            den = jnp.sum(p, axis=-1, keepdims=True)
            y_ref[blk, :] = (_dot(p.astype(BF16), v_ref[win, :], "nn") / den).astype(BF16)
            lse_ref[blk, :] = m + jnp.log(den)

    return _pc(body, name="attn_b_fwd", grid=(N_HEADS_B, rows // ROWS_B), in_specs=_b_specs(T),
               out_specs=[pl.BlockSpec((BQ_B, HEAD_DIM), lambda h, r: (r, h)),
                          pl.BlockSpec((None, BQ_B, 1), lambda h, r: (h, r, 0))],
               out_shape=[jax.ShapeDtypeStruct((T, N_HEADS_B * HEAD_DIM), BF16),
                          jax.ShapeDtypeStruct((N_HEADS_B, T, 1), F32)],
               scratch=[pltpu.VMEM((ROWS_B, GRID_W, WK_B), F32)], sem=("parallel", "parallel"))(qkv, qkv, qkv, tiles)


def _attn_b_bwd(qkv, tiles, dy, y, lse):
    T = qkv.shape[0]
    rows = T // GRID_W
    steps = rows // ROWS_B

    def body(q_ref, k_ref, v_ref, t_ref, dy_ref, y_ref, lse_ref, dq_ref, dk_ref, dv_ref, dt_ref,
             bias_ref, ds_ref, dk_acc, dv_acc):
        step = pl.program_id(1)

        @pl.when(step == 0)
        def _():
            dk_acc[...] = jnp.zeros_like(dk_acc)
            dv_acc[...] = jnp.zeros_like(dv_acc)
            dt_ref[...] = jnp.zeros_like(dt_ref)

        for u in range(ROWS_B):
            blk, win, first_tile = _b_row(step, u, rows, t_ref, bias_ref)
            qv = q_ref[blk, :]
            kw = k_ref[win, :]
            dyv = dy_ref[blk, :]
            s = _dot(qv, kw, "nt") * SCALE + bias_ref[u]
            p = jnp.exp(s - lse_ref[blk, :])
            delta = jnp.sum(dyv.astype(F32) * y_ref[blk, :].astype(F32), axis=-1, keepdims=True)
            ds = p * (_dot(dyv, v_ref[win, :], "nt") - delta)
            ds_ref[u] = ds
            for a in range(NA_ROWS):
                dt_ref[first_tile + a] += ds_ref[u, :, a * GRID_W:(a + 1) * GRID_W]
            dsb = ds.astype(BF16)
            dq_ref[blk, :] = (_dot(dsb, kw, "nn") * SCALE).astype(BF16)
            dk_acc[win, :] += _dot(dsb, qv, "tn") * SCALE
            dv_acc[win, :] += _dot(p.astype(BF16), dyv, "tn")

        @pl.when(step == steps - 1)
        def _():
            dk_ref[...] = dk_acc[...].astype(BF16)
            dv_ref[...] = dv_acc[...].astype(BF16)

    blk_spec = pl.BlockSpec((BQ_B, HEAD_DIM), lambda h, r: (r, h))
    whole = pl.BlockSpec((T, HEAD_DIM), lambda h, r: (0, h))
    full = jax.ShapeDtypeStruct((T, N_HEADS_B * HEAD_DIM), BF16)
    return _pc(body, name="attn_b_bwd", grid=(N_HEADS_B, steps),
               in_specs=_b_specs(T) + [blk_spec, blk_spec, pl.BlockSpec((None, BQ_B, 1), lambda h, r: (h, r, 0))],
               out_specs=[blk_spec, whole, whole, pl.BlockSpec((None, N_DR, GRID_W, GRID_W), lambda h, r: (h, 0, 0, 0))],
               out_shape=[full, full, full, jax.ShapeDtypeStruct((N_HEADS_B, N_DR, GRID_W, GRID_W), F32)],
               scratch=[pltpu.VMEM((ROWS_B, GRID_W, WK_B), F32), pltpu.VMEM((ROWS_B, GRID_W, WK_B), F32),
                        pltpu.VMEM((T, HEAD_DIM), F32), pltpu.VMEM((T, HEAD_DIM), F32)],
               sem=("parallel", "arbitrary"))(qkv, qkv, qkv, tiles, dy, y, lse)


def _sds(shape, dtype):
    return jax.ShapeDtypeStruct(shape, dtype)


def _behind(value, token):
    return value if token is None else value + token[0, 0]


def _local_step(x, tgt, norm_mix, b_gate, rpb, norm_mlp, norm_final, late_weights, send):
    T, D = x.shape
    nt = T // TM
    PP, PA = ("parallel", "parallel", "arbitrary"), ("arbitrary", "arbitrary", "arbitrary")
    slopes = jnp.asarray(_alibi_slopes())
    sel_np, mask_np = _toeplitz_consts()
    sel, mask = jnp.asarray(sel_np), jnp.asarray(mask_np)

    def rows(w, tm=TM):
        return pl.BlockSpec((tm, w), lambda i, j, k: (i, j))

    def full_rows(tm, w=D):
        return pl.BlockSpec((tm, w), lambda i, j, k: (i, 0))

    def tall(w):
        return pl.BlockSpec((T, w), lambda j, i, k: (0, i))

    def tall_j(w):
        return pl.BlockSpec((T, w), lambda j, i, k: (0, j))

    rvec = pl.BlockSpec((TS, 1), lambda i, j, k: (i, 0))
    gvec = pl.BlockSpec((1, D), lambda i, j, k: (0, 0))

    h1, r1 = _rmsnorm_fwd(x, norm_mix, "rmsnorm_mix")
    (wq,) = late_weights("qkv", h1)

    def store_bf16(acc, ex, o, pids):
        o[0][...] = acc.astype(BF16)

    (qkv,) = _mm("qkv_proj", (T // TL, N_DEV // 2, 1),
                 [(h1, full_rows(TL), wq, pl.BlockSpec((2, D, 768), lambda i, j, k: (j, 0, 0)), "nn_shards", None)],
                 [], [(_sds((T, 3 * D), BF16), rows(1536, TL))], store_bf16, None, PP)

    def gate_epi(acc, ex, o, pids):
        o[0][...] = jax.nn.sigmoid(acc + ex[0][...]).astype(BF16)

    late_weights("attention", qkv)
    rpb_rows = jnp.pad(rpb.reshape(N_HEADS_B * N_DR, N_DC), ((0, 0), (0, 128 - N_DC)))
    tiles = _rpb_expand(rpb_rows, sel, mask).reshape(N_HEADS_B, N_DR, GRID_W, GRID_W)
    ya, lse_a = _dilated_fwd(qkv, slopes)
    yb, lse_b = _attn_b_fwd(qkv, tiles)

    wg, wpa, wpb, wo, token = late_weights("mixer", ya)
    (gates,) = _mm("gate_proj", (T // TL, N_DEV // 2, 1),
                   [(h1, full_rows(TL), wg, pl.BlockSpec((2, D, 512), lambda i, j, k: (j, 0, 0)), "nn_shards", None)],
                   [(_behind(b_gate, token), pl.BlockSpec((1, 1024), lambda i, j, k: (0, j)))],
                   [(_sds((T, 2 * D), BF16), rows(1024, TL))], gate_epi, None, PP)

    def merge_body(ya_ref, wa_ref, yb_ref, wb_ref, ga_ref, gb_ref, pa_ref, pb_ref, mg_ref):
        ya_v, yb_v = ya_ref[...], yb_ref[...]
        for s in range(N_DEV):
            cols = slice(s * 256, (s + 1) * 256)
            pa = _dot(ya_v, wa_ref[s], "nn")
            pb = _dot(yb_v, wb_ref[s], "nn")
            pa_ref[:, cols] = pa.astype(BF16)
            pb_ref[:, cols] = pb.astype(BF16)
            mg_ref[:, cols] = (ga_ref[:, cols] * pa + gb_ref[:, cols] * pb).astype(BF16)

    wo2 = wo.reshape(D, D)
    y_spec = pl.BlockSpec((TS, 512), lambda i: (i, 0))
    wp_spec = pl.BlockSpec((N_DEV, 512, 256), lambda i: (0, 0, 0))
    c_spec = pl.BlockSpec((TS, D), lambda i: (i, 0))
    pa, pb, merged = _pc(merge_body, name="merge_proj", grid=(T // TS,),
                         in_specs=[y_spec, wp_spec, y_spec, wp_spec, c_spec, pl.BlockSpec((TS, D), lambda i: (i, 1))],
                         out_specs=[c_spec, c_spec, c_spec],
                         out_shape=[_sds((T, D), BF16)] * 3, sem=("parallel",))(ya, wpa, yb, wpb, gates, gates)

    def add_res(acc, ex, o, pids):
        o[0][...] = ex[0][...] + acc

    def out_norm_epi(acc, ex, o, pids):
        x2v = ex[0][...] + acc
        r = lax.rsqrt(jnp.mean(x2v * x2v, axis=-1, keepdims=True) + EPS)
        o[0][...] = x2v
        o[1][...] = (x2v * r * ex[1][...]).astype(BF16)
        o[2][...] = r

    x2, h2, r2 = _mm("out_proj", (nt, 1, 1),
                     [(merged, full_rows(TM), wo2, pl.BlockSpec((D, D), lambda i, j, k: (0, 0)), "nn", None)],
                     [(x, full_rows(TM)), (norm_mlp, gvec)],
                     [(_sds((T, D), F32), full_rows(TM)), (_sds((T, D), BF16), full_rows(TM)),
                      (_sds((T, 1), F32), pl.BlockSpec((TM, 1), lambda i, j, k: (i, 0)))], out_norm_epi, None, PP)

    (token,) = late_weights("residual", x2)
    (wu,) = late_weights("up", h2)
    behind = lambda token: [] if token is None else [(token, pl.BlockSpec(memory_space=pl.ANY))]

    def up_epi(acc, ex, o, pids):
        o[0][...] = acc.astype(BF16)
        r = jnp.maximum(acc, 0.0)
        o[1][...] = (r * r).astype(BF16)

    u, act = _mm("mlp_up", (T // TL, N_DEV, 1),
                 [(h2, full_rows(TL), wu, pl.BlockSpec((None, D, 1024), lambda i, j, k: (j, 0, 0)), "nn", None)],
                 behind(token), [(_sds((T, D_FF), BF16), rows(1024, TL))] * 2, up_epi, None, PP)
    (wd,) = late_weights("down", act)
    wd2 = wd.reshape(D_FF, D)
    half_ji = pl.BlockSpec((TS, D // 2), lambda j, i, k: (i, j))
    (x3,) = _mm("mlp_down", (2, T // TS, 1),
                [(act, pl.BlockSpec((TS, D_FF), lambda j, i, k: (i, 0)), wd2,
                  pl.BlockSpec((D_FF, D // 2), lambda j, i, k: (0, j)), "nn", None)],
                [(x2, half_ji)], [(_sds((T, D), F32), half_ji)], add_res, None, PP)

    loss, dx3, dx3b, dg3 = _final_norm_loss(x3, norm_final.reshape(1, D), tgt)

    def du_epi(acc, ex, o, pids):
        o[0][...] = (acc * (2.0 * jnp.maximum(ex[0][...].astype(F32), 0.0))).astype(BF16)

    (du,) = _mm("mlp_down_dx", (T // TL, N_DEV, 1),
                [(dx3b, full_rows(TL), wd2, pl.BlockSpec((1024, D), lambda i, j, k: (j, 0)), "nt", None)],
                [(u, rows(1024, TL))], [(_sds((T, D_FF), BF16), rows(1024, TL))], du_epi, None, PP)
    (dwd,) = _mm("mlp_down_dw", (D // 1024, D_FF // 512, 1), [(act, tall(512), dx3b, tall_j(1024), "tn", None)],
                 [], [(_sds((D_FF, D), BF16), pl.BlockSpec((512, 1024), lambda j, i, k: (i, j)))], store_bf16, None, PP)
    dwd = dwd.reshape(N_DEV, D_FF // N_DEV, D)
    (dwu,) = _mm("mlp_up_dw", (N_DEV, D // 512, 1), [(h2, tall(512), du, tall_j(1024), "tn", None)],
                 [], [(_sds((N_DEV, D, 1024), BF16), pl.BlockSpec((None, 512, 1024), lambda j, i, k: (j, i, 0)))],
                 store_bf16, None, PP)

    def store_f32(acc, ex, o, pids):
        o[0][...] = acc

    out_ji = pl.BlockSpec((TM, 512), lambda j, i, k: (i, j))
    a_ji = lambda w: pl.BlockSpec((TM, w), lambda j, i, k: (i, 0))
    w_ji = lambda n: pl.BlockSpec((N_DEV, 512, n), lambda j, i, k: (0, j, 0))
    (dh2,) = _mm("mlp_up_dx", (D // 512, nt, 1), [(du, a_ji(D_FF), wu, w_ji(1024), "nt_shards", None)],
                 behind(send("mlp", dict(w_down=dwd, w_up=dwu))), [(_sds((T, D), F32), out_ji)], store_f32, None, PP)
    dx2, dx2b, dg2 = _rmsnorm_bwd("rmsnorm_mlp_bwd", dh2, x2, r2, norm_mlp, dx3, True)

    (dwo,) = _mm("out_proj_dw", (D // 1024, D // 512, 1), [(merged, tall(512), dx2b, tall_j(1024), "tn", None)],
                 [], [(_sds((D, D), BF16), pl.BlockSpec((512, 1024), lambda j, i, k: (i, j)))], store_bf16, None, PP)

    def dmerge_epi(acc, ex, o, pids):
        ga, gb, pa_v, pb_v = (e[...].astype(F32) for e in ex)
        o[0][...] = (acc * ga).astype(BF16)
        o[1][...] = (acc * gb).astype(BF16)
        dga = acc * pa_v * ga * (1.0 - ga)
        dgb = acc * pb_v * gb * (1.0 - gb)
        o[2][...] = dga.astype(BF16)
        o[3][...] = dgb.astype(BF16)
        sa = jnp.sum(dga, axis=0, keepdims=True)
        sb = jnp.sum(dgb, axis=0, keepdims=True)
        first = pids[1] == 0

        @pl.when(first)
        def _():
            o[4][...] = sa
            o[5][...] = sb

        @pl.when(jnp.logical_not(first))
        def _():
            o[4][...] += sa
            o[5][...] += sb

    cj = pl.BlockSpec((TL, 512), lambda j, i, k: (i, j))
    bj = pl.BlockSpec((1, 512), lambda j, i, k: (0, j))
    dpa, dpb, dga, dgb, dba, dbb = _mm(
        "out_proj_dx", (4, T // TL, 1),
        [(dx2b, pl.BlockSpec((TL, D), lambda j, i, k: (i, 0)), wo2, pl.BlockSpec((512, D), lambda j, i, k: (j, 0)), "nt", None)],
        [(gates, cj), (gates, pl.BlockSpec((TL, 512), lambda j, i, k: (i, 4 + j))), (pa, cj), (pb, cj)],
        [(_sds((T, D), BF16), cj)] * 4 + [(_sds((1, D), F32), bj)] * 2, dmerge_epi, None,
        ("parallel", "arbitrary", "arbitrary"))

    def store_shards(acc, ex, o, pids):
        for s in range(4):
            o[0][s] = acc[:, s * 256:(s + 1) * 256].astype(BF16)

    def proj_dw(name, y, dp):
        (dw,) = _mm(name, (2, 1, 1), [(y, tall(512), dp, tall_j(1024), "tn", None)],
                    [], [(_sds((N_DEV, 512, 256), BF16), pl.BlockSpec((4, 512, 256), lambda j, i, k: (j, 0, 0)))],
                    store_shards, None, PP)
        return dw

    def proj_dx(name, dp, w):
        (dy,) = _mm(name, (nt, 1, 1),
                    [(dp, full_rows(TM), w, pl.BlockSpec((N_DEV, 512, 256), lambda i, j, k: (0, 0, 0)), "nt_shards", None)],
                    [], [(_sds((T, 512), BF16), full_rows(TM, 512))], store_bf16, None, PP)
        return dy

    def in_dw(name, dact, n):
        def store_pair(acc, ex, o, pids):
            o[0][0] = acc[:, :n].astype(BF16)
            o[0][1] = acc[:, n:].astype(BF16)

        (dw,) = _mm(name, (N_DEV // 2, D // 512, 1), [(h1, tall(512), dact, tall_j(2 * n), "tn", None)],
                    [], [(_sds((N_DEV, D, n), BF16), pl.BlockSpec((2, 512, n), lambda j, i, k: (j, i, 0)))],
                    store_pair, None, PP)
        return dw

    dgate = jnp.concatenate([dga, dgb], axis=1)
    dwg = in_dw("gate_proj_dw", dgate, 512)
    dwpa = proj_dw("proj_a_dw", ya, dpa)
    dwpb = proj_dw("proj_b_dw", yb, dpb)
    token = send("proj", dict(w_gate=dwg, w_out=dwo.reshape(N_DEV, D // N_DEV, D), w_proj_a=dwpa, w_proj_b=dwpb))
    dya = proj_dx("proj_a_dx", dpa, wpa)
    dyb = proj_dx("proj_b_dx", dpb, wpb)

    dq_a, dk_a, dv_a = _dilated_bwd(qkv, _behind(slopes, token), dya, ya, lse_a)
    dq_b, dk_b, dv_b, dtiles = _attn_b_bwd(qkv, tiles, dyb, yb, lse_b)
    drpb = _rpb_reduce(dtiles.reshape(N_HEADS_B * N_DR, GRID_W * GRID_W), sel)[:, :N_DC]
    dqkv = jnp.concatenate([dq_a, dq_b, dk_a, dk_b, dv_a, dv_b], axis=1)
    dwq = in_dw("qkv_proj_dw", dqkv, 768)
    (dh1,) = _mm("in_proj_dx", (D // 512, nt, 1),
                 [(dqkv, a_ji(3 * D), wq, w_ji(768), "nt_shards", None), (dgate, a_ji(2 * D), wg, w_ji(512), "nt_shards", None)],
                 behind(send("in", dict(w_qkv=dwq))), [(_sds((T, D), F32), out_ji)], store_f32, None, PP)
    grad_x, dg1 = _rmsnorm_bwd("rmsnorm_mix_bwd", dh1, x, r1, norm_mix, dx2, False)

    small = dict(norm_mix=dg1, b_gate=jnp.concatenate([dba, dbb], axis=1), rpb=drpb, norm_mlp=dg2, norm_final=dg3)
    return loss, grad_x, small


BIG = ("w_qkv", "w_gate", "w_proj_a", "w_proj_b", "w_out", "w_up", "w_down")
SMALL = ("norm_mix", "b_gate", "rpb", "norm_mlp", "norm_final")
SMALL_ROWS = 96


def _place():
    return lax.axis_index("x"), lax.axis_index("y"), lax.axis_index("c")


def _peer(x, y, c, mask):
    return x ^ (mask >> 2), y ^ ((mask >> 1) & 1), c ^ (mask & 1)


_HBM = pl.BlockSpec(memory_space=pltpu.HBM)
_SEM = pl.BlockSpec(memory_space=pltpu.SEMAPHORE)
_ANY = pl.BlockSpec(memory_space=pl.ANY)
_EFFECT = pltpu.SideEffectType.DATAFLOW_SIDE_EFFECTING


def _hbm(a):
    return pltpu.with_memory_space_constraint(a, pltpu.HBM)


def _other_chips(x, y):
    return [(1 - x, y), (x, 1 - y), (1 - x, 1 - y)]


def _block(x, y, c):
    return 4 * x + 2 * y + c


def _rdma(src, dst, send_sem, recv_sem, to):
    return pltpu.make_async_remote_copy(src_ref=src, dst_ref=dst, send_sem=send_sem, recv_sem=recv_sem, device_id=to,
                                        device_id_type=MESH)


def _gather_start(name, shards):
    n = len(shards)

    def body(*refs):
        srcs, lands = refs[:n], refs[n:2 * n]
        outs = refs[2 * n:]
        send, recv_d2d, recv_ici, local = outs[:n], outs[n:2 * n], outs[2 * n:3 * n], outs[3 * n:4 * n]
        token_ref = outs[6 * n]
        x, y, c = _place()
        for a in range(n):
            mine = lands[a].at[_block(x, y, c)]
            _rdma(srcs[a], mine, send[a].at[0], recv_d2d[a].at[0], (x, y, 1 - c)).start()
            for j, chip in enumerate(_other_chips(x, y)):
                _rdma(srcs[a], mine, send[a].at[1 + j], recv_ici[a].at[j], (*chip, c)).start()
        for a in range(n):
            pltpu.make_async_copy(srcs[a], lands[a].at[_block(x, y, c)], local[a].at[0]).start()
        token_ref[...] = jnp.zeros_like(token_ref)

    lands = [_hbm(lax.empty((N_DEV,) + s.shape, s.dtype)) for s in shards]
    srcs = [_hbm(s) for s in shards]
    dma = pltpu.SemaphoreType.DMA
    res = pl.pallas_call(
        body, name=name,
        out_shape=[dma((4,))] * n + [dma((1,))] * n + [dma((3,))] * n + [dma((1,))] * n
        + [pltpu.HBM(s.shape, s.dtype) for s in srcs] + [pltpu.HBM(l.shape, l.dtype) for l in lands]
        + [jax.ShapeDtypeStruct((8, 128), F32)],
        in_specs=[_HBM] * (2 * n), out_specs=[_SEM] * (4 * n) + [_HBM] * (2 * n) + [pl.BlockSpec(memory_space=pltpu.VMEM)],
        input_output_aliases={i: 4 * n + i for i in range(2 * n)},
        compiler_params=pltpu.CompilerParams(has_side_effects=_EFFECT))(*srcs, *lands)
    keys = ("send", "recv_d2d", "recv_ici", "local", "src", "land")
    return [dict(zip(keys, (res[k * n + a] for k in range(6)))) for a in range(n)], res[6 * n]


def _gather_forward(name, handles, after):
    n = len(handles)

    def body(*refs):
        lands, recv_ici = refs[:n], refs[n:2 * n]
        outs = refs[2 * n + 1:]
        fwd_send, fwd_recv, token_ref = outs[n:2 * n], outs[2 * n:3 * n], outs[3 * n]
        x, y, c = _place()
        for a in range(n):
            for j, chip in enumerate(_other_chips(x, y)):
                blk = lands[a].at[_block(*chip, c)]
                _rdma(blk, blk, fwd_send[a].at[j], recv_ici[a].at[j], (*chip, c)).wait_recv()
                _rdma(blk, blk, fwd_send[a].at[j], fwd_recv[a].at[j], (x, y, 1 - c)).start()
        token_ref[...] = jnp.zeros_like(token_ref)

    dma = pltpu.SemaphoreType.DMA
    lands = [h["land"] for h in handles]
    res = pl.pallas_call(
        body, name=name,
        out_shape=[pltpu.HBM(l.shape, l.dtype) for l in lands] + [dma((3,))] * (2 * n) + [jax.ShapeDtypeStruct((8, 128), F32)],
        in_specs=[_HBM] * n + [_SEM] * n + [_ANY],
        out_specs=[_HBM] * n + [_SEM] * (2 * n) + [pl.BlockSpec(memory_space=pltpu.VMEM)],
        input_output_aliases={i: i for i in range(n)},
        compiler_params=pltpu.CompilerParams(has_side_effects=_EFFECT))(*lands, *[h["recv_ici"] for h in handles], after)
    return [dict(h, land=res[a], fwd_send=res[n + a], fwd_recv=res[2 * n + a]) for a, h in enumerate(handles)], res[3 * n]


def _gather_wait(name, handles, after):
    n = len(handles)

    def body(*refs):
        srcs, lands = refs[:n], refs[n:2 * n]
        send, recv_d2d, local, fwd_send, fwd_recv = (refs[(2 + k) * n:(3 + k) * n] for k in range(5))
        x, y, c = _place()
        sibling = (x, y, 1 - c)
        for a in range(n):
            mine = lands[a].at[_block(x, y, c)]
            _rdma(srcs[a], mine, send[a].at[0], recv_d2d[a].at[0], sibling).wait_send()
            _rdma(srcs[a], lands[a].at[_block(*sibling)], send[a].at[0], recv_d2d[a].at[0], sibling).wait_recv()
            pltpu.make_async_copy(srcs[a], mine, local[a].at[0]).wait()
            for j, chip in enumerate(_other_chips(x, y)):
                _rdma(srcs[a], mine, send[a].at[1 + j], fwd_recv[a].at[j], (*chip, c)).wait_send()
                blk = lands[a].at[_block(*chip, c)]
                _rdma(blk, blk, fwd_send[a].at[j], fwd_recv[a].at[j], sibling).wait_send()
                got = lands[a].at[_block(*chip, 1 - c)]
                _rdma(got, got, fwd_send[a].at[j], fwd_recv[a].at[j], sibling).wait_recv()

    srcs = [h["src"] for h in handles]
    lands = [h["land"] for h in handles]
    sems = [h[k] for k in ("send", "recv_d2d", "local", "fwd_send", "fwd_recv") for h in handles]
    res = pl.pallas_call(
        body, name=name, out_shape=[pltpu.HBM(s.shape, s.dtype) for s in srcs] + [pltpu.HBM(l.shape, l.dtype) for l in lands],
        in_specs=[_HBM] * (2 * n) + [_SEM] * (5 * n) + [_ANY], out_specs=[_HBM] * (2 * n),
        input_output_aliases={i: i for i in range(2 * n)},
        compiler_params=pltpu.CompilerParams(has_side_effects=_EFFECT))(*srcs, *lands, *sems, after)
    return list(res[n:])


def _gather_small(small):
    def body(in_ref, out_ref, send_sems, recv_sems, local_sem):
        x, y, c = _place()
        me = 4 * x + 2 * y + c
        copies = [pltpu.make_async_copy(in_ref, out_ref.at[me], local_sem)]
        for mask in range(1, N_DEV):
            copies.append(pltpu.make_async_remote_copy(
                src_ref=in_ref, dst_ref=out_ref.at[me], send_sem=send_sems.at[mask - 1],
                recv_sem=recv_sems.at[mask - 1], device_id=_peer(x, y, c, mask), device_id_type=MESH))
        for cp in copies:
            cp.start()
        for cp in copies:
            cp.wait()

    any_spec = pl.BlockSpec(memory_space=pl.ANY)
    return pl.pallas_call(
        body, name="gather_small", in_specs=[any_spec], out_specs=any_spec,
        out_shape=jax.ShapeDtypeStruct((N_DEV,) + small.shape, small.dtype),
        scratch_shapes=[pltpu.SemaphoreType.DMA((7,)), pltpu.SemaphoreType.DMA((7,)), pltpu.SemaphoreType.DMA])(small)


def _send_copy(src_refs, land_refs, send_sems, recv_sems, a, mask, scatter, x, y, c):
    px, py, pc = _peer(x, y, c, mask)
    src = src_refs[a].at[4 * px + 2 * py + pc] if scatter else src_refs[a]
    return pltpu.make_async_remote_copy(
        src_ref=src, dst_ref=land_refs[a].at[4 * x + 2 * y + c], send_sem=send_sems[a].at[mask - 1],
        recv_sem=recv_sems[a].at[mask - 1], device_id=(px, py, pc), device_id_type=MESH)


def _local_copy(src_refs, land_refs, local_sems, a, scatter, x, y, c):
    me = 4 * x + 2 * y + c
    return pltpu.make_async_copy(src_refs[a].at[me] if scatter else src_refs[a], land_refs[a].at[me], local_sems[a].at[0])


def _send_start(name, srcs, scatter, after=None):
    n = len(srcs)
    n_in = 2 * n + (after is not None)
    shapes = [s.shape[1:] if scatter else s.shape for s in srcs]

    def body(*refs):
        src_refs, land_refs = refs[:n], refs[n:2 * n]
        outs = refs[n_in:]
        send_sems, recv_sems, local_sems = outs[:n], outs[n:2 * n], outs[2 * n:3 * n]
        token_ref = outs[5 * n]
        x, y, c = _place()
        for a in range(n):
            for mask in range(1, N_DEV):
                _send_copy(src_refs, land_refs, send_sems, recv_sems, a, mask, scatter, x, y, c).start()
        for a in range(n):
            _local_copy(src_refs, land_refs, local_sems, a, scatter, x, y, c).start()
        token_ref[...] = jnp.zeros_like(token_ref)

    lands = [pltpu.with_memory_space_constraint(lax.empty((N_DEV,) + sh, s.dtype), pltpu.HBM) for sh, s in zip(shapes, srcs)]
    srcs = [pltpu.with_memory_space_constraint(s, pltpu.HBM) for s in srcs]
    res = pl.pallas_call(
        body, name=name,
        out_shape=[pltpu.SemaphoreType.DMA((7,))] * (2 * n) + [pltpu.SemaphoreType.DMA((1,))] * n
        + [pltpu.HBM(s.shape, s.dtype) for s in srcs] + [pltpu.HBM(l.shape, l.dtype) for l in lands]
        + [jax.ShapeDtypeStruct((8, 128), F32)],
        in_specs=[_HBM] * (2 * n) + [pl.BlockSpec(memory_space=pl.ANY)] * (n_in - 2 * n),
        out_specs=[_SEM] * (3 * n) + [_HBM] * (2 * n) + [pl.BlockSpec(memory_space=pltpu.VMEM)],
        input_output_aliases={i: 3 * n + i for i in range(2 * n)},
        compiler_params=pltpu.CompilerParams(has_side_effects=_EFFECT))(*srcs, *lands, *([] if after is None else [after]))
    handles = [tuple(res[k * n + a] for k in range(5)) for a in range(n)]
    return handles, res[5 * n]


def _send_wait(name, handles, scatter, after):
    n = len(handles)

    def body(*refs):
        src_refs, land_refs = refs[:n], refs[n:2 * n]
        send_sems, recv_sems, local_sems = refs[2 * n:3 * n], refs[3 * n:4 * n], refs[4 * n:5 * n]
        x, y, c = _place()
        for a in range(n):
            for mask in range(1, N_DEV):
                cp = _send_copy(src_refs, land_refs, send_sems, recv_sems, a, mask, scatter, x, y, c)
                cp.wait_send()
                cp.wait_recv()
            _local_copy(src_refs, land_refs, local_sems, a, scatter, x, y, c).wait()

    srcs = [h[3] for h in handles]
    lands = [h[4] for h in handles]
    res = pl.pallas_call(
        body, name=name, out_shape=[pltpu.HBM(s.shape, s.dtype) for s in srcs] + [pltpu.HBM(l.shape, l.dtype) for l in lands],
        in_specs=[_HBM] * (2 * n) + [_SEM] * (3 * n) + [pl.BlockSpec(memory_space=pl.ANY)], out_specs=[_HBM] * (2 * n),
        input_output_aliases={i: i for i in range(2 * n)},
        compiler_params=pltpu.CompilerParams(has_side_effects=_EFFECT))(
            *srcs, *lands, *[h[0] for h in handles], *[h[1] for h in handles], *[h[2] for h in handles], after)
    return list(res[n:])


def _adamw(name, parts, w, m, v):
    K, n = w.shape
    tr = min(K, 128)

    def body(p_ref, w_ref, m_ref, v_ref, g_ref, d_ref, nm_ref, nv_ref):
        g = p_ref[0].astype(F32)
        for i in range(1, N_DEV):
            g = g + p_ref[i].astype(F32)
        nm = ADAM_B1 * m_ref[...] + (1.0 - ADAM_B1) * g
        nv = ADAM_B2 * v_ref[...] + (1.0 - ADAM_B2) * jnp.square(g)
        m_hat = nm / (1.0 - ADAM_B1 ** ADAM_STEP)
        v_hat = nv / (1.0 - ADAM_B2 ** ADAM_STEP)
        g_ref[...] = g
        d_ref[...] = -ADAM_LR * (m_hat / (jnp.sqrt(v_hat) + ADAM_EPS) + ADAM_WD * w_ref[...])
        nm_ref[...] = nm
        nv_ref[...] = nv

    blk = pl.BlockSpec((tr, n), lambda i: (i, 0))
    return _pc(body, name=name, grid=(K // tr,), in_specs=[pl.BlockSpec((N_DEV, tr, n), lambda i: (0, i, 0)), blk, blk, blk],
               out_specs=[blk] * 4, out_shape=[jax.ShapeDtypeStruct((K, n), F32)] * 4, sem=("parallel",))(parts, w, m, v)


def _pack_small(vals):
    flat = jnp.concatenate([v.reshape(-1) for v in vals])
    return jnp.pad(flat, (0, SMALL_ROWS * 128 - flat.shape[0])).reshape(SMALL_ROWS, 128)


def _unpack_small(packed, shapes):
    flat, out, off = packed.reshape(-1), [], 0
    for s in shapes:
        size = int(np.prod(s))
        out.append(flat[off:off + size].reshape(s))
        off += size
    return out


def kernel(x, norm_mix, w_qkv, w_gate, b_gate, rpb, w_proj_a, w_proj_b, w_out, norm_mlp, w_up, w_down, norm_final, loss_target, m_norm_mix, m_w_qkv, m_w_gate, m_b_gate, m_rpb, m_w_proj_a, m_w_proj_b, m_w_out, m_norm_mlp, m_w_up, m_w_down, m_norm_final, v_norm_mix, v_w_qkv, v_w_gate, v_b_gate, v_rpb, v_w_proj_a, v_w_proj_b, v_w_out, v_norm_mlp, v_w_up, v_w_down, v_norm_final):
    w = dict(norm_mix=norm_mix, w_qkv=w_qkv, w_gate=w_gate, b_gate=b_gate, rpb=rpb, w_proj_a=w_proj_a, w_proj_b=w_proj_b,
             w_out=w_out, norm_mlp=norm_mlp, w_up=w_up, w_down=w_down, norm_final=norm_final)
    m = dict(norm_mix=m_norm_mix, w_qkv=m_w_qkv, w_gate=m_w_gate, b_gate=m_b_gate, rpb=m_rpb, w_proj_a=m_w_proj_a,
             w_proj_b=m_w_proj_b, w_out=m_w_out, norm_mlp=m_norm_mlp, w_up=m_w_up, w_down=m_w_down, norm_final=m_norm_final)
    v = dict(norm_mix=v_norm_mix, w_qkv=v_w_qkv, w_gate=v_w_gate, b_gate=v_b_gate, rpb=v_rpb, w_proj_a=v_w_proj_a,
             w_proj_b=v_w_proj_b, w_out=v_w_out, norm_mlp=v_norm_mlp, w_up=v_w_up, w_down=v_w_down, norm_final=v_norm_final)
    order = ("norm_mix", "w_qkv", "w_gate", "b_gate", "rpb", "w_proj_a", "w_proj_b", "w_out", "norm_mlp", "w_up",
             "w_down", "norm_final")

    shard = {n: w[n][0].astype(BF16) for n in BIG}
    stages = dict(qkv=("w_qkv",), gate=("w_gate", "w_proj_a", "w_proj_b", "w_out"), mlp=("w_up", "w_down"))
    names = [n for s in stages.values() for n in s]
    handles, token = _gather_start("gather_start", [shard[n] for n in names])
    handles = dict(zip(names, handles))

    def late_weights(stage, after):
        if stage == "qkv":
            return tuple(_gather_wait("gather_wait_qkv", _gather_forward("gather_forward_qkv", [handles["w_qkv"]], after)[0], after))
        if stage == "attention":
            handles["gate"], tok = _gather_forward("gather_forward_gate", [handles[n] for n in stages["gate"]], after)
            return (tok,)
        if stage == "mixer":
            handles["up"], tok = _gather_forward("gather_forward_up", [handles["w_up"]], after)
            return tuple(_gather_wait("gather_wait_gate", handles["gate"], after)) + (tok,)
        if stage == "residual":
            handles["down"], tok = _gather_forward("gather_forward_down", [handles["w_down"]], after)
            return (tok,)
        return tuple(_gather_wait("gather_wait_" + stage, handles[stage], after))

    sent = {}

    def send(stage, grads):
        handles, tok = _send_start("grads_start_" + stage, list(grads.values()), True)
        sent[stage] = (tuple(grads), handles)
        return tok

    loss, grad_x, small = _local_step(x[0], loss_target[0], _behind(norm_mix, token), b_gate, rpb[0], norm_mlp, norm_final,
                                      late_weights, send)

    grad, delta, new_m, new_v = {}, {}, {}, {}
    after = grad_x
    for stage in ("mlp", "proj", "in"):
        names, handles = sent[stage]
        for n, parts in zip(names, _send_wait("grads_wait_" + stage, handles, True, after)):
            res = _adamw("adamw_" + n, parts, w[n][0], m[n][0], v[n][0])
            grad[n], delta[n], new_m[n], new_v[n] = (r[None] for r in res)
            after = res[1]
    res = _adamw("adamw_small", _gather_small(_pack_small([small[n] for n in SMALL])),
                 _pack_small([w[n] for n in SMALL]), _pack_small([m[n] for n in SMALL]), _pack_small([v[n] for n in SMALL]))
    shapes = [w[n].shape for n in SMALL]
    for tree, packed in zip((grad, delta, new_m, new_v), res):
        for n, val in zip(SMALL, _unpack_small(packed, shapes)):
            tree[n] = val

    total = lax.psum(loss[0, 0], ("x", "y", "c"))
    return (total, grad_x[None], *[grad[n] for n in order], *[delta[n] for n in order],
            *[new_m[n] for n in order], *[new_v[n] for n in order])
```

```python
import numpy as np
import jax
import jax.numpy as jnp
from jax import lax
from jax.experimental import pallas as pl
from jax.experimental.pallas import tpu as pltpu

F32 = jnp.float32
BF16 = jnp.bfloat16
MESH = pl.DeviceIdType.MESH

D_MODEL = 2048
HEAD_DIM = 128
N_HEADS = 16
N_HEADS_A = 12
N_HEADS_B = 4
DILATIONS = (1, 4, 16)
HALF_WINDOW = 64
GRID_W = 64
NA_ROWS = 8
NA_COLS = 16
D_FF = 4 * D_MODEL
N_DEV = 8
EPS = 1e-6
NEG = -1e30
SCALE = HEAD_DIM ** -0.5
ADAM_LR, ADAM_B1, ADAM_B2, ADAM_EPS, ADAM_WD, ADAM_STEP = 0.001, 0.9, 0.999, 1e-08, 0.01, 10

VMEM_LIMIT = 56 * 1024 * 1024
TM = 512
TL = 1024
TS = 256
BQ_A = 128


def _pc(body, *, name, grid, in_specs, out_specs, out_shape, scratch=(), sem=None):
    return pl.pallas_call(
        body, name=name, grid=grid, in_specs=in_specs, out_specs=out_specs, out_shape=out_shape,
        scratch_shapes=list(scratch),
        compiler_params=pltpu.CompilerParams(dimension_semantics=sem, vmem_limit_bytes=VMEM_LIMIT))


_DOT_DIMS = {"nn": (((1,), (0,)), ((), ())), "nt": (((1,), (1,)), ((), ())), "tn": (((0,), (0,)), ((), ()))}


def _dot(a, b, mode, precision=None):
    if mode == "nn_shards":
        return jnp.concatenate([_dot(a, b[s], "nn") for s in range(b.shape[0])], axis=1)
    if mode == "nt_shards":
        n = b.shape[2]
        out = _dot(a[:, :n], b[0], "nt")
        for s in range(1, b.shape[0]):
            out = out + _dot(a[:, s * n:(s + 1) * n], b[s], "nt")
        return out
    return lax.dot_general(a, b, _DOT_DIMS[mode], preferred_element_type=F32, precision=precision)


def _mm(name, grid, pairs, extras, outs, epilogue, acc_shape, sem):
    n_pairs, n_extra, n_out = len(pairs), len(extras), len(outs)
    k_axis = len(grid) - 1
    nk = grid[k_axis]

    def body(*refs):
        pids = [pl.program_id(ax) for ax in range(len(grid))]
        ab = refs[:2 * n_pairs]
        ex = refs[2 * n_pairs:2 * n_pairs + n_extra]
        o = refs[2 * n_pairs + n_extra:2 * n_pairs + n_extra + n_out]
        part = None
        for p, pair in enumerate(pairs):
            a = ab[2 * p][...]
            if pair[5] is not None:
                a = pair[5](a)
            d = _dot(a, ab[2 * p + 1][...], pair[4])
            part = d if part is None else part + d
        if nk == 1:
            epilogue(part, ex, o, pids)
        else:
            acc = refs[-1]
            k = pids[k_axis]

            @pl.when(k == 0)
            def _():
                acc[...] = part

            @pl.when(k > 0)
            def _():
                acc[...] += part

            @pl.when(k == nk - 1)
            def _():
                epilogue(acc[...], ex, o, pids)

    operands, in_specs = [], []
    for a, a_spec, b, b_spec, _, _ in pairs:
        operands += [a, b]
        in_specs += [a_spec, b_spec]
    for e, e_spec in extras:
        operands.append(e)
        in_specs.append(e_spec)
    res = _pc(body, name=name, grid=grid, in_specs=in_specs, out_specs=[s for _, s in outs],
              out_shape=[o for o, _ in outs], scratch=[pltpu.VMEM(acc_shape, F32)] if nk > 1 else [], sem=sem)(*operands)
    return res


def _rmsnorm_fwd(x, g, name):
    T, D = x.shape

    def body(x_ref, g_ref, h_ref, r_ref):
        xv = x_ref[...]
        r = lax.rsqrt(jnp.mean(xv * xv, axis=-1, keepdims=True) + EPS)
        h_ref[...] = (xv * r * g_ref[...]).astype(BF16)
        r_ref[...] = r

    return _pc(body, name=name, grid=(T // TM,),
               in_specs=[pl.BlockSpec((TM, D), lambda i: (i, 0)), pl.BlockSpec((1, D), lambda i: (0, 0))],
               out_specs=[pl.BlockSpec((TM, D), lambda i: (i, 0)), pl.BlockSpec((TM, 1), lambda i: (i, 0))],
               out_shape=[jax.ShapeDtypeStruct((T, D), BF16), jax.ShapeDtypeStruct((T, 1), F32)],
               sem=("parallel",))(x, g)


def _rmsnorm_bwd_block(dh, x, r, g, dres):
    xh = x * r
    dxh = dh * g
    dx = dres + r * (dxh - xh * jnp.mean(dxh * xh, axis=-1, keepdims=True))
    return dx, jnp.sum(dh * xh, axis=0, keepdims=True)


def _rmsnorm_bwd(name, dh, x, r, g, dres, want_bf16):
    T, D = x.shape

    def body(dh_ref, x_ref, r_ref, g_ref, dres_ref, *outs):
        dx, dgp = _rmsnorm_bwd_block(dh_ref[...], x_ref[...], r_ref[...], g_ref[...], dres_ref[...])
        outs[0][...] = dx
        if want_bf16:
            outs[1][...] = dx.astype(BF16)
        first = pl.program_id(0) == 0

        @pl.when(first)
        def _():
            outs[-1][...] = dgp

        @pl.when(jnp.logical_not(first))
        def _():
            outs[-1][...] += dgp

    row = pl.BlockSpec((TS, D), lambda i: (i, 0))
    vec = pl.BlockSpec((1, D), lambda i: (0, 0))
    n_row = 2 if want_bf16 else 1
    return _pc(body, name=name, grid=(T // TS,), in_specs=[row, row, pl.BlockSpec((TS, 1), lambda i: (i, 0)), vec, row],
               out_specs=[row] * n_row + [vec],
               out_shape=[jax.ShapeDtypeStruct((T, D), F32)] + [jax.ShapeDtypeStruct((T, D), BF16)] * (n_row - 1)
               + [jax.ShapeDtypeStruct((1, D), F32)], sem=("arbitrary",))(dh, x, r, g, dres)


def _final_norm_loss(x3, g, tgt):
    T, D = x3.shape

    def body(x_ref, g_ref, t_ref, loss_ref, dx_ref, dxb_ref, dg_ref):
        i = pl.program_id(0)
        xv = x_ref[...]
        gv = g_ref[...]
        r = lax.rsqrt(jnp.mean(xv * xv, axis=-1, keepdims=True) + EPS)
        xh = xv * r
        e = xh * gv - t_ref[...]
        part = 0.5 * jnp.sum(jnp.mean(e * e, axis=-1, keepdims=True))
        dy = e * (1.0 / D)
        dxh = dy * gv
        dx = r * (dxh - xh * jnp.mean(dxh * xh, axis=-1, keepdims=True))
        dx_ref[...] = dx
        dxb_ref[...] = dx.astype(BF16)
        dgp = jnp.sum(dy * xh, axis=0, keepdims=True)

        @pl.when(i == 0)
        def _():
            loss_ref[...] = jnp.zeros(loss_ref.shape, F32) + part
            dg_ref[...] = dgp

        @pl.when(i > 0)
        def _():
            loss_ref[...] += part
            dg_ref[...] += dgp

    row = pl.BlockSpec((TM, D), lambda i: (i, 0))
    vec = pl.BlockSpec((1, D), lambda i: (0, 0))
    return _pc(body, name="final_norm_loss", grid=(T // TM,), in_specs=[row, vec, row],
               out_specs=[pl.BlockSpec((8, 128), lambda i: (0, 0)), row, row, vec],
               out_shape=[jax.ShapeDtypeStruct((8, 128), F32), jax.ShapeDtypeStruct((T, D), F32),
                          jax.ShapeDtypeStruct((T, D), BF16), jax.ShapeDtypeStruct((1, D), F32)],
               sem=("arbitrary",))(x3, g, tgt)


def _alibi_slopes():
    return np.asarray(2.0 ** (-8.0 * np.arange(1, N_HEADS_A + 1) / N_HEADS_A), dtype=np.float32)


def _col_to_row(col):
    return jnp.transpose(jnp.broadcast_to(col, (BQ_A, BQ_A)))[0:1, :]


def _row_to_col(row):
    return jnp.transpose(jnp.broadcast_to(row, (BQ_A, BQ_A)))[:, 0:1]


def _row_spec(rows_per_step, first=0):
    return pl.BlockSpec((None, rows_per_step // BQ_A, BQ_A), lambda j, i: (first + j, i, 0))


def _a_windows(T):
    half = [HALF_WINDOW * d for d in DILATIONS]
    return half, [min(BQ_A + 2 * w, T) for w in half]


def _a_bias(slope, g, offset, wk, half):
    d = DILATIONS[g]
    dlt = offset + lax.broadcasted_iota(jnp.int32, (BQ_A, wk), 1) - lax.broadcasted_iota(jnp.int32, (BQ_A, wk), 0)
    ad = jnp.maximum(dlt, -dlt)
    ok = (ad <= half) & ((ad & (d - 1)) == 0)
    return jnp.where(ok, -slope * ad.astype(F32), NEG)


def _a_scores(q, kw, slope, g, offset, wk, half):
    return _dot(q, kw, "nt") * SCALE + _a_bias(slope, g, offset, wk, half)


N_JOINT = 2


SUB_J = 8
BS_J = SUB_J * BQ_A


def _a_qkv_specs(T):
    specs = []
    for g in range(N_JOINT):
        specs.append(pl.BlockSpec((BS_J, HEAD_DIM), lambda j, i, g=g: (i, 4 * g + j)))
        specs.append(pl.BlockSpec((T, HEAD_DIM), lambda j, i, g=g: (0, N_HEADS + 4 * g + j)))
        specs.append(pl.BlockSpec((T, HEAD_DIM), lambda j, i, g=g: (0, 2 * N_HEADS + 4 * g + j)))
    return specs


def _a_window(step, u, g, T, half, wks):
    t0 = (step * SUB_J + u) * BQ_A
    ws = pl.multiple_of(jnp.clip(t0 - half[g], 0, T - wks[g]), 64)
    return pl.ds(ws, wks[g]), ws - t0


def _attn_a_fwd(qkv, slopes, o_last, lse_last):
    T = qkv.shape[0]
    half, wks = _a_windows(T)

    def body(sl_ref, *refs):
        qkv_refs, (o_ref, l_ref, y_ref, lse_ref) = refs[:3 * N_JOINT], refs[3 * N_JOINT:]
        j = pl.program_id(0)
        for u in range(SUB_J):
            rows = slice(u * BQ_A, (u + 1) * BQ_A)
            ss, vws = [], []
            lse_last = _row_to_col(l_ref[u:u + 1, :])
            m = lse_last
            for g in range(N_JOINT):
                q_ref, k_ref, v_ref = qkv_refs[3 * g:3 * g + 3]
                win, offset = _a_window(pl.program_id(1), u, g, T, half, wks)
                ss.append(_a_scores(q_ref[rows, :], k_ref[win, :], sl_ref[4 * g + j], g, offset, wks[g], half[g]))
                vws.append(v_ref[win, :])
                m = jnp.maximum(m, jnp.max(ss[g], axis=-1, keepdims=True))
            den = jnp.exp(lse_last - m)
            acc = den * o_ref[rows, :].astype(F32)
            for g in range(N_JOINT):
                p = jnp.exp(ss[g] - m)
                den = den + jnp.sum(p, axis=-1, keepdims=True)
                acc = acc + _dot(p.astype(BF16), vws[g], "nn")
            y_ref[rows, :] = (acc / den).astype(BF16)
            lse_ref[u:u + 1, :] = _col_to_row(m + jnp.log(den))

    blk = pl.BlockSpec((BS_J, HEAD_DIM), lambda j, i: (i, j))
    col = _row_spec(BS_J)
    return _pc(body, name="attn_a_fwd", grid=(4, T // BS_J),
               in_specs=[pl.BlockSpec(memory_space=pltpu.SMEM)] + _a_qkv_specs(T) + [blk, col], out_specs=[blk, col],
               out_shape=[jax.ShapeDtypeStruct((T, 4 * HEAD_DIM), BF16), jax.ShapeDtypeStruct((4, T // BQ_A, BQ_A), F32)],
               sem=("parallel", "parallel"))(slopes, *([qkv] * (3 * N_JOINT)), o_last, lse_last)


def _attn_a_bwd(qkv, slopes, dy, y, lse):
    T = qkv.shape[0]
    steps = T // BS_J
    half, wks = _a_windows(T)

    def body(sl_ref, *refs):
        qkv_refs = refs[:3 * N_JOINT]
        dy_ref, y_ref, lse_ref, dq_ref, dk_ref, dv_ref, delta_ref, dk_acc, dv_acc = refs[3 * N_JOINT:]
        j = pl.program_id(0)
        step = pl.program_id(1)

        @pl.when(step == 0)
        def _():
            dk_acc[...] = jnp.zeros_like(dk_acc)
            dv_acc[...] = jnp.zeros_like(dv_acc)

        for u in range(SUB_J):
            rows = slice(u * BQ_A, (u + 1) * BQ_A)
            dyv = dy_ref[rows, :]
            delta = jnp.sum(dyv.astype(F32) * y_ref[rows, :].astype(F32), axis=-1, keepdims=True)
            delta_ref[u:u + 1, :] = _col_to_row(delta)
            lse_v = _row_to_col(lse_ref[u:u + 1, :])
            for g in range(N_JOINT):
                q_ref, k_ref, v_ref = qkv_refs[3 * g:3 * g + 3]
                win, offset = _a_window(step, u, g, T, half, wks)
                qv = q_ref[rows, :]
                kw = k_ref[win, :]
                s = _a_scores(qv, kw, sl_ref[4 * g + j], g, offset, wks[g], half[g])
                p = jnp.exp(s - lse_v)
                dp = _dot(dyv, v_ref[win, :], "nt")
                ds = (p * (dp - delta)).astype(BF16)
                dq_ref[g, rows, :] = (_dot(ds, kw, "nn") * SCALE).astype(BF16)
                dk_acc[g, win, :] += _dot(ds, qv, "tn") * SCALE
                dv_acc[g, win, :] += _dot(p.astype(BF16), dyv, "tn")

        @pl.when(step == steps - 1)
        def _():
            dk_ref[...] = dk_acc[...].astype(BF16)
            dv_ref[...] = dv_acc[...].astype(BF16)

    blk = pl.BlockSpec((BS_J, HEAD_DIM), lambda j, i: (i, j))
    col = _row_spec(BS_J)
    whole = pl.BlockSpec((N_JOINT, T, HEAD_DIM), lambda j, i: (0, 0, j))
    full = jax.ShapeDtypeStruct((N_JOINT, T, 4 * HEAD_DIM), BF16)
    return _pc(body, name="attn_a_bwd", grid=(4, steps),
               in_specs=[pl.BlockSpec(memory_space=pltpu.SMEM)] + _a_qkv_specs(T) + [blk, blk, col],
               out_specs=[pl.BlockSpec((N_JOINT, BS_J, HEAD_DIM), lambda j, i: (0, i, j)), whole, whole, col],
               out_shape=[full, full, full, jax.ShapeDtypeStruct((4, T // BQ_A, BQ_A), F32)],
               scratch=[pltpu.VMEM((N_JOINT, T, HEAD_DIM), F32), pltpu.VMEM((N_JOINT, T, HEAD_DIM), F32)],
               sem=("parallel", "arbitrary"))(slopes, *([qkv] * (3 * N_JOINT)), dy, y, lse)


SUB_A = 8


def _to_residue_major(a, d, axis=0):
    if d == 1:
        return a
    sh = a.shape
    a = a.reshape(sh[:axis] + (sh[axis] // d, d) + sh[axis + 1:])
    return jnp.swapaxes(a, axis, axis + 1).reshape(sh)


def _to_token_order(a, d, axis=0):
    if d == 1:
        return a
    sh = a.shape
    a = a.reshape(sh[:axis] + (d, sh[axis] // d) + sh[axis + 1:])
    return jnp.swapaxes(a, axis, axis + 1).reshape(sh)


def _band_bias(slope, offset, wk):
    dlt = offset + lax.broadcasted_iota(jnp.int32, (BQ_A, wk), 1) - lax.broadcasted_iota(jnp.int32, (BQ_A, wk), 0)
    ad = jnp.maximum(dlt, -dlt)
    return jnp.where(ad <= HALF_WINDOW, -slope * ad.astype(F32), NEG)


def _band_window(step, u, seg, wk):
    t0 = (step * SUB_A + u) * BQ_A
    lo = (t0 // seg) * seg
    ws = pl.multiple_of(jnp.clip(t0 - HALF_WINDOW, lo, lo + seg - wk), 64)
    return slice(u * BQ_A, (u + 1) * BQ_A), pl.ds(ws, wk), ws - t0


def _band_specs(T, cols):
    bs = SUB_A * BQ_A
    assert T % bs == 0 and T // max(DILATIONS) >= BQ_A, "a query sub-block must lie inside one segment"
    return [pl.BlockSpec(memory_space=pltpu.SMEM),
            pl.BlockSpec((bs, HEAD_DIM), lambda j, i: (i, cols[0] + j)),
            pl.BlockSpec((T, HEAD_DIM), lambda j, i: (0, cols[1] + j)),
            pl.BlockSpec((T, HEAD_DIM), lambda j, i: (0, cols[2] + j))]


def _banded_fwd(name, src, cols, seg, slopes):
    T = src.shape[0]
    bs = SUB_A * BQ_A
    wk = min(BQ_A + 2 * HALF_WINDOW, seg)

    def body(sl_ref, q_ref, k_ref, v_ref, o_ref, lse_ref):
        slope = sl_ref[pl.program_id(0)]
        for u in range(SUB_A):
            rows, win, offset = _band_window(pl.program_id(1), u, seg, wk)
            s = _dot(q_ref[rows, :], k_ref[win, :], "nt") * SCALE + _band_bias(slope, offset, wk)
            m = jnp.max(s, axis=-1, keepdims=True)
            p = jnp.exp(s - m)
            den = jnp.sum(p, axis=-1, keepdims=True)
            o_ref[rows, :] = (_dot(p.astype(BF16), v_ref[win, :], "nn") / den).astype(BF16)
            lse_ref[u:u + 1, :] = _col_to_row(m + jnp.log(den))

    return _pc(body, name=name, grid=(4, T // bs), in_specs=_band_specs(T, cols),
               out_specs=[pl.BlockSpec((bs, HEAD_DIM), lambda j, i: (i, j)), _row_spec(bs)],
               out_shape=[jax.ShapeDtypeStruct((T, 4 * HEAD_DIM), BF16), jax.ShapeDtypeStruct((4, T // BQ_A, BQ_A), F32)],
               sem=("parallel", "parallel"))(slopes, src, src, src)


def _banded_bwd(name, src, cols, seg, slopes, dy, delta_lse):
    T = src.shape[0]
    bs = SUB_A * BQ_A
    steps = T // bs
    wk = min(BQ_A + 2 * HALF_WINDOW, seg)

    def body(sl_ref, q_ref, k_ref, v_ref, dy_ref, delta_ref, lse_ref, dq_ref, dk_ref, dv_ref, dk_acc, dv_acc):
        slope = sl_ref[pl.program_id(0)]
        step = pl.program_id(1)

        @pl.when(step == 0)
        def _():
            dk_acc[...] = jnp.zeros_like(dk_acc)
            dv_acc[...] = jnp.zeros_like(dv_acc)

        for u in range(SUB_A):
            rows, win, offset = _band_window(step, u, seg, wk)
            qv, kw, dyv = q_ref[rows, :], k_ref[win, :], dy_ref[rows, :]
            s = _dot(qv, kw, "nt") * SCALE + _band_bias(slope, offset, wk)
            p = jnp.exp(s - _row_to_col(lse_ref[u:u + 1, :]))
            ds = (p * (_dot(dyv, v_ref[win, :], "nt") - _row_to_col(delta_ref[u:u + 1, :]))).astype(BF16)
            dq_ref[rows, :] = (_dot(ds, kw, "nn") * SCALE).astype(BF16)
            dk_acc[win, :] += _dot(ds, qv, "tn") * SCALE
            dv_acc[win, :] += _dot(p.astype(BF16), dyv, "tn")

        @pl.when(step == steps - 1)
        def _():
            dk_ref[...] = dk_acc[...].astype(BF16)
            dv_ref[...] = dv_acc[...].astype(BF16)

    blk = pl.BlockSpec((bs, HEAD_DIM), lambda j, i: (i, j))
    whole = pl.BlockSpec((T, HEAD_DIM), lambda j, i: (0, j))
    full = jax.ShapeDtypeStruct((T, 4 * HEAD_DIM), BF16)
    return _pc(body, name=name, grid=(4, steps),
               in_specs=_band_specs(T, cols) + [blk, _row_spec(bs), _row_spec(bs, 4)],
               out_specs=[blk, whole, whole], out_shape=[full, full, full],
               scratch=[pltpu.VMEM((T, HEAD_DIM), F32), pltpu.VMEM((T, HEAD_DIM), F32)],
               sem=("parallel", "arbitrary"))(slopes, src, src, src, dy, delta_lse, delta_lse)


def _last_group_source(qkv):
    third = qkv.shape[1] // 3
    own = jnp.concatenate([qkv[:, t * third + 512 * N_JOINT:t * third + 512 * (N_JOINT + 1)] for t in range(3)], axis=1)
    return _to_residue_major(own, DILATIONS[N_JOINT])


def _dilated_fwd(qkv, slopes):
    d = DILATIONS[N_JOINT]
    last_src = _last_group_source(qkv)
    o, lse = _banded_fwd("attn_a_last_fwd", last_src, (0, 4, 8), qkv.shape[0] // d, slopes[4 * N_JOINT:] * float(d))
    lse = _to_token_order(lse.reshape(4, -1), d, axis=1).reshape(lse.shape)
    y, joint_lse = _attn_a_fwd(qkv, slopes, _to_token_order(o, d), lse)
    return y, joint_lse, last_src


def _dilated_bwd(qkv, last_src, slopes, dy, y, lse):
    d = DILATIONS[N_JOINT]
    dq, dk, dv, delta = _attn_a_bwd(qkv, slopes, dy, y, lse)
    delta_lse = jnp.concatenate([delta, lse], axis=0)
    delta_lse = _to_residue_major(delta_lse.reshape(8, -1), d, axis=1).reshape(delta_lse.shape)
    last = _banded_bwd("attn_a_last_bwd", last_src, (0, 4, 8), qkv.shape[0] // d,
                       slopes[4 * N_JOINT:] * float(d), _to_residue_major(dy, d), delta_lse)
    return [jnp.concatenate([joint[g] for g in range(N_JOINT)] + [_to_token_order(own, d)], axis=1)
            for joint, own in zip((dq, dk, dv), last)]


N_DR = 2 * NA_ROWS - 1
N_DC = 2 * NA_COLS - 1
WK_B = NA_ROWS * GRID_W


def _toeplitz_consts():
    col = np.arange(GRID_W)
    start = np.clip(col - NA_COLS // 2, 0, GRID_W - NA_COLS)
    ok = (col[None, :] >= start[:, None]) & (col[None, :] < start[:, None] + NA_COLS)
    dc = np.clip(col[None, :] - col[:, None], -(NA_COLS - 1), NA_COLS - 1) + NA_COLS - 1
    sel = np.zeros((128, GRID_W * GRID_W), np.float32)
    sel[dc.reshape(-1), np.arange(GRID_W * GRID_W)] = ok.reshape(-1).astype(np.float32)
    mask = np.where(ok, 0.0, NEG).astype(np.float32).reshape(1, -1)
    return sel, mask


def _rpb_expand(rpb_rows, sel, mask):
    def body(r_ref, s_ref, m_ref, o_ref):
        o_ref[...] = _dot(r_ref[...], s_ref[...], "nn", precision=lax.Precision.HIGHEST) + m_ref[...]

    n = rpb_rows.shape[0]
    return _pc(body, name="rpb_expand", grid=(1,),
               in_specs=[pl.BlockSpec((n, 128), lambda i: (0, 0)), pl.BlockSpec(sel.shape, lambda i: (0, 0)),
                         pl.BlockSpec(mask.shape, lambda i: (0, 0))],
               out_specs=pl.BlockSpec((n, sel.shape[1]), lambda i: (0, 0)),
               out_shape=jax.ShapeDtypeStruct((n, sel.shape[1]), F32), sem=("arbitrary",))(rpb_rows, sel, mask)


def _rpb_reduce(dtiles, sel):
    def body(t_ref, s_ref, o_ref):
        o_ref[...] = _dot(t_ref[...], s_ref[...], "nt", precision=lax.Precision.HIGHEST)

    n = dtiles.shape[0]
    return _pc(body, name="rpb_reduce", grid=(1,),
               in_specs=[pl.BlockSpec(dtiles.shape, lambda i: (0, 0)), pl.BlockSpec(sel.shape, lambda i: (0, 0))],
               out_specs=pl.BlockSpec((n, 128), lambda i: (0, 0)),
               out_shape=jax.ShapeDtypeStruct((n, 128), F32), sem=("arbitrary",))(dtiles, sel)


ROWS_B = 16
BQ_B = ROWS_B * GRID_W


def _b_specs(T):
    return [pl.BlockSpec((BQ_B, HEAD_DIM), lambda h, r: (r, N_HEADS_A + h)),
            pl.BlockSpec((T, HEAD_DIM), lambda h, r: (0, N_HEADS + N_HEADS_A + h)),
            pl.BlockSpec((T, HEAD_DIM), lambda h, r: (0, 2 * N_HEADS + N_HEADS_A + h)),
            pl.BlockSpec((None, N_DR, GRID_W, GRID_W), lambda h, r: (h, 0, 0, 0))]


def _b_row(step, u, rows, t_ref, bias_ref):
    r = step * ROWS_B + u
    rs = jnp.clip(r - NA_ROWS // 2, 0, rows - NA_ROWS)
    first_tile = rs - r + NA_ROWS - 1
    for a in range(NA_ROWS):
        bias_ref[u, :, a * GRID_W:(a + 1) * GRID_W] = t_ref[first_tile + a]
    return slice(u * GRID_W, (u + 1) * GRID_W), pl.ds(pl.multiple_of(rs * GRID_W, GRID_W), WK_B), first_tile


def _attn_b_fwd(qkv, tiles):
    T = qkv.shape[0]
    rows = T // GRID_W

    def body(q_ref, k_ref, v_ref, t_ref, y_ref, lse_ref, bias_ref):
        for u in range(ROWS_B):
            blk, win, _ = _b_row(pl.program_id(1), u, rows, t_ref, bias_ref)
            s = _dot(q_ref[blk, :], k_ref[win, :], "nt") * SCALE + bias_ref[u]
            m = jnp.max(s, axis=-1, keepdims=True)
            p = jnp.exp(s - m)
            den = jnp.sum(p, axis=-1, keepdims=True)
            y_ref[blk, :] = (_dot(p.astype(BF16), v_ref[win, :], "nn") / den).astype(BF16)
            lse_ref[blk, :] = m + jnp.log(den)

    return _pc(body, name="attn_b_fwd", grid=(N_HEADS_B, rows // ROWS_B), in_specs=_b_specs(T),
               out_specs=[pl.BlockSpec((BQ_B, HEAD_DIM), lambda h, r: (r, h)),
                          pl.BlockSpec((None, BQ_B, 1), lambda h, r: (h, r, 0))],
               out_shape=[jax.ShapeDtypeStruct((T, N_HEADS_B * HEAD_DIM), BF16),
                          jax.ShapeDtypeStruct((N_HEADS_B, T, 1), F32)],
               scratch=[pltpu.VMEM((ROWS_B, GRID_W, WK_B), F32)], sem=("parallel", "parallel"))(qkv, qkv, qkv, tiles)


def _attn_b_bwd(qkv, tiles, dy, y, lse):
    T = qkv.shape[0]
    rows = T // GRID_W
    steps = rows // ROWS_B

    def body(q_ref, k_ref, v_ref, t_ref, dy_ref, y_ref, lse_ref, dq_ref, dk_ref, dv_ref, dt_ref,
             bias_ref, ds_ref, dk_acc, dv_acc):
        step = pl.program_id(1)

        @pl.when(step == 0)
        def _():
            dk_acc[...] = jnp.zeros_like(dk_acc)
            dv_acc[...] = jnp.zeros_like(dv_acc)
            dt_ref[...] = jnp.zeros_like(dt_ref)

        for u in range(ROWS_B):
            blk, win, first_tile = _b_row(step, u, rows, t_ref, bias_ref)
            qv = q_ref[blk, :]
            kw = k_ref[win, :]
            dyv = dy_ref[blk, :]
            s = _dot(qv, kw, "nt") * SCALE + bias_ref[u]
            p = jnp.exp(s - lse_ref[blk, :])
            delta = jnp.sum(dyv.astype(F32) * y_ref[blk, :].astype(F32), axis=-1, keepdims=True)
            ds = p * (_dot(dyv, v_ref[win, :], "nt") - delta)
            ds_ref[u] = ds
            for a in range(NA_ROWS):
                dt_ref[first_tile + a] += ds_ref[u, :, a * GRID_W:(a + 1) * GRID_W]
            dsb = ds.astype(BF16)
            dq_ref[blk, :] = (_dot(dsb, kw, "nn") * SCALE).astype(BF16)
            dk_acc[win, :] += _dot(dsb, qv, "tn") * SCALE
            dv_acc[win, :] += _dot(p.astype(BF16), dyv, "tn")

        @pl.when(step == steps - 1)
        def _():
            dk_ref[...] = dk_acc[...].astype(BF16)
            dv_ref[...] = dv_acc[...].astype(BF16)

    blk_spec = pl.BlockSpec((BQ_B, HEAD_DIM), lambda h, r: (r, h))
    whole = pl.BlockSpec((T, HEAD_DIM), lambda h, r: (0, h))
    full = jax.ShapeDtypeStruct((T, N_HEADS_B * HEAD_DIM), BF16)
    return _pc(body, name="attn_b_bwd", grid=(N_HEADS_B, steps),
               in_specs=_b_specs(T) + [blk_spec, blk_spec, pl.BlockSpec((None, BQ_B, 1), lambda h, r: (h, r, 0))],
               out_specs=[blk_spec, whole, whole, pl.BlockSpec((None, N_DR, GRID_W, GRID_W), lambda h, r: (h, 0, 0, 0))],
               out_shape=[full, full, full, jax.ShapeDtypeStruct((N_HEADS_B, N_DR, GRID_W, GRID_W), F32)],
               scratch=[pltpu.VMEM((ROWS_B, GRID_W, WK_B), F32), pltpu.VMEM((ROWS_B, GRID_W, WK_B), F32),
                        pltpu.VMEM((T, HEAD_DIM), F32), pltpu.VMEM((T, HEAD_DIM), F32)],
               sem=("parallel", "arbitrary"))(qkv, qkv, qkv, tiles, dy, y, lse)


def _sds(shape, dtype):
    return jax.ShapeDtypeStruct(shape, dtype)


def _behind(value, token):
    return value if token is None else value + token[0, 0]


def _local_step(x, tgt, norm_mix, b_gate, rpb, norm_mlp, norm_final, late_weights, send):
    T, D = x.shape
    nt = T // TM
    PP, PA = ("parallel", "parallel", "arbitrary"), ("arbitrary", "arbitrary", "arbitrary")
    slopes = jnp.asarray(_alibi_slopes())
    sel_np, mask_np = _toeplitz_consts()
    sel, mask = jnp.asarray(sel_np), jnp.asarray(mask_np)

    def rows(w, tm=TM):
        return pl.BlockSpec((tm, w), lambda i, j, k: (i, j))

    def full_rows(tm, w=D):
        return pl.BlockSpec((tm, w), lambda i, j, k: (i, 0))

    def tall(w):
        return pl.BlockSpec((T, w), lambda j, i, k: (0, i))

    def tall_j(w):
        return pl.BlockSpec((T, w), lambda j, i, k: (0, j))

    rvec = pl.BlockSpec((TS, 1), lambda i, j, k: (i, 0))
    gvec = pl.BlockSpec((1, D), lambda i, j, k: (0, 0))

    h1, r1 = _rmsnorm_fwd(x, norm_mix, "rmsnorm_mix")
    rpb_rows = jnp.pad(rpb.reshape(N_HEADS_B * N_DR, N_DC), ((0, 0), (0, 128 - N_DC)))
    tiles = _rpb_expand(rpb_rows, sel, mask).reshape(N_HEADS_B, N_DR, GRID_W, GRID_W)
    (wq,) = late_weights("qkv", (h1, tiles))

    def store_bf16(acc, ex, o, pids):
        o[0][...] = acc.astype(BF16)

    (qkv,) = _mm("qkv_proj", (T // TL, N_DEV // 2, 1),
                 [(h1, full_rows(TL), wq, pl.BlockSpec((2, D, 768), lambda i, j, k: (j, 0, 0)), "nn_shards", None)],
                 [], [(_sds((T, 3 * D), BF16), rows(1536, TL))], store_bf16, None, PP)

    def gate_epi(acc, ex, o, pids):
        o[0][...] = jax.nn.sigmoid(acc + ex[0][...]).astype(BF16)

    late_weights("attention", qkv)
    ya, lse_a, last_src = _dilated_fwd(qkv, slopes)
    yb, lse_b = _attn_b_fwd(qkv, tiles)

    wg, wpa, wpb, wo, token = late_weights("mixer", ya)
    (gates,) = _mm("gate_proj", (T // TL, N_DEV // 2, 1),
                   [(h1, full_rows(TL), wg, pl.BlockSpec((2, D, 512), lambda i, j, k: (j, 0, 0)), "nn_shards", None)],
                   [(_behind(b_gate, token), pl.BlockSpec((1, 1024), lambda i, j, k: (0, j)))],
                   [(_sds((T, 2 * D), BF16), rows(1024, TL))], gate_epi, None, PP)

    def merge_body(ya_ref, wa_ref, yb_ref, wb_ref, ga_ref, gb_ref, pa_ref, pb_ref, mg_ref):
        ya_v, yb_v = ya_ref[...], yb_ref[...]
        for s in range(N_DEV):
            cols = slice(s * 256, (s + 1) * 256)
            pa = _dot(ya_v, wa_ref[s], "nn")
            pb = _dot(yb_v, wb_ref[s], "nn")
            pa_ref[:, cols] = pa.astype(BF16)
            pb_ref[:, cols] = pb.astype(BF16)
            mg_ref[:, cols] = (ga_ref[:, cols] * pa + gb_ref[:, cols] * pb).astype(BF16)

    wo2 = wo.reshape(D, D)
    y_spec = pl.BlockSpec((TS, 512), lambda i: (i, 0))
    wp_spec = pl.BlockSpec((N_DEV, 512, 256), lambda i: (0, 0, 0))
    c_spec = pl.BlockSpec((TS, D), lambda i: (i, 0))
    pa, pb, merged = _pc(merge_body, name="merge_proj", grid=(T // TS,),
                         in_specs=[y_spec, wp_spec, y_spec, wp_spec, c_spec, pl.BlockSpec((TS, D), lambda i: (i, 1))],
                         out_specs=[c_spec, c_spec, c_spec],
                         out_shape=[_sds((T, D), BF16)] * 3, sem=("parallel",))(ya, wpa, yb, wpb, gates, gates)

    def add_res(acc, ex, o, pids):
        o[0][...] = ex[0][...] + acc

    def out_norm_epi(acc, ex, o, pids):
        x2v = ex[0][...] + acc
        r = lax.rsqrt(jnp.mean(x2v * x2v, axis=-1, keepdims=True) + EPS)
        o[0][...] = x2v
        o[1][...] = (x2v * r * ex[1][...]).astype(BF16)
        o[2][...] = r

    x2, h2, r2 = _mm("out_proj", (nt, 1, 1),
                     [(merged, full_rows(TM), wo2, pl.BlockSpec((D, D), lambda i, j, k: (0, 0)), "nn", None)],
                     [(x, full_rows(TM)), (norm_mlp, gvec)],
                     [(_sds((T, D), F32), full_rows(TM)), (_sds((T, D), BF16), full_rows(TM)),
                      (_sds((T, 1), F32), pl.BlockSpec((TM, 1), lambda i, j, k: (i, 0)))], out_norm_epi, None, PP)

    (token,) = late_weights("residual", x2)
    (wu,) = late_weights("up", h2)
    behind = lambda token: [] if token is None else [(token, pl.BlockSpec(memory_space=pl.ANY))]

    def up_epi(acc, ex, o, pids):
        o[0][...] = acc.astype(BF16)
        r = jnp.maximum(acc, 0.0)
        o[1][...] = (r * r).astype(BF16)

    u, act = _mm("mlp_up", (T // TL, N_DEV, 1),
                 [(h2, full_rows(TL), wu, pl.BlockSpec((None, D, 1024), lambda i, j, k: (j, 0, 0)), "nn", None)],
                 behind(token), [(_sds((T, D_FF), BF16), rows(1024, TL))] * 2, up_epi, None, PP)
    (wd,) = late_weights("down", act)
    wd2 = wd.reshape(D_FF, D)
    half_ji = pl.BlockSpec((TS, D // 2), lambda j, i, k: (i, j))
    (x3,) = _mm("mlp_down", (2, T // TS, 1),
                [(act, pl.BlockSpec((TS, D_FF), lambda j, i, k: (i, 0)), wd2,
                  pl.BlockSpec((D_FF, D // 2), lambda j, i, k: (0, j)), "nn", None)],
                [(x2, half_ji)], [(_sds((T, D), F32), half_ji)], add_res, None, PP)

    loss, dx3, dx3b, dg3 = _final_norm_loss(x3, norm_final.reshape(1, D), tgt)

    def du_epi(acc, ex, o, pids):
        o[0][...] = (acc * (2.0 * jnp.maximum(ex[0][...].astype(F32), 0.0))).astype(BF16)

    (du,) = _mm("mlp_down_dx", (T // TL, N_DEV, 1),
                [(dx3b, full_rows(TL), wd2, pl.BlockSpec((1024, D), lambda i, j, k: (j, 0)), "nt", None)],
                [(u, rows(1024, TL))], [(_sds((T, D_FF), BF16), rows(1024, TL))], du_epi, None, PP)
    (dwd,) = _mm("mlp_down_dw", (D // 1024, D_FF // 512, 1), [(act, tall(512), dx3b, tall_j(1024), "tn", None)],
                 [], [(_sds((D_FF, D), BF16), pl.BlockSpec((512, 1024), lambda j, i, k: (i, j)))], store_bf16, None, PP)
    dwd = dwd.reshape(N_DEV, D_FF // N_DEV, D)
    (dwu,) = _mm("mlp_up_dw", (N_DEV, D // 512, 1), [(h2, tall(512), du, tall_j(1024), "tn", None)],
                 [], [(_sds((N_DEV, D, 1024), BF16), pl.BlockSpec((None, 512, 1024), lambda j, i, k: (j, i, 0)))],
                 store_bf16, None, PP)

    def store_f32(acc, ex, o, pids):
        o[0][...] = acc

    out_ji = pl.BlockSpec((TM, 512), lambda j, i, k: (i, j))
    a_ji = lambda w: pl.BlockSpec((TM, w), lambda j, i, k: (i, 0))
    w_ji = lambda n: pl.BlockSpec((N_DEV, 512, n), lambda j, i, k: (0, j, 0))
    (dh2,) = _mm("mlp_up_dx", (D // 512, nt, 1), [(du, a_ji(D_FF), wu, w_ji(1024), "nt_shards", None)],
                 behind(send("mlp", dict(w_down=dwd, w_up=dwu))), [(_sds((T, D), F32), out_ji)], store_f32, None, PP)
    dx2, dx2b, dg2 = _rmsnorm_bwd("rmsnorm_mlp_bwd", dh2, x2, r2, norm_mlp, dx3, True)

    (dwo,) = _mm("out_proj_dw", (D // 1024, D // 512, 1), [(merged, tall(512), dx2b, tall_j(1024), "tn", None)],
                 [], [(_sds((D, D), BF16), pl.BlockSpec((512, 1024), lambda j, i, k: (i, j)))], store_bf16, None, PP)

    def dmerge_epi(acc, ex, o, pids):
        ga, gb, pa_v, pb_v = (e[...].astype(F32) for e in ex)
        o[0][...] = (acc * ga).astype(BF16)
        o[1][...] = (acc * gb).astype(BF16)
        dga = acc * pa_v * ga * (1.0 - ga)
        dgb = acc * pb_v * gb * (1.0 - gb)
        o[2][...] = dga.astype(BF16)
        o[3][...] = dgb.astype(BF16)
        sa = jnp.sum(dga, axis=0, keepdims=True)
        sb = jnp.sum(dgb, axis=0, keepdims=True)
        first = pids[1] == 0

        @pl.when(first)
        def _():
            o[4][...] = sa
            o[5][...] = sb

        @pl.when(jnp.logical_not(first))
        def _():
            o[4][...] += sa
            o[5][...] += sb

    cj = pl.BlockSpec((TL, 512), lambda j, i, k: (i, j))
    bj = pl.BlockSpec((1, 512), lambda j, i, k: (0, j))
    dpa, dpb, dga, dgb, dba, dbb = _mm(
        "out_proj_dx", (4, T // TL, 1),
        [(dx2b, pl.BlockSpec((TL, D), lambda j, i, k: (i, 0)), wo2, pl.BlockSpec((512, D), lambda j, i, k: (j, 0)), "nt", None)],
        [(gates, cj), (gates, pl.BlockSpec((TL, 512), lambda j, i, k: (i, 4 + j))), (pa, cj), (pb, cj)],
        [(_sds((T, D), BF16), cj)] * 4 + [(_sds((1, D), F32), bj)] * 2, dmerge_epi, None,
        ("parallel", "arbitrary", "arbitrary"))

    def store_shards(acc, ex, o, pids):
        for s in range(4):
            o[0][s] = acc[:, s * 256:(s + 1) * 256].astype(BF16)

    def proj_dw(name, y, dp):
        (dw,) = _mm(name, (2, 1, 1), [(y, tall(512), dp, tall_j(1024), "tn", None)],
                    [], [(_sds((N_DEV, 512, 256), BF16), pl.BlockSpec((4, 512, 256), lambda j, i, k: (j, 0, 0)))],
                    store_shards, None, PP)
        return dw

    def proj_dx(name, dp, w):
        (dy,) = _mm(name, (nt, 1, 1),
                    [(dp, full_rows(TM), w, pl.BlockSpec((N_DEV, 512, 256), lambda i, j, k: (0, 0, 0)), "nt_shards", None)],
                    [], [(_sds((T, 512), BF16), full_rows(TM, 512))], store_bf16, None, PP)
        return dy

    def in_dw(name, dact, n):
        def store_pair(acc, ex, o, pids):
            o[0][0] = acc[:, :n].astype(BF16)
            o[0][1] = acc[:, n:].astype(BF16)

        (dw,) = _mm(name, (N_DEV // 2, D // 512, 1), [(h1, tall(512), dact, tall_j(2 * n), "tn", None)],
                    [], [(_sds((N_DEV, D, n), BF16), pl.BlockSpec((2, 512, n), lambda j, i, k: (j, i, 0)))],
                    store_pair, None, PP)
        return dw

    dgate = jnp.concatenate([dga, dgb], axis=1)
    dwg = in_dw("gate_proj_dw", dgate, 512)
    dwpa = proj_dw("proj_a_dw", ya, dpa)
    dwpb = proj_dw("proj_b_dw", yb, dpb)
    token = send("proj", dict(w_gate=dwg, w_out=dwo.reshape(N_DEV, D // N_DEV, D), w_proj_a=dwpa, w_proj_b=dwpb))
    dya = proj_dx("proj_a_dx", dpa, wpa)
    dyb = proj_dx("proj_b_dx", dpb, wpb)

    dq_a, dk_a, dv_a = _dilated_bwd(qkv, last_src, _behind(slopes, token), dya, ya, lse_a)
    dq_b, dk_b, dv_b, dtiles = _attn_b_bwd(qkv, tiles, dyb, yb, lse_b)
    drpb = _rpb_reduce(dtiles.reshape(N_HEADS_B * N_DR, GRID_W * GRID_W), sel)[:, :N_DC]
    dqkv = jnp.concatenate([dq_a, dq_b, dk_a, dk_b, dv_a, dv_b], axis=1)
    dwq = in_dw("qkv_proj_dw", dqkv, 768)
    (dh1,) = _mm("in_proj_dx", (D // 512, nt, 1),
                 [(dqkv, a_ji(3 * D), wq, w_ji(768), "nt_shards", None), (dgate, a_ji(2 * D), wg, w_ji(512), "nt_shards", None)],
                 behind(send("in", dict(w_qkv=dwq))), [(_sds((T, D), F32), out_ji)], store_f32, None, PP)
    grad_x, dg1 = _rmsnorm_bwd("rmsnorm_mix_bwd", dh1, x, r1, norm_mix, dx2, False)

    small = dict(norm_mix=dg1, b_gate=jnp.concatenate([dba, dbb], axis=1), rpb=drpb, norm_mlp=dg2, norm_final=dg3)
    return loss, grad_x, small


BIG = ("w_qkv", "w_gate", "w_proj_a", "w_proj_b", "w_out", "w_up", "w_down")
SMALL = ("norm_mix", "b_gate", "rpb", "norm_mlp", "norm_final")
SMALL_ROWS = 96


def _place():
    return lax.axis_index("x"), lax.axis_index("y"), lax.axis_index("c")


def _peer(x, y, c, mask):
    return x ^ (mask >> 2), y ^ ((mask >> 1) & 1), c ^ (mask & 1)


_HBM = pl.BlockSpec(memory_space=pltpu.HBM)
_SEM = pl.BlockSpec(memory_space=pltpu.SEMAPHORE)
_ANY = pl.BlockSpec(memory_space=pl.ANY)
_EFFECT = pltpu.SideEffectType.DATAFLOW_SIDE_EFFECTING


def _hbm(a):
    return pltpu.with_memory_space_constraint(a, pltpu.HBM)


def _other_chips(x, y):
    return [(1 - x, y), (x, 1 - y), (1 - x, 1 - y)]


def _block(x, y, c):
    return 4 * x + 2 * y + c


def _rdma(src, dst, send_sem, recv_sem, to):
    return pltpu.make_async_remote_copy(src_ref=src, dst_ref=dst, send_sem=send_sem, recv_sem=recv_sem, device_id=to,
                                        device_id_type=MESH)


def _gather_start(name, shards):
    n = len(shards)

    def body(*refs):
        srcs, lands = refs[:n], refs[n:2 * n]
        outs = refs[2 * n:]
        send, recv_d2d, recv_ici, local = outs[:n], outs[n:2 * n], outs[2 * n:3 * n], outs[3 * n:4 * n]
        token_ref = outs[6 * n]
        x, y, c = _place()
        for a in range(n):
            mine = lands[a].at[_block(x, y, c)]
            _rdma(srcs[a], mine, send[a].at[0], recv_d2d[a].at[0], (x, y, 1 - c)).start()
            for j, chip in enumerate(_other_chips(x, y)):
                _rdma(srcs[a], mine, send[a].at[1 + j], recv_ici[a].at[j], (*chip, c)).start()
        for a in range(n):
            pltpu.make_async_copy(srcs[a], lands[a].at[_block(x, y, c)], local[a].at[0]).start()
        token_ref[...] = jnp.zeros_like(token_ref)

    lands = [_hbm(lax.empty((N_DEV,) + s.shape, s.dtype)) for s in shards]
    srcs = [_hbm(s) for s in shards]
    dma = pltpu.SemaphoreType.DMA
    res = pl.pallas_call(
        body, name=name,
        out_shape=[dma((4,))] * n + [dma((1,))] * n + [dma((3,))] * n + [dma((1,))] * n
        + [pltpu.HBM(s.shape, s.dtype) for s in srcs] + [pltpu.HBM(l.shape, l.dtype) for l in lands]
        + [jax.ShapeDtypeStruct((8, 128), F32)],
        in_specs=[_HBM] * (2 * n), out_specs=[_SEM] * (4 * n) + [_HBM] * (2 * n) + [pl.BlockSpec(memory_space=pltpu.VMEM)],
        input_output_aliases={i: 4 * n + i for i in range(2 * n)},
        compiler_params=pltpu.CompilerParams(has_side_effects=_EFFECT))(*srcs, *lands)
    keys = ("send", "recv_d2d", "recv_ici", "local", "src", "land")
    return [dict(zip(keys, (res[k * n + a] for k in range(6)))) for a in range(n)], res[6 * n]


def _gather_forward(name, handles, after):
    n = len(handles)
    after = after if isinstance(after, (tuple, list)) else (after,)

    def body(*refs):
        lands, recv_ici = refs[:n], refs[n:2 * n]
        outs = refs[2 * n + len(after):]
        fwd_send, fwd_recv, token_ref = outs[n:2 * n], outs[2 * n:3 * n], outs[3 * n]
        x, y, c = _place()
        for a in range(n):
            for j, chip in enumerate(_other_chips(x, y)):
                blk = lands[a].at[_block(*chip, c)]
                _rdma(blk, blk, fwd_send[a].at[j], recv_ici[a].at[j], (*chip, c)).wait_recv()
                _rdma(blk, blk, fwd_send[a].at[j], fwd_recv[a].at[j], (x, y, 1 - c)).start()
        token_ref[...] = jnp.zeros_like(token_ref)

    dma = pltpu.SemaphoreType.DMA
    lands = [h["land"] for h in handles]
    res = pl.pallas_call(
        body, name=name,
        out_shape=[pltpu.HBM(l.shape, l.dtype) for l in lands] + [dma((3,))] * (2 * n) + [jax.ShapeDtypeStruct((8, 128), F32)],
        in_specs=[_HBM] * n + [_SEM] * n + [_ANY] * len(after),
        out_specs=[_HBM] * n + [_SEM] * (2 * n) + [pl.BlockSpec(memory_space=pltpu.VMEM)],
        input_output_aliases={i: i for i in range(n)},
        compiler_params=pltpu.CompilerParams(has_side_effects=_EFFECT))(*lands, *[h["recv_ici"] for h in handles], *after)
    return [dict(h, land=res[a], fwd_send=res[n + a], fwd_recv=res[2 * n + a]) for a, h in enumerate(handles)], res[3 * n]


def _gather_wait(name, handles, after):
    n = len(handles)

    def body(*refs):
        srcs, lands = refs[:n], refs[n:2 * n]
        send, recv_d2d, local, fwd_send, fwd_recv = (refs[(2 + k) * n:(3 + k) * n] for k in range(5))
        x, y, c = _place()
        sibling = (x, y, 1 - c)
        for a in range(n):
            mine = lands[a].at[_block(x, y, c)]
            _rdma(srcs[a], mine, send[a].at[0], recv_d2d[a].at[0], sibling).wait_send()
            _rdma(srcs[a], lands[a].at[_block(*sibling)], send[a].at[0], recv_d2d[a].at[0], sibling).wait_recv()
            pltpu.make_async_copy(srcs[a], mine, local[a].at[0]).wait()
            for j, chip in enumerate(_other_chips(x, y)):
                _rdma(srcs[a], mine, send[a].at[1 + j], fwd_recv[a].at[j], (*chip, c)).wait_send()
                blk = lands[a].at[_block(*chip, c)]
                _rdma(blk, blk, fwd_send[a].at[j], fwd_recv[a].at[j], sibling).wait_send()
                got = lands[a].at[_block(*chip, 1 - c)]
                _rdma(got, got, fwd_send[a].at[j], fwd_recv[a].at[j], sibling).wait_recv()

    srcs = [h["src"] for h in handles]
    lands = [h["land"] for h in handles]
    sems = [h[k] for k in ("send", "recv_d2d", "local", "fwd_send", "fwd_recv") for h in handles]
    res = pl.pallas_call(
        body, name=name, out_shape=[pltpu.HBM(s.shape, s.dtype) for s in srcs] + [pltpu.HBM(l.shape, l.dtype) for l in lands],
        in_specs=[_HBM] * (2 * n) + [_SEM] * (5 * n) + [_ANY], out_specs=[_HBM] * (2 * n),
        input_output_aliases={i: i for i in range(2 * n)},
        compiler_params=pltpu.CompilerParams(has_side_effects=_EFFECT))(*srcs, *lands, *sems, after)
    return list(res[n:])


def _gather_small(small):
    def body(in_ref, out_ref, send_sems, recv_sems, local_sem):
        x, y, c = _place()
        me = 4 * x + 2 * y + c
        copies = [pltpu.make_async_copy(in_ref, out_ref.at[me], local_sem)]
        for mask in range(1, N_DEV):
            copies.append(pltpu.make_async_remote_copy(
                src_ref=in_ref, dst_ref=out_ref.at[me], send_sem=send_sems.at[mask - 1],
                recv_sem=recv_sems.at[mask - 1], device_id=_peer(x, y, c, mask), device_id_type=MESH))
        for cp in copies:
            cp.start()
        for cp in copies:
            cp.wait()

    any_spec = pl.BlockSpec(memory_space=pl.ANY)
    return pl.pallas_call(
        body, name="gather_small", in_specs=[any_spec], out_specs=any_spec,
        out_shape=jax.ShapeDtypeStruct((N_DEV,) + small.shape, small.dtype),
        scratch_shapes=[pltpu.SemaphoreType.DMA((7,)), pltpu.SemaphoreType.DMA((7,)), pltpu.SemaphoreType.DMA])(small)


def _send_copy(src_refs, land_refs, send_sems, recv_sems, a, mask, scatter, x, y, c):
    px, py, pc = _peer(x, y, c, mask)
    src = src_refs[a].at[4 * px + 2 * py + pc] if scatter else src_refs[a]
    return pltpu.make_async_remote_copy(
        src_ref=src, dst_ref=land_refs[a].at[4 * x + 2 * y + c], send_sem=send_sems[a].at[mask - 1],
        recv_sem=recv_sems[a].at[mask - 1], device_id=(px, py, pc), device_id_type=MESH)


def _local_copy(src_refs, land_refs, local_sems, a, scatter, x, y, c):
    me = 4 * x + 2 * y + c
    return pltpu.make_async_copy(src_refs[a].at[me] if scatter else src_refs[a], land_refs[a].at[me], local_sems[a].at[0])


def _send_start(name, srcs, scatter, after=None):
    n = len(srcs)
    n_in = 2 * n + (after is not None)
    shapes = [s.shape[1:] if scatter else s.shape for s in srcs]

    def body(*refs):
        src_refs, land_refs = refs[:n], refs[n:2 * n]
        outs = refs[n_in:]
        send_sems, recv_sems, local_sems = outs[:n], outs[n:2 * n], outs[2 * n:3 * n]
        token_ref = outs[5 * n]
        x, y, c = _place()
        for a in range(n):
            for mask in range(1, N_DEV):
                _send_copy(src_refs, land_refs, send_sems, recv_sems, a, mask, scatter, x, y, c).start()
        for a in range(n):
            _local_copy(src_refs, land_refs, local_sems, a, scatter, x, y, c).start()
        token_ref[...] = jnp.zeros_like(token_ref)

    lands = [pltpu.with_memory_space_constraint(lax.empty((N_DEV,) + sh, s.dtype), pltpu.HBM) for sh, s in zip(shapes, srcs)]
    srcs = [pltpu.with_memory_space_constraint(s, pltpu.HBM) for s in srcs]
    res = pl.pallas_call(
        body, name=name,
        out_shape=[pltpu.SemaphoreType.DMA((7,))] * (2 * n) + [pltpu.SemaphoreType.DMA((1,))] * n
        + [pltpu.HBM(s.shape, s.dtype) for s in srcs] + [pltpu.HBM(l.shape, l.dtype) for l in lands]
        + [jax.ShapeDtypeStruct((8, 128), F32)],
        in_specs=[_HBM] * (2 * n) + [pl.BlockSpec(memory_space=pl.ANY)] * (n_in - 2 * n),
        out_specs=[_SEM] * (3 * n) + [_HBM] * (2 * n) + [pl.BlockSpec(memory_space=pltpu.VMEM)],
        input_output_aliases={i: 3 * n + i for i in range(2 * n)},
        compiler_params=pltpu.CompilerParams(has_side_effects=_EFFECT))(*srcs, *lands, *([] if after is None else [after]))
    handles = [tuple(res[k * n + a] for k in range(5)) for a in range(n)]
    return handles, res[5 * n]


def _send_wait(name, handles, scatter, after):
    n = len(handles)

    def body(*refs):
        src_refs, land_refs = refs[:n], refs[n:2 * n]
        send_sems, recv_sems, local_sems = refs[2 * n:3 * n], refs[3 * n:4 * n], refs[4 * n:5 * n]
        x, y, c = _place()
        for a in range(n):
            for mask in range(1, N_DEV):
                cp = _send_copy(src_refs, land_refs, send_sems, recv_sems, a, mask, scatter, x, y, c)
                cp.wait_send()
                cp.wait_recv()
            _local_copy(src_refs, land_refs, local_sems, a, scatter, x, y, c).wait()

    srcs = [h[3] for h in handles]
    lands = [h[4] for h in handles]
    res = pl.pallas_call(
        body, name=name, out_shape=[pltpu.HBM(s.shape, s.dtype) for s in srcs] + [pltpu.HBM(l.shape, l.dtype) for l in lands],
        in_specs=[_HBM] * (2 * n) + [_SEM] * (3 * n) + [pl.BlockSpec(memory_space=pl.ANY)], out_specs=[_HBM] * (2 * n),
        input_output_aliases={i: i for i in range(2 * n)},
        compiler_params=pltpu.CompilerParams(has_side_effects=_EFFECT))(
            *srcs, *lands, *[h[0] for h in handles], *[h[1] for h in handles], *[h[2] for h in handles], after)
    return list(res[n:])


def _adamw(name, parts, w, m, v):
    K, n = w.shape
    tr = min(K, 128)

    def body(p_ref, w_ref, m_ref, v_ref, g_ref, d_ref, nm_ref, nv_ref):
        g = p_ref[0].astype(F32)
        for i in range(1, N_DEV):
            g = g + p_ref[i].astype(F32)
        nm = ADAM_B1 * m_ref[...] + (1.0 - ADAM_B1) * g
        nv = ADAM_B2 * v_ref[...] + (1.0 - ADAM_B2) * jnp.square(g)
        m_hat = nm / (1.0 - ADAM_B1 ** ADAM_STEP)
        v_hat = nv / (1.0 - ADAM_B2 ** ADAM_STEP)
        g_ref[...] = g
        d_ref[...] = -ADAM_LR * (m_hat / (jnp.sqrt(v_hat) + ADAM_EPS) + ADAM_WD * w_ref[...])
        nm_ref[...] = nm
        nv_ref[...] = nv

    blk = pl.BlockSpec((tr, n), lambda i: (i, 0))
    return _pc(body, name=name, grid=(K // tr,), in_specs=[pl.BlockSpec((N_DEV, tr, n), lambda i: (0, i, 0)), blk, blk, blk],
               out_specs=[blk] * 4, out_shape=[jax.ShapeDtypeStruct((K, n), F32)] * 4, sem=("parallel",))(parts, w, m, v)


def _pack_small(vals):
    flat = jnp.concatenate([v.reshape(-1) for v in vals])
    return jnp.pad(flat, (0, SMALL_ROWS * 128 - flat.shape[0])).reshape(SMALL_ROWS, 128)


def _unpack_small(packed, shapes):
    flat, out, off = packed.reshape(-1), [], 0
    for s in shapes:
        size = int(np.prod(s))
        out.append(flat[off:off + size].reshape(s))
        off += size
    return out


def kernel(x, norm_mix, w_qkv, w_gate, b_gate, rpb, w_proj_a, w_proj_b, w_out, norm_mlp, w_up, w_down, norm_final, loss_target, m_norm_mix, m_w_qkv, m_w_gate, m_b_gate, m_rpb, m_w_proj_a, m_w_proj_b, m_w_out, m_norm_mlp, m_w_up, m_w_down, m_norm_final, v_norm_mix, v_w_qkv, v_w_gate, v_b_gate, v_rpb, v_w_proj_a, v_w_proj_b, v_w_out, v_norm_mlp, v_w_up, v_w_down, v_norm_final):
    w = dict(norm_mix=norm_mix, w_qkv=w_qkv, w_gate=w_gate, b_gate=b_gate, rpb=rpb, w_proj_a=w_proj_a, w_proj_b=w_proj_b,
             w_out=w_out, norm_mlp=norm_mlp, w_up=w_up, w_down=w_down, norm_final=norm_final)
    m = dict(norm_mix=m_norm_mix, w_qkv=m_w_qkv, w_gate=m_w_gate, b_gate=m_b_gate, rpb=m_rpb, w_proj_a=m_w_proj_a,
             w_proj_b=m_w_proj_b, w_out=m_w_out, norm_mlp=m_norm_mlp, w_up=m_w_up, w_down=m_w_down, norm_final=m_norm_final)
    v = dict(norm_mix=v_norm_mix, w_qkv=v_w_qkv, w_gate=v_w_gate, b_gate=v_b_gate, rpb=v_rpb, w_proj_a=v_w_proj_a,
             w_proj_b=v_w_proj_b, w_out=v_w_out, norm_mlp=v_norm_mlp, w_up=v_w_up, w_down=v_w_down, norm_final=v_norm_final)
    order = ("norm_mix", "w_qkv", "w_gate", "b_gate", "rpb", "w_proj_a", "w_proj_b", "w_out", "norm_mlp", "w_up",
             "w_down", "norm_final")

    stages = dict(qkv=("w_qkv",), gate=("w_gate", "w_proj_a", "w_proj_b", "w_out"), mlp=("w_up", "w_down"))
    first, token = _gather_start("gather_start_qkv", [w["w_qkv"][0].astype(BF16)])
    names = [n for s in ("gate", "mlp") for n in stages[s]]
    rest, token = _gather_start("gather_start", [_behind(w[n][0], token).astype(BF16) for n in names])
    handles = dict(zip(names, rest), w_qkv=first[0])

    def late_weights(stage, after):
        if stage == "qkv":
            forwarded, _ = _gather_forward("gather_forward_qkv", [handles["w_qkv"]], after)
            return tuple(_gather_wait("gather_wait_qkv", forwarded, after[0]))
        if stage == "attention":
            handles["gate"], tok = _gather_forward("gather_forward_gate", [handles[n] for n in stages["gate"]], after)
            return (tok,)
        if stage == "mixer":
            handles["up"], tok = _gather_forward("gather_forward_up", [handles["w_up"]], after)
            return tuple(_gather_wait("gather_wait_gate", handles["gate"], after)) + (tok,)
        if stage == "residual":
            handles["down"], tok = _gather_forward("gather_forward_down", [handles["w_down"]], after)
            return (tok,)
        return tuple(_gather_wait("gather_wait_" + stage, handles[stage], after))

    sent = {}

    def send(stage, grads):
        handles, tok = _send_start("grads_start_" + stage, list(grads.values()), True)
        sent[stage] = (tuple(grads), handles)
        return tok

    loss, grad_x, small = _local_step(x[0], loss_target[0], _behind(norm_mix, token), b_gate, rpb[0], norm_mlp, norm_final,
                                      late_weights, send)

    grad, delta, new_m, new_v = {}, {}, {}, {}
    after = grad_x
    for stage in ("mlp", "proj", "in"):
        names, handles = sent[stage]
        for n, parts in zip(names, _send_wait("grads_wait_" + stage, handles, True, after)):
            res = _adamw("adamw_" + n, parts, w[n][0], m[n][0], v[n][0])
            grad[n], delta[n], new_m[n], new_v[n] = (r[None] for r in res)
            after = res[1]
    res = _adamw("adamw_small", _gather_small(_pack_small([small[n] for n in SMALL])),
                 _pack_small([w[n] for n in SMALL]), _pack_small([m[n] for n in SMALL]), _pack_small([v[n] for n in SMALL]))
    shapes = [w[n].shape for n in SMALL]
    for tree, packed in zip((grad, delta, new_m, new_v), res):
        for n, val in zip(SMALL, _unpack_small(packed, shapes)):
            tree[n] = val

    total = lax.psum(loss[0, 0], ("x", "y", "c"))
    return (total, grad_x[None], *[grad[n] for n in order], *[delta[n] for n in order],
            *[new_m[n] for n in order], *[new_v[n] for n in order])
```

```python
import numpy as np
import jax
import jax.numpy as jnp
from jax import lax
from jax.experimental import pallas as pl
from jax.experimental.pallas import tpu as pltpu

F32 = jnp.float32
BF16 = jnp.bfloat16
MESH = pl.DeviceIdType.MESH

D_MODEL = 2048
HEAD_DIM = 128
N_HEADS = 16
N_HEADS_A = 12
N_HEADS_B = 4
DILATIONS = (1, 4, 16)
HALF_WINDOW = 64
GRID_W = 64
NA_ROWS = 8
NA_COLS = 16
D_FF = 4 * D_MODEL
N_DEV = 8
EPS = 1e-6
NEG = -1e30
SCALE = HEAD_DIM ** -0.5
ADAM_LR, ADAM_B1, ADAM_B2, ADAM_EPS, ADAM_WD, ADAM_STEP = 0.001, 0.9, 0.999, 1e-08, 0.01, 10

VMEM_LIMIT = 56 * 1024 * 1024
TM = 512
TL = 1024
TS = 256
BQ_A = 128


def _pc(body, *, name, grid, in_specs, out_specs, out_shape, scratch=(), sem=None):
    return pl.pallas_call(
        body, name=name, grid=grid, in_specs=in_specs, out_specs=out_specs, out_shape=out_shape,
        scratch_shapes=list(scratch),
        compiler_params=pltpu.CompilerParams(dimension_semantics=sem, vmem_limit_bytes=VMEM_LIMIT))


_DOT_DIMS = {"nn": (((1,), (0,)), ((), ())), "nt": (((1,), (1,)), ((), ())), "tn": (((0,), (0,)), ((), ()))}


def _dot(a, b, mode, precision=None):
    if mode == "nn_shards":
        return jnp.concatenate([_dot(a, b[s], "nn") for s in range(b.shape[0])], axis=1)
    if mode == "nt_shards":
        n = b.shape[2]
        out = _dot(a[:, :n], b[0], "nt")
        for s in range(1, b.shape[0]):
            out = out + _dot(a[:, s * n:(s + 1) * n], b[s], "nt")
        return out
    return lax.dot_general(a, b, _DOT_DIMS[mode], preferred_element_type=F32, precision=precision)


def _mm(name, grid, pairs, extras, outs, epilogue, acc_shape, sem):
    n_pairs, n_extra, n_out = len(pairs), len(extras), len(outs)
    k_axis = len(grid) - 1
    nk = grid[k_axis]

    def body(*refs):
        pids = [pl.program_id(ax) for ax in range(len(grid))]
        ab = refs[:2 * n_pairs]
        ex = refs[2 * n_pairs:2 * n_pairs + n_extra]
        o = refs[2 * n_pairs + n_extra:2 * n_pairs + n_extra + n_out]
        part = None
        for p, pair in enumerate(pairs):
            a = ab[2 * p][...]
            if pair[5] is not None:
                a = pair[5](a)
            d = _dot(a, ab[2 * p + 1][...], pair[4])
            part = d if part is None else part + d
        if nk == 1:
            epilogue(part, ex, o, pids)
        else:
            acc = refs[-1]
            k = pids[k_axis]

            @pl.when(k == 0)
            def _():
                acc[...] = part

            @pl.when(k > 0)
            def _():
                acc[...] += part

            @pl.when(k == nk - 1)
            def _():
                epilogue(acc[...], ex, o, pids)

    operands, in_specs = [], []
    for a, a_spec, b, b_spec, _, _ in pairs:
        operands += [a, b]
        in_specs += [a_spec, b_spec]
    for e, e_spec in extras:
        operands.append(e)
        in_specs.append(e_spec)
    res = _pc(body, name=name, grid=grid, in_specs=in_specs, out_specs=[s for _, s in outs],
              out_shape=[o for o, _ in outs], scratch=[pltpu.VMEM(acc_shape, F32)] if nk > 1 else [], sem=sem)(*operands)
    return res


def _rmsnorm_fwd(x, g, name):
    T, D = x.shape

    def body(x_ref, g_ref, h_ref, r_ref):
        xv = x_ref[...]
        r = lax.rsqrt(jnp.mean(xv * xv, axis=-1, keepdims=True) + EPS)
        h_ref[...] = (xv * r * g_ref[...]).astype(BF16)
        r_ref[...] = r

    return _pc(body, name=name, grid=(T // TM,),
               in_specs=[pl.BlockSpec((TM, D), lambda i: (i, 0)), pl.BlockSpec((1, D), lambda i: (0, 0))],
               out_specs=[pl.BlockSpec((TM, D), lambda i: (i, 0)), pl.BlockSpec((TM, 1), lambda i: (i, 0))],
               out_shape=[jax.ShapeDtypeStruct((T, D), BF16), jax.ShapeDtypeStruct((T, 1), F32)],
               sem=("parallel",))(x, g)


def _rmsnorm_bwd_block(dh, x, r, g, dres):
    xh = x * r
    dxh = dh * g
    dx = dres + r * (dxh - xh * jnp.mean(dxh * xh, axis=-1, keepdims=True))
    return dx, jnp.sum(dh * xh, axis=0, keepdims=True)


def _rmsnorm_bwd(name, dh, x, r, g, dres, want_bf16):
    T, D = x.shape

    def body(dh_ref, x_ref, r_ref, g_ref, dres_ref, *outs):
        dx, dgp = _rmsnorm_bwd_block(dh_ref[...], x_ref[...], r_ref[...], g_ref[...], dres_ref[...])
        outs[0][...] = dx
        if want_bf16:
            outs[1][...] = dx.astype(BF16)
        first = pl.program_id(0) == 0

        @pl.when(first)
        def _():
            outs[-1][...] = dgp

        @pl.when(jnp.logical_not(first))
        def _():
            outs[-1][...] += dgp

    row = pl.BlockSpec((TS, D), lambda i: (i, 0))
    vec = pl.BlockSpec((1, D), lambda i: (0, 0))
    n_row = 2 if want_bf16 else 1
    return _pc(body, name=name, grid=(T // TS,), in_specs=[row, row, pl.BlockSpec((TS, 1), lambda i: (i, 0)), vec, row],
               out_specs=[row] * n_row + [vec],
               out_shape=[jax.ShapeDtypeStruct((T, D), F32)] + [jax.ShapeDtypeStruct((T, D), BF16)] * (n_row - 1)
               + [jax.ShapeDtypeStruct((1, D), F32)], sem=("arbitrary",))(dh, x, r, g, dres)


def _final_norm_loss(x3, g, tgt):
    T, D = x3.shape

    def body(x_ref, g_ref, t_ref, loss_ref, dx_ref, dxb_ref, dg_ref):
        i = pl.program_id(0)
        xv = x_ref[...]
        gv = g_ref[...]
        r = lax.rsqrt(jnp.mean(xv * xv, axis=-1, keepdims=True) + EPS)
        xh = xv * r
        e = xh * gv - t_ref[...]
        part = 0.5 * jnp.sum(jnp.mean(e * e, axis=-1, keepdims=True))
        dy = e * (1.0 / D)
        dxh = dy * gv
        dx = r * (dxh - xh * jnp.mean(dxh * xh, axis=-1, keepdims=True))
        dx_ref[...] = dx
        dxb_ref[...] = dx.astype(BF16)
        dgp = jnp.sum(dy * xh, axis=0, keepdims=True)

        @pl.when(i == 0)
        def _():
            loss_ref[...] = jnp.zeros(loss_ref.shape, F32) + part
            dg_ref[...] = dgp

        @pl.when(i > 0)
        def _():
            loss_ref[...] += part
            dg_ref[...] += dgp

    row = pl.BlockSpec((TM, D), lambda i: (i, 0))
    vec = pl.BlockSpec((1, D), lambda i: (0, 0))
    return _pc(body, name="final_norm_loss", grid=(T // TM,), in_specs=[row, vec, row],
               out_specs=[pl.BlockSpec((8, 128), lambda i: (0, 0)), row, row, vec],
               out_shape=[jax.ShapeDtypeStruct((8, 128), F32), jax.ShapeDtypeStruct((T, D), F32),
                          jax.ShapeDtypeStruct((T, D), BF16), jax.ShapeDtypeStruct((1, D), F32)],
               sem=("arbitrary",))(x3, g, tgt)


def _alibi_slopes():
    return np.asarray(2.0 ** (-8.0 * np.arange(1, N_HEADS_A + 1) / N_HEADS_A), dtype=np.float32)


def _col_to_row(col):
    return jnp.transpose(jnp.broadcast_to(col, (BQ_A, BQ_A)))[0:1, :]


def _row_to_col(row):
    return jnp.transpose(jnp.broadcast_to(row, (BQ_A, BQ_A)))[:, 0:1]


def _row_spec(rows_per_step, first=0):
    return pl.BlockSpec((None, rows_per_step // BQ_A, BQ_A), lambda j, i: (first + j, i, 0))


def _a_windows(T):
    half = [HALF_WINDOW * d for d in DILATIONS]
    return half, [min(BQ_A + 2 * w, T) for w in half]


def _a_bias(slope, g, offset, wk, half):
    d = DILATIONS[g]
    dlt = offset + lax.broadcasted_iota(jnp.int32, (BQ_A, wk), 1) - lax.broadcasted_iota(jnp.int32, (BQ_A, wk), 0)
    ad = jnp.maximum(dlt, -dlt)
    ok = (ad <= half) & ((ad & (d - 1)) == 0)
    return jnp.where(ok, -slope * ad.astype(F32), NEG)


def _a_scores(q, kw, slope, g, offset, wk, half):
    return _dot(q, kw, "nt") * SCALE + _a_bias(slope, g, offset, wk, half)


N_JOINT = 2


SUB_J = 8
BS_J = SUB_J * BQ_A


def _a_qkv_specs(T):
    specs = []
    for g in range(N_JOINT):
        specs.append(pl.BlockSpec((BS_J, HEAD_DIM), lambda j, i, g=g: (i, 4 * g + j)))
        specs.append(pl.BlockSpec((T, HEAD_DIM), lambda j, i, g=g: (0, N_HEADS + 4 * g + j)))
        specs.append(pl.BlockSpec((T, HEAD_DIM), lambda j, i, g=g: (0, 2 * N_HEADS + 4 * g + j)))
    return specs


def _a_window(step, u, g, T, half, wks):
    t0 = (step * SUB_J + u) * BQ_A
    ws = pl.multiple_of(jnp.clip(t0 - half[g], 0, T - wks[g]), 64)
    return pl.ds(ws, wks[g]), ws - t0


def _attn_a_fwd(qkv, slopes, o_last, lse_last):
    T = qkv.shape[0]
    half, wks = _a_windows(T)

    def body(sl_ref, *refs):
        qkv_refs, (o_ref, l_ref, y_ref, lse_ref) = refs[:3 * N_JOINT], refs[3 * N_JOINT:]
        j = pl.program_id(0)
        for u in range(SUB_J):
            rows = slice(u * BQ_A, (u + 1) * BQ_A)
            ss, vws = [], []
            lse_last = _row_to_col(l_ref[u:u + 1, :])
            m = lse_last
            for g in range(N_JOINT):
                q_ref, k_ref, v_ref = qkv_refs[3 * g:3 * g + 3]
                win, offset = _a_window(pl.program_id(1), u, g, T, half, wks)
                ss.append(_a_scores(q_ref[rows, :], k_ref[win, :], sl_ref[4 * g + j], g, offset, wks[g], half[g]))
                vws.append(v_ref[win, :])
                m = jnp.maximum(m, jnp.max(ss[g], axis=-1, keepdims=True))
            den = jnp.exp(lse_last - m)
            acc = den * o_ref[rows, :].astype(F32)
            for g in range(N_JOINT):
                p = jnp.exp(ss[g] - m)
                den = den + jnp.sum(p, axis=-1, keepdims=True)
                acc = acc + _dot(p.astype(BF16), vws[g], "nn")
            y_ref[rows, :] = (acc / den).astype(BF16)
            lse_ref[u:u + 1, :] = _col_to_row(m + jnp.log(den))

    blk = pl.BlockSpec((BS_J, HEAD_DIM), lambda j, i: (i, j))
    col = _row_spec(BS_J)
    return _pc(body, name="attn_a_fwd", grid=(4, T // BS_J),
               in_specs=[pl.BlockSpec(memory_space=pltpu.SMEM)] + _a_qkv_specs(T) + [blk, col], out_specs=[blk, col],
               out_shape=[jax.ShapeDtypeStruct((T, 4 * HEAD_DIM), BF16), jax.ShapeDtypeStruct((4, T // BQ_A, BQ_A), F32)],
               sem=("parallel", "parallel"))(slopes, *([qkv] * (3 * N_JOINT)), o_last, lse_last)


def _attn_a_bwd(qkv, slopes, dy, y, lse):
    T = qkv.shape[0]
    steps = T // BS_J
    half, wks = _a_windows(T)

    def body(sl_ref, *refs):
        qkv_refs = refs[:3 * N_JOINT]
        dy_ref, y_ref, lse_ref, dq_ref, dk_ref, dv_ref, delta_ref, dk_acc, dv_acc = refs[3 * N_JOINT:]
        j = pl.program_id(0)
        step = pl.program_id(1)

        @pl.when(step == 0)
        def _():
            dk_acc[...] = jnp.zeros_like(dk_acc)
            dv_acc[...] = jnp.zeros_like(dv_acc)

        for u in range(SUB_J):
            rows = slice(u * BQ_A, (u + 1) * BQ_A)
            dyv = dy_ref[rows, :]
            delta = jnp.sum(dyv.astype(F32) * y_ref[rows, :].astype(F32), axis=-1, keepdims=True)
            delta_ref[u:u + 1, :] = _col_to_row(delta)
            lse_v = _row_to_col(lse_ref[u:u + 1, :])
            for g in range(N_JOINT):
                q_ref, k_ref, v_ref = qkv_refs[3 * g:3 * g + 3]
                win, offset = _a_window(step, u, g, T, half, wks)
                qv = q_ref[rows, :]
                kw = k_ref[win, :]
                s = _a_scores(qv, kw, sl_ref[4 * g + j], g, offset, wks[g], half[g])
                p = jnp.exp(s - lse_v)
                dp = _dot(dyv, v_ref[win, :], "nt")
                ds = (p * (dp - delta)).astype(BF16)
                dq_ref[g, rows, :] = (_dot(ds, kw, "nn") * SCALE).astype(BF16)
                dk_acc[g, win, :] += _dot(ds, qv, "tn") * SCALE
                dv_acc[g, win, :] += _dot(p.astype(BF16), dyv, "tn")

        @pl.when(step == steps - 1)
        def _():
            dk_ref[...] = dk_acc[...].astype(BF16)
            dv_ref[...] = dv_acc[...].astype(BF16)

    blk = pl.BlockSpec((BS_J, HEAD_DIM), lambda j, i: (i, j))
    col = _row_spec(BS_J)
    whole = pl.BlockSpec((N_JOINT, T, HEAD_DIM), lambda j, i: (0, 0, j))
    full = jax.ShapeDtypeStruct((N_JOINT, T, 4 * HEAD_DIM), BF16)
    return _pc(body, name="attn_a_bwd", grid=(4, steps),
               in_specs=[pl.BlockSpec(memory_space=pltpu.SMEM)] + _a_qkv_specs(T) + [blk, blk, col],
               out_specs=[pl.BlockSpec((N_JOINT, BS_J, HEAD_DIM), lambda j, i: (0, i, j)), whole, whole, col],
               out_shape=[full, full, full, jax.ShapeDtypeStruct((4, T // BQ_A, BQ_A), F32)],
               scratch=[pltpu.VMEM((N_JOINT, T, HEAD_DIM), F32), pltpu.VMEM((N_JOINT, T, HEAD_DIM), F32)],
               sem=("parallel", "arbitrary"))(slopes, *([qkv] * (3 * N_JOINT)), dy, y, lse)


SUB_A = 8


def _to_residue_major(a, d, axis=0):
    if d == 1:
        return a
    sh = a.shape
    a = a.reshape(sh[:axis] + (sh[axis] // d, d) + sh[axis + 1:])
    return jnp.swapaxes(a, axis, axis + 1).reshape(sh)


def _to_token_order(a, d, axis=0):
    if d == 1:
        return a
    sh = a.shape
    a = a.reshape(sh[:axis] + (d, sh[axis] // d) + sh[axis + 1:])
    return jnp.swapaxes(a, axis, axis + 1).reshape(sh)


def _band_bias(slope, offset, wk):
    dlt = offset + lax.broadcasted_iota(jnp.int32, (BQ_A, wk), 1) - lax.broadcasted_iota(jnp.int32, (BQ_A, wk), 0)
    ad = jnp.maximum(dlt, -dlt)
    return jnp.where(ad <= HALF_WINDOW, -slope * ad.astype(F32), NEG)


def _band_window(step, u, seg, wk):
    t0 = (step * SUB_A + u) * BQ_A
    lo = (t0 // seg) * seg
    ws = pl.multiple_of(jnp.clip(t0 - HALF_WINDOW, lo, lo + seg - wk), 64)
    return slice(u * BQ_A, (u + 1) * BQ_A), pl.ds(ws, wk), ws - t0


def _band_specs(T, cols):
    bs = SUB_A * BQ_A
    assert T % bs == 0 and T // max(DILATIONS) >= BQ_A, "a query sub-block must lie inside one segment"
    return [pl.BlockSpec(memory_space=pltpu.SMEM),
            pl.BlockSpec((bs, HEAD_DIM), lambda j, i: (i, cols[0] + j)),
            pl.BlockSpec((T, HEAD_DIM), lambda j, i: (0, cols[1] + j)),
            pl.BlockSpec((T, HEAD_DIM), lambda j, i: (0, cols[2] + j))]


def _banded_fwd(name, src, cols, seg, slopes):
    T = src.shape[0]
    bs = SUB_A * BQ_A
    wk = min(BQ_A + 2 * HALF_WINDOW, seg)

    def body(sl_ref, q_ref, k_ref, v_ref, o_ref, lse_ref):
        slope = sl_ref[pl.program_id(0)]
        for u in range(SUB_A):
            rows, win, offset = _band_window(pl.program_id(1), u, seg, wk)
            s = _dot(q_ref[rows, :], k_ref[win, :], "nt") * SCALE + _band_bias(slope, offset, wk)
            m = jnp.max(s, axis=-1, keepdims=True)
            p = jnp.exp(s - m)
            den = jnp.sum(p, axis=-1, keepdims=True)
            o_ref[rows, :] = (_dot(p.astype(BF16), v_ref[win, :], "nn") / den).astype(BF16)
            lse_ref[u:u + 1, :] = _col_to_row(m + jnp.log(den))

    return _pc(body, name=name, grid=(4, T // bs), in_specs=_band_specs(T, cols),
               out_specs=[pl.BlockSpec((bs, HEAD_DIM), lambda j, i: (i, j)), _row_spec(bs)],
               out_shape=[jax.ShapeDtypeStruct((T, 4 * HEAD_DIM), BF16), jax.ShapeDtypeStruct((4, T // BQ_A, BQ_A), F32)],
               sem=("parallel", "parallel"))(slopes, src, src, src)


def _banded_bwd(name, src, cols, seg, slopes, dy, delta_lse):
    T = src.shape[0]
    bs = SUB_A * BQ_A
    steps = T // bs
    wk = min(BQ_A + 2 * HALF_WINDOW, seg)

    def body(sl_ref, q_ref, k_ref, v_ref, dy_ref, delta_ref, lse_ref, dq_ref, dk_ref, dv_ref, dk_acc, dv_acc):
        slope = sl_ref[pl.program_id(0)]
        step = pl.program_id(1)

        @pl.when(step == 0)
        def _():
            dk_acc[...] = jnp.zeros_like(dk_acc)
            dv_acc[...] = jnp.zeros_like(dv_acc)

        for u in range(SUB_A):
            rows, win, offset = _band_window(step, u, seg, wk)
            qv, kw, dyv = q_ref[rows, :], k_ref[win, :], dy_ref[rows, :]
            s = _dot(qv, kw, "nt") * SCALE + _band_bias(slope, offset, wk)
            p = jnp.exp(s - _row_to_col(lse_ref[u:u + 1, :]))
            ds = (p * (_dot(dyv, v_ref[win, :], "nt") - _row_to_col(delta_ref[u:u + 1, :]))).astype(BF16)
            dq_ref[rows, :] = (_dot(ds, kw, "nn") * SCALE).astype(BF16)
            dk_acc[win, :] += _dot(ds, qv, "tn") * SCALE
            dv_acc[win, :] += _dot(p.astype(BF16), dyv, "tn")

        @pl.when(step == steps - 1)
        def _():
            dk_ref[...] = dk_acc[...].astype(BF16)
            dv_ref[...] = dv_acc[...].astype(BF16)

    blk = pl.BlockSpec((bs, HEAD_DIM), lambda j, i: (i, j))
    whole = pl.BlockSpec((T, HEAD_DIM), lambda j, i: (0, j))
    full = jax.ShapeDtypeStruct((T, 4 * HEAD_DIM), BF16)
    return _pc(body, name=name, grid=(4, steps),
               in_specs=_band_specs(T, cols) + [blk, _row_spec(bs), _row_spec(bs, 4)],
               out_specs=[blk, whole, whole], out_shape=[full, full, full],
               scratch=[pltpu.VMEM((T, HEAD_DIM), F32), pltpu.VMEM((T, HEAD_DIM), F32)],
               sem=("parallel", "arbitrary"))(slopes, src, src, src, dy, delta_lse, delta_lse)


def _last_group_source(qkv):
    third = qkv.shape[1] // 3
    own = jnp.concatenate([qkv[:, t * third + 512 * N_JOINT:t * third + 512 * (N_JOINT + 1)] for t in range(3)], axis=1)
    return _to_residue_major(own, DILATIONS[N_JOINT])


def _dilated_fwd(qkv, slopes):
    d = DILATIONS[N_JOINT]
    last_src = _last_group_source(qkv)
    o, lse = _banded_fwd("attn_a_last_fwd", last_src, (0, 4, 8), qkv.shape[0] // d, slopes[4 * N_JOINT:] * float(d))
    lse = _to_token_order(lse.reshape(4, -1), d, axis=1).reshape(lse.shape)
    y, joint_lse = _attn_a_fwd(qkv, slopes, _to_token_order(o, d), lse)
    return y, joint_lse, last_src


def _dilated_bwd(qkv, last_src, slopes, dy, y, lse):
    d = DILATIONS[N_JOINT]
    dq, dk, dv, delta = _attn_a_bwd(qkv, slopes, dy, y, lse)
    delta_lse = jnp.concatenate([delta, lse], axis=0)
    delta_lse = _to_residue_major(delta_lse.reshape(8, -1), d, axis=1).reshape(delta_lse.shape)
    last = _banded_bwd("attn_a_last_bwd", last_src, (0, 4, 8), qkv.shape[0] // d,
                       slopes[4 * N_JOINT:] * float(d), _to_residue_major(dy, d), delta_lse)
    return [jnp.concatenate([joint[g] for g in range(N_JOINT)] + [_to_token_order(own, d)], axis=1)
            for joint, own in zip((dq, dk, dv), last)]


N_DR = 2 * NA_ROWS - 1
N_DC = 2 * NA_COLS - 1
WK_B = NA_ROWS * GRID_W


def _toeplitz_consts():
    col = np.arange(GRID_W)
    start = np.clip(col - NA_COLS // 2, 0, GRID_W - NA_COLS)
    ok = (col[None, :] >= start[:, None]) & (col[None, :] < start[:, None] + NA_COLS)
    dc = np.clip(col[None, :] - col[:, None], -(NA_COLS - 1), NA_COLS - 1) + NA_COLS - 1
    sel = np.zeros((128, GRID_W * GRID_W), np.float32)
    sel[dc.reshape(-1), np.arange(GRID_W * GRID_W)] = ok.reshape(-1).astype(np.float32)
    mask = np.where(ok, 0.0, NEG).astype(np.float32).reshape(1, -1)
    return sel, mask


def _rpb_expand(rpb_rows, sel, mask):
    def body(r_ref, s_ref, m_ref, o_ref):
        o_ref[...] = _dot(r_ref[...], s_ref[...], "nn", precision=lax.Precision.HIGHEST) + m_ref[...]

    n = rpb_rows.shape[0]
    return _pc(body, name="rpb_expand", grid=(1,),
               in_specs=[pl.BlockSpec((n, 128), lambda i: (0, 0)), pl.BlockSpec(sel.shape, lambda i: (0, 0)),
                         pl.BlockSpec(mask.shape, lambda i: (0, 0))],
               out_specs=pl.BlockSpec((n, sel.shape[1]), lambda i: (0, 0)),
               out_shape=jax.ShapeDtypeStruct((n, sel.shape[1]), F32), sem=("arbitrary",))(rpb_rows, sel, mask)


def _rpb_reduce(dtiles, sel):
    def body(t_ref, s_ref, o_ref):
        o_ref[...] = _dot(t_ref[...], s_ref[...], "nt", precision=lax.Precision.HIGHEST)

    n = dtiles.shape[0]
    return _pc(body, name="rpb_reduce", grid=(1,),
               in_specs=[pl.BlockSpec(dtiles.shape, lambda i: (0, 0)), pl.BlockSpec(sel.shape, lambda i: (0, 0))],
               out_specs=pl.BlockSpec((n, 128), lambda i: (0, 0)),
               out_shape=jax.ShapeDtypeStruct((n, 128), F32), sem=("arbitrary",))(dtiles, sel)


ROWS_B = 16
BQ_B = ROWS_B * GRID_W


def _b_specs(T):
    return [pl.BlockSpec((BQ_B, HEAD_DIM), lambda h, r: (r, N_HEADS_A + h)),
            pl.BlockSpec((T, HEAD_DIM), lambda h, r: (0, N_HEADS + N_HEADS_A + h)),
            pl.BlockSpec((T, HEAD_DIM), lambda h, r: (0, 2 * N_HEADS + N_HEADS_A + h)),
            pl.BlockSpec((None, N_DR, GRID_W, GRID_W), lambda h, r: (h, 0, 0, 0))]


def _b_row(step, u, rows, t_ref, bias_ref):
    r = step * ROWS_B + u
    rs = jnp.clip(r - NA_ROWS // 2, 0, rows - NA_ROWS)
    first_tile = rs - r + NA_ROWS - 1
    for a in range(NA_ROWS):
        bias_ref[u, :, a * GRID_W:(a + 1) * GRID_W] = t_ref[first_tile + a]
    return slice(u * GRID_W, (u + 1) * GRID_W), pl.ds(pl.multiple_of(rs * GRID_W, GRID_W), WK_B), first_tile


def _attn_b_fwd(qkv, tiles):
    T = qkv.shape[0]
    rows = T // GRID_W

    def body(q_ref, k_ref, v_ref, t_ref, y_ref, lse_ref, bias_ref):
        for u in range(ROWS_B):
            blk, win, _ = _b_row(pl.program_id(1), u, rows, t_ref, bias_ref)
            s = _dot(q_ref[blk, :], k_ref[win, :], "nt") * SCALE + bias_ref[u]
            m = jnp.max(s, axis=-1, keepdims=True)
            p = jnp.exp(s - m)
            den = jnp.sum(p, axis=-1, keepdims=True)
            y_ref[blk, :] = (_dot(p.astype(BF16), v_ref[win, :], "nn") / den).astype(BF16)
            lse_ref[blk, :] = m + jnp.log(den)

    return _pc(body, name="attn_b_fwd", grid=(N_HEADS_B, rows // ROWS_B), in_specs=_b_specs(T),
               out_specs=[pl.BlockSpec((BQ_B, HEAD_DIM), lambda h, r: (r, h)),
                          pl.BlockSpec((None, BQ_B, 1), lambda h, r: (h, r, 0))],
               out_shape=[jax.ShapeDtypeStruct((T, N_HEADS_B * HEAD_DIM), BF16),
                          jax.ShapeDtypeStruct((N_HEADS_B, T, 1), F32)],
               scratch=[pltpu.VMEM((ROWS_B, GRID_W, WK_B), F32)], sem=("parallel", "parallel"))(qkv, qkv, qkv, tiles)


def _attn_b_bwd(qkv, tiles, dy, y, lse):
    T = qkv.shape[0]
    rows = T // GRID_W
    steps = rows // ROWS_B

    def body(q_ref, k_ref, v_ref, t_ref, dy_ref, y_ref, lse_ref, dq_ref, dk_ref, dv_ref, dt_ref,
             bias_ref, ds_ref, dk_acc, dv_acc):
        step = pl.program_id(1)

        @pl.when(step == 0)
        def _():
            dk_acc[...] = jnp.zeros_like(dk_acc)
            dv_acc[...] = jnp.zeros_like(dv_acc)
            dt_ref[...] = jnp.zeros_like(dt_ref)

        for u in range(ROWS_B):
            blk, win, first_tile = _b_row(step, u, rows, t_ref, bias_ref)
            qv = q_ref[blk, :]
            kw = k_ref[win, :]
            dyv = dy_ref[blk, :]
            s = _dot(qv, kw, "nt") * SCALE + bias_ref[u]
            p = jnp.exp(s - lse_ref[blk, :])
            delta = jnp.sum(dyv.astype(F32) * y_ref[blk, :].astype(F32), axis=-1, keepdims=True)
            ds = p * (_dot(dyv, v_ref[win, :], "nt") - delta)
            ds_ref[u] = ds
            for a in range(NA_ROWS):
                dt_ref[first_tile + a] += ds_ref[u, :, a * GRID_W:(a + 1) * GRID_W]
            dsb = ds.astype(BF16)
            dq_ref[blk, :] = (_dot(dsb, kw, "nn") * SCALE).astype(BF16)
            dk_acc[win, :] += _dot(dsb, qv, "tn") * SCALE
            dv_acc[win, :] += _dot(p.astype(BF16), dyv, "tn")

        @pl.when(step == steps - 1)
        def _():
            dk_ref[...] = dk_acc[...].astype(BF16)
            dv_ref[...] = dv_acc[...].astype(BF16)

    blk_spec = pl.BlockSpec((BQ_B, HEAD_DIM), lambda h, r: (r, h))
    whole = pl.BlockSpec((T, HEAD_DIM), lambda h, r: (0, h))
    full = jax.ShapeDtypeStruct((T, N_HEADS_B * HEAD_DIM), BF16)
    return _pc(body, name="attn_b_bwd", grid=(N_HEADS_B, steps),
               in_specs=_b_specs(T) + [blk_spec, blk_spec, pl.BlockSpec((None, BQ_B, 1), lambda h, r: (h, r, 0))],
               out_specs=[blk_spec, whole, whole, pl.BlockSpec((None, N_DR, GRID_W, GRID_W), lambda h, r: (h, 0, 0, 0))],
               out_shape=[full, full, full, jax.ShapeDtypeStruct((N_HEADS_B, N_DR, GRID_W, GRID_W), F32)],
               scratch=[pltpu.VMEM((ROWS_B, GRID_W, WK_B), F32), pltpu.VMEM((ROWS_B, GRID_W, WK_B), F32),
                        pltpu.VMEM((T, HEAD_DIM), F32), pltpu.VMEM((T, HEAD_DIM), F32)],
               sem=("parallel", "arbitrary"))(qkv, qkv, qkv, tiles, dy, y, lse)


def _sds(shape, dtype):
    return jax.ShapeDtypeStruct(shape, dtype)


def _behind(value, token):
    return value if token is None else value + token[0, 0]


def _local_step(x, tgt, norm_mix, b_gate, rpb, norm_mlp, norm_final, late_weights, send):
    T, D = x.shape
    nt = T // TM
    PP, PA = ("parallel", "parallel", "arbitrary"), ("arbitrary", "arbitrary", "arbitrary")
    slopes = jnp.asarray(_alibi_slopes())
    sel_np, mask_np = _toeplitz_consts()
    sel, mask = jnp.asarray(sel_np), jnp.asarray(mask_np)

    def rows(w, tm=TM):
        return pl.BlockSpec((tm, w), lambda i, j, k: (i, j))

    def full_rows(tm, w=D):
        return pl.BlockSpec((tm, w), lambda i, j, k: (i, 0))

    def tall(w):
        return pl.BlockSpec((T, w), lambda j, i, k: (0, i))

    def tall_j(w):
        return pl.BlockSpec((T, w), lambda j, i, k: (0, j))

    rvec = pl.BlockSpec((TS, 1), lambda i, j, k: (i, 0))
    gvec = pl.BlockSpec((1, D), lambda i, j, k: (0, 0))

    h1, r1 = _rmsnorm_fwd(x, norm_mix, "rmsnorm_mix")
    rpb_rows = jnp.pad(rpb.reshape(N_HEADS_B * N_DR, N_DC), ((0, 0), (0, 128 - N_DC)))
    tiles = _rpb_expand(rpb_rows, sel, mask).reshape(N_HEADS_B, N_DR, GRID_W, GRID_W)
    (wq,) = late_weights("qkv", (h1, tiles))

    def store_bf16(acc, ex, o, pids):
        o[0][...] = acc.astype(BF16)

    (qkv,) = _mm("qkv_proj", (T // TL, N_DEV // 2, 1),
                 [(h1, full_rows(TL), wq, pl.BlockSpec((2, D, 768), lambda i, j, k: (j, 0, 0)), "nn_shards", None)],
                 [], [(_sds((T, 3 * D), BF16), rows(1536, TL))], store_bf16, None, PP)

    def gate_epi(acc, ex, o, pids):
        o[0][...] = jax.nn.sigmoid(acc + ex[0][...]).astype(BF16)

    late_weights("attention", qkv)
    ya, lse_a, last_src = _dilated_fwd(qkv, slopes)
    yb, lse_b = _attn_b_fwd(qkv, tiles)

    wg, wpa, wpb, wo, token = late_weights("mixer", ya)
    (gates,) = _mm("gate_proj", (T // TL, N_DEV // 2, 1),
                   [(h1, full_rows(TL), wg, pl.BlockSpec((2, D, 512), lambda i, j, k: (j, 0, 0)), "nn_shards", None)],
                   [(_behind(b_gate, token), pl.BlockSpec((1, 1024), lambda i, j, k: (0, j)))],
                   [(_sds((T, 2 * D), BF16), rows(1024, TL))], gate_epi, None, PP)

    def merge_body(ya_ref, wa_ref, yb_ref, wb_ref, ga_ref, gb_ref, pa_ref, pb_ref, mg_ref):
        ya_v, yb_v = ya_ref[...], yb_ref[...]
        for s in range(N_DEV):
            cols = slice(s * 256, (s + 1) * 256)
            pa = _dot(ya_v, wa_ref[s], "nn")
            pb = _dot(yb_v, wb_ref[s], "nn")
            pa_ref[:, cols] = pa.astype(BF16)
            pb_ref[:, cols] = pb.astype(BF16)
            mg_ref[:, cols] = (ga_ref[:, cols] * pa + gb_ref[:, cols] * pb).astype(BF16)

    wo2 = wo.reshape(D, D)
    y_spec = pl.BlockSpec((TS, 512), lambda i: (i, 0))
    wp_spec = pl.BlockSpec((N_DEV, 512, 256), lambda i: (0, 0, 0))
    c_spec = pl.BlockSpec((TS, D), lambda i: (i, 0))
    pa, pb, merged = _pc(merge_body, name="merge_proj", grid=(T // TS,),
                         in_specs=[y_spec, wp_spec, y_spec, wp_spec, c_spec, pl.BlockSpec((TS, D), lambda i: (i, 1))],
                         out_specs=[c_spec, c_spec, c_spec],
                         out_shape=[_sds((T, D), BF16)] * 3, sem=("parallel",))(ya, wpa, yb, wpb, gates, gates)

    def add_res(acc, ex, o, pids):
        o[0][...] = ex[0][...] + acc

    def out_norm_epi(acc, ex, o, pids):
        x2v = ex[0][...] + acc
        r = lax.rsqrt(jnp.mean(x2v * x2v, axis=-1, keepdims=True) + EPS)
        o[0][...] = x2v
        o[1][...] = (x2v * r * ex[1][...]).astype(BF16)
        o[2][...] = r

    x2, h2, r2 = _mm("out_proj", (nt, 1, 1),
                     [(merged, full_rows(TM), wo2, pl.BlockSpec((D, D), lambda i, j, k: (0, 0)), "nn", None)],
                     [(x, full_rows(TM)), (norm_mlp, gvec)],
                     [(_sds((T, D), F32), full_rows(TM)), (_sds((T, D), BF16), full_rows(TM)),
                      (_sds((T, 1), F32), pl.BlockSpec((TM, 1), lambda i, j, k: (i, 0)))], out_norm_epi, None, PP)

    (token,) = late_weights("residual", x2)
    (wu,) = late_weights("up", h2)
    behind = lambda token: [] if token is None else [(token, pl.BlockSpec(memory_space=pl.ANY))]

    def up_epi(acc, ex, o, pids):
        o[0][...] = acc.astype(BF16)
        r = jnp.maximum(acc, 0.0)
        o[1][...] = (r * r).astype(BF16)

    u, act = _mm("mlp_up", (T // TL, N_DEV, 1),
                 [(h2, full_rows(TL), wu, pl.BlockSpec((None, D, 1024), lambda i, j, k: (j, 0, 0)), "nn", None)],
                 behind(token), [(_sds((T, D_FF), BF16), rows(1024, TL))] * 2, up_epi, None, PP)
    (wd,) = late_weights("down", act)
    wd2 = wd.reshape(D_FF, D)
    half_ji = pl.BlockSpec((TS, D // 2), lambda j, i, k: (i, j))
    (x3,) = _mm("mlp_down", (2, T // TS, 1),
                [(act, pl.BlockSpec((TS, D_FF), lambda j, i, k: (i, 0)), wd2,
                  pl.BlockSpec((D_FF, D // 2), lambda j, i, k: (0, j)), "nn", None)],
                [(x2, half_ji)], [(_sds((T, D), F32), half_ji)], add_res, None, PP)

    loss, dx3, dx3b, dg3 = _final_norm_loss(x3, norm_final.reshape(1, D), tgt)

    def du_epi(acc, ex, o, pids):
        o[0][...] = (acc * (2.0 * jnp.maximum(ex[0][...].astype(F32), 0.0))).astype(BF16)

    (du,) = _mm("mlp_down_dx", (T // TL, N_DEV, 1),
                [(dx3b, full_rows(TL), wd2, pl.BlockSpec((1024, D), lambda i, j, k: (j, 0)), "nt", None)],
                [(u, rows(1024, TL))], [(_sds((T, D_FF), BF16), rows(1024, TL))], du_epi, None, PP)
    (dwd,) = _mm("mlp_down_dw", (D // 1024, D_FF // 512, 1), [(act, tall(512), dx3b, tall_j(1024), "tn", None)],
                 [], [(_sds((D_FF, D), BF16), pl.BlockSpec((512, 1024), lambda j, i, k: (i, j)))], store_bf16, None, PP)
    dwd = dwd.reshape(N_DEV, D_FF // N_DEV, D)
    (dwu,) = _mm("mlp_up_dw", (N_DEV, D // 512, 1), [(h2, tall(512), du, tall_j(1024), "tn", None)],
                 [], [(_sds((N_DEV, D, 1024), BF16), pl.BlockSpec((None, 512, 1024), lambda j, i, k: (j, i, 0)))],
                 store_bf16, None, PP)

    def store_f32(acc, ex, o, pids):
        o[0][...] = acc

    out_ji = pl.BlockSpec((TM, 512), lambda j, i, k: (i, j))
    a_ji = lambda w: pl.BlockSpec((TM, w), lambda j, i, k: (i, 0))
    w_ji = lambda n: pl.BlockSpec((N_DEV, 512, n), lambda j, i, k: (0, j, 0))
    (dh2,) = _mm("mlp_up_dx", (D // 512, nt, 1), [(du, a_ji(D_FF), wu, w_ji(1024), "nt_shards", None)],
                 behind(send("mlp", dict(w_down=dwd, w_up=dwu))), [(_sds((T, D), F32), out_ji)], store_f32, None, PP)
    dx2, dx2b, dg2 = _rmsnorm_bwd("rmsnorm_mlp_bwd", dh2, x2, r2, norm_mlp, dx3, True)

    (dwo,) = _mm("out_proj_dw", (D // 1024, D // 512, 1), [(merged, tall(512), dx2b, tall_j(1024), "tn", None)],
                 [], [(_sds((D, D), BF16), pl.BlockSpec((512, 1024), lambda j, i, k: (i, j)))], store_bf16, None, PP)

    def dmerge_epi(acc, ex, o, pids):
        ga, gb, pa_v, pb_v = (e[...].astype(F32) for e in ex)
        o[0][...] = (acc * ga).astype(BF16)
        o[1][...] = (acc * gb).astype(BF16)
        dga = acc * pa_v * ga * (1.0 - ga)
        dgb = acc * pb_v * gb * (1.0 - gb)
        o[2][...] = dga.astype(BF16)
        o[3][...] = dgb.astype(BF16)
        sa = jnp.sum(dga, axis=0, keepdims=True)
        sb = jnp.sum(dgb, axis=0, keepdims=True)
        first = pids[1] == 0

        @pl.when(first)
        def _():
            o[4][...] = sa
            o[5][...] = sb

        @pl.when(jnp.logical_not(first))
        def _():
            o[4][...] += sa
            o[5][...] += sb

    cj = pl.BlockSpec((TL, 512), lambda j, i, k: (i, j))
    bj = pl.BlockSpec((1, 512), lambda j, i, k: (0, j))
    dpa, dpb, dga, dgb, dba, dbb = _mm(
        "out_proj_dx", (4, T // TL, 1),
        [(dx2b, pl.BlockSpec((TL, D), lambda j, i, k: (i, 0)), wo2, pl.BlockSpec((512, D), lambda j, i, k: (j, 0)), "nt", None)],
        [(gates, cj), (gates, pl.BlockSpec((TL, 512), lambda j, i, k: (i, 4 + j))), (pa, cj), (pb, cj)],
        [(_sds((T, D), BF16), cj)] * 4 + [(_sds((1, D), F32), bj)] * 2, dmerge_epi, None,
        ("parallel", "arbitrary", "arbitrary"))

    def store_shards(acc, ex, o, pids):
        for s in range(4):
            o[0][s] = acc[:, s * 256:(s + 1) * 256].astype(BF16)

    def proj_dw(name, y, dp):
        (dw,) = _mm(name, (2, 1, 1), [(y, tall(512), dp, tall_j(1024), "tn", None)],
                    [], [(_sds((N_DEV, 512, 256), BF16), pl.BlockSpec((4, 512, 256), lambda j, i, k: (j, 0, 0)))],
                    store_shards, None, PP)
        return dw

    def proj_dx(name, dp, w):
        (dy,) = _mm(name, (nt, 1, 1),
                    [(dp, full_rows(TM), w, pl.BlockSpec((N_DEV, 512, 256), lambda i, j, k: (0, 0, 0)), "nt_shards", None)],
                    [], [(_sds((T, 512), BF16), full_rows(TM, 512))], store_bf16, None, PP)
        return dy

    def in_dw(name, dact, n):
        def store_pair(acc, ex, o, pids):
            o[0][0] = acc[:, :n].astype(BF16)
            o[0][1] = acc[:, n:].astype(BF16)

        (dw,) = _mm(name, (N_DEV // 2, D // 512, 1), [(h1, tall(512), dact, tall_j(2 * n), "tn", None)],
                    [], [(_sds((N_DEV, D, n), BF16), pl.BlockSpec((2, 512, n), lambda j, i, k: (j, i, 0)))],
                    store_pair, None, PP)
        return dw

    dgate = jnp.concatenate([dga, dgb], axis=1)
    dwg = in_dw("gate_proj_dw", dgate, 512)
    dwpa = proj_dw("proj_a_dw", ya, dpa)
    dwpb = proj_dw("proj_b_dw", yb, dpb)
    token = send("proj", dict(w_gate=dwg, w_out=dwo.reshape(N_DEV, D // N_DEV, D), w_proj_a=dwpa, w_proj_b=dwpb))
    dya = proj_dx("proj_a_dx", dpa, wpa)
    dyb = proj_dx("proj_b_dx", dpb, wpb)

    dq_a, dk_a, dv_a = _dilated_bwd(qkv, last_src, _behind(slopes, token), dya, ya, lse_a)
    dq_b, dk_b, dv_b, dtiles = _attn_b_bwd(qkv, tiles, dyb, yb, lse_b)
    drpb = _rpb_reduce(dtiles.reshape(N_HEADS_B * N_DR, GRID_W * GRID_W), sel)[:, :N_DC]
    dqkv = jnp.concatenate([dq_a, dq_b, dk_a, dk_b, dv_a, dv_b], axis=1)
    dwq = in_dw("qkv_proj_dw", dqkv, 768)
    (dh1,) = _mm("in_proj_dx", (D // 512, nt, 1),
                 [(dqkv, a_ji(3 * D), wq, w_ji(768), "nt_shards", None), (dgate, a_ji(2 * D), wg, w_ji(512), "nt_shards", None)],
                 behind(send("in", dict(w_qkv=dwq))), [(_sds((T, D), F32), out_ji)], store_f32, None, PP)
    grad_x, dg1 = _rmsnorm_bwd("rmsnorm_mix_bwd", dh1, x, r1, norm_mix, dx2, False)

    small = dict(norm_mix=dg1, b_gate=jnp.concatenate([dba, dbb], axis=1), rpb=drpb, norm_mlp=dg2, norm_final=dg3)
    return loss, grad_x, small


BIG = ("w_qkv", "w_gate", "w_proj_a", "w_proj_b", "w_out", "w_up", "w_down")
SMALL = ("norm_mix", "b_gate", "rpb", "norm_mlp", "norm_final")
SMALL_ROWS = 96


def _place():
    return lax.axis_index("x"), lax.axis_index("y"), lax.axis_index("c")


def _peer(x, y, c, mask):
    return x ^ (mask >> 2), y ^ ((mask >> 1) & 1), c ^ (mask & 1)


_HBM = pl.BlockSpec(memory_space=pltpu.HBM)
_SEM = pl.BlockSpec(memory_space=pltpu.SEMAPHORE)
_ANY = pl.BlockSpec(memory_space=pl.ANY)
_EFFECT = pltpu.SideEffectType.DATAFLOW_SIDE_EFFECTING


def _hbm(a):
    return pltpu.with_memory_space_constraint(a, pltpu.HBM)


def _other_chips(x, y):
    return [(1 - x, y), (x, 1 - y), (1 - x, 1 - y)]


def _block(x, y, c):
    return 4 * x + 2 * y + c


def _rdma(src, dst, send_sem, recv_sem, to):
    return pltpu.make_async_remote_copy(src_ref=src, dst_ref=dst, send_sem=send_sem, recv_sem=recv_sem, device_id=to,
                                        device_id_type=MESH)


def _gather_start(name, shards):
    n = len(shards)

    def body(*refs):
        srcs, lands = refs[:n], refs[n:2 * n]
        outs = refs[2 * n:]
        send, recv_d2d, recv_ici, local = outs[:n], outs[n:2 * n], outs[2 * n:3 * n], outs[3 * n:4 * n]
        token_ref = outs[6 * n]
        x, y, c = _place()
        for a in range(n):
            mine = lands[a].at[_block(x, y, c)]
            _rdma(srcs[a], mine, send[a].at[0], recv_d2d[a].at[0], (x, y, 1 - c)).start()
            for j, chip in enumerate(_other_chips(x, y)):
                _rdma(srcs[a], mine, send[a].at[1 + j], recv_ici[a].at[j], (*chip, c)).start()
        for a in range(n):
            pltpu.make_async_copy(srcs[a], lands[a].at[_block(x, y, c)], local[a].at[0]).start()
        token_ref[...] = jnp.zeros_like(token_ref)

    lands = [_hbm(lax.empty((N_DEV,) + s.shape, s.dtype)) for s in shards]
    srcs = [_hbm(s) for s in shards]
    dma = pltpu.SemaphoreType.DMA
    res = pl.pallas_call(
        body, name=name,
        out_shape=[dma((4,))] * n + [dma((1,))] * n + [dma((3,))] * n + [dma((1,))] * n
        + [pltpu.HBM(s.shape, s.dtype) for s in srcs] + [pltpu.HBM(l.shape, l.dtype) for l in lands]
        + [jax.ShapeDtypeStruct((8, 128), F32)],
        in_specs=[_HBM] * (2 * n), out_specs=[_SEM] * (4 * n) + [_HBM] * (2 * n) + [pl.BlockSpec(memory_space=pltpu.VMEM)],
        input_output_aliases={i: 4 * n + i for i in range(2 * n)},
        compiler_params=pltpu.CompilerParams(has_side_effects=_EFFECT))(*srcs, *lands)
    keys = ("send", "recv_d2d", "recv_ici", "local", "src", "land")
    return [dict(zip(keys, (res[k * n + a] for k in range(6)))) for a in range(n)], res[6 * n]


def _gather_forward(name, handles, after):
    n = len(handles)
    after = after if isinstance(after, (tuple, list)) else (after,)

    def body(*refs):
        lands, recv_ici = refs[:n], refs[n:2 * n]
        outs = refs[2 * n + len(after):]
        fwd_send, fwd_recv, token_ref = outs[n:2 * n], outs[2 * n:3 * n], outs[3 * n]
        x, y, c = _place()
        for a in range(n):
            for j, chip in enumerate(_other_chips(x, y)):
                blk = lands[a].at[_block(*chip, c)]
                _rdma(blk, blk, fwd_send[a].at[j], recv_ici[a].at[j], (*chip, c)).wait_recv()
                _rdma(blk, blk, fwd_send[a].at[j], fwd_recv[a].at[j], (x, y, 1 - c)).start()
        token_ref[...] = jnp.zeros_like(token_ref)

    dma = pltpu.SemaphoreType.DMA
    lands = [h["land"] for h in handles]
    res = pl.pallas_call(
        body, name=name,
        out_shape=[pltpu.HBM(l.shape, l.dtype) for l in lands] + [dma((3,))] * (2 * n) + [jax.ShapeDtypeStruct((8, 128), F32)],
        in_specs=[_HBM] * n + [_SEM] * n + [_ANY] * len(after),
        out_specs=[_HBM] * n + [_SEM] * (2 * n) + [pl.BlockSpec(memory_space=pltpu.VMEM)],
        input_output_aliases={i: i for i in range(n)},
        compiler_params=pltpu.CompilerParams(has_side_effects=_EFFECT))(*lands, *[h["recv_ici"] for h in handles], *after)
    return [dict(h, land=res[a], fwd_send=res[n + a], fwd_recv=res[2 * n + a]) for a, h in enumerate(handles)], res[3 * n]


def _gather_wait(name, handles, after):
    n = len(handles)

    def body(*refs):
        srcs, lands = refs[:n], refs[n:2 * n]
        send, recv_d2d, local, fwd_send, fwd_recv = (refs[(2 + k) * n:(3 + k) * n] for k in range(5))
        x, y, c = _place()
        sibling = (x, y, 1 - c)
        for a in range(n):
            mine = lands[a].at[_block(x, y, c)]
            _rdma(srcs[a], mine, send[a].at[0], recv_d2d[a].at[0], sibling).wait_send()
            _rdma(srcs[a], lands[a].at[_block(*sibling)], send[a].at[0], recv_d2d[a].at[0], sibling).wait_recv()
            pltpu.make_async_copy(srcs[a], mine, local[a].at[0]).wait()
            for j, chip in enumerate(_other_chips(x, y)):
                _rdma(srcs[a], mine, send[a].at[1 + j], fwd_recv[a].at[j], (*chip, c)).wait_send()
                blk = lands[a].at[_block(*chip, c)]
                _rdma(blk, blk, fwd_send[a].at[j], fwd_recv[a].at[j], sibling).wait_send()
                got = lands[a].at[_block(*chip, 1 - c)]
                _rdma(got, got, fwd_send[a].at[j], fwd_recv[a].at[j], sibling).wait_recv()

    srcs = [h["src"] for h in handles]
    lands = [h["land"] for h in handles]
    sems = [h[k] for k in ("send", "recv_d2d", "local", "fwd_send", "fwd_recv") for h in handles]
    res = pl.pallas_call(
        body, name=name, out_shape=[pltpu.HBM(s.shape, s.dtype) for s in srcs] + [pltpu.HBM(l.shape, l.dtype) for l in lands],
        in_specs=[_HBM] * (2 * n) + [_SEM] * (5 * n) + [_ANY], out_specs=[_HBM] * (2 * n),
        input_output_aliases={i: i for i in range(2 * n)},
        compiler_params=pltpu.CompilerParams(has_side_effects=_EFFECT))(*srcs, *lands, *sems, after)
    return list(res[n:])


def _gather_small(small, after):
    def body(in_ref, *refs):
        out_ref, send_sems, recv_sems, local_sem = refs[len(after):]
        x, y, c = _place()
        me = 4 * x + 2 * y + c
        copies = [pltpu.make_async_copy(in_ref, out_ref.at[me], local_sem)]
        for mask in range(1, N_DEV):
            copies.append(pltpu.make_async_remote_copy(
                src_ref=in_ref, dst_ref=out_ref.at[me], send_sem=send_sems.at[mask - 1],
                recv_sem=recv_sems.at[mask - 1], device_id=_peer(x, y, c, mask), device_id_type=MESH))
        for cp in copies:
            cp.start()
        for cp in copies:
            cp.wait()

    any_spec = pl.BlockSpec(memory_space=pl.ANY)
    return pl.pallas_call(
        body, name="gather_small", in_specs=[any_spec] * (1 + len(after)), out_specs=any_spec,
        out_shape=jax.ShapeDtypeStruct((N_DEV,) + small.shape, small.dtype),
        scratch_shapes=[pltpu.SemaphoreType.DMA((7,)), pltpu.SemaphoreType.DMA((7,)), pltpu.SemaphoreType.DMA])(small, *after)


def _send_copy(src_refs, land_refs, send_sems, recv_sems, a, mask, scatter, x, y, c):
    px, py, pc = _peer(x, y, c, mask)
    src = src_refs[a].at[4 * px + 2 * py + pc] if scatter else src_refs[a]
    return pltpu.make_async_remote_copy(
        src_ref=src, dst_ref=land_refs[a].at[4 * x + 2 * y + c], send_sem=send_sems[a].at[mask - 1],
        recv_sem=recv_sems[a].at[mask - 1], device_id=(px, py, pc), device_id_type=MESH)


def _local_copy(src_refs, land_refs, local_sems, a, scatter, x, y, c):
    me = 4 * x + 2 * y + c
    return pltpu.make_async_copy(src_refs[a].at[me] if scatter else src_refs[a], land_refs[a].at[me], local_sems[a].at[0])


def _send_start(name, srcs, scatter, after=None):
    n = len(srcs)
    n_in = 2 * n + (after is not None)
    shapes = [s.shape[1:] if scatter else s.shape for s in srcs]

    def body(*refs):
        src_refs, land_refs = refs[:n], refs[n:2 * n]
        outs = refs[n_in:]
        send_sems, recv_sems, local_sems = outs[:n], outs[n:2 * n], outs[2 * n:3 * n]
        token_ref = outs[5 * n]
        x, y, c = _place()
        for a in range(n):
            for mask in range(1, N_DEV):
                _send_copy(src_refs, land_refs, send_sems, recv_sems, a, mask, scatter, x, y, c).start()
        for a in range(n):
            _local_copy(src_refs, land_refs, local_sems, a, scatter, x, y, c).start()
        token_ref[...] = jnp.zeros_like(token_ref)

    lands = [pltpu.with_memory_space_constraint(lax.empty((N_DEV,) + sh, s.dtype), pltpu.HBM) for sh, s in zip(shapes, srcs)]
    srcs = [pltpu.with_memory_space_constraint(s, pltpu.HBM) for s in srcs]
    res = pl.pallas_call(
        body, name=name,
        out_shape=[pltpu.SemaphoreType.DMA((7,))] * (2 * n) + [pltpu.SemaphoreType.DMA((1,))] * n
        + [pltpu.HBM(s.shape, s.dtype) for s in srcs] + [pltpu.HBM(l.shape, l.dtype) for l in lands]
        + [jax.ShapeDtypeStruct((8, 128), F32)],
        in_specs=[_HBM] * (2 * n) + [pl.BlockSpec(memory_space=pl.ANY)] * (n_in - 2 * n),
        out_specs=[_SEM] * (3 * n) + [_HBM] * (2 * n) + [pl.BlockSpec(memory_space=pltpu.VMEM)],
        input_output_aliases={i: 3 * n + i for i in range(2 * n)},
        compiler_params=pltpu.CompilerParams(has_side_effects=_EFFECT))(*srcs, *lands, *([] if after is None else [after]))
    handles = [tuple(res[k * n + a] for k in range(5)) for a in range(n)]
    return handles, res[5 * n]


def _send_wait(name, handles, scatter, after):
    n = len(handles)

    def body(*refs):
        src_refs, land_refs = refs[:n], refs[n:2 * n]
        send_sems, recv_sems, local_sems = refs[2 * n:3 * n], refs[3 * n:4 * n], refs[4 * n:5 * n]
        x, y, c = _place()
        for a in range(n):
            for mask in range(1, N_DEV):
                cp = _send_copy(src_refs, land_refs, send_sems, recv_sems, a, mask, scatter, x, y, c)
                cp.wait_send()
                cp.wait_recv()
            _local_copy(src_refs, land_refs, local_sems, a, scatter, x, y, c).wait()

    srcs = [h[3] for h in handles]
    lands = [h[4] for h in handles]
    res = pl.pallas_call(
        body, name=name, out_shape=[pltpu.HBM(s.shape, s.dtype) for s in srcs] + [pltpu.HBM(l.shape, l.dtype) for l in lands],
        in_specs=[_HBM] * (2 * n) + [_SEM] * (3 * n) + [pl.BlockSpec(memory_space=pl.ANY)], out_specs=[_HBM] * (2 * n),
        input_output_aliases={i: i for i in range(2 * n)},
        compiler_params=pltpu.CompilerParams(has_side_effects=_EFFECT))(
            *srcs, *lands, *[h[0] for h in handles], *[h[1] for h in handles], *[h[2] for h in handles], after)
    return list(res[n:])


def _adamw(name, parts, w, m, v):
    K, n = w.shape
    tr = min(K, 128)

    def body(p_ref, w_ref, m_ref, v_ref, g_ref, d_ref, nm_ref, nv_ref):
        g = p_ref[0].astype(F32)
        for i in range(1, N_DEV):
            g = g + p_ref[i].astype(F32)
        nm = ADAM_B1 * m_ref[...] + (1.0 - ADAM_B1) * g
        nv = ADAM_B2 * v_ref[...] + (1.0 - ADAM_B2) * jnp.square(g)
        m_hat = nm / (1.0 - ADAM_B1 ** ADAM_STEP)
        v_hat = nv / (1.0 - ADAM_B2 ** ADAM_STEP)
        g_ref[...] = g
        d_ref[...] = -ADAM_LR * (m_hat / (jnp.sqrt(v_hat) + ADAM_EPS) + ADAM_WD * w_ref[...])
        nm_ref[...] = nm
        nv_ref[...] = nv

    blk = pl.BlockSpec((tr, n), lambda i: (i, 0))
    return _pc(body, name=name, grid=(K // tr,), in_specs=[pl.BlockSpec((N_DEV, tr, n), lambda i: (0, i, 0)), blk, blk, blk],
               out_specs=[blk] * 4, out_shape=[jax.ShapeDtypeStruct((K, n), F32)] * 4, sem=("parallel",))(parts, w, m, v)


def _pack_small(vals):
    flat = jnp.concatenate([v.reshape(-1) for v in vals])
    return jnp.pad(flat, (0, SMALL_ROWS * 128 - flat.shape[0])).reshape(SMALL_ROWS, 128)


def _unpack_small(packed, shapes):
    flat, out, off = packed.reshape(-1), [], 0
    for s in shapes:
        size = int(np.prod(s))
        out.append(flat[off:off + size].reshape(s))
        off += size
    return out


def kernel(x, norm_mix, w_qkv, w_gate, b_gate, rpb, w_proj_a, w_proj_b, w_out, norm_mlp, w_up, w_down, norm_final, loss_target, m_norm_mix, m_w_qkv, m_w_gate, m_b_gate, m_rpb, m_w_proj_a, m_w_proj_b, m_w_out, m_norm_mlp, m_w_up, m_w_down, m_norm_final, v_norm_mix, v_w_qkv, v_w_gate, v_b_gate, v_rpb, v_w_proj_a, v_w_proj_b, v_w_out, v_norm_mlp, v_w_up, v_w_down, v_norm_final):
    w = dict(norm_mix=norm_mix, w_qkv=w_qkv, w_gate=w_gate, b_gate=b_gate, rpb=rpb, w_proj_a=w_proj_a, w_proj_b=w_proj_b,
             w_out=w_out, norm_mlp=norm_mlp, w_up=w_up, w_down=w_down, norm_final=norm_final)
    m = dict(norm_mix=m_norm_mix, w_qkv=m_w_qkv, w_gate=m_w_gate, b_gate=m_b_gate, rpb=m_rpb, w_proj_a=m_w_proj_a,
             w_proj_b=m_w_proj_b, w_out=m_w_out, norm_mlp=m_norm_mlp, w_up=m_w_up, w_down=m_w_down, norm_final=m_norm_final)
    v = dict(norm_mix=v_norm_mix, w_qkv=v_w_qkv, w_gate=v_w_gate, b_gate=v_b_gate, rpb=v_rpb, w_proj_a=v_w_proj_a,
             w_proj_b=v_w_proj_b, w_out=v_w_out, norm_mlp=v_norm_mlp, w_up=v_w_up, w_down=v_w_down, norm_final=v_norm_final)
    order = ("norm_mix", "w_qkv", "w_gate", "b_gate", "rpb", "w_proj_a", "w_proj_b", "w_out", "norm_mlp", "w_up",
             "w_down", "norm_final")

    stages = dict(qkv=("w_qkv",), gate=("w_gate", "w_proj_a", "w_proj_b", "w_out"), mlp=("w_up", "w_down"))
    first, token = _gather_start("gather_start_qkv", [w["w_qkv"][0].astype(BF16)])
    names = [n for s in ("gate", "mlp") for n in stages[s]]
    rest, token = _gather_start("gather_start", [_behind(w[n][0], token).astype(BF16) for n in names])
    handles = dict(zip(names, rest), w_qkv=first[0])

    def late_weights(stage, after):
        if stage == "qkv":
            forwarded, _ = _gather_forward("gather_forward_qkv", [handles["w_qkv"]], after)
            return tuple(_gather_wait("gather_wait_qkv", forwarded, after[0]))
        if stage == "attention":
            handles["gate"], tok = _gather_forward("gather_forward_gate", [handles[n] for n in stages["gate"]], after)
            return (tok,)
        if stage == "mixer":
            handles["up"], tok = _gather_forward("gather_forward_up", [handles["w_up"]], after)
            return tuple(_gather_wait("gather_wait_gate", handles["gate"], after)) + (tok,)
        if stage == "residual":
            handles["down"], tok = _gather_forward("gather_forward_down", [handles["w_down"]], after)
            return (tok,)
        return tuple(_gather_wait("gather_wait_" + stage, handles[stage], after))

    sent = {}

    def send(stage, grads):
        handles, tok = _send_start("grads_start_" + stage, list(grads.values()), True)
        sent[stage] = (tuple(grads), handles)
        return tok

    loss, grad_x, small = _local_step(x[0], loss_target[0], _behind(norm_mix, token), b_gate, rpb[0], norm_mlp, norm_final,
                                      late_weights, send)

    grad, delta, new_m, new_v = {}, {}, {}, {}
    after = grad_x
    for stage in ("mlp", "proj", "small", "in"):
        if stage == "small":
            all_small = after = _gather_small(_pack_small([small[n] for n in SMALL]), [d[0] for d in delta.values()])
            continue
        names, handles = sent[stage]
        for n, parts in zip(names, _send_wait("grads_wait_" + stage, handles, True, after)):
            res = _adamw("adamw_" + n, parts, w[n][0], m[n][0], v[n][0])
            grad[n], delta[n], new_m[n], new_v[n] = (r[None] for r in res)
            after = res[1]
    res = _adamw("adamw_small", all_small,
                 _pack_small([w[n] for n in SMALL]), _pack_small([m[n] for n in SMALL]), _pack_small([v[n] for n in SMALL]))
    shapes = [w[n].shape for n in SMALL]
    for tree, packed in zip((grad, delta, new_m, new_v), res):
        for n, val in zip(SMALL, _unpack_small(packed, shapes)):
            tree[n] = val

    total = lax.psum(loss[0, 0], ("x", "y", "c"))
    return (total, grad_x[None], *[grad[n] for n in order], *[delta[n] for n in order],
            *[new_m[n] for n in order], *[new_v[n] for n in order])
```

```python
import numpy as np
import jax
import jax.numpy as jnp
from jax import lax
from jax.experimental import pallas as pl
from jax.experimental.pallas import tpu as pltpu

F32 = jnp.float32
BF16 = jnp.bfloat16
MESH = pl.DeviceIdType.MESH

D_MODEL = 2048
HEAD_DIM = 128
N_HEADS = 16
N_HEADS_A = 12
N_HEADS_B = 4
DILATIONS = (1, 4, 16)
HALF_WINDOW = 64
GRID_W = 64
NA_ROWS = 8
NA_COLS = 16
D_FF = 4 * D_MODEL
N_DEV = 8
EPS = 1e-6
NEG = -1e30
SCALE = HEAD_DIM ** -0.5
ADAM_LR, ADAM_B1, ADAM_B2, ADAM_EPS, ADAM_WD, ADAM_STEP = 0.001, 0.9, 0.999, 1e-08, 0.01, 10

VMEM_LIMIT = 56 * 1024 * 1024
TM = 512
TL = 1024
TS = 256
TE = 512
BQ_A = 128


def _pc(body, *, name, grid, in_specs, out_specs, out_shape, scratch=(), sem=None):
    return pl.pallas_call(
        body, name=name, grid=grid, in_specs=in_specs, out_specs=out_specs, out_shape=out_shape,
        scratch_shapes=list(scratch),
        compiler_params=pltpu.CompilerParams(dimension_semantics=sem, vmem_limit_bytes=VMEM_LIMIT))


_DOT_DIMS = {"nn": (((1,), (0,)), ((), ())), "nt": (((1,), (1,)), ((), ())), "tn": (((0,), (0,)), ((), ()))}


def _dot(a, b, mode, precision=None):
    if mode == "nn_shards":
        return jnp.concatenate([_dot(a, b[s], "nn") for s in range(b.shape[0])], axis=1)
    if mode == "nt_shards":
        n = b.shape[2]
        out = _dot(a[:, :n], b[0], "nt")
        for s in range(1, b.shape[0]):
            out = out + _dot(a[:, s * n:(s + 1) * n], b[s], "nt")
        return out
    return lax.dot_general(a, b, _DOT_DIMS[mode], preferred_element_type=F32, precision=precision)


def _mm(name, grid, pairs, extras, outs, epilogue, acc_shape, sem):
    n_pairs, n_extra, n_out = len(pairs), len(extras), len(outs)
    k_axis = len(grid) - 1
    nk = grid[k_axis]

    def body(*refs):
        pids = [pl.program_id(ax) for ax in range(len(grid))]
        ab = refs[:2 * n_pairs]
        ex = refs[2 * n_pairs:2 * n_pairs + n_extra]
        o = refs[2 * n_pairs + n_extra:2 * n_pairs + n_extra + n_out]
        part = None
        for p, pair in enumerate(pairs):
            a = ab[2 * p][...]
            if pair[5] is not None:
                a = pair[5](a)
            d = _dot(a, ab[2 * p + 1][...], pair[4])
            part = d if part is None else part + d
        if nk == 1:
            epilogue(part, ex, o, pids)
        else:
            acc = refs[-1]
            k = pids[k_axis]

            @pl.when(k == 0)
            def _():
                acc[...] = part

            @pl.when(k > 0)
            def _():
                acc[...] += part

            @pl.when(k == nk - 1)
            def _():
                epilogue(acc[...], ex, o, pids)

    operands, in_specs = [], []
    for a, a_spec, b, b_spec, _, _ in pairs:
        operands += [a, b]
        in_specs += [a_spec, b_spec]
    for e, e_spec in extras:
        operands.append(e)
        in_specs.append(e_spec)
    res = _pc(body, name=name, grid=grid, in_specs=in_specs, out_specs=[s for _, s in outs],
              out_shape=[o for o, _ in outs], scratch=[pltpu.VMEM(acc_shape, F32)] if nk > 1 else [], sem=sem)(*operands)
    return res


def _rmsnorm_fwd(x, g, name):
    T, D = x.shape

    def body(x_ref, g_ref, h_ref, r_ref):
        xv = x_ref[...]
        r = lax.rsqrt(jnp.mean(xv * xv, axis=-1, keepdims=True) + EPS)
        h_ref[...] = (xv * r * g_ref[...]).astype(BF16)
        r_ref[...] = r

    return _pc(body, name=name, grid=(T // TM,),
               in_specs=[pl.BlockSpec((TM, D), lambda i: (i, 0)), pl.BlockSpec((1, D), lambda i: (0, 0))],
               out_specs=[pl.BlockSpec((TM, D), lambda i: (i, 0)), pl.BlockSpec((TM, 1), lambda i: (i, 0))],
               out_shape=[jax.ShapeDtypeStruct((T, D), BF16), jax.ShapeDtypeStruct((T, 1), F32)],
               sem=("parallel",))(x, g)


def _rmsnorm_bwd_block(dh, x, r, g, dres):
    xh = x * r
    dxh = dh * g
    dx = dres + r * (dxh - xh * jnp.mean(dxh * xh, axis=-1, keepdims=True))
    return dx, jnp.sum(dh * xh, axis=0, keepdims=True)


def _rmsnorm_bwd(name, dh, x, r, g, dres, want_bf16):
    T, D = x.shape

    def body(dh_ref, x_ref, r_ref, g_ref, dres_ref, *outs):
        dx, dgp = _rmsnorm_bwd_block(dh_ref[...], x_ref[...], r_ref[...], g_ref[...], dres_ref[...])
        outs[0][...] = dx
        if want_bf16:
            outs[1][...] = dx.astype(BF16)
        first = pl.program_id(0) == 0

        @pl.when(first)
        def _():
            outs[-1][...] = dgp

        @pl.when(jnp.logical_not(first))
        def _():
            outs[-1][...] += dgp

    row = pl.BlockSpec((TE, D), lambda i: (i, 0))
    vec = pl.BlockSpec((1, D), lambda i: (0, 0))
    n_row = 2 if want_bf16 else 1
    return _pc(body, name=name, grid=(T // TE,), in_specs=[row, row, pl.BlockSpec((TE, 1), lambda i: (i, 0)), vec, row],
               out_specs=[row] * n_row + [vec],
               out_shape=[jax.ShapeDtypeStruct((T, D), F32)] + [jax.ShapeDtypeStruct((T, D), BF16)] * (n_row - 1)
               + [jax.ShapeDtypeStruct((1, D), F32)], sem=("arbitrary",))(dh, x, r, g, dres)


def _final_norm_loss(x3, g, tgt):
    T, D = x3.shape

    def body(x_ref, g_ref, t_ref, loss_ref, dx_ref, dxb_ref, dg_ref):
        i = pl.program_id(0)
        xv = x_ref[...]
        gv = g_ref[...]
        r = lax.rsqrt(jnp.mean(xv * xv, axis=-1, keepdims=True) + EPS)
        xh = xv * r
        e = xh * gv - t_ref[...]
        part = 0.5 * jnp.sum(jnp.mean(e * e, axis=-1, keepdims=True))
        dy = e * (1.0 / D)
        dxh = dy * gv
        dx = r * (dxh - xh * jnp.mean(dxh * xh, axis=-1, keepdims=True))
        dx_ref[...] = dx
        dxb_ref[...] = dx.astype(BF16)
        dgp = jnp.sum(dy * xh, axis=0, keepdims=True)

        @pl.when(i == 0)
        def _():
            loss_ref[...] = jnp.zeros(loss_ref.shape, F32) + part
            dg_ref[...] = dgp

        @pl.when(i > 0)
        def _():
            loss_ref[...] += part
            dg_ref[...] += dgp

    row = pl.BlockSpec((TE, D), lambda i: (i, 0))
    vec = pl.BlockSpec((1, D), lambda i: (0, 0))
    return _pc(body, name="final_norm_loss", grid=(T // TE,), in_specs=[row, vec, row],
               out_specs=[pl.BlockSpec((8, 128), lambda i: (0, 0)), row, row, vec],
               out_shape=[jax.ShapeDtypeStruct((8, 128), F32), jax.ShapeDtypeStruct((T, D), F32),
                          jax.ShapeDtypeStruct((T, D), BF16), jax.ShapeDtypeStruct((1, D), F32)],
               sem=("arbitrary",))(x3, g, tgt)


def _alibi_slopes():
    return np.asarray(2.0 ** (-8.0 * np.arange(1, N_HEADS_A + 1) / N_HEADS_A), dtype=np.float32)


def _col_to_row(col):
    return jnp.transpose(jnp.broadcast_to(col, (BQ_A, BQ_A)))[0:1, :]


def _row_to_col(row):
    return jnp.transpose(jnp.broadcast_to(row, (BQ_A, BQ_A)))[:, 0:1]


def _row_spec(rows_per_step, first=0):
    return pl.BlockSpec((None, rows_per_step // BQ_A, BQ_A), lambda j, i: (first + j, i, 0))


def _a_windows(T):
    half = [HALF_WINDOW * d for d in DILATIONS]
    return half, [min(BQ_A + 2 * w, T) for w in half]


def _a_bias(slope, g, offset, wk, half):
    d = DILATIONS[g]
    dlt = offset + lax.broadcasted_iota(jnp.int32, (BQ_A, wk), 1) - lax.broadcasted_iota(jnp.int32, (BQ_A, wk), 0)
    ad = jnp.maximum(dlt, -dlt)
    ok = (ad <= half) & ((ad & (d - 1)) == 0)
    return jnp.where(ok, -slope * ad.astype(F32), NEG)


def _a_scores(q, kw, slope, g, offset, wk, half):
    return _dot(q, kw, "nt") * SCALE + _a_bias(slope, g, offset, wk, half)


N_JOINT = 2


SUB_J = 8
BS_J = SUB_J * BQ_A


def _a_qkv_specs(T):
    specs = []
    for g in range(N_JOINT):
        specs.append(pl.BlockSpec((BS_J, HEAD_DIM), lambda j, i, g=g: (i, 4 * g + j)))
        specs.append(pl.BlockSpec((T, HEAD_DIM), lambda j, i, g=g: (0, N_HEADS + 4 * g + j)))
        specs.append(pl.BlockSpec((T, HEAD_DIM), lambda j, i, g=g: (0, 2 * N_HEADS + 4 * g + j)))
    return specs


def _a_window(step, u, g, T, half, wks):
    t0 = (step * SUB_J + u) * BQ_A
    ws = pl.multiple_of(jnp.clip(t0 - half[g], 0, T - wks[g]), 64)
    return pl.ds(ws, wks[g]), ws - t0


def _attn_a_fwd(qkv, slopes, o_last, lse_last):
    T = qkv.shape[0]
    half, wks = _a_windows(T)

    def body(sl_ref, *refs):
        qkv_refs, (o_ref, l_ref, y_ref, lse_ref) = refs[:3 * N_JOINT], refs[3 * N_JOINT:]
        j = pl.program_id(0)
        for u in range(SUB_J):
            rows = slice(u * BQ_A, (u + 1) * BQ_A)
            ss, vws = [], []
            lse_last = _row_to_col(l_ref[u:u + 1, :])
            m = lse_last
            for g in range(N_JOINT):
                q_ref, k_ref, v_ref = qkv_refs[3 * g:3 * g + 3]
                win, offset = _a_window(pl.program_id(1), u, g, T, half, wks)
                ss.append(_a_scores(q_ref[rows, :], k_ref[win, :], sl_ref[4 * g + j], g, offset, wks[g], half[g]))
                vws.append(v_ref[win, :])
                m = jnp.maximum(m, jnp.max(ss[g], axis=-1, keepdims=True))
            den = jnp.exp(lse_last - m)
            acc = den * o_ref[rows, :].astype(F32)
            for g in range(N_JOINT):
                p = jnp.exp(ss[g] - m)
                den = den + jnp.sum(p, axis=-1, keepdims=True)
                acc = acc + _dot(p.astype(BF16), vws[g], "nn")
            y_ref[rows, :] = (acc / den).astype(BF16)
            lse_ref[u:u + 1, :] = _col_to_row(m + jnp.log(den))

    blk = pl.BlockSpec((BS_J, HEAD_DIM), lambda j, i: (i, j))
    col = _row_spec(BS_J)
    return _pc(body, name="attn_a_fwd", grid=(4, T // BS_J),
               in_specs=[pl.BlockSpec(memory_space=pltpu.SMEM)] + _a_qkv_specs(T) + [blk, col], out_specs=[blk, col],
               out_shape=[jax.ShapeDtypeStruct((T, 4 * HEAD_DIM), BF16), jax.ShapeDtypeStruct((4, T // BQ_A, BQ_A), F32)],
               sem=("parallel", "parallel"))(slopes, *([qkv] * (3 * N_JOINT)), o_last, lse_last)


def _attn_a_bwd(qkv, slopes, dy, y, lse):
    T = qkv.shape[0]
    steps = T // BS_J
    half, wks = _a_windows(T)

    def body(sl_ref, *refs):
        qkv_refs = refs[:3 * N_JOINT]
        dy_ref, y_ref, lse_ref, dq_ref, dk_ref, dv_ref, delta_ref, dk_acc, dv_acc = refs[3 * N_JOINT:]
        j = pl.program_id(0)
        step = pl.program_id(1)

        @pl.when(step == 0)
        def _():
            dk_acc[...] = jnp.zeros_like(dk_acc)
            dv_acc[...] = jnp.zeros_like(dv_acc)

        for u in range(SUB_J):
            rows = slice(u * BQ_A, (u + 1) * BQ_A)
            dyv = dy_ref[rows, :]
            delta = jnp.sum(dyv.astype(F32) * y_ref[rows, :].astype(F32), axis=-1, keepdims=True)
            delta_ref[u:u + 1, :] = _col_to_row(delta)
            lse_v = _row_to_col(lse_ref[u:u + 1, :])
            for g in range(N_JOINT):
                q_ref, k_ref, v_ref = qkv_refs[3 * g:3 * g + 3]
                win, offset = _a_window(step, u, g, T, half, wks)
                qv = q_ref[rows, :]
                kw = k_ref[win, :]
                s = _a_scores(qv, kw, sl_ref[4 * g + j], g, offset, wks[g], half[g])
                p = jnp.exp(s - lse_v)
                dp = _dot(dyv, v_ref[win, :], "nt")
                ds = (p * (dp - delta)).astype(BF16)
                dq_ref[g, rows, :] = (_dot(ds, kw, "nn") * SCALE).astype(BF16)
                dk_acc[g, win, :] += _dot(ds, qv, "tn") * SCALE
                dv_acc[g, win, :] += _dot(p.astype(BF16), dyv, "tn")

        @pl.when(step == steps - 1)
        def _():
            dk_ref[...] = dk_acc[...].astype(BF16)
            dv_ref[...] = dv_acc[...].astype(BF16)

    blk = pl.BlockSpec((BS_J, HEAD_DIM), lambda j, i: (i, j))
    col = _row_spec(BS_J)
    whole = pl.BlockSpec((N_JOINT, T, HEAD_DIM), lambda j, i: (0, 0, j))
    full = jax.ShapeDtypeStruct((N_JOINT, T, 4 * HEAD_DIM), BF16)
    return _pc(body, name="attn_a_bwd", grid=(4, steps),
               in_specs=[pl.BlockSpec(memory_space=pltpu.SMEM)] + _a_qkv_specs(T) + [blk, blk, col],
               out_specs=[pl.BlockSpec((N_JOINT, BS_J, HEAD_DIM), lambda j, i: (0, i, j)), whole, whole, col],
               out_shape=[full, full, full, jax.ShapeDtypeStruct((4, T // BQ_A, BQ_A), F32)],
               scratch=[pltpu.VMEM((N_JOINT, T, HEAD_DIM), F32), pltpu.VMEM((N_JOINT, T, HEAD_DIM), F32)],
               sem=("parallel", "arbitrary"))(slopes, *([qkv] * (3 * N_JOINT)), dy, y, lse)


SUB_A = 8


def _to_residue_major(a, d, axis=0):
    if d == 1:
        return a
    sh = a.shape
    a = a.reshape(sh[:axis] + (sh[axis] // d, d) + sh[axis + 1:])
    return jnp.swapaxes(a, axis, axis + 1).reshape(sh)


def _to_token_order(a, d, axis=0):
    if d == 1:
        return a
    sh = a.shape
    a = a.reshape(sh[:axis] + (d, sh[axis] // d) + sh[axis + 1:])
    return jnp.swapaxes(a, axis, axis + 1).reshape(sh)


def _band_bias(slope, offset, wk):
    dlt = offset + lax.broadcasted_iota(jnp.int32, (BQ_A, wk), 1) - lax.broadcasted_iota(jnp.int32, (BQ_A, wk), 0)
    ad = jnp.maximum(dlt, -dlt)
    return jnp.where(ad <= HALF_WINDOW, -slope * ad.astype(F32), NEG)


def _band_window(step, u, seg, wk):
    t0 = (step * SUB_A + u) * BQ_A
    lo = (t0 // seg) * seg
    ws = pl.multiple_of(jnp.clip(t0 - HALF_WINDOW, lo, lo + seg - wk), 64)
    return slice(u * BQ_A, (u + 1) * BQ_A), pl.ds(ws, wk), ws - t0


def _band_specs(T, cols):
    bs = SUB_A * BQ_A
    assert T % bs == 0 and T // max(DILATIONS) >= BQ_A, "a query sub-block must lie inside one segment"
    return [pl.BlockSpec(memory_space=pltpu.SMEM),
            pl.BlockSpec((bs, HEAD_DIM), lambda j, i: (i, cols[0] + j)),
            pl.BlockSpec((T, HEAD_DIM), lambda j, i: (0, cols[1] + j)),
            pl.BlockSpec((T, HEAD_DIM), lambda j, i: (0, cols[2] + j))]


def _banded_fwd(name, src, cols, seg, slopes):
    T = src.shape[0]
    bs = SUB_A * BQ_A
    wk = min(BQ_A + 2 * HALF_WINDOW, seg)

    def body(sl_ref, q_ref, k_ref, v_ref, o_ref, lse_ref):
        slope = sl_ref[pl.program_id(0)]
        for u in range(SUB_A):
            rows, win, offset = _band_window(pl.program_id(1), u, seg, wk)
            s = _dot(q_ref[rows, :], k_ref[win, :], "nt") * SCALE + _band_bias(slope, offset, wk)
            m = jnp.max(s, axis=-1, keepdims=True)
            p = jnp.exp(s - m)
            den = jnp.sum(p, axis=-1, keepdims=True)
            o_ref[rows, :] = (_dot(p.astype(BF16), v_ref[win, :], "nn") / den).astype(BF16)
            lse_ref[u:u + 1, :] = _col_to_row(m + jnp.log(den))

    return _pc(body, name=name, grid=(4, T // bs), in_specs=_band_specs(T, cols),
               out_specs=[pl.BlockSpec((bs, HEAD_DIM), lambda j, i: (i, j)), _row_spec(bs)],
               out_shape=[jax.ShapeDtypeStruct((T, 4 * HEAD_DIM), BF16), jax.ShapeDtypeStruct((4, T // BQ_A, BQ_A), F32)],
               sem=("parallel", "parallel"))(slopes, src, src, src)


def _banded_bwd(name, src, cols, seg, slopes, dy, delta_lse):
    T = src.shape[0]
    bs = SUB_A * BQ_A
    steps = T // bs
    wk = min(BQ_A + 2 * HALF_WINDOW, seg)

    def body(sl_ref, q_ref, k_ref, v_ref, dy_ref, delta_ref, lse_ref, dq_ref, dk_ref, dv_ref, dk_acc, dv_acc):
        slope = sl_ref[pl.program_id(0)]
        step = pl.program_id(1)

        @pl.when(step == 0)
        def _():
            dk_acc[...] = jnp.zeros_like(dk_acc)
            dv_acc[...] = jnp.zeros_like(dv_acc)

        for u in range(SUB_A):
            rows, win, offset = _band_window(step, u, seg, wk)
            qv, kw, dyv = q_ref[rows, :], k_ref[win, :], dy_ref[rows, :]
            s = _dot(qv, kw, "nt") * SCALE + _band_bias(slope, offset, wk)
            p = jnp.exp(s - _row_to_col(lse_ref[u:u + 1, :]))
            ds = (p * (_dot(dyv, v_ref[win, :], "nt") - _row_to_col(delta_ref[u:u + 1, :]))).astype(BF16)
            dq_ref[rows, :] = (_dot(ds, kw, "nn") * SCALE).astype(BF16)
            dk_acc[win, :] += _dot(ds, qv, "tn") * SCALE
            dv_acc[win, :] += _dot(p.astype(BF16), dyv, "tn")

        @pl.when(step == steps - 1)
        def _():
            dk_ref[...] = dk_acc[...].astype(BF16)
            dv_ref[...] = dv_acc[...].astype(BF16)

    blk = pl.BlockSpec((bs, HEAD_DIM), lambda j, i: (i, j))
    whole = pl.BlockSpec((T, HEAD_DIM), lambda j, i: (0, j))
    full = jax.ShapeDtypeStruct((T, 4 * HEAD_DIM), BF16)
    return _pc(body, name=name, grid=(4, steps),
               in_specs=_band_specs(T, cols) + [blk, _row_spec(bs), _row_spec(bs, 4)],
               out_specs=[blk, whole, whole], out_shape=[full, full, full],
               scratch=[pltpu.VMEM((T, HEAD_DIM), F32), pltpu.VMEM((T, HEAD_DIM), F32)],
               sem=("parallel", "arbitrary"))(slopes, src, src, src, dy, delta_lse, delta_lse)


def _last_group_source(qkv):
    third = qkv.shape[1] // 3
    own = jnp.concatenate([qkv[:, t * third + 512 * N_JOINT:t * third + 512 * (N_JOINT + 1)] for t in range(3)], axis=1)
    return _to_residue_major(own, DILATIONS[N_JOINT])


def _dilated_fwd(qkv, slopes):
    d = DILATIONS[N_JOINT]
    last_src = _last_group_source(qkv)
    o, lse = _banded_fwd("attn_a_last_fwd", last_src, (0, 4, 8), qkv.shape[0] // d, slopes[4 * N_JOINT:] * float(d))
    lse = _to_token_order(lse.reshape(4, -1), d, axis=1).reshape(lse.shape)
    y, joint_lse = _attn_a_fwd(qkv, slopes, _to_token_order(o, d), lse)
    return y, joint_lse, last_src


def _dilated_bwd(qkv, last_src, slopes, dy, y, lse):
    d = DILATIONS[N_JOINT]
    dq, dk, dv, delta = _attn_a_bwd(qkv, slopes, dy, y, lse)
    delta_lse = jnp.concatenate([delta, lse], axis=0)
    delta_lse = _to_residue_major(delta_lse.reshape(8, -1), d, axis=1).reshape(delta_lse.shape)
    last = _banded_bwd("attn_a_last_bwd", last_src, (0, 4, 8), qkv.shape[0] // d,
                       slopes[4 * N_JOINT:] * float(d), _to_residue_major(dy, d), delta_lse)
    return [[joint[g] for g in range(N_JOINT)] + [_to_token_order(own, d)] for joint, own in zip((dq, dk, dv), last)]


N_DR = 2 * NA_ROWS - 1
N_DC = 2 * NA_COLS - 1
WK_B = NA_ROWS * GRID_W


def _toeplitz_consts():
    col = np.arange(GRID_W)
    start = np.clip(col - NA_COLS // 2, 0, GRID_W - NA_COLS)
    ok = (col[None, :] >= start[:, None]) & (col[None, :] < start[:, None] + NA_COLS)
    dc = np.clip(col[None, :] - col[:, None], -(NA_COLS - 1), NA_COLS - 1) + NA_COLS - 1
    sel = np.zeros((128, GRID_W * GRID_W), np.float32)
    sel[dc.reshape(-1), np.arange(GRID_W * GRID_W)] = ok.reshape(-1).astype(np.float32)
    mask = np.where(ok, 0.0, NEG).astype(np.float32).reshape(1, -1)
    return sel, mask


def _rpb_expand(rpb_rows, sel, mask):
    def body(r_ref, s_ref, m_ref, o_ref):
        o_ref[...] = _dot(r_ref[...], s_ref[...], "nn", precision=lax.Precision.HIGHEST) + m_ref[...]

    n = rpb_rows.shape[0]
    return _pc(body, name="rpb_expand", grid=(1,),
               in_specs=[pl.BlockSpec((n, 128), lambda i: (0, 0)), pl.BlockSpec(sel.shape, lambda i: (0, 0)),
                         pl.BlockSpec(mask.shape, lambda i: (0, 0))],
               out_specs=pl.BlockSpec((n, sel.shape[1]), lambda i: (0, 0)),
               out_shape=jax.ShapeDtypeStruct((n, sel.shape[1]), F32), sem=("arbitrary",))(rpb_rows, sel, mask)


def _rpb_reduce(dtiles, sel):
    def body(t_ref, s_ref, o_ref):
        o_ref[...] = _dot(t_ref[...], s_ref[...], "nt", precision=lax.Precision.HIGHEST)

    n = dtiles.shape[0]
    return _pc(body, name="rpb_reduce", grid=(1,),
               in_specs=[pl.BlockSpec(dtiles.shape, lambda i: (0, 0)), pl.BlockSpec(sel.shape, lambda i: (0, 0))],
               out_specs=pl.BlockSpec((n, 128), lambda i: (0, 0)),
               out_shape=jax.ShapeDtypeStruct((n, 128), F32), sem=("arbitrary",))(dtiles, sel)


ROWS_B = 16
BQ_B = ROWS_B * GRID_W


def _b_specs(T):
    return [pl.BlockSpec((BQ_B, HEAD_DIM), lambda h, r: (r, N_HEADS_A + h)),
            pl.BlockSpec((T, HEAD_DIM), lambda h, r: (0, N_HEADS + N_HEADS_A + h)),
            pl.BlockSpec((T, HEAD_DIM), lambda h, r: (0, 2 * N_HEADS + N_HEADS_A + h)),
            pl.BlockSpec((None, N_DR, GRID_W, GRID_W), lambda h, r: (h, 0, 0, 0))]


def _b_row(step, u, rows, t_ref, bias_ref):
    r = step * ROWS_B + u
    rs = jnp.clip(r - NA_ROWS // 2, 0, rows - NA_ROWS)
    first_tile = rs - r + NA_ROWS - 1
    for a in range(NA_ROWS):
        bias_ref[u, :, a * GRID_W:(a + 1) * GRID_W] = t_ref[first_tile + a]
    return slice(u * GRID_W, (u + 1) * GRID_W), pl.ds(pl.multiple_of(rs * GRID_W, GRID_W), WK_B), first_tile


def _attn_b_fwd(qkv, tiles):
    T = qkv.shape[0]
    rows = T // GRID_W

    def body(q_ref, k_ref, v_ref, t_ref, y_ref, lse_ref, bias_ref):
        for u in range(ROWS_B):
            blk, win, _ = _b_row(pl.program_id(1), u, rows, t_ref, bias_ref)
            s = _dot(q_ref[blk, :], k_ref[win, :], "nt") * SCALE + bias_ref[u]
            m = jnp.max(s, axis=-1, keepdims=True)
            p = jnp.exp(s - m)
            den = jnp.sum(p, axis=-1, keepdims=True)
            y_ref[blk, :] = (_dot(p.astype(BF16), v_ref[win, :], "nn") / den).astype(BF16)
            lse_ref[blk, :] = m + jnp.log(den)

    return _pc(body, name="attn_b_fwd", grid=(N_HEADS_B, rows // ROWS_B), in_specs=_b_specs(T),
               out_specs=[pl.BlockSpec((BQ_B, HEAD_DIM), lambda h, r: (r, h)),
                          pl.BlockSpec((None, BQ_B, 1), lambda h, r: (h, r, 0))],
               out_shape=[jax.ShapeDtypeStruct((T, N_HEADS_B * HEAD_DIM), BF16),
                          jax.ShapeDtypeStruct((N_HEADS_B, T, 1), F32)],
               scratch=[pltpu.VMEM((ROWS_B, GRID_W, WK_B), F32)], sem=("parallel", "parallel"))(qkv, qkv, qkv, tiles)


def _attn_b_bwd(qkv, tiles, dy, y, lse):
    T = qkv.shape[0]
    rows = T // GRID_W
    steps = rows // ROWS_B

    def body(q_ref, k_ref, v_ref, t_ref, dy_ref, y_ref, lse_ref, dq_ref, dk_ref, dv_ref, dt_ref,
             bias_ref, ds_ref, dk_acc, dv_acc):
        step = pl.program_id(1)

        @pl.when(step == 0)
        def _():
            dk_acc[...] = jnp.zeros_like(dk_acc)
            dv_acc[...] = jnp.zeros_like(dv_acc)
            dt_ref[...] = jnp.zeros_like(dt_ref)

        for u in range(ROWS_B):
            blk, win, first_tile = _b_row(step, u, rows, t_ref, bias_ref)
            qv = q_ref[blk, :]
            kw = k_ref[win, :]
            dyv = dy_ref[blk, :]
            s = _dot(qv, kw, "nt") * SCALE + bias_ref[u]
            p = jnp.exp(s - lse_ref[blk, :])
            delta = jnp.sum(dyv.astype(F32) * y_ref[blk, :].astype(F32), axis=-1, keepdims=True)
            ds = p * (_dot(dyv, v_ref[win, :], "nt") - delta)
            ds_ref[u] = ds
            for a in range(NA_ROWS):
                dt_ref[first_tile + a] += ds_ref[u, :, a * GRID_W:(a + 1) * GRID_W]
            dsb = ds.astype(BF16)
            dq_ref[blk, :] = (_dot(dsb, kw, "nn") * SCALE).astype(BF16)
            dk_acc[win, :] += _dot(dsb, qv, "tn") * SCALE
            dv_acc[win, :] += _dot(p.astype(BF16), dyv, "tn")

        @pl.when(step == steps - 1)
        def _():
            dk_ref[...] = dk_acc[...].astype(BF16)
            dv_ref[...] = dv_acc[...].astype(BF16)

    blk_spec = pl.BlockSpec((BQ_B, HEAD_DIM), lambda h, r: (r, h))
    whole = pl.BlockSpec((T, HEAD_DIM), lambda h, r: (0, h))
    full = jax.ShapeDtypeStruct((T, N_HEADS_B * HEAD_DIM), BF16)
    return _pc(body, name="attn_b_bwd", grid=(N_HEADS_B, steps),
               in_specs=_b_specs(T) + [blk_spec, blk_spec, pl.BlockSpec((None, BQ_B, 1), lambda h, r: (h, r, 0))],
               out_specs=[blk_spec, whole, whole, pl.BlockSpec((None, N_DR, GRID_W, GRID_W), lambda h, r: (h, 0, 0, 0))],
               out_shape=[full, full, full, jax.ShapeDtypeStruct((N_HEADS_B, N_DR, GRID_W, GRID_W), F32)],
               scratch=[pltpu.VMEM((ROWS_B, GRID_W, WK_B), F32), pltpu.VMEM((ROWS_B, GRID_W, WK_B), F32),
                        pltpu.VMEM((T, HEAD_DIM), F32), pltpu.VMEM((T, HEAD_DIM), F32)],
               sem=("parallel", "arbitrary"))(qkv, qkv, qkv, tiles, dy, y, lse)


def _sds(shape, dtype):
    return jax.ShapeDtypeStruct(shape, dtype)


def _behind(value, token):
    return value if token is None else value + token[0, 0]


def _local_step(x, tgt, norm_mix, b_gate, rpb, norm_mlp, norm_final, late_weights, send):
    T, D = x.shape
    nt = T // TM
    PP, PA = ("parallel", "parallel", "arbitrary"), ("arbitrary", "arbitrary", "arbitrary")
    slopes = jnp.asarray(_alibi_slopes())
    sel_np, mask_np = _toeplitz_consts()
    sel, mask = jnp.asarray(sel_np), jnp.asarray(mask_np)

    def rows(w, tm=TM):
        return pl.BlockSpec((tm, w), lambda i, j, k: (i, j))

    def full_rows(tm, w=D):
        return pl.BlockSpec((tm, w), lambda i, j, k: (i, 0))

    def tall(w):
        return pl.BlockSpec((T, w), lambda j, i, k: (0, i))

    def tall_j(w):
        return pl.BlockSpec((T, w), lambda j, i, k: (0, j))

    rvec = pl.BlockSpec((TS, 1), lambda i, j, k: (i, 0))
    gvec = pl.BlockSpec((1, D), lambda i, j, k: (0, 0))

    h1, r1 = _rmsnorm_fwd(x, norm_mix, "rmsnorm_mix")
    rpb_rows = jnp.pad(rpb.reshape(N_HEADS_B * N_DR, N_DC), ((0, 0), (0, 128 - N_DC)))
    tiles = _rpb_expand(rpb_rows, sel, mask).reshape(N_HEADS_B, N_DR, GRID_W, GRID_W)
    (wq,) = late_weights("qkv", (h1, tiles))

    def store_bf16(acc, ex, o, pids):
        o[0][...] = acc.astype(BF16)

    (qkv,) = _mm("qkv_proj", (T // TL, N_DEV // 2, 1),
                 [(h1, full_rows(TL), wq, pl.BlockSpec((2, D, 768), lambda i, j, k: (j, 0, 0)), "nn_shards", None)],
                 [], [(_sds((T, 3 * D), BF16), rows(1536, TL))], store_bf16, None, PP)

    def gate_epi(acc, ex, o, pids):
        o[0][...] = jax.nn.sigmoid(acc + ex[0][...]).astype(BF16)

    late_weights("attention", qkv)
    ya, lse_a, last_src = _dilated_fwd(qkv, slopes)
    yb, lse_b = _attn_b_fwd(qkv, tiles)

    wg, wpa, wpb, wo, token = late_weights("mixer", ya)
    (gates,) = _mm("gate_proj", (T // TL, N_DEV // 2, 1),
                   [(h1, full_rows(TL), wg, pl.BlockSpec((2, D, 512), lambda i, j, k: (j, 0, 0)), "nn_shards", None)],
                   [(_behind(b_gate, token), pl.BlockSpec((1, 1024), lambda i, j, k: (0, j)))],
                   [(_sds((T, 2 * D), BF16), rows(1024, TL))], gate_epi, None, PP)

    def merge_body(ya_ref, wa_ref, yb_ref, wb_ref, ga_ref, gb_ref, pa_ref, pb_ref, mg_ref):
        ya_v, yb_v = ya_ref[...], yb_ref[...]
        for s in range(N_DEV):
            cols = slice(s * 256, (s + 1) * 256)
            pa = _dot(ya_v, wa_ref[s], "nn")
            pb = _dot(yb_v, wb_ref[s], "nn")
            pa_ref[:, cols] = pa.astype(BF16)
            pb_ref[:, cols] = pb.astype(BF16)
            mg_ref[:, cols] = (ga_ref[:, cols] * pa + gb_ref[:, cols] * pb).astype(BF16)

    wo2 = wo.reshape(D, D)
    y_spec = pl.BlockSpec((TS, 512), lambda i: (i, 0))
    wp_spec = pl.BlockSpec((N_DEV, 512, 256), lambda i: (0, 0, 0))
    c_spec = pl.BlockSpec((TS, D), lambda i: (i, 0))
    pa, pb, merged = _pc(merge_body, name="merge_proj", grid=(T // TS,),
                         in_specs=[y_spec, wp_spec, y_spec, wp_spec, c_spec, pl.BlockSpec((TS, D), lambda i: (i, 1))],
                         out_specs=[c_spec, c_spec, c_spec],
                         out_shape=[_sds((T, D), BF16)] * 3, sem=("parallel",))(ya, wpa, yb, wpb, gates, gates)

    def add_res(acc, ex, o, pids):
        o[0][...] = ex[0][...] + acc

    def out_norm_epi(acc, ex, o, pids):
        x2v = ex[0][...] + acc
        r = lax.rsqrt(jnp.mean(x2v * x2v, axis=-1, keepdims=True) + EPS)
        o[0][...] = x2v
        o[1][...] = (x2v * r * ex[1][...]).astype(BF16)
        o[2][...] = r

    x2, h2, r2 = _mm("out_proj", (nt, 1, 1),
                     [(merged, full_rows(TM), wo2, pl.BlockSpec((D, D), lambda i, j, k: (0, 0)), "nn", None)],
                     [(x, full_rows(TM)), (norm_mlp, gvec)],
                     [(_sds((T, D), F32), full_rows(TM)), (_sds((T, D), BF16), full_rows(TM)),
                      (_sds((T, 1), F32), pl.BlockSpec((TM, 1), lambda i, j, k: (i, 0)))], out_norm_epi, None, PP)

    (token,) = late_weights("residual", x2)
    (wu,) = late_weights("up", h2)
    behind = lambda token: [] if token is None else [(token, pl.BlockSpec(memory_space=pl.ANY))]

    def up_epi(acc, ex, o, pids):
        o[0][...] = acc.astype(BF16)
        r = jnp.maximum(acc, 0.0)
        o[1][...] = (r * r).astype(BF16)

    u, act = _mm("mlp_up", (T // TL, N_DEV, 1),
                 [(h2, full_rows(TL), wu, pl.BlockSpec((None, D, 1024), lambda i, j, k: (j, 0, 0)), "nn", None)],
                 behind(token), [(_sds((T, D_FF), BF16), rows(1024, TL))] * 2, up_epi, None, PP)
    (wd,) = late_weights("down", act)
    wd2 = wd.reshape(D_FF, D)
    half_ji = pl.BlockSpec((TS, D // 2), lambda j, i, k: (i, j))
    (x3,) = _mm("mlp_down", (2, T // TS, 1),
                [(act, pl.BlockSpec((TS, D_FF), lambda j, i, k: (i, 0)), wd2,
                  pl.BlockSpec((D_FF, D // 2), lambda j, i, k: (0, j)), "nn", None)],
                [(x2, half_ji)], [(_sds((T, D), F32), half_ji)], add_res, None, PP)

    loss, dx3, dx3b, dg3 = _final_norm_loss(x3, norm_final.reshape(1, D), tgt)

    def du_epi(acc, ex, o, pids):
        o[0][...] = (acc * (2.0 * jnp.maximum(ex[0][...].astype(F32), 0.0))).astype(BF16)

    (du,) = _mm("mlp_down_dx", (T // TL, N_DEV, 1),
                [(dx3b, full_rows(TL), wd2, pl.BlockSpec((1024, D), lambda i, j, k: (j, 0)), "nt", None)],
                [(u, rows(1024, TL))], [(_sds((T, D_FF), BF16), rows(1024, TL))], du_epi, None, PP)
    (dwd,) = _mm("mlp_down_dw", (D // 1024, D_FF // 512, 1), [(act, tall(512), dx3b, tall_j(1024), "tn", None)],
                 [], [(_sds((D_FF, D), BF16), pl.BlockSpec((512, 1024), lambda j, i, k: (i, j)))], store_bf16, None, PP)
    dwd = dwd.reshape(N_DEV, D_FF // N_DEV, D)
    (dwu,) = _mm("mlp_up_dw", (N_DEV, D // 512, 1), [(h2, tall(512), du, tall_j(1024), "tn", None)],
                 [], [(_sds((N_DEV, D, 1024), BF16), pl.BlockSpec((None, 512, 1024), lambda j, i, k: (j, i, 0)))],
                 store_bf16, None, PP)

    def store_f32(acc, ex, o, pids):
        o[0][...] = acc

    out_ji = pl.BlockSpec((TM, 512), lambda j, i, k: (i, j))
    a_ji = lambda w: pl.BlockSpec((TM, w), lambda j, i, k: (i, 0))
    w_ji = lambda n: pl.BlockSpec((N_DEV, 512, n), lambda j, i, k: (0, j, 0))
    (dh2,) = _mm("mlp_up_dx", (D // 512, nt, 1), [(du, a_ji(D_FF), wu, w_ji(1024), "nt_shards", None)],
                 behind(send("mlp", dict(w_down=dwd, w_up=dwu))), [(_sds((T, D), F32), out_ji)], store_f32, None, PP)
    dx2, dx2b, dg2 = _rmsnorm_bwd("rmsnorm_mlp_bwd", dh2, x2, r2, norm_mlp, dx3, True)

    (dwo,) = _mm("out_proj_dw", (D // 1024, D // 512, 1), [(merged, tall(512), dx2b, tall_j(1024), "tn", None)],
                 [], [(_sds((D, D), BF16), pl.BlockSpec((512, 1024), lambda j, i, k: (i, j)))], store_bf16, None, PP)

    def dmerge_epi(acc, ex, o, pids):
        ga, gb, pa_v, pb_v = (e[...].astype(F32) for e in ex)
        o[0][...] = (acc * ga).astype(BF16)
        o[1][...] = (acc * gb).astype(BF16)
        dga = acc * pa_v * ga * (1.0 - ga)
        dgb = acc * pb_v * gb * (1.0 - gb)
        o[2][0] = dga.astype(BF16)
        o[2][1] = dgb.astype(BF16)
        sa = jnp.sum(dga, axis=0, keepdims=True)
        sb = jnp.sum(dgb, axis=0, keepdims=True)
        first = pids[1] == 0

        @pl.when(first)
        def _():
            o[3][...] = sa
            o[4][...] = sb

        @pl.when(jnp.logical_not(first))
        def _():
            o[3][...] += sa
            o[4][...] += sb

    cj = pl.BlockSpec((TL, 512), lambda j, i, k: (i, j))
    bj = pl.BlockSpec((1, 512), lambda j, i, k: (0, j))
    dpa, dpb, dgate, dba, dbb = _mm(
        "out_proj_dx", (4, T // TL, 1),
        [(dx2b, pl.BlockSpec((TL, D), lambda j, i, k: (i, 0)), wo2, pl.BlockSpec((512, D), lambda j, i, k: (j, 0)), "nt", None)],
        [(gates, cj), (gates, pl.BlockSpec((TL, 512), lambda j, i, k: (i, 4 + j))), (pa, cj), (pb, cj)],
        [(_sds((T, D), BF16), cj)] * 2 + [(_sds((2, T, D), BF16), pl.BlockSpec((2, TL, 512), lambda j, i, k: (0, i, j)))]
        + [(_sds((1, D), F32), bj)] * 2, dmerge_epi, None, ("parallel", "arbitrary", "arbitrary"))

    def store_shards(acc, ex, o, pids):
        for s in range(4):
            o[0][s] = acc[:, s * 256:(s + 1) * 256].astype(BF16)

    def proj_dw(name, y, dp):
        (dw,) = _mm(name, (2, 1, 1), [(y, tall(512), dp, tall_j(1024), "tn", None)],
                    [], [(_sds((N_DEV, 512, 256), BF16), pl.BlockSpec((4, 512, 256), lambda j, i, k: (j, 0, 0)))],
                    store_shards, None, PP)
        return dw

    def proj_dx(name, dp, w):
        (dy,) = _mm(name, (nt, 1, 1),
                    [(dp, full_rows(TM), w, pl.BlockSpec((N_DEV, 512, 256), lambda i, j, k: (0, 0, 0)), "nt_shards", None)],
                    [], [(_sds((T, 512), BF16), full_rows(TM, 512))], store_bf16, None, PP)
        return dy

    def in_dw(name, dact, dact_spec, n):
        def store_pair(acc, ex, o, pids):
            o[0][0] = acc[:, :n].astype(BF16)
            o[0][1] = acc[:, n:].astype(BF16)

        (dw,) = _mm(name, (N_DEV // 2, D // 512, 1), [(h1, tall(512), dact, dact_spec, "tn", None)],
                    [], [(_sds((N_DEV, D, n), BF16), pl.BlockSpec((2, 512, n), lambda j, i, k: (j, i, 0)))],
                    store_pair, None, PP)
        return dw

    dwg = in_dw("gate_proj_dw", dgate, pl.BlockSpec((None, T, 1024), lambda j, i, k: (j // 2, 0, j % 2)), 512)
    dwpa = proj_dw("proj_a_dw", ya, dpa)
    dwpb = proj_dw("proj_b_dw", yb, dpb)
    token = send("proj", dict(w_gate=dwg, w_out=dwo.reshape(N_DEV, D // N_DEV, D), w_proj_a=dwpa, w_proj_b=dwpb))
    dya = proj_dx("proj_a_dx", dpa, wpa)
    dyb = proj_dx("proj_b_dx", dpb, wpb)

    dq_a, dk_a, dv_a = _dilated_bwd(qkv, last_src, _behind(slopes, token), dya, ya, lse_a)
    dq_b, dk_b, dv_b, dtiles = _attn_b_bwd(qkv, tiles, dyb, yb, lse_b)
    drpb = _rpb_reduce(dtiles.reshape(N_HEADS_B * N_DR, GRID_W * GRID_W), sel)[:, :N_DC]
    dqkv = jnp.concatenate([*dq_a, dq_b, *dk_a, dk_b, *dv_a, dv_b], axis=1)
    dwq = in_dw("qkv_proj_dw", dqkv, tall_j(1536), 768)
    half_a = lambda h: pl.BlockSpec((None, TM, D), lambda j, i, k: (h, i, 0))
    half_w = lambda h: pl.BlockSpec((4, 512, 512), lambda j, i, k: (h, j, 0))
    (dh1,) = _mm("in_proj_dx", (D // 512, nt, 1),
                 [(dqkv, a_ji(3 * D), wq, w_ji(768), "nt_shards", None),
                  (dgate, half_a(0), wg, half_w(0), "nt_shards", None), (dgate, half_a(1), wg, half_w(1), "nt_shards", None)],
                 behind(send("in", dict(w_qkv=dwq))), [(_sds((T, D), F32), out_ji)], store_f32, None, PP)
    grad_x, dg1 = _rmsnorm_bwd("rmsnorm_mix_bwd", dh1, x, r1, norm_mix, dx2, False)

    small = dict(norm_mix=dg1, b_gate=jnp.concatenate([dba, dbb], axis=1), rpb=drpb, norm_mlp=dg2, norm_final=dg3)
    return loss, grad_x, small


BIG = ("w_qkv", "w_gate", "w_proj_a", "w_proj_b", "w_out", "w_up", "w_down")
SMALL = ("norm_mix", "b_gate", "rpb", "norm_mlp", "norm_final")
SMALL_LANES = 96 * 128


def _place():
    return lax.axis_index("x"), lax.axis_index("y"), lax.axis_index("c")


def _peer(x, y, c, mask):
    return x ^ (mask >> 2), y ^ ((mask >> 1) & 1), c ^ (mask & 1)


_HBM = pl.BlockSpec(memory_space=pltpu.HBM)
_SEM = pl.BlockSpec(memory_space=pltpu.SEMAPHORE)
_ANY = pl.BlockSpec(memory_space=pl.ANY)
_EFFECT = pltpu.SideEffectType.DATAFLOW_SIDE_EFFECTING


def _hbm(a):
    return pltpu.with_memory_space_constraint(a, pltpu.HBM)


def _other_chips(x, y):
    return [(1 - x, y), (x, 1 - y), (1 - x, 1 - y)]


def _block(x, y, c):
    return 4 * x + 2 * y + c


def _rdma(src, dst, send_sem, recv_sem, to):
    return pltpu.make_async_remote_copy(src_ref=src, dst_ref=dst, send_sem=send_sem, recv_sem=recv_sem, device_id=to,
                                        device_id_type=MESH)


def _gather_start(name, shards):
    n = len(shards)

    def body(*refs):
        srcs, lands = refs[:n], refs[n:2 * n]
        outs = refs[2 * n:]
        send, recv_d2d, recv_ici, local = outs[:n], outs[n:2 * n], outs[2 * n:3 * n], outs[3 * n:4 * n]
        token_ref = outs[6 * n]
        x, y, c = _place()
        for a in range(n):
            mine = lands[a].at[_block(x, y, c)]
            _rdma(srcs[a], mine, send[a].at[0], recv_d2d[a].at[0], (x, y, 1 - c)).start()
            for j, chip in enumerate(_other_chips(x, y)):
                _rdma(srcs[a], mine, send[a].at[1 + j], recv_ici[a].at[j], (*chip, c)).start()
        for a in range(n):
            pltpu.make_async_copy(srcs[a], lands[a].at[_block(x, y, c)], local[a].at[0]).start()
        token_ref[...] = jnp.zeros_like(token_ref)

    lands = [_hbm(lax.empty((N_DEV,) + s.shape, s.dtype)) for s in shards]
    srcs = [_hbm(s) for s in shards]
    dma = pltpu.SemaphoreType.DMA
    res = pl.pallas_call(
        body, name=name,
        out_shape=[dma((4,))] * n + [dma((1,))] * n + [dma((3,))] * n + [dma((1,))] * n
        + [pltpu.HBM(s.shape, s.dtype) for s in srcs] + [pltpu.HBM(l.shape, l.dtype) for l in lands]
        + [jax.ShapeDtypeStruct((8, 128), F32)],
        in_specs=[_HBM] * (2 * n), out_specs=[_SEM] * (4 * n) + [_HBM] * (2 * n) + [pl.BlockSpec(memory_space=pltpu.VMEM)],
        input_output_aliases={i: 4 * n + i for i in range(2 * n)},
        compiler_params=pltpu.CompilerParams(has_side_effects=_EFFECT))(*srcs, *lands)
    keys = ("send", "recv_d2d", "recv_ici", "local", "src", "land")
    return [dict(zip(keys, (res[k * n + a] for k in range(6)))) for a in range(n)], res[6 * n]


def _gather_forward(name, handles, after):
    n = len(handles)
    after = after if isinstance(after, (tuple, list)) else (after,)

    def body(*refs):
        lands, recv_ici = refs[:n], refs[n:2 * n]
        outs = refs[2 * n + len(after):]
        fwd_send, fwd_recv, token_ref = outs[n:2 * n], outs[2 * n:3 * n], outs[3 * n]
        x, y, c = _place()
        for a in range(n):
            for j, chip in enumerate(_other_chips(x, y)):
                blk = lands[a].at[_block(*chip, c)]
                _rdma(blk, blk, fwd_send[a].at[j], recv_ici[a].at[j], (*chip, c)).wait_recv()
                _rdma(blk, blk, fwd_send[a].at[j], fwd_recv[a].at[j], (x, y, 1 - c)).start()
        token_ref[...] = jnp.zeros_like(token_ref)

    dma = pltpu.SemaphoreType.DMA
    lands = [h["land"] for h in handles]
    res = pl.pallas_call(
        body, name=name,
        out_shape=[pltpu.HBM(l.shape, l.dtype) for l in lands] + [dma((3,))] * (2 * n) + [jax.ShapeDtypeStruct((8, 128), F32)],
        in_specs=[_HBM] * n + [_SEM] * n + [_ANY] * len(after),
        out_specs=[_HBM] * n + [_SEM] * (2 * n) + [pl.BlockSpec(memory_space=pltpu.VMEM)],
        input_output_aliases={i: i for i in range(n)},
        compiler_params=pltpu.CompilerParams(has_side_effects=_EFFECT))(*lands, *[h["recv_ici"] for h in handles], *after)
    return [dict(h, land=res[a], fwd_send=res[n + a], fwd_recv=res[2 * n + a]) for a, h in enumerate(handles)], res[3 * n]


def _gather_wait(name, handles, after):
    n = len(handles)

    def body(*refs):
        srcs, lands = refs[:n], refs[n:2 * n]
        send, recv_d2d, local, fwd_send, fwd_recv = (refs[(2 + k) * n:(3 + k) * n] for k in range(5))
        x, y, c = _place()
        sibling = (x, y, 1 - c)
        for a in range(n):
            mine = lands[a].at[_block(x, y, c)]
            _rdma(srcs[a], mine, send[a].at[0], recv_d2d[a].at[0], sibling).wait_send()
            _rdma(srcs[a], lands[a].at[_block(*sibling)], send[a].at[0], recv_d2d[a].at[0], sibling).wait_recv()
            pltpu.make_async_copy(srcs[a], mine, local[a].at[0]).wait()
            for j, chip in enumerate(_other_chips(x, y)):
                _rdma(srcs[a], mine, send[a].at[1 + j], fwd_recv[a].at[j], (*chip, c)).wait_send()
                blk = lands[a].at[_block(*chip, c)]
                _rdma(blk, blk, fwd_send[a].at[j], fwd_recv[a].at[j], sibling).wait_send()
                got = lands[a].at[_block(*chip, 1 - c)]
                _rdma(got, got, fwd_send[a].at[j], fwd_recv[a].at[j], sibling).wait_recv()

    srcs = [h["src"] for h in handles]
    lands = [h["land"] for h in handles]
    sems = [h[k] for k in ("send", "recv_d2d", "local", "fwd_send", "fwd_recv") for h in handles]
    res = pl.pallas_call(
        body, name=name, out_shape=[pltpu.HBM(s.shape, s.dtype) for s in srcs] + [pltpu.HBM(l.shape, l.dtype) for l in lands],
        in_specs=[_HBM] * (2 * n) + [_SEM] * (5 * n) + [_ANY], out_specs=[_HBM] * (2 * n),
        input_output_aliases={i: i for i in range(2 * n)},
        compiler_params=pltpu.CompilerParams(has_side_effects=_EFFECT))(*srcs, *lands, *sems, after)
    return list(res[n:])


def _gather_small(small, after):
    def body(in_ref, *refs):
        out_ref, send_sems, recv_sems, local_sem = refs[len(after):]
        x, y, c = _place()
        me = 4 * x + 2 * y + c
        copies = [pltpu.make_async_copy(in_ref, out_ref.at[me], local_sem)]
        for mask in range(1, N_DEV):
            copies.append(pltpu.make_async_remote_copy(
                src_ref=in_ref, dst_ref=out_ref.at[me], send_sem=send_sems.at[mask - 1],
                recv_sem=recv_sems.at[mask - 1], device_id=_peer(x, y, c, mask), device_id_type=MESH))
        for cp in copies:
            cp.start()
        for cp in copies:
            cp.wait()

    any_spec = pl.BlockSpec(memory_space=pl.ANY)
    return pl.pallas_call(
        body, name="gather_small", in_specs=[any_spec] * (1 + len(after)), out_specs=any_spec,
        out_shape=jax.ShapeDtypeStruct((N_DEV,) + small.shape, small.dtype),
        scratch_shapes=[pltpu.SemaphoreType.DMA((7,)), pltpu.SemaphoreType.DMA((7,)), pltpu.SemaphoreType.DMA])(small, *after)


def _send_copy(src_refs, land_refs, send_sems, recv_sems, a, mask, scatter, x, y, c):
    px, py, pc = _peer(x, y, c, mask)
    src = src_refs[a].at[4 * px + 2 * py + pc] if scatter else src_refs[a]
    return pltpu.make_async_remote_copy(
        src_ref=src, dst_ref=land_refs[a].at[4 * x + 2 * y + c], send_sem=send_sems[a].at[mask - 1],
        recv_sem=recv_sems[a].at[mask - 1], device_id=(px, py, pc), device_id_type=MESH)


def _local_copy(src_refs, land_refs, local_sems, a, scatter, x, y, c):
    me = 4 * x + 2 * y + c
    return pltpu.make_async_copy(src_refs[a].at[me] if scatter else src_refs[a], land_refs[a].at[me], local_sems[a].at[0])


def _send_start(name, srcs, scatter, after=None):
    n = len(srcs)
    n_in = 2 * n + (after is not None)
    shapes = [s.shape[1:] if scatter else s.shape for s in srcs]

    def body(*refs):
        src_refs, land_refs = refs[:n], refs[n:2 * n]
        outs = refs[n_in:]
        send_sems, recv_sems, local_sems = outs[:n], outs[n:2 * n], outs[2 * n:3 * n]
        token_ref = outs[5 * n]
        x, y, c = _place()
        for a in range(n):
            for mask in range(1, N_DEV):
                _send_copy(src_refs, land_refs, send_sems, recv_sems, a, mask, scatter, x, y, c).start()
        for a in range(n):
            _local_copy(src_refs, land_refs, local_sems, a, scatter, x, y, c).start()
        token_ref[...] = jnp.zeros_like(token_ref)

    lands = [pltpu.with_memory_space_constraint(lax.empty((N_DEV,) + sh, s.dtype), pltpu.HBM) for sh, s in zip(shapes, srcs)]
    srcs = [pltpu.with_memory_space_constraint(s, pltpu.HBM) for s in srcs]
    res = pl.pallas_call(
        body, name=name,
        out_shape=[pltpu.SemaphoreType.DMA((7,))] * (2 * n) + [pltpu.SemaphoreType.DMA((1,))] * n
        + [pltpu.HBM(s.shape, s.dtype) for s in srcs] + [pltpu.HBM(l.shape, l.dtype) for l in lands]
        + [jax.ShapeDtypeStruct((8, 128), F32)],
        in_specs=[_HBM] * (2 * n) + [pl.BlockSpec(memory_space=pl.ANY)] * (n_in - 2 * n),
        out_specs=[_SEM] * (3 * n) + [_HBM] * (2 * n) + [pl.BlockSpec(memory_space=pltpu.VMEM)],
        input_output_aliases={i: 3 * n + i for i in range(2 * n)},
        compiler_params=pltpu.CompilerParams(has_side_effects=_EFFECT))(*srcs, *lands, *([] if after is None else [after]))
    handles = [tuple(res[k * n + a] for k in range(5)) for a in range(n)]
    return handles, res[5 * n]


def _send_wait(name, handles, scatter, after):
    n = len(handles)

    def body(*refs):
        src_refs, land_refs = refs[:n], refs[n:2 * n]
        send_sems, recv_sems, local_sems = refs[2 * n:3 * n], refs[3 * n:4 * n], refs[4 * n:5 * n]
        x, y, c = _place()
        for a in range(n):
            for mask in range(1, N_DEV):
                cp = _send_copy(src_refs, land_refs, send_sems, recv_sems, a, mask, scatter, x, y, c)
                cp.wait_send()
                cp.wait_recv()
            _local_copy(src_refs, land_refs, local_sems, a, scatter, x, y, c).wait()

    srcs = [h[3] for h in handles]
    lands = [h[4] for h in handles]
    res = pl.pallas_call(
        body, name=name, out_shape=[pltpu.HBM(s.shape, s.dtype) for s in srcs] + [pltpu.HBM(l.shape, l.dtype) for l in lands],
        in_specs=[_HBM] * (2 * n) + [_SEM] * (3 * n) + [pl.BlockSpec(memory_space=pl.ANY)], out_specs=[_HBM] * (2 * n),
        input_output_aliases={i: i for i in range(2 * n)},
        compiler_params=pltpu.CompilerParams(has_side_effects=_EFFECT))(
            *srcs, *lands, *[h[0] for h in handles], *[h[1] for h in handles], *[h[2] for h in handles], after)
    return list(res[n:])


def _adamw(name, parts, w, m, v):
    K, n = w.shape
    tr = min(K, 128)

    def body(p_ref, w_ref, m_ref, v_ref, g_ref, d_ref, nm_ref, nv_ref):
        g = p_ref[0].astype(F32)
        for i in range(1, N_DEV):
            g = g + p_ref[i].astype(F32)
        nm = ADAM_B1 * m_ref[...] + (1.0 - ADAM_B1) * g
        nv = ADAM_B2 * v_ref[...] + (1.0 - ADAM_B2) * jnp.square(g)
        m_hat = nm / (1.0 - ADAM_B1 ** ADAM_STEP)
        v_hat = nv / (1.0 - ADAM_B2 ** ADAM_STEP)
        g_ref[...] = g
        d_ref[...] = -ADAM_LR * (m_hat / (jnp.sqrt(v_hat) + ADAM_EPS) + ADAM_WD * w_ref[...])
        nm_ref[...] = nm
        nv_ref[...] = nv

    blk = pl.BlockSpec((tr, n), lambda i: (i, 0))
    return _pc(body, name=name, grid=(K // tr,), in_specs=[pl.BlockSpec((N_DEV, tr, n), lambda i: (0, i, 0)), blk, blk, blk],
               out_specs=[blk] * 4, out_shape=[jax.ShapeDtypeStruct((K, n), F32)] * 4, sem=("parallel",))(parts, w, m, v)


def _lane_padded(size):
    return -(-size // 128) * 128


def _pack_small(vals):
    rows = [jnp.pad(v.reshape(1, -1), ((0, 0), (0, _lane_padded(v.size) - v.size))) for v in vals]
    used = sum(r.shape[1] for r in rows)
    return jnp.concatenate(rows + [jnp.zeros((1, SMALL_LANES - used), F32)], axis=1)


def _unpack_small(packed, shapes):
    out, off = [], 0
    for s in shapes:
        size = int(np.prod(s))
        out.append(packed[0, off:off + size].reshape(s))
        off += _lane_padded(size)
    return out


def kernel(x, norm_mix, w_qkv, w_gate, b_gate, rpb, w_proj_a, w_proj_b, w_out, norm_mlp, w_up, w_down, norm_final, loss_target, m_norm_mix, m_w_qkv, m_w_gate, m_b_gate, m_rpb, m_w_proj_a, m_w_proj_b, m_w_out, m_norm_mlp, m_w_up, m_w_down, m_norm_final, v_norm_mix, v_w_qkv, v_w_gate, v_b_gate, v_rpb, v_w_proj_a, v_w_proj_b, v_w_out, v_norm_mlp, v_w_up, v_w_down, v_norm_final):
    w = dict(norm_mix=norm_mix, w_qkv=w_qkv, w_gate=w_gate, b_gate=b_gate, rpb=rpb, w_proj_a=w_proj_a, w_proj_b=w_proj_b,
             w_out=w_out, norm_mlp=norm_mlp, w_up=w_up, w_down=w_down, norm_final=norm_final)
    m = dict(norm_mix=m_norm_mix, w_qkv=m_w_qkv, w_gate=m_w_gate, b_gate=m_b_gate, rpb=m_rpb, w_proj_a=m_w_proj_a,
             w_proj_b=m_w_proj_b, w_out=m_w_out, norm_mlp=m_norm_mlp, w_up=m_w_up, w_down=m_w_down, norm_final=m_norm_final)
    v = dict(norm_mix=v_norm_mix, w_qkv=v_w_qkv, w_gate=v_w_gate, b_gate=v_b_gate, rpb=v_rpb, w_proj_a=v_w_proj_a,
             w_proj_b=v_w_proj_b, w_out=v_w_out, norm_mlp=v_norm_mlp, w_up=v_w_up, w_down=v_w_down, norm_final=v_norm_final)
    order = ("norm_mix", "w_qkv", "w_gate", "b_gate", "rpb", "w_proj_a", "w_proj_b", "w_out", "norm_mlp", "w_up",
             "w_down", "norm_final")

    stages = dict(qkv=("w_qkv",), gate=("w_gate", "w_proj_a", "w_proj_b", "w_out"), mlp=("w_up", "w_down"))
    first, token = _gather_start("gather_start_qkv", [w["w_qkv"][0].astype(BF16)])
    names = [n for s in ("gate", "mlp") for n in stages[s]]
    rest, token = _gather_start("gather_start", [_behind(w[n][0], token).astype(BF16) for n in names])
    handles = dict(zip(names, rest), w_qkv=first[0])

    def late_weights(stage, after):
        if stage == "qkv":
            forwarded, _ = _gather_forward("gather_forward_qkv", [handles["w_qkv"]], after)
            return tuple(_gather_wait("gather_wait_qkv", forwarded, after[0]))
        if stage == "attention":
            handles["gate"], tok = _gather_forward("gather_forward_gate", [handles[n] for n in stages["gate"]], after)
            return (tok,)
        if stage == "mixer":
            handles["up"], tok = _gather_forward("gather_forward_up", [handles["w_up"]], after)
            return tuple(_gather_wait("gather_wait_gate", handles["gate"], after)) + (tok,)
        if stage == "residual":
            handles["down"], tok = _gather_forward("gather_forward_down", [handles["w_down"]], after)
            return (tok,)
        return tuple(_gather_wait("gather_wait_" + stage, handles[stage], after))

    sent = {}

    def send(stage, grads):
        handles, tok = _send_start("grads_start_" + stage, list(grads.values()), True)
        sent[stage] = (tuple(grads), handles)
        return tok

    loss, grad_x, small = _local_step(x[0], loss_target[0], _behind(norm_mix, token), b_gate, rpb[0], norm_mlp, norm_final,
                                      late_weights, send)

    grad, delta, new_m, new_v = {}, {}, {}, {}
    after = grad_x
    for stage in ("mlp", "proj", "small", "in"):
        if stage == "small":
            all_small = after = _gather_small(_pack_small([small[n] for n in SMALL]), [d[0] for d in delta.values()])
            continue
        names, handles = sent[stage]
        for n, parts in zip(names, _send_wait("grads_wait_" + stage, handles, True, after)):
            res = _adamw("adamw_" + n, parts, w[n][0], m[n][0], v[n][0])
            grad[n], delta[n], new_m[n], new_v[n] = (r[None] for r in res)
            after = res[1]
    res = _adamw("adamw_small", all_small,
                 _pack_small([w[n] for n in SMALL]), _pack_small([m[n] for n in SMALL]), _pack_small([v[n] for n in SMALL]))
    shapes = [w[n].shape for n in SMALL]
    for tree, packed in zip((grad, delta, new_m, new_v), res):
        for n, val in zip(SMALL, _unpack_small(packed, shapes)):
            tree[n] = val

    total = lax.psum(loss[0, 0], ("x", "y", "c"))
    return (total, grad_x[None], *[grad[n] for n in order], *[delta[n] for n in order],
            *[new_m[n] for n in order], *[new_v[n] for n in order])
```

```python
import numpy as np
import jax
import jax.numpy as jnp
from jax import lax
from jax.experimental import pallas as pl
from jax.experimental.pallas import tpu as pltpu

F32 = jnp.float32
BF16 = jnp.bfloat16
MESH = pl.DeviceIdType.MESH

D_MODEL = 2048
HEAD_DIM = 128
N_HEADS = 16
N_HEADS_A = 12
N_HEADS_B = 4
DILATIONS = (1, 4, 16)
HALF_WINDOW = 64
GRID_W = 64
NA_ROWS = 8
NA_COLS = 16
D_FF = 4 * D_MODEL
N_DEV = 8
EPS = 1e-6
NEG = -1e30
SCALE = HEAD_DIM ** -0.5
ADAM_LR, ADAM_B1, ADAM_B2, ADAM_EPS, ADAM_WD, ADAM_STEP = 0.001, 0.9, 0.999, 1e-08, 0.01, 10

VMEM_LIMIT = 56 * 1024 * 1024
TM = 512
TL = 1024
TS = 256
TE = 512
BQ_A = 128


def _pc(body, *, name, grid, in_specs, out_specs, out_shape, scratch=(), sem=None):
    return pl.pallas_call(
        body, name=name, grid=grid, in_specs=in_specs, out_specs=out_specs, out_shape=out_shape,
        scratch_shapes=list(scratch),
        compiler_params=pltpu.CompilerParams(dimension_semantics=sem, vmem_limit_bytes=VMEM_LIMIT))


_DOT_DIMS = {"nn": (((1,), (0,)), ((), ())), "nt": (((1,), (1,)), ((), ())), "tn": (((0,), (0,)), ((), ()))}


def _dot(a, b, mode, precision=None):
    if mode == "nn_shards":
        return jnp.concatenate([_dot(a, b[s], "nn") for s in range(b.shape[0])], axis=1)
    if mode == "nt_shards":
        n = b.shape[2]
        out = _dot(a[:, :n], b[0], "nt")
        for s in range(1, b.shape[0]):
            out = out + _dot(a[:, s * n:(s + 1) * n], b[s], "nt")
        return out
    return lax.dot_general(a, b, _DOT_DIMS[mode], preferred_element_type=F32, precision=precision)


def _mm(name, grid, pairs, extras, outs, epilogue, acc_shape, sem):
    n_pairs, n_extra, n_out = len(pairs), len(extras), len(outs)
    k_axis = len(grid) - 1
    nk = grid[k_axis]

    def body(*refs):
        pids = [pl.program_id(ax) for ax in range(len(grid))]
        ab = refs[:2 * n_pairs]
        ex = refs[2 * n_pairs:2 * n_pairs + n_extra]
        o = refs[2 * n_pairs + n_extra:2 * n_pairs + n_extra + n_out]
        part = None
        for p, pair in enumerate(pairs):
            a = ab[2 * p][...]
            if pair[5] is not None:
                a = pair[5](a)
            d = _dot(a, ab[2 * p + 1][...], pair[4])
            part = d if part is None else part + d
        if nk == 1:
            epilogue(part, ex, o, pids)
        else:
            acc = refs[-1]
            k = pids[k_axis]

            @pl.when(k == 0)
            def _():
                acc[...] = part

            @pl.when(k > 0)
            def _():
                acc[...] += part

            @pl.when(k == nk - 1)
            def _():
                epilogue(acc[...], ex, o, pids)

    operands, in_specs = [], []
    for a, a_spec, b, b_spec, _, _ in pairs:
        operands += [a, b]
        in_specs += [a_spec, b_spec]
    for e, e_spec in extras:
        operands.append(e)
        in_specs.append(e_spec)
    res = _pc(body, name=name, grid=grid, in_specs=in_specs, out_specs=[s for _, s in outs],
              out_shape=[o for o, _ in outs], scratch=[pltpu.VMEM(acc_shape, F32)] if nk > 1 else [], sem=sem)(*operands)
    return res


def _rmsnorm_fwd(x, g, name):
    T, D = x.shape

    def body(x_ref, g_ref, h_ref, r_ref):
        xv = x_ref[...]
        r = lax.rsqrt(jnp.mean(xv * xv, axis=-1, keepdims=True) + EPS)
        h_ref[...] = (xv * r * g_ref[...]).astype(BF16)
        r_ref[...] = r

    return _pc(body, name=name, grid=(T // TM,),
               in_specs=[pl.BlockSpec((TM, D), lambda i: (i, 0)), pl.BlockSpec((1, D), lambda i: (0, 0))],
               out_specs=[pl.BlockSpec((TM, D), lambda i: (i, 0)), pl.BlockSpec((TM, 1), lambda i: (i, 0))],
               out_shape=[jax.ShapeDtypeStruct((T, D), BF16), jax.ShapeDtypeStruct((T, 1), F32)],
               sem=("parallel",))(x, g)


def _rmsnorm_bwd_block(dh, x, r, g, dres):
    xh = x * r
    dxh = dh * g
    dx = dres + r * (dxh - xh * jnp.mean(dxh * xh, axis=-1, keepdims=True))
    return dx, jnp.sum(dh * xh, axis=0, keepdims=True)


def _rmsnorm_bwd(name, dh, x, r, g, dres, want_bf16):
    T, D = x.shape

    def body(dh_ref, x_ref, r_ref, g_ref, dres_ref, *outs):
        dx, dgp = _rmsnorm_bwd_block(dh_ref[...], x_ref[...], r_ref[...], g_ref[...], dres_ref[...])
        outs[0][...] = dx
        if want_bf16:
            outs[1][...] = dx.astype(BF16)
        first = pl.program_id(0) == 0

        @pl.when(first)
        def _():
            outs[-1][...] = dgp

        @pl.when(jnp.logical_not(first))
        def _():
            outs[-1][...] += dgp

    row = pl.BlockSpec((TE, D), lambda i: (i, 0))
    vec = pl.BlockSpec((1, D), lambda i: (0, 0))
    n_row = 2 if want_bf16 else 1
    return _pc(body, name=name, grid=(T // TE,), in_specs=[row, row, pl.BlockSpec((TE, 1), lambda i: (i, 0)), vec, row],
               out_specs=[row] * n_row + [vec],
               out_shape=[jax.ShapeDtypeStruct((T, D), F32)] + [jax.ShapeDtypeStruct((T, D), BF16)] * (n_row - 1)
               + [jax.ShapeDtypeStruct((1, D), F32)], sem=("arbitrary",))(dh, x, r, g, dres)


def _final_norm_loss(x3, g, tgt):
    T, D = x3.shape

    def body(x_ref, g_ref, t_ref, loss_ref, dx_ref, dxb_ref, dg_ref):
        i = pl.program_id(0)
        xv = x_ref[...]
        gv = g_ref[...]
        r = lax.rsqrt(jnp.mean(xv * xv, axis=-1, keepdims=True) + EPS)
        xh = xv * r
        e = xh * gv - t_ref[...]
        part = 0.5 * jnp.sum(jnp.mean(e * e, axis=-1, keepdims=True))
        dy = e * (1.0 / D)
        dxh = dy * gv
        dx = r * (dxh - xh * jnp.mean(dxh * xh, axis=-1, keepdims=True))
        dx_ref[...] = dx
        dxb_ref[...] = dx.astype(BF16)
        dgp = jnp.sum(dy * xh, axis=0, keepdims=True)

        @pl.when(i == 0)
        def _():
            loss_ref[...] = jnp.zeros(loss_ref.shape, F32) + part
            dg_ref[...] = dgp

        @pl.when(i > 0)
        def _():
            loss_ref[...] += part
            dg_ref[...] += dgp

    row = pl.BlockSpec((TE, D), lambda i: (i, 0))
    vec = pl.BlockSpec((1, D), lambda i: (0, 0))
    return _pc(body, name="final_norm_loss", grid=(T // TE,), in_specs=[row, vec, row],
               out_specs=[pl.BlockSpec((8, 128), lambda i: (0, 0)), row, row, vec],
               out_shape=[jax.ShapeDtypeStruct((8, 128), F32), jax.ShapeDtypeStruct((T, D), F32),
                          jax.ShapeDtypeStruct((T, D), BF16), jax.ShapeDtypeStruct((1, D), F32)],
               sem=("arbitrary",))(x3, g, tgt)


def _alibi_slopes():
    return np.asarray(2.0 ** (-8.0 * np.arange(1, N_HEADS_A + 1) / N_HEADS_A), dtype=np.float32)


def _col_to_row(col):
    return jnp.transpose(jnp.broadcast_to(col, (BQ_A, BQ_A)))[0:1, :]


def _row_to_col(row):
    return jnp.transpose(jnp.broadcast_to(row, (BQ_A, BQ_A)))[:, 0:1]


def _row_spec(rows_per_step, first=0):
    return pl.BlockSpec((None, rows_per_step // BQ_A, BQ_A), lambda j, i: (first + j, i, 0))


def _a_windows(T):
    half = [HALF_WINDOW * d for d in DILATIONS]
    return half, [min(BQ_A + 2 * w, T) for w in half]


def _a_bias(slope, g, offset, wk, half):
    d = DILATIONS[g]
    dlt = offset + lax.broadcasted_iota(jnp.int32, (BQ_A, wk), 1) - lax.broadcasted_iota(jnp.int32, (BQ_A, wk), 0)
    ad = jnp.maximum(dlt, -dlt)
    ok = (ad <= half) & ((ad & (d - 1)) == 0)
    return jnp.where(ok, -slope * ad.astype(F32), NEG)


def _a_scores(q, kw, slope, g, offset, wk, half):
    return _dot(q, kw, "nt") * SCALE + _a_bias(slope, g, offset, wk, half)


N_JOINT = 2


SUB_J = 8
BS_J = SUB_J * BQ_A


def _a_qkv_specs(T):
    specs = []
    for g in range(N_JOINT):
        specs.append(pl.BlockSpec((BS_J, HEAD_DIM), lambda j, i, g=g: (i, 4 * g + j)))
        specs.append(pl.BlockSpec((T, HEAD_DIM), lambda j, i, g=g: (0, N_HEADS + 4 * g + j)))
        specs.append(pl.BlockSpec((T, HEAD_DIM), lambda j, i, g=g: (0, 2 * N_HEADS + 4 * g + j)))
    return specs


def _a_window(step, u, g, T, half, wks):
    t0 = (step * SUB_J + u) * BQ_A
    ws = pl.multiple_of(jnp.clip(t0 - half[g], 0, T - wks[g]), 64)
    return pl.ds(ws, wks[g]), ws - t0


def _attn_a_fwd(qkv, slopes, o_last, lse_last):
    T = qkv.shape[0]
    half, wks = _a_windows(T)

    def body(sl_ref, *refs):
        qkv_refs, (o_ref, l_ref, y_ref, lse_ref) = refs[:3 * N_JOINT], refs[3 * N_JOINT:]
        j = pl.program_id(0)
        for u in range(SUB_J):
            rows = slice(u * BQ_A, (u + 1) * BQ_A)
            ss, vws = [], []
            lse_last = _row_to_col(l_ref[u:u + 1, :])
            m = lse_last
            for g in range(N_JOINT):
                q_ref, k_ref, v_ref = qkv_refs[3 * g:3 * g + 3]
                win, offset = _a_window(pl.program_id(1), u, g, T, half, wks)
                ss.append(_a_scores(q_ref[rows, :], k_ref[win, :], sl_ref[4 * g + j], g, offset, wks[g], half[g]))
                vws.append(v_ref[win, :])
                m = jnp.maximum(m, jnp.max(ss[g], axis=-1, keepdims=True))
            den = jnp.exp(lse_last - m)
            acc = den * o_ref[rows, :].astype(F32)
            for g in range(N_JOINT):
                p = jnp.exp(ss[g] - m)
                den = den + jnp.sum(p, axis=-1, keepdims=True)
                acc = acc + _dot(p.astype(BF16), vws[g], "nn")
            y_ref[rows, :] = (acc / den).astype(BF16)
            lse_ref[u:u + 1, :] = _col_to_row(m + jnp.log(den))

    blk = pl.BlockSpec((BS_J, HEAD_DIM), lambda j, i: (i, j))
    col = _row_spec(BS_J)
    return _pc(body, name="attn_a_fwd", grid=(4, T // BS_J),
               in_specs=[pl.BlockSpec(memory_space=pltpu.SMEM)] + _a_qkv_specs(T) + [blk, col], out_specs=[blk, col],
               out_shape=[jax.ShapeDtypeStruct((T, 4 * HEAD_DIM), BF16), jax.ShapeDtypeStruct((4, T // BQ_A, BQ_A), F32)],
               sem=("parallel", "parallel"))(slopes, *([qkv] * (3 * N_JOINT)), o_last, lse_last)


def _attn_a_bwd(qkv, slopes, dy, y, lse):
    T = qkv.shape[0]
    steps = T // BS_J
    half, wks = _a_windows(T)

    def body(sl_ref, *refs):
        qkv_refs = refs[:3 * N_JOINT]
        dy_ref, y_ref, lse_ref, dq_ref, dk_ref, dv_ref, delta_ref, dk_acc, dv_acc = refs[3 * N_JOINT:]
        j = pl.program_id(0)
        step = pl.program_id(1)

        @pl.when(step == 0)
        def _():
            dk_acc[...] = jnp.zeros_like(dk_acc)
            dv_acc[...] = jnp.zeros_like(dv_acc)

        for u in range(SUB_J):
            rows = slice(u * BQ_A, (u + 1) * BQ_A)
            dyv = dy_ref[rows, :]
            delta = jnp.sum(dyv.astype(F32) * y_ref[rows, :].astype(F32), axis=-1, keepdims=True)
            delta_ref[u:u + 1, :] = _col_to_row(delta)
            lse_v = _row_to_col(lse_ref[u:u + 1, :])
            for g in range(N_JOINT):
                q_ref, k_ref, v_ref = qkv_refs[3 * g:3 * g + 3]
                win, offset = _a_window(step, u, g, T, half, wks)
                qv = q_ref[rows, :]
                kw = k_ref[win, :]
                s = _a_scores(qv, kw, sl_ref[4 * g + j], g, offset, wks[g], half[g])
                p = jnp.exp(s - lse_v)
                dp = _dot(dyv, v_ref[win, :], "nt")
                ds = (p * (dp - delta)).astype(BF16)
                dq_ref[g, rows, :] = (_dot(ds, kw, "nn") * SCALE).astype(BF16)
                dk_acc[g, win, :] += _dot(ds, qv, "tn") * SCALE
                dv_acc[g, win, :] += _dot(p.astype(BF16), dyv, "tn")

        @pl.when(step == steps - 1)
        def _():
            dk_ref[...] = dk_acc[...].astype(BF16)
            dv_ref[...] = dv_acc[...].astype(BF16)

    blk = pl.BlockSpec((BS_J, HEAD_DIM), lambda j, i: (i, j))
    col = _row_spec(BS_J)
    whole = pl.BlockSpec((N_JOINT, T, HEAD_DIM), lambda j, i: (0, 0, j))
    full = jax.ShapeDtypeStruct((N_JOINT, T, 4 * HEAD_DIM), BF16)
    return _pc(body, name="attn_a_bwd", grid=(4, steps),
               in_specs=[pl.BlockSpec(memory_space=pltpu.SMEM)] + _a_qkv_specs(T) + [blk, blk, col],
               out_specs=[pl.BlockSpec((N_JOINT, BS_J, HEAD_DIM), lambda j, i: (0, i, j)), whole, whole, col],
               out_shape=[full, full, full, jax.ShapeDtypeStruct((4, T // BQ_A, BQ_A), F32)],
               scratch=[pltpu.VMEM((N_JOINT, T, HEAD_DIM), F32), pltpu.VMEM((N_JOINT, T, HEAD_DIM), F32)],
               sem=("parallel", "arbitrary"))(slopes, *([qkv] * (3 * N_JOINT)), dy, y, lse)


SUB_A = 8


def _to_residue_major(a, d, axis=0):
    if d == 1:
        return a
    sh = a.shape
    a = a.reshape(sh[:axis] + (sh[axis] // d, d) + sh[axis + 1:])
    return jnp.swapaxes(a, axis, axis + 1).reshape(sh)


def _to_token_order(a, d, axis=0):
    if d == 1:
        return a
    sh = a.shape
    a = a.reshape(sh[:axis] + (d, sh[axis] // d) + sh[axis + 1:])
    return jnp.swapaxes(a, axis, axis + 1).reshape(sh)


def _band_bias(slope, offset, wk):
    dlt = offset + lax.broadcasted_iota(jnp.int32, (BQ_A, wk), 1) - lax.broadcasted_iota(jnp.int32, (BQ_A, wk), 0)
    ad = jnp.maximum(dlt, -dlt)
    return jnp.where(ad <= HALF_WINDOW, -slope * ad.astype(F32), NEG)


def _band_window(step, u, seg, wk):
    t0 = (step * SUB_A + u) * BQ_A
    lo = (t0 // seg) * seg
    ws = pl.multiple_of(jnp.clip(t0 - HALF_WINDOW, lo, lo + seg - wk), 64)
    return slice(u * BQ_A, (u + 1) * BQ_A), pl.ds(ws, wk), ws - t0


def _band_specs(T, cols):
    bs = SUB_A * BQ_A
    assert T % bs == 0 and T // max(DILATIONS) >= BQ_A, "a query sub-block must lie inside one segment"
    return [pl.BlockSpec(memory_space=pltpu.SMEM),
            pl.BlockSpec((bs, HEAD_DIM), lambda j, i: (i, cols[0] + j)),
            pl.BlockSpec((T, HEAD_DIM), lambda j, i: (0, cols[1] + j)),
            pl.BlockSpec((T, HEAD_DIM), lambda j, i: (0, cols[2] + j))]


def _banded_fwd(name, src, cols, seg, slopes):
    T = src.shape[0]
    bs = SUB_A * BQ_A
    wk = min(BQ_A + 2 * HALF_WINDOW, seg)

    def body(sl_ref, q_ref, k_ref, v_ref, o_ref, lse_ref):
        slope = sl_ref[pl.program_id(0)]
        for u in range(SUB_A):
            rows, win, offset = _band_window(pl.program_id(1), u, seg, wk)
            s = _dot(q_ref[rows, :], k_ref[win, :], "nt") * SCALE + _band_bias(slope, offset, wk)
            m = jnp.max(s, axis=-1, keepdims=True)
            p = jnp.exp(s - m)
            den = jnp.sum(p, axis=-1, keepdims=True)
            o_ref[rows, :] = (_dot(p.astype(BF16), v_ref[win, :], "nn") / den).astype(BF16)
            lse_ref[u:u + 1, :] = _col_to_row(m + jnp.log(den))

    return _pc(body, name=name, grid=(4, T // bs), in_specs=_band_specs(T, cols),
               out_specs=[pl.BlockSpec((bs, HEAD_DIM), lambda j, i: (i, j)), _row_spec(bs)],
               out_shape=[jax.ShapeDtypeStruct((T, 4 * HEAD_DIM), BF16), jax.ShapeDtypeStruct((4, T // BQ_A, BQ_A), F32)],
               sem=("parallel", "parallel"))(slopes, src, src, src)


def _banded_bwd(name, src, cols, seg, slopes, dy, delta_lse):
    T = src.shape[0]
    bs = SUB_A * BQ_A
    steps = T // bs
    wk = min(BQ_A + 2 * HALF_WINDOW, seg)

    def body(sl_ref, q_ref, k_ref, v_ref, dy_ref, delta_ref, lse_ref, dq_ref, dk_ref, dv_ref, dk_acc, dv_acc):
        slope = sl_ref[pl.program_id(0)]
        step = pl.program_id(1)

        @pl.when(step == 0)
        def _():
            dk_acc[...] = jnp.zeros_like(dk_acc)
            dv_acc[...] = jnp.zeros_like(dv_acc)

        for u in range(SUB_A):
            rows, win, offset = _band_window(step, u, seg, wk)
            qv, kw, dyv = q_ref[rows, :], k_ref[win, :], dy_ref[rows, :]
            s = _dot(qv, kw, "nt") * SCALE + _band_bias(slope, offset, wk)
            p = jnp.exp(s - _row_to_col(lse_ref[u:u + 1, :]))
            ds = (p * (_dot(dyv, v_ref[win, :], "nt") - _row_to_col(delta_ref[u:u + 1, :]))).astype(BF16)
            dq_ref[rows, :] = (_dot(ds, kw, "nn") * SCALE).astype(BF16)
            dk_acc[win, :] += _dot(ds, qv, "tn") * SCALE
            dv_acc[win, :] += _dot(p.astype(BF16), dyv, "tn")

        @pl.when(step == steps - 1)
        def _():
            dk_ref[...] = dk_acc[...].astype(BF16)
            dv_ref[...] = dv_acc[...].astype(BF16)

    blk = pl.BlockSpec((bs, HEAD_DIM), lambda j, i: (i, j))
    whole = pl.BlockSpec((T, HEAD_DIM), lambda j, i: (0, j))
    full = jax.ShapeDtypeStruct((T, 4 * HEAD_DIM), BF16)
    return _pc(body, name=name, grid=(4, steps),
               in_specs=_band_specs(T, cols) + [blk, _row_spec(bs), _row_spec(bs, 4)],
               out_specs=[blk, whole, whole], out_shape=[full, full, full],
               scratch=[pltpu.VMEM((T, HEAD_DIM), F32), pltpu.VMEM((T, HEAD_DIM), F32)],
               sem=("parallel", "arbitrary"))(slopes, src, src, src, dy, delta_lse, delta_lse)


def _last_group_source(qkv):
    third = qkv.shape[1] // 3
    own = jnp.concatenate([qkv[:, t * third + 512 * N_JOINT:t * third + 512 * (N_JOINT + 1)] for t in range(3)], axis=1)
    return _to_residue_major(own, DILATIONS[N_JOINT])


def _dilated_fwd(qkv, slopes):
    d = DILATIONS[N_JOINT]
    last_src = _last_group_source(qkv)
    o, lse = _banded_fwd("attn_a_last_fwd", last_src, (0, 4, 8), qkv.shape[0] // d, slopes[4 * N_JOINT:] * float(d))
    lse = _to_token_order(lse.reshape(4, -1), d, axis=1).reshape(lse.shape)
    y, joint_lse = _attn_a_fwd(qkv, slopes, _to_token_order(o, d), lse)
    return y, joint_lse, last_src


def _dilated_bwd(qkv, last_src, slopes, dy, y, lse):
    d = DILATIONS[N_JOINT]
    dq, dk, dv, delta = _attn_a_bwd(qkv, slopes, dy, y, lse)
    delta_lse = jnp.concatenate([delta, lse], axis=0)
    delta_lse = _to_residue_major(delta_lse.reshape(8, -1), d, axis=1).reshape(delta_lse.shape)
    last = _banded_bwd("attn_a_last_bwd", last_src, (0, 4, 8), qkv.shape[0] // d,
                       slopes[4 * N_JOINT:] * float(d), _to_residue_major(dy, d), delta_lse)
    return [[joint[g] for g in range(N_JOINT)] + [_to_token_order(own, d)] for joint, own in zip((dq, dk, dv), last)]


N_DR = 2 * NA_ROWS - 1
N_DC = 2 * NA_COLS - 1
WK_B = NA_ROWS * GRID_W


def _toeplitz_consts():
    col = np.arange(GRID_W)
    start = np.clip(col - NA_COLS // 2, 0, GRID_W - NA_COLS)
    ok = (col[None, :] >= start[:, None]) & (col[None, :] < start[:, None] + NA_COLS)
    dc = np.clip(col[None, :] - col[:, None], -(NA_COLS - 1), NA_COLS - 1) + NA_COLS - 1
    sel = np.zeros((128, GRID_W * GRID_W), np.float32)
    sel[dc.reshape(-1), np.arange(GRID_W * GRID_W)] = ok.reshape(-1).astype(np.float32)
    mask = np.where(ok, 0.0, NEG).astype(np.float32).reshape(1, -1)
    return sel, mask


def _rpb_expand(rpb_rows, sel, mask):
    def body(r_ref, s_ref, m_ref, o_ref):
        o_ref[...] = _dot(r_ref[...], s_ref[...], "nn", precision=lax.Precision.HIGHEST) + m_ref[...]

    n = rpb_rows.shape[0]
    return _pc(body, name="rpb_expand", grid=(1,),
               in_specs=[pl.BlockSpec((n, 128), lambda i: (0, 0)), pl.BlockSpec(sel.shape, lambda i: (0, 0)),
                         pl.BlockSpec(mask.shape, lambda i: (0, 0))],
               out_specs=pl.BlockSpec((n, sel.shape[1]), lambda i: (0, 0)),
               out_shape=jax.ShapeDtypeStruct((n, sel.shape[1]), F32), sem=("arbitrary",))(rpb_rows, sel, mask)


def _rpb_reduce(dtiles, sel):
    def body(t_ref, s_ref, o_ref):
        o_ref[...] = _dot(t_ref[...], s_ref[...], "nt", precision=lax.Precision.HIGHEST)

    n = dtiles.shape[0]
    return _pc(body, name="rpb_reduce", grid=(1,),
               in_specs=[pl.BlockSpec(dtiles.shape, lambda i: (0, 0)), pl.BlockSpec(sel.shape, lambda i: (0, 0))],
               out_specs=pl.BlockSpec((n, 128), lambda i: (0, 0)),
               out_shape=jax.ShapeDtypeStruct((n, 128), F32), sem=("arbitrary",))(dtiles, sel)


ROWS_B = 16
BQ_B = ROWS_B * GRID_W


def _b_specs(T):
    return [pl.BlockSpec((BQ_B, HEAD_DIM), lambda h, r: (r, N_HEADS_A + h)),
            pl.BlockSpec((T, HEAD_DIM), lambda h, r: (0, N_HEADS + N_HEADS_A + h)),
            pl.BlockSpec((T, HEAD_DIM), lambda h, r: (0, 2 * N_HEADS + N_HEADS_A + h)),
            pl.BlockSpec((None, N_DR, GRID_W, GRID_W), lambda h, r: (h, 0, 0, 0))]


def _b_row(step, u, rows, t_ref, bias_ref):
    r = step * ROWS_B + u
    rs = jnp.clip(r - NA_ROWS // 2, 0, rows - NA_ROWS)
    first_tile = rs - r + NA_ROWS - 1
    for a in range(NA_ROWS):
        bias_ref[u, :, a * GRID_W:(a + 1) * GRID_W] = t_ref[first_tile + a]
    return slice(u * GRID_W, (u + 1) * GRID_W), pl.ds(pl.multiple_of(rs * GRID_W, GRID_W), WK_B), first_tile


def _attn_b_fwd(qkv, tiles):
    T = qkv.shape[0]
    rows = T // GRID_W

    def body(q_ref, k_ref, v_ref, t_ref, y_ref, lse_ref, bias_ref):
        for u in range(ROWS_B):
            blk, win, _ = _b_row(pl.program_id(1), u, rows, t_ref, bias_ref)
            s = _dot(q_ref[blk, :], k_ref[win, :], "nt") * SCALE + bias_ref[u]
            m = jnp.max(s, axis=-1, keepdims=True)
            p = jnp.exp(s - m)
            den = jnp.sum(p, axis=-1, keepdims=True)
            y_ref[blk, :] = (_dot(p.astype(BF16), v_ref[win, :], "nn") / den).astype(BF16)
            lse_ref[blk, :] = m + jnp.log(den)

    return _pc(body, name="attn_b_fwd", grid=(N_HEADS_B, rows // ROWS_B), in_specs=_b_specs(T),
               out_specs=[pl.BlockSpec((BQ_B, HEAD_DIM), lambda h, r: (r, h)),
                          pl.BlockSpec((None, BQ_B, 1), lambda h, r: (h, r, 0))],
               out_shape=[jax.ShapeDtypeStruct((T, N_HEADS_B * HEAD_DIM), BF16),
                          jax.ShapeDtypeStruct((N_HEADS_B, T, 1), F32)],
               scratch=[pltpu.VMEM((ROWS_B, GRID_W, WK_B), F32)], sem=("parallel", "parallel"))(qkv, qkv, qkv, tiles)


def _attn_b_bwd(qkv, tiles, dy, y, lse):
    T = qkv.shape[0]
    rows = T // GRID_W
    steps = rows // ROWS_B

    def body(q_ref, k_ref, v_ref, t_ref, dy_ref, y_ref, lse_ref, dq_ref, dk_ref, dv_ref, dt_ref,
             bias_ref, ds_ref, dk_acc, dv_acc):
        step = pl.program_id(1)

        @pl.when(step == 0)
        def _():
            dk_acc[...] = jnp.zeros_like(dk_acc)
            dv_acc[...] = jnp.zeros_like(dv_acc)
            dt_ref[...] = jnp.zeros_like(dt_ref)

        for u in range(ROWS_B):
            blk, win, first_tile = _b_row(step, u, rows, t_ref, bias_ref)
            qv = q_ref[blk, :]
            kw = k_ref[win, :]
            dyv = dy_ref[blk, :]
            s = _dot(qv, kw, "nt") * SCALE + bias_ref[u]
            p = jnp.exp(s - lse_ref[blk, :])
            delta = jnp.sum(dyv.astype(F32) * y_ref[blk, :].astype(F32), axis=-1, keepdims=True)
            ds = p * (_dot(dyv, v_ref[win, :], "nt") - delta)
            ds_ref[u] = ds
            for a in range(NA_ROWS):
                dt_ref[first_tile + a] += ds_ref[u, :, a * GRID_W:(a + 1) * GRID_W]
            dsb = ds.astype(BF16)
            dq_ref[blk, :] = (_dot(dsb, kw, "nn") * SCALE).astype(BF16)
            dk_acc[win, :] += _dot(dsb, qv, "tn") * SCALE
            dv_acc[win, :] += _dot(p.astype(BF16), dyv, "tn")

        @pl.when(step == steps - 1)
        def _():
            dk_ref[...] = dk_acc[...].astype(BF16)
            dv_ref[...] = dv_acc[...].astype(BF16)

    blk_spec = pl.BlockSpec((BQ_B, HEAD_DIM), lambda h, r: (r, h))
    whole = pl.BlockSpec((T, HEAD_DIM), lambda h, r: (0, h))
    full = jax.ShapeDtypeStruct((T, N_HEADS_B * HEAD_DIM), BF16)
    return _pc(body, name="attn_b_bwd", grid=(N_HEADS_B, steps),
               in_specs=_b_specs(T) + [blk_spec, blk_spec, pl.BlockSpec((None, BQ_B, 1), lambda h, r: (h, r, 0))],
               out_specs=[blk_spec, whole, whole, pl.BlockSpec((None, N_DR, GRID_W, GRID_W), lambda h, r: (h, 0, 0, 0))],
               out_shape=[full, full, full, jax.ShapeDtypeStruct((N_HEADS_B, N_DR, GRID_W, GRID_W), F32)],
               scratch=[pltpu.VMEM((ROWS_B, GRID_W, WK_B), F32), pltpu.VMEM((ROWS_B, GRID_W, WK_B), F32),
                        pltpu.VMEM((T, HEAD_DIM), F32), pltpu.VMEM((T, HEAD_DIM), F32)],
               sem=("parallel", "arbitrary"))(qkv, qkv, qkv, tiles, dy, y, lse)


def _sds(shape, dtype):
    return jax.ShapeDtypeStruct(shape, dtype)


def _behind(value, token):
    return value if token is None else value + token[0, 0]


def _local_step(x, tgt, norm_mix, b_gate, rpb, norm_mlp, norm_final, late_weights, send):
    T, D = x.shape
    nt = T // TM
    PP, PA = ("parallel", "parallel", "arbitrary"), ("arbitrary", "arbitrary", "arbitrary")
    slopes = jnp.asarray(_alibi_slopes())
    sel_np, mask_np = _toeplitz_consts()
    sel, mask = jnp.asarray(sel_np), jnp.asarray(mask_np)

    def rows(w, tm=TM):
        return pl.BlockSpec((tm, w), lambda i, j, k: (i, j))

    def full_rows(tm, w=D):
        return pl.BlockSpec((tm, w), lambda i, j, k: (i, 0))

    def tall(w):
        return pl.BlockSpec((T, w), lambda j, i, k: (0, i))

    def tall_j(w):
        return pl.BlockSpec((T, w), lambda j, i, k: (0, j))

    rvec = pl.BlockSpec((TS, 1), lambda i, j, k: (i, 0))
    gvec = pl.BlockSpec((1, D), lambda i, j, k: (0, 0))

    h1, r1 = _rmsnorm_fwd(x, norm_mix, "rmsnorm_mix")
    rpb_rows = jnp.pad(rpb.reshape(N_HEADS_B * N_DR, N_DC), ((0, 0), (0, 128 - N_DC)))
    tiles = _rpb_expand(rpb_rows, sel, mask).reshape(N_HEADS_B, N_DR, GRID_W, GRID_W)
    (wq,) = late_weights("qkv", (h1, tiles))

    def store_bf16(acc, ex, o, pids):
        o[0][...] = acc.astype(BF16)

    (qkv,) = _mm("qkv_proj", (T // TL, N_DEV // 2, 1),
                 [(h1, full_rows(TL), wq, pl.BlockSpec((2, D, 768), lambda i, j, k: (j, 0, 0)), "nn_shards", None)],
                 [], [(_sds((T, 3 * D), BF16), rows(1536, TL))], store_bf16, None, PP)

    def gate_epi(acc, ex, o, pids):
        o[0][...] = jax.nn.sigmoid(acc + ex[0][...]).astype(BF16)

    late_weights("attention", qkv)
    ya, lse_a, last_src = _dilated_fwd(qkv, slopes)
    yb, lse_b = _attn_b_fwd(qkv, tiles)

    wg, wpa, wpb, wo, token = late_weights("mixer", ya)
    (gates,) = _mm("gate_proj", (T // TL, N_DEV // 2, 1),
                   [(h1, full_rows(TL), wg, pl.BlockSpec((2, D, 512), lambda i, j, k: (j, 0, 0)), "nn_shards", None)],
                   [(_behind(b_gate, token), pl.BlockSpec((1, 1024), lambda i, j, k: (0, j)))],
                   [(_sds((T, 2 * D), BF16), rows(1024, TL))], gate_epi, None, PP)

    def merge_body(ya_ref, wa_ref, yb_ref, wb_ref, ga_ref, gb_ref, pa_ref, pb_ref, mg_ref):
        ya_v, yb_v = ya_ref[...], yb_ref[...]
        for s in range(N_DEV):
            cols = slice(s * 256, (s + 1) * 256)
            pa = _dot(ya_v, wa_ref[s], "nn")
            pb = _dot(yb_v, wb_ref[s], "nn")
            pa_ref[:, cols] = pa.astype(BF16)
            pb_ref[:, cols] = pb.astype(BF16)
            mg_ref[:, cols] = (ga_ref[:, cols] * pa + gb_ref[:, cols] * pb).astype(BF16)

    wo2 = wo.reshape(D, D)
    y_spec = pl.BlockSpec((TS, 512), lambda i: (i, 0))
    wp_spec = pl.BlockSpec((N_DEV, 512, 256), lambda i: (0, 0, 0))
    c_spec = pl.BlockSpec((TS, D), lambda i: (i, 0))
    pa, pb, merged = _pc(merge_body, name="merge_proj", grid=(T // TS,),
                         in_specs=[y_spec, wp_spec, y_spec, wp_spec, c_spec, pl.BlockSpec((TS, D), lambda i: (i, 1))],
                         out_specs=[c_spec, c_spec, c_spec],
                         out_shape=[_sds((T, D), BF16)] * 3, sem=("parallel",))(ya, wpa, yb, wpb, gates, gates)

    def add_res(acc, ex, o, pids):
        o[0][...] = ex[0][...] + acc

    def out_norm_epi(acc, ex, o, pids):
        x2v = ex[0][...] + acc
        r = lax.rsqrt(jnp.mean(x2v * x2v, axis=-1, keepdims=True) + EPS)
        o[0][...] = x2v
        o[1][...] = (x2v * r * ex[1][...]).astype(BF16)
        o[2][...] = r

    x2, h2, r2 = _mm("out_proj", (nt, 1, 1),
                     [(merged, full_rows(TM), wo2, pl.BlockSpec((D, D), lambda i, j, k: (0, 0)), "nn", None)],
                     [(x, full_rows(TM)), (norm_mlp, gvec)],
                     [(_sds((T, D), F32), full_rows(TM)), (_sds((T, D), BF16), full_rows(TM)),
                      (_sds((T, 1), F32), pl.BlockSpec((TM, 1), lambda i, j, k: (i, 0)))], out_norm_epi, None, PP)

    (token,) = late_weights("residual", x2)
    (wu,) = late_weights("up", h2)
    behind = lambda token: [] if token is None else [(token, pl.BlockSpec(memory_space=pl.ANY))]

    def up_epi(acc, ex, o, pids):
        o[0][...] = acc.astype(BF16)
        r = jnp.maximum(acc, 0.0)
        o[1][...] = (r * r).astype(BF16)

    u, act = _mm("mlp_up", (T // TL, N_DEV, 1),
                 [(h2, full_rows(TL), wu, pl.BlockSpec((None, D, 1024), lambda i, j, k: (j, 0, 0)), "nn", None)],
                 behind(token), [(_sds((T, D_FF), BF16), rows(1024, TL))] * 2, up_epi, None, PP)
    (wd,) = late_weights("down", act)
    wd2 = wd.reshape(D_FF, D)
    half_ji = pl.BlockSpec((TS, D // 2), lambda j, i, k: (i, j))
    (x3,) = _mm("mlp_down", (2, T // TS, 1),
                [(act, pl.BlockSpec((TS, D_FF), lambda j, i, k: (i, 0)), wd2,
                  pl.BlockSpec((D_FF, D // 2), lambda j, i, k: (0, j)), "nn", None)],
                [(x2, half_ji)], [(_sds((T, D), F32), half_ji)], add_res, None, PP)

    loss, dx3, dx3b, dg3 = _final_norm_loss(x3, norm_final.reshape(1, D), tgt)

    def du_epi(acc, ex, o, pids):
        o[0][...] = (acc * (2.0 * jnp.maximum(ex[0][...].astype(F32), 0.0))).astype(BF16)

    (du,) = _mm("mlp_down_dx", (T // TL, N_DEV, 1),
                [(dx3b, full_rows(TL), wd2, pl.BlockSpec((1024, D), lambda i, j, k: (j, 0)), "nt", None)],
                [(u, rows(1024, TL))], [(_sds((T, D_FF), BF16), rows(1024, TL))], du_epi, None, PP)
    (dwd,) = _mm("mlp_down_dw", (D // 1024, D_FF // 512, 1), [(act, tall(512), dx3b, tall_j(1024), "tn", None)],
                 [], [(_sds((D_FF, D), BF16), pl.BlockSpec((512, 1024), lambda j, i, k: (i, j)))], store_bf16, None, PP)
    dwd = dwd.reshape(N_DEV, D_FF // N_DEV, D)
    (dwu,) = _mm("mlp_up_dw", (N_DEV, D // 512, 1), [(h2, tall(512), du, tall_j(1024), "tn", None)],
                 [], [(_sds((N_DEV, D, 1024), BF16), pl.BlockSpec((None, 512, 1024), lambda j, i, k: (j, i, 0)))],
                 store_bf16, None, PP)

    def store_f32(acc, ex, o, pids):
        o[0][...] = acc

    out_ji = pl.BlockSpec((TM, 512), lambda j, i, k: (i, j))
    a_ji = lambda w: pl.BlockSpec((TM, w), lambda j, i, k: (i, 0))
    w_ji = lambda n: pl.BlockSpec((N_DEV, 512, n), lambda j, i, k: (0, j, 0))
    (dh2,) = _mm("mlp_up_dx", (D // 512, nt, 1), [(du, a_ji(D_FF), wu, w_ji(1024), "nt_shards", None)],
                 behind(send("mlp", dict(w_down=dwd, w_up=dwu))), [(_sds((T, D), F32), out_ji)], store_f32, None, PP)
    dx2, dx2b, dg2 = _rmsnorm_bwd("rmsnorm_mlp_bwd", dh2, x2, r2, norm_mlp, dx3, True)

    (dwo,) = _mm("out_proj_dw", (D // 1024, D // 512, 1), [(merged, tall(512), dx2b, tall_j(1024), "tn", None)],
                 [], [(_sds((D, D), BF16), pl.BlockSpec((512, 1024), lambda j, i, k: (i, j)))], store_bf16, None, PP)

    def dmerge_epi(acc, ex, o, pids):
        ga, gb, pa_v, pb_v = (e[...].astype(F32) for e in ex)
        o[0][...] = (acc * ga).astype(BF16)
        o[1][...] = (acc * gb).astype(BF16)
        dga = acc * pa_v * ga * (1.0 - ga)
        dgb = acc * pb_v * gb * (1.0 - gb)
        o[2][0] = dga.astype(BF16)
        o[2][1] = dgb.astype(BF16)
        sa = jnp.sum(dga, axis=0, keepdims=True)
        sb = jnp.sum(dgb, axis=0, keepdims=True)
        first = pids[1] == 0

        @pl.when(first)
        def _():
            o[3][...] = sa
            o[4][...] = sb

        @pl.when(jnp.logical_not(first))
        def _():
            o[3][...] += sa
            o[4][...] += sb

    cj = pl.BlockSpec((TL, 512), lambda j, i, k: (i, j))
    bj = pl.BlockSpec((1, 512), lambda j, i, k: (0, j))
    dpa, dpb, dgate, dba, dbb = _mm(
        "out_proj_dx", (4, T // TL, 1),
        [(dx2b, pl.BlockSpec((TL, D), lambda j, i, k: (i, 0)), wo2, pl.BlockSpec((512, D), lambda j, i, k: (j, 0)), "nt", None)],
        [(gates, cj), (gates, pl.BlockSpec((TL, 512), lambda j, i, k: (i, 4 + j))), (pa, cj), (pb, cj)],
        [(_sds((T, D), BF16), cj)] * 2 + [(_sds((2, T, D), BF16), pl.BlockSpec((2, TL, 512), lambda j, i, k: (0, i, j)))]
        + [(_sds((1, D), F32), bj)] * 2, dmerge_epi, None, ("parallel", "arbitrary", "arbitrary"))

    def store_shards(acc, ex, o, pids):
        for s in range(4):
            o[0][s] = acc[:, s * 256:(s + 1) * 256].astype(BF16)

    def proj_dw(name, y, dp):
        (dw,) = _mm(name, (2, 1, 1), [(y, tall(512), dp, tall_j(1024), "tn", None)],
                    [], [(_sds((N_DEV, 512, 256), BF16), pl.BlockSpec((4, 512, 256), lambda j, i, k: (j, 0, 0)))],
                    store_shards, None, PP)
        return dw

    def proj_dx(name, dp, w):
        (dy,) = _mm(name, (nt, 1, 1),
                    [(dp, full_rows(TM), w, pl.BlockSpec((N_DEV, 512, 256), lambda i, j, k: (0, 0, 0)), "nt_shards", None)],
                    [], [(_sds((T, 512), BF16), full_rows(TM, 512))], store_bf16, None, PP)
        return dy

    def in_dw(name, dact, dact_spec, n):
        def store_pair(acc, ex, o, pids):
            o[0][0] = acc[:, :n].astype(BF16)
            o[0][1] = acc[:, n:].astype(BF16)

        (dw,) = _mm(name, (N_DEV // 2, D // 512, 1), [(h1, tall(512), dact, dact_spec, "tn", None)],
                    [], [(_sds((N_DEV, D, n), BF16), pl.BlockSpec((2, 512, n), lambda j, i, k: (j, i, 0)))],
                    store_pair, None, PP)
        return dw

    dwg = in_dw("gate_proj_dw", dgate, pl.BlockSpec((None, T, 1024), lambda j, i, k: (j // 2, 0, j % 2)), 512)
    dwpa = proj_dw("proj_a_dw", ya, dpa)
    dwpb = proj_dw("proj_b_dw", yb, dpb)
    token = send("proj", dict(w_gate=dwg, w_out=dwo.reshape(N_DEV, D // N_DEV, D), w_proj_a=dwpa, w_proj_b=dwpb))
    dya = proj_dx("proj_a_dx", dpa, wpa)
    dyb = proj_dx("proj_b_dx", dpb, wpb)

    dq_a, dk_a, dv_a = _dilated_bwd(qkv, last_src, _behind(slopes, token), dya, ya, lse_a)
    dq_b, dk_b, dv_b, dtiles = _attn_b_bwd(qkv, tiles, dyb, yb, lse_b)
    drpb = _rpb_reduce(dtiles.reshape(N_HEADS_B * N_DR, GRID_W * GRID_W), sel)[:, :N_DC]
    dqkv = jnp.concatenate([*dq_a, dq_b, *dk_a, dk_b, *dv_a, dv_b], axis=1)
    dwq = in_dw("qkv_proj_dw", dqkv, tall_j(1536), 768)
    half_a = lambda h: pl.BlockSpec((None, TM, D), lambda j, i, k: (h, i, 0))
    half_w = lambda h: pl.BlockSpec((4, 512, 512), lambda j, i, k: (h, j, 0))
    (dh1,) = _mm("in_proj_dx", (D // 512, nt, 1),
                 [(dqkv, a_ji(3 * D), wq, w_ji(768), "nt_shards", None),
                  (dgate, half_a(0), wg, half_w(0), "nt_shards", None), (dgate, half_a(1), wg, half_w(1), "nt_shards", None)],
                 behind(send("in", dict(w_qkv=dwq))), [(_sds((T, D), F32), out_ji)], store_f32, None, PP)
    grad_x, dg1 = _rmsnorm_bwd("rmsnorm_mix_bwd", dh1, x, r1, norm_mix, dx2, False)

    small = dict(norm_mix=dg1, b_gate=jnp.concatenate([dba, dbb], axis=1), rpb=drpb, norm_mlp=dg2, norm_final=dg3)
    return loss, grad_x, small


BIG = ("w_qkv", "w_gate", "w_proj_a", "w_proj_b", "w_out", "w_up", "w_down")
SMALL = ("norm_mix", "b_gate", "rpb", "norm_mlp", "norm_final")
SMALL_LANES = 96 * 128


def _place():
    return lax.axis_index("x"), lax.axis_index("y"), lax.axis_index("c")


def _peer(x, y, c, mask):
    return x ^ (mask >> 2), y ^ ((mask >> 1) & 1), c ^ (mask & 1)


_HBM = pl.BlockSpec(memory_space=pltpu.HBM)
_SEM = pl.BlockSpec(memory_space=pltpu.SEMAPHORE)
_ANY = pl.BlockSpec(memory_space=pl.ANY)
_EFFECT = pltpu.SideEffectType.DATAFLOW_SIDE_EFFECTING


def _hbm(a):
    return pltpu.with_memory_space_constraint(a, pltpu.HBM)


def _other_chips(x, y):
    return [(1 - x, y), (x, 1 - y), (1 - x, 1 - y)]


def _block(x, y, c):
    return 4 * x + 2 * y + c


def _rdma(src, dst, send_sem, recv_sem, to):
    return pltpu.make_async_remote_copy(src_ref=src, dst_ref=dst, send_sem=send_sem, recv_sem=recv_sem, device_id=to,
                                        device_id_type=MESH)


def _gather_start(name, shards):
    n = len(shards)

    def body(*refs):
        srcs, lands = refs[:n], refs[n:2 * n]
        outs = refs[2 * n:]
        send, recv_d2d, recv_ici, local = outs[:n], outs[n:2 * n], outs[2 * n:3 * n], outs[3 * n:4 * n]
        token_ref = outs[6 * n]
        x, y, c = _place()
        for a in range(n):
            mine = lands[a].at[_block(x, y, c)]
            _rdma(srcs[a], mine, send[a].at[0], recv_d2d[a].at[0], (x, y, 1 - c)).start()
            for j, chip in enumerate(_other_chips(x, y)):
                _rdma(srcs[a], mine, send[a].at[1 + j], recv_ici[a].at[j], (*chip, c)).start()
        for a in range(n):
            pltpu.make_async_copy(srcs[a], lands[a].at[_block(x, y, c)], local[a].at[0]).start()
        token_ref[...] = jnp.zeros_like(token_ref)

    lands = [_hbm(lax.empty((N_DEV,) + s.shape, s.dtype)) for s in shards]
    srcs = [_hbm(s) for s in shards]
    dma = pltpu.SemaphoreType.DMA
    res = pl.pallas_call(
        body, name=name,
        out_shape=[dma((4,))] * n + [dma((1,))] * n + [dma((3,))] * n + [dma((1,))] * n
        + [pltpu.HBM(s.shape, s.dtype) for s in srcs] + [pltpu.HBM(l.shape, l.dtype) for l in lands]
        + [jax.ShapeDtypeStruct((8, 128), F32)],
        in_specs=[_HBM] * (2 * n), out_specs=[_SEM] * (4 * n) + [_HBM] * (2 * n) + [pl.BlockSpec(memory_space=pltpu.VMEM)],
        input_output_aliases={i: 4 * n + i for i in range(2 * n)},
        compiler_params=pltpu.CompilerParams(has_side_effects=_EFFECT))(*srcs, *lands)
    keys = ("send", "recv_d2d", "recv_ici", "local", "src", "land")
    return [dict(zip(keys, (res[k * n + a] for k in range(6)))) for a in range(n)], res[6 * n]


def _gather_forward(name, handles, after):
    n = len(handles)
    after = after if isinstance(after, (tuple, list)) else (after,)

    def body(*refs):
        lands, recv_ici = refs[:n], refs[n:2 * n]
        outs = refs[2 * n + len(after):]
        fwd_send, fwd_recv, token_ref = outs[n:2 * n], outs[2 * n:3 * n], outs[3 * n]
        x, y, c = _place()
        for a in range(n):
            for j, chip in enumerate(_other_chips(x, y)):
                blk = lands[a].at[_block(*chip, c)]
                _rdma(blk, blk, fwd_send[a].at[j], recv_ici[a].at[j], (*chip, c)).wait_recv()
                _rdma(blk, blk, fwd_send[a].at[j], fwd_recv[a].at[j], (x, y, 1 - c)).start()
        token_ref[...] = jnp.zeros_like(token_ref)

    dma = pltpu.SemaphoreType.DMA
    lands = [h["land"] for h in handles]
    res = pl.pallas_call(
        body, name=name,
        out_shape=[pltpu.HBM(l.shape, l.dtype) for l in lands] + [dma((3,))] * (2 * n) + [jax.ShapeDtypeStruct((8, 128), F32)],
        in_specs=[_HBM] * n + [_SEM] * n + [_ANY] * len(after),
        out_specs=[_HBM] * n + [_SEM] * (2 * n) + [pl.BlockSpec(memory_space=pltpu.VMEM)],
        input_output_aliases={i: i for i in range(n)},
        compiler_params=pltpu.CompilerParams(has_side_effects=_EFFECT))(*lands, *[h["recv_ici"] for h in handles], *after)
    return [dict(h, land=res[a], fwd_send=res[n + a], fwd_recv=res[2 * n + a]) for a, h in enumerate(handles)], res[3 * n]


def _gather_wait(name, handles, after):
    n = len(handles)

    def body(*refs):
        srcs, lands = refs[:n], refs[n:2 * n]
        send, recv_d2d, local, fwd_send, fwd_recv = (refs[(2 + k) * n:(3 + k) * n] for k in range(5))
        x, y, c = _place()
        sibling = (x, y, 1 - c)
        for a in range(n):
            mine = lands[a].at[_block(x, y, c)]
            _rdma(srcs[a], mine, send[a].at[0], recv_d2d[a].at[0], sibling).wait_send()
            _rdma(srcs[a], lands[a].at[_block(*sibling)], send[a].at[0], recv_d2d[a].at[0], sibling).wait_recv()
            pltpu.make_async_copy(srcs[a], mine, local[a].at[0]).wait()
            for j, chip in enumerate(_other_chips(x, y)):
                _rdma(srcs[a], mine, send[a].at[1 + j], fwd_recv[a].at[j], (*chip, c)).wait_send()
                blk = lands[a].at[_block(*chip, c)]
                _rdma(blk, blk, fwd_send[a].at[j], fwd_recv[a].at[j], sibling).wait_send()
                got = lands[a].at[_block(*chip, 1 - c)]
                _rdma(got, got, fwd_send[a].at[j], fwd_recv[a].at[j], sibling).wait_recv()

    srcs = [h["src"] for h in handles]
    lands = [h["land"] for h in handles]
    sems = [h[k] for k in ("send", "recv_d2d", "local", "fwd_send", "fwd_recv") for h in handles]
    res = pl.pallas_call(
        body, name=name, out_shape=[pltpu.HBM(s.shape, s.dtype) for s in srcs] + [pltpu.HBM(l.shape, l.dtype) for l in lands],
        in_specs=[_HBM] * (2 * n) + [_SEM] * (5 * n) + [_ANY], out_specs=[_HBM] * (2 * n),
        input_output_aliases={i: i for i in range(2 * n)},
        compiler_params=pltpu.CompilerParams(has_side_effects=_EFFECT))(*srcs, *lands, *sems, after)
    return list(res[n:])


def _gather_small(small, after):
    def body(in_ref, *refs):
        out_ref, send_sems, recv_sems, local_sem = refs[len(after):]
        x, y, c = _place()
        me = 4 * x + 2 * y + c
        copies = [pltpu.make_async_copy(in_ref, out_ref.at[me], local_sem)]
        for mask in range(1, N_DEV):
            copies.append(pltpu.make_async_remote_copy(
                src_ref=in_ref, dst_ref=out_ref.at[me], send_sem=send_sems.at[mask - 1],
                recv_sem=recv_sems.at[mask - 1], device_id=_peer(x, y, c, mask), device_id_type=MESH))
        for cp in copies:
            cp.start()
        for cp in copies:
            cp.wait()

    any_spec = pl.BlockSpec(memory_space=pl.ANY)
    return pl.pallas_call(
        body, name="gather_small", in_specs=[any_spec] * (1 + len(after)), out_specs=any_spec,
        out_shape=jax.ShapeDtypeStruct((N_DEV,) + small.shape, small.dtype),
        scratch_shapes=[pltpu.SemaphoreType.DMA((7,)), pltpu.SemaphoreType.DMA((7,)), pltpu.SemaphoreType.DMA])(small, *after)


def _send_copy(src_refs, land_refs, send_sems, recv_sems, a, mask, scatter, x, y, c):
    px, py, pc = _peer(x, y, c, mask)
    src = src_refs[a].at[4 * px + 2 * py + pc] if scatter else src_refs[a]
    return pltpu.make_async_remote_copy(
        src_ref=src, dst_ref=land_refs[a].at[4 * x + 2 * y + c], send_sem=send_sems[a].at[mask - 1],
        recv_sem=recv_sems[a].at[mask - 1], device_id=(px, py, pc), device_id_type=MESH)


def _local_copy(src_refs, land_refs, local_sems, a, scatter, x, y, c):
    me = 4 * x + 2 * y + c
    return pltpu.make_async_copy(src_refs[a].at[me] if scatter else src_refs[a], land_refs[a].at[me], local_sems[a].at[0])


def _send_start(name, srcs, scatter, after=None):
    n = len(srcs)
    n_in = 2 * n + (after is not None)
    shapes = [s.shape[1:] if scatter else s.shape for s in srcs]

    def body(*refs):
        src_refs, land_refs = refs[:n], refs[n:2 * n]
        outs = refs[n_in:]
        send_sems, recv_sems, local_sems = outs[:n], outs[n:2 * n], outs[2 * n:3 * n]
        token_ref = outs[5 * n]
        x, y, c = _place()
        for a in range(n):
            for mask in range(1, N_DEV):
                _send_copy(src_refs, land_refs, send_sems, recv_sems, a, mask, scatter, x, y, c).start()
        for a in range(n):
            _local_copy(src_refs, land_refs, local_sems, a, scatter, x, y, c).start()
        token_ref[...] = jnp.zeros_like(token_ref)

    lands = [pltpu.with_memory_space_constraint(lax.empty((N_DEV,) + sh, s.dtype), pltpu.HBM) for sh, s in zip(shapes, srcs)]
    srcs = [pltpu.with_memory_space_constraint(s, pltpu.HBM) for s in srcs]
    res = pl.pallas_call(
        body, name=name,
        out_shape=[pltpu.SemaphoreType.DMA((7,))] * (2 * n) + [pltpu.SemaphoreType.DMA((1,))] * n
        + [pltpu.HBM(s.shape, s.dtype) for s in srcs] + [pltpu.HBM(l.shape, l.dtype) for l in lands]
        + [jax.ShapeDtypeStruct((8, 128), F32)],
        in_specs=[_HBM] * (2 * n) + [pl.BlockSpec(memory_space=pl.ANY)] * (n_in - 2 * n),
        out_specs=[_SEM] * (3 * n) + [_HBM] * (2 * n) + [pl.BlockSpec(memory_space=pltpu.VMEM)],
        input_output_aliases={i: 3 * n + i for i in range(2 * n)},
        compiler_params=pltpu.CompilerParams(has_side_effects=_EFFECT))(*srcs, *lands, *([] if after is None else [after]))
    handles = [tuple(res[k * n + a] for k in range(5)) for a in range(n)]
    return handles, res[5 * n]


def _send_wait(name, handles, scatter, after):
    n = len(handles)

    def body(*refs):
        src_refs, land_refs = refs[:n], refs[n:2 * n]
        send_sems, recv_sems, local_sems = refs[2 * n:3 * n], refs[3 * n:4 * n], refs[4 * n:5 * n]
        x, y, c = _place()
        for a in range(n):
            for mask in range(1, N_DEV):
                cp = _send_copy(src_refs, land_refs, send_sems, recv_sems, a, mask, scatter, x, y, c)
                cp.wait_send()
                cp.wait_recv()
            _local_copy(src_refs, land_refs, local_sems, a, scatter, x, y, c).wait()

    srcs = [h[3] for h in handles]
    lands = [h[4] for h in handles]
    res = pl.pallas_call(
        body, name=name, out_shape=[pltpu.HBM(s.shape, s.dtype) for s in srcs] + [pltpu.HBM(l.shape, l.dtype) for l in lands],
        in_specs=[_HBM] * (2 * n) + [_SEM] * (3 * n) + [pl.BlockSpec(memory_space=pl.ANY)], out_specs=[_HBM] * (2 * n),
        input_output_aliases={i: i for i in range(2 * n)},
        compiler_params=pltpu.CompilerParams(has_side_effects=_EFFECT))(
            *srcs, *lands, *[h[0] for h in handles], *[h[1] for h in handles], *[h[2] for h in handles], after)
    return list(res[n:])


def _adamw_math(g, w, m, v):
    nm = ADAM_B1 * m + (1.0 - ADAM_B1) * g
    nv = ADAM_B2 * v + (1.0 - ADAM_B2) * jnp.square(g)
    m_hat = nm / (1.0 - ADAM_B1 ** ADAM_STEP)
    v_hat = nv / (1.0 - ADAM_B2 ** ADAM_STEP)
    return -ADAM_LR * (m_hat / (jnp.sqrt(v_hat) + ADAM_EPS) + ADAM_WD * w), nm, nv


def _adamw_small(gathered, ws, ms, vs):
    n = len(ws)
    widths = [w.shape[1] for w in ws]

    def body(p_ref, *refs):
        ins, outs = refs[:3 * n], refs[3 * n:]
        off = 0
        for a in range(n):
            g = p_ref[0, :, off:off + widths[a]]
            for i in range(1, N_DEV):
                g = g + p_ref[i, :, off:off + widths[a]]
            outs[4 * a][...] = g
            outs[4 * a + 1][...], outs[4 * a + 2][...], outs[4 * a + 3][...] = _adamw_math(
                g, ins[a][...], ins[n + a][...], ins[2 * n + a][...])
            off += widths[a]

    whole = lambda shape: pl.BlockSpec(shape, lambda i: (0,) * len(shape))
    rows = [whole((1, wd)) for wd in widths]
    res = _pc(body, name="adamw_small", grid=(1,), in_specs=[whole(gathered.shape)] + rows * 3,
              out_specs=[r for r in rows for _ in range(4)],
              out_shape=[jax.ShapeDtypeStruct((1, wd), F32) for wd in widths for _ in range(4)],
              sem=("arbitrary",))(gathered, *ws, *ms, *vs)
    return [res[4 * a:4 * a + 4] for a in range(n)]


def _adamw(name, parts, w, m, v):
    K, n = w.shape
    tr = min(K, 128)

    def body(p_ref, w_ref, m_ref, v_ref, g_ref, d_ref, nm_ref, nv_ref):
        g = p_ref[0].astype(F32)
        for i in range(1, N_DEV):
            g = g + p_ref[i].astype(F32)
        g_ref[...] = g
        d_ref[...], nm_ref[...], nv_ref[...] = _adamw_math(g, w_ref[...], m_ref[...], v_ref[...])

    blk = pl.BlockSpec((tr, n), lambda i: (i, 0))
    return _pc(body, name=name, grid=(K // tr,), in_specs=[pl.BlockSpec((N_DEV, tr, n), lambda i: (0, i, 0)), blk, blk, blk],
               out_specs=[blk] * 4, out_shape=[jax.ShapeDtypeStruct((K, n), F32)] * 4, sem=("parallel",))(parts, w, m, v)


def _lane_padded(size):
    return -(-size // 128) * 128


def _pack_small(vals):
    rows = [jnp.pad(v.reshape(1, -1), ((0, 0), (0, _lane_padded(v.size) - v.size))) for v in vals]
    used = sum(r.shape[1] for r in rows)
    return jnp.concatenate(rows + [jnp.zeros((1, SMALL_LANES - used), F32)], axis=1)


def kernel(x, norm_mix, w_qkv, w_gate, b_gate, rpb, w_proj_a, w_proj_b, w_out, norm_mlp, w_up, w_down, norm_final, loss_target, m_norm_mix, m_w_qkv, m_w_gate, m_b_gate, m_rpb, m_w_proj_a, m_w_proj_b, m_w_out, m_norm_mlp, m_w_up, m_w_down, m_norm_final, v_norm_mix, v_w_qkv, v_w_gate, v_b_gate, v_rpb, v_w_proj_a, v_w_proj_b, v_w_out, v_norm_mlp, v_w_up, v_w_down, v_norm_final):
    w = dict(norm_mix=norm_mix, w_qkv=w_qkv, w_gate=w_gate, b_gate=b_gate, rpb=rpb, w_proj_a=w_proj_a, w_proj_b=w_proj_b,
             w_out=w_out, norm_mlp=norm_mlp, w_up=w_up, w_down=w_down, norm_final=norm_final)
    m = dict(norm_mix=m_norm_mix, w_qkv=m_w_qkv, w_gate=m_w_gate, b_gate=m_b_gate, rpb=m_rpb, w_proj_a=m_w_proj_a,
             w_proj_b=m_w_proj_b, w_out=m_w_out, norm_mlp=m_norm_mlp, w_up=m_w_up, w_down=m_w_down, norm_final=m_norm_final)
    v = dict(norm_mix=v_norm_mix, w_qkv=v_w_qkv, w_gate=v_w_gate, b_gate=v_b_gate, rpb=v_rpb, w_proj_a=v_w_proj_a,
             w_proj_b=v_w_proj_b, w_out=v_w_out, norm_mlp=v_norm_mlp, w_up=v_w_up, w_down=v_w_down, norm_final=v_norm_final)
    order = ("norm_mix", "w_qkv", "w_gate", "b_gate", "rpb", "w_proj_a", "w_proj_b", "w_out", "norm_mlp", "w_up",
             "w_down", "norm_final")

    stages = dict(qkv=("w_qkv",), gate=("w_gate", "w_proj_a", "w_proj_b", "w_out"), mlp=("w_up", "w_down"))
    first, token = _gather_start("gather_start_qkv", [w["w_qkv"][0].astype(BF16)])
    names = [n for s in ("gate", "mlp") for n in stages[s]]
    rest, token = _gather_start("gather_start", [_behind(w[n][0], token).astype(BF16) for n in names])
    handles = dict(zip(names, rest), w_qkv=first[0])

    def late_weights(stage, after):
        if stage == "qkv":
            forwarded, _ = _gather_forward("gather_forward_qkv", [handles["w_qkv"]], after)
            return tuple(_gather_wait("gather_wait_qkv", forwarded, after[0]))
        if stage == "attention":
            handles["gate"], tok = _gather_forward("gather_forward_gate", [handles[n] for n in stages["gate"]], after)
            return (tok,)
        if stage == "mixer":
            handles["up"], tok = _gather_forward("gather_forward_up", [handles["w_up"]], after)
            return tuple(_gather_wait("gather_wait_gate", handles["gate"], after)) + (tok,)
        if stage == "residual":
            handles["down"], tok = _gather_forward("gather_forward_down", [handles["w_down"]], after)
            return (tok,)
        return tuple(_gather_wait("gather_wait_" + stage, handles[stage], after))

    sent = {}

    def send(stage, grads):
        handles, tok = _send_start("grads_start_" + stage, list(grads.values()), True)
        sent[stage] = (tuple(grads), handles)
        return tok

    loss, grad_x, small = _local_step(x[0], loss_target[0], _behind(norm_mix, token), b_gate, rpb[0], norm_mlp, norm_final,
                                      late_weights, send)

    grad, delta, new_m, new_v = {}, {}, {}, {}
    after = grad_x
    for stage in ("mlp", "proj", "small", "in"):
        if stage == "small":
            all_small = after = _gather_small(_pack_small([small[n] for n in SMALL]), [d[0] for d in delta.values()])
            continue
        names, handles = sent[stage]
        for n, parts in zip(names, _send_wait("grads_wait_" + stage, handles, True, after)):
            res = _adamw("adamw_" + n, parts, w[n][0], m[n][0], v[n][0])
            grad[n], delta[n], new_m[n], new_v[n] = (r[None] for r in res)
            after = res[1]
    as_row = lambda a: jnp.pad(a.reshape(1, -1), ((0, 0), (0, _lane_padded(a.size) - a.size)))
    res = _adamw_small(all_small, *[[as_row(t[n]) for n in SMALL] for t in (w, m, v)])
    for n, rows in zip(SMALL, res):
        for tree, row in zip((grad, delta, new_m, new_v), rows):
            tree[n] = row[0, :w[n].size].reshape(w[n].shape)

    local_loss, _, _ = lax.optimization_barrier((loss[0, 0], res[0][1], delta["w_qkv"]))
    total = lax.psum(local_loss, ("x", "y", "c"))
    return (total, grad_x[None], *[grad[n] for n in order], *[delta[n] for n in order],
            *[new_m[n] for n in order], *[new_v[n] for n in order])
```

```python
import numpy as np
import jax
import jax.numpy as jnp
from jax import lax
from jax.experimental import pallas as pl
from jax.experimental.pallas import tpu as pltpu

F32 = jnp.float32
BF16 = jnp.bfloat16
MESH = pl.DeviceIdType.MESH

D_MODEL = 2048
HEAD_DIM = 128
N_HEADS = 16
N_HEADS_A = 12
N_HEADS_B = 4
DILATIONS = (1, 4, 16)
HALF_WINDOW = 64
GRID_W = 64
NA_ROWS = 8
NA_COLS = 16
D_FF = 4 * D_MODEL
N_DEV = 8
EPS = 1e-6
NEG = -1e30
SCALE = HEAD_DIM ** -0.5
ADAM_LR, ADAM_B1, ADAM_B2, ADAM_EPS, ADAM_WD, ADAM_STEP = 0.001, 0.9, 0.999, 1e-08, 0.01, 10

VMEM_LIMIT = 56 * 1024 * 1024
TM = 512
TL = 1024
TS = 256
TE = 512
BQ_A = 128


def _pc(body, *, name, grid, in_specs, out_specs, out_shape, scratch=(), sem=None):
    return pl.pallas_call(
        body, name=name, grid=grid, in_specs=in_specs, out_specs=out_specs, out_shape=out_shape,
        scratch_shapes=list(scratch),
        compiler_params=pltpu.CompilerParams(dimension_semantics=sem, vmem_limit_bytes=VMEM_LIMIT))


_DOT_DIMS = {"nn": (((1,), (0,)), ((), ())), "nt": (((1,), (1,)), ((), ())), "tn": (((0,), (0,)), ((), ()))}


def _dot(a, b, mode, precision=None):
    if mode == "nn_shards":
        return jnp.concatenate([_dot(a, b[s], "nn") for s in range(b.shape[0])], axis=1)
    if mode == "nt_shards":
        n = b.shape[2]
        out = _dot(a[:, :n], b[0], "nt")
        for s in range(1, b.shape[0]):
            out = out + _dot(a[:, s * n:(s + 1) * n], b[s], "nt")
        return out
    return lax.dot_general(a, b, _DOT_DIMS[mode], preferred_element_type=F32, precision=precision)


def _mm(name, grid, pairs, extras, outs, epilogue, acc_shape, sem):
    n_pairs, n_extra, n_out = len(pairs), len(extras), len(outs)
    k_axis = len(grid) - 1
    nk = grid[k_axis]

    def body(*refs):
        pids = [pl.program_id(ax) for ax in range(len(grid))]
        ab = refs[:2 * n_pairs]
        ex = refs[2 * n_pairs:2 * n_pairs + n_extra]
        o = refs[2 * n_pairs + n_extra:2 * n_pairs + n_extra + n_out]
        part = None
        for p, pair in enumerate(pairs):
            a = ab[2 * p][...]
            if pair[5] is not None:
                a = pair[5](a)
            d = _dot(a, ab[2 * p + 1][...], pair[4])
            part = d if part is None else part + d
        if nk == 1:
            epilogue(part, ex, o, pids)
        else:
            acc = refs[-1]
            k = pids[k_axis]

            @pl.when(k == 0)
            def _():
                acc[...] = part

            @pl.when(k > 0)
            def _():
                acc[...] += part

            @pl.when(k == nk - 1)
            def _():
                epilogue(acc[...], ex, o, pids)

    operands, in_specs = [], []
    for a, a_spec, b, b_spec, _, _ in pairs:
        operands += [a, b]
        in_specs += [a_spec, b_spec]
    for e, e_spec in extras:
        operands.append(e)
        in_specs.append(e_spec)
    res = _pc(body, name=name, grid=grid, in_specs=in_specs, out_specs=[s for _, s in outs],
              out_shape=[o for o, _ in outs], scratch=[pltpu.VMEM(acc_shape, F32)] if nk > 1 else [], sem=sem)(*operands)
    return res


def _rmsnorm_fwd(x, g, name):
    T, D = x.shape

    def body(x_ref, g_ref, h_ref, r_ref):
        xv = x_ref[...]
        r = lax.rsqrt(jnp.mean(xv * xv, axis=-1, keepdims=True) + EPS)
        h_ref[...] = (xv * r * g_ref[...]).astype(BF16)
        r_ref[...] = r

    return _pc(body, name=name, grid=(T // TM,),
               in_specs=[pl.BlockSpec((TM, D), lambda i: (i, 0)), pl.BlockSpec((1, D), lambda i: (0, 0))],
               out_specs=[pl.BlockSpec((TM, D), lambda i: (i, 0)), pl.BlockSpec((TM, 1), lambda i: (i, 0))],
               out_shape=[jax.ShapeDtypeStruct((T, D), BF16), jax.ShapeDtypeStruct((T, 1), F32)],
               sem=("parallel",))(x, g)


def _rmsnorm_bwd_block(dh, x, r, g, dres):
    xh = x * r
    dxh = dh * g
    dx = dres + r * (dxh - xh * jnp.mean(dxh * xh, axis=-1, keepdims=True))
    return dx, jnp.sum(dh * xh, axis=0, keepdims=True)


def _rmsnorm_bwd(name, dh, x, r, g, dres, want_bf16):
    T, D = x.shape

    def body(dh_ref, x_ref, r_ref, g_ref, dres_ref, *outs):
        dx, dgp = _rmsnorm_bwd_block(dh_ref[...].astype(F32), x_ref[...], r_ref[...], g_ref[...], dres_ref[...])
        outs[0][...] = dx
        if want_bf16:
            outs[1][...] = dx.astype(BF16)
        first = pl.program_id(0) == 0

        @pl.when(first)
        def _():
            outs[-1][...] = dgp

        @pl.when(jnp.logical_not(first))
        def _():
            outs[-1][...] += dgp

    row = pl.BlockSpec((TE, D), lambda i: (i, 0))
    vec = pl.BlockSpec((1, D), lambda i: (0, 0))
    n_row = 2 if want_bf16 else 1
    return _pc(body, name=name, grid=(T // TE,), in_specs=[row, row, pl.BlockSpec((TE, 1), lambda i: (i, 0)), vec, row],
               out_specs=[row] * n_row + [vec],
               out_shape=[jax.ShapeDtypeStruct((T, D), F32)] + [jax.ShapeDtypeStruct((T, D), BF16)] * (n_row - 1)
               + [jax.ShapeDtypeStruct((1, D), F32)], sem=("arbitrary",))(dh, x, r, g, dres)


def _final_norm_loss(x3, g, tgt):
    T, D = x3.shape

    def body(x_ref, g_ref, t_ref, loss_ref, dx_ref, dxb_ref, dg_ref):
        i = pl.program_id(0)
        xv = x_ref[...]
        gv = g_ref[...]
        r = lax.rsqrt(jnp.mean(xv * xv, axis=-1, keepdims=True) + EPS)
        xh = xv * r
        e = xh * gv - t_ref[...]
        part = 0.5 * jnp.sum(jnp.mean(e * e, axis=-1, keepdims=True))
        dy = e * (1.0 / D)
        dxh = dy * gv
        dx = r * (dxh - xh * jnp.mean(dxh * xh, axis=-1, keepdims=True))
        dx_ref[...] = dx
        dxb_ref[...] = dx.astype(BF16)
        dgp = jnp.sum(dy * xh, axis=0, keepdims=True)

        @pl.when(i == 0)
        def _():
            loss_ref[...] = jnp.zeros(loss_ref.shape, F32) + part
            dg_ref[...] = dgp

        @pl.when(i > 0)
        def _():
            loss_ref[...] += part
            dg_ref[...] += dgp

    row = pl.BlockSpec((TE, D), lambda i: (i, 0))
    vec = pl.BlockSpec((1, D), lambda i: (0, 0))
    return _pc(body, name="final_norm_loss", grid=(T // TE,), in_specs=[row, vec, row],
               out_specs=[pl.BlockSpec((8, 128), lambda i: (0, 0)), row, row, vec],
               out_shape=[jax.ShapeDtypeStruct((8, 128), F32), jax.ShapeDtypeStruct((T, D), F32),
                          jax.ShapeDtypeStruct((T, D), BF16), jax.ShapeDtypeStruct((1, D), F32)],
               sem=("arbitrary",))(x3, g, tgt)


def _alibi_slopes():
    return np.asarray(2.0 ** (-8.0 * np.arange(1, N_HEADS_A + 1) / N_HEADS_A), dtype=np.float32)


def _col_to_row(col):
    return jnp.transpose(jnp.broadcast_to(col, (BQ_A, BQ_A)))[0:1, :]


def _row_to_col(row):
    return jnp.transpose(jnp.broadcast_to(row, (BQ_A, BQ_A)))[:, 0:1]


def _row_spec(rows_per_step, first=0):
    return pl.BlockSpec((None, rows_per_step // BQ_A, BQ_A), lambda j, i: (first + j, i, 0))


def _a_windows(T):
    half = [HALF_WINDOW * d for d in DILATIONS]
    return half, [min(BQ_A + 2 * w, T) for w in half]


def _a_bias(slope, g, offset, wk, half):
    d = DILATIONS[g]
    dlt = offset + lax.broadcasted_iota(jnp.int32, (BQ_A, wk), 1) - lax.broadcasted_iota(jnp.int32, (BQ_A, wk), 0)
    ad = jnp.maximum(dlt, -dlt)
    ok = (ad <= half) & ((ad & (d - 1)) == 0)
    return jnp.where(ok, -slope * ad.astype(F32), NEG)


def _a_scores(q, kw, slope, g, offset, wk, half):
    return _dot(q, kw, "nt") * SCALE + _a_bias(slope, g, offset, wk, half)


N_JOINT = 2


SUB_J = 8
BS_J = SUB_J * BQ_A


def _a_qkv_specs(T):
    specs = []
    for g in range(N_JOINT):
        specs.append(pl.BlockSpec((BS_J, HEAD_DIM), lambda j, i, g=g: (i, 4 * g + j)))
        specs.append(pl.BlockSpec((T, HEAD_DIM), lambda j, i, g=g: (0, N_HEADS + 4 * g + j)))
        specs.append(pl.BlockSpec((T, HEAD_DIM), lambda j, i, g=g: (0, 2 * N_HEADS + 4 * g + j)))
    return specs


def _a_window(step, u, g, T, half, wks):
    t0 = (step * SUB_J + u) * BQ_A
    ws = pl.multiple_of(jnp.clip(t0 - half[g], 0, T - wks[g]), 64)
    return pl.ds(ws, wks[g]), ws - t0


def _attn_a_fwd(qkv, slopes, o_last, lse_last):
    T = qkv.shape[0]
    half, wks = _a_windows(T)

    def body(sl_ref, *refs):
        qkv_refs, (o_ref, l_ref, y_ref, lse_ref) = refs[:3 * N_JOINT], refs[3 * N_JOINT:]
        j = pl.program_id(0)
        for u in range(SUB_J):
            rows = slice(u * BQ_A, (u + 1) * BQ_A)
            ss, vws = [], []
            lse_last = _row_to_col(l_ref[u:u + 1, :])
            m = lse_last
            for g in range(N_JOINT):
                q_ref, k_ref, v_ref = qkv_refs[3 * g:3 * g + 3]
                win, offset = _a_window(pl.program_id(1), u, g, T, half, wks)
                ss.append(_a_scores(q_ref[rows, :], k_ref[win, :], sl_ref[4 * g + j], g, offset, wks[g], half[g]))
                vws.append(v_ref[win, :])
                m = jnp.maximum(m, jnp.max(ss[g], axis=-1, keepdims=True))
            den = jnp.exp(lse_last - m)
            acc = den * o_ref[rows, :].astype(F32)
            for g in range(N_JOINT):
                p = jnp.exp(ss[g] - m)
                den = den + jnp.sum(p, axis=-1, keepdims=True)
                acc = acc + _dot(p.astype(BF16), vws[g], "nn")
            y_ref[rows, :] = (acc / den).astype(BF16)
            lse_ref[u:u + 1, :] = _col_to_row(m + jnp.log(den))

    blk = pl.BlockSpec((BS_J, HEAD_DIM), lambda j, i: (i, j))
    col = _row_spec(BS_J)
    return _pc(body, name="attn_a_fwd", grid=(4, T // BS_J),
               in_specs=[pl.BlockSpec(memory_space=pltpu.SMEM)] + _a_qkv_specs(T) + [blk, col], out_specs=[blk, col],
               out_shape=[jax.ShapeDtypeStruct((T, 4 * HEAD_DIM), BF16), jax.ShapeDtypeStruct((4, T // BQ_A, BQ_A), F32)],
               sem=("parallel", "parallel"))(slopes, *([qkv] * (3 * N_JOINT)), o_last, lse_last)


def _attn_a_bwd(qkv, slopes, dy, y, lse):
    T = qkv.shape[0]
    steps = T // BS_J
    half, wks = _a_windows(T)

    def body(sl_ref, *refs):
        qkv_refs = refs[:3 * N_JOINT]
        dy_ref, y_ref, lse_ref, dq_ref, dk_ref, dv_ref, delta_ref, dk_acc, dv_acc = refs[3 * N_JOINT:]
        j = pl.program_id(0)
        step = pl.program_id(1)

        @pl.when(step == 0)
        def _():
            dk_acc[...] = jnp.zeros_like(dk_acc)
            dv_acc[...] = jnp.zeros_like(dv_acc)

        for u in range(SUB_J):
            rows = slice(u * BQ_A, (u + 1) * BQ_A)
            dyv = dy_ref[rows, :]
            delta = jnp.sum(dyv.astype(F32) * y_ref[rows, :].astype(F32), axis=-1, keepdims=True)
            delta_ref[u:u + 1, :] = _col_to_row(delta)
            lse_v = _row_to_col(lse_ref[u:u + 1, :])
            for g in range(N_JOINT):
                q_ref, k_ref, v_ref = qkv_refs[3 * g:3 * g + 3]
                win, offset = _a_window(step, u, g, T, half, wks)
                qv = q_ref[rows, :]
                kw = k_ref[win, :]
                s = _a_scores(qv, kw, sl_ref[4 * g + j], g, offset, wks[g], half[g])
                p = jnp.exp(s - lse_v)
                dp = _dot(dyv, v_ref[win, :], "nt")
                ds = (p * (dp - delta)).astype(BF16)
                dq_ref[g, rows, :] = (_dot(ds, kw, "nn") * SCALE).astype(BF16)
                dk_acc[g, win, :] += _dot(ds, qv, "tn") * SCALE
                dv_acc[g, win, :] += _dot(p.astype(BF16), dyv, "tn")

        @pl.when(step == steps - 1)
        def _():
            dk_ref[...] = dk_acc[...].astype(BF16)
            dv_ref[...] = dv_acc[...].astype(BF16)

    blk = pl.BlockSpec((BS_J, HEAD_DIM), lambda j, i: (i, j))
    col = _row_spec(BS_J)
    whole = pl.BlockSpec((N_JOINT, T, HEAD_DIM), lambda j, i: (0, 0, j))
    full = jax.ShapeDtypeStruct((N_JOINT, T, 4 * HEAD_DIM), BF16)
    return _pc(body, name="attn_a_bwd", grid=(4, steps),
               in_specs=[pl.BlockSpec(memory_space=pltpu.SMEM)] + _a_qkv_specs(T) + [blk, blk, col],
               out_specs=[pl.BlockSpec((N_JOINT, BS_J, HEAD_DIM), lambda j, i: (0, i, j)), whole, whole, col],
               out_shape=[full, full, full, jax.ShapeDtypeStruct((4, T // BQ_A, BQ_A), F32)],
               scratch=[pltpu.VMEM((N_JOINT, T, HEAD_DIM), F32), pltpu.VMEM((N_JOINT, T, HEAD_DIM), F32)],
               sem=("parallel", "arbitrary"))(slopes, *([qkv] * (3 * N_JOINT)), dy, y, lse)


SUB_A = 8


def _to_residue_major(a, d, axis=0):
    if d == 1:
        return a
    sh = a.shape
    a = a.reshape(sh[:axis] + (sh[axis] // d, d) + sh[axis + 1:])
    return jnp.swapaxes(a, axis, axis + 1).reshape(sh)


def _to_token_order(a, d, axis=0):
    if d == 1:
        return a
    sh = a.shape
    a = a.reshape(sh[:axis] + (d, sh[axis] // d) + sh[axis + 1:])
    return jnp.swapaxes(a, axis, axis + 1).reshape(sh)


def _band_bias(slope, offset, wk):
    dlt = offset + lax.broadcasted_iota(jnp.int32, (BQ_A, wk), 1) - lax.broadcasted_iota(jnp.int32, (BQ_A, wk), 0)
    ad = jnp.maximum(dlt, -dlt)
    return jnp.where(ad <= HALF_WINDOW, -slope * ad.astype(F32), NEG)


def _band_window(step, u, seg, wk):
    t0 = (step * SUB_A + u) * BQ_A
    lo = (t0 // seg) * seg
    ws = pl.multiple_of(jnp.clip(t0 - HALF_WINDOW, lo, lo + seg - wk), 64)
    return slice(u * BQ_A, (u + 1) * BQ_A), pl.ds(ws, wk), ws - t0


def _band_specs(T, cols):
    bs = SUB_A * BQ_A
    assert T % bs == 0 and T // max(DILATIONS) >= BQ_A, "a query sub-block must lie inside one segment"
    return [pl.BlockSpec(memory_space=pltpu.SMEM),
            pl.BlockSpec((bs, HEAD_DIM), lambda j, i: (i, cols[0] + j)),
            pl.BlockSpec((T, HEAD_DIM), lambda j, i: (0, cols[1] + j)),
            pl.BlockSpec((T, HEAD_DIM), lambda j, i: (0, cols[2] + j))]


def _banded_fwd(name, src, cols, seg, slopes):
    T = src.shape[0]
    bs = SUB_A * BQ_A
    wk = min(BQ_A + 2 * HALF_WINDOW, seg)

    def body(sl_ref, q_ref, k_ref, v_ref, o_ref, lse_ref):
        slope = sl_ref[pl.program_id(0)]
        for u in range(SUB_A):
            rows, win, offset = _band_window(pl.program_id(1), u, seg, wk)
            s = _dot(q_ref[rows, :], k_ref[win, :], "nt") * SCALE + _band_bias(slope, offset, wk)
            m = jnp.max(s, axis=-1, keepdims=True)
            p = jnp.exp(s - m)
            den = jnp.sum(p, axis=-1, keepdims=True)
            o_ref[rows, :] = (_dot(p.astype(BF16), v_ref[win, :], "nn") / den).astype(BF16)
            lse_ref[u:u + 1, :] = _col_to_row(m + jnp.log(den))

    return _pc(body, name=name, grid=(4, T // bs), in_specs=_band_specs(T, cols),
               out_specs=[pl.BlockSpec((bs, HEAD_DIM), lambda j, i: (i, j)), _row_spec(bs)],
               out_shape=[jax.ShapeDtypeStruct((T, 4 * HEAD_DIM), BF16), jax.ShapeDtypeStruct((4, T // BQ_A, BQ_A), F32)],
               sem=("parallel", "parallel"))(slopes, src, src, src)


def _banded_bwd(name, src, cols, seg, slopes, dy, delta_lse):
    T = src.shape[0]
    bs = SUB_A * BQ_A
    steps = T // bs
    wk = min(BQ_A + 2 * HALF_WINDOW, seg)

    def body(sl_ref, q_ref, k_ref, v_ref, dy_ref, delta_ref, lse_ref, dq_ref, dk_ref, dv_ref, dk_acc, dv_acc):
        slope = sl_ref[pl.program_id(0)]
        step = pl.program_id(1)

        @pl.when(step == 0)
        def _():
            dk_acc[...] = jnp.zeros_like(dk_acc)
            dv_acc[...] = jnp.zeros_like(dv_acc)

        for u in range(SUB_A):
            rows, win, offset = _band_window(step, u, seg, wk)
            qv, kw, dyv = q_ref[rows, :], k_ref[win, :], dy_ref[rows, :]
            s = _dot(qv, kw, "nt") * SCALE + _band_bias(slope, offset, wk)
            p = jnp.exp(s - _row_to_col(lse_ref[u:u + 1, :]))
            ds = (p * (_dot(dyv, v_ref[win, :], "nt") - _row_to_col(delta_ref[u:u + 1, :]))).astype(BF16)
            dq_ref[rows, :] = (_dot(ds, kw, "nn") * SCALE).astype(BF16)
            dk_acc[win, :] += _dot(ds, qv, "tn") * SCALE
            dv_acc[win, :] += _dot(p.astype(BF16), dyv, "tn")

        @pl.when(step == steps - 1)
        def _():
            dk_ref[...] = dk_acc[...].astype(BF16)
            dv_ref[...] = dv_acc[...].astype(BF16)

    blk = pl.BlockSpec((bs, HEAD_DIM), lambda j, i: (i, j))
    whole = pl.BlockSpec((T, HEAD_DIM), lambda j, i: (0, j))
    full = jax.ShapeDtypeStruct((T, 4 * HEAD_DIM), BF16)
    return _pc(body, name=name, grid=(4, steps),
               in_specs=_band_specs(T, cols) + [blk, _row_spec(bs), _row_spec(bs, 4)],
               out_specs=[blk, whole, whole], out_shape=[full, full, full],
               scratch=[pltpu.VMEM((T, HEAD_DIM), F32), pltpu.VMEM((T, HEAD_DIM), F32)],
               sem=("parallel", "arbitrary"))(slopes, src, src, src, dy, delta_lse, delta_lse)


def _last_group_source(qkv):
    third = qkv.shape[1] // 3
    own = jnp.concatenate([qkv[:, t * third + 512 * N_JOINT:t * third + 512 * (N_JOINT + 1)] for t in range(3)], axis=1)
    return _to_residue_major(own, DILATIONS[N_JOINT])


def _dilated_fwd(qkv, slopes):
    d = DILATIONS[N_JOINT]
    last_src = _last_group_source(qkv)
    o, lse = _banded_fwd("attn_a_last_fwd", last_src, (0, 4, 8), qkv.shape[0] // d, slopes[4 * N_JOINT:] * float(d))
    lse = _to_token_order(lse.reshape(4, -1), d, axis=1).reshape(lse.shape)
    y, joint_lse = _attn_a_fwd(qkv, slopes, _to_token_order(o, d), lse)
    return y, joint_lse, last_src


def _dilated_bwd(qkv, last_src, slopes, dy, y, lse):
    d = DILATIONS[N_JOINT]
    dq, dk, dv, delta = _attn_a_bwd(qkv, slopes, dy, y, lse)
    delta_lse = jnp.concatenate([delta, lse], axis=0)
    delta_lse = _to_residue_major(delta_lse.reshape(8, -1), d, axis=1).reshape(delta_lse.shape)
    last = _banded_bwd("attn_a_last_bwd", last_src, (0, 4, 8), qkv.shape[0] // d,
                       slopes[4 * N_JOINT:] * float(d), _to_residue_major(dy, d), delta_lse)
    return [[joint[g] for g in range(N_JOINT)] + [_to_token_order(own, d)] for joint, own in zip((dq, dk, dv), last)]


N_DR = 2 * NA_ROWS - 1
N_DC = 2 * NA_COLS - 1
WK_B = NA_ROWS * GRID_W


def _toeplitz_consts():
    col = np.arange(GRID_W)
    start = np.clip(col - NA_COLS // 2, 0, GRID_W - NA_COLS)
    ok = (col[None, :] >= start[:, None]) & (col[None, :] < start[:, None] + NA_COLS)
    dc = np.clip(col[None, :] - col[:, None], -(NA_COLS - 1), NA_COLS - 1) + NA_COLS - 1
    sel = np.zeros((128, GRID_W * GRID_W), np.float32)
    sel[dc.reshape(-1), np.arange(GRID_W * GRID_W)] = ok.reshape(-1).astype(np.float32)
    mask = np.where(ok, 0.0, NEG).astype(np.float32).reshape(1, -1)
    return sel, mask


def _rpb_expand(rpb_rows, sel, mask):
    def body(r_ref, s_ref, m_ref, o_ref):
        o_ref[...] = _dot(r_ref[...], s_ref[...], "nn", precision=lax.Precision.HIGHEST) + m_ref[...]

    n = rpb_rows.shape[0]
    return _pc(body, name="rpb_expand", grid=(1,),
               in_specs=[pl.BlockSpec((n, 128), lambda i: (0, 0)), pl.BlockSpec(sel.shape, lambda i: (0, 0)),
                         pl.BlockSpec(mask.shape, lambda i: (0, 0))],
               out_specs=pl.BlockSpec((n, sel.shape[1]), lambda i: (0, 0)),
               out_shape=jax.ShapeDtypeStruct((n, sel.shape[1]), F32), sem=("arbitrary",))(rpb_rows, sel, mask)


def _rpb_reduce(dtiles, sel):
    def body(t_ref, s_ref, o_ref):
        o_ref[...] = _dot(t_ref[...], s_ref[...], "nt", precision=lax.Precision.HIGHEST)

    n = dtiles.shape[0]
    return _pc(body, name="rpb_reduce", grid=(1,),
               in_specs=[pl.BlockSpec(dtiles.shape, lambda i: (0, 0)), pl.BlockSpec(sel.shape, lambda i: (0, 0))],
               out_specs=pl.BlockSpec((n, 128), lambda i: (0, 0)),
               out_shape=jax.ShapeDtypeStruct((n, 128), F32), sem=("arbitrary",))(dtiles, sel)


ROWS_B = 16
BQ_B = ROWS_B * GRID_W


def _b_specs(T):
    return [pl.BlockSpec((BQ_B, HEAD_DIM), lambda h, r: (r, N_HEADS_A + h)),
            pl.BlockSpec((T, HEAD_DIM), lambda h, r: (0, N_HEADS + N_HEADS_A + h)),
            pl.BlockSpec((T, HEAD_DIM), lambda h, r: (0, 2 * N_HEADS + N_HEADS_A + h)),
            pl.BlockSpec((None, N_DR, GRID_W, GRID_W), lambda h, r: (h, 0, 0, 0))]


def _b_row(step, u, rows, t_ref, bias_ref):
    r = step * ROWS_B + u
    rs = jnp.clip(r - NA_ROWS // 2, 0, rows - NA_ROWS)
    first_tile = rs - r + NA_ROWS - 1
    for a in range(NA_ROWS):
        bias_ref[u, :, a * GRID_W:(a + 1) * GRID_W] = t_ref[first_tile + a]
    return slice(u * GRID_W, (u + 1) * GRID_W), pl.ds(pl.multiple_of(rs * GRID_W, GRID_W), WK_B), first_tile


def _attn_b_fwd(qkv, tiles):
    T = qkv.shape[0]
    rows = T // GRID_W

    def body(q_ref, k_ref, v_ref, t_ref, y_ref, lse_ref, bias_ref):
        for u in range(ROWS_B):
            blk, win, _ = _b_row(pl.program_id(1), u, rows, t_ref, bias_ref)
            s = _dot(q_ref[blk, :], k_ref[win, :], "nt") * SCALE + bias_ref[u]
            m = jnp.max(s, axis=-1, keepdims=True)
            p = jnp.exp(s - m)
            den = jnp.sum(p, axis=-1, keepdims=True)
            y_ref[blk, :] = (_dot(p.astype(BF16), v_ref[win, :], "nn") / den).astype(BF16)
            lse_ref[blk, :] = m + jnp.log(den)

    return _pc(body, name="attn_b_fwd", grid=(N_HEADS_B, rows // ROWS_B), in_specs=_b_specs(T),
               out_specs=[pl.BlockSpec((BQ_B, HEAD_DIM), lambda h, r: (r, h)),
                          pl.BlockSpec((None, BQ_B, 1), lambda h, r: (h, r, 0))],
               out_shape=[jax.ShapeDtypeStruct((T, N_HEADS_B * HEAD_DIM), BF16),
                          jax.ShapeDtypeStruct((N_HEADS_B, T, 1), F32)],
               scratch=[pltpu.VMEM((ROWS_B, GRID_W, WK_B), F32)], sem=("parallel", "parallel"))(qkv, qkv, qkv, tiles)


def _attn_b_bwd(qkv, tiles, dy, y, lse):
    T = qkv.shape[0]
    rows = T // GRID_W
    steps = rows // ROWS_B

    def body(q_ref, k_ref, v_ref, t_ref, dy_ref, y_ref, lse_ref, dq_ref, dk_ref, dv_ref, dt_ref,
             bias_ref, ds_ref, dk_acc, dv_acc):
        step = pl.program_id(1)

        @pl.when(step == 0)
        def _():
            dk_acc[...] = jnp.zeros_like(dk_acc)
            dv_acc[...] = jnp.zeros_like(dv_acc)
            dt_ref[...] = jnp.zeros_like(dt_ref)

        for u in range(ROWS_B):
            blk, win, first_tile = _b_row(step, u, rows, t_ref, bias_ref)
            qv = q_ref[blk, :]
            kw = k_ref[win, :]
            dyv = dy_ref[blk, :]
            s = _dot(qv, kw, "nt") * SCALE + bias_ref[u]
            p = jnp.exp(s - lse_ref[blk, :])
            delta = jnp.sum(dyv.astype(F32) * y_ref[blk, :].astype(F32), axis=-1, keepdims=True)
            ds = p * (_dot(dyv, v_ref[win, :], "nt") - delta)
            ds_ref[u] = ds
            for a in range(NA_ROWS):
                dt_ref[first_tile + a] += ds_ref[u, :, a * GRID_W:(a + 1) * GRID_W]
            dsb = ds.astype(BF16)
            dq_ref[blk, :] = (_dot(dsb, kw, "nn") * SCALE).astype(BF16)
            dk_acc[win, :] += _dot(dsb, qv, "tn") * SCALE
            dv_acc[win, :] += _dot(p.astype(BF16), dyv, "tn")

        @pl.when(step == steps - 1)
        def _():
            dk_ref[...] = dk_acc[...].astype(BF16)
            dv_ref[...] = dv_acc[...].astype(BF16)

    blk_spec = pl.BlockSpec((BQ_B, HEAD_DIM), lambda h, r: (r, h))
    whole = pl.BlockSpec((T, HEAD_DIM), lambda h, r: (0, h))
    full = jax.ShapeDtypeStruct((T, N_HEADS_B * HEAD_DIM), BF16)
    return _pc(body, name="attn_b_bwd", grid=(N_HEADS_B, steps),
               in_specs=_b_specs(T) + [blk_spec, blk_spec, pl.BlockSpec((None, BQ_B, 1), lambda h, r: (h, r, 0))],
               out_specs=[blk_spec, whole, whole, pl.BlockSpec((None, N_DR, GRID_W, GRID_W), lambda h, r: (h, 0, 0, 0))],
               out_shape=[full, full, full, jax.ShapeDtypeStruct((N_HEADS_B, N_DR, GRID_W, GRID_W), F32)],
               scratch=[pltpu.VMEM((ROWS_B, GRID_W, WK_B), F32), pltpu.VMEM((ROWS_B, GRID_W, WK_B), F32),
                        pltpu.VMEM((T, HEAD_DIM), F32), pltpu.VMEM((T, HEAD_DIM), F32)],
               sem=("parallel", "arbitrary"))(qkv, qkv, qkv, tiles, dy, y, lse)


def _sds(shape, dtype):
    return jax.ShapeDtypeStruct(shape, dtype)


def _behind(value, token):
    return value if token is None else value + token[0, 0]


def _local_step(x, tgt, norm_mix, b_gate, rpb, norm_mlp, norm_final, late_weights, send, riders=None):
    T, D = x.shape
    nt = T // TM
    PP = ("parallel", "parallel", "arbitrary")
    slopes = jnp.asarray(_alibi_slopes())
    sel_np, mask_np = _toeplitz_consts()
    sel, mask = jnp.asarray(sel_np), jnp.asarray(mask_np)

    def rows(w, tm=TM):
        return pl.BlockSpec((tm, w), lambda i, j, k: (i, j))

    def full_rows(tm, w=D):
        return pl.BlockSpec((tm, w), lambda i, j, k: (i, 0))

    def tall(w):
        return pl.BlockSpec((T, w), lambda j, i, k: (0, i))

    def tall_j(w):
        return pl.BlockSpec((T, w), lambda j, i, k: (0, j))

    gvec = pl.BlockSpec((1, D), lambda i, j, k: (0, 0))

    h1, r1 = _rmsnorm_fwd(x, norm_mix, "rmsnorm_mix")
    rpb_rows = jnp.pad(rpb.reshape(N_HEADS_B * N_DR, N_DC), ((0, 0), (0, 128 - N_DC)))
    tiles = _rpb_expand(rpb_rows, sel, mask).reshape(N_HEADS_B, N_DR, GRID_W, GRID_W)
    (wq,) = late_weights("qkv", (h1, tiles))

    def store_bf16(acc, ex, o, pids):
        o[0][...] = acc.astype(BF16)

    (qkv,) = _mm("qkv_proj", (T // TL, N_DEV // 2, 1),
                 [(h1, full_rows(TL), wq, pl.BlockSpec((2, D, 768), lambda i, j, k: (j, 0, 0)), "nn_shards", None)],
                 [], [(_sds((T, 3 * D), BF16), rows(1536, TL))], store_bf16, None, PP)

    def gate_epi(acc, ex, o, pids):
        o[0][...] = jax.nn.sigmoid(acc + ex[0][...]).astype(BF16)

    late_weights("attention", qkv)
    ya, lse_a, last_src = _dilated_fwd(qkv, slopes)
    yb, lse_b = _attn_b_fwd(qkv, tiles)

    wg, wpa, wpb, wo, token = late_weights("mixer", ya)
    (gates,) = _mm("gate_proj", (T // TL, N_DEV // 2, 1),
                   [(h1, full_rows(TL), wg, pl.BlockSpec((2, D, 512), lambda i, j, k: (j, 0, 0)), "nn_shards", None)],
                   [(_behind(b_gate, token), pl.BlockSpec((1, 1024), lambda i, j, k: (0, j)))],
                   [(_sds((T, 2 * D), BF16), rows(1024, TL))], gate_epi, None, PP)

    def merge_body(ya_ref, wa_ref, yb_ref, wb_ref, ga_ref, gb_ref, pa_ref, pb_ref, mg_ref):
        ya_v, yb_v = ya_ref[...], yb_ref[...]
        for s in range(N_DEV):
            cols = slice(s * 256, (s + 1) * 256)
            pa = _dot(ya_v, wa_ref[s], "nn")
            pb = _dot(yb_v, wb_ref[s], "nn")
            pa_ref[:, cols] = pa.astype(BF16)
            pb_ref[:, cols] = pb.astype(BF16)
            mg_ref[:, cols] = (ga_ref[:, cols] * pa + gb_ref[:, cols] * pb).astype(BF16)

    wo2 = wo.reshape(D, D)
    y_spec = pl.BlockSpec((TS, 512), lambda i: (i, 0))
    wp_spec = pl.BlockSpec((N_DEV, 512, 256), lambda i: (0, 0, 0))
    c_spec = pl.BlockSpec((TS, D), lambda i: (i, 0))
    pa, pb, merged = _pc(merge_body, name="merge_proj", grid=(T // TS,),
                         in_specs=[y_spec, wp_spec, y_spec, wp_spec, c_spec, pl.BlockSpec((TS, D), lambda i: (i, 1))],
                         out_specs=[c_spec, c_spec, c_spec],
                         out_shape=[_sds((T, D), BF16)] * 3, sem=("parallel",))(ya, wpa, yb, wpb, gates, gates)

    def add_res(acc, ex, o, pids):
        o[0][...] = ex[0][...] + acc

    def out_norm_epi(acc, ex, o, pids):
        x2v = ex[0][...] + acc
        r = lax.rsqrt(jnp.mean(x2v * x2v, axis=-1, keepdims=True) + EPS)
        o[0][...] = x2v
        o[1][...] = (x2v * r * ex[1][...]).astype(BF16)
        o[2][...] = r

    x2, h2, r2 = _mm("out_proj", (nt, 1, 1),
                     [(merged, full_rows(TM), wo2, pl.BlockSpec((D, D), lambda i, j, k: (0, 0)), "nn", None)],
                     [(x, full_rows(TM)), (norm_mlp, gvec)],
                     [(_sds((T, D), F32), full_rows(TM)), (_sds((T, D), BF16), full_rows(TM)),
                      (_sds((T, 1), F32), pl.BlockSpec((TM, 1), lambda i, j, k: (i, 0)))], out_norm_epi, None, PP)

    (token,) = late_weights("residual", x2)
    (wu,) = late_weights("up", h2)
    behind = lambda token: [] if token is None else [(token, pl.BlockSpec(memory_space=pl.ANY))]

    def up_epi(acc, ex, o, pids):
        o[0][...] = acc.astype(BF16)
        r = jnp.maximum(acc, 0.0)
        o[1][...] = (r * r).astype(BF16)

    u, act = _mm("mlp_up", (T // TL, N_DEV, 1),
                 [(h2, full_rows(TL), wu, pl.BlockSpec((None, D, 1024), lambda i, j, k: (j, 0, 0)), "nn", None)],
                 behind(token), [(_sds((T, D_FF), BF16), rows(1024, TL))] * 2, up_epi, None, PP)
    (wd,) = late_weights("down", act)
    wd2 = wd.reshape(D_FF, D)
    half_ji = pl.BlockSpec((TS, D // 2), lambda j, i, k: (i, j))
    (x3,) = _mm("mlp_down", (2, T // TS, 1),
                [(act, pl.BlockSpec((TS, D_FF), lambda j, i, k: (i, 0)), wd2,
                  pl.BlockSpec((D_FF, D // 2), lambda j, i, k: (0, j)), "nn", None)],
                [(x2, half_ji)], [(_sds((T, D), F32), half_ji)], add_res, None, PP)

    loss, dx3, dx3b, dg3 = _final_norm_loss(x3, norm_final.reshape(1, D), tgt)

    def du_epi(acc, ex, o, pids):
        o[0][...] = (acc * (2.0 * jnp.maximum(ex[0][...].astype(F32), 0.0))).astype(BF16)

    (du,) = _mm("mlp_down_dx", (T // TL, N_DEV, 1),
                [(dx3b, full_rows(TL), wd2, pl.BlockSpec((1024, D), lambda i, j, k: (j, 0)), "nt", None)],
                [(u, rows(1024, TL))], [(_sds((T, D_FF), BF16), rows(1024, TL))], du_epi, None, PP)
    (dwd,) = _mm("mlp_down_dw", (D // 1024, D_FF // 512, 1), [(act, tall(512), dx3b, tall_j(1024), "tn", None)],
                 [], [(_sds((D_FF, D), BF16), pl.BlockSpec((512, 1024), lambda j, i, k: (i, j)))], store_bf16, None, PP)
    dwd = dwd.reshape(N_DEV, D_FF // N_DEV, D)
    (dwu,) = _mm("mlp_up_dw", (N_DEV, D // 512, 1), [(h2, tall(512), du, tall_j(1024), "tn", None)],
                 [], [(_sds((N_DEV, D, 1024), BF16), pl.BlockSpec((None, 512, 1024), lambda j, i, k: (j, i, 0)))],
                 store_bf16, None, PP)

    out_ji = pl.BlockSpec((TM, 512), lambda j, i, k: (i, j))
    a_ji = lambda w: pl.BlockSpec((TM, w), lambda j, i, k: (i, 0))
    w_ji = lambda n: pl.BlockSpec((N_DEV, 512, n), lambda j, i, k: (0, j, 0))
    (dh2,) = _mm("mlp_up_dx", (D // 512, nt, 1), [(du, a_ji(D_FF), wu, w_ji(1024), "nt_shards", None)],
                 behind(send("mlp", dict(w_down=dwd, w_up=dwu))), [(_sds((T, D), BF16), out_ji)], store_bf16, None, PP)
    dx2, dx2b, dg2 = _rmsnorm_bwd("rmsnorm_mlp_bwd", dh2, x2, r2, norm_mlp, dx3, True)

    (dwo,) = _mm("out_proj_dw", (D // 1024, D // 512, 1), [(merged, tall(512), dx2b, tall_j(1024), "tn", None)],
                 [], [(_sds((D, D), BF16), pl.BlockSpec((512, 1024), lambda j, i, k: (i, j)))], store_bf16, None, PP)

    def dmerge_epi(acc, ex, o, pids):
        ga, gb, pa_v, pb_v = (e[...].astype(F32) for e in ex)
        o[0][...] = (acc * ga).astype(BF16)
        o[1][...] = (acc * gb).astype(BF16)
        dga = acc * pa_v * ga * (1.0 - ga)
        dgb = acc * pb_v * gb * (1.0 - gb)
        o[2][0] = dga.astype(BF16)
        o[2][1] = dgb.astype(BF16)
        sa = jnp.sum(dga, axis=0, keepdims=True)
        sb = jnp.sum(dgb, axis=0, keepdims=True)
        first = pids[1] == 0

        @pl.when(first)
        def _():
            o[3][...] = sa
            o[4][...] = sb

        @pl.when(jnp.logical_not(first))
        def _():
            o[3][...] += sa
            o[4][...] += sb

    cj = pl.BlockSpec((TL, 512), lambda j, i, k: (i, j))
    bj = pl.BlockSpec((1, 512), lambda j, i, k: (0, j))
    dpa, dpb, dgate, dba, dbb = _mm(
        "out_proj_dx", (4, T // TL, 1),
        [(dx2b, pl.BlockSpec((TL, D), lambda j, i, k: (i, 0)), wo2, pl.BlockSpec((512, D), lambda j, i, k: (j, 0)), "nt", None)],
        [(gates, cj), (gates, pl.BlockSpec((TL, 512), lambda j, i, k: (i, 4 + j))), (pa, cj), (pb, cj)],
        [(_sds((T, D), BF16), cj)] * 2 + [(_sds((2, T, D), BF16), pl.BlockSpec((2, TL, 512), lambda j, i, k: (0, i, j)))]
        + [(_sds((1, D), F32), bj)] * 2, dmerge_epi, None, ("parallel", "arbitrary", "arbitrary"))

    def store_shards(acc, ex, o, pids):
        for s in range(4):
            o[0][s] = acc[:, s * 256:(s + 1) * 256].astype(BF16)

    def proj_dw(name, y, dp):
        (dw,) = _mm(name, (2, 1, 1), [(y, tall(512), dp, tall_j(1024), "tn", None)],
                    [], [(_sds((N_DEV, 512, 256), BF16), pl.BlockSpec((4, 512, 256), lambda j, i, k: (j, 0, 0)))],
                    store_shards, None, PP)
        return dw

    def proj_dx(name, dp, w):
        (dy,) = _mm(name, (nt, 1, 1),
                    [(dp, full_rows(TM), w, pl.BlockSpec((N_DEV, 512, 256), lambda i, j, k: (0, 0, 0)), "nt_shards", None)],
                    [], [(_sds((T, 512), BF16), full_rows(TM, 512))], store_bf16, None, PP)
        return dy

    def in_dw(name, dact, dact_spec, n):
        def store_pair(acc, ex, o, pids):
            o[0][0] = acc[:, :n].astype(BF16)
            o[0][1] = acc[:, n:].astype(BF16)

        (dw,) = _mm(name, (N_DEV // 2, D // 512, 1), [(h1, tall(512), dact, dact_spec, "tn", None)],
                    [], [(_sds((N_DEV, D, n), BF16), pl.BlockSpec((2, 512, n), lambda j, i, k: (j, i, 0)))],
                    store_pair, None, PP)
        return dw

    dwg = in_dw("gate_proj_dw", dgate, pl.BlockSpec((None, T, 1024), lambda j, i, k: (j // 2, 0, j % 2)), 512)
    dwpa = proj_dw("proj_a_dw", ya, dpa)
    dwpb = proj_dw("proj_b_dw", yb, dpb)
    token = send("proj", dict(w_gate=dwg, w_out=dwo.reshape(N_DEV, D // N_DEV, D), w_proj_a=dwpa, w_proj_b=dwpb))
    dya = proj_dx("proj_a_dx", dpa, wpa)
    dyb = proj_dx("proj_b_dx", dpb, wpb)

    dq_a, dk_a, dv_a = _dilated_bwd(qkv, last_src, _behind(slopes, token), dya, ya, lse_a)
    dq_b, dk_b, dv_b, dtiles = _attn_b_bwd(qkv, tiles, dyb, yb, lse_b)
    drpb = _rpb_reduce(dtiles.reshape(N_HEADS_B * N_DR, GRID_W * GRID_W), sel)[:, :N_DC]
    dqkv = jnp.concatenate([*dq_a, dq_b, *dk_a, dk_b, *dv_a, dv_b], axis=1)
    dwq = in_dw("qkv_proj_dw", dqkv, tall_j(1536), 768)
    half_a = lambda h: pl.BlockSpec((None, TM, D), lambda j, i, k: (h, i, 0))
    half_w = lambda h: pl.BlockSpec((4, 512, 512), lambda j, i, k: (h, j, 0))
    token = send("in", dict(w_qkv=dwq))
    riders = riders(dwq if token is None else token, 4 * nt, lambda j, i, k: j * nt + i) if riders else []
    lead = behind(token)

    def store_and_ride(acc, ex, o, pids):
        o[0][...] = acc.astype(BF16)
        for r, (_, _, run) in enumerate(riders):
            run(ex[len(lead) + 4 * r:len(lead) + 4 * r + 4], o[1 + 4 * r:5 + 4 * r])

    dh1, *rode = _mm("in_proj_dx", (D // 512, nt, 1),
                     [(dqkv, a_ji(3 * D), wq, w_ji(768), "nt_shards", None),
                      (dgate, half_a(0), wg, half_w(0), "nt_shards", None), (dgate, half_a(1), wg, half_w(1), "nt_shards", None)],
                     lead + [e for r in riders for e in r[0]],
                     [(_sds((T, D), BF16), out_ji)] + [o for r in riders for o in r[1]], store_and_ride, None, PP)
    grad_x, dg1 = _rmsnorm_bwd("rmsnorm_mix_bwd", dh1, x, r1, norm_mix, dx2, False)

    small = dict(norm_mix=dg1, b_gate=jnp.concatenate([dba, dbb], axis=1), rpb=drpb, norm_mlp=dg2, norm_final=dg3)
    return loss, grad_x, small, [rode[4 * r:4 * r + 4] for r in range(len(riders))]


BIG = ("w_qkv", "w_gate", "w_proj_a", "w_proj_b", "w_out", "w_up", "w_down")
SMALL = ("norm_mix", "b_gate", "rpb", "norm_mlp", "norm_final")
SMALL_LANES = 96 * 128
RIDING = ("w_up", "w_down")


def _place():
    return lax.axis_index("x"), lax.axis_index("y"), lax.axis_index("c")


def _peer(x, y, c, mask):
    return x ^ (mask >> 2), y ^ ((mask >> 1) & 1), c ^ (mask & 1)


_HBM = pl.BlockSpec(memory_space=pltpu.HBM)
_SEM = pl.BlockSpec(memory_space=pltpu.SEMAPHORE)
_ANY = pl.BlockSpec(memory_space=pl.ANY)
_EFFECT = pltpu.SideEffectType.DATAFLOW_SIDE_EFFECTING


def _hbm(a):
    return pltpu.with_memory_space_constraint(a, pltpu.HBM)


def _other_chips(x, y):
    return [(1 - x, y), (x, 1 - y), (1 - x, 1 - y)]


def _block(x, y, c):
    return 4 * x + 2 * y + c


def _rdma(src, dst, send_sem, recv_sem, to):
    return pltpu.make_async_remote_copy(src_ref=src, dst_ref=dst, send_sem=send_sem, recv_sem=recv_sem, device_id=to,
                                        device_id_type=MESH)


def _gather_start(name, shards):
    n = len(shards)

    def body(*refs):
        srcs, lands = refs[:n], refs[n:2 * n]
        outs = refs[2 * n:]
        send, recv_d2d, recv_ici, local = outs[:n], outs[n:2 * n], outs[2 * n:3 * n], outs[3 * n:4 * n]
        token_ref = outs[6 * n]
        x, y, c = _place()
        for a in range(n):
            mine = lands[a].at[_block(x, y, c)]
            _rdma(srcs[a], mine, send[a].at[0], recv_d2d[a].at[0], (x, y, 1 - c)).start()
            for j, chip in enumerate(_other_chips(x, y)):
                _rdma(srcs[a], mine, send[a].at[1 + j], recv_ici[a].at[j], (*chip, c)).start()
        for a in range(n):
            pltpu.make_async_copy(srcs[a], lands[a].at[_block(x, y, c)], local[a].at[0]).start()
        token_ref[...] = jnp.zeros_like(token_ref)

    lands = [_hbm(lax.empty((N_DEV,) + s.shape, s.dtype)) for s in shards]
    srcs = [_hbm(s) for s in shards]
    dma = pltpu.SemaphoreType.DMA
    res = pl.pallas_call(
        body, name=name,
        out_shape=[dma((4,))] * n + [dma((1,))] * n + [dma((3,))] * n + [dma((1,))] * n
        + [pltpu.HBM(s.shape, s.dtype) for s in srcs] + [pltpu.HBM(l.shape, l.dtype) for l in lands]
        + [jax.ShapeDtypeStruct((8, 128), F32)],
        in_specs=[_HBM] * (2 * n), out_specs=[_SEM] * (4 * n) + [_HBM] * (2 * n) + [pl.BlockSpec(memory_space=pltpu.VMEM)],
        input_output_aliases={i: 4 * n + i for i in range(2 * n)},
        compiler_params=pltpu.CompilerParams(has_side_effects=_EFFECT))(*srcs, *lands)
    keys = ("send", "recv_d2d", "recv_ici", "local", "src", "land")
    return [dict(zip(keys, (res[k * n + a] for k in range(6)))) for a in range(n)], res[6 * n]


def _gather_forward(name, handles, after):
    n = len(handles)
    after = after if isinstance(after, (tuple, list)) else (after,)

    def body(*refs):
        lands, recv_ici = refs[:n], refs[n:2 * n]
        outs = refs[2 * n + len(after):]
        fwd_send, fwd_recv, token_ref = outs[n:2 * n], outs[2 * n:3 * n], outs[3 * n]
        x, y, c = _place()
        for a in range(n):
            for j, chip in enumerate(_other_chips(x, y)):
                blk = lands[a].at[_block(*chip, c)]
                _rdma(blk, blk, fwd_send[a].at[j], recv_ici[a].at[j], (*chip, c)).wait_recv()
                _rdma(blk, blk, fwd_send[a].at[j], fwd_recv[a].at[j], (x, y, 1 - c)).start()
        token_ref[...] = jnp.zeros_like(token_ref)

    dma = pltpu.SemaphoreType.DMA
    lands = [h["land"] for h in handles]
    res = pl.pallas_call(
        body, name=name,
        out_shape=[pltpu.HBM(l.shape, l.dtype) for l in lands] + [dma((3,))] * (2 * n) + [jax.ShapeDtypeStruct((8, 128), F32)],
        in_specs=[_HBM] * n + [_SEM] * n + [_ANY] * len(after),
        out_specs=[_HBM] * n + [_SEM] * (2 * n) + [pl.BlockSpec(memory_space=pltpu.VMEM)],
        input_output_aliases={i: i for i in range(n)},
        compiler_params=pltpu.CompilerParams(has_side_effects=_EFFECT))(*lands, *[h["recv_ici"] for h in handles], *after)
    return [dict(h, land=res[a], fwd_send=res[n + a], fwd_recv=res[2 * n + a]) for a, h in enumerate(handles)], res[3 * n]


def _gather_wait(name, handles, after):
    n = len(handles)

    def body(*refs):
        srcs, lands = refs[:n], refs[n:2 * n]
        send, recv_d2d, local, fwd_send, fwd_recv = (refs[(2 + k) * n:(3 + k) * n] for k in range(5))
        x, y, c = _place()
        sibling = (x, y, 1 - c)
        for a in range(n):
            mine = lands[a].at[_block(x, y, c)]
            _rdma(srcs[a], mine, send[a].at[0], recv_d2d[a].at[0], sibling).wait_send()
            _rdma(srcs[a], lands[a].at[_block(*sibling)], send[a].at[0], recv_d2d[a].at[0], sibling).wait_recv()
            pltpu.make_async_copy(srcs[a], mine, local[a].at[0]).wait()
            for j, chip in enumerate(_other_chips(x, y)):
                _rdma(srcs[a], mine, send[a].at[1 + j], fwd_recv[a].at[j], (*chip, c)).wait_send()
                blk = lands[a].at[_block(*chip, c)]
                _rdma(blk, blk, fwd_send[a].at[j], fwd_recv[a].at[j], sibling).wait_send()
                got = lands[a].at[_block(*chip, 1 - c)]
                _rdma(got, got, fwd_send[a].at[j], fwd_recv[a].at[j], sibling).wait_recv()

    srcs = [h["src"] for h in handles]
    lands = [h["land"] for h in handles]
    sems = [h[k] for k in ("send", "recv_d2d", "local", "fwd_send", "fwd_recv") for h in handles]
    res = pl.pallas_call(
        body, name=name, out_shape=[pltpu.HBM(s.shape, s.dtype) for s in srcs] + [pltpu.HBM(l.shape, l.dtype) for l in lands],
        in_specs=[_HBM] * (2 * n) + [_SEM] * (5 * n) + [_ANY], out_specs=[_HBM] * (2 * n),
        input_output_aliases={i: i for i in range(2 * n)},
        compiler_params=pltpu.CompilerParams(has_side_effects=_EFFECT))(*srcs, *lands, *sems, after)
    return list(res[n:])


def _gather_small(small, after):
    def body(in_ref, *refs):
        out_ref, send_sems, recv_sems, local_sem = refs[len(after):]
        x, y, c = _place()
        me = 4 * x + 2 * y + c
        copies = [pltpu.make_async_copy(in_ref, out_ref.at[me], local_sem)]
        for mask in range(1, N_DEV):
            copies.append(pltpu.make_async_remote_copy(
                src_ref=in_ref, dst_ref=out_ref.at[me], send_sem=send_sems.at[mask - 1],
                recv_sem=recv_sems.at[mask - 1], device_id=_peer(x, y, c, mask), device_id_type=MESH))
        for cp in copies:
            cp.start()
        for cp in copies:
            cp.wait()

    any_spec = pl.BlockSpec(memory_space=pl.ANY)
    return pl.pallas_call(
        body, name="gather_small", in_specs=[any_spec] * (1 + len(after)), out_specs=any_spec,
        out_shape=jax.ShapeDtypeStruct((N_DEV,) + small.shape, small.dtype),
        scratch_shapes=[pltpu.SemaphoreType.DMA((7,)), pltpu.SemaphoreType.DMA((7,)), pltpu.SemaphoreType.DMA])(small, *after)


def _send_copy(src_refs, land_refs, send_sems, recv_sems, a, mask, scatter, x, y, c):
    px, py, pc = _peer(x, y, c, mask)
    src = src_refs[a].at[4 * px + 2 * py + pc] if scatter else src_refs[a]
    return pltpu.make_async_remote_copy(
        src_ref=src, dst_ref=land_refs[a].at[4 * x + 2 * y + c], send_sem=send_sems[a].at[mask - 1],
        recv_sem=recv_sems[a].at[mask - 1], device_id=(px, py, pc), device_id_type=MESH)


def _local_copy(src_refs, land_refs, local_sems, a, scatter, x, y, c):
    me = 4 * x + 2 * y + c
    return pltpu.make_async_copy(src_refs[a].at[me] if scatter else src_refs[a], land_refs[a].at[me], local_sems[a].at[0])


def _send_start(name, srcs, scatter, after=None):
    n = len(srcs)
    n_in = 2 * n + (after is not None)
    shapes = [s.shape[1:] if scatter else s.shape for s in srcs]

    def body(*refs):
        src_refs, land_refs = refs[:n], refs[n:2 * n]
        outs = refs[n_in:]
        send_sems, recv_sems, local_sems = outs[:n], outs[n:2 * n], outs[2 * n:3 * n]
        token_ref = outs[5 * n]
        x, y, c = _place()
        for a in range(n):
            for mask in range(1, N_DEV):
                _send_copy(src_refs, land_refs, send_sems, recv_sems, a, mask, scatter, x, y, c).start()
        for a in range(n):
            _local_copy(src_refs, land_refs, local_sems, a, scatter, x, y, c).start()
        token_ref[...] = jnp.zeros_like(token_ref)

    lands = [pltpu.with_memory_space_constraint(lax.empty((N_DEV,) + sh, s.dtype), pltpu.HBM) for sh, s in zip(shapes, srcs)]
    srcs = [pltpu.with_memory_space_constraint(s, pltpu.HBM) for s in srcs]
    res = pl.pallas_call(
        body, name=name,
        out_shape=[pltpu.SemaphoreType.DMA((7,))] * (2 * n) + [pltpu.SemaphoreType.DMA((1,))] * n
        + [pltpu.HBM(s.shape, s.dtype) for s in srcs] + [pltpu.HBM(l.shape, l.dtype) for l in lands]
        + [jax.ShapeDtypeStruct((8, 128), F32)],
        in_specs=[_HBM] * (2 * n) + [pl.BlockSpec(memory_space=pl.ANY)] * (n_in - 2 * n),
        out_specs=[_SEM] * (3 * n) + [_HBM] * (2 * n) + [pl.BlockSpec(memory_space=pltpu.VMEM)],
        input_output_aliases={i: 3 * n + i for i in range(2 * n)},
        compiler_params=pltpu.CompilerParams(has_side_effects=_EFFECT))(*srcs, *lands, *([] if after is None else [after]))
    handles = [tuple(res[k * n + a] for k in range(5)) for a in range(n)]
    return handles, res[5 * n]


def _send_wait(name, handles, scatter, after):
    n = len(handles)

    def body(*refs):
        src_refs, land_refs = refs[:n], refs[n:2 * n]
        send_sems, recv_sems, local_sems = refs[2 * n:3 * n], refs[3 * n:4 * n], refs[4 * n:5 * n]
        x, y, c = _place()
        for a in range(n):
            for mask in range(1, N_DEV):
                cp = _send_copy(src_refs, land_refs, send_sems, recv_sems, a, mask, scatter, x, y, c)
                cp.wait_send()
                cp.wait_recv()
            _local_copy(src_refs, land_refs, local_sems, a, scatter, x, y, c).wait()

    srcs = [h[3] for h in handles]
    lands = [h[4] for h in handles]
    res = pl.pallas_call(
        body, name=name, out_shape=[pltpu.HBM(s.shape, s.dtype) for s in srcs] + [pltpu.HBM(l.shape, l.dtype) for l in lands],
        in_specs=[_HBM] * (2 * n) + [_SEM] * (3 * n) + [pl.BlockSpec(memory_space=pl.ANY)], out_specs=[_HBM] * (2 * n),
        input_output_aliases={i: i for i in range(2 * n)},
        compiler_params=pltpu.CompilerParams(has_side_effects=_EFFECT))(
            *srcs, *lands, *[h[0] for h in handles], *[h[1] for h in handles], *[h[2] for h in handles], after)
    return list(res[n:])


def _adamw_math(g, w, m, v):
    nm = ADAM_B1 * m + (1.0 - ADAM_B1) * g
    nv = ADAM_B2 * v + (1.0 - ADAM_B2) * jnp.square(g)
    m_hat = nm / (1.0 - ADAM_B1 ** ADAM_STEP)
    v_hat = nv / (1.0 - ADAM_B2 ** ADAM_STEP)
    return -ADAM_LR * (m_hat / (jnp.sqrt(v_hat) + ADAM_EPS) + ADAM_WD * w), nm, nv


def _adamw_small(gathered, ws, ms, vs):
    n = len(ws)
    widths = [w.shape[1] for w in ws]

    def body(p_ref, *refs):
        ins, outs = refs[:3 * n], refs[3 * n:]
        off = 0
        for a in range(n):
            g = p_ref[0, :, off:off + widths[a]]
            for i in range(1, N_DEV):
                g = g + p_ref[i, :, off:off + widths[a]]
            outs[4 * a][...] = g
            outs[4 * a + 1][...], outs[4 * a + 2][...], outs[4 * a + 3][...] = _adamw_math(
                g, ins[a][...], ins[n + a][...], ins[2 * n + a][...])
            off += widths[a]

    whole = lambda shape: pl.BlockSpec(shape, lambda i: (0,) * len(shape))
    rows = [whole((1, wd)) for wd in widths]
    res = _pc(body, name="adamw_small", grid=(1,), in_specs=[whole(gathered.shape)] + rows * 3,
              out_specs=[r for r in rows for _ in range(4)],
              out_shape=[jax.ShapeDtypeStruct((1, wd), F32) for wd in widths for _ in range(4)],
              sem=("arbitrary",))(gathered, *ws, *ms, *vs)
    return [res[4 * a:4 * a + 4] for a in range(n)]


def _adamw(name, parts, w, m, v):
    K, n = w.shape
    tr = min(K, 128)

    def body(p_ref, w_ref, m_ref, v_ref, g_ref, d_ref, nm_ref, nv_ref):
        g = p_ref[0].astype(F32)
        for i in range(1, N_DEV):
            g = g + p_ref[i].astype(F32)
        g_ref[...] = g
        d_ref[...], nm_ref[...], nv_ref[...] = _adamw_math(g, w_ref[...], m_ref[...], v_ref[...])

    blk = pl.BlockSpec((tr, n), lambda i: (i, 0))
    return _pc(body, name=name, grid=(K // tr,), in_specs=[pl.BlockSpec((N_DEV, tr, n), lambda i: (0, i, 0)), blk, blk, blk],
               out_specs=[blk] * 4, out_shape=[jax.ShapeDtypeStruct((K, n), F32)] * 4, sem=("parallel",))(parts, w, m, v)


def _adamw_rider(parts, w, m, v, steps, step_of):
    K, n = w.shape
    tr = K // steps
    blk = pl.BlockSpec((tr, n), lambda *ids: (step_of(*ids), 0))
    extras = [(parts, pl.BlockSpec((N_DEV, tr, n), lambda *ids: (0, step_of(*ids), 0))), (w, blk), (m, blk), (v, blk)]

    def run(ex, o):
        g = ex[0][0].astype(F32)
        for i in range(1, N_DEV):
            g = g + ex[0][i].astype(F32)
        o[0][...] = g
        o[1][...], o[2][...], o[3][...] = _adamw_math(g, ex[1][...], ex[2][...], ex[3][...])

    return extras, [(jax.ShapeDtypeStruct((K, n), F32), blk)] * 4, run


def _lane_padded(size):
    return -(-size // 128) * 128


def _pack_small(vals):
    rows = [jnp.pad(v.reshape(1, -1), ((0, 0), (0, _lane_padded(v.size) - v.size))) for v in vals]
    used = sum(r.shape[1] for r in rows)
    return jnp.concatenate(rows + [jnp.zeros((1, SMALL_LANES - used), F32)], axis=1)


def kernel(x, norm_mix, w_qkv, w_gate, b_gate, rpb, w_proj_a, w_proj_b, w_out, norm_mlp, w_up, w_down, norm_final, loss_target, m_norm_mix, m_w_qkv, m_w_gate, m_b_gate, m_rpb, m_w_proj_a, m_w_proj_b, m_w_out, m_norm_mlp, m_w_up, m_w_down, m_norm_final, v_norm_mix, v_w_qkv, v_w_gate, v_b_gate, v_rpb, v_w_proj_a, v_w_proj_b, v_w_out, v_norm_mlp, v_w_up, v_w_down, v_norm_final):
    w = dict(norm_mix=norm_mix, w_qkv=w_qkv, w_gate=w_gate, b_gate=b_gate, rpb=rpb, w_proj_a=w_proj_a, w_proj_b=w_proj_b,
             w_out=w_out, norm_mlp=norm_mlp, w_up=w_up, w_down=w_down, norm_final=norm_final)
    m = dict(norm_mix=m_norm_mix, w_qkv=m_w_qkv, w_gate=m_w_gate, b_gate=m_b_gate, rpb=m_rpb, w_proj_a=m_w_proj_a,
             w_proj_b=m_w_proj_b, w_out=m_w_out, norm_mlp=m_norm_mlp, w_up=m_w_up, w_down=m_w_down, norm_final=m_norm_final)
    v = dict(norm_mix=v_norm_mix, w_qkv=v_w_qkv, w_gate=v_w_gate, b_gate=v_b_gate, rpb=v_rpb, w_proj_a=v_w_proj_a,
             w_proj_b=v_w_proj_b, w_out=v_w_out, norm_mlp=v_norm_mlp, w_up=v_w_up, w_down=v_w_down, norm_final=v_norm_final)
    order = ("norm_mix", "w_qkv", "w_gate", "b_gate", "rpb", "w_proj_a", "w_proj_b", "w_out", "norm_mlp", "w_up",
             "w_down", "norm_final")

    stages = dict(qkv=("w_qkv",), gate=("w_gate", "w_proj_a", "w_proj_b", "w_out"), mlp=("w_up", "w_down"))
    first, token = _gather_start("gather_start_qkv", [w["w_qkv"][0].astype(BF16)])
    names = [n for s in ("gate", "mlp") for n in stages[s]]
    rest, token = _gather_start("gather_start", [_behind(w[n][0], token).astype(BF16) for n in names])
    handles = dict(zip(names, rest), w_qkv=first[0])

    def late_weights(stage, after):
        if stage == "qkv":
            forwarded, _ = _gather_forward("gather_forward_qkv", [handles["w_qkv"]], after)
            return tuple(_gather_wait("gather_wait_qkv", forwarded, after[0]))
        if stage == "attention":
            handles["gate"], tok = _gather_forward("gather_forward_gate", [handles[n] for n in stages["gate"]], after)
            return (tok,)
        if stage == "mixer":
            handles["up"], tok = _gather_forward("gather_forward_up", [handles["w_up"]], after)
            return tuple(_gather_wait("gather_wait_gate", handles["gate"], after)) + (tok,)
        if stage == "residual":
            handles["down"], tok = _gather_forward("gather_forward_down", [handles["w_down"]], after)
            return (tok,)
        return tuple(_gather_wait("gather_wait_" + stage, handles[stage], after))

    sent = {}

    def send(stage, grads):
        handles, tok = _send_start("grads_start_" + stage, list(grads.values()), True)
        sent[stage] = (tuple(grads), handles)
        return tok

    waited = {}

    def riders(after, steps, step_of):
        names, handles = sent["mlp"]
        waited["mlp"] = dict(zip(names, _send_wait("grads_wait_mlp", handles, True, after)))
        return [_adamw_rider(waited["mlp"][n], w[n][0], m[n][0], v[n][0], steps, step_of) for n in RIDING]

    loss, grad_x, small, rode = _local_step(x[0], loss_target[0], _behind(norm_mix, token), b_gate, rpb[0], norm_mlp,
                                            norm_final, late_weights, send, riders)

    grad, delta, new_m, new_v = {}, {}, {}, {}
    for n, res in zip(RIDING, rode):
        grad[n], delta[n], new_m[n], new_v[n] = (r[None] for r in res)
    after = grad_x
    for stage in ("mlp", "proj", "small", "in"):
        if stage == "small":
            all_small = after = _gather_small(_pack_small([small[n] for n in SMALL]), [d[0] for d in delta.values()])
            continue
        names, handles = sent[stage]
        landed = waited[stage] if stage in waited else dict(zip(names, _send_wait("grads_wait_" + stage, handles, True, after)))
        for n, parts in landed.items():
            if n in RIDING:
                continue
            res = _adamw("adamw_" + n, parts, w[n][0], m[n][0], v[n][0])
            grad[n], delta[n], new_m[n], new_v[n] = (r[None] for r in res)
            after = res[1]
    as_row = lambda a: jnp.pad(a.reshape(1, -1), ((0, 0), (0, _lane_padded(a.size) - a.size)))
    res = _adamw_small(all_small, *[[as_row(t[n]) for n in SMALL] for t in (w, m, v)])
    for n, rows in zip(SMALL, res):
        for tree, row in zip((grad, delta, new_m, new_v), rows):
            tree[n] = row[0, :w[n].size].reshape(w[n].shape)

    local_loss, _, _ = lax.optimization_barrier((loss[0, 0], res[0][1], delta["w_qkv"]))
    total = lax.psum(local_loss, ("x", "y", "c"))
    return (total, grad_x[None], *[grad[n] for n in order], *[delta[n] for n in order],
            *[new_m[n] for n in order], *[new_v[n] for n in order])
```

```python
import numpy as np
import jax
import jax.numpy as jnp
from jax import lax
from jax.experimental import pallas as pl
from jax.experimental.pallas import tpu as pltpu

F32 = jnp.float32
BF16 = jnp.bfloat16
MESH = pl.DeviceIdType.MESH

D_MODEL = 2048
HEAD_DIM = 128
N_HEADS = 16
N_HEADS_A = 12
N_HEADS_B = 4
DILATIONS = (1, 4, 16)
HALF_WINDOW = 64
GRID_W = 64
NA_ROWS = 8
NA_COLS = 16
D_FF = 4 * D_MODEL
N_DEV = 8
EPS = 1e-6
NEG = -1e30
SCALE = HEAD_DIM ** -0.5
ADAM_LR, ADAM_B1, ADAM_B2, ADAM_EPS, ADAM_WD, ADAM_STEP = 0.001, 0.9, 0.999, 1e-08, 0.01, 10

VMEM_LIMIT = 56 * 1024 * 1024
TM = 512
TL = 1024
TS = 256
TE = 512
BQ_A = 128


def _pc(body, *, name, grid, in_specs, out_specs, out_shape, scratch=(), sem=None):
    return pl.pallas_call(
        body, name=name, grid=grid, in_specs=in_specs, out_specs=out_specs, out_shape=out_shape,
        scratch_shapes=list(scratch),
        compiler_params=pltpu.CompilerParams(dimension_semantics=sem, vmem_limit_bytes=VMEM_LIMIT))


_DOT_DIMS = {"nn": (((1,), (0,)), ((), ())), "nt": (((1,), (1,)), ((), ())), "tn": (((0,), (0,)), ((), ()))}


def _dot(a, b, mode, precision=None):
    if mode == "nn_shards":
        return jnp.concatenate([_dot(a, b[s], "nn") for s in range(b.shape[0])], axis=1)
    if mode == "nt_shards":
        n = b.shape[2]
        out = _dot(a[:, :n], b[0], "nt")
        for s in range(1, b.shape[0]):
            out = out + _dot(a[:, s * n:(s + 1) * n], b[s], "nt")
        return out
    return lax.dot_general(a, b, _DOT_DIMS[mode], preferred_element_type=F32, precision=precision)


def _mm(name, grid, pairs, extras, outs, epilogue, acc_shape, sem):
    n_pairs, n_extra, n_out = len(pairs), len(extras), len(outs)
    k_axis = len(grid) - 1
    nk = grid[k_axis]

    def body(*refs):
        pids = [pl.program_id(ax) for ax in range(len(grid))]
        ab = refs[:2 * n_pairs]
        ex = refs[2 * n_pairs:2 * n_pairs + n_extra]
        o = refs[2 * n_pairs + n_extra:2 * n_pairs + n_extra + n_out]
        part = None
        for p, pair in enumerate(pairs):
            a = ab[2 * p][...]
            if pair[5] is not None:
                a = pair[5](a)
            d = _dot(a, ab[2 * p + 1][...], pair[4])
            part = d if part is None else part + d
        if nk == 1:
            epilogue(part, ex, o, pids)
        else:
            acc = refs[-1]
            k = pids[k_axis]

            @pl.when(k == 0)
            def _():
                acc[...] = part

            @pl.when(k > 0)
            def _():
                acc[...] += part

            @pl.when(k == nk - 1)
            def _():
                epilogue(acc[...], ex, o, pids)

    operands, in_specs = [], []
    for a, a_spec, b, b_spec, _, _ in pairs:
        operands += [a, b]
        in_specs += [a_spec, b_spec]
    for e, e_spec in extras:
        operands.append(e)
        in_specs.append(e_spec)
    res = _pc(body, name=name, grid=grid, in_specs=in_specs, out_specs=[s for _, s in outs],
              out_shape=[o for o, _ in outs], scratch=[pltpu.VMEM(acc_shape, F32)] if nk > 1 else [], sem=sem)(*operands)
    return res


def _rmsnorm_fwd(x, g, name):
    T, D = x.shape

    def body(x_ref, g_ref, h_ref, r_ref):
        xv = x_ref[...]
        r = lax.rsqrt(jnp.mean(xv * xv, axis=-1, keepdims=True) + EPS)
        h_ref[...] = (xv * r * g_ref[...]).astype(BF16)
        r_ref[...] = r

    return _pc(body, name=name, grid=(T // TM,),
               in_specs=[pl.BlockSpec((TM, D), lambda i: (i, 0)), pl.BlockSpec((1, D), lambda i: (0, 0))],
               out_specs=[pl.BlockSpec((TM, D), lambda i: (i, 0)), pl.BlockSpec((TM, 1), lambda i: (i, 0))],
               out_shape=[jax.ShapeDtypeStruct((T, D), BF16), jax.ShapeDtypeStruct((T, 1), F32)],
               sem=("parallel",))(x, g)


def _rmsnorm_bwd_block(dh, x, r, g, dres):
    xh = x * r
    dxh = dh * g
    dx = dres + r * (dxh - xh * jnp.mean(dxh * xh, axis=-1, keepdims=True))
    return dx, jnp.sum(dh * xh, axis=0, keepdims=True)


def _rmsnorm_bwd(name, dh, x, r, g, dres, want_bf16):
    T, D = x.shape

    def body(dh_ref, x_ref, r_ref, g_ref, dres_ref, *outs):
        dx, dgp = _rmsnorm_bwd_block(dh_ref[...].astype(F32), x_ref[...], r_ref[...], g_ref[...], dres_ref[...])
        outs[0][...] = dx
        if want_bf16:
            outs[1][...] = dx.astype(BF16)
        first = pl.program_id(0) == 0

        @pl.when(first)
        def _():
            outs[-1][...] = dgp

        @pl.when(jnp.logical_not(first))
        def _():
            outs[-1][...] += dgp

    row = pl.BlockSpec((TE, D), lambda i: (i, 0))
    vec = pl.BlockSpec((1, D), lambda i: (0, 0))
    n_row = 2 if want_bf16 else 1
    return _pc(body, name=name, grid=(T // TE,), in_specs=[row, row, pl.BlockSpec((TE, 1), lambda i: (i, 0)), vec, row],
               out_specs=[row] * n_row + [vec],
               out_shape=[jax.ShapeDtypeStruct((T, D), F32)] + [jax.ShapeDtypeStruct((T, D), BF16)] * (n_row - 1)
               + [jax.ShapeDtypeStruct((1, D), F32)], sem=("arbitrary",))(dh, x, r, g, dres)


def _final_norm_loss(x3, g, tgt):
    T, D = x3.shape

    def body(x_ref, g_ref, t_ref, loss_ref, dx_ref, dxb_ref, dg_ref):
        i = pl.program_id(0)
        xv = x_ref[...]
        gv = g_ref[...]
        r = lax.rsqrt(jnp.mean(xv * xv, axis=-1, keepdims=True) + EPS)
        xh = xv * r
        e = xh * gv - t_ref[...]
        part = 0.5 * jnp.sum(jnp.mean(e * e, axis=-1, keepdims=True))
        dy = e * (1.0 / D)
        dxh = dy * gv
        dx = r * (dxh - xh * jnp.mean(dxh * xh, axis=-1, keepdims=True))
        dx_ref[...] = dx
        dxb_ref[...] = dx.astype(BF16)
        dgp = jnp.sum(dy * xh, axis=0, keepdims=True)

        @pl.when(i == 0)
        def _():
            loss_ref[...] = jnp.zeros(loss_ref.shape, F32) + part
            dg_ref[...] = dgp

        @pl.when(i > 0)
        def _():
            loss_ref[...] += part
            dg_ref[...] += dgp

    row = pl.BlockSpec((TE, D), lambda i: (i, 0))
    vec = pl.BlockSpec((1, D), lambda i: (0, 0))
    return _pc(body, name="final_norm_loss", grid=(T // TE,), in_specs=[row, vec, row],
               out_specs=[pl.BlockSpec((8, 128), lambda i: (0, 0)), row, row, vec],
               out_shape=[jax.ShapeDtypeStruct((8, 128), F32), jax.ShapeDtypeStruct((T, D), F32),
                          jax.ShapeDtypeStruct((T, D), BF16), jax.ShapeDtypeStruct((1, D), F32)],
               sem=("arbitrary",))(x3, g, tgt)


def _alibi_slopes():
    return np.asarray(2.0 ** (-8.0 * np.arange(1, N_HEADS_A + 1) / N_HEADS_A), dtype=np.float32)


def _col_to_row(col):
    return jnp.transpose(jnp.broadcast_to(col, (BQ_A, BQ_A)))[0:1, :]


def _row_to_col(row):
    return jnp.transpose(jnp.broadcast_to(row, (BQ_A, BQ_A)))[:, 0:1]


def _row_spec(rows_per_step, first=0):
    return pl.BlockSpec((None, rows_per_step // BQ_A, BQ_A), lambda j, i: (first + j, i, 0))


def _a_windows(T):
    half = [HALF_WINDOW * d for d in DILATIONS]
    return half, [min(BQ_A + 2 * w, T) for w in half]


def _a_bias(slope, g, offset, wk, half):
    d = DILATIONS[g]
    dlt = offset + lax.broadcasted_iota(jnp.int32, (BQ_A, wk), 1) - lax.broadcasted_iota(jnp.int32, (BQ_A, wk), 0)
    ad = jnp.maximum(dlt, -dlt)
    ok = (ad <= half) & ((ad & (d - 1)) == 0)
    return jnp.where(ok, -slope * ad.astype(F32), NEG)


def _a_scores(q, kw, slope, g, offset, wk, half):
    return _dot(q, kw, "nt") * SCALE + _a_bias(slope, g, offset, wk, half)


N_JOINT = 2


SUB_J = 8
BS_J = SUB_J * BQ_A


def _a_qkv_specs(T):
    specs = []
    for g in range(N_JOINT):
        specs.append(pl.BlockSpec((BS_J, HEAD_DIM), lambda j, i, g=g: (i, 4 * g + j)))
        specs.append(pl.BlockSpec((T, HEAD_DIM), lambda j, i, g=g: (0, N_HEADS + 4 * g + j)))
        specs.append(pl.BlockSpec((T, HEAD_DIM), lambda j, i, g=g: (0, 2 * N_HEADS + 4 * g + j)))
    return specs


def _a_window(step, u, g, T, half, wks):
    t0 = (step * SUB_J + u) * BQ_A
    ws = pl.multiple_of(jnp.clip(t0 - half[g], 0, T - wks[g]), 64)
    return pl.ds(ws, wks[g]), ws - t0


def _attn_a_fwd(qkv, slopes, o_last, lse_last):
    T = qkv.shape[0]
    half, wks = _a_windows(T)

    def body(sl_ref, *refs):
        qkv_refs, (o_ref, l_ref, y_ref, lse_ref) = refs[:3 * N_JOINT], refs[3 * N_JOINT:]
        j = pl.program_id(0)
        for u in range(SUB_J):
            rows = slice(u * BQ_A, (u + 1) * BQ_A)
            ss, vws = [], []
            lse_last = _row_to_col(l_ref[u:u + 1, :])
            m = lse_last
            for g in range(N_JOINT):
                q_ref, k_ref, v_ref = qkv_refs[3 * g:3 * g + 3]
                win, offset = _a_window(pl.program_id(1), u, g, T, half, wks)
                ss.append(_a_scores(q_ref[rows, :], k_ref[win, :], sl_ref[4 * g + j], g, offset, wks[g], half[g]))
                vws.append(v_ref[win, :])
                m = jnp.maximum(m, jnp.max(ss[g], axis=-1, keepdims=True))
            den = jnp.exp(lse_last - m)
            acc = den * o_ref[rows, :].astype(F32)
            for g in range(N_JOINT):
                p = jnp.exp(ss[g] - m)
                den = den + jnp.sum(p, axis=-1, keepdims=True)
                acc = acc + _dot(p.astype(BF16), vws[g], "nn")
            y_ref[rows, :] = (acc / den).astype(BF16)
            lse_ref[u:u + 1, :] = _col_to_row(m + jnp.log(den))

    blk = pl.BlockSpec((BS_J, HEAD_DIM), lambda j, i: (i, j))
    col = _row_spec(BS_J)
    return _pc(body, name="attn_a_fwd", grid=(4, T // BS_J),
               in_specs=[pl.BlockSpec(memory_space=pltpu.SMEM)] + _a_qkv_specs(T) + [blk, col], out_specs=[blk, col],
               out_shape=[jax.ShapeDtypeStruct((T, 4 * HEAD_DIM), BF16), jax.ShapeDtypeStruct((4, T // BQ_A, BQ_A), F32)],
               sem=("parallel", "parallel"))(slopes, *([qkv] * (3 * N_JOINT)), o_last, lse_last)


def _attn_a_bwd(qkv, slopes, dy, y, lse):
    T = qkv.shape[0]
    steps = T // BS_J
    half, wks = _a_windows(T)

    def body(sl_ref, *refs):
        qkv_refs = refs[:3 * N_JOINT]
        dy_ref, y_ref, lse_ref, dq_ref, dk_ref, dv_ref, delta_ref, dk_acc, dv_acc = refs[3 * N_JOINT:]
        j = pl.program_id(0)
        step = pl.program_id(1)

        @pl.when(step == 0)
        def _():
            dk_acc[...] = jnp.zeros_like(dk_acc)
            dv_acc[...] = jnp.zeros_like(dv_acc)

        for u in range(SUB_J):
            rows = slice(u * BQ_A, (u + 1) * BQ_A)
            dyv = dy_ref[rows, :]
            delta = jnp.sum(dyv.astype(F32) * y_ref[rows, :].astype(F32), axis=-1, keepdims=True)
            delta_ref[u:u + 1, :] = _col_to_row(delta)
            lse_v = _row_to_col(lse_ref[u:u + 1, :])
            for g in range(N_JOINT):
                q_ref, k_ref, v_ref = qkv_refs[3 * g:3 * g + 3]
                win, offset = _a_window(step, u, g, T, half, wks)
                qv = q_ref[rows, :]
                kw = k_ref[win, :]
                s = _a_scores(qv, kw, sl_ref[4 * g + j], g, offset, wks[g], half[g])
                p = jnp.exp(s - lse_v)
                dp = _dot(dyv, v_ref[win, :], "nt")
                ds = (p * (dp - delta)).astype(BF16)
                dq_ref[g, rows, :] = (_dot(ds, kw, "nn") * SCALE).astype(BF16)
                dk_acc[g, win, :] += _dot(ds, qv, "tn") * SCALE
                dv_acc[g, win, :] += _dot(p.astype(BF16), dyv, "tn")

        @pl.when(step == steps - 1)
        def _():
            dk_ref[...] = dk_acc[...].astype(BF16)
            dv_ref[...] = dv_acc[...].astype(BF16)

    blk = pl.BlockSpec((BS_J, HEAD_DIM), lambda j, i: (i, j))
    col = _row_spec(BS_J)
    whole = pl.BlockSpec((N_JOINT, T, HEAD_DIM), lambda j, i: (0, 0, j))
    full = jax.ShapeDtypeStruct((N_JOINT, T, 4 * HEAD_DIM), BF16)
    return _pc(body, name="attn_a_bwd", grid=(4, steps),
               in_specs=[pl.BlockSpec(memory_space=pltpu.SMEM)] + _a_qkv_specs(T) + [blk, blk, col],
               out_specs=[pl.BlockSpec((N_JOINT, BS_J, HEAD_DIM), lambda j, i: (0, i, j)), whole, whole, col],
               out_shape=[full, full, full, jax.ShapeDtypeStruct((4, T // BQ_A, BQ_A), F32)],
               scratch=[pltpu.VMEM((N_JOINT, T, HEAD_DIM), F32), pltpu.VMEM((N_JOINT, T, HEAD_DIM), F32)],
               sem=("parallel", "arbitrary"))(slopes, *([qkv] * (3 * N_JOINT)), dy, y, lse)


SUB_A = 8


def _to_residue_major(a, d, axis=0):
    if d == 1:
        return a
    sh = a.shape
    a = a.reshape(sh[:axis] + (sh[axis] // d, d) + sh[axis + 1:])
    return jnp.swapaxes(a, axis, axis + 1).reshape(sh)


def _to_token_order(a, d, axis=0):
    if d == 1:
        return a
    sh = a.shape
    a = a.reshape(sh[:axis] + (d, sh[axis] // d) + sh[axis + 1:])
    return jnp.swapaxes(a, axis, axis + 1).reshape(sh)


def _band_bias(slope, offset, wk):
    dlt = offset + lax.broadcasted_iota(jnp.int32, (BQ_A, wk), 1) - lax.broadcasted_iota(jnp.int32, (BQ_A, wk), 0)
    ad = jnp.maximum(dlt, -dlt)
    return jnp.where(ad <= HALF_WINDOW, -slope * ad.astype(F32), NEG)


def _band_window(step, u, seg, wk):
    t0 = (step * SUB_A + u) * BQ_A
    lo = (t0 // seg) * seg
    ws = pl.multiple_of(jnp.clip(t0 - HALF_WINDOW, lo, lo + seg - wk), 64)
    return slice(u * BQ_A, (u + 1) * BQ_A), pl.ds(ws, wk), ws - t0


def _band_specs(T, cols):
    bs = SUB_A * BQ_A
    assert T % bs == 0 and T // max(DILATIONS) >= BQ_A, "a query sub-block must lie inside one segment"
    return [pl.BlockSpec(memory_space=pltpu.SMEM),
            pl.BlockSpec((bs, HEAD_DIM), lambda j, i: (i, cols[0] + j)),
            pl.BlockSpec((T, HEAD_DIM), lambda j, i: (0, cols[1] + j)),
            pl.BlockSpec((T, HEAD_DIM), lambda j, i: (0, cols[2] + j))]


def _banded_fwd(name, src, cols, seg, slopes):
    T = src.shape[0]
    bs = SUB_A * BQ_A
    wk = min(BQ_A + 2 * HALF_WINDOW, seg)

    def body(sl_ref, q_ref, k_ref, v_ref, o_ref, lse_ref):
        slope = sl_ref[pl.program_id(0)]
        for u in range(SUB_A):
            rows, win, offset = _band_window(pl.program_id(1), u, seg, wk)
            s = _dot(q_ref[rows, :], k_ref[win, :], "nt") * SCALE + _band_bias(slope, offset, wk)
            m = jnp.max(s, axis=-1, keepdims=True)
            p = jnp.exp(s - m)
            den = jnp.sum(p, axis=-1, keepdims=True)
            o_ref[rows, :] = (_dot(p.astype(BF16), v_ref[win, :], "nn") / den).astype(BF16)
            lse_ref[u:u + 1, :] = _col_to_row(m + jnp.log(den))

    return _pc(body, name=name, grid=(4, T // bs), in_specs=_band_specs(T, cols),
               out_specs=[pl.BlockSpec((bs, HEAD_DIM), lambda j, i: (i, j)), _row_spec(bs)],
               out_shape=[jax.ShapeDtypeStruct((T, 4 * HEAD_DIM), BF16), jax.ShapeDtypeStruct((4, T // BQ_A, BQ_A), F32)],
               sem=("parallel", "parallel"))(slopes, src, src, src)


def _banded_bwd(name, src, cols, seg, slopes, dy, delta_lse):
    T = src.shape[0]
    bs = SUB_A * BQ_A
    steps = T // bs
    wk = min(BQ_A + 2 * HALF_WINDOW, seg)

    def body(sl_ref, q_ref, k_ref, v_ref, dy_ref, delta_ref, lse_ref, dq_ref, dk_ref, dv_ref, dk_acc, dv_acc):
        slope = sl_ref[pl.program_id(0)]
        step = pl.program_id(1)

        @pl.when(step == 0)
        def _():
            dk_acc[...] = jnp.zeros_like(dk_acc)
            dv_acc[...] = jnp.zeros_like(dv_acc)

        for u in range(SUB_A):
            rows, win, offset = _band_window(step, u, seg, wk)
            qv, kw, dyv = q_ref[rows, :], k_ref[win, :], dy_ref[rows, :]
            s = _dot(qv, kw, "nt") * SCALE + _band_bias(slope, offset, wk)
            p = jnp.exp(s - _row_to_col(lse_ref[u:u + 1, :]))
            ds = (p * (_dot(dyv, v_ref[win, :], "nt") - _row_to_col(delta_ref[u:u + 1, :]))).astype(BF16)
            dq_ref[rows, :] = (_dot(ds, kw, "nn") * SCALE).astype(BF16)
            dk_acc[win, :] += _dot(ds, qv, "tn") * SCALE
            dv_acc[win, :] += _dot(p.astype(BF16), dyv, "tn")

        @pl.when(step == steps - 1)
        def _():
            dk_ref[...] = dk_acc[...].astype(BF16)
            dv_ref[...] = dv_acc[...].astype(BF16)

    blk = pl.BlockSpec((bs, HEAD_DIM), lambda j, i: (i, j))
    whole = pl.BlockSpec((T, HEAD_DIM), lambda j, i: (0, j))
    full = jax.ShapeDtypeStruct((T, 4 * HEAD_DIM), BF16)
    return _pc(body, name=name, grid=(4, steps),
               in_specs=_band_specs(T, cols) + [blk, _row_spec(bs), _row_spec(bs, 4)],
               out_specs=[blk, whole, whole], out_shape=[full, full, full],
               scratch=[pltpu.VMEM((T, HEAD_DIM), F32), pltpu.VMEM((T, HEAD_DIM), F32)],
               sem=("parallel", "arbitrary"))(slopes, src, src, src, dy, delta_lse, delta_lse)


def _last_group_source(qkv):
    third = qkv.shape[1] // 3
    own = jnp.concatenate([qkv[:, t * third + 512 * N_JOINT:t * third + 512 * (N_JOINT + 1)] for t in range(3)], axis=1)
    return _to_residue_major(own, DILATIONS[N_JOINT])


def _dilated_fwd(qkv, slopes):
    d = DILATIONS[N_JOINT]
    last_src = _last_group_source(qkv)
    o, lse = _banded_fwd("attn_a_last_fwd", last_src, (0, 4, 8), qkv.shape[0] // d, slopes[4 * N_JOINT:] * float(d))
    lse = _to_token_order(lse.reshape(4, -1), d, axis=1).reshape(lse.shape)
    y, joint_lse = _attn_a_fwd(qkv, slopes, _to_token_order(o, d), lse)
    return y, joint_lse, last_src


def _dilated_bwd(qkv, last_src, slopes, dy, y, lse):
    d = DILATIONS[N_JOINT]
    dq, dk, dv, delta = _attn_a_bwd(qkv, slopes, dy, y, lse)
    delta_lse = jnp.concatenate([delta, lse], axis=0)
    delta_lse = _to_residue_major(delta_lse.reshape(8, -1), d, axis=1).reshape(delta_lse.shape)
    last = _banded_bwd("attn_a_last_bwd", last_src, (0, 4, 8), qkv.shape[0] // d,
                       slopes[4 * N_JOINT:] * float(d), _to_residue_major(dy, d), delta_lse)
    return [[joint[g] for g in range(N_JOINT)] + [_to_token_order(own, d)] for joint, own in zip((dq, dk, dv), last)]


N_DR = 2 * NA_ROWS - 1
N_DC = 2 * NA_COLS - 1
WK_B = NA_ROWS * GRID_W


def _toeplitz_consts():
    col = np.arange(GRID_W)
    start = np.clip(col - NA_COLS // 2, 0, GRID_W - NA_COLS)
    ok = (col[None, :] >= start[:, None]) & (col[None, :] < start[:, None] + NA_COLS)
    dc = np.clip(col[None, :] - col[:, None], -(NA_COLS - 1), NA_COLS - 1) + NA_COLS - 1
    sel = np.zeros((128, GRID_W * GRID_W), np.float32)
    sel[dc.reshape(-1), np.arange(GRID_W * GRID_W)] = ok.reshape(-1).astype(np.float32)
    mask = np.where(ok, 0.0, NEG).astype(np.float32).reshape(1, -1)
    return sel, mask


def _rpb_expand(rpb_rows, sel, mask):
    def body(r_ref, s_ref, m_ref, o_ref):
        o_ref[...] = _dot(r_ref[...], s_ref[...], "nn", precision=lax.Precision.HIGHEST) + m_ref[...]

    n = rpb_rows.shape[0]
    return _pc(body, name="rpb_expand", grid=(1,),
               in_specs=[pl.BlockSpec((n, 128), lambda i: (0, 0)), pl.BlockSpec(sel.shape, lambda i: (0, 0)),
                         pl.BlockSpec(mask.shape, lambda i: (0, 0))],
               out_specs=pl.BlockSpec((n, sel.shape[1]), lambda i: (0, 0)),
               out_shape=jax.ShapeDtypeStruct((n, sel.shape[1]), F32), sem=("arbitrary",))(rpb_rows, sel, mask)


def _rpb_reduce(dtiles, sel):
    def body(t_ref, s_ref, o_ref):
        o_ref[...] = _dot(t_ref[...], s_ref[...], "nt", precision=lax.Precision.HIGHEST)

    n = dtiles.shape[0]
    return _pc(body, name="rpb_reduce", grid=(1,),
               in_specs=[pl.BlockSpec(dtiles.shape, lambda i: (0, 0)), pl.BlockSpec(sel.shape, lambda i: (0, 0))],
               out_specs=pl.BlockSpec((n, 128), lambda i: (0, 0)),
               out_shape=jax.ShapeDtypeStruct((n, 128), F32), sem=("arbitrary",))(dtiles, sel)


ROWS_B = 16
BQ_B = ROWS_B * GRID_W


def _b_specs(T):
    return [pl.BlockSpec((BQ_B, HEAD_DIM), lambda h, r: (r, N_HEADS_A + h)),
            pl.BlockSpec((T, HEAD_DIM), lambda h, r: (0, N_HEADS + N_HEADS_A + h)),
            pl.BlockSpec((T, HEAD_DIM), lambda h, r: (0, 2 * N_HEADS + N_HEADS_A + h)),
            pl.BlockSpec((None, N_DR, GRID_W, GRID_W), lambda h, r: (h, 0, 0, 0))]


def _b_row(step, u, rows, t_ref, bias_ref):
    r = step * ROWS_B + u
    rs = jnp.clip(r - NA_ROWS // 2, 0, rows - NA_ROWS)
    first_tile = rs - r + NA_ROWS - 1
    for a in range(NA_ROWS):
        bias_ref[u, :, a * GRID_W:(a + 1) * GRID_W] = t_ref[first_tile + a]
    return slice(u * GRID_W, (u + 1) * GRID_W), pl.ds(pl.multiple_of(rs * GRID_W, GRID_W), WK_B), first_tile


def _attn_b_fwd(qkv, tiles):
    T = qkv.shape[0]
    rows = T // GRID_W

    def body(q_ref, k_ref, v_ref, t_ref, y_ref, lse_ref, bias_ref):
        for u in range(ROWS_B):
            blk, win, _ = _b_row(pl.program_id(1), u, rows, t_ref, bias_ref)
            s = _dot(q_ref[blk, :], k_ref[win, :], "nt") * SCALE + bias_ref[u]
            m = jnp.max(s, axis=-1, keepdims=True)
            p = jnp.exp(s - m)
            den = jnp.sum(p, axis=-1, keepdims=True)
            y_ref[blk, :] = (_dot(p.astype(BF16), v_ref[win, :], "nn") / den).astype(BF16)
            lse_ref[blk, :] = m + jnp.log(den)

    return _pc(body, name="attn_b_fwd", grid=(N_HEADS_B, rows // ROWS_B), in_specs=_b_specs(T),
               out_specs=[pl.BlockSpec((BQ_B, HEAD_DIM), lambda h, r: (r, h)),
                          pl.BlockSpec((None, BQ_B, 1), lambda h, r: (h, r, 0))],
               out_shape=[jax.ShapeDtypeStruct((T, N_HEADS_B * HEAD_DIM), BF16),
                          jax.ShapeDtypeStruct((N_HEADS_B, T, 1), F32)],
               scratch=[pltpu.VMEM((ROWS_B, GRID_W, WK_B), F32)], sem=("parallel", "parallel"))(qkv, qkv, qkv, tiles)


def _attn_b_bwd(qkv, tiles, dy, y, lse):
    T = qkv.shape[0]
    rows = T // GRID_W
    steps = rows // ROWS_B

    def body(q_ref, k_ref, v_ref, t_ref, dy_ref, y_ref, lse_ref, dq_ref, dk_ref, dv_ref, dt_ref,
             bias_ref, ds_ref, dk_acc, dv_acc):
        step = pl.program_id(1)

        @pl.when(step == 0)
        def _():
            dk_acc[...] = jnp.zeros_like(dk_acc)
            dv_acc[...] = jnp.zeros_like(dv_acc)
            dt_ref[...] = jnp.zeros_like(dt_ref)

        for u in range(ROWS_B):
            blk, win, first_tile = _b_row(step, u, rows, t_ref, bias_ref)
            qv = q_ref[blk, :]
            kw = k_ref[win, :]
            dyv = dy_ref[blk, :]
            s = _dot(qv, kw, "nt") * SCALE + bias_ref[u]
            p = jnp.exp(s - lse_ref[blk, :])
            delta = jnp.sum(dyv.astype(F32) * y_ref[blk, :].astype(F32), axis=-1, keepdims=True)
            ds = p * (_dot(dyv, v_ref[win, :], "nt") - delta)
            ds_ref[u] = ds
            for a in range(NA_ROWS):
                dt_ref[first_tile + a] += ds_ref[u, :, a * GRID_W:(a + 1) * GRID_W]
            dsb = ds.astype(BF16)
            dq_ref[blk, :] = (_dot(dsb, kw, "nn") * SCALE).astype(BF16)
            dk_acc[win, :] += _dot(dsb, qv, "tn") * SCALE
            dv_acc[win, :] += _dot(p.astype(BF16), dyv, "tn")

        @pl.when(step == steps - 1)
        def _():
            dk_ref[...] = dk_acc[...].astype(BF16)
            dv_ref[...] = dv_acc[...].astype(BF16)

    blk_spec = pl.BlockSpec((BQ_B, HEAD_DIM), lambda h, r: (r, h))
    whole = pl.BlockSpec((T, HEAD_DIM), lambda h, r: (0, h))
    full = jax.ShapeDtypeStruct((T, N_HEADS_B * HEAD_DIM), BF16)
    return _pc(body, name="attn_b_bwd", grid=(N_HEADS_B, steps),
               in_specs=_b_specs(T) + [blk_spec, blk_spec, pl.BlockSpec((None, BQ_B, 1), lambda h, r: (h, r, 0))],
               out_specs=[blk_spec, whole, whole, pl.BlockSpec((None, N_DR, GRID_W, GRID_W), lambda h, r: (h, 0, 0, 0))],
               out_shape=[full, full, full, jax.ShapeDtypeStruct((N_HEADS_B, N_DR, GRID_W, GRID_W), F32)],
               scratch=[pltpu.VMEM((ROWS_B, GRID_W, WK_B), F32), pltpu.VMEM((ROWS_B, GRID_W, WK_B), F32),
                        pltpu.VMEM((T, HEAD_DIM), F32), pltpu.VMEM((T, HEAD_DIM), F32)],
               sem=("parallel", "arbitrary"))(qkv, qkv, qkv, tiles, dy, y, lse)


def _sds(shape, dtype):
    return jax.ShapeDtypeStruct(shape, dtype)


def _behind(value, token):
    return value if token is None else value + token[0, 0]


def _local_step(x, tgt, norm_mix, b_gate, rpb, norm_mlp, norm_final, late_weights, send, riders=None):
    T, D = x.shape
    nt = T // TM
    PP = ("parallel", "parallel", "arbitrary")
    slopes = jnp.asarray(_alibi_slopes())
    sel_np, mask_np = _toeplitz_consts()
    sel, mask = jnp.asarray(sel_np), jnp.asarray(mask_np)

    def rows(w, tm=TM):
        return pl.BlockSpec((tm, w), lambda i, j, k: (i, j))

    def full_rows(tm, w=D):
        return pl.BlockSpec((tm, w), lambda i, j, k: (i, 0))

    def tall(w):
        return pl.BlockSpec((T, w), lambda j, i, k: (0, i))

    def tall_j(w):
        return pl.BlockSpec((T, w), lambda j, i, k: (0, j))

    gvec = pl.BlockSpec((1, D), lambda i, j, k: (0, 0))

    h1, r1 = _rmsnorm_fwd(x, norm_mix, "rmsnorm_mix")
    rpb_rows = jnp.pad(rpb.reshape(N_HEADS_B * N_DR, N_DC), ((0, 0), (0, 128 - N_DC)))
    tiles = _rpb_expand(rpb_rows, sel, mask).reshape(N_HEADS_B, N_DR, GRID_W, GRID_W)
    (wq,) = late_weights("qkv", (h1, tiles))

    def store_bf16(acc, ex, o, pids):
        o[0][...] = acc.astype(BF16)

    (qkv,) = _mm("qkv_proj", (T // TL, N_DEV // 2, 1),
                 [(h1, full_rows(TL), wq, pl.BlockSpec((2, D, 768), lambda i, j, k: (j, 0, 0)), "nn_shards", None)],
                 [], [(_sds((T, 3 * D), BF16), rows(1536, TL))], store_bf16, None, PP)

    def gate_epi(acc, ex, o, pids):
        o[0][...] = jax.nn.sigmoid(acc + ex[0][...]).astype(BF16)

    late_weights("attention", qkv)
    ya, lse_a, last_src = _dilated_fwd(qkv, slopes)
    yb, lse_b = _attn_b_fwd(qkv, tiles)

    wg, wpa, wpb, wo, token = late_weights("mixer", ya)
    (gates,) = _mm("gate_proj", (T // TL, N_DEV // 2, 1),
                   [(h1, full_rows(TL), wg, pl.BlockSpec((2, D, 512), lambda i, j, k: (j, 0, 0)), "nn_shards", None)],
                   [(_behind(b_gate, token), pl.BlockSpec((1, 1024), lambda i, j, k: (0, j)))],
                   [(_sds((T, 2 * D), BF16), rows(1024, TL))], gate_epi, None, PP)

    def merge_body(ya_ref, wa_ref, yb_ref, wb_ref, ga_ref, gb_ref, pa_ref, pb_ref, mg_ref):
        ya_v, yb_v = ya_ref[...], yb_ref[...]
        for s in range(N_DEV):
            cols = slice(s * 256, (s + 1) * 256)
            pa = _dot(ya_v, wa_ref[s], "nn")
            pb = _dot(yb_v, wb_ref[s], "nn")
            pa_ref[:, cols] = pa.astype(BF16)
            pb_ref[:, cols] = pb.astype(BF16)
            mg_ref[:, cols] = (ga_ref[:, cols] * pa + gb_ref[:, cols] * pb).astype(BF16)

    wo2 = wo.reshape(D, D)
    y_spec = pl.BlockSpec((TS, 512), lambda i: (i, 0))
    wp_spec = pl.BlockSpec((N_DEV, 512, 256), lambda i: (0, 0, 0))
    c_spec = pl.BlockSpec((TS, D), lambda i: (i, 0))
    pa, pb, merged = _pc(merge_body, name="merge_proj", grid=(T // TS,),
                         in_specs=[y_spec, wp_spec, y_spec, wp_spec, c_spec, pl.BlockSpec((TS, D), lambda i: (i, 1))],
                         out_specs=[c_spec, c_spec, c_spec],
                         out_shape=[_sds((T, D), BF16)] * 3, sem=("parallel",))(ya, wpa, yb, wpb, gates, gates)

    def add_res(acc, ex, o, pids):
        o[0][...] = ex[0][...] + acc

    def out_norm_epi(acc, ex, o, pids):
        x2v = ex[0][...] + acc
        r = lax.rsqrt(jnp.mean(x2v * x2v, axis=-1, keepdims=True) + EPS)
        o[0][...] = x2v
        o[1][...] = (x2v * r * ex[1][...]).astype(BF16)
        o[2][...] = r

    x2, h2, r2 = _mm("out_proj", (nt, 1, 1),
                     [(merged, full_rows(TM), wo2, pl.BlockSpec((D, D), lambda i, j, k: (0, 0)), "nn", None)],
                     [(x, full_rows(TM)), (norm_mlp, gvec)],
                     [(_sds((T, D), F32), full_rows(TM)), (_sds((T, D), BF16), full_rows(TM)),
                      (_sds((T, 1), F32), pl.BlockSpec((TM, 1), lambda i, j, k: (i, 0)))], out_norm_epi, None, PP)

    (token,) = late_weights("residual", x2)
    (wu,) = late_weights("up", h2)
    behind = lambda token: [] if token is None else [(token, pl.BlockSpec(memory_space=pl.ANY))]

    def up_epi(acc, ex, o, pids):
        o[0][...] = acc.astype(BF16)
        r = jnp.maximum(acc, 0.0)
        o[1][...] = (r * r).astype(BF16)

    u, act = _mm("mlp_up", (T // TL, N_DEV, 1),
                 [(h2, full_rows(TL), wu, pl.BlockSpec((None, D, 1024), lambda i, j, k: (j, 0, 0)), "nn", None)],
                 behind(token), [(_sds((T, D_FF), BF16), rows(1024, TL))] * 2, up_epi, None, PP)
    (wd,) = late_weights("down", act)
    wd2 = wd.reshape(D_FF, D)
    half_ji = pl.BlockSpec((TS, D // 2), lambda j, i, k: (i, j))
    (x3,) = _mm("mlp_down", (2, T // TS, 1),
                [(act, pl.BlockSpec((TS, D_FF), lambda j, i, k: (i, 0)), wd2,
                  pl.BlockSpec((D_FF, D // 2), lambda j, i, k: (0, j)), "nn", None)],
                [(x2, half_ji)], [(_sds((T, D), F32), half_ji)], add_res, None, PP)

    loss, dx3, dx3b, dg3 = _final_norm_loss(x3, norm_final.reshape(1, D), tgt)

    def du_epi(acc, ex, o, pids):
        o[0][...] = (acc * (2.0 * jnp.maximum(ex[0][...].astype(F32), 0.0))).astype(BF16)

    (du,) = _mm("mlp_down_dx", (T // TL, N_DEV, 1),
                [(dx3b, full_rows(TL), wd2, pl.BlockSpec((1024, D), lambda i, j, k: (j, 0)), "nt", None)],
                [(u, rows(1024, TL))], [(_sds((T, D_FF), BF16), rows(1024, TL))], du_epi, None, PP)
    (dwd,) = _mm("mlp_down_dw", (D // 1024, D_FF // 512, 1), [(act, tall(512), dx3b, tall_j(1024), "tn", None)],
                 [], [(_sds((D_FF, D), BF16), pl.BlockSpec((512, 1024), lambda j, i, k: (i, j)))], store_bf16, None, PP)
    dwd = dwd.reshape(N_DEV, D_FF // N_DEV, D)
    (dwu,) = _mm("mlp_up_dw", (N_DEV, D // 512, 1), [(h2, tall(512), du, tall_j(1024), "tn", None)],
                 [], [(_sds((N_DEV, D, 1024), BF16), pl.BlockSpec((None, 512, 1024), lambda j, i, k: (j, i, 0)))],
                 store_bf16, None, PP)

    out_ji = pl.BlockSpec((TM, 512), lambda j, i, k: (i, j))
    a_ji = lambda w: pl.BlockSpec((TM, w), lambda j, i, k: (i, 0))
    w_ji = lambda n: pl.BlockSpec((N_DEV, 512, n), lambda j, i, k: (0, j, 0))
    (dh2,) = _mm("mlp_up_dx", (D // 512, nt, 1), [(du, a_ji(D_FF), wu, w_ji(1024), "nt_shards", None)],
                 behind(send("mlp", dict(w_down=dwd, w_up=dwu))), [(_sds((T, D), BF16), out_ji)], store_bf16, None, PP)
    dx2, dx2b, dg2 = _rmsnorm_bwd("rmsnorm_mlp_bwd", dh2, x2, r2, norm_mlp, dx3, True)

    (dwo,) = _mm("out_proj_dw", (D // 1024, D // 512, 1), [(merged, tall(512), dx2b, tall_j(1024), "tn", None)],
                 [], [(_sds((D, D), BF16), pl.BlockSpec((512, 1024), lambda j, i, k: (i, j)))], store_bf16, None, PP)

    def dmerge_epi(acc, ex, o, pids):
        ga, gb, pa_v, pb_v = (e[...].astype(F32) for e in ex)
        o[0][...] = (acc * ga).astype(BF16)
        o[1][...] = (acc * gb).astype(BF16)
        dga = acc * pa_v * ga * (1.0 - ga)
        dgb = acc * pb_v * gb * (1.0 - gb)
        o[2][0] = dga.astype(BF16)
        o[2][1] = dgb.astype(BF16)
        sa = jnp.sum(dga, axis=0, keepdims=True)
        sb = jnp.sum(dgb, axis=0, keepdims=True)
        first = pids[1] == 0

        @pl.when(first)
        def _():
            o[3][...] = sa
            o[4][...] = sb

        @pl.when(jnp.logical_not(first))
        def _():
            o[3][...] += sa
            o[4][...] += sb

    cj = pl.BlockSpec((TL, 512), lambda j, i, k: (i, j))
    bj = pl.BlockSpec((1, 512), lambda j, i, k: (0, j))
    dpa, dpb, dgate, dba, dbb = _mm(
        "out_proj_dx", (4, T // TL, 1),
        [(dx2b, pl.BlockSpec((TL, D), lambda j, i, k: (i, 0)), wo2, pl.BlockSpec((512, D), lambda j, i, k: (j, 0)), "nt", None)],
        [(gates, cj), (gates, pl.BlockSpec((TL, 512), lambda j, i, k: (i, 4 + j))), (pa, cj), (pb, cj)],
        [(_sds((T, D), BF16), cj)] * 2 + [(_sds((2, T, D), BF16), pl.BlockSpec((2, TL, 512), lambda j, i, k: (0, i, j)))]
        + [(_sds((1, D), F32), bj)] * 2, dmerge_epi, None, ("parallel", "arbitrary", "arbitrary"))

    def store_shards(acc, ex, o, pids):
        for s in range(4):
            o[0][s] = acc[:, s * 256:(s + 1) * 256].astype(BF16)

    def proj_dw(name, y, dp):
        (dw,) = _mm(name, (2, 1, 1), [(y, tall(512), dp, tall_j(1024), "tn", None)],
                    [], [(_sds((N_DEV, 512, 256), BF16), pl.BlockSpec((4, 512, 256), lambda j, i, k: (j, 0, 0)))],
                    store_shards, None, PP)
        return dw

    def proj_dx(name, dp, w):
        (dy,) = _mm(name, (nt, 1, 1),
                    [(dp, full_rows(TM), w, pl.BlockSpec((N_DEV, 512, 256), lambda i, j, k: (0, 0, 0)), "nt_shards", None)],
                    [], [(_sds((T, 512), BF16), full_rows(TM, 512))], store_bf16, None, PP)
        return dy

    def in_dw(name, dact, dact_spec, n):
        def store_pair(acc, ex, o, pids):
            o[0][0] = acc[:, :n].astype(BF16)
            o[0][1] = acc[:, n:].astype(BF16)

        (dw,) = _mm(name, (N_DEV // 2, D // 512, 1), [(h1, tall(512), dact, dact_spec, "tn", None)],
                    [], [(_sds((N_DEV, D, n), BF16), pl.BlockSpec((2, 512, n), lambda j, i, k: (j, i, 0)))],
                    store_pair, None, PP)
        return dw

    dwg = in_dw("gate_proj_dw", dgate, pl.BlockSpec((None, T, 1024), lambda j, i, k: (j // 2, 0, j % 2)), 512)
    dwpa = proj_dw("proj_a_dw", ya, dpa)
    dwpb = proj_dw("proj_b_dw", yb, dpb)
    token = send("proj", dict(w_gate=dwg, w_out=dwo.reshape(N_DEV, D // N_DEV, D), w_proj_a=dwpa, w_proj_b=dwpb))
    dya = proj_dx("proj_a_dx", dpa, wpa)
    dyb = proj_dx("proj_b_dx", dpb, wpb)

    dq_a, dk_a, dv_a = _dilated_bwd(qkv, last_src, _behind(slopes, token), dya, ya, lse_a)
    dq_b, dk_b, dv_b, dtiles = _attn_b_bwd(qkv, tiles, dyb, yb, lse_b)
    drpb = _rpb_reduce(dtiles.reshape(N_HEADS_B * N_DR, GRID_W * GRID_W), sel)[:, :N_DC]
    dqkv = jnp.concatenate([*dq_a, dq_b, *dk_a, dk_b, *dv_a, dv_b], axis=1)
    dwq = in_dw("qkv_proj_dw", dqkv, tall_j(1536), 768)
    half_a = lambda h: pl.BlockSpec((None, TM, D), lambda j, i, k: (h, i, 0))
    half_w = lambda h: pl.BlockSpec((4, 512, 512), lambda j, i, k: (h, j, 0))
    token = send("in", dict(w_qkv=dwq))
    riders = riders(dwq if token is None else token, 4 * nt, lambda j, i, k: j * nt + i) if riders else []
    lead = behind(token)

    def store_and_ride(acc, ex, o, pids):
        o[0][...] = acc.astype(BF16)
        for r, (_, _, run) in enumerate(riders):
            run(ex[len(lead) + 4 * r:len(lead) + 4 * r + 4], o[1 + 4 * r:5 + 4 * r])

    dh1, *rode = _mm("in_proj_dx", (D // 512, nt, 1),
                     [(dqkv, a_ji(3 * D), wq, w_ji(768), "nt_shards", None),
                      (dgate, half_a(0), wg, half_w(0), "nt_shards", None), (dgate, half_a(1), wg, half_w(1), "nt_shards", None)],
                     lead + [e for r in riders for e in r[0]],
                     [(_sds((T, D), BF16), out_ji)] + [o for r in riders for o in r[1]], store_and_ride, None, PP)
    grad_x, dg1 = _rmsnorm_bwd("rmsnorm_mix_bwd", dh1, x, r1, norm_mix, dx2, False)

    small = dict(norm_mix=dg1, b_gate=jnp.concatenate([dba, dbb], axis=1), rpb=drpb, norm_mlp=dg2, norm_final=dg3)
    return loss, grad_x, small, [rode[4 * r:4 * r + 4] for r in range(len(riders))]


BIG = ("w_qkv", "w_gate", "w_proj_a", "w_proj_b", "w_out", "w_up", "w_down")
SMALL = ("norm_mix", "b_gate", "rpb", "norm_mlp", "norm_final")
SMALL_LANES = 96 * 128
RIDING = ("w_up", "w_down", "w_gate")


def _place():
    return lax.axis_index("x"), lax.axis_index("y"), lax.axis_index("c")


def _peer(x, y, c, mask):
    return x ^ (mask >> 2), y ^ ((mask >> 1) & 1), c ^ (mask & 1)


_HBM = pl.BlockSpec(memory_space=pltpu.HBM)
_SEM = pl.BlockSpec(memory_space=pltpu.SEMAPHORE)
_ANY = pl.BlockSpec(memory_space=pl.ANY)
_EFFECT = pltpu.SideEffectType.DATAFLOW_SIDE_EFFECTING


def _hbm(a):
    return pltpu.with_memory_space_constraint(a, pltpu.HBM)


def _other_chips(x, y):
    return [(1 - x, y), (x, 1 - y), (1 - x, 1 - y)]


def _block(x, y, c):
    return 4 * x + 2 * y + c


def _rdma(src, dst, send_sem, recv_sem, to):
    return pltpu.make_async_remote_copy(src_ref=src, dst_ref=dst, send_sem=send_sem, recv_sem=recv_sem, device_id=to,
                                        device_id_type=MESH)


def _gather_start(name, shards):
    n = len(shards)

    def body(*refs):
        srcs, lands = refs[:n], refs[n:2 * n]
        outs = refs[2 * n:]
        send, recv_d2d, recv_ici, local = outs[:n], outs[n:2 * n], outs[2 * n:3 * n], outs[3 * n:4 * n]
        token_ref = outs[6 * n]
        x, y, c = _place()
        for a in range(n):
            mine = lands[a].at[_block(x, y, c)]
            _rdma(srcs[a], mine, send[a].at[0], recv_d2d[a].at[0], (x, y, 1 - c)).start()
            for j, chip in enumerate(_other_chips(x, y)):
                _rdma(srcs[a], mine, send[a].at[1 + j], recv_ici[a].at[j], (*chip, c)).start()
        for a in range(n):
            pltpu.make_async_copy(srcs[a], lands[a].at[_block(x, y, c)], local[a].at[0]).start()
        token_ref[...] = jnp.zeros_like(token_ref)

    lands = [_hbm(lax.empty((N_DEV,) + s.shape, s.dtype)) for s in shards]
    srcs = [_hbm(s) for s in shards]
    dma = pltpu.SemaphoreType.DMA
    res = pl.pallas_call(
        body, name=name,
        out_shape=[dma((4,))] * n + [dma((1,))] * n + [dma((3,))] * n + [dma((1,))] * n
        + [pltpu.HBM(s.shape, s.dtype) for s in srcs] + [pltpu.HBM(l.shape, l.dtype) for l in lands]
        + [jax.ShapeDtypeStruct((8, 128), F32)],
        in_specs=[_HBM] * (2 * n), out_specs=[_SEM] * (4 * n) + [_HBM] * (2 * n) + [pl.BlockSpec(memory_space=pltpu.VMEM)],
        input_output_aliases={i: 4 * n + i for i in range(2 * n)},
        compiler_params=pltpu.CompilerParams(has_side_effects=_EFFECT))(*srcs, *lands)
    keys = ("send", "recv_d2d", "recv_ici", "local", "src", "land")
    return [dict(zip(keys, (res[k * n + a] for k in range(6)))) for a in range(n)], res[6 * n]


def _gather_forward(name, handles, after):
    n = len(handles)
    after = after if isinstance(after, (tuple, list)) else (after,)

    def body(*refs):
        lands, recv_ici = refs[:n], refs[n:2 * n]
        outs = refs[2 * n + len(after):]
        fwd_send, fwd_recv, token_ref = outs[n:2 * n], outs[2 * n:3 * n], outs[3 * n]
        x, y, c = _place()
        for a in range(n):
            for j, chip in enumerate(_other_chips(x, y)):
                blk = lands[a].at[_block(*chip, c)]
                _rdma(blk, blk, fwd_send[a].at[j], recv_ici[a].at[j], (*chip, c)).wait_recv()
                _rdma(blk, blk, fwd_send[a].at[j], fwd_recv[a].at[j], (x, y, 1 - c)).start()
        token_ref[...] = jnp.zeros_like(token_ref)

    dma = pltpu.SemaphoreType.DMA
    lands = [h["land"] for h in handles]
    res = pl.pallas_call(
        body, name=name,
        out_shape=[pltpu.HBM(l.shape, l.dtype) for l in lands] + [dma((3,))] * (2 * n) + [jax.ShapeDtypeStruct((8, 128), F32)],
        in_specs=[_HBM] * n + [_SEM] * n + [_ANY] * len(after),
        out_specs=[_HBM] * n + [_SEM] * (2 * n) + [pl.BlockSpec(memory_space=pltpu.VMEM)],
        input_output_aliases={i: i for i in range(n)},
        compiler_params=pltpu.CompilerParams(has_side_effects=_EFFECT))(*lands, *[h["recv_ici"] for h in handles], *after)
    return [dict(h, land=res[a], fwd_send=res[n + a], fwd_recv=res[2 * n + a]) for a, h in enumerate(handles)], res[3 * n]


def _gather_wait(name, handles, after):
    n = len(handles)

    def body(*refs):
        srcs, lands = refs[:n], refs[n:2 * n]
        send, recv_d2d, local, fwd_send, fwd_recv = (refs[(2 + k) * n:(3 + k) * n] for k in range(5))
        x, y, c = _place()
        sibling = (x, y, 1 - c)
        for a in range(n):
            mine = lands[a].at[_block(x, y, c)]
            _rdma(srcs[a], mine, send[a].at[0], recv_d2d[a].at[0], sibling).wait_send()
            _rdma(srcs[a], lands[a].at[_block(*sibling)], send[a].at[0], recv_d2d[a].at[0], sibling).wait_recv()
            pltpu.make_async_copy(srcs[a], mine, local[a].at[0]).wait()
            for j, chip in enumerate(_other_chips(x, y)):
                _rdma(srcs[a], mine, send[a].at[1 + j], fwd_recv[a].at[j], (*chip, c)).wait_send()
                blk = lands[a].at[_block(*chip, c)]
                _rdma(blk, blk, fwd_send[a].at[j], fwd_recv[a].at[j], sibling).wait_send()
                got = lands[a].at[_block(*chip, 1 - c)]
                _rdma(got, got, fwd_send[a].at[j], fwd_recv[a].at[j], sibling).wait_recv()

    srcs = [h["src"] for h in handles]
    lands = [h["land"] for h in handles]
    sems = [h[k] for k in ("send", "recv_d2d", "local", "fwd_send", "fwd_recv") for h in handles]
    res = pl.pallas_call(
        body, name=name, out_shape=[pltpu.HBM(s.shape, s.dtype) for s in srcs] + [pltpu.HBM(l.shape, l.dtype) for l in lands],
        in_specs=[_HBM] * (2 * n) + [_SEM] * (5 * n) + [_ANY], out_specs=[_HBM] * (2 * n),
        input_output_aliases={i: i for i in range(2 * n)},
        compiler_params=pltpu.CompilerParams(has_side_effects=_EFFECT))(*srcs, *lands, *sems, after)
    return list(res[n:])


def _gather_small(small, after):
    def body(in_ref, *refs):
        out_ref, send_sems, recv_sems, local_sem = refs[len(after):]
        x, y, c = _place()
        me = 4 * x + 2 * y + c
        copies = [pltpu.make_async_copy(in_ref, out_ref.at[me], local_sem)]
        for mask in range(1, N_DEV):
            copies.append(pltpu.make_async_remote_copy(
                src_ref=in_ref, dst_ref=out_ref.at[me], send_sem=send_sems.at[mask - 1],
                recv_sem=recv_sems.at[mask - 1], device_id=_peer(x, y, c, mask), device_id_type=MESH))
        for cp in copies:
            cp.start()
        for cp in copies:
            cp.wait()

    any_spec = pl.BlockSpec(memory_space=pl.ANY)
    return pl.pallas_call(
        body, name="gather_small", in_specs=[any_spec] * (1 + len(after)), out_specs=any_spec,
        out_shape=jax.ShapeDtypeStruct((N_DEV,) + small.shape, small.dtype),
        scratch_shapes=[pltpu.SemaphoreType.DMA((7,)), pltpu.SemaphoreType.DMA((7,)), pltpu.SemaphoreType.DMA])(small, *after)


def _send_copy(src_refs, land_refs, send_sems, recv_sems, a, mask, scatter, x, y, c):
    px, py, pc = _peer(x, y, c, mask)
    src = src_refs[a].at[4 * px + 2 * py + pc] if scatter else src_refs[a]
    return pltpu.make_async_remote_copy(
        src_ref=src, dst_ref=land_refs[a].at[4 * x + 2 * y + c], send_sem=send_sems[a].at[mask - 1],
        recv_sem=recv_sems[a].at[mask - 1], device_id=(px, py, pc), device_id_type=MESH)


def _local_copy(src_refs, land_refs, local_sems, a, scatter, x, y, c):
    me = 4 * x + 2 * y + c
    return pltpu.make_async_copy(src_refs[a].at[me] if scatter else src_refs[a], land_refs[a].at[me], local_sems[a].at[0])


def _send_start(name, srcs, scatter, after=None):
    n = len(srcs)
    n_in = 2 * n + (after is not None)
    shapes = [s.shape[1:] if scatter else s.shape for s in srcs]

    def body(*refs):
        src_refs, land_refs = refs[:n], refs[n:2 * n]
        outs = refs[n_in:]
        send_sems, recv_sems, local_sems = outs[:n], outs[n:2 * n], outs[2 * n:3 * n]
        token_ref = outs[5 * n]
        x, y, c = _place()
        for a in range(n):
            for mask in range(1, N_DEV):
                _send_copy(src_refs, land_refs, send_sems, recv_sems, a, mask, scatter, x, y, c).start()
        for a in range(n):
            _local_copy(src_refs, land_refs, local_sems, a, scatter, x, y, c).start()
        token_ref[...] = jnp.zeros_like(token_ref)

    lands = [pltpu.with_memory_space_constraint(lax.empty((N_DEV,) + sh, s.dtype), pltpu.HBM) for sh, s in zip(shapes, srcs)]
    srcs = [pltpu.with_memory_space_constraint(s, pltpu.HBM) for s in srcs]
    res = pl.pallas_call(
        body, name=name,
        out_shape=[pltpu.SemaphoreType.DMA((7,))] * (2 * n) + [pltpu.SemaphoreType.DMA((1,))] * n
        + [pltpu.HBM(s.shape, s.dtype) for s in srcs] + [pltpu.HBM(l.shape, l.dtype) for l in lands]
        + [jax.ShapeDtypeStruct((8, 128), F32)],
        in_specs=[_HBM] * (2 * n) + [pl.BlockSpec(memory_space=pl.ANY)] * (n_in - 2 * n),
        out_specs=[_SEM] * (3 * n) + [_HBM] * (2 * n) + [pl.BlockSpec(memory_space=pltpu.VMEM)],
        input_output_aliases={i: 3 * n + i for i in range(2 * n)},
        compiler_params=pltpu.CompilerParams(has_side_effects=_EFFECT))(*srcs, *lands, *([] if after is None else [after]))
    handles = [tuple(res[k * n + a] for k in range(5)) for a in range(n)]
    return handles, res[5 * n]


def _send_wait(name, handles, scatter, after):
    n = len(handles)

    def body(*refs):
        src_refs, land_refs = refs[:n], refs[n:2 * n]
        send_sems, recv_sems, local_sems = refs[2 * n:3 * n], refs[3 * n:4 * n], refs[4 * n:5 * n]
        x, y, c = _place()
        for a in range(n):
            for mask in range(1, N_DEV):
                cp = _send_copy(src_refs, land_refs, send_sems, recv_sems, a, mask, scatter, x, y, c)
                cp.wait_send()
                cp.wait_recv()
            _local_copy(src_refs, land_refs, local_sems, a, scatter, x, y, c).wait()

    srcs = [h[3] for h in handles]
    lands = [h[4] for h in handles]
    res = pl.pallas_call(
        body, name=name, out_shape=[pltpu.HBM(s.shape, s.dtype) for s in srcs] + [pltpu.HBM(l.shape, l.dtype) for l in lands],
        in_specs=[_HBM] * (2 * n) + [_SEM] * (3 * n) + [pl.BlockSpec(memory_space=pl.ANY)], out_specs=[_HBM] * (2 * n),
        input_output_aliases={i: i for i in range(2 * n)},
        compiler_params=pltpu.CompilerParams(has_side_effects=_EFFECT))(
            *srcs, *lands, *[h[0] for h in handles], *[h[1] for h in handles], *[h[2] for h in handles], after)
    return list(res[n:])


def _adamw_math(g, w, m, v):
    nm = ADAM_B1 * m + (1.0 - ADAM_B1) * g
    nv = ADAM_B2 * v + (1.0 - ADAM_B2) * jnp.square(g)
    m_hat = nm / (1.0 - ADAM_B1 ** ADAM_STEP)
    v_hat = nv / (1.0 - ADAM_B2 ** ADAM_STEP)
    return -ADAM_LR * (m_hat / (jnp.sqrt(v_hat) + ADAM_EPS) + ADAM_WD * w), nm, nv


def _adamw_small(gathered, ws, ms, vs):
    n = len(ws)
    widths = [w.shape[1] for w in ws]

    def body(p_ref, *refs):
        ins, outs = refs[:3 * n], refs[3 * n:]
        off = 0
        for a in range(n):
            g = p_ref[0, :, off:off + widths[a]]
            for i in range(1, N_DEV):
                g = g + p_ref[i, :, off:off + widths[a]]
            outs[4 * a][...] = g
            outs[4 * a + 1][...], outs[4 * a + 2][...], outs[4 * a + 3][...] = _adamw_math(
                g, ins[a][...], ins[n + a][...], ins[2 * n + a][...])
            off += widths[a]

    whole = lambda shape: pl.BlockSpec(shape, lambda i: (0,) * len(shape))
    rows = [whole((1, wd)) for wd in widths]
    res = _pc(body, name="adamw_small", grid=(1,), in_specs=[whole(gathered.shape)] + rows * 3,
              out_specs=[r for r in rows for _ in range(4)],
              out_shape=[jax.ShapeDtypeStruct((1, wd), F32) for wd in widths for _ in range(4)],
              sem=("arbitrary",))(gathered, *ws, *ms, *vs)
    return [res[4 * a:4 * a + 4] for a in range(n)]


def _adamw(name, parts, w, m, v):
    K, n = w.shape
    tr = min(K, 128)

    def body(p_ref, w_ref, m_ref, v_ref, g_ref, d_ref, nm_ref, nv_ref):
        g = p_ref[0].astype(F32)
        for i in range(1, N_DEV):
            g = g + p_ref[i].astype(F32)
        g_ref[...] = g
        d_ref[...], nm_ref[...], nv_ref[...] = _adamw_math(g, w_ref[...], m_ref[...], v_ref[...])

    blk = pl.BlockSpec((tr, n), lambda i: (i, 0))
    return _pc(body, name=name, grid=(K // tr,), in_specs=[pl.BlockSpec((N_DEV, tr, n), lambda i: (0, i, 0)), blk, blk, blk],
               out_specs=[blk] * 4, out_shape=[jax.ShapeDtypeStruct((K, n), F32)] * 4, sem=("parallel",))(parts, w, m, v)


def _adamw_rider(parts, w, m, v, steps, step_of):
    K, n = w.shape
    tr = K // steps
    blk = pl.BlockSpec((tr, n), lambda *ids: (step_of(*ids), 0))
    extras = [(parts, pl.BlockSpec((N_DEV, tr, n), lambda *ids: (0, step_of(*ids), 0))), (w, blk), (m, blk), (v, blk)]

    def run(ex, o):
        g = ex[0][0].astype(F32)
        for i in range(1, N_DEV):
            g = g + ex[0][i].astype(F32)
        o[0][...] = g
        o[1][...], o[2][...], o[3][...] = _adamw_math(g, ex[1][...], ex[2][...], ex[3][...])

    return extras, [(jax.ShapeDtypeStruct((K, n), F32), blk)] * 4, run


def _lane_padded(size):
    return -(-size // 128) * 128


def _pack_small(vals):
    rows = [jnp.pad(v.reshape(1, -1), ((0, 0), (0, _lane_padded(v.size) - v.size))) for v in vals]
    used = sum(r.shape[1] for r in rows)
    return jnp.concatenate(rows + [jnp.zeros((1, SMALL_LANES - used), F32)], axis=1)


def kernel(x, norm_mix, w_qkv, w_gate, b_gate, rpb, w_proj_a, w_proj_b, w_out, norm_mlp, w_up, w_down, norm_final, loss_target, m_norm_mix, m_w_qkv, m_w_gate, m_b_gate, m_rpb, m_w_proj_a, m_w_proj_b, m_w_out, m_norm_mlp, m_w_up, m_w_down, m_norm_final, v_norm_mix, v_w_qkv, v_w_gate, v_b_gate, v_rpb, v_w_proj_a, v_w_proj_b, v_w_out, v_norm_mlp, v_w_up, v_w_down, v_norm_final):
    w = dict(norm_mix=norm_mix, w_qkv=w_qkv, w_gate=w_gate, b_gate=b_gate, rpb=rpb, w_proj_a=w_proj_a, w_proj_b=w_proj_b,
             w_out=w_out, norm_mlp=norm_mlp, w_up=w_up, w_down=w_down, norm_final=norm_final)
    m = dict(norm_mix=m_norm_mix, w_qkv=m_w_qkv, w_gate=m_w_gate, b_gate=m_b_gate, rpb=m_rpb, w_proj_a=m_w_proj_a,
             w_proj_b=m_w_proj_b, w_out=m_w_out, norm_mlp=m_norm_mlp, w_up=m_w_up, w_down=m_w_down, norm_final=m_norm_final)
    v = dict(norm_mix=v_norm_mix, w_qkv=v_w_qkv, w_gate=v_w_gate, b_gate=v_b_gate, rpb=v_rpb, w_proj_a=v_w_proj_a,
             w_proj_b=v_w_proj_b, w_out=v_w_out, norm_mlp=v_norm_mlp, w_up=v_w_up, w_down=v_w_down, norm_final=v_norm_final)
    order = ("norm_mix", "w_qkv", "w_gate", "b_gate", "rpb", "w_proj_a", "w_proj_b", "w_out", "norm_mlp", "w_up",
             "w_down", "norm_final")

    stages = dict(qkv=("w_qkv",), gate=("w_gate", "w_proj_a", "w_proj_b", "w_out"), mlp=("w_up", "w_down"))
    first, token = _gather_start("gather_start_qkv", [w["w_qkv"][0].astype(BF16)])
    names = [n for s in ("gate", "mlp") for n in stages[s]]
    rest, token = _gather_start("gather_start", [_behind(w[n][0], token).astype(BF16) for n in names])
    handles = dict(zip(names, rest), w_qkv=first[0])

    def late_weights(stage, after):
        if stage == "qkv":
            forwarded, _ = _gather_forward("gather_forward_qkv", [handles["w_qkv"]], after)
            return tuple(_gather_wait("gather_wait_qkv", forwarded, after[0]))
        if stage == "attention":
            handles["gate"], tok = _gather_forward("gather_forward_gate", [handles[n] for n in stages["gate"]], after)
            return (tok,)
        if stage == "mixer":
            handles["up"], tok = _gather_forward("gather_forward_up", [handles["w_up"]], after)
            return tuple(_gather_wait("gather_wait_gate", handles["gate"], after)) + (tok,)
        if stage == "residual":
            handles["down"], tok = _gather_forward("gather_forward_down", [handles["w_down"]], after)
            return (tok,)
        return tuple(_gather_wait("gather_wait_" + stage, handles[stage], after))

    sent = {}

    def send(stage, grads):
        handles, tok = _send_start("grads_start_" + stage, list(grads.values()), True)
        sent[stage] = (tuple(grads), handles)
        return tok

    waited = {}

    def riders(after, steps, step_of):
        landed = {}
        for stage in ("mlp", "proj"):
            names, handles = sent[stage]
            waited[stage] = dict(zip(names, _send_wait("grads_wait_" + stage, handles, True, after)))
            after = waited[stage][names[0]]
            landed.update(waited[stage])
        return [_adamw_rider(landed[n], w[n][0], m[n][0], v[n][0], steps, step_of) for n in RIDING]

    loss, grad_x, small, rode = _local_step(x[0], loss_target[0], _behind(norm_mix, token), b_gate, rpb[0], norm_mlp,
                                            norm_final, late_weights, send, riders)

    grad, delta, new_m, new_v = {}, {}, {}, {}
    for n, res in zip(RIDING, rode):
        grad[n], delta[n], new_m[n], new_v[n] = (r[None] for r in res)
    after = grad_x
    for stage in ("mlp", "proj", "small", "in"):
        if stage == "small":
            all_small = after = _gather_small(_pack_small([small[n] for n in SMALL]), [d[0] for d in delta.values()])
            continue
        names, handles = sent[stage]
        landed = waited[stage] if stage in waited else dict(zip(names, _send_wait("grads_wait_" + stage, handles, True, after)))
        for n, parts in landed.items():
            if n in RIDING:
                continue
            res = _adamw("adamw_" + n, parts, w[n][0], m[n][0], v[n][0])
            grad[n], delta[n], new_m[n], new_v[n] = (r[None] for r in res)
            after = res[1]
    as_row = lambda a: jnp.pad(a.reshape(1, -1), ((0, 0), (0, _lane_padded(a.size) - a.size)))
    res = _adamw_small(all_small, *[[as_row(t[n]) for n in SMALL] for t in (w, m, v)])
    for n, rows in zip(SMALL, res):
        for tree, row in zip((grad, delta, new_m, new_v), rows):
            tree[n] = row[0, :w[n].size].reshape(w[n].shape)

    local_loss, _, _ = lax.optimization_barrier((loss[0, 0], res[0][1], delta["w_qkv"]))
    total = lax.psum(local_loss, ("x", "y", "c"))
    return (total, grad_x[None], *[grad[n] for n in order], *[delta[n] for n in order],
            *[new_m[n] for n in order], *[new_v[n] for n in order])
```

```python
import numpy as np
import jax
import jax.numpy as jnp
from jax import lax
from jax.experimental import pallas as pl
from jax.experimental.pallas import tpu as pltpu

F32 = jnp.float32
BF16 = jnp.bfloat16
MESH = pl.DeviceIdType.MESH

D_MODEL = 2048
HEAD_DIM = 128
N_HEADS = 16
N_HEADS_A = 12
N_HEADS_B = 4
DILATIONS = (1, 4, 16)
HALF_WINDOW = 64
GRID_W = 64
NA_ROWS = 8
NA_COLS = 16
D_FF = 4 * D_MODEL
N_DEV = 8
EPS = 1e-6
NEG = -1e30
SCALE = HEAD_DIM ** -0.5
ADAM_LR, ADAM_B1, ADAM_B2, ADAM_EPS, ADAM_WD, ADAM_STEP = 0.001, 0.9, 0.999, 1e-08, 0.01, 10

VMEM_LIMIT = 56 * 1024 * 1024
TM = 512
TL = 1024
TS = 256
TE = 512
BQ_A = 128


def _pc(body, *, name, grid, in_specs, out_specs, out_shape, scratch=(), sem=None):
    return pl.pallas_call(
        body, name=name, grid=grid, in_specs=in_specs, out_specs=out_specs, out_shape=out_shape,
        scratch_shapes=list(scratch),
        compiler_params=pltpu.CompilerParams(dimension_semantics=sem, vmem_limit_bytes=VMEM_LIMIT))


_DOT_DIMS = {"nn": (((1,), (0,)), ((), ())), "nt": (((1,), (1,)), ((), ())), "tn": (((0,), (0,)), ((), ()))}


def _dot(a, b, mode, precision=None):
    if mode == "nn_shards":
        return jnp.concatenate([_dot(a, b[s], "nn") for s in range(b.shape[0])], axis=1)
    if mode == "nt_shards":
        n = b.shape[2]
        out = _dot(a[:, :n], b[0], "nt")
        for s in range(1, b.shape[0]):
            out = out + _dot(a[:, s * n:(s + 1) * n], b[s], "nt")
        return out
    return lax.dot_general(a, b, _DOT_DIMS[mode], preferred_element_type=F32, precision=precision)


def _mm(name, grid, pairs, extras, outs, epilogue, acc_shape, sem):
    n_pairs, n_extra, n_out = len(pairs), len(extras), len(outs)
    k_axis = len(grid) - 1
    nk = grid[k_axis]

    def body(*refs):
        pids = [pl.program_id(ax) for ax in range(len(grid))]
        ab = refs[:2 * n_pairs]
        ex = refs[2 * n_pairs:2 * n_pairs + n_extra]
        o = refs[2 * n_pairs + n_extra:2 * n_pairs + n_extra + n_out]
        part = None
        for p, pair in enumerate(pairs):
            a = ab[2 * p][...]
            if pair[5] is not None:
                a = pair[5](a)
            d = _dot(a, ab[2 * p + 1][...], pair[4])
            part = d if part is None else part + d
        if nk == 1:
            epilogue(part, ex, o, pids)
        else:
            acc = refs[-1]
            k = pids[k_axis]

            @pl.when(k == 0)
            def _():
                acc[...] = part

            @pl.when(k > 0)
            def _():
                acc[...] += part

            @pl.when(k == nk - 1)
            def _():
                epilogue(acc[...], ex, o, pids)

    operands, in_specs = [], []
    for a, a_spec, b, b_spec, _, _ in pairs:
        operands += [a, b]
        in_specs += [a_spec, b_spec]
    for e, e_spec in extras:
        operands.append(e)
        in_specs.append(e_spec)
    res = _pc(body, name=name, grid=grid, in_specs=in_specs, out_specs=[s for _, s in outs],
              out_shape=[o for o, _ in outs], scratch=[pltpu.VMEM(acc_shape, F32)] if nk > 1 else [], sem=sem)(*operands)
    return res


def _rmsnorm_fwd(x, g, name):
    T, D = x.shape

    def body(x_ref, g_ref, h_ref, r_ref):
        xv = x_ref[...]
        r = lax.rsqrt(jnp.mean(xv * xv, axis=-1, keepdims=True) + EPS)
        h_ref[...] = (xv * r * g_ref[...]).astype(BF16)
        r_ref[...] = r

    return _pc(body, name=name, grid=(T // TM,),
               in_specs=[pl.BlockSpec((TM, D), lambda i: (i, 0)), pl.BlockSpec((1, D), lambda i: (0, 0))],
               out_specs=[pl.BlockSpec((TM, D), lambda i: (i, 0)), pl.BlockSpec((TM, 1), lambda i: (i, 0))],
               out_shape=[jax.ShapeDtypeStruct((T, D), BF16), jax.ShapeDtypeStruct((T, 1), F32)],
               sem=("parallel",))(x, g)


def _rmsnorm_bwd_block(dh, x, r, g, dres):
    xh = x * r
    dxh = dh * g
    dx = dres + r * (dxh - xh * jnp.mean(dxh * xh, axis=-1, keepdims=True))
    return dx, jnp.sum(dh * xh, axis=0, keepdims=True)


def _rmsnorm_bwd(name, dh, x, r, g, dres, want_bf16):
    T, D = x.shape

    def body(dh_ref, x_ref, r_ref, g_ref, dres_ref, *outs):
        dx, dgp = _rmsnorm_bwd_block(dh_ref[...].astype(F32), x_ref[...], r_ref[...], g_ref[...], dres_ref[...])
        outs[0][...] = dx
        if want_bf16:
            outs[1][...] = dx.astype(BF16)
        first = pl.program_id(0) == 0

        @pl.when(first)
        def _():
            outs[-1][...] = dgp

        @pl.when(jnp.logical_not(first))
        def _():
            outs[-1][...] += dgp

    row = pl.BlockSpec((TE, D), lambda i: (i, 0))
    vec = pl.BlockSpec((1, D), lambda i: (0, 0))
    n_row = 2 if want_bf16 else 1
    return _pc(body, name=name, grid=(T // TE,), in_specs=[row, row, pl.BlockSpec((TE, 1), lambda i: (i, 0)), vec, row],
               out_specs=[row] * n_row + [vec],
               out_shape=[jax.ShapeDtypeStruct((T, D), F32)] + [jax.ShapeDtypeStruct((T, D), BF16)] * (n_row - 1)
               + [jax.ShapeDtypeStruct((1, D), F32)], sem=("arbitrary",))(dh, x, r, g, dres)


def _final_norm_loss(x3, g, tgt):
    T, D = x3.shape

    def body(x_ref, g_ref, t_ref, loss_ref, dx_ref, dxb_ref, dg_ref):
        i = pl.program_id(0)
        xv = x_ref[...]
        gv = g_ref[...]
        r = lax.rsqrt(jnp.mean(xv * xv, axis=-1, keepdims=True) + EPS)
        xh = xv * r
        e = xh * gv - t_ref[...]
        part = 0.5 * jnp.sum(jnp.mean(e * e, axis=-1, keepdims=True))
        dy = e * (1.0 / D)
        dxh = dy * gv
        dx = r * (dxh - xh * jnp.mean(dxh * xh, axis=-1, keepdims=True))
        dx_ref[...] = dx
        dxb_ref[...] = dx.astype(BF16)
        dgp = jnp.sum(dy * xh, axis=0, keepdims=True)

        @pl.when(i == 0)
        def _():
            loss_ref[...] = jnp.zeros(loss_ref.shape, F32) + part
            dg_ref[...] = dgp

        @pl.when(i > 0)
        def _():
            loss_ref[...] += part
            dg_ref[...] += dgp

    row = pl.BlockSpec((TE, D), lambda i: (i, 0))
    vec = pl.BlockSpec((1, D), lambda i: (0, 0))
    return _pc(body, name="final_norm_loss", grid=(T // TE,), in_specs=[row, vec, row],
               out_specs=[pl.BlockSpec((8, 128), lambda i: (0, 0)), row, row, vec],
               out_shape=[jax.ShapeDtypeStruct((8, 128), F32), jax.ShapeDtypeStruct((T, D), F32),
                          jax.ShapeDtypeStruct((T, D), BF16), jax.ShapeDtypeStruct((1, D), F32)],
               sem=("arbitrary",))(x3, g, tgt)


def _alibi_slopes():
    return np.asarray(2.0 ** (-8.0 * np.arange(1, N_HEADS_A + 1) / N_HEADS_A), dtype=np.float32)


def _col_to_row(col):
    return jnp.transpose(jnp.broadcast_to(col, (BQ_A, BQ_A)))[0:1, :]


def _row_to_col(row):
    return jnp.transpose(jnp.broadcast_to(row, (BQ_A, BQ_A)))[:, 0:1]


def _row_spec(rows_per_step, first=0):
    return pl.BlockSpec((None, rows_per_step // BQ_A, BQ_A), lambda j, i: (first + j, i, 0))


def _a_windows(T):
    half = [HALF_WINDOW * d for d in DILATIONS]
    return half, [min(BQ_A + 2 * w, T) for w in half]


def _a_bias(slope, g, offset, wk, half):
    d = DILATIONS[g]
    dlt = offset + lax.broadcasted_iota(jnp.int32, (BQ_A, wk), 1) - lax.broadcasted_iota(jnp.int32, (BQ_A, wk), 0)
    ad = jnp.maximum(dlt, -dlt)
    ok = (ad <= half) & ((ad & (d - 1)) == 0)
    return jnp.where(ok, -slope * ad.astype(F32), NEG)


def _a_scores(q, kw, slope, g, offset, wk, half):
    return _dot(q, kw, "nt") * SCALE + _a_bias(slope, g, offset, wk, half)


N_JOINT = 2


SUB_J = 8
BS_J = SUB_J * BQ_A


def _a_qkv_specs(T):
    specs = []
    for g in range(N_JOINT):
        specs.append(pl.BlockSpec((BS_J, HEAD_DIM), lambda j, i, g=g: (i, 4 * g + j)))
        specs.append(pl.BlockSpec((T, HEAD_DIM), lambda j, i, g=g: (0, N_HEADS + 4 * g + j)))
        specs.append(pl.BlockSpec((T, HEAD_DIM), lambda j, i, g=g: (0, 2 * N_HEADS + 4 * g + j)))
    return specs


def _a_window(step, u, g, T, half, wks):
    t0 = (step * SUB_J + u) * BQ_A
    ws = pl.multiple_of(jnp.clip(t0 - half[g], 0, T - wks[g]), 64)
    return pl.ds(ws, wks[g]), ws - t0


def _attn_a_fwd(qkv, slopes, o_last, lse_last):
    T = qkv.shape[0]
    half, wks = _a_windows(T)

    def body(sl_ref, *refs):
        qkv_refs, (o_ref, l_ref, y_ref, lse_ref) = refs[:3 * N_JOINT], refs[3 * N_JOINT:]
        j = pl.program_id(0)
        for u in range(SUB_J):
            rows = slice(u * BQ_A, (u + 1) * BQ_A)
            ss, vws = [], []
            lse_last = _row_to_col(l_ref[u:u + 1, :])
            m = lse_last
            for g in range(N_JOINT):
                q_ref, k_ref, v_ref = qkv_refs[3 * g:3 * g + 3]
                win, offset = _a_window(pl.program_id(1), u, g, T, half, wks)
                ss.append(_a_scores(q_ref[rows, :], k_ref[win, :], sl_ref[4 * g + j], g, offset, wks[g], half[g]))
                vws.append(v_ref[win, :])
                m = jnp.maximum(m, jnp.max(ss[g], axis=-1, keepdims=True))
            den = jnp.exp(lse_last - m)
            acc = den * o_ref[rows, :].astype(F32)
            for g in range(N_JOINT):
                p = jnp.exp(ss[g] - m)
                den = den + jnp.sum(p, axis=-1, keepdims=True)
                acc = acc + _dot(p.astype(BF16), vws[g], "nn")
            y_ref[rows, :] = (acc / den).astype(BF16)
            lse_ref[u:u + 1, :] = _col_to_row(m + jnp.log(den))

    blk = pl.BlockSpec((BS_J, HEAD_DIM), lambda j, i: (i, j))
    col = _row_spec(BS_J)
    return _pc(body, name="attn_a_fwd", grid=(4, T // BS_J),
               in_specs=[pl.BlockSpec(memory_space=pltpu.SMEM)] + _a_qkv_specs(T) + [blk, col], out_specs=[blk, col],
               out_shape=[jax.ShapeDtypeStruct((T, 4 * HEAD_DIM), BF16), jax.ShapeDtypeStruct((4, T // BQ_A, BQ_A), F32)],
               sem=("parallel", "parallel"))(slopes, *([qkv] * (3 * N_JOINT)), o_last, lse_last)


def _attn_a_bwd(qkv, slopes, dy, y, lse):
    T = qkv.shape[0]
    steps = T // BS_J
    half, wks = _a_windows(T)

    def body(sl_ref, *refs):
        qkv_refs = refs[:3 * N_JOINT]
        dy_ref, y_ref, lse_ref, dq_ref, dk_ref, dv_ref, delta_ref, dk_acc, dv_acc = refs[3 * N_JOINT:]
        j = pl.program_id(0)
        step = pl.program_id(1)

        @pl.when(step == 0)
        def _():
            dk_acc[...] = jnp.zeros_like(dk_acc)
            dv_acc[...] = jnp.zeros_like(dv_acc)

        for u in range(SUB_J):
            rows = slice(u * BQ_A, (u + 1) * BQ_A)
            dyv = dy_ref[rows, :]
            delta = jnp.sum(dyv.astype(F32) * y_ref[rows, :].astype(F32), axis=-1, keepdims=True)
            delta_ref[u:u + 1, :] = _col_to_row(delta)
            lse_v = _row_to_col(lse_ref[u:u + 1, :])
            for g in range(N_JOINT):
                q_ref, k_ref, v_ref = qkv_refs[3 * g:3 * g + 3]
                win, offset = _a_window(step, u, g, T, half, wks)
                qv = q_ref[rows, :]
                kw = k_ref[win, :]
                s = _a_scores(qv, kw, sl_ref[4 * g + j], g, offset, wks[g], half[g])
                p = jnp.exp(s - lse_v)
                dp = _dot(dyv, v_ref[win, :], "nt")
                ds = (p * (dp - delta)).astype(BF16)
                dq_ref[g, rows, :] = (_dot(ds, kw, "nn") * SCALE).astype(BF16)
                dk_acc[g, win, :] += _dot(ds, qv, "tn") * SCALE
                dv_acc[g, win, :] += _dot(p.astype(BF16), dyv, "tn")

        @pl.when(step == steps - 1)
        def _():
            dk_ref[...] = dk_acc[...].astype(BF16)
            dv_ref[...] = dv_acc[...].astype(BF16)

    blk = pl.BlockSpec((BS_J, HEAD_DIM), lambda j, i: (i, j))
    col = _row_spec(BS_J)
    whole = pl.BlockSpec((N_JOINT, T, HEAD_DIM), lambda j, i: (0, 0, j))
    full = jax.ShapeDtypeStruct((N_JOINT, T, 4 * HEAD_DIM), BF16)
    return _pc(body, name="attn_a_bwd", grid=(4, steps),
               in_specs=[pl.BlockSpec(memory_space=pltpu.SMEM)] + _a_qkv_specs(T) + [blk, blk, col],
               out_specs=[pl.BlockSpec((N_JOINT, BS_J, HEAD_DIM), lambda j, i: (0, i, j)), whole, whole, col],
               out_shape=[full, full, full, jax.ShapeDtypeStruct((4, T // BQ_A, BQ_A), F32)],
               scratch=[pltpu.VMEM((N_JOINT, T, HEAD_DIM), F32), pltpu.VMEM((N_JOINT, T, HEAD_DIM), F32)],
               sem=("parallel", "arbitrary"))(slopes, *([qkv] * (3 * N_JOINT)), dy, y, lse)


SUB_A = 8


def _to_residue_major(a, d, axis=0):
    if d == 1:
        return a
    sh = a.shape
    a = a.reshape(sh[:axis] + (sh[axis] // d, d) + sh[axis + 1:])
    return jnp.swapaxes(a, axis, axis + 1).reshape(sh)


def _to_token_order(a, d, axis=0):
    if d == 1:
        return a
    sh = a.shape
    a = a.reshape(sh[:axis] + (d, sh[axis] // d) + sh[axis + 1:])
    return jnp.swapaxes(a, axis, axis + 1).reshape(sh)


def _band_bias(slope, offset, wk):
    dlt = offset + lax.broadcasted_iota(jnp.int32, (BQ_A, wk), 1) - lax.broadcasted_iota(jnp.int32, (BQ_A, wk), 0)
    ad = jnp.maximum(dlt, -dlt)
    return jnp.where(ad <= HALF_WINDOW, -slope * ad.astype(F32), NEG)


def _band_window(step, u, seg, wk):
    t0 = (step * SUB_A + u) * BQ_A
    lo = (t0 // seg) * seg
    ws = pl.multiple_of(jnp.clip(t0 - HALF_WINDOW, lo, lo + seg - wk), 64)
    return slice(u * BQ_A, (u + 1) * BQ_A), pl.ds(ws, wk), ws - t0


def _band_specs(T, cols):
    bs = SUB_A * BQ_A
    assert T % bs == 0 and T // max(DILATIONS) >= BQ_A, "a query sub-block must lie inside one segment"
    return [pl.BlockSpec(memory_space=pltpu.SMEM),
            pl.BlockSpec((bs, HEAD_DIM), lambda j, i: (i, cols[0] + j)),
            pl.BlockSpec((T, HEAD_DIM), lambda j, i: (0, cols[1] + j)),
            pl.BlockSpec((T, HEAD_DIM), lambda j, i: (0, cols[2] + j))]


def _banded_fwd(name, src, cols, seg, slopes):
    T = src.shape[0]
    bs = SUB_A * BQ_A
    wk = min(BQ_A + 2 * HALF_WINDOW, seg)

    def body(sl_ref, q_ref, k_ref, v_ref, o_ref, lse_ref):
        slope = sl_ref[pl.program_id(0)]
        for u in range(SUB_A):
            rows, win, offset = _band_window(pl.program_id(1), u, seg, wk)
            s = _dot(q_ref[rows, :], k_ref[win, :], "nt") * SCALE + _band_bias(slope, offset, wk)
            m = jnp.max(s, axis=-1, keepdims=True)
            p = jnp.exp(s - m)
            den = jnp.sum(p, axis=-1, keepdims=True)
            o_ref[rows, :] = (_dot(p.astype(BF16), v_ref[win, :], "nn") / den).astype(BF16)
            lse_ref[u:u + 1, :] = _col_to_row(m + jnp.log(den))

    return _pc(body, name=name, grid=(4, T // bs), in_specs=_band_specs(T, cols),
               out_specs=[pl.BlockSpec((bs, HEAD_DIM), lambda j, i: (i, j)), _row_spec(bs)],
               out_shape=[jax.ShapeDtypeStruct((T, 4 * HEAD_DIM), BF16), jax.ShapeDtypeStruct((4, T // BQ_A, BQ_A), F32)],
               sem=("parallel", "parallel"))(slopes, src, src, src)


def _banded_bwd(name, src, cols, seg, slopes, dy, delta_lse):
    T = src.shape[0]
    bs = SUB_A * BQ_A
    steps = T // bs
    wk = min(BQ_A + 2 * HALF_WINDOW, seg)

    def body(sl_ref, q_ref, k_ref, v_ref, dy_ref, delta_ref, lse_ref, dq_ref, dk_ref, dv_ref, dk_acc, dv_acc):
        slope = sl_ref[pl.program_id(0)]
        step = pl.program_id(1)

        @pl.when(step == 0)
        def _():
            dk_acc[...] = jnp.zeros_like(dk_acc)
            dv_acc[...] = jnp.zeros_like(dv_acc)

        for u in range(SUB_A):
            rows, win, offset = _band_window(step, u, seg, wk)
            qv, kw, dyv = q_ref[rows, :], k_ref[win, :], dy_ref[rows, :]
            s = _dot(qv, kw, "nt") * SCALE + _band_bias(slope, offset, wk)
            p = jnp.exp(s - _row_to_col(lse_ref[u:u + 1, :]))
            ds = (p * (_dot(dyv, v_ref[win, :], "nt") - _row_to_col(delta_ref[u:u + 1, :]))).astype(BF16)
            dq_ref[rows, :] = (_dot(ds, kw, "nn") * SCALE).astype(BF16)
            dk_acc[win, :] += _dot(ds, qv, "tn") * SCALE
            dv_acc[win, :] += _dot(p.astype(BF16), dyv, "tn")

        @pl.when(step == steps - 1)
        def _():
            dk_ref[...] = dk_acc[...].astype(BF16)
            dv_ref[...] = dv_acc[...].astype(BF16)

    blk = pl.BlockSpec((bs, HEAD_DIM), lambda j, i: (i, j))
    whole = pl.BlockSpec((T, HEAD_DIM), lambda j, i: (0, j))
    full = jax.ShapeDtypeStruct((T, 4 * HEAD_DIM), BF16)
    return _pc(body, name=name, grid=(4, steps),
               in_specs=_band_specs(T, cols) + [blk, _row_spec(bs), _row_spec(bs, 4)],
               out_specs=[blk, whole, whole], out_shape=[full, full, full],
               scratch=[pltpu.VMEM((T, HEAD_DIM), F32), pltpu.VMEM((T, HEAD_DIM), F32)],
               sem=("parallel", "arbitrary"))(slopes, src, src, src, dy, delta_lse, delta_lse)


def _last_group_source(qkv):
    third = qkv.shape[1] // 3
    own = jnp.concatenate([qkv[:, t * third + 512 * N_JOINT:t * third + 512 * (N_JOINT + 1)] for t in range(3)], axis=1)
    return _to_residue_major(own, DILATIONS[N_JOINT])


def _dilated_fwd(qkv, slopes):
    d = DILATIONS[N_JOINT]
    last_src = _last_group_source(qkv)
    o, lse = _banded_fwd("attn_a_last_fwd", last_src, (0, 4, 8), qkv.shape[0] // d, slopes[4 * N_JOINT:] * float(d))
    lse = _to_token_order(lse.reshape(4, -1), d, axis=1).reshape(lse.shape)
    y, joint_lse = _attn_a_fwd(qkv, slopes, _to_token_order(o, d), lse)
    return y, joint_lse, last_src


def _dilated_bwd(qkv, last_src, slopes, dy, y, lse):
    d = DILATIONS[N_JOINT]
    dq, dk, dv, delta = _attn_a_bwd(qkv, slopes, dy, y, lse)
    delta_lse = jnp.concatenate([delta, lse], axis=0)
    delta_lse = _to_residue_major(delta_lse.reshape(8, -1), d, axis=1).reshape(delta_lse.shape)
    last = _banded_bwd("attn_a_last_bwd", last_src, (0, 4, 8), qkv.shape[0] // d,
                       slopes[4 * N_JOINT:] * float(d), _to_residue_major(dy, d), delta_lse)
    return [[joint[g] for g in range(N_JOINT)] + [_to_token_order(own, d)] for joint, own in zip((dq, dk, dv), last)]


N_DR = 2 * NA_ROWS - 1
N_DC = 2 * NA_COLS - 1
WK_B = NA_ROWS * GRID_W


def _toeplitz_consts():
    col = np.arange(GRID_W)
    start = np.clip(col - NA_COLS // 2, 0, GRID_W - NA_COLS)
    ok = (col[None, :] >= start[:, None]) & (col[None, :] < start[:, None] + NA_COLS)
    dc = np.clip(col[None, :] - col[:, None], -(NA_COLS - 1), NA_COLS - 1) + NA_COLS - 1
    sel = np.zeros((128, GRID_W * GRID_W), np.float32)
    sel[dc.reshape(-1), np.arange(GRID_W * GRID_W)] = ok.reshape(-1).astype(np.float32)
    mask = np.where(ok, 0.0, NEG).astype(np.float32).reshape(1, -1)
    return sel, mask


def _rpb_expand(rpb_rows, sel, mask):
    def body(r_ref, s_ref, m_ref, o_ref):
        o_ref[...] = _dot(r_ref[...], s_ref[...], "nn", precision=lax.Precision.HIGHEST) + m_ref[...]

    n = rpb_rows.shape[0]
    return _pc(body, name="rpb_expand", grid=(1,),
               in_specs=[pl.BlockSpec((n, 128), lambda i: (0, 0)), pl.BlockSpec(sel.shape, lambda i: (0, 0)),
                         pl.BlockSpec(mask.shape, lambda i: (0, 0))],
               out_specs=pl.BlockSpec((n, sel.shape[1]), lambda i: (0, 0)),
               out_shape=jax.ShapeDtypeStruct((n, sel.shape[1]), F32), sem=("arbitrary",))(rpb_rows, sel, mask)


def _rpb_reduce(dtiles, sel):
    def body(t_ref, s_ref, o_ref):
        o_ref[...] = _dot(t_ref[...], s_ref[...], "nt", precision=lax.Precision.HIGHEST)

    n = dtiles.shape[0]
    return _pc(body, name="rpb_reduce", grid=(1,),
               in_specs=[pl.BlockSpec(dtiles.shape, lambda i: (0, 0)), pl.BlockSpec(sel.shape, lambda i: (0, 0))],
               out_specs=pl.BlockSpec((n, 128), lambda i: (0, 0)),
               out_shape=jax.ShapeDtypeStruct((n, 128), F32), sem=("arbitrary",))(dtiles, sel)


ROWS_B = 16
BQ_B = ROWS_B * GRID_W


def _b_specs(T):
    return [pl.BlockSpec((BQ_B, HEAD_DIM), lambda h, r: (r, N_HEADS_A + h)),
            pl.BlockSpec((T, HEAD_DIM), lambda h, r: (0, N_HEADS + N_HEADS_A + h)),
            pl.BlockSpec((T, HEAD_DIM), lambda h, r: (0, 2 * N_HEADS + N_HEADS_A + h)),
            pl.BlockSpec((None, N_DR, GRID_W, GRID_W), lambda h, r: (h, 0, 0, 0))]


def _b_row(step, u, rows, t_ref, bias_ref):
    r = step * ROWS_B + u
    rs = jnp.clip(r - NA_ROWS // 2, 0, rows - NA_ROWS)
    first_tile = rs - r + NA_ROWS - 1
    for a in range(NA_ROWS):
        bias_ref[u, :, a * GRID_W:(a + 1) * GRID_W] = t_ref[first_tile + a]
    return slice(u * GRID_W, (u + 1) * GRID_W), pl.ds(pl.multiple_of(rs * GRID_W, GRID_W), WK_B), first_tile


def _attn_b_fwd(qkv, tiles):
    T = qkv.shape[0]
    rows = T // GRID_W

    def body(q_ref, k_ref, v_ref, t_ref, y_ref, lse_ref, bias_ref):
        for u in range(ROWS_B):
            blk, win, _ = _b_row(pl.program_id(1), u, rows, t_ref, bias_ref)
            s = _dot(q_ref[blk, :], k_ref[win, :], "nt") * SCALE + bias_ref[u]
            m = jnp.max(s, axis=-1, keepdims=True)
            p = jnp.exp(s - m)
            den = jnp.sum(p, axis=-1, keepdims=True)
            y_ref[blk, :] = (_dot(p.astype(BF16), v_ref[win, :], "nn") / den).astype(BF16)
            lse_ref[blk, :] = m + jnp.log(den)

    return _pc(body, name="attn_b_fwd", grid=(N_HEADS_B, rows // ROWS_B), in_specs=_b_specs(T),
               out_specs=[pl.BlockSpec((BQ_B, HEAD_DIM), lambda h, r: (r, h)),
                          pl.BlockSpec((None, BQ_B, 1), lambda h, r: (h, r, 0))],
               out_shape=[jax.ShapeDtypeStruct((T, N_HEADS_B * HEAD_DIM), BF16),
                          jax.ShapeDtypeStruct((N_HEADS_B, T, 1), F32)],
               scratch=[pltpu.VMEM((ROWS_B, GRID_W, WK_B), F32)], sem=("parallel", "parallel"))(qkv, qkv, qkv, tiles)


def _attn_b_bwd(qkv, tiles, dy, y, lse):
    T = qkv.shape[0]
    rows = T // GRID_W
    steps = rows // ROWS_B

    def body(q_ref, k_ref, v_ref, t_ref, dy_ref, y_ref, lse_ref, dq_ref, dk_ref, dv_ref, dt_ref,
             bias_ref, ds_ref, dk_acc, dv_acc):
        step = pl.program_id(1)

        @pl.when(step == 0)
        def _():
            dk_acc[...] = jnp.zeros_like(dk_acc)
            dv_acc[...] = jnp.zeros_like(dv_acc)
            dt_ref[...] = jnp.zeros_like(dt_ref)

        for u in range(ROWS_B):
            blk, win, first_tile = _b_row(step, u, rows, t_ref, bias_ref)
            qv = q_ref[blk, :]
            kw = k_ref[win, :]
            dyv = dy_ref[blk, :]
            s = _dot(qv, kw, "nt") * SCALE + bias_ref[u]
            p = jnp.exp(s - lse_ref[blk, :])
            delta = jnp.sum(dyv.astype(F32) * y_ref[blk, :].astype(F32), axis=-1, keepdims=True)
            ds = p * (_dot(dyv, v_ref[win, :], "nt") - delta)
            ds_ref[u] = ds
            for a in range(NA_ROWS):
                dt_ref[first_tile + a] += ds_ref[u, :, a * GRID_W:(a + 1) * GRID_W]
            dsb = ds.astype(BF16)
            dq_ref[blk, :] = (_dot(dsb, kw, "nn") * SCALE).astype(BF16)
            dk_acc[win, :] += _dot(dsb, qv, "tn") * SCALE
            dv_acc[win, :] += _dot(p.astype(BF16), dyv, "tn")

        @pl.when(step == steps - 1)
        def _():
            dk_ref[...] = dk_acc[...].astype(BF16)
            dv_ref[...] = dv_acc[...].astype(BF16)

    blk_spec = pl.BlockSpec((BQ_B, HEAD_DIM), lambda h, r: (r, h))
    whole = pl.BlockSpec((T, HEAD_DIM), lambda h, r: (0, h))
    full = jax.ShapeDtypeStruct((T, N_HEADS_B * HEAD_DIM), BF16)
    return _pc(body, name="attn_b_bwd", grid=(N_HEADS_B, steps),
               in_specs=_b_specs(T) + [blk_spec, blk_spec, pl.BlockSpec((None, BQ_B, 1), lambda h, r: (h, r, 0))],
               out_specs=[blk_spec, whole, whole, pl.BlockSpec((None, N_DR, GRID_W, GRID_W), lambda h, r: (h, 0, 0, 0))],
               out_shape=[full, full, full, jax.ShapeDtypeStruct((N_HEADS_B, N_DR, GRID_W, GRID_W), F32)],
               scratch=[pltpu.VMEM((ROWS_B, GRID_W, WK_B), F32), pltpu.VMEM((ROWS_B, GRID_W, WK_B), F32),
                        pltpu.VMEM((T, HEAD_DIM), F32), pltpu.VMEM((T, HEAD_DIM), F32)],
               sem=("parallel", "arbitrary"))(qkv, qkv, qkv, tiles, dy, y, lse)


def _sds(shape, dtype):
    return jax.ShapeDtypeStruct(shape, dtype)


def _behind(value, token):
    return value if token is None else value + token[0, 0]


def _local_step(x, tgt, norm_mix, b_gate, rpb, norm_mlp, norm_final, late_weights, send, riders=None):
    T, D = x.shape
    nt = T // TM
    PP = ("parallel", "parallel", "arbitrary")
    slopes = jnp.asarray(_alibi_slopes())
    sel_np, mask_np = _toeplitz_consts()
    sel, mask = jnp.asarray(sel_np), jnp.asarray(mask_np)

    def rows(w, tm=TM):
        return pl.BlockSpec((tm, w), lambda i, j, k: (i, j))

    def full_rows(tm, w=D):
        return pl.BlockSpec((tm, w), lambda i, j, k: (i, 0))

    def tall(w):
        return pl.BlockSpec((T, w), lambda j, i, k: (0, i))

    def tall_j(w):
        return pl.BlockSpec((T, w), lambda j, i, k: (0, j))

    gvec = pl.BlockSpec((1, D), lambda i, j, k: (0, 0))

    h1, r1 = _rmsnorm_fwd(x, norm_mix, "rmsnorm_mix")
    rpb_rows = jnp.pad(rpb.reshape(N_HEADS_B * N_DR, N_DC), ((0, 0), (0, 128 - N_DC)))
    tiles = _rpb_expand(rpb_rows, sel, mask).reshape(N_HEADS_B, N_DR, GRID_W, GRID_W)
    (wq,) = late_weights("qkv", (h1, tiles))

    def store_bf16(acc, ex, o, pids):
        o[0][...] = acc.astype(BF16)

    (qkv,) = _mm("qkv_proj", (T // TL, N_DEV // 2, 1),
                 [(h1, full_rows(TL), wq, pl.BlockSpec((2, D, 768), lambda i, j, k: (j, 0, 0)), "nn_shards", None)],
                 [], [(_sds((T, 3 * D), BF16), rows(1536, TL))], store_bf16, None, PP)

    def gate_epi(acc, ex, o, pids):
        o[0][...] = jax.nn.sigmoid(acc + ex[0][...]).astype(BF16)

    late_weights("attention", qkv)
    ya, lse_a, last_src = _dilated_fwd(qkv, slopes)
    yb, lse_b = _attn_b_fwd(qkv, tiles)

    wg, wpa, wpb, wo, token = late_weights("mixer", ya)
    (gates,) = _mm("gate_proj", (T // TL, N_DEV // 2, 1),
                   [(h1, full_rows(TL), wg, pl.BlockSpec((2, D, 512), lambda i, j, k: (j, 0, 0)), "nn_shards", None)],
                   [(_behind(b_gate, token), pl.BlockSpec((1, 1024), lambda i, j, k: (0, j)))],
                   [(_sds((T, 2 * D), BF16), rows(1024, TL))], gate_epi, None, PP)

    def merge_body(ya_ref, wa_ref, yb_ref, wb_ref, ga_ref, gb_ref, pa_ref, pb_ref, mg_ref):
        ya_v, yb_v = ya_ref[...], yb_ref[...]
        for s in range(N_DEV):
            cols = slice(s * 256, (s + 1) * 256)
            pa = _dot(ya_v, wa_ref[s], "nn")
            pb = _dot(yb_v, wb_ref[s], "nn")
            pa_ref[:, cols] = pa.astype(BF16)
            pb_ref[:, cols] = pb.astype(BF16)
            mg_ref[:, cols] = (ga_ref[:, cols] * pa + gb_ref[:, cols] * pb).astype(BF16)

    wo2 = wo.reshape(D, D)
    y_spec = pl.BlockSpec((TS, 512), lambda i: (i, 0))
    wp_spec = pl.BlockSpec((N_DEV, 512, 256), lambda i: (0, 0, 0))
    c_spec = pl.BlockSpec((TS, D), lambda i: (i, 0))
    pa, pb, merged = _pc(merge_body, name="merge_proj", grid=(T // TS,),
                         in_specs=[y_spec, wp_spec, y_spec, wp_spec, c_spec, pl.BlockSpec((TS, D), lambda i: (i, 1))],
                         out_specs=[c_spec, c_spec, c_spec],
                         out_shape=[_sds((T, D), BF16)] * 3, sem=("parallel",))(ya, wpa, yb, wpb, gates, gates)

    def add_res(acc, ex, o, pids):
        o[0][...] = ex[0][...] + acc

    def out_norm_epi(acc, ex, o, pids):
        x2v = ex[0][...] + acc
        r = lax.rsqrt(jnp.mean(x2v * x2v, axis=-1, keepdims=True) + EPS)
        o[0][...] = x2v
        o[1][...] = (x2v * r * ex[1][...]).astype(BF16)
        o[2][...] = r

    x2, h2, r2 = _mm("out_proj", (nt, 1, 1),
                     [(merged, full_rows(TM), wo2, pl.BlockSpec((D, D), lambda i, j, k: (0, 0)), "nn", None)],
                     [(x, full_rows(TM)), (norm_mlp, gvec)],
                     [(_sds((T, D), F32), full_rows(TM)), (_sds((T, D), BF16), full_rows(TM)),
                      (_sds((T, 1), F32), pl.BlockSpec((TM, 1), lambda i, j, k: (i, 0)))], out_norm_epi, None, PP)

    (token,) = late_weights("residual", x2)
    (wu,) = late_weights("up", h2)
    behind = lambda token: [] if token is None else [(token, pl.BlockSpec(memory_space=pl.ANY))]

    def up_epi(acc, ex, o, pids):
        o[0][...] = acc.astype(BF16)
        r = jnp.maximum(acc, 0.0)
        o[1][...] = (r * r).astype(BF16)

    u, act = _mm("mlp_up", (T // TL, N_DEV, 1),
                 [(h2, full_rows(TL), wu, pl.BlockSpec((None, D, 1024), lambda i, j, k: (j, 0, 0)), "nn", None)],
                 behind(token), [(_sds((T, D_FF), BF16), rows(1024, TL))] * 2, up_epi, None, PP)
    (wd,) = late_weights("down", act)
    wd2 = wd.reshape(D_FF, D)
    half_ji = pl.BlockSpec((TS, D // 2), lambda j, i, k: (i, j))
    (x3,) = _mm("mlp_down", (2, T // TS, 1),
                [(act, pl.BlockSpec((TS, D_FF), lambda j, i, k: (i, 0)), wd2,
                  pl.BlockSpec((D_FF, D // 2), lambda j, i, k: (0, j)), "nn", None)],
                [(x2, half_ji)], [(_sds((T, D), F32), half_ji)], add_res, None, PP)

    loss, dx3, dx3b, dg3 = _final_norm_loss(x3, norm_final.reshape(1, D), tgt)

    def du_epi(acc, ex, o, pids):
        o[0][...] = (acc * (2.0 * jnp.maximum(ex[0][...].astype(F32), 0.0))).astype(BF16)

    (du,) = _mm("mlp_down_dx", (T // TL, N_DEV, 1),
                [(dx3b, full_rows(TL), wd2, pl.BlockSpec((1024, D), lambda i, j, k: (j, 0)), "nt", None)],
                [(u, rows(1024, TL))], [(_sds((T, D_FF), BF16), rows(1024, TL))], du_epi, None, PP)
    (dwd,) = _mm("mlp_down_dw", (D // 1024, D_FF // 512, 1), [(act, tall(512), dx3b, tall_j(1024), "tn", None)],
                 [], [(_sds((D_FF, D), BF16), pl.BlockSpec((512, 1024), lambda j, i, k: (i, j)))], store_bf16, None, PP)
    dwd = dwd.reshape(N_DEV, D_FF // N_DEV, D)
    (dwu,) = _mm("mlp_up_dw", (N_DEV, D // 512, 1), [(h2, tall(512), du, tall_j(1024), "tn", None)],
                 behind(send("down", dict(w_down=dwd))), [(_sds((N_DEV, D, 1024), BF16), pl.BlockSpec((None, 512, 1024), lambda j, i, k: (j, i, 0)))],
                 store_bf16, None, PP)

    out_ji = pl.BlockSpec((TM, 512), lambda j, i, k: (i, j))
    a_ji = lambda w: pl.BlockSpec((TM, w), lambda j, i, k: (i, 0))
    w_ji = lambda n: pl.BlockSpec((N_DEV, 512, n), lambda j, i, k: (0, j, 0))
    (dh2,) = _mm("mlp_up_dx", (D // 512, nt, 1), [(du, a_ji(D_FF), wu, w_ji(1024), "nt_shards", None)],
                 behind(send("mlp", dict(w_up=dwu))), [(_sds((T, D), BF16), out_ji)], store_bf16, None, PP)
    dx2, dx2b, dg2 = _rmsnorm_bwd("rmsnorm_mlp_bwd", dh2, x2, r2, norm_mlp, dx3, True)

    (dwo,) = _mm("out_proj_dw", (D // 1024, D // 512, 1), [(merged, tall(512), dx2b, tall_j(1024), "tn", None)],
                 [], [(_sds((D, D), BF16), pl.BlockSpec((512, 1024), lambda j, i, k: (i, j)))], store_bf16, None, PP)

    def dmerge_epi(acc, ex, o, pids):
        ga, gb, pa_v, pb_v = (e[...].astype(F32) for e in ex)
        o[0][...] = (acc * ga).astype(BF16)
        o[1][...] = (acc * gb).astype(BF16)
        dga = acc * pa_v * ga * (1.0 - ga)
        dgb = acc * pb_v * gb * (1.0 - gb)
        o[2][0] = dga.astype(BF16)
        o[2][1] = dgb.astype(BF16)
        sa = jnp.sum(dga, axis=0, keepdims=True)
        sb = jnp.sum(dgb, axis=0, keepdims=True)
        first = pids[1] == 0

        @pl.when(first)
        def _():
            o[3][...] = sa
            o[4][...] = sb

        @pl.when(jnp.logical_not(first))
        def _():
            o[3][...] += sa
            o[4][...] += sb

    cj = pl.BlockSpec((TL, 512), lambda j, i, k: (i, j))
    bj = pl.BlockSpec((1, 512), lambda j, i, k: (0, j))
    dpa, dpb, dgate, dba, dbb = _mm(
        "out_proj_dx", (4, T // TL, 1),
        [(dx2b, pl.BlockSpec((TL, D), lambda j, i, k: (i, 0)), wo2, pl.BlockSpec((512, D), lambda j, i, k: (j, 0)), "nt", None)],
        [(gates, cj), (gates, pl.BlockSpec((TL, 512), lambda j, i, k: (i, 4 + j))), (pa, cj), (pb, cj)],
        [(_sds((T, D), BF16), cj)] * 2 + [(_sds((2, T, D), BF16), pl.BlockSpec((2, TL, 512), lambda j, i, k: (0, i, j)))]
        + [(_sds((1, D), F32), bj)] * 2, dmerge_epi, None, ("parallel", "arbitrary", "arbitrary"))

    def store_shards(acc, ex, o, pids):
        for s in range(4):
            o[0][s] = acc[:, s * 256:(s + 1) * 256].astype(BF16)

    def proj_dw(name, y, dp):
        (dw,) = _mm(name, (2, 1, 1), [(y, tall(512), dp, tall_j(1024), "tn", None)],
                    [], [(_sds((N_DEV, 512, 256), BF16), pl.BlockSpec((4, 512, 256), lambda j, i, k: (j, 0, 0)))],
                    store_shards, None, PP)
        return dw

    def proj_dx(name, dp, w):
        (dy,) = _mm(name, (nt, 1, 1),
                    [(dp, full_rows(TM), w, pl.BlockSpec((N_DEV, 512, 256), lambda i, j, k: (0, 0, 0)), "nt_shards", None)],
                    [], [(_sds((T, 512), BF16), full_rows(TM, 512))], store_bf16, None, PP)
        return dy

    def in_dw(name, dact, dact_spec, n):
        def store_pair(acc, ex, o, pids):
            o[0][0] = acc[:, :n].astype(BF16)
            o[0][1] = acc[:, n:].astype(BF16)

        (dw,) = _mm(name, (N_DEV // 2, D // 512, 1), [(h1, tall(512), dact, dact_spec, "tn", None)],
                    [], [(_sds((N_DEV, D, n), BF16), pl.BlockSpec((2, 512, n), lambda j, i, k: (j, i, 0)))],
                    store_pair, None, PP)
        return dw

    dwg = in_dw("gate_proj_dw", dgate, pl.BlockSpec((None, T, 1024), lambda j, i, k: (j // 2, 0, j % 2)), 512)
    dwpa = proj_dw("proj_a_dw", ya, dpa)
    dwpb = proj_dw("proj_b_dw", yb, dpb)
    token = send("proj", dict(w_gate=dwg, w_out=dwo.reshape(N_DEV, D // N_DEV, D), w_proj_a=dwpa, w_proj_b=dwpb))
    dya = proj_dx("proj_a_dx", dpa, wpa)
    dyb = proj_dx("proj_b_dx", dpb, wpb)

    dq_a, dk_a, dv_a = _dilated_bwd(qkv, last_src, _behind(slopes, token), dya, ya, lse_a)
    dq_b, dk_b, dv_b, dtiles = _attn_b_bwd(qkv, tiles, dyb, yb, lse_b)
    drpb = _rpb_reduce(dtiles.reshape(N_HEADS_B * N_DR, GRID_W * GRID_W), sel)[:, :N_DC]
    dqkv = jnp.concatenate([*dq_a, dq_b, *dk_a, dk_b, *dv_a, dv_b], axis=1)
    dwq = in_dw("qkv_proj_dw", dqkv, tall_j(1536), 768)
    half_a = lambda h: pl.BlockSpec((None, TM, D), lambda j, i, k: (h, i, 0))
    half_w = lambda h: pl.BlockSpec((4, 512, 512), lambda j, i, k: (h, j, 0))
    token = send("in", dict(w_qkv=dwq))
    riders = riders(dwq if token is None else token, 4 * nt, lambda j, i, k: j * nt + i) if riders else []
    lead = behind(token)

    def store_and_ride(acc, ex, o, pids):
        o[0][...] = acc.astype(BF16)
        for r, (_, _, run) in enumerate(riders):
            run(ex[len(lead) + 4 * r:len(lead) + 4 * r + 4], o[1 + 4 * r:5 + 4 * r])

    dh1, *rode = _mm("in_proj_dx", (D // 512, nt, 1),
                     [(dqkv, a_ji(3 * D), wq, w_ji(768), "nt_shards", None),
                      (dgate, half_a(0), wg, half_w(0), "nt_shards", None), (dgate, half_a(1), wg, half_w(1), "nt_shards", None)],
                     lead + [e for r in riders for e in r[0]],
                     [(_sds((T, D), BF16), out_ji)] + [o for r in riders for o in r[1]], store_and_ride, None, PP)
    grad_x, dg1 = _rmsnorm_bwd("rmsnorm_mix_bwd", dh1, x, r1, norm_mix, dx2, False)

    small = dict(norm_mix=dg1, b_gate=jnp.concatenate([dba, dbb], axis=1), rpb=drpb, norm_mlp=dg2, norm_final=dg3)
    return loss, grad_x, small, [rode[4 * r:4 * r + 4] for r in range(len(riders))]


BIG = ("w_qkv", "w_gate", "w_proj_a", "w_proj_b", "w_out", "w_up", "w_down")
SMALL = ("norm_mix", "b_gate", "rpb", "norm_mlp", "norm_final")
SMALL_LANES = 96 * 128
RIDING = ("w_up", "w_down")


def _place():
    return lax.axis_index("x"), lax.axis_index("y"), lax.axis_index("c")


def _peer(x, y, c, mask):
    return x ^ (mask >> 2), y ^ ((mask >> 1) & 1), c ^ (mask & 1)


_HBM = pl.BlockSpec(memory_space=pltpu.HBM)
_SEM = pl.BlockSpec(memory_space=pltpu.SEMAPHORE)
_ANY = pl.BlockSpec(memory_space=pl.ANY)
_EFFECT = pltpu.SideEffectType.DATAFLOW_SIDE_EFFECTING


def _hbm(a):
    return pltpu.with_memory_space_constraint(a, pltpu.HBM)


def _other_chips(x, y):
    return [(1 - x, y), (x, 1 - y), (1 - x, 1 - y)]


def _block(x, y, c):
    return 4 * x + 2 * y + c


def _rdma(src, dst, send_sem, recv_sem, to):
    return pltpu.make_async_remote_copy(src_ref=src, dst_ref=dst, send_sem=send_sem, recv_sem=recv_sem, device_id=to,
                                        device_id_type=MESH)


def _gather_start(name, shards):
    n = len(shards)

    def body(*refs):
        srcs, lands = refs[:n], refs[n:2 * n]
        outs = refs[2 * n:]
        send, recv_d2d, recv_ici, local = outs[:n], outs[n:2 * n], outs[2 * n:3 * n], outs[3 * n:4 * n]
        token_ref = outs[6 * n]
        x, y, c = _place()
        for a in range(n):
            mine = lands[a].at[_block(x, y, c)]
            _rdma(srcs[a], mine, send[a].at[0], recv_d2d[a].at[0], (x, y, 1 - c)).start()
            for j, chip in enumerate(_other_chips(x, y)):
                _rdma(srcs[a], mine, send[a].at[1 + j], recv_ici[a].at[j], (*chip, c)).start()
        for a in range(n):
            pltpu.make_async_copy(srcs[a], lands[a].at[_block(x, y, c)], local[a].at[0]).start()
        token_ref[...] = jnp.zeros_like(token_ref)

    lands = [_hbm(lax.empty((N_DEV,) + s.shape, s.dtype)) for s in shards]
    srcs = [_hbm(s) for s in shards]
    dma = pltpu.SemaphoreType.DMA
    res = pl.pallas_call(
        body, name=name,
        out_shape=[dma((4,))] * n + [dma((1,))] * n + [dma((3,))] * n + [dma((1,))] * n
        + [pltpu.HBM(s.shape, s.dtype) for s in srcs] + [pltpu.HBM(l.shape, l.dtype) for l in lands]
        + [jax.ShapeDtypeStruct((8, 128), F32)],
        in_specs=[_HBM] * (2 * n), out_specs=[_SEM] * (4 * n) + [_HBM] * (2 * n) + [pl.BlockSpec(memory_space=pltpu.VMEM)],
        input_output_aliases={i: 4 * n + i for i in range(2 * n)},
        compiler_params=pltpu.CompilerParams(has_side_effects=_EFFECT))(*srcs, *lands)
    keys = ("send", "recv_d2d", "recv_ici", "local", "src", "land")
    return [dict(zip(keys, (res[k * n + a] for k in range(6)))) for a in range(n)], res[6 * n]


def _gather_forward(name, handles, after):
    n = len(handles)
    after = after if isinstance(after, (tuple, list)) else (after,)

    def body(*refs):
        lands, recv_ici = refs[:n], refs[n:2 * n]
        outs = refs[2 * n + len(after):]
        fwd_send, fwd_recv, token_ref = outs[n:2 * n], outs[2 * n:3 * n], outs[3 * n]
        x, y, c = _place()
        for a in range(n):
            for j, chip in enumerate(_other_chips(x, y)):
                blk = lands[a].at[_block(*chip, c)]
                _rdma(blk, blk, fwd_send[a].at[j], recv_ici[a].at[j], (*chip, c)).wait_recv()
                _rdma(blk, blk, fwd_send[a].at[j], fwd_recv[a].at[j], (x, y, 1 - c)).start()
        token_ref[...] = jnp.zeros_like(token_ref)

    dma = pltpu.SemaphoreType.DMA
    lands = [h["land"] for h in handles]
    res = pl.pallas_call(
        body, name=name,
        out_shape=[pltpu.HBM(l.shape, l.dtype) for l in lands] + [dma((3,))] * (2 * n) + [jax.ShapeDtypeStruct((8, 128), F32)],
        in_specs=[_HBM] * n + [_SEM] * n + [_ANY] * len(after),
        out_specs=[_HBM] * n + [_SEM] * (2 * n) + [pl.BlockSpec(memory_space=pltpu.VMEM)],
        input_output_aliases={i: i for i in range(n)},
        compiler_params=pltpu.CompilerParams(has_side_effects=_EFFECT))(*lands, *[h["recv_ici"] for h in handles], *after)
    return [dict(h, land=res[a], fwd_send=res[n + a], fwd_recv=res[2 * n + a]) for a, h in enumerate(handles)], res[3 * n]


def _gather_wait(name, handles, after):
    n = len(handles)

    def body(*refs):
        srcs, lands = refs[:n], refs[n:2 * n]
        send, recv_d2d, local, fwd_send, fwd_recv = (refs[(2 + k) * n:(3 + k) * n] for k in range(5))
        x, y, c = _place()
        sibling = (x, y, 1 - c)
        for a in range(n):
            mine = lands[a].at[_block(x, y, c)]
            _rdma(srcs[a], mine, send[a].at[0], recv_d2d[a].at[0], sibling).wait_send()
            _rdma(srcs[a], lands[a].at[_block(*sibling)], send[a].at[0], recv_d2d[a].at[0], sibling).wait_recv()
            pltpu.make_async_copy(srcs[a], mine, local[a].at[0]).wait()
            for j, chip in enumerate(_other_chips(x, y)):
                _rdma(srcs[a], mine, send[a].at[1 + j], fwd_recv[a].at[j], (*chip, c)).wait_send()
                blk = lands[a].at[_block(*chip, c)]
                _rdma(blk, blk, fwd_send[a].at[j], fwd_recv[a].at[j], sibling).wait_send()
                got = lands[a].at[_block(*chip, 1 - c)]
                _rdma(got, got, fwd_send[a].at[j], fwd_recv[a].at[j], sibling).wait_recv()

    srcs = [h["src"] for h in handles]
    lands = [h["land"] for h in handles]
    sems = [h[k] for k in ("send", "recv_d2d", "local", "fwd_send", "fwd_recv") for h in handles]
    res = pl.pallas_call(
        body, name=name, out_shape=[pltpu.HBM(s.shape, s.dtype) for s in srcs] + [pltpu.HBM(l.shape, l.dtype) for l in lands],
        in_specs=[_HBM] * (2 * n) + [_SEM] * (5 * n) + [_ANY], out_specs=[_HBM] * (2 * n),
        input_output_aliases={i: i for i in range(2 * n)},
        compiler_params=pltpu.CompilerParams(has_side_effects=_EFFECT))(*srcs, *lands, *sems, after)
    return list(res[n:])


def _gather_small(small, after):
    def body(in_ref, *refs):
        out_ref, send_sems, recv_sems, local_sem = refs[len(after):]
        x, y, c = _place()
        me = 4 * x + 2 * y + c
        copies = [pltpu.make_async_copy(in_ref, out_ref.at[me], local_sem)]
        for mask in range(1, N_DEV):
            copies.append(pltpu.make_async_remote_copy(
                src_ref=in_ref, dst_ref=out_ref.at[me], send_sem=send_sems.at[mask - 1],
                recv_sem=recv_sems.at[mask - 1], device_id=_peer(x, y, c, mask), device_id_type=MESH))
        for cp in copies:
            cp.start()
        for cp in copies:
            cp.wait()

    any_spec = pl.BlockSpec(memory_space=pl.ANY)
    return pl.pallas_call(
        body, name="gather_small", in_specs=[any_spec] * (1 + len(after)), out_specs=any_spec,
        out_shape=jax.ShapeDtypeStruct((N_DEV,) + small.shape, small.dtype),
        scratch_shapes=[pltpu.SemaphoreType.DMA((7,)), pltpu.SemaphoreType.DMA((7,)), pltpu.SemaphoreType.DMA])(small, *after)


def _send_copy(src_refs, land_refs, send_sems, recv_sems, a, mask, scatter, x, y, c):
    px, py, pc = _peer(x, y, c, mask)
    src = src_refs[a].at[4 * px + 2 * py + pc] if scatter else src_refs[a]
    return pltpu.make_async_remote_copy(
        src_ref=src, dst_ref=land_refs[a].at[4 * x + 2 * y + c], send_sem=send_sems[a].at[mask - 1],
        recv_sem=recv_sems[a].at[mask - 1], device_id=(px, py, pc), device_id_type=MESH)


def _local_copy(src_refs, land_refs, local_sems, a, scatter, x, y, c):
    me = 4 * x + 2 * y + c
    return pltpu.make_async_copy(src_refs[a].at[me] if scatter else src_refs[a], land_refs[a].at[me], local_sems[a].at[0])


def _send_start(name, srcs, scatter, after=None):
    n = len(srcs)
    n_in = 2 * n + (after is not None)
    shapes = [s.shape[1:] if scatter else s.shape for s in srcs]

    def body(*refs):
        src_refs, land_refs = refs[:n], refs[n:2 * n]
        outs = refs[n_in:]
        send_sems, recv_sems, local_sems = outs[:n], outs[n:2 * n], outs[2 * n:3 * n]
        token_ref = outs[5 * n]
        x, y, c = _place()
        for a in range(n):
            for mask in range(1, N_DEV):
                _send_copy(src_refs, land_refs, send_sems, recv_sems, a, mask, scatter, x, y, c).start()
        for a in range(n):
            _local_copy(src_refs, land_refs, local_sems, a, scatter, x, y, c).start()
        token_ref[...] = jnp.zeros_like(token_ref)

    lands = [pltpu.with_memory_space_constraint(lax.empty((N_DEV,) + sh, s.dtype), pltpu.HBM) for sh, s in zip(shapes, srcs)]
    srcs = [pltpu.with_memory_space_constraint(s, pltpu.HBM) for s in srcs]
    res = pl.pallas_call(
        body, name=name,
        out_shape=[pltpu.SemaphoreType.DMA((7,))] * (2 * n) + [pltpu.SemaphoreType.DMA((1,))] * n
        + [pltpu.HBM(s.shape, s.dtype) for s in srcs] + [pltpu.HBM(l.shape, l.dtype) for l in lands]
        + [jax.ShapeDtypeStruct((8, 128), F32)],
        in_specs=[_HBM] * (2 * n) + [pl.BlockSpec(memory_space=pl.ANY)] * (n_in - 2 * n),
        out_specs=[_SEM] * (3 * n) + [_HBM] * (2 * n) + [pl.BlockSpec(memory_space=pltpu.VMEM)],
        input_output_aliases={i: 3 * n + i for i in range(2 * n)},
        compiler_params=pltpu.CompilerParams(has_side_effects=_EFFECT))(*srcs, *lands, *([] if after is None else [after]))
    handles = [tuple(res[k * n + a] for k in range(5)) for a in range(n)]
    return handles, res[5 * n]


def _send_wait(name, handles, scatter, after):
    n = len(handles)

    def body(*refs):
        src_refs, land_refs = refs[:n], refs[n:2 * n]
        send_sems, recv_sems, local_sems = refs[2 * n:3 * n], refs[3 * n:4 * n], refs[4 * n:5 * n]
        x, y, c = _place()
        for a in range(n):
            for mask in range(1, N_DEV):
                cp = _send_copy(src_refs, land_refs, send_sems, recv_sems, a, mask, scatter, x, y, c)
                cp.wait_send()
                cp.wait_recv()
            _local_copy(src_refs, land_refs, local_sems, a, scatter, x, y, c).wait()

    srcs = [h[3] for h in handles]
    lands = [h[4] for h in handles]
    res = pl.pallas_call(
        body, name=name, out_shape=[pltpu.HBM(s.shape, s.dtype) for s in srcs] + [pltpu.HBM(l.shape, l.dtype) for l in lands],
        in_specs=[_HBM] * (2 * n) + [_SEM] * (3 * n) + [pl.BlockSpec(memory_space=pl.ANY)], out_specs=[_HBM] * (2 * n),
        input_output_aliases={i: i for i in range(2 * n)},
        compiler_params=pltpu.CompilerParams(has_side_effects=_EFFECT))(
            *srcs, *lands, *[h[0] for h in handles], *[h[1] for h in handles], *[h[2] for h in handles], after)
    return list(res[n:])


def _adamw_math(g, w, m, v):
    nm = ADAM_B1 * m + (1.0 - ADAM_B1) * g
    nv = ADAM_B2 * v + (1.0 - ADAM_B2) * jnp.square(g)
    m_hat = nm / (1.0 - ADAM_B1 ** ADAM_STEP)
    v_hat = nv / (1.0 - ADAM_B2 ** ADAM_STEP)
    return -ADAM_LR * (m_hat / (jnp.sqrt(v_hat) + ADAM_EPS) + ADAM_WD * w), nm, nv


def _adamw_small(gathered, ws, ms, vs):
    n = len(ws)
    widths = [w.shape[1] for w in ws]

    def body(p_ref, *refs):
        ins, outs = refs[:3 * n], refs[3 * n:]
        off = 0
        for a in range(n):
            g = p_ref[0, :, off:off + widths[a]]
            for i in range(1, N_DEV):
                g = g + p_ref[i, :, off:off + widths[a]]
            outs[4 * a][...] = g
            outs[4 * a + 1][...], outs[4 * a + 2][...], outs[4 * a + 3][...] = _adamw_math(
                g, ins[a][...], ins[n + a][...], ins[2 * n + a][...])
            off += widths[a]

    whole = lambda shape: pl.BlockSpec(shape, lambda i: (0,) * len(shape))
    rows = [whole((1, wd)) for wd in widths]
    res = _pc(body, name="adamw_small", grid=(1,), in_specs=[whole(gathered.shape)] + rows * 3,
              out_specs=[r for r in rows for _ in range(4)],
              out_shape=[jax.ShapeDtypeStruct((1, wd), F32) for wd in widths for _ in range(4)],
              sem=("arbitrary",))(gathered, *ws, *ms, *vs)
    return [res[4 * a:4 * a + 4] for a in range(n)]


def _adamw(name, parts, w, m, v):
    K, n = w.shape
    tr = min(K, 128)

    def body(p_ref, w_ref, m_ref, v_ref, g_ref, d_ref, nm_ref, nv_ref):
        g = p_ref[0].astype(F32)
        for i in range(1, N_DEV):
            g = g + p_ref[i].astype(F32)
        g_ref[...] = g
        d_ref[...], nm_ref[...], nv_ref[...] = _adamw_math(g, w_ref[...], m_ref[...], v_ref[...])

    blk = pl.BlockSpec((tr, n), lambda i: (i, 0))
    return _pc(body, name=name, grid=(K // tr,), in_specs=[pl.BlockSpec((N_DEV, tr, n), lambda i: (0, i, 0)), blk, blk, blk],
               out_specs=[blk] * 4, out_shape=[jax.ShapeDtypeStruct((K, n), F32)] * 4, sem=("parallel",))(parts, w, m, v)


def _adamw_rider(parts, w, m, v, steps, step_of):
    K, n = w.shape
    tr = K // steps
    blk = pl.BlockSpec((tr, n), lambda *ids: (step_of(*ids), 0))
    extras = [(parts, pl.BlockSpec((N_DEV, tr, n), lambda *ids: (0, step_of(*ids), 0))), (w, blk), (m, blk), (v, blk)]

    def run(ex, o):
        g = ex[0][0].astype(F32)
        for i in range(1, N_DEV):
            g = g + ex[0][i].astype(F32)
        o[0][...] = g
        o[1][...], o[2][...], o[3][...] = _adamw_math(g, ex[1][...], ex[2][...], ex[3][...])

    return extras, [(jax.ShapeDtypeStruct((K, n), F32), blk)] * 4, run


def _lane_padded(size):
    return -(-size // 128) * 128


def _pack_small(vals):
    rows = [jnp.pad(v.reshape(1, -1), ((0, 0), (0, _lane_padded(v.size) - v.size))) for v in vals]
    used = sum(r.shape[1] for r in rows)
    return jnp.concatenate(rows + [jnp.zeros((1, SMALL_LANES - used), F32)], axis=1)


def kernel(x, norm_mix, w_qkv, w_gate, b_gate, rpb, w_proj_a, w_proj_b, w_out, norm_mlp, w_up, w_down, norm_final, loss_target, m_norm_mix, m_w_qkv, m_w_gate, m_b_gate, m_rpb, m_w_proj_a, m_w_proj_b, m_w_out, m_norm_mlp, m_w_up, m_w_down, m_norm_final, v_norm_mix, v_w_qkv, v_w_gate, v_b_gate, v_rpb, v_w_proj_a, v_w_proj_b, v_w_out, v_norm_mlp, v_w_up, v_w_down, v_norm_final):
    w = dict(norm_mix=norm_mix, w_qkv=w_qkv, w_gate=w_gate, b_gate=b_gate, rpb=rpb, w_proj_a=w_proj_a, w_proj_b=w_proj_b,
             w_out=w_out, norm_mlp=norm_mlp, w_up=w_up, w_down=w_down, norm_final=norm_final)
    m = dict(norm_mix=m_norm_mix, w_qkv=m_w_qkv, w_gate=m_w_gate, b_gate=m_b_gate, rpb=m_rpb, w_proj_a=m_w_proj_a,
             w_proj_b=m_w_proj_b, w_out=m_w_out, norm_mlp=m_norm_mlp, w_up=m_w_up, w_down=m_w_down, norm_final=m_norm_final)
    v = dict(norm_mix=v_norm_mix, w_qkv=v_w_qkv, w_gate=v_w_gate, b_gate=v_b_gate, rpb=v_rpb, w_proj_a=v_w_proj_a,
             w_proj_b=v_w_proj_b, w_out=v_w_out, norm_mlp=v_norm_mlp, w_up=v_w_up, w_down=v_w_down, norm_final=v_norm_final)
    order = ("norm_mix", "w_qkv", "w_gate", "b_gate", "rpb", "w_proj_a", "w_proj_b", "w_out", "norm_mlp", "w_up",
             "w_down", "norm_final")

    stages = dict(qkv=("w_qkv",), gate=("w_gate", "w_proj_a", "w_proj_b", "w_out"), mlp=("w_up", "w_down"))
    first, token = _gather_start("gather_start_qkv", [w["w_qkv"][0].astype(BF16)])
    names = [n for s in ("gate", "mlp") for n in stages[s]]
    rest, token = _gather_start("gather_start", [_behind(w[n][0], token).astype(BF16) for n in names])
    handles = dict(zip(names, rest), w_qkv=first[0])

    def late_weights(stage, after):
        if stage == "qkv":
            forwarded, _ = _gather_forward("gather_forward_qkv", [handles["w_qkv"]], after)
            return tuple(_gather_wait("gather_wait_qkv", forwarded, after[0]))
        if stage == "attention":
            handles["gate"], tok = _gather_forward("gather_forward_gate", [handles[n] for n in stages["gate"]], after)
            return (tok,)
        if stage == "mixer":
            handles["up"], tok = _gather_forward("gather_forward_up", [handles["w_up"]], after)
            return tuple(_gather_wait("gather_wait_gate", handles["gate"], after)) + (tok,)
        if stage == "residual":
            handles["down"], tok = _gather_forward("gather_forward_down", [handles["w_down"]], after)
            return (tok,)
        return tuple(_gather_wait("gather_wait_" + stage, handles[stage], after))

    sent = {}

    def send(stage, grads):
        handles, tok = _send_start("grads_start_" + stage, list(grads.values()), True)
        sent[stage] = (tuple(grads), handles)
        return tok

    waited = {}

    def riders(after, steps, step_of):
        landed = {}
        for stage in ("down", "mlp"):
            names, handles = sent[stage]
            waited[stage] = dict(zip(names, _send_wait("grads_wait_" + stage, handles, True, after)))
            after = waited[stage][names[0]]
            landed.update(waited[stage])
        return [_adamw_rider(landed[n], w[n][0], m[n][0], v[n][0], steps, step_of) for n in RIDING]

    loss, grad_x, small, rode = _local_step(x[0], loss_target[0], _behind(norm_mix, token), b_gate, rpb[0], norm_mlp,
                                            norm_final, late_weights, send, riders)

    grad, delta, new_m, new_v = {}, {}, {}, {}
    for n, res in zip(RIDING, rode):
        grad[n], delta[n], new_m[n], new_v[n] = (r[None] for r in res)
    after = grad_x
    for stage in ("down", "mlp", "proj", "small", "in"):
        if stage == "small":
            all_small = after = _gather_small(_pack_small([small[n] for n in SMALL]), [d[0] for d in delta.values()])
            continue
        names, handles = sent[stage]
        landed = waited[stage] if stage in waited else dict(zip(names, _send_wait("grads_wait_" + stage, handles, True, after)))
        for n, parts in landed.items():
            if n in RIDING:
                continue
            res = _adamw("adamw_" + n, parts, w[n][0], m[n][0], v[n][0])
            grad[n], delta[n], new_m[n], new_v[n] = (r[None] for r in res)
            after = res[1]
    as_row = lambda a: jnp.pad(a.reshape(1, -1), ((0, 0), (0, _lane_padded(a.size) - a.size)))
    res = _adamw_small(all_small, *[[as_row(t[n]) for n in SMALL] for t in (w, m, v)])
    for n, rows in zip(SMALL, res):
        for tree, row in zip((grad, delta, new_m, new_v), rows):
            tree[n] = row[0, :w[n].size].reshape(w[n].shape)

    local_loss, _, _ = lax.optimization_barrier((loss[0, 0], res[0][1], delta["w_qkv"]))
    total = lax.psum(local_loss, ("x", "y", "c"))
    return (total, grad_x[None], *[grad[n] for n in order], *[delta[n] for n in order],
            *[new_m[n] for n in order], *[new_v[n] for n in order])
```

```python
import numpy as np
import jax
import jax.numpy as jnp
from jax import lax
from jax.experimental import pallas as pl
from jax.experimental.pallas import tpu as pltpu

F32 = jnp.float32
BF16 = jnp.bfloat16
MESH = pl.DeviceIdType.MESH

D_MODEL = 2048
HEAD_DIM = 128
N_HEADS = 16
N_HEADS_A = 12
N_HEADS_B = 4
DILATIONS = (1, 4, 16)
HALF_WINDOW = 64
GRID_W = 64
NA_ROWS = 8
NA_COLS = 16
D_FF = 4 * D_MODEL
N_DEV = 8
EPS = 1e-6
NEG = -1e30
SCALE = HEAD_DIM ** -0.5
ADAM_LR, ADAM_B1, ADAM_B2, ADAM_EPS, ADAM_WD, ADAM_STEP = 0.001, 0.9, 0.999, 1e-08, 0.01, 10

VMEM_LIMIT = 56 * 1024 * 1024
TM = 512
TL = 1024
TS = 256
TE = 512
BQ_A = 128


def _pc(body, *, name, grid, in_specs, out_specs, out_shape, scratch=(), sem=None):
    return pl.pallas_call(
        body, name=name, grid=grid, in_specs=in_specs, out_specs=out_specs, out_shape=out_shape,
        scratch_shapes=list(scratch),
        compiler_params=pltpu.CompilerParams(dimension_semantics=sem, vmem_limit_bytes=VMEM_LIMIT))


_DOT_DIMS = {"nn": (((1,), (0,)), ((), ())), "nt": (((1,), (1,)), ((), ())), "tn": (((0,), (0,)), ((), ()))}


def _dot(a, b, mode, precision=None):
    if mode == "nn_shards":
        return jnp.concatenate([_dot(a, b[s], "nn") for s in range(b.shape[0])], axis=1)
    if mode == "nt_shards":
        n = b.shape[2]
        out = _dot(a[:, :n], b[0], "nt")
        for s in range(1, b.shape[0]):
            out = out + _dot(a[:, s * n:(s + 1) * n], b[s], "nt")
        return out
    return lax.dot_general(a, b, _DOT_DIMS[mode], preferred_element_type=F32, precision=precision)


def _mm(name, grid, pairs, extras, outs, epilogue, acc_shape, sem):
    n_pairs, n_extra, n_out = len(pairs), len(extras), len(outs)
    k_axis = len(grid) - 1
    nk = grid[k_axis]

    def body(*refs):
        pids = [pl.program_id(ax) for ax in range(len(grid))]
        ab = refs[:2 * n_pairs]
        ex = refs[2 * n_pairs:2 * n_pairs + n_extra]
        o = refs[2 * n_pairs + n_extra:2 * n_pairs + n_extra + n_out]
        part = None
        for p, pair in enumerate(pairs):
            a = ab[2 * p][...]
            if pair[5] is not None:
                a = pair[5](a)
            d = _dot(a, ab[2 * p + 1][...], pair[4])
            part = d if part is None else part + d
        if nk == 1:
            epilogue(part, ex, o, pids)
        else:
            acc = refs[-1]
            k = pids[k_axis]

            @pl.when(k == 0)
            def _():
                acc[...] = part

            @pl.when(k > 0)
            def _():
                acc[...] += part

            @pl.when(k == nk - 1)
            def _():
                epilogue(acc[...], ex, o, pids)

    operands, in_specs = [], []
    for a, a_spec, b, b_spec, _, _ in pairs:
        operands += [a, b]
        in_specs += [a_spec, b_spec]
    for e, e_spec in extras:
        operands.append(e)
        in_specs.append(e_spec)
    res = _pc(body, name=name, grid=grid, in_specs=in_specs, out_specs=[s for _, s in outs],
              out_shape=[o for o, _ in outs], scratch=[pltpu.VMEM(acc_shape, F32)] if nk > 1 else [], sem=sem)(*operands)
    return res


def _rmsnorm_fwd(x, g, name):
    T, D = x.shape

    def body(x_ref, g_ref, h_ref, r_ref):
        xv = x_ref[...]
        r = lax.rsqrt(jnp.mean(xv * xv, axis=-1, keepdims=True) + EPS)
        h_ref[...] = (xv * r * g_ref[...]).astype(BF16)
        r_ref[...] = r

    return _pc(body, name=name, grid=(T // TM,),
               in_specs=[pl.BlockSpec((TM, D), lambda i: (i, 0)), pl.BlockSpec((1, D), lambda i: (0, 0))],
               out_specs=[pl.BlockSpec((TM, D), lambda i: (i, 0)), pl.BlockSpec((TM, 1), lambda i: (i, 0))],
               out_shape=[jax.ShapeDtypeStruct((T, D), BF16), jax.ShapeDtypeStruct((T, 1), F32)],
               sem=("parallel",))(x, g)


def _rmsnorm_bwd_block(dh, x, r, g, dres):
    xh = x * r
    dxh = dh * g
    dx = dres + r * (dxh - xh * jnp.mean(dxh * xh, axis=-1, keepdims=True))
    return dx, jnp.sum(dh * xh, axis=0, keepdims=True)


def _rmsnorm_bwd(name, dh, x, r, g, dres, want_bf16):
    T, D = x.shape

    def body(dh_ref, x_ref, r_ref, g_ref, dres_ref, *outs):
        dx, dgp = _rmsnorm_bwd_block(dh_ref[...].astype(F32), x_ref[...], r_ref[...], g_ref[...], dres_ref[...])
        outs[0][...] = dx
        if want_bf16:
            outs[1][...] = dx.astype(BF16)
        first = pl.program_id(0) == 0

        @pl.when(first)
        def _():
            outs[-1][...] = dgp

        @pl.when(jnp.logical_not(first))
        def _():
            outs[-1][...] += dgp

    row = pl.BlockSpec((TE, D), lambda i: (i, 0))
    vec = pl.BlockSpec((1, D), lambda i: (0, 0))
    n_row = 2 if want_bf16 else 1
    return _pc(body, name=name, grid=(T // TE,), in_specs=[row, row, pl.BlockSpec((TE, 1), lambda i: (i, 0)), vec, row],
               out_specs=[row] * n_row + [vec],
               out_shape=[jax.ShapeDtypeStruct((T, D), F32)] + [jax.ShapeDtypeStruct((T, D), BF16)] * (n_row - 1)
               + [jax.ShapeDtypeStruct((1, D), F32)], sem=("arbitrary",))(dh, x, r, g, dres)


def _final_norm_loss(x3, g, tgt):
    T, D = x3.shape

    def body(x_ref, g_ref, t_ref, loss_ref, dx_ref, dxb_ref, dg_ref):
        i = pl.program_id(0)
        xv = x_ref[...]
        gv = g_ref[...]
        r = lax.rsqrt(jnp.mean(xv * xv, axis=-1, keepdims=True) + EPS)
        xh = xv * r
        e = xh * gv - t_ref[...]
        part = 0.5 * jnp.sum(jnp.mean(e * e, axis=-1, keepdims=True))
        dy = e * (1.0 / D)
        dxh = dy * gv
        dx = r * (dxh - xh * jnp.mean(dxh * xh, axis=-1, keepdims=True))
        dx_ref[...] = dx
        dxb_ref[...] = dx.astype(BF16)
        dgp = jnp.sum(dy * xh, axis=0, keepdims=True)

        @pl.when(i == 0)
        def _():
            loss_ref[...] = jnp.zeros(loss_ref.shape, F32) + part
            dg_ref[...] = dgp

        @pl.when(i > 0)
        def _():
            loss_ref[...] += part
            dg_ref[...] += dgp

    row = pl.BlockSpec((TE, D), lambda i: (i, 0))
    vec = pl.BlockSpec((1, D), lambda i: (0, 0))
    return _pc(body, name="final_norm_loss", grid=(T // TE,), in_specs=[row, vec, row],
               out_specs=[pl.BlockSpec((8, 128), lambda i: (0, 0)), row, row, vec],
               out_shape=[jax.ShapeDtypeStruct((8, 128), F32), jax.ShapeDtypeStruct((T, D), F32),
                          jax.ShapeDtypeStruct((T, D), BF16), jax.ShapeDtypeStruct((1, D), F32)],
               sem=("arbitrary",))(x3, g, tgt)


def _alibi_slopes():
    return np.asarray(2.0 ** (-8.0 * np.arange(1, N_HEADS_A + 1) / N_HEADS_A), dtype=np.float32)


def _col_to_row(col):
    return jnp.transpose(jnp.broadcast_to(col, (BQ_A, BQ_A)))[0:1, :]


def _row_to_col(row):
    return jnp.transpose(jnp.broadcast_to(row, (BQ_A, BQ_A)))[:, 0:1]


def _row_spec(rows_per_step, first=0):
    return pl.BlockSpec((None, rows_per_step // BQ_A, BQ_A), lambda j, i: (first + j, i, 0))


def _a_windows(T):
    half = [HALF_WINDOW * d for d in DILATIONS]
    return half, [min(BQ_A + 2 * w, T) for w in half]


def _a_bias(slope, g, offset, wk, half):
    d = DILATIONS[g]
    dlt = offset + lax.broadcasted_iota(jnp.int32, (BQ_A, wk), 1) - lax.broadcasted_iota(jnp.int32, (BQ_A, wk), 0)
    ad = jnp.maximum(dlt, -dlt)
    ok = (ad <= half) & ((ad & (d - 1)) == 0)
    return jnp.where(ok, -slope * ad.astype(F32), NEG)


def _a_scores(q, kw, slope, g, offset, wk, half):
    return _dot(q, kw, "nt") * SCALE + _a_bias(slope, g, offset, wk, half)


N_JOINT = 2


SUB_J = 8
BS_J = SUB_J * BQ_A


def _a_qkv_specs(T):
    specs = []
    for g in range(N_JOINT):
        specs.append(pl.BlockSpec((BS_J, HEAD_DIM), lambda j, i, g=g: (i, 4 * g + j)))
        specs.append(pl.BlockSpec((T, HEAD_DIM), lambda j, i, g=g: (0, N_HEADS + 4 * g + j)))
        specs.append(pl.BlockSpec((T, HEAD_DIM), lambda j, i, g=g: (0, 2 * N_HEADS + 4 * g + j)))
    return specs


def _a_window(step, u, g, T, half, wks):
    t0 = (step * SUB_J + u) * BQ_A
    ws = pl.multiple_of(jnp.clip(t0 - half[g], 0, T - wks[g]), 64)
    return pl.ds(ws, wks[g]), ws - t0


def _attn_a_fwd(qkv, slopes, o_last, lse_last):
    T = qkv.shape[0]
    half, wks = _a_windows(T)

    def body(sl_ref, *refs):
        qkv_refs, (o_ref, l_ref, y_ref, lse_ref) = refs[:3 * N_JOINT], refs[3 * N_JOINT:]
        j = pl.program_id(0)
        for u in range(SUB_J):
            rows = slice(u * BQ_A, (u + 1) * BQ_A)
            ss, vws = [], []
            lse_last = _row_to_col(l_ref[u:u + 1, :])
            m = lse_last
            for g in range(N_JOINT):
                q_ref, k_ref, v_ref = qkv_refs[3 * g:3 * g + 3]
                win, offset = _a_window(pl.program_id(1), u, g, T, half, wks)
                ss.append(_a_scores(q_ref[rows, :], k_ref[win, :], sl_ref[4 * g + j], g, offset, wks[g], half[g]))
                vws.append(v_ref[win, :])
                m = jnp.maximum(m, jnp.max(ss[g], axis=-1, keepdims=True))
            den = jnp.exp(lse_last - m)
            acc = den * o_ref[rows, :].astype(F32)
            for g in range(N_JOINT):
                p = jnp.exp(ss[g] - m)
                den = den + jnp.sum(p, axis=-1, keepdims=True)
                acc = acc + _dot(p.astype(BF16), vws[g], "nn")
            y_ref[rows, :] = (acc / den).astype(BF16)
            lse_ref[u:u + 1, :] = _col_to_row(m + jnp.log(den))

    blk = pl.BlockSpec((BS_J, HEAD_DIM), lambda j, i: (i, j))
    col = _row_spec(BS_J)
    return _pc(body, name="attn_a_fwd", grid=(4, T // BS_J),
               in_specs=[pl.BlockSpec(memory_space=pltpu.SMEM)] + _a_qkv_specs(T) + [blk, col], out_specs=[blk, col],
               out_shape=[jax.ShapeDtypeStruct((T, 4 * HEAD_DIM), BF16), jax.ShapeDtypeStruct((4, T // BQ_A, BQ_A), F32)],
               sem=("parallel", "parallel"))(slopes, *([qkv] * (3 * N_JOINT)), o_last, lse_last)


def _attn_a_bwd(qkv, slopes, dy, y, lse):
    T = qkv.shape[0]
    steps = T // BS_J
    half, wks = _a_windows(T)

    def body(sl_ref, *refs):
        qkv_refs = refs[:3 * N_JOINT]
        dy_ref, y_ref, lse_ref, dq_ref, dk_ref, dv_ref, delta_ref, dk_acc, dv_acc = refs[3 * N_JOINT:]
        j = pl.program_id(0)
        step = pl.program_id(1)

        @pl.when(step == 0)
        def _():
            dk_acc[...] = jnp.zeros_like(dk_acc)
            dv_acc[...] = jnp.zeros_like(dv_acc)

        for u in range(SUB_J):
            rows = slice(u * BQ_A, (u + 1) * BQ_A)
            dyv = dy_ref[rows, :]
            delta = jnp.sum(dyv.astype(F32) * y_ref[rows, :].astype(F32), axis=-1, keepdims=True)
            delta_ref[u:u + 1, :] = _col_to_row(delta)
            lse_v = _row_to_col(lse_ref[u:u + 1, :])
            for g in range(N_JOINT):
                q_ref, k_ref, v_ref = qkv_refs[3 * g:3 * g + 3]
                win, offset = _a_window(step, u, g, T, half, wks)
                qv = q_ref[rows, :]
                kw = k_ref[win, :]
                s = _a_scores(qv, kw, sl_ref[4 * g + j], g, offset, wks[g], half[g])
                p = jnp.exp(s - lse_v)
                dp = _dot(dyv, v_ref[win, :], "nt")
                ds = (p * (dp - delta)).astype(BF16)
                dq_ref[g, rows, :] = (_dot(ds, kw, "nn") * SCALE).astype(BF16)
                dk_acc[g, win, :] += _dot(ds, qv, "tn") * SCALE
                dv_acc[g, win, :] += _dot(p.astype(BF16), dyv, "tn")

        @pl.when(step == steps - 1)
        def _():
            dk_ref[...] = dk_acc[...].astype(BF16)
            dv_ref[...] = dv_acc[...].astype(BF16)

    blk = pl.BlockSpec((BS_J, HEAD_DIM), lambda j, i: (i, j))
    col = _row_spec(BS_J)
    whole = pl.BlockSpec((N_JOINT, T, HEAD_DIM), lambda j, i: (0, 0, j))
    full = jax.ShapeDtypeStruct((N_JOINT, T, 4 * HEAD_DIM), BF16)
    return _pc(body, name="attn_a_bwd", grid=(4, steps),
               in_specs=[pl.BlockSpec(memory_space=pltpu.SMEM)] + _a_qkv_specs(T) + [blk, blk, col],
               out_specs=[pl.BlockSpec((N_JOINT, BS_J, HEAD_DIM), lambda j, i: (0, i, j)), whole, whole, col],
               out_shape=[full, full, full, jax.ShapeDtypeStruct((4, T // BQ_A, BQ_A), F32)],
               scratch=[pltpu.VMEM((N_JOINT, T, HEAD_DIM), F32), pltpu.VMEM((N_JOINT, T, HEAD_DIM), F32)],
               sem=("parallel", "arbitrary"))(slopes, *([qkv] * (3 * N_JOINT)), dy, y, lse)


SUB_A = 8


def _to_residue_major(a, d, axis=0):
    if d == 1:
        return a
    sh = a.shape
    a = a.reshape(sh[:axis] + (sh[axis] // d, d) + sh[axis + 1:])
    return jnp.swapaxes(a, axis, axis + 1).reshape(sh)


def _to_token_order(a, d, axis=0):
    if d == 1:
        return a
    sh = a.shape
    a = a.reshape(sh[:axis] + (d, sh[axis] // d) + sh[axis + 1:])
    return jnp.swapaxes(a, axis, axis + 1).reshape(sh)


def _band_bias(slope, offset, wk):
    dlt = offset + lax.broadcasted_iota(jnp.int32, (BQ_A, wk), 1) - lax.broadcasted_iota(jnp.int32, (BQ_A, wk), 0)
    ad = jnp.maximum(dlt, -dlt)
    return jnp.where(ad <= HALF_WINDOW, -slope * ad.astype(F32), NEG)


def _band_window(step, u, seg, wk):
    t0 = (step * SUB_A + u) * BQ_A
    lo = (t0 // seg) * seg
    ws = pl.multiple_of(jnp.clip(t0 - HALF_WINDOW, lo, lo + seg - wk), 64)
    return slice(u * BQ_A, (u + 1) * BQ_A), pl.ds(ws, wk), ws - t0


def _band_specs(T, cols):
    bs = SUB_A * BQ_A
    assert T % bs == 0 and T // max(DILATIONS) >= BQ_A, "a query sub-block must lie inside one segment"
    return [pl.BlockSpec(memory_space=pltpu.SMEM),
            pl.BlockSpec((bs, HEAD_DIM), lambda j, i: (i, cols[0] + j)),
            pl.BlockSpec((T, HEAD_DIM), lambda j, i: (0, cols[1] + j)),
            pl.BlockSpec((T, HEAD_DIM), lambda j, i: (0, cols[2] + j))]


def _banded_fwd(name, src, cols, seg, slopes):
    T = src.shape[0]
    bs = SUB_A * BQ_A
    wk = min(BQ_A + 2 * HALF_WINDOW, seg)

    def body(sl_ref, q_ref, k_ref, v_ref, o_ref, lse_ref):
        slope = sl_ref[pl.program_id(0)]
        for u in range(SUB_A):
            rows, win, offset = _band_window(pl.program_id(1), u, seg, wk)
            s = _dot(q_ref[rows, :], k_ref[win, :], "nt") * SCALE + _band_bias(slope, offset, wk)
            m = jnp.max(s, axis=-1, keepdims=True)
            p = jnp.exp(s - m)
            den = jnp.sum(p, axis=-1, keepdims=True)
            o_ref[rows, :] = (_dot(p.astype(BF16), v_ref[win, :], "nn") / den).astype(BF16)
            lse_ref[u:u + 1, :] = _col_to_row(m + jnp.log(den))

    return _pc(body, name=name, grid=(4, T // bs), in_specs=_band_specs(T, cols),
               out_specs=[pl.BlockSpec((bs, HEAD_DIM), lambda j, i: (i, j)), _row_spec(bs)],
               out_shape=[jax.ShapeDtypeStruct((T, 4 * HEAD_DIM), BF16), jax.ShapeDtypeStruct((4, T // BQ_A, BQ_A), F32)],
               sem=("parallel", "parallel"))(slopes, src, src, src)


def _banded_bwd(name, src, cols, seg, slopes, dy, delta_lse):
    T = src.shape[0]
    bs = SUB_A * BQ_A
    steps = T // bs
    wk = min(BQ_A + 2 * HALF_WINDOW, seg)

    def body(sl_ref, q_ref, k_ref, v_ref, dy_ref, delta_ref, lse_ref, dq_ref, dk_ref, dv_ref, dk_acc, dv_acc):
        slope = sl_ref[pl.program_id(0)]
        step = pl.program_id(1)

        @pl.when(step == 0)
        def _():
            dk_acc[...] = jnp.zeros_like(dk_acc)
            dv_acc[...] = jnp.zeros_like(dv_acc)

        for u in range(SUB_A):
            rows, win, offset = _band_window(step, u, seg, wk)
            qv, kw, dyv = q_ref[rows, :], k_ref[win, :], dy_ref[rows, :]
            s = _dot(qv, kw, "nt") * SCALE + _band_bias(slope, offset, wk)
            p = jnp.exp(s - _row_to_col(lse_ref[u:u + 1, :]))
            ds = (p * (_dot(dyv, v_ref[win, :], "nt") - _row_to_col(delta_ref[u:u + 1, :]))).astype(BF16)
            dq_ref[rows, :] = (_dot(ds, kw, "nn") * SCALE).astype(BF16)
            dk_acc[win, :] += _dot(ds, qv, "tn") * SCALE
            dv_acc[win, :] += _dot(p.astype(BF16), dyv, "tn")

        @pl.when(step == steps - 1)
        def _():
            dk_ref[...] = dk_acc[...].astype(BF16)
            dv_ref[...] = dv_acc[...].astype(BF16)

    blk = pl.BlockSpec((bs, HEAD_DIM), lambda j, i: (i, j))
    whole = pl.BlockSpec((T, HEAD_DIM), lambda j, i: (0, j))
    full = jax.ShapeDtypeStruct((T, 4 * HEAD_DIM), BF16)
    return _pc(body, name=name, grid=(4, steps),
               in_specs=_band_specs(T, cols) + [blk, _row_spec(bs), _row_spec(bs, 4)],
               out_specs=[blk, whole, whole], out_shape=[full, full, full],
               scratch=[pltpu.VMEM((T, HEAD_DIM), F32), pltpu.VMEM((T, HEAD_DIM), F32)],
               sem=("parallel", "arbitrary"))(slopes, src, src, src, dy, delta_lse, delta_lse)


def _last_group_source(qkv):
    third = qkv.shape[1] // 3
    own = jnp.concatenate([qkv[:, t * third + 512 * N_JOINT:t * third + 512 * (N_JOINT + 1)] for t in range(3)], axis=1)
    return _to_residue_major(own, DILATIONS[N_JOINT])


def _dilated_fwd(qkv, slopes):
    d = DILATIONS[N_JOINT]
    last_src = _last_group_source(qkv)
    o, lse = _banded_fwd("attn_a_last_fwd", last_src, (0, 4, 8), qkv.shape[0] // d, slopes[4 * N_JOINT:] * float(d))
    lse = _to_token_order(lse.reshape(4, -1), d, axis=1).reshape(lse.shape)
    y, joint_lse = _attn_a_fwd(qkv, slopes, _to_token_order(o, d), lse)
    return y, joint_lse, last_src


def _dilated_bwd(qkv, last_src, slopes, dy, y, lse):
    d = DILATIONS[N_JOINT]
    dq, dk, dv, delta = _attn_a_bwd(qkv, slopes, dy, y, lse)
    delta_lse = jnp.concatenate([delta, lse], axis=0)
    delta_lse = _to_residue_major(delta_lse.reshape(8, -1), d, axis=1).reshape(delta_lse.shape)
    last = _banded_bwd("attn_a_last_bwd", last_src, (0, 4, 8), qkv.shape[0] // d,
                       slopes[4 * N_JOINT:] * float(d), _to_residue_major(dy, d), delta_lse)
    return [[joint[g] for g in range(N_JOINT)] + [_to_token_order(own, d)] for joint, own in zip((dq, dk, dv), last)]


N_DR = 2 * NA_ROWS - 1
N_DC = 2 * NA_COLS - 1
WK_B = NA_ROWS * GRID_W


def _toeplitz_consts():
    col = np.arange(GRID_W)
    start = np.clip(col - NA_COLS // 2, 0, GRID_W - NA_COLS)
    ok = (col[None, :] >= start[:, None]) & (col[None, :] < start[:, None] + NA_COLS)
    dc = np.clip(col[None, :] - col[:, None], -(NA_COLS - 1), NA_COLS - 1) + NA_COLS - 1
    sel = np.zeros((128, GRID_W * GRID_W), np.float32)
    sel[dc.reshape(-1), np.arange(GRID_W * GRID_W)] = ok.reshape(-1).astype(np.float32)
    mask = np.where(ok, 0.0, NEG).astype(np.float32).reshape(1, -1)
    return sel, mask


def _rpb_expand(rpb_rows, sel, mask):
    def body(r_ref, s_ref, m_ref, o_ref):
        o_ref[...] = _dot(r_ref[...], s_ref[...], "nn", precision=lax.Precision.HIGHEST) + m_ref[...]

    n = rpb_rows.shape[0]
    return _pc(body, name="rpb_expand", grid=(1,),
               in_specs=[pl.BlockSpec((n, 128), lambda i: (0, 0)), pl.BlockSpec(sel.shape, lambda i: (0, 0)),
                         pl.BlockSpec(mask.shape, lambda i: (0, 0))],
               out_specs=pl.BlockSpec((n, sel.shape[1]), lambda i: (0, 0)),
               out_shape=jax.ShapeDtypeStruct((n, sel.shape[1]), F32), sem=("arbitrary",))(rpb_rows, sel, mask)


def _rpb_reduce(dtiles, sel):
    def body(t_ref, s_ref, o_ref):
        o_ref[...] = _dot(t_ref[...], s_ref[...], "nt", precision=lax.Precision.HIGHEST)

    n = dtiles.shape[0]
    return _pc(body, name="rpb_reduce", grid=(1,),
               in_specs=[pl.BlockSpec(dtiles.shape, lambda i: (0, 0)), pl.BlockSpec(sel.shape, lambda i: (0, 0))],
               out_specs=pl.BlockSpec((n, 128), lambda i: (0, 0)),
               out_shape=jax.ShapeDtypeStruct((n, 128), F32), sem=("arbitrary",))(dtiles, sel)


ROWS_B = 16
BQ_B = ROWS_B * GRID_W


def _b_specs(T):
    return [pl.BlockSpec((BQ_B, HEAD_DIM), lambda h, r: (r, N_HEADS_A + h)),
            pl.BlockSpec((T, HEAD_DIM), lambda h, r: (0, N_HEADS + N_HEADS_A + h)),
            pl.BlockSpec((T, HEAD_DIM), lambda h, r: (0, 2 * N_HEADS + N_HEADS_A + h)),
            pl.BlockSpec((None, N_DR, GRID_W, GRID_W), lambda h, r: (h, 0, 0, 0))]


def _b_row(step, u, rows, t_ref, bias_ref):
    r = step * ROWS_B + u
    rs = jnp.clip(r - NA_ROWS // 2, 0, rows - NA_ROWS)
    first_tile = rs - r + NA_ROWS - 1
    for a in range(NA_ROWS):
        bias_ref[u, :, a * GRID_W:(a + 1) * GRID_W] = t_ref[first_tile + a]
    return slice(u * GRID_W, (u + 1) * GRID_W), pl.ds(pl.multiple_of(rs * GRID_W, GRID_W), WK_B), first_tile


def _attn_b_fwd(qkv, tiles):
    T = qkv.shape[0]
    rows = T // GRID_W

    def body(q_ref, k_ref, v_ref, t_ref, y_ref, lse_ref, bias_ref):
        for u in range(ROWS_B):
            blk, win, _ = _b_row(pl.program_id(1), u, rows, t_ref, bias_ref)
            s = _dot(q_ref[blk, :], k_ref[win, :], "nt") * SCALE + bias_ref[u]
            m = jnp.max(s, axis=-1, keepdims=True)
            p = jnp.exp(s - m)
            den = jnp.sum(p, axis=-1, keepdims=True)
            y_ref[blk, :] = (_dot(p.astype(BF16), v_ref[win, :], "nn") / den).astype(BF16)
            lse_ref[blk, :] = m + jnp.log(den)

    return _pc(body, name="attn_b_fwd", grid=(N_HEADS_B, rows // ROWS_B), in_specs=_b_specs(T),
               out_specs=[pl.BlockSpec((BQ_B, HEAD_DIM), lambda h, r: (r, h)),
                          pl.BlockSpec((None, BQ_B, 1), lambda h, r: (h, r, 0))],
               out_shape=[jax.ShapeDtypeStruct((T, N_HEADS_B * HEAD_DIM), BF16),
                          jax.ShapeDtypeStruct((N_HEADS_B, T, 1), F32)],
               scratch=[pltpu.VMEM((ROWS_B, GRID_W, WK_B), F32)], sem=("parallel", "parallel"))(qkv, qkv, qkv, tiles)


def _attn_b_bwd(qkv, tiles, dy, y, lse):
    T = qkv.shape[0]
    rows = T // GRID_W
    steps = rows // ROWS_B

    def body(q_ref, k_ref, v_ref, t_ref, dy_ref, y_ref, lse_ref, dq_ref, dk_ref, dv_ref, dt_ref,
             bias_ref, ds_ref, dk_acc, dv_acc):
        step = pl.program_id(1)

        @pl.when(step == 0)
        def _():
            dk_acc[...] = jnp.zeros_like(dk_acc)
            dv_acc[...] = jnp.zeros_like(dv_acc)
            dt_ref[...] = jnp.zeros_like(dt_ref)

        for u in range(ROWS_B):
            blk, win, first_tile = _b_row(step, u, rows, t_ref, bias_ref)
            qv = q_ref[blk, :]
            kw = k_ref[win, :]
            dyv = dy_ref[blk, :]
            s = _dot(qv, kw, "nt") * SCALE + bias_ref[u]
            p = jnp.exp(s - lse_ref[blk, :])
            delta = jnp.sum(dyv.astype(F32) * y_ref[blk, :].astype(F32), axis=-1, keepdims=True)
            ds = p * (_dot(dyv, v_ref[win, :], "nt") - delta)
            ds_ref[u] = ds
            for a in range(NA_ROWS):
                dt_ref[first_tile + a] += ds_ref[u, :, a * GRID_W:(a + 1) * GRID_W]
            dsb = ds.astype(BF16)
            dq_ref[blk, :] = (_dot(dsb, kw, "nn") * SCALE).astype(BF16)
            dk_acc[win, :] += _dot(dsb, qv, "tn") * SCALE
            dv_acc[win, :] += _dot(p.astype(BF16), dyv, "tn")

        @pl.when(step == steps - 1)
        def _():
            dk_ref[...] = dk_acc[...].astype(BF16)
            dv_ref[...] = dv_acc[...].astype(BF16)

    blk_spec = pl.BlockSpec((BQ_B, HEAD_DIM), lambda h, r: (r, h))
    whole = pl.BlockSpec((T, HEAD_DIM), lambda h, r: (0, h))
    full = jax.ShapeDtypeStruct((T, N_HEADS_B * HEAD_DIM), BF16)
    return _pc(body, name="attn_b_bwd", grid=(N_HEADS_B, steps),
               in_specs=_b_specs(T) + [blk_spec, blk_spec, pl.BlockSpec((None, BQ_B, 1), lambda h, r: (h, r, 0))],
               out_specs=[blk_spec, whole, whole, pl.BlockSpec((None, N_DR, GRID_W, GRID_W), lambda h, r: (h, 0, 0, 0))],
               out_shape=[full, full, full, jax.ShapeDtypeStruct((N_HEADS_B, N_DR, GRID_W, GRID_W), F32)],
               scratch=[pltpu.VMEM((ROWS_B, GRID_W, WK_B), F32), pltpu.VMEM((ROWS_B, GRID_W, WK_B), F32),
                        pltpu.VMEM((T, HEAD_DIM), F32), pltpu.VMEM((T, HEAD_DIM), F32)],
               sem=("parallel", "arbitrary"))(qkv, qkv, qkv, tiles, dy, y, lse)


def _sds(shape, dtype):
    return jax.ShapeDtypeStruct(shape, dtype)


def _behind(value, token):
    return value if token is None else value + token[0, 0]


def _local_step(x, tgt, norm_mix, b_gate, rpb, norm_mlp, norm_final, late_weights, send, riders=None):
    T, D = x.shape
    nt = T // TM
    PP = ("parallel", "parallel", "arbitrary")
    slopes = jnp.asarray(_alibi_slopes())
    sel_np, mask_np = _toeplitz_consts()
    sel, mask = jnp.asarray(sel_np), jnp.asarray(mask_np)

    def rows(w, tm=TM):
        return pl.BlockSpec((tm, w), lambda i, j, k: (i, j))

    def full_rows(tm, w=D):
        return pl.BlockSpec((tm, w), lambda i, j, k: (i, 0))

    def tall(w):
        return pl.BlockSpec((T, w), lambda j, i, k: (0, i))

    def tall_j(w):
        return pl.BlockSpec((T, w), lambda j, i, k: (0, j))

    gvec = pl.BlockSpec((1, D), lambda i, j, k: (0, 0))

    h1, r1 = _rmsnorm_fwd(x, norm_mix, "rmsnorm_mix")
    rpb_rows = jnp.pad(rpb.reshape(N_HEADS_B * N_DR, N_DC), ((0, 0), (0, 128 - N_DC)))
    tiles = _rpb_expand(rpb_rows, sel, mask).reshape(N_HEADS_B, N_DR, GRID_W, GRID_W)
    (wq,) = late_weights("qkv", (h1, tiles))

    def store_bf16(acc, ex, o, pids):
        o[0][...] = acc.astype(BF16)

    (qkv,) = _mm("qkv_proj", (T // TL, N_DEV // 2, 1),
                 [(h1, full_rows(TL), wq, pl.BlockSpec((2, D, 768), lambda i, j, k: (j, 0, 0)), "nn_shards", None)],
                 [], [(_sds((T, 3 * D), BF16), rows(1536, TL))], store_bf16, None, PP)

    def gate_epi(acc, ex, o, pids):
        o[0][...] = jax.nn.sigmoid(acc + ex[0][...]).astype(BF16)

    late_weights("attention", qkv)
    ya, lse_a, last_src = _dilated_fwd(qkv, slopes)
    yb, lse_b = _attn_b_fwd(qkv, tiles)

    wg, wpa, wpb, wo, token = late_weights("mixer", ya)
    (gates,) = _mm("gate_proj", (T // TL, N_DEV // 2, 1),
                   [(h1, full_rows(TL), wg, pl.BlockSpec((2, D, 512), lambda i, j, k: (j, 0, 0)), "nn_shards", None)],
                   [(_behind(b_gate, token), pl.BlockSpec((1, 1024), lambda i, j, k: (0, j)))],
                   [(_sds((T, 2 * D), BF16), rows(1024, TL))], gate_epi, None, PP)

    def merge_body(ya_ref, wa_ref, yb_ref, wb_ref, ga_ref, gb_ref, pa_ref, pb_ref, mg_ref):
        ya_v, yb_v = ya_ref[...], yb_ref[...]
        for s in range(N_DEV):
            cols = slice(s * 256, (s + 1) * 256)
            pa = _dot(ya_v, wa_ref[s], "nn")
            pb = _dot(yb_v, wb_ref[s], "nn")
            pa_ref[:, cols] = pa.astype(BF16)
            pb_ref[:, cols] = pb.astype(BF16)
            mg_ref[:, cols] = (ga_ref[:, cols] * pa + gb_ref[:, cols] * pb).astype(BF16)

    wo2 = wo.reshape(D, D)
    y_spec = pl.BlockSpec((TS, 512), lambda i: (i, 0))
    wp_spec = pl.BlockSpec((N_DEV, 512, 256), lambda i: (0, 0, 0))
    c_spec = pl.BlockSpec((TS, D), lambda i: (i, 0))
    pa, pb, merged = _pc(merge_body, name="merge_proj", grid=(T // TS,),
                         in_specs=[y_spec, wp_spec, y_spec, wp_spec, c_spec, pl.BlockSpec((TS, D), lambda i: (i, 1))],
                         out_specs=[c_spec, c_spec, c_spec],
                         out_shape=[_sds((T, D), BF16)] * 3, sem=("parallel",))(ya, wpa, yb, wpb, gates, gates)

    def add_res(acc, ex, o, pids):
        o[0][...] = ex[0][...] + acc

    def out_norm_epi(acc, ex, o, pids):
        x2v = ex[0][...] + acc
        r = lax.rsqrt(jnp.mean(x2v * x2v, axis=-1, keepdims=True) + EPS)
        o[0][...] = x2v
        o[1][...] = (x2v * r * ex[1][...]).astype(BF16)
        o[2][...] = r

    x2, h2, r2 = _mm("out_proj", (nt, 1, 1),
                     [(merged, full_rows(TM), wo2, pl.BlockSpec((D, D), lambda i, j, k: (0, 0)), "nn", None)],
                     [(x, full_rows(TM)), (norm_mlp, gvec)],
                     [(_sds((T, D), F32), full_rows(TM)), (_sds((T, D), BF16), full_rows(TM)),
                      (_sds((T, 1), F32), pl.BlockSpec((TM, 1), lambda i, j, k: (i, 0)))], out_norm_epi, None, PP)

    (token,) = late_weights("residual", x2)
    (wu,) = late_weights("up", h2)
    behind = lambda token: [] if token is None else [(token, pl.BlockSpec(memory_space=pl.ANY))]

    def up_epi(acc, ex, o, pids):
        o[0][...] = acc.astype(BF16)
        r = jnp.maximum(acc, 0.0)
        o[1][...] = (r * r).astype(BF16)

    u, act = _mm("mlp_up", (T // TL, N_DEV, 1),
                 [(h2, full_rows(TL), wu, pl.BlockSpec((None, D, 1024), lambda i, j, k: (j, 0, 0)), "nn", None)],
                 behind(token), [(_sds((T, D_FF), BF16), rows(1024, TL))] * 2, up_epi, None, PP)
    (wd,) = late_weights("down", act)
    wd2 = wd.reshape(D_FF, D)
    half_ji = pl.BlockSpec((TS, D // 2), lambda j, i, k: (i, j))
    (x3,) = _mm("mlp_down", (2, T // TS, 1),
                [(act, pl.BlockSpec((TS, D_FF), lambda j, i, k: (i, 0)), wd2,
                  pl.BlockSpec((D_FF, D // 2), lambda j, i, k: (0, j)), "nn", None)],
                [(x2, half_ji)], [(_sds((T, D), F32), half_ji)], add_res, None, PP)

    loss, dx3, dx3b, dg3 = _final_norm_loss(x3, norm_final.reshape(1, D), tgt)

    def du_epi(acc, ex, o, pids):
        o[0][...] = (acc * (2.0 * jnp.maximum(ex[0][...].astype(F32), 0.0))).astype(BF16)

    (du,) = _mm("mlp_down_dx", (T // TL, N_DEV, 1),
                [(dx3b, full_rows(TL), wd2, pl.BlockSpec((1024, D), lambda i, j, k: (j, 0)), "nt", None)],
                [(u, rows(1024, TL))], [(_sds((T, D_FF), BF16), rows(1024, TL))], du_epi, None, PP)
    (dwd,) = _mm("mlp_down_dw", (D // 1024, D_FF // 512, 1), [(act, tall(512), dx3b, tall_j(1024), "tn", None)],
                 [], [(_sds((D_FF, D), BF16), pl.BlockSpec((512, 1024), lambda j, i, k: (i, j)))], store_bf16, None, PP)
    dwd = dwd.reshape(N_DEV, D_FF // N_DEV, D)
    (dwu,) = _mm("mlp_up_dw", (N_DEV, D // 512, 1), [(h2, tall(512), du, tall_j(1024), "tn", None)],
                 behind(send("down", dict(w_down=dwd))), [(_sds((N_DEV, D, 1024), BF16), pl.BlockSpec((None, 512, 1024), lambda j, i, k: (j, i, 0)))],
                 store_bf16, None, PP)

    out_ji = pl.BlockSpec((TM, 512), lambda j, i, k: (i, j))
    a_ji = lambda w: pl.BlockSpec((TM, w), lambda j, i, k: (i, 0))
    w_ji = lambda n: pl.BlockSpec((N_DEV, 512, n), lambda j, i, k: (0, j, 0))
    (dh2,) = _mm("mlp_up_dx", (D // 512, nt, 1), [(du, a_ji(D_FF), wu, w_ji(1024), "nt_shards", None)],
                 behind(send("mlp", dict(w_up=dwu))), [(_sds((T, D), BF16), out_ji)], store_bf16, None, PP)
    dx2, dx2b, dg2 = _rmsnorm_bwd("rmsnorm_mlp_bwd", dh2, x2, r2, norm_mlp, dx3, True)

    (dwo,) = _mm("out_proj_dw", (D // 1024, D // 512, 1), [(merged, tall(512), dx2b, tall_j(1024), "tn", None)],
                 [], [(_sds((D, D), BF16), pl.BlockSpec((512, 1024), lambda j, i, k: (i, j)))], store_bf16, None, PP)

    def dmerge_epi(acc, ex, o, pids):
        ga, gb, pa_v, pb_v = (e[...].astype(F32) for e in ex)
        o[0][...] = (acc * ga).astype(BF16)
        o[1][...] = (acc * gb).astype(BF16)
        dga = acc * pa_v * ga * (1.0 - ga)
        dgb = acc * pb_v * gb * (1.0 - gb)
        o[2][0] = dga.astype(BF16)
        o[2][1] = dgb.astype(BF16)
        sa = jnp.sum(dga, axis=0, keepdims=True)
        sb = jnp.sum(dgb, axis=0, keepdims=True)
        first = pids[1] == 0

        @pl.when(first)
        def _():
            o[3][...] = sa
            o[4][...] = sb

        @pl.when(jnp.logical_not(first))
        def _():
            o[3][...] += sa
            o[4][...] += sb

    cj = pl.BlockSpec((TL, 512), lambda j, i, k: (i, j))
    bj = pl.BlockSpec((1, 512), lambda j, i, k: (0, j))
    dpa, dpb, dgate, dba, dbb = _mm(
        "out_proj_dx", (4, T // TL, 1),
        [(dx2b, pl.BlockSpec((TL, D), lambda j, i, k: (i, 0)), wo2, pl.BlockSpec((512, D), lambda j, i, k: (j, 0)), "nt", None)],
        [(gates, cj), (gates, pl.BlockSpec((TL, 512), lambda j, i, k: (i, 4 + j))), (pa, cj), (pb, cj)],
        [(_sds((T, D), BF16), cj)] * 2 + [(_sds((2, T, D), BF16), pl.BlockSpec((2, TL, 512), lambda j, i, k: (0, i, j)))]
        + [(_sds((1, D), F32), bj)] * 2, dmerge_epi, None, ("parallel", "arbitrary", "arbitrary"))

    def store_shards(acc, ex, o, pids):
        for s in range(4):
            o[0][s] = acc[:, s * 256:(s + 1) * 256].astype(BF16)

    def proj_dw(name, y, dp):
        (dw,) = _mm(name, (2, 1, 1), [(y, tall(512), dp, tall_j(1024), "tn", None)],
                    [], [(_sds((N_DEV, 512, 256), BF16), pl.BlockSpec((4, 512, 256), lambda j, i, k: (j, 0, 0)))],
                    store_shards, None, PP)
        return dw

    def proj_dx(name, dp, w):
        (dy,) = _mm(name, (nt, 1, 1),
                    [(dp, full_rows(TM), w, pl.BlockSpec((N_DEV, 512, 256), lambda i, j, k: (0, 0, 0)), "nt_shards", None)],
                    [], [(_sds((T, 512), BF16), full_rows(TM, 512))], store_bf16, None, PP)
        return dy

    def in_dw(name, dact, dact_spec, n):
        def store_pair(acc, ex, o, pids):
            o[0][0] = acc[:, :n].astype(BF16)
            o[0][1] = acc[:, n:].astype(BF16)

        (dw,) = _mm(name, (N_DEV // 2, D // 512, 1), [(h1, tall(512), dact, dact_spec, "tn", None)],
                    [], [(_sds((N_DEV, D, n), BF16), pl.BlockSpec((2, 512, n), lambda j, i, k: (j, i, 0)))],
                    store_pair, None, PP)
        return dw

    dwg = in_dw("gate_proj_dw", dgate, pl.BlockSpec((None, T, 1024), lambda j, i, k: (j // 2, 0, j % 2)), 512)
    dwpa = proj_dw("proj_a_dw", ya, dpa)
    dwpb = proj_dw("proj_b_dw", yb, dpb)
    token = send("proj", dict(w_gate=dwg, w_out=dwo.reshape(N_DEV, D // N_DEV, D), w_proj_a=dwpa, w_proj_b=dwpb))
    dya = proj_dx("proj_a_dx", dpa, wpa)
    dyb = proj_dx("proj_b_dx", dpb, wpb)

    dq_a, dk_a, dv_a = _dilated_bwd(qkv, last_src, _behind(slopes, token), dya, ya, lse_a)
    dq_b, dk_b, dv_b, dtiles = _attn_b_bwd(qkv, tiles, dyb, yb, lse_b)
    drpb = _rpb_reduce(dtiles.reshape(N_HEADS_B * N_DR, GRID_W * GRID_W), sel)[:, :N_DC]
    dqkv = jnp.concatenate([*dq_a, dq_b, *dk_a, dk_b, *dv_a, dv_b], axis=1)
    dwq = in_dw("qkv_proj_dw", dqkv, tall_j(1536), 768)
    half_a = lambda h: pl.BlockSpec((None, TM, D), lambda j, i, k: (h, i, 0))
    half_w = lambda h: pl.BlockSpec((4, 512, 512), lambda j, i, k: (h, j, 0))
    token = send("in", dict(w_qkv=dwq))
    riders = riders(dwq if token is None else token, 4 * nt, lambda j, i, k: j * nt + i) if riders else []
    lead = behind(token)

    def store_and_ride(acc, ex, o, pids):
        o[0][...] = acc.astype(BF16)
        for r, (_, _, run) in enumerate(riders):
            run(ex[len(lead) + 4 * r:len(lead) + 4 * r + 4], o[1 + 4 * r:5 + 4 * r])

    dh1, *rode = _mm("in_proj_dx", (D // 512, nt, 1),
                     [(dqkv, a_ji(3 * D), wq, w_ji(768), "nt_shards", None),
                      (dgate, half_a(0), wg, half_w(0), "nt_shards", None), (dgate, half_a(1), wg, half_w(1), "nt_shards", None)],
                     lead + [e for r in riders for e in r[0]],
                     [(_sds((T, D), BF16), out_ji)] + [o for r in riders for o in r[1]], store_and_ride, None, PP)
    grad_x, dg1 = _rmsnorm_bwd("rmsnorm_mix_bwd", dh1, x, r1, norm_mix, dx2, False)

    small = dict(norm_mix=dg1, b_gate=jnp.concatenate([dba, dbb], axis=1), rpb=drpb, norm_mlp=dg2, norm_final=dg3)
    return loss, grad_x, small, [rode[4 * r:4 * r + 4] for r in range(len(riders))]


BIG = ("w_qkv", "w_gate", "w_proj_a", "w_proj_b", "w_out", "w_up", "w_down")
SMALL = ("norm_mix", "b_gate", "rpb", "norm_mlp", "norm_final")
SMALL_LANES = 96 * 128
RIDING = ("w_up", "w_down", "w_gate")


def _place():
    return lax.axis_index("x"), lax.axis_index("y"), lax.axis_index("c")


def _peer(x, y, c, mask):
    return x ^ (mask >> 2), y ^ ((mask >> 1) & 1), c ^ (mask & 1)


_HBM = pl.BlockSpec(memory_space=pltpu.HBM)
_SEM = pl.BlockSpec(memory_space=pltpu.SEMAPHORE)
_ANY = pl.BlockSpec(memory_space=pl.ANY)
_EFFECT = pltpu.SideEffectType.DATAFLOW_SIDE_EFFECTING


def _hbm(a):
    return pltpu.with_memory_space_constraint(a, pltpu.HBM)


def _other_chips(x, y):
    return [(1 - x, y), (x, 1 - y), (1 - x, 1 - y)]


def _block(x, y, c):
    return 4 * x + 2 * y + c


def _rdma(src, dst, send_sem, recv_sem, to):
    return pltpu.make_async_remote_copy(src_ref=src, dst_ref=dst, send_sem=send_sem, recv_sem=recv_sem, device_id=to,
                                        device_id_type=MESH)


def _gather_start(name, shards):
    n = len(shards)

    def body(*refs):
        srcs, lands = refs[:n], refs[n:2 * n]
        outs = refs[2 * n:]
        send, recv_d2d, recv_ici, local = outs[:n], outs[n:2 * n], outs[2 * n:3 * n], outs[3 * n:4 * n]
        token_ref = outs[6 * n]
        x, y, c = _place()
        for a in range(n):
            mine = lands[a].at[_block(x, y, c)]
            _rdma(srcs[a], mine, send[a].at[0], recv_d2d[a].at[0], (x, y, 1 - c)).start()
            for j, chip in enumerate(_other_chips(x, y)):
                _rdma(srcs[a], mine, send[a].at[1 + j], recv_ici[a].at[j], (*chip, c)).start()
        for a in range(n):
            pltpu.make_async_copy(srcs[a], lands[a].at[_block(x, y, c)], local[a].at[0]).start()
        token_ref[...] = jnp.zeros_like(token_ref)

    lands = [_hbm(lax.empty((N_DEV,) + s.shape, s.dtype)) for s in shards]
    srcs = [_hbm(s) for s in shards]
    dma = pltpu.SemaphoreType.DMA
    res = pl.pallas_call(
        body, name=name,
        out_shape=[dma((4,))] * n + [dma((1,))] * n + [dma((3,))] * n + [dma((1,))] * n
        + [pltpu.HBM(s.shape, s.dtype) for s in srcs] + [pltpu.HBM(l.shape, l.dtype) for l in lands]
        + [jax.ShapeDtypeStruct((8, 128), F32)],
        in_specs=[_HBM] * (2 * n), out_specs=[_SEM] * (4 * n) + [_HBM] * (2 * n) + [pl.BlockSpec(memory_space=pltpu.VMEM)],
        input_output_aliases={i: 4 * n + i for i in range(2 * n)},
        compiler_params=pltpu.CompilerParams(has_side_effects=_EFFECT))(*srcs, *lands)
    keys = ("send", "recv_d2d", "recv_ici", "local", "src", "land")
    return [dict(zip(keys, (res[k * n + a] for k in range(6)))) for a in range(n)], res[6 * n]


def _gather_forward(name, handles, after):
    n = len(handles)
    after = after if isinstance(after, (tuple, list)) else (after,)

    def body(*refs):
        lands, recv_ici = refs[:n], refs[n:2 * n]
        outs = refs[2 * n + len(after):]
        fwd_send, fwd_recv, token_ref = outs[n:2 * n], outs[2 * n:3 * n], outs[3 * n]
        x, y, c = _place()
        for a in range(n):
            for j, chip in enumerate(_other_chips(x, y)):
                blk = lands[a].at[_block(*chip, c)]
                _rdma(blk, blk, fwd_send[a].at[j], recv_ici[a].at[j], (*chip, c)).wait_recv()
                _rdma(blk, blk, fwd_send[a].at[j], fwd_recv[a].at[j], (x, y, 1 - c)).start()
        token_ref[...] = jnp.zeros_like(token_ref)

    dma = pltpu.SemaphoreType.DMA
    lands = [h["land"] for h in handles]
    res = pl.pallas_call(
        body, name=name,
        out_shape=[pltpu.HBM(l.shape, l.dtype) for l in lands] + [dma((3,))] * (2 * n) + [jax.ShapeDtypeStruct((8, 128), F32)],
        in_specs=[_HBM] * n + [_SEM] * n + [_ANY] * len(after),
        out_specs=[_HBM] * n + [_SEM] * (2 * n) + [pl.BlockSpec(memory_space=pltpu.VMEM)],
        input_output_aliases={i: i for i in range(n)},
        compiler_params=pltpu.CompilerParams(has_side_effects=_EFFECT))(*lands, *[h["recv_ici"] for h in handles], *after)
    return [dict(h, land=res[a], fwd_send=res[n + a], fwd_recv=res[2 * n + a]) for a, h in enumerate(handles)], res[3 * n]


def _gather_wait(name, handles, after):
    n = len(handles)

    def body(*refs):
        srcs, lands = refs[:n], refs[n:2 * n]
        send, recv_d2d, local, fwd_send, fwd_recv = (refs[(2 + k) * n:(3 + k) * n] for k in range(5))
        x, y, c = _place()
        sibling = (x, y, 1 - c)
        for a in range(n):
            mine = lands[a].at[_block(x, y, c)]
            _rdma(srcs[a], mine, send[a].at[0], recv_d2d[a].at[0], sibling).wait_send()
            _rdma(srcs[a], lands[a].at[_block(*sibling)], send[a].at[0], recv_d2d[a].at[0], sibling).wait_recv()
            pltpu.make_async_copy(srcs[a], mine, local[a].at[0]).wait()
            for j, chip in enumerate(_other_chips(x, y)):
                _rdma(srcs[a], mine, send[a].at[1 + j], fwd_recv[a].at[j], (*chip, c)).wait_send()
                blk = lands[a].at[_block(*chip, c)]
                _rdma(blk, blk, fwd_send[a].at[j], fwd_recv[a].at[j], sibling).wait_send()
                got = lands[a].at[_block(*chip, 1 - c)]
                _rdma(got, got, fwd_send[a].at[j], fwd_recv[a].at[j], sibling).wait_recv()

    srcs = [h["src"] for h in handles]
    lands = [h["land"] for h in handles]
    sems = [h[k] for k in ("send", "recv_d2d", "local", "fwd_send", "fwd_recv") for h in handles]
    res = pl.pallas_call(
        body, name=name, out_shape=[pltpu.HBM(s.shape, s.dtype) for s in srcs] + [pltpu.HBM(l.shape, l.dtype) for l in lands],
        in_specs=[_HBM] * (2 * n) + [_SEM] * (5 * n) + [_ANY], out_specs=[_HBM] * (2 * n),
        input_output_aliases={i: i for i in range(2 * n)},
        compiler_params=pltpu.CompilerParams(has_side_effects=_EFFECT))(*srcs, *lands, *sems, after)
    return list(res[n:])


def _gather_small(small, after):
    def body(in_ref, *refs):
        out_ref, send_sems, recv_sems, local_sem = refs[len(after):]
        x, y, c = _place()
        me = 4 * x + 2 * y + c
        copies = [pltpu.make_async_copy(in_ref, out_ref.at[me], local_sem)]
        for mask in range(1, N_DEV):
            copies.append(pltpu.make_async_remote_copy(
                src_ref=in_ref, dst_ref=out_ref.at[me], send_sem=send_sems.at[mask - 1],
                recv_sem=recv_sems.at[mask - 1], device_id=_peer(x, y, c, mask), device_id_type=MESH))
        for cp in copies:
            cp.start()
        for cp in copies:
            cp.wait()

    any_spec = pl.BlockSpec(memory_space=pl.ANY)
    return pl.pallas_call(
        body, name="gather_small", in_specs=[any_spec] * (1 + len(after)), out_specs=any_spec,
        out_shape=jax.ShapeDtypeStruct((N_DEV,) + small.shape, small.dtype),
        scratch_shapes=[pltpu.SemaphoreType.DMA((7,)), pltpu.SemaphoreType.DMA((7,)), pltpu.SemaphoreType.DMA])(small, *after)


def _send_copy(src_refs, land_refs, send_sems, recv_sems, a, mask, scatter, x, y, c):
    px, py, pc = _peer(x, y, c, mask)
    src = src_refs[a].at[4 * px + 2 * py + pc] if scatter else src_refs[a]
    return pltpu.make_async_remote_copy(
        src_ref=src, dst_ref=land_refs[a].at[4 * x + 2 * y + c], send_sem=send_sems[a].at[mask - 1],
        recv_sem=recv_sems[a].at[mask - 1], device_id=(px, py, pc), device_id_type=MESH)


def _local_copy(src_refs, land_refs, local_sems, a, scatter, x, y, c):
    me = 4 * x + 2 * y + c
    return pltpu.make_async_copy(src_refs[a].at[me] if scatter else src_refs[a], land_refs[a].at[me], local_sems[a].at[0])


def _send_start(name, srcs, scatter, after=None):
    n = len(srcs)
    n_in = 2 * n + (after is not None)
    shapes = [s.shape[1:] if scatter else s.shape for s in srcs]

    def body(*refs):
        src_refs, land_refs = refs[:n], refs[n:2 * n]
        outs = refs[n_in:]
        send_sems, recv_sems, local_sems = outs[:n], outs[n:2 * n], outs[2 * n:3 * n]
        token_ref = outs[5 * n]
        x, y, c = _place()
        for a in range(n):
            for mask in range(1, N_DEV):
                _send_copy(src_refs, land_refs, send_sems, recv_sems, a, mask, scatter, x, y, c).start()
        for a in range(n):
            _local_copy(src_refs, land_refs, local_sems, a, scatter, x, y, c).start()
        token_ref[...] = jnp.zeros_like(token_ref)

    lands = [pltpu.with_memory_space_constraint(lax.empty((N_DEV,) + sh, s.dtype), pltpu.HBM) for sh, s in zip(shapes, srcs)]
    srcs = [pltpu.with_memory_space_constraint(s, pltpu.HBM) for s in srcs]
    res = pl.pallas_call(
        body, name=name,
        out_shape=[pltpu.SemaphoreType.DMA((7,))] * (2 * n) + [pltpu.SemaphoreType.DMA((1,))] * n
        + [pltpu.HBM(s.shape, s.dtype) for s in srcs] + [pltpu.HBM(l.shape, l.dtype) for l in lands]
        + [jax.ShapeDtypeStruct((8, 128), F32)],
        in_specs=[_HBM] * (2 * n) + [pl.BlockSpec(memory_space=pl.ANY)] * (n_in - 2 * n),
        out_specs=[_SEM] * (3 * n) + [_HBM] * (2 * n) + [pl.BlockSpec(memory_space=pltpu.VMEM)],
        input_output_aliases={i: 3 * n + i for i in range(2 * n)},
        compiler_params=pltpu.CompilerParams(has_side_effects=_EFFECT))(*srcs, *lands, *([] if after is None else [after]))
    handles = [tuple(res[k * n + a] for k in range(5)) for a in range(n)]
    return handles, res[5 * n]


def _send_wait(name, handles, scatter, after):
    n = len(handles)

    def body(*refs):
        src_refs, land_refs = refs[:n], refs[n:2 * n]
        send_sems, recv_sems, local_sems = refs[2 * n:3 * n], refs[3 * n:4 * n], refs[4 * n:5 * n]
        x, y, c = _place()
        for a in range(n):
            for mask in range(1, N_DEV):
                cp = _send_copy(src_refs, land_refs, send_sems, recv_sems, a, mask, scatter, x, y, c)
                cp.wait_send()
                cp.wait_recv()
            _local_copy(src_refs, land_refs, local_sems, a, scatter, x, y, c).wait()

    srcs = [h[3] for h in handles]
    lands = [h[4] for h in handles]
    res = pl.pallas_call(
        body, name=name, out_shape=[pltpu.HBM(s.shape, s.dtype) for s in srcs] + [pltpu.HBM(l.shape, l.dtype) for l in lands],
        in_specs=[_HBM] * (2 * n) + [_SEM] * (3 * n) + [pl.BlockSpec(memory_space=pl.ANY)], out_specs=[_HBM] * (2 * n),
        input_output_aliases={i: i for i in range(2 * n)},
        compiler_params=pltpu.CompilerParams(has_side_effects=_EFFECT))(
            *srcs, *lands, *[h[0] for h in handles], *[h[1] for h in handles], *[h[2] for h in handles], after)
    return list(res[n:])


def _adamw_math(g, w, m, v):
    nm = ADAM_B1 * m + (1.0 - ADAM_B1) * g
    nv = ADAM_B2 * v + (1.0 - ADAM_B2) * jnp.square(g)
    m_hat = nm / (1.0 - ADAM_B1 ** ADAM_STEP)
    v_hat = nv / (1.0 - ADAM_B2 ** ADAM_STEP)
    return -ADAM_LR * (m_hat / (jnp.sqrt(v_hat) + ADAM_EPS) + ADAM_WD * w), nm, nv


def _adamw_small(gathered, ws, ms, vs):
    n = len(ws)
    widths = [w.shape[1] for w in ws]

    def body(p_ref, *refs):
        ins, outs = refs[:3 * n], refs[3 * n:]
        off = 0
        for a in range(n):
            g = p_ref[0, :, off:off + widths[a]]
            for i in range(1, N_DEV):
                g = g + p_ref[i, :, off:off + widths[a]]
            outs[4 * a][...] = g
            outs[4 * a + 1][...], outs[4 * a + 2][...], outs[4 * a + 3][...] = _adamw_math(
                g, ins[a][...], ins[n + a][...], ins[2 * n + a][...])
            off += widths[a]

    whole = lambda shape: pl.BlockSpec(shape, lambda i: (0,) * len(shape))
    rows = [whole((1, wd)) for wd in widths]
    res = _pc(body, name="adamw_small", grid=(1,), in_specs=[whole(gathered.shape)] + rows * 3,
              out_specs=[r for r in rows for _ in range(4)],
              out_shape=[jax.ShapeDtypeStruct((1, wd), F32) for wd in widths for _ in range(4)],
              sem=("arbitrary",))(gathered, *ws, *ms, *vs)
    return [res[4 * a:4 * a + 4] for a in range(n)]


def _adamw(name, parts, w, m, v):
    K, n = w.shape
    tr = min(K, 128)

    def body(p_ref, w_ref, m_ref, v_ref, g_ref, d_ref, nm_ref, nv_ref):
        g = p_ref[0].astype(F32)
        for i in range(1, N_DEV):
            g = g + p_ref[i].astype(F32)
        g_ref[...] = g
        d_ref[...], nm_ref[...], nv_ref[...] = _adamw_math(g, w_ref[...], m_ref[...], v_ref[...])

    blk = pl.BlockSpec((tr, n), lambda i: (i, 0))
    return _pc(body, name=name, grid=(K // tr,), in_specs=[pl.BlockSpec((N_DEV, tr, n), lambda i: (0, i, 0)), blk, blk, blk],
               out_specs=[blk] * 4, out_shape=[jax.ShapeDtypeStruct((K, n), F32)] * 4, sem=("parallel",))(parts, w, m, v)


def _adamw_rider(parts, w, m, v, steps, step_of):
    K, n = w.shape
    tr = K // steps
    blk = pl.BlockSpec((tr, n), lambda *ids: (step_of(*ids), 0))
    extras = [(parts, pl.BlockSpec((N_DEV, tr, n), lambda *ids: (0, step_of(*ids), 0))), (w, blk), (m, blk), (v, blk)]

    def run(ex, o):
        g = ex[0][0].astype(F32)
        for i in range(1, N_DEV):
            g = g + ex[0][i].astype(F32)
        o[0][...] = g
        o[1][...], o[2][...], o[3][...] = _adamw_math(g, ex[1][...], ex[2][...], ex[3][...])

    return extras, [(jax.ShapeDtypeStruct((K, n), F32), blk)] * 4, run


def _lane_padded(size):
    return -(-size // 128) * 128


def _pack_small(vals):
    rows = [jnp.pad(v.reshape(1, -1), ((0, 0), (0, _lane_padded(v.size) - v.size))) for v in vals]
    used = sum(r.shape[1] for r in rows)
    return jnp.concatenate(rows + [jnp.zeros((1, SMALL_LANES - used), F32)], axis=1)


def kernel(x, norm_mix, w_qkv, w_gate, b_gate, rpb, w_proj_a, w_proj_b, w_out, norm_mlp, w_up, w_down, norm_final, loss_target, m_norm_mix, m_w_qkv, m_w_gate, m_b_gate, m_rpb, m_w_proj_a, m_w_proj_b, m_w_out, m_norm_mlp, m_w_up, m_w_down, m_norm_final, v_norm_mix, v_w_qkv, v_w_gate, v_b_gate, v_rpb, v_w_proj_a, v_w_proj_b, v_w_out, v_norm_mlp, v_w_up, v_w_down, v_norm_final):
    w = dict(norm_mix=norm_mix, w_qkv=w_qkv, w_gate=w_gate, b_gate=b_gate, rpb=rpb, w_proj_a=w_proj_a, w_proj_b=w_proj_b,
             w_out=w_out, norm_mlp=norm_mlp, w_up=w_up, w_down=w_down, norm_final=norm_final)
    m = dict(norm_mix=m_norm_mix, w_qkv=m_w_qkv, w_gate=m_w_gate, b_gate=m_b_gate, rpb=m_rpb, w_proj_a=m_w_proj_a,
             w_proj_b=m_w_proj_b, w_out=m_w_out, norm_mlp=m_norm_mlp, w_up=m_w_up, w_down=m_w_down, norm_final=m_norm_final)
    v = dict(norm_mix=v_norm_mix, w_qkv=v_w_qkv, w_gate=v_w_gate, b_gate=v_b_gate, rpb=v_rpb, w_proj_a=v_w_proj_a,
             w_proj_b=v_w_proj_b, w_out=v_w_out, norm_mlp=v_norm_mlp, w_up=v_w_up, w_down=v_w_down, norm_final=v_norm_final)
    order = ("norm_mix", "w_qkv", "w_gate", "b_gate", "rpb", "w_proj_a", "w_proj_b", "w_out", "norm_mlp", "w_up",
             "w_down", "norm_final")

    stages = dict(qkv=("w_qkv",), gate=("w_gate", "w_proj_a", "w_proj_b", "w_out"), mlp=("w_up", "w_down"))
    first, token = _gather_start("gather_start_qkv", [w["w_qkv"][0].astype(BF16)])
    names = [n for s in ("gate", "mlp") for n in stages[s]]
    rest, token = _gather_start("gather_start", [_behind(w[n][0], token).astype(BF16) for n in names])
    handles = dict(zip(names, rest), w_qkv=first[0])

    def late_weights(stage, after):
        if stage == "qkv":
            forwarded, _ = _gather_forward("gather_forward_qkv", [handles["w_qkv"]], after)
            return tuple(_gather_wait("gather_wait_qkv", forwarded, after[0]))
        if stage == "attention":
            handles["gate"], tok = _gather_forward("gather_forward_gate", [handles[n] for n in stages["gate"]], after)
            return (tok,)
        if stage == "mixer":
            handles["up"], tok = _gather_forward("gather_forward_up", [handles["w_up"]], after)
            return tuple(_gather_wait("gather_wait_gate", handles["gate"], after)) + (tok,)
        if stage == "residual":
            handles["down"], tok = _gather_forward("gather_forward_down", [handles["w_down"]], after)
            return (tok,)
        return tuple(_gather_wait("gather_wait_" + stage, handles[stage], after))

    sent = {}

    def send(stage, grads):
        handles, tok = _send_start("grads_start_" + stage, list(grads.values()), True)
        sent[stage] = (tuple(grads), handles)
        return tok

    waited = {}

    def riders(after, steps, step_of):
        landed = {}
        for stage in ("down", "mlp", "proj"):
            names, handles = sent[stage]
            waited[stage] = dict(zip(names, _send_wait("grads_wait_" + stage, handles, True, after)))
            after = waited[stage][names[0]]
            landed.update(waited[stage])
        return [_adamw_rider(landed[n], w[n][0], m[n][0], v[n][0], steps, step_of) for n in RIDING]

    loss, grad_x, small, rode = _local_step(x[0], loss_target[0], _behind(norm_mix, token), b_gate, rpb[0], norm_mlp,
                                            norm_final, late_weights, send, riders)

    grad, delta, new_m, new_v = {}, {}, {}, {}
    for n, res in zip(RIDING, rode):
        grad[n], delta[n], new_m[n], new_v[n] = (r[None] for r in res)
    after = grad_x
    for stage in ("down", "mlp", "proj", "small", "in"):
        if stage == "small":
            all_small = after = _gather_small(_pack_small([small[n] for n in SMALL]), [d[0] for d in delta.values()])
            continue
        names, handles = sent[stage]
        landed = waited[stage] if stage in waited else dict(zip(names, _send_wait("grads_wait_" + stage, handles, True, after)))
        for n, parts in landed.items():
            if n in RIDING:
                continue
            res = _adamw("adamw_" + n, parts, w[n][0], m[n][0], v[n][0])
            grad[n], delta[n], new_m[n], new_v[n] = (r[None] for r in res)
            after = res[1]
    as_row = lambda a: jnp.pad(a.reshape(1, -1), ((0, 0), (0, _lane_padded(a.size) - a.size)))
    res = _adamw_small(all_small, *[[as_row(t[n]) for n in SMALL] for t in (w, m, v)])
    for n, rows in zip(SMALL, res):
        for tree, row in zip((grad, delta, new_m, new_v), rows):
            tree[n] = row[0, :w[n].size].reshape(w[n].shape)

    local_loss, _, _ = lax.optimization_barrier((loss[0, 0], res[0][1], delta["w_qkv"]))
    total = lax.psum(local_loss, ("x", "y", "c"))
    return (total, grad_x[None], *[grad[n] for n in order], *[delta[n] for n in order],
            *[new_m[n] for n in order], *[new_v[n] for n in order])
```
